```python
import math
import jax, jax.numpy as jnp
from jax import lax
import numpy as np

D_MODEL = 1024
BATCH = 32
SEQ = 2048
DEPTH = 4

N_A_LAYERS = DEPTH // 2
N_B_LAYERS = DEPTH - N_A_LAYERS
SSM_EXPAND = 2
SSM_D_INNER = SSM_EXPAND * D_MODEL
SSM_HEAD_DIM = 64
SSM_HEADS = SSM_D_INNER // SSM_HEAD_DIM
SSM_GROUPS = 4
SSM_HEADS_PER_GROUP = SSM_HEADS // SSM_GROUPS
SSM_STATE = 128
SSM_CONV = 4
SSM_CHUNK = 128
SSM_CONV_DIM = SSM_D_INNER + 2 * SSM_GROUPS * SSM_STATE
SSM_IN_DIM = 2 * SSM_D_INNER + 2 * SSM_GROUPS * SSM_STATE + SSM_HEADS
SB_HEADS = 16
SB_HEAD_DIM = 64
SB_WIDTH = SB_HEADS * SB_HEAD_DIM
SB_BLOCK = 128
D_FF = 2816
FFN_CONV = 3
PLE_DIM = 256
NORM_EPS = 1e-6
SSM_NORM_EPS = 1e-5

kernel_name = "yoco_mamba2_stickbreak_hybrid"


def rms_norm(x, gain, eps=NORM_EPS):
    xf = x.astype(jnp.float32)
    y = xf * lax.rsqrt(jnp.mean(xf * xf, axis=-1, keepdims=True) + eps)
    return (y * gain.astype(jnp.float32)).astype(x.dtype)


def causal_depthwise_conv(u, w, b):
    k = w.shape[0]
    out = lax.conv_general_dilated(
        u, w[:, None, :].astype(u.dtype), window_strides=(1,), padding=[(k - 1, 0)],
        dimension_numbers=("NWC", "WIO", "NWC"), feature_group_count=u.shape[-1])
    return out + b.astype(u.dtype)


def ssd_chunked(x, dt, a, b_mat, c_mat):
    bsz, seq = x.shape[0], x.shape[1]
    nc, L = seq // SSM_CHUNK, SSM_CHUNK
    G, R, P, N = SSM_GROUPS, SSM_HEADS_PER_GROUP, SSM_HEAD_DIM, SSM_STATE
    xr = (x * dt[..., None]).reshape(bsz, nc, L, G, R, P)
    br = b_mat.reshape(bsz, nc, L, G, N)
    cr = c_mat.reshape(bsz, nc, L, G, N)
    a_dt = (dt * a).reshape(bsz, nc, L, G, R).transpose(0, 1, 3, 4, 2)
    a_cs = jnp.cumsum(a_dt, axis=-1)
    tri = jnp.tril(jnp.ones((L, L), dtype=bool))
    decay_in = jnp.exp(jnp.where(tri, a_cs[..., :, None] - a_cs[..., None, :], -jnp.inf))
    cb = jnp.einsum("bclgn,bcsgn->bcgls", cr, br)
    y_diag = jnp.einsum("bcgls,bcgrls,bcsgrp->bclgrp", cb, decay_in, xr)
    decay_states = jnp.exp(a_cs[..., -1:] - a_cs)
    states = jnp.einsum("bclgn,bcgrl,bclgrp->bcgrpn", br, decay_states, xr)
    states = jnp.concatenate([jnp.zeros_like(states[:, :1]), states], axis=1)
    chunk_sum = jnp.pad(a_cs[..., -1].transpose(0, 2, 3, 1), ((0, 0), (0, 0), (0, 0), (1, 0)))
    chunk_cs = jnp.cumsum(chunk_sum, axis=-1)
    tri_c = jnp.tril(jnp.ones((nc + 1, nc + 1), dtype=bool))
    decay_chunk = jnp.exp(jnp.where(tri_c, chunk_cs[..., :, None] - chunk_cs[..., None, :], -jnp.inf))
    new_states = jnp.einsum("bgrzc,bcgrpn->bzgrpn", decay_chunk, states)
    prev_states = new_states[:, :-1]
    y_off = jnp.einsum("bclgn,bcgrpn,bcgrl->bclgrp", cr, prev_states, jnp.exp(a_cs))
    return (y_diag + y_off).reshape(bsz, seq, SSM_HEADS, P)


def mamba2_mixer(h, in_proj, conv_w, conv_b, dt_bias, a_log, d_skip, norm_w, out_proj):
    bsz, seq, _ = h.shape
    zxbcdt = h @ in_proj
    z = zxbcdt[..., :SSM_D_INNER]
    xbc = zxbcdt[..., SSM_D_INNER:SSM_D_INNER + SSM_CONV_DIM]
    dt_raw = zxbcdt[..., SSM_D_INNER + SSM_CONV_DIM:]
    xbc = jax.nn.silu(causal_depthwise_conv(xbc, conv_w, conv_b)).astype(jnp.float32)
    xs = xbc[..., :SSM_D_INNER].reshape(bsz, seq, SSM_HEADS, SSM_HEAD_DIM)
    bm = xbc[..., SSM_D_INNER:SSM_D_INNER + SSM_GROUPS * SSM_STATE].reshape(bsz, seq, SSM_GROUPS, SSM_STATE)
    cm = xbc[..., SSM_D_INNER + SSM_GROUPS * SSM_STATE:].reshape(bsz, seq, SSM_GROUPS, SSM_STATE)
    dt = jax.nn.softplus(dt_raw.astype(jnp.float32) + dt_bias.astype(jnp.float32))
    a = -jnp.exp(a_log.astype(jnp.float32))
    y = ssd_chunked(xs, dt, a, bm, cm) + xs * d_skip.astype(jnp.float32)[:, None]
    g = (y.reshape(bsz, seq, SSM_D_INNER) * jax.nn.silu(z.astype(jnp.float32))).reshape(bsz, seq, SSM_GROUPS, -1)
    g = g * lax.rsqrt(jnp.mean(g * g, axis=-1, keepdims=True) + SSM_NORM_EPS)
    g = g.reshape(bsz, seq, SSM_D_INNER) * norm_w.astype(jnp.float32)
    return g.astype(h.dtype) @ out_proj


def stick_breaking_attention(q, k, v):
    seq = q.shape[1]
    scale = SB_HEAD_DIM ** -0.5
    outs = []
    for blk in range(seq // SB_BLOCK):
        q0 = blk * SB_BLOCK
        q1 = q0 + SB_BLOCK
        z = jnp.einsum("bthd,bshd->bhts", q[:, q0:q1], k[:, :q1]).astype(jnp.float32) * scale
        causal = (q0 + jnp.arange(SB_BLOCK))[:, None] > jnp.arange(q1)[None, :]
        log_beta = jax.nn.log_sigmoid(z)
        log_keep = jnp.where(causal, jax.nn.log_sigmoid(-z), 0.0)
        log_rest = lax.cumsum(log_keep, axis=3, reverse=True) - log_keep
        w = jnp.where(causal, jnp.exp(log_beta + log_rest), 0.0)
        outs.append(jnp.einsum("bhts,bshd->bthd", w.astype(v.dtype), v[:, :q1]))
    return jnp.concatenate(outs, axis=1)


def conv_glu_ffn(h, w_up, conv_w, conv_b, w_down):
    u = causal_depthwise_conv(h @ w_up, conv_w, conv_b)
    gate, val = jnp.split(u, 2, axis=-1)
    return (jax.nn.silu(gate) * val) @ w_down


def per_layer_embedding(h, p_i, norm_gain, w_gate, w_proj):
    gate = jax.nn.sigmoid(rms_norm(h, norm_gain) @ w_gate)
    return gate * (p_i @ w_proj)


def _fwd_setup_inputs(seed: int = 0) -> dict:
    key = jax.random.key(seed)
    ks = iter(jax.random.split(key, 32))

    def normal(shape, scale):
        return scale * jax.random.normal(next(ks), shape, jnp.float32)

    def gain(shape):
        return 1.0 + normal(shape, 0.02)

    x = normal((BATCH, SEQ, D_MODEL), 1.0)
    p = normal((DEPTH, BATCH, SEQ, PLE_DIM), 1.0)
    attn_norm = gain((DEPTH, D_MODEL))
    ffn_norm = gain((DEPTH, D_MODEL))
    ple_norm = gain((DEPTH, D_MODEL))
    ssm_in_proj = normal((N_A_LAYERS, D_MODEL, SSM_IN_DIM), D_MODEL ** -0.5)
    ssm_conv_w = normal((N_A_LAYERS, SSM_CONV, SSM_CONV_DIM), SSM_CONV ** -0.5)
    ssm_conv_b = normal((N_A_LAYERS, SSM_CONV_DIM), 0.01)
    u = jax.random.uniform(next(ks), (N_A_LAYERS, SSM_HEADS), jnp.float32)
    dt0 = jnp.exp(u * (math.log(0.1) - math.log(0.001)) + math.log(0.001))
    ssm_dt_bias = dt0 + jnp.log(-jnp.expm1(-dt0))
    ssm_a_log = jnp.log(jax.random.uniform(next(ks), (N_A_LAYERS, SSM_HEADS), jnp.float32, 1.0, 16.0))
    ssm_d = 1.0 + normal((N_A_LAYERS, SSM_HEADS), 0.1)
    ssm_norm = gain((N_A_LAYERS, SSM_D_INNER))
    ssm_out_proj = normal((N_A_LAYERS, SSM_D_INNER, D_MODEL), SSM_D_INNER ** -0.5)
    kv_norm = gain((D_MODEL,))
    w_kv = normal((D_MODEL, 2 * SB_WIDTH), D_MODEL ** -0.5)
    w_q = normal((N_B_LAYERS, D_MODEL, SB_WIDTH), D_MODEL ** -0.5)
    w_o = normal((N_B_LAYERS, SB_WIDTH, D_MODEL), SB_WIDTH ** -0.5)
    ffn_up = normal((DEPTH, D_MODEL, 2 * D_FF), D_MODEL ** -0.5)
    ffn_conv_w = normal((DEPTH, FFN_CONV, 2 * D_FF), FFN_CONV ** -0.5)
    ffn_conv_b = normal((DEPTH, 2 * D_FF), 0.01)
    ffn_down = normal((DEPTH, D_FF, D_MODEL), D_FF ** -0.5)
    ple_gate = normal((DEPTH, D_MODEL, D_MODEL), D_MODEL ** -0.5)
    ple_proj = normal((DEPTH, PLE_DIM, D_MODEL), PLE_DIM ** -0.5)
    final_norm = gain((D_MODEL,))
    return {"x": x, "p": p, "attn_norm": attn_norm, "ffn_norm": ffn_norm, "ple_norm": ple_norm,
            "ssm_in_proj": ssm_in_proj, "ssm_conv_w": ssm_conv_w, "ssm_conv_b": ssm_conv_b,
            "ssm_dt_bias": ssm_dt_bias, "ssm_a_log": ssm_a_log, "ssm_d": ssm_d,
            "ssm_norm": ssm_norm, "ssm_out_proj": ssm_out_proj, "kv_norm": kv_norm, "w_kv": w_kv,
            "w_q": w_q, "w_o": w_o, "ffn_up": ffn_up, "ffn_conv_w": ffn_conv_w,
            "ffn_conv_b": ffn_conv_b, "ffn_down": ffn_down, "ple_gate": ple_gate,
            "ple_proj": ple_proj, "final_norm": final_norm}


def _fwd_reference(x, p, attn_norm, ffn_norm, ple_norm, ssm_in_proj, ssm_conv_w, ssm_conv_b,
              ssm_dt_bias, ssm_a_log, ssm_d, ssm_norm, ssm_out_proj, kv_norm, w_kv,
              w_q, w_o, ffn_up, ffn_conv_w, ffn_conv_b, ffn_down, ple_gate, ple_proj,
              final_norm):
    bsz, seq, _ = x.shape
    h = x
    k_shared = None
    v_shared = None
    for i in range(DEPTH):
        hn = rms_norm(h, attn_norm[i])
        if i < N_A_LAYERS:
            mix = mamba2_mixer(hn, ssm_in_proj[i], ssm_conv_w[i], ssm_conv_b[i], ssm_dt_bias[i],
                               ssm_a_log[i], ssm_d[i], ssm_norm[i], ssm_out_proj[i])
        else:
            j = i - N_A_LAYERS
            q = (hn @ w_q[j]).reshape(bsz, seq, SB_HEADS, SB_HEAD_DIM)
            o = stick_breaking_attention(q, k_shared, v_shared)
            mix = o.reshape(bsz, seq, SB_WIDTH) @ w_o[j]
        h = h + mix
        h = h + conv_glu_ffn(rms_norm(h, ffn_norm[i]), ffn_up[i], ffn_conv_w[i], ffn_conv_b[i], ffn_down[i])
        h = h + per_layer_embedding(h, p[i], ple_norm[i], ple_gate[i], ple_proj[i])
        if i == N_A_LAYERS - 1:
            kv = rms_norm(h, kv_norm) @ w_kv
            k_shared = kv[..., :SB_WIDTH].reshape(bsz, seq, SB_HEADS, SB_HEAD_DIM)
            v_shared = kv[..., SB_WIDTH:].reshape(bsz, seq, SB_HEADS, SB_HEAD_DIM)
    return rms_norm(h, final_norm)


import jax as _jax
import jax.numpy as _jnp

TWIN_FORMAT = 'train_step'
FWD_PARAMS = ['x', 'p', 'attn_norm', 'ffn_norm', 'ple_norm', 'ssm_in_proj', 'ssm_conv_w', 'ssm_conv_b', 'ssm_dt_bias', 'ssm_a_log', 'ssm_d', 'ssm_norm', 'ssm_out_proj', 'kv_norm', 'w_kv', 'w_q', 'w_o', 'ffn_up', 'ffn_conv_w', 'ffn_conv_b', 'ffn_down', 'ple_gate', 'ple_proj', 'final_norm']
TWIN_WEIGHTS = ['attn_norm', 'ffn_norm', 'ple_norm', 'ssm_in_proj', 'ssm_conv_w', 'ssm_conv_b', 'ssm_dt_bias', 'ssm_a_log', 'ssm_d', 'ssm_norm', 'ssm_out_proj', 'kv_norm', 'w_kv', 'w_q', 'w_o', 'ffn_up', 'ffn_conv_w', 'ffn_conv_b', 'ffn_down', 'ple_gate', 'ple_proj', 'final_norm']
TWIN_DIFF_INPUT = 'x'
TWIN_INPUTS = ['x', 'p', 'attn_norm', 'ffn_norm', 'ple_norm', 'ssm_in_proj', 'ssm_conv_w', 'ssm_conv_b', 'ssm_dt_bias', 'ssm_a_log', 'ssm_d', 'ssm_norm', 'ssm_out_proj', 'kv_norm', 'w_kv', 'w_q', 'w_o', 'ffn_up', 'ffn_conv_w', 'ffn_conv_b', 'ffn_down', 'ple_gate', 'ple_proj', 'final_norm', 'loss_target', 'm_attn_norm', 'm_ffn_norm', 'm_ple_norm', 'm_ssm_in_proj', 'm_ssm_conv_w', 'm_ssm_conv_b', 'm_ssm_dt_bias', 'm_ssm_a_log', 'm_ssm_d', 'm_ssm_norm', 'm_ssm_out_proj', 'm_kv_norm', 'm_w_kv', 'm_w_q', 'm_w_o', 'm_ffn_up', 'm_ffn_conv_w', 'm_ffn_conv_b', 'm_ffn_down', 'm_ple_gate', 'm_ple_proj', 'm_final_norm', 'v_attn_norm', 'v_ffn_norm', 'v_ple_norm', 'v_ssm_in_proj', 'v_ssm_conv_w', 'v_ssm_conv_b', 'v_ssm_dt_bias', 'v_ssm_a_log', 'v_ssm_d', 'v_ssm_norm', 'v_ssm_out_proj', 'v_kv_norm', 'v_w_kv', 'v_w_q', 'v_w_o', 'v_ffn_up', 'v_ffn_conv_w', 'v_ffn_conv_b', 'v_ffn_down', 'v_ple_gate', 'v_ple_proj', 'v_final_norm']
TWIN_OUTPUTS = ['loss', 'grad_x', 'grad_attn_norm', 'grad_ffn_norm', 'grad_ple_norm', 'grad_ssm_in_proj', 'grad_ssm_conv_w', 'grad_ssm_conv_b', 'grad_ssm_dt_bias', 'grad_ssm_a_log', 'grad_ssm_d', 'grad_ssm_norm', 'grad_ssm_out_proj', 'grad_kv_norm', 'grad_w_kv', 'grad_w_q', 'grad_w_o', 'grad_ffn_up', 'grad_ffn_conv_w', 'grad_ffn_conv_b', 'grad_ffn_down', 'grad_ple_gate', 'grad_ple_proj', 'grad_final_norm', 'delta_attn_norm', 'delta_ffn_norm', 'delta_ple_norm', 'delta_ssm_in_proj', 'delta_ssm_conv_w', 'delta_ssm_conv_b', 'delta_ssm_dt_bias', 'delta_ssm_a_log', 'delta_ssm_d', 'delta_ssm_norm', 'delta_ssm_out_proj', 'delta_kv_norm', 'delta_w_kv', 'delta_w_q', 'delta_w_o', 'delta_ffn_up', 'delta_ffn_conv_w', 'delta_ffn_conv_b', 'delta_ffn_down', 'delta_ple_gate', 'delta_ple_proj', 'delta_final_norm', 'new_m_attn_norm', 'new_m_ffn_norm', 'new_m_ple_norm', 'new_m_ssm_in_proj', 'new_m_ssm_conv_w', 'new_m_ssm_conv_b', 'new_m_ssm_dt_bias', 'new_m_ssm_a_log', 'new_m_ssm_d', 'new_m_ssm_norm', 'new_m_ssm_out_proj', 'new_m_kv_norm', 'new_m_w_kv', 'new_m_w_q', 'new_m_w_o', 'new_m_ffn_up', 'new_m_ffn_conv_w', 'new_m_ffn_conv_b', 'new_m_ffn_down', 'new_m_ple_gate', 'new_m_ple_proj', 'new_m_final_norm', 'new_v_attn_norm', 'new_v_ffn_norm', 'new_v_ple_norm', 'new_v_ssm_in_proj', 'new_v_ssm_conv_w', 'new_v_ssm_conv_b', 'new_v_ssm_dt_bias', 'new_v_ssm_a_log', 'new_v_ssm_d', 'new_v_ssm_norm', 'new_v_ssm_out_proj', 'new_v_kv_norm', 'new_v_w_kv', 'new_v_w_q', 'new_v_w_o', 'new_v_ffn_up', 'new_v_ffn_conv_w', 'new_v_ffn_conv_b', 'new_v_ffn_down', 'new_v_ple_gate', 'new_v_ple_proj', 'new_v_final_norm']
TWIN_LEAF_KINDS = {'loss': 'loss', 'grad_x': 'grad_x', 'grad_attn_norm': 'grad_w', 'grad_ffn_norm': 'grad_w', 'grad_ple_norm': 'grad_w', 'grad_ssm_in_proj': 'grad_w', 'grad_ssm_conv_w': 'grad_w', 'grad_ssm_conv_b': 'grad_w', 'grad_ssm_dt_bias': 'grad_w', 'grad_ssm_a_log': 'grad_w', 'grad_ssm_d': 'grad_w', 'grad_ssm_norm': 'grad_w', 'grad_ssm_out_proj': 'grad_w', 'grad_kv_norm': 'grad_w', 'grad_w_kv': 'grad_w', 'grad_w_q': 'grad_w', 'grad_w_o': 'grad_w', 'grad_ffn_up': 'grad_w', 'grad_ffn_conv_w': 'grad_w', 'grad_ffn_conv_b': 'grad_w', 'grad_ffn_down': 'grad_w', 'grad_ple_gate': 'grad_w', 'grad_ple_proj': 'grad_w', 'grad_final_norm': 'grad_w', 'delta_attn_norm': 'delta_w', 'delta_ffn_norm': 'delta_w', 'delta_ple_norm': 'delta_w', 'delta_ssm_in_proj': 'delta_w', 'delta_ssm_conv_w': 'delta_w', 'delta_ssm_conv_b': 'delta_w', 'delta_ssm_dt_bias': 'delta_w', 'delta_ssm_a_log': 'delta_w', 'delta_ssm_d': 'delta_w', 'delta_ssm_norm': 'delta_w', 'delta_ssm_out_proj': 'delta_w', 'delta_kv_norm': 'delta_w', 'delta_w_kv': 'delta_w', 'delta_w_q': 'delta_w', 'delta_w_o': 'delta_w', 'delta_ffn_up': 'delta_w', 'delta_ffn_conv_w': 'delta_w', 'delta_ffn_conv_b': 'delta_w', 'delta_ffn_down': 'delta_w', 'delta_ple_gate': 'delta_w', 'delta_ple_proj': 'delta_w', 'delta_final_norm': 'delta_w', 'new_m_attn_norm': 'new_m', 'new_m_ffn_norm': 'new_m', 'new_m_ple_norm': 'new_m', 'new_m_ssm_in_proj': 'new_m', 'new_m_ssm_conv_w': 'new_m', 'new_m_ssm_conv_b': 'new_m', 'new_m_ssm_dt_bias': 'new_m', 'new_m_ssm_a_log': 'new_m', 'new_m_ssm_d': 'new_m', 'new_m_ssm_norm': 'new_m', 'new_m_ssm_out_proj': 'new_m', 'new_m_kv_norm': 'new_m', 'new_m_w_kv': 'new_m', 'new_m_w_q': 'new_m', 'new_m_w_o': 'new_m', 'new_m_ffn_up': 'new_m', 'new_m_ffn_conv_w': 'new_m', 'new_m_ffn_conv_b': 'new_m', 'new_m_ffn_down': 'new_m', 'new_m_ple_gate': 'new_m', 'new_m_ple_proj': 'new_m', 'new_m_final_norm': 'new_m', 'new_v_attn_norm': 'new_v', 'new_v_ffn_norm': 'new_v', 'new_v_ple_norm': 'new_v', 'new_v_ssm_in_proj': 'new_v', 'new_v_ssm_conv_w': 'new_v', 'new_v_ssm_conv_b': 'new_v', 'new_v_ssm_dt_bias': 'new_v', 'new_v_ssm_a_log': 'new_v', 'new_v_ssm_d': 'new_v', 'new_v_ssm_norm': 'new_v', 'new_v_ssm_out_proj': 'new_v', 'new_v_kv_norm': 'new_v', 'new_v_w_kv': 'new_v', 'new_v_w_q': 'new_v', 'new_v_w_o': 'new_v', 'new_v_ffn_up': 'new_v', 'new_v_ffn_conv_w': 'new_v', 'new_v_ffn_conv_b': 'new_v', 'new_v_ffn_down': 'new_v', 'new_v_ple_gate': 'new_v', 'new_v_ple_proj': 'new_v', 'new_v_final_norm': 'new_v'}


def _forward(args):
    return _fwd_reference(*[args[k] for k in FWD_PARAMS])


def _output_shape():
    out = _jax.eval_shape(lambda: _forward(_fwd_setup_inputs(0)))
    return out.shape, out.dtype

N_MICROBATCH = 1
ADAM_LR = 0.001
ADAM_B1 = 0.9
ADAM_B2 = 0.999
ADAM_EPS = 1e-08
ADAM_WD = 0.01
ADAM_STEP = 10
PER_EXAMPLE_BATCH_AXIS = {'x': 0, 'p': 1, 'loss_target': 0}
SHARED_INPUTS = []
_WEIGHT_DTYPES = {'attn_norm': _jnp.float32, 'ffn_norm': _jnp.float32, 'ple_norm': _jnp.float32, 'ssm_in_proj': _jnp.float32, 'ssm_conv_w': _jnp.float32, 'ssm_conv_b': _jnp.float32, 'ssm_dt_bias': _jnp.float32, 'ssm_a_log': _jnp.float32, 'ssm_d': _jnp.float32, 'ssm_norm': _jnp.float32, 'ssm_out_proj': _jnp.float32, 'kv_norm': _jnp.float32, 'w_kv': _jnp.float32, 'w_q': _jnp.float32, 'w_o': _jnp.float32, 'ffn_up': _jnp.float32, 'ffn_conv_w': _jnp.float32, 'ffn_conv_b': _jnp.float32, 'ffn_down': _jnp.float32, 'ple_gate': _jnp.float32, 'ple_proj': _jnp.float32, 'final_norm': _jnp.float32}
MOMENT_SCALE = {'attn_norm': 2.084617e-01, 'ffn_norm': 1.225462e-01, 'ple_norm': 2.871779e-02, 'ssm_in_proj': 1.242057e-01, 'ssm_conv_w': 1.153584e-01, 'ssm_conv_b': 1.673495e-01, 'ssm_dt_bias': 3.088535e-01, 'ssm_a_log': 7.353540e-01, 'ssm_d': 7.242084e-01, 'ssm_norm': 1.339159e-01, 'ssm_out_proj': 1.891555e-01, 'kv_norm': 1.219753e-01, 'w_kv': 7.925653e-02, 'w_q': 3.249585e-02, 'w_o': 7.269573e-02, 'ffn_up': 5.079749e-02, 'ffn_conv_w': 5.073789e-02, 'ffn_conv_b': 5.213014e-02, 'ffn_down': 8.259536e-02, 'ple_gate': 2.870146e-02, 'ple_proj': 7.347113e-02, 'final_norm': 6.405063e+01}


def _to_microbatches(a, axis):
    t = _jnp.moveaxis(a, axis, 0)
    t = t.reshape((N_MICROBATCH, t.shape[0] // N_MICROBATCH) + t.shape[1:])
    return _jnp.moveaxis(t, 1, axis + 1)


def setup_inputs(seed: int = 0) -> dict:
    inp = _fwd_setup_inputs(seed)
    key = _jax.random.fold_in(_jax.random.key(seed), 7919)
    shape, _ = _output_shape()
    out = dict(inp)
    out["loss_target"] = _jax.random.normal(_jax.random.fold_in(key, 0), shape, _jnp.float32)
    for i, name in enumerate(TWIN_WEIGHTS):
        w = inp[name].astype(_jnp.float32)
        if MOMENT_SCALE is None:
            s = _jnp.sqrt(_jnp.mean(_jnp.square(w)) + 1e-30)
        else:
            s = MOMENT_SCALE[name]
        km, kv = _jax.random.split(_jax.random.fold_in(key, i + 1))
        out[name] = w
        out["m_" + name] = s * _jax.random.normal(km, w.shape, _jnp.float32)
        out["v_" + name] = (s * s) * _jax.random.uniform(kv, w.shape, _jnp.float32, 0.5, 1.5)
    if N_MICROBATCH > 1:
        for name, axis in PER_EXAMPLE_BATCH_AXIS.items():
            out[name] = _to_microbatches(out[name], axis)
    return {'x': out['x'], 'p': out['p'], 'attn_norm': out['attn_norm'], 'ffn_norm': out['ffn_norm'], 'ple_norm': out['ple_norm'], 'ssm_in_proj': out['ssm_in_proj'], 'ssm_conv_w': out['ssm_conv_w'], 'ssm_conv_b': out['ssm_conv_b'], 'ssm_dt_bias': out['ssm_dt_bias'], 'ssm_a_log': out['ssm_a_log'], 'ssm_d': out['ssm_d'], 'ssm_norm': out['ssm_norm'], 'ssm_out_proj': out['ssm_out_proj'], 'kv_norm': out['kv_norm'], 'w_kv': out['w_kv'], 'w_q': out['w_q'], 'w_o': out['w_o'], 'ffn_up': out['ffn_up'], 'ffn_conv_w': out['ffn_conv_w'], 'ffn_conv_b': out['ffn_conv_b'], 'ffn_down': out['ffn_down'], 'ple_gate': out['ple_gate'], 'ple_proj': out['ple_proj'], 'final_norm': out['final_norm'], 'loss_target': out['loss_target'], 'm_attn_norm': out['m_attn_norm'], 'm_ffn_norm': out['m_ffn_norm'], 'm_ple_norm': out['m_ple_norm'], 'm_ssm_in_proj': out['m_ssm_in_proj'], 'm_ssm_conv_w': out['m_ssm_conv_w'], 'm_ssm_conv_b': out['m_ssm_conv_b'], 'm_ssm_dt_bias': out['m_ssm_dt_bias'], 'm_ssm_a_log': out['m_ssm_a_log'], 'm_ssm_d': out['m_ssm_d'], 'm_ssm_norm': out['m_ssm_norm'], 'm_ssm_out_proj': out['m_ssm_out_proj'], 'm_kv_norm': out['m_kv_norm'], 'm_w_kv': out['m_w_kv'], 'm_w_q': out['m_w_q'], 'm_w_o': out['m_w_o'], 'm_ffn_up': out['m_ffn_up'], 'm_ffn_conv_w': out['m_ffn_conv_w'], 'm_ffn_conv_b': out['m_ffn_conv_b'], 'm_ffn_down': out['m_ffn_down'], 'm_ple_gate': out['m_ple_gate'], 'm_ple_proj': out['m_ple_proj'], 'm_final_norm': out['m_final_norm'], 'v_attn_norm': out['v_attn_norm'], 'v_ffn_norm': out['v_ffn_norm'], 'v_ple_norm': out['v_ple_norm'], 'v_ssm_in_proj': out['v_ssm_in_proj'], 'v_ssm_conv_w': out['v_ssm_conv_w'], 'v_ssm_conv_b': out['v_ssm_conv_b'], 'v_ssm_dt_bias': out['v_ssm_dt_bias'], 'v_ssm_a_log': out['v_ssm_a_log'], 'v_ssm_d': out['v_ssm_d'], 'v_ssm_norm': out['v_ssm_norm'], 'v_ssm_out_proj': out['v_ssm_out_proj'], 'v_kv_norm': out['v_kv_norm'], 'v_w_kv': out['v_w_kv'], 'v_w_q': out['v_w_q'], 'v_w_o': out['v_w_o'], 'v_ffn_up': out['v_ffn_up'], 'v_ffn_conv_w': out['v_ffn_conv_w'], 'v_ffn_conv_b': out['v_ffn_conv_b'], 'v_ffn_down': out['v_ffn_down'], 'v_ple_gate': out['v_ple_gate'], 'v_ple_proj': out['v_ple_proj'], 'v_final_norm': out['v_final_norm']}


def _loss(weights, diff, rest, loss_target):
    with _jax.named_scope("forward"):
        args = {**rest, TWIN_DIFF_INPUT: diff, **{k: w.astype(_WEIGHT_DTYPES[k]) for k, w in weights.items()}}
        y = _forward(args)
    with _jax.named_scope("loss_head"):
        err = _jnp.square(y.astype(_jnp.float32) - loss_target)
        return 0.5 * _jnp.sum(_jnp.mean(err, axis=-1)) if err.ndim else 0.5 * err


def _adamw(w, g, m, v):
    m = ADAM_B1 * m + (1.0 - ADAM_B1) * g
    v = ADAM_B2 * v + (1.0 - ADAM_B2) * _jnp.square(g)
    m_hat = m / (1.0 - ADAM_B1 ** ADAM_STEP)
    v_hat = v / (1.0 - ADAM_B2 ** ADAM_STEP)
    delta = -ADAM_LR * (m_hat / (_jnp.sqrt(v_hat) + ADAM_EPS) + ADAM_WD * w)
    return delta, m, v


def reference(x, p, attn_norm, ffn_norm, ple_norm, ssm_in_proj, ssm_conv_w, ssm_conv_b, ssm_dt_bias, ssm_a_log, ssm_d, ssm_norm, ssm_out_proj, kv_norm, w_kv, w_q, w_o, ffn_up, ffn_conv_w, ffn_conv_b, ffn_down, ple_gate, ple_proj, final_norm, loss_target, m_attn_norm, m_ffn_norm, m_ple_norm, m_ssm_in_proj, m_ssm_conv_w, m_ssm_conv_b, m_ssm_dt_bias, m_ssm_a_log, m_ssm_d, m_ssm_norm, m_ssm_out_proj, m_kv_norm, m_w_kv, m_w_q, m_w_o, m_ffn_up, m_ffn_conv_w, m_ffn_conv_b, m_ffn_down, m_ple_gate, m_ple_proj, m_final_norm, v_attn_norm, v_ffn_norm, v_ple_norm, v_ssm_in_proj, v_ssm_conv_w, v_ssm_conv_b, v_ssm_dt_bias, v_ssm_a_log, v_ssm_d, v_ssm_norm, v_ssm_out_proj, v_kv_norm, v_w_kv, v_w_q, v_w_o, v_ffn_up, v_ffn_conv_w, v_ffn_conv_b, v_ffn_down, v_ple_gate, v_ple_proj, v_final_norm):
    given = dict(x=x, p=p, attn_norm=attn_norm, ffn_norm=ffn_norm, ple_norm=ple_norm, ssm_in_proj=ssm_in_proj, ssm_conv_w=ssm_conv_w, ssm_conv_b=ssm_conv_b, ssm_dt_bias=ssm_dt_bias, ssm_a_log=ssm_a_log, ssm_d=ssm_d, ssm_norm=ssm_norm, ssm_out_proj=ssm_out_proj, kv_norm=kv_norm, w_kv=w_kv, w_q=w_q, w_o=w_o, ffn_up=ffn_up, ffn_conv_w=ffn_conv_w, ffn_conv_b=ffn_conv_b, ffn_down=ffn_down, ple_gate=ple_gate, ple_proj=ple_proj, final_norm=final_norm, loss_target=loss_target, m_attn_norm=m_attn_norm, m_ffn_norm=m_ffn_norm, m_ple_norm=m_ple_norm, m_ssm_in_proj=m_ssm_in_proj, m_ssm_conv_w=m_ssm_conv_w, m_ssm_conv_b=m_ssm_conv_b, m_ssm_dt_bias=m_ssm_dt_bias, m_ssm_a_log=m_ssm_a_log, m_ssm_d=m_ssm_d, m_ssm_norm=m_ssm_norm, m_ssm_out_proj=m_ssm_out_proj, m_kv_norm=m_kv_norm, m_w_kv=m_w_kv, m_w_q=m_w_q, m_w_o=m_w_o, m_ffn_up=m_ffn_up, m_ffn_conv_w=m_ffn_conv_w, m_ffn_conv_b=m_ffn_conv_b, m_ffn_down=m_ffn_down, m_ple_gate=m_ple_gate, m_ple_proj=m_ple_proj, m_final_norm=m_final_norm, v_attn_norm=v_attn_norm, v_ffn_norm=v_ffn_norm, v_ple_norm=v_ple_norm, v_ssm_in_proj=v_ssm_in_proj, v_ssm_conv_w=v_ssm_conv_w, v_ssm_conv_b=v_ssm_conv_b, v_ssm_dt_bias=v_ssm_dt_bias, v_ssm_a_log=v_ssm_a_log, v_ssm_d=v_ssm_d, v_ssm_norm=v_ssm_norm, v_ssm_out_proj=v_ssm_out_proj, v_kv_norm=v_kv_norm, v_w_kv=v_w_kv, v_w_q=v_w_q, v_w_o=v_w_o, v_ffn_up=v_ffn_up, v_ffn_conv_w=v_ffn_conv_w, v_ffn_conv_b=v_ffn_conv_b, v_ffn_down=v_ffn_down, v_ple_gate=v_ple_gate, v_ple_proj=v_ple_proj, v_final_norm=v_final_norm)
    weights = {n: given[n] for n in TWIN_WEIGHTS}
    shared = {n: given[n] for n in SHARED_INPUTS}
    per_example = {n: given[n] for n in ['x', 'p']}
    grad_fn = _jax.value_and_grad(_loss, argnums=(0, 1))

    def one_microbatch(ex, loss_target):
        ex = dict(ex)
        diff = ex.pop(TWIN_DIFF_INPUT)
        return grad_fn(weights, diff, {**shared, **ex}, loss_target)

    if N_MICROBATCH == 1:
        loss, (grad_w, grad_x) = one_microbatch(per_example, given["loss_target"])
    else:
        def body(carry, xs):
            loss_sum, grad_sum = carry
            l_k, (gw_k, gx_k) = one_microbatch(xs[0], xs[1])
            with _jax.named_scope("update"):
                return (loss_sum + l_k, _jax.tree.map(_jnp.add, grad_sum, gw_k)), gx_k

        init = (_jnp.zeros((), _jnp.float32), _jax.tree.map(_jnp.zeros_like, weights))
        (loss, grad_w), grad_x = _jax.lax.scan(body, init, (per_example, given["loss_target"]))
    with _jax.named_scope("update"):
        delta_w, new_m, new_v = {}, {}, {}
        for n in TWIN_WEIGHTS:
            delta_w[n], new_m[n], new_v[n] = _adamw(weights[n], grad_w[n], given["m_" + n], given["v_" + n])
    return (loss, grad_x, *[grad_w[n] for n in TWIN_WEIGHTS], *[delta_w[n] for n in TWIN_WEIGHTS],
            *[new_m[n] for n in TWIN_WEIGHTS], *[new_v[n] for n in TWIN_WEIGHTS])
```

```python
import functools

import jax
import jax.numpy as jnp
from jax import lax
from jax.experimental import pallas as pl
from jax.experimental.pallas import tpu as pltpu

F32 = jnp.float32
BF16 = jnp.bfloat16
I32 = jnp.int32

NORM_EPS = 1e-6
SSM_NORM_EPS = 1e-5
SSM_HEAD_DIM = 64
SSM_STATE = 128
SSM_CHUNK = 128
SSM_HEADS_PER_GROUP = 8
SSM_GROUP_W = SSM_HEADS_PER_GROUP * SSM_HEAD_DIM
SSM_CONV = 4
SSM_ROWS = 16
SB_HEAD_DIM = 64
SB_BLOCK = 128
SB_SCALE = SB_HEAD_DIM ** -0.5
FFN_CONV = 3
LANES = 128
N_CHIPS = 4
N_DEV = 8

ADAM_LR = 0.001
ADAM_B1 = 0.9
ADAM_B2 = 0.999
ADAM_EPS = 1e-08
ADAM_WD = 0.01
ADAM_STEP = 10

MESH = pl.DeviceIdType.MESH
ANY = pl.BlockSpec(memory_space=pl.ANY)

SHARD_AXIS = {
    "ssm_in_proj": 2, "ssm_conv_w": 2, "ssm_conv_b": 1, "ssm_norm": 1, "ssm_out_proj": 1,
    "w_kv": 1, "w_q": 1, "w_o": 1, "ffn_up": 2, "ffn_conv_w": 2, "ffn_down": 1,
    "ple_gate": 1, "ple_proj": 2,
}
REPLICATED = ["attn_norm", "ffn_norm", "ple_norm", "ssm_dt_bias", "ssm_a_log", "ssm_d",
              "kv_norm", "ffn_conv_b", "final_norm"]
WEIGHTS = ["attn_norm", "ffn_norm", "ple_norm", "ssm_in_proj", "ssm_conv_w", "ssm_conv_b",
           "ssm_dt_bias", "ssm_a_log", "ssm_d", "ssm_norm", "ssm_out_proj", "kv_norm", "w_kv",
           "w_q", "w_o", "ffn_up", "ffn_conv_w", "ffn_conv_b", "ffn_down", "ple_gate",
           "ple_proj", "final_norm"]
SHARDED = [n for n in WEIGHTS if n in SHARD_AXIS]
PACK_ROWS = 2048


def _tile(n, pref):
    t = (min(pref, n) // 128) * 128
    while t >= 128:
        if n % t == 0:
            return t
        t -= 128
    return n


def _dot(a, b):
    return jnp.dot(a, b, preferred_element_type=F32)


def _dot_nt(a, b):
    return lax.dot_general(a, b, (((1,), (1,)), ((), ())), preferred_element_type=F32)


def _dot_tn(a, b):
    return lax.dot_general(a, b, (((0,), (0,)), ((), ())), preferred_element_type=F32)


def _split2(x):
    hi = x.astype(BF16)
    lo = (x - hi.astype(F32)).astype(BF16)
    return hi, lo


def _dot2(x, m):
    hi, lo = _split2(x)
    return _dot(hi, m) + _dot(lo, m)


def _dot2_left(m, x):
    hi, lo = _split2(x)
    return _dot(m, hi) + _dot(m, lo)


def _softplus(x):
    return jnp.maximum(x, 0.0) + jnp.log(1.0 + jnp.exp(-jnp.abs(x)))


def _sigmoid(x):
    return jax.nn.sigmoid(x)


def _params(*sem):
    return pltpu.CompilerParams(dimension_semantics=sem)


def _mm(a, b, *, name, ta=False, tb=False, add=None, out_dtype=F32, tm=1024, tn=512, tk=1024):
    m = a.shape[1] if ta else a.shape[0]
    k = a.shape[0] if ta else a.shape[1]
    n = b.shape[0] if tb else b.shape[1]
    assert (b.shape[1] if tb else b.shape[0]) == k, (a.shape, b.shape, ta, tb)
    tm, tn, tk = _tile(m, tm), _tile(n, tn), _tile(k, tk)
    nk = k // tk
    dims = (((0 if ta else 1,), (1 if tb else 0,)), ((), ()))
    has_add = add is not None

    def body(*refs):
        if has_add:
            a_ref, b_ref, add_ref, o_ref, acc_ref = refs
        else:
            a_ref, b_ref, o_ref, acc_ref = refs
        kk = pl.program_id(2)

        @pl.when(kk == 0)
        def _():
            acc_ref[...] = jnp.zeros_like(acc_ref)

        acc_ref[...] += lax.dot_general(a_ref[...].astype(BF16), b_ref[...].astype(BF16), dims,
                                        preferred_element_type=F32)

        @pl.when(kk == nk - 1)
        def _():
            r = acc_ref[...]
            if has_add:
                r = r + add_ref[...].astype(F32)
            o_ref[...] = r.astype(out_dtype)

    a_spec = (pl.BlockSpec((tk, tm), lambda i, j, kk: (kk, i)) if ta
              else pl.BlockSpec((tm, tk), lambda i, j, kk: (i, kk)))
    b_spec = (pl.BlockSpec((tn, tk), lambda i, j, kk: (j, kk)) if tb
              else pl.BlockSpec((tk, tn), lambda i, j, kk: (kk, j)))
    o_spec = pl.BlockSpec((tm, tn), lambda i, j, kk: (i, j))
    in_specs = [a_spec, b_spec] + ([o_spec] if has_add else [])
    args = (a, b) + ((add,) if has_add else ())
    return pl.pallas_call(
        body, name=name, grid=(m // tm, n // tn, nk), in_specs=in_specs, out_specs=o_spec,
        out_shape=jax.ShapeDtypeStruct((m, n), out_dtype),
        scratch_shapes=[pltpu.VMEM((tm, tn), F32)],
        compiler_params=_params("parallel", "parallel", "arbitrary"),
    )(*args)


def _rmsnorm_fwd(x, gain, *, name, rows=512):
    t, d = x.shape
    tr = _tile(t, rows)

    def body(x_ref, g_ref, o_ref):
        xv = x_ref[...]
        r = lax.rsqrt(jnp.mean(xv * xv, axis=-1, keepdims=True) + NORM_EPS)
        o_ref[...] = ((xv * r) * g_ref[...]).astype(BF16)

    return pl.pallas_call(
        body, name=name, grid=(t // tr,),
        in_specs=[pl.BlockSpec((tr, d), lambda i: (i, 0)), pl.BlockSpec((1, d), lambda i: (0, 0))],
        out_specs=pl.BlockSpec((tr, d), lambda i: (i, 0)),
        out_shape=jax.ShapeDtypeStruct((t, d), BF16),
        compiler_params=_params("parallel"),
    )(x, gain.reshape(1, d))


def _rmsnorm_bwd(x, gain, dy, dres, *, name, rows=512):
    t, d = x.shape
    tr = _tile(t, rows)

    def body(x_ref, g_ref, dy_ref, dres_ref, dx_ref, dg_ref):
        xv = x_ref[...]
        r = lax.rsqrt(jnp.mean(xv * xv, axis=-1, keepdims=True) + NORM_EPS)
        xh = xv * r
        dyv = dy_ref[...].astype(F32)
        dxh = dyv * g_ref[...]
        dx = r * (dxh - xh * jnp.mean(dxh * xh, axis=-1, keepdims=True))
        dx_ref[...] = dres_ref[...] + dx
        part = jnp.sum(dyv * xh, axis=0, keepdims=True)

        @pl.when(pl.program_id(0) == 0)
        def _():
            dg_ref[...] = part

        @pl.when(pl.program_id(0) > 0)
        def _():
            dg_ref[...] += part

    row = pl.BlockSpec((tr, d), lambda i: (i, 0))
    vec = pl.BlockSpec((1, d), lambda i: (0, 0))
    dx, dg = pl.pallas_call(
        body, name=name, grid=(t // tr,), in_specs=[row, vec, row, row], out_specs=[row, vec],
        out_shape=[jax.ShapeDtypeStruct((t, d), F32), jax.ShapeDtypeStruct((1, d), F32)],
        compiler_params=_params("arbitrary"),
    )(x, gain.reshape(1, d), dy, dres)
    return dx, dg.reshape(d)


def _final_loss(h, gain, target, *, rows=512):
    t, d = h.shape
    tr = _tile(t, rows)

    def body(x_ref, g_ref, tg_ref, dx_ref, dg_ref, loss_ref):
        xv = x_ref[...]
        g = g_ref[...]
        r = lax.rsqrt(jnp.mean(xv * xv, axis=-1, keepdims=True) + NORM_EPS)
        xh = xv * r
        err = xh * g - tg_ref[...]
        dyv = err * (1.0 / d)
        dxh = dyv * g
        dx_ref[...] = r * (dxh - xh * jnp.mean(dxh * xh, axis=-1, keepdims=True))
        part = jnp.sum(dyv * xh, axis=0, keepdims=True)
        lpart = jnp.zeros((1, LANES), F32) + (0.5 / d) * jnp.sum(err * err)

        @pl.when(pl.program_id(0) == 0)
        def _():
            dg_ref[...] = part
            loss_ref[...] = lpart

        @pl.when(pl.program_id(0) > 0)
        def _():
            dg_ref[...] += part
            loss_ref[...] += lpart

    row = pl.BlockSpec((tr, d), lambda i: (i, 0))
    vec = pl.BlockSpec((1, d), lambda i: (0, 0))
    dx, dg, loss = pl.pallas_call(
        body, name="final_loss", grid=(t // tr,), in_specs=[row, vec, row],
        out_specs=[row, vec, pl.BlockSpec((1, LANES), lambda i: (0, 0))],
        out_shape=[jax.ShapeDtypeStruct((t, d), F32), jax.ShapeDtypeStruct((1, d), F32),
                   jax.ShapeDtypeStruct((1, LANES), F32)],
        compiler_params=_params("arbitrary"),
    )(h, gain.reshape(1, d), target)
    return loss[0, 0], dx, dg.reshape(d)


def _ple_fwd(h, a, pp, *, name, rows=512):
    t, d = h.shape
    tr = _tile(t, rows)

    def body(h_ref, a_ref, p_ref, o_ref):
        o_ref[...] = h_ref[...] + _sigmoid(a_ref[...]) * p_ref[...]

    row = pl.BlockSpec((tr, d), lambda i: (i, 0))
    return pl.pallas_call(
        body, name=name, grid=(t // tr,), in_specs=[row, row, row], out_specs=row,
        out_shape=jax.ShapeDtypeStruct((t, d), F32), compiler_params=_params("parallel"),
    )(h, a, pp)


def _ple_bwd(dh, a, pp, *, name, rows=512):
    t, d = dh.shape
    tr = _tile(t, rows)

    def body(dh_ref, a_ref, p_ref, da_ref, dp_ref):
        s = _sigmoid(a_ref[...])
        dhv = dh_ref[...]
        da_ref[...] = (dhv * p_ref[...] * (s * (1.0 - s))).astype(BF16)
        dp_ref[...] = (dhv * s).astype(BF16)

    row = pl.BlockSpec((tr, d), lambda i: (i, 0))
    return pl.pallas_call(
        body, name=name, grid=(t // tr,), in_specs=[row, row, row], out_specs=[row, row],
        out_shape=[jax.ShapeDtypeStruct((t, d), BF16)] * 2, compiler_params=_params("parallel"),
    )(dh, a, pp)


def _shift_down(u, j):
    if j == 0:
        return u
    rows = lax.broadcasted_iota(I32, u.shape, 0)
    return jnp.where(rows >= j, pltpu.roll(u, j, 0), 0.0)


def _shift_up(u, j):
    if j == 0:
        return u
    s = u.shape[0]
    rows = lax.broadcasted_iota(I32, u.shape, 0)
    return jnp.where(rows < s - j, pltpu.roll(u, s - j, 0), 0.0)


def _conv_pre(u, wv, bv):
    kw = wv.shape[0]
    shifted = [_shift_down(u, kw - 1 - k) for k in range(kw)]
    pre = bv + wv[0:1, :] * shifted[0]
    for k in range(1, kw):
        pre = pre + wv[k:k + 1, :] * shifted[k]
    return pre, shifted


def _conv_back(dpre, wv, shifted, dw_ref, db_ref, first):
    kw = wv.shape[0]
    du = wv[kw - 1:kw, :] * dpre
    for k in range(kw - 1):
        du = du + wv[k:k + 1, :] * _shift_up(dpre, kw - 1 - k)
    dws = [jnp.sum(dpre * shifted[k], axis=0, keepdims=True) for k in range(kw)]
    dbs = jnp.sum(dpre, axis=0, keepdims=True)

    @pl.when(first)
    def _():
        for k in range(kw):
            dw_ref[k:k + 1, :] = dws[k]
        db_ref[...] = dbs

    @pl.when(jnp.logical_not(first))
    def _():
        for k in range(kw):
            dw_ref[k:k + 1, :] += dws[k]
        db_ref[...] += dbs

    return du


def _dsilu(pre):
    s = _sigmoid(pre)
    return s, s * (1.0 + pre * (1.0 - s))


def _conv_silu_fwd(zx, off, w, b, *, name, tc=128):
    bsz, s, _ = zx.shape
    kw, c = w.shape
    o0 = off // tc

    def body(u_ref, w_ref, b_ref, o_ref):
        pre, _ = _conv_pre(u_ref[0], w_ref[...], b_ref[...])
        o_ref[0] = pre * _sigmoid(pre)

    return pl.pallas_call(
        body, name=name, grid=(bsz, c // tc),
        in_specs=[pl.BlockSpec((1, s, tc), lambda i, j: (i, 0, o0 + j)),
                  pl.BlockSpec((kw, tc), lambda i, j: (0, j)),
                  pl.BlockSpec((1, tc), lambda i, j: (0, j))],
        out_specs=pl.BlockSpec((1, s, tc), lambda i, j: (i, 0, j)),
        out_shape=jax.ShapeDtypeStruct((bsz, s, c), F32),
        compiler_params=_params("parallel", "parallel"),
    )(zx, w, b.reshape(1, c))


def _conv_silu_bwd(zx, off, w, b, dout, *, name, tc=128):
    bsz, s, _ = zx.shape
    kw, c = w.shape
    o0 = off // tc

    def body(u_ref, w_ref, b_ref, dy_ref, du_ref, dw_ref, db_ref):
        wv = w_ref[...]
        pre, shifted = _conv_pre(u_ref[0], wv, b_ref[...])
        _, ds = _dsilu(pre)
        du = _conv_back(dy_ref[0] * ds, wv, shifted, dw_ref, db_ref, pl.program_id(1) == 0)
        du_ref[0] = du.astype(BF16)

    du, dw, db = pl.pallas_call(
        body, name=name, grid=(c // tc, bsz),
        in_specs=[pl.BlockSpec((1, s, tc), lambda j, i: (i, 0, o0 + j)),
                  pl.BlockSpec((kw, tc), lambda j, i: (0, j)),
                  pl.BlockSpec((1, tc), lambda j, i: (0, j)),
                  pl.BlockSpec((1, s, tc), lambda j, i: (i, 0, j))],
        out_specs=[pl.BlockSpec((1, s, tc), lambda j, i: (i, 0, j)),
                   pl.BlockSpec((kw, tc), lambda j, i: (0, j)),
                   pl.BlockSpec((1, tc), lambda j, i: (0, j))],
        out_shape=[jax.ShapeDtypeStruct((bsz, s, c), BF16), jax.ShapeDtypeStruct((kw, c), F32),
                   jax.ShapeDtypeStruct((1, c), F32)],
        compiler_params=_params("parallel", "arbitrary"),
    )(zx, w, b.reshape(1, c), dout)
    return du, dw, db.reshape(c)


def _conv_glu_fwd(up, w, b, *, name, tc=128):
    bsz, s, c2 = up.shape
    kw = w.shape[0]
    f = c2 // 2
    nt = f // tc

    def body(ug_ref, uv_ref, wg_ref, wv_ref, bg_ref, bv_ref, o_ref):
        pg, _ = _conv_pre(ug_ref[0], wg_ref[...], bg_ref[...])
        pv, _ = _conv_pre(uv_ref[0], wv_ref[...], bv_ref[...])
        o_ref[0] = (pg * _sigmoid(pg) * pv).astype(BF16)

    b2 = b.reshape(1, c2)
    return pl.pallas_call(
        body, name=name, grid=(bsz, nt),
        in_specs=[pl.BlockSpec((1, s, tc), lambda i, j: (i, 0, j)),
                  pl.BlockSpec((1, s, tc), lambda i, j: (i, 0, nt + j)),
                  pl.BlockSpec((kw, tc), lambda i, j: (0, j)),
                  pl.BlockSpec((kw, tc), lambda i, j: (0, nt + j)),
                  pl.BlockSpec((1, tc), lambda i, j: (0, j)),
                  pl.BlockSpec((1, tc), lambda i, j: (0, nt + j))],
        out_specs=pl.BlockSpec((1, s, tc), lambda i, j: (i, 0, j)),
        out_shape=jax.ShapeDtypeStruct((bsz, s, f), BF16),
        compiler_params=_params("parallel", "parallel"),
    )(up, up, w, w, b2, b2)


def _conv_glu_bwd(up, w, b, df, *, name, tc=128):
    bsz, s, c2 = up.shape
    kw = w.shape[0]
    f = c2 // 2
    nt = f // tc

    def body(ug_ref, uv_ref, wg_ref, wv_ref, bg_ref, bv_ref, df_ref,
             dug_ref, duv_ref, dwg_ref, dwv_ref, dbg_ref, dbv_ref):
        first = pl.program_id(1) == 0
        wg = wg_ref[...]
        wv = wv_ref[...]
        pg, sg = _conv_pre(ug_ref[0], wg, bg_ref[...])
        pv, sv = _conv_pre(uv_ref[0], wv, bv_ref[...])
        sig, dsl = _dsilu(pg)
        dfv = df_ref[0]
        dug_ref[0] = _conv_back(dfv * pv * dsl, wg, sg, dwg_ref, dbg_ref, first).astype(BF16)
        duv_ref[0] = _conv_back(dfv * (pg * sig), wv, sv, dwv_ref, dbv_ref, first).astype(BF16)

    b2 = b.reshape(1, c2)
    act = lambda j, i: (i, 0, j)
    wsp = pl.BlockSpec((kw, tc), lambda j, i: (0, j))
    bsp = pl.BlockSpec((1, tc), lambda j, i: (0, j))
    dug, duv, dwg, dwv, dbg, dbv = pl.pallas_call(
        body, name=name, grid=(nt, bsz),
        in_specs=[pl.BlockSpec((1, s, tc), act),
                  pl.BlockSpec((1, s, tc), lambda j, i: (i, 0, nt + j)),
                  wsp, pl.BlockSpec((kw, tc), lambda j, i: (0, nt + j)),
                  bsp, pl.BlockSpec((1, tc), lambda j, i: (0, nt + j)),
                  pl.BlockSpec((1, s, tc), act)],
        out_specs=[pl.BlockSpec((1, s, tc), act), pl.BlockSpec((1, s, tc), act), wsp, wsp, bsp, bsp],
        out_shape=[jax.ShapeDtypeStruct((bsz, s, f), BF16)] * 2
        + [jax.ShapeDtypeStruct((kw, f), F32)] * 2 + [jax.ShapeDtypeStruct((1, f), F32)] * 2,
        compiler_params=_params("parallel", "arbitrary"),
    )(up, up, w, w, b2, b2, df)
    return (dug, duv, jnp.concatenate([dwg, dwv], axis=1),
            jnp.concatenate([dbg.reshape(f), dbv.reshape(f)]))


def _ssd_shared(xs, bm, cm, dtc_raw, dtr_raw, plane, psub, st):
    cl = SSM_CHUNK
    bias_l, a_l = plane[0:1, :], plane[1:2, :]
    bias_s, a_s = psub[:, 0:1], psub[:, 1:2]
    ri = lax.broadcasted_iota(I32, (cl, cl), 0)
    ci = lax.broadcasted_iota(I32, (cl, cl), 1)
    tril = ri >= ci
    low_incl = tril.astype(BF16)
    up_incl = (ri <= ci).astype(BF16)
    seg_t = (lax.broadcasted_iota(I32, (LANES, SSM_GROUP_W), 0)
             == lax.broadcasted_iota(I32, (LANES, SSM_GROUP_W), 1) // SSM_HEAD_DIM).astype(BF16)
    dt_c = _softplus(dtc_raw + bias_l)
    cs_c = _dot2_left(low_incl, dt_c * a_l)
    dt_r = _softplus(dtr_raw + bias_s)
    cs_r = _dot2(dt_r * a_s, up_incl)
    dt_ch = _dot2(dt_c, seg_t)
    cs_ch = _dot2(cs_c, seg_t)
    cs_last = cs_ch[cl - 1:cl, :]
    decay_ch = jnp.exp(cs_ch)
    w_ch = jnp.exp(cs_last - cs_ch)
    tot_ch = jnp.exp(cs_last)
    xdt = xs * dt_ch
    bm_b, cm_b = bm.astype(BF16), cm.astype(BF16)
    gmat = _dot_nt(cm_b, bm_b)
    cst = _dot(cm_b, st.astype(BF16))
    yoff = decay_ch * cst
    return dict(tril=tril, low_incl=low_incl, up_incl=up_incl, seg_t=seg_t, a_l=a_l, bias_l=bias_l,
                dt_c=dt_c, cs_c=cs_c, cs_r=cs_r, dt_ch=dt_ch, decay_ch=decay_ch, w_ch=w_ch,
                tot_ch=tot_ch, xdt=xdt, bm_b=bm_b, cm_b=cm_b, gmat=gmat, yoff=yoff)


def _head_decay(q, r):
    diff = q["cs_c"][:, r:r + 1] - q["cs_r"][r:r + 1, :]
    return jnp.where(q["tril"], jnp.exp(jnp.minimum(diff, 0.0)), 0.0)


def _half_mask(hh):
    lane = lax.broadcasted_iota(I32, (SSM_CHUNK, LANES), 1)
    return (lane < SSM_HEAD_DIM) if hh == 0 else (lane >= SSM_HEAD_DIM)


def _ssd_ydiag(q):
    pairs = []
    for pr in range(SSM_HEADS_PER_GROUP // 2):
        xp = q["xdt"][:, pr * LANES:(pr + 1) * LANES]
        acc = None
        for hh in range(2):
            mm_ = (q["gmat"] * _head_decay(q, 2 * pr + hh)).astype(BF16)
            part = _dot(mm_, jnp.where(_half_mask(hh), xp, 0.0).astype(BF16))
            acc = part if acc is None else acc + part
        pairs.append(acc)
    return jnp.concatenate(pairs, axis=1)


def _ssd_specs(bsz, s, g_n, d_inner, rev):
    cl = SSM_CHUNK
    nc = s // cl
    cc = (lambda c: nc - 1 - c) if rev else (lambda c: c)
    gb = d_inner // LANES
    dt0 = (d_inner + d_inner + 2 * g_n * SSM_STATE) // LANES
    return dict(
        z=pl.BlockSpec((1, cl, SSM_GROUP_W), lambda b, g, c: (b, cc(c), g)),
        dtc=pl.BlockSpec((1, cl, LANES), lambda b, g, c: (b, cc(c), dt0 + g)),
        xs=pl.BlockSpec((1, cl, SSM_GROUP_W), lambda b, g, c: (b, cc(c), g)),
        bm=pl.BlockSpec((1, cl, LANES), lambda b, g, c: (b, cc(c), gb + g)),
        cm=pl.BlockSpec((1, cl, LANES), lambda b, g, c: (b, cc(c), gb + g_n + g)),
        dtr=pl.BlockSpec((1, 1, SSM_ROWS, cl), lambda b, g, c: (b, g, 0, cc(c))),
        plane=pl.BlockSpec((1, 8, LANES), lambda b, g, c: (g, 0, 0)),
        psub=pl.BlockSpec((1, SSM_ROWS, LANES), lambda b, g, c: (g, 0, 0)),
        chan=pl.BlockSpec((1, SSM_GROUP_W), lambda b, g, c: (0, g)),
        state=pl.BlockSpec((1, 1, 1, SSM_STATE, SSM_GROUP_W), lambda b, g, c: (b, g, cc(c), 0, 0)),
        bgrp=pl.BlockSpec((1, cl, LANES), lambda b, g, c: (b, cc(c), g)),
    )


def _ssd_fwd(zx, xbc, dtr_row, plane, psub, d_ch, nw, *, name):
    bsz, s, _ = zx.shape
    d_inner = d_ch.shape[1]
    g_n = d_inner // SSM_GROUP_W
    nc = s // SSM_CHUNK
    sp = _ssd_specs(bsz, s, g_n, d_inner, False)

    def body(z_ref, dtc_ref, xs_ref, bm_ref, cm_ref, dtr_ref, plane_ref, psub_ref, d_ref, nw_ref,
             gn_ref, st_out_ref, st_ref):
        @pl.when(pl.program_id(2) == 0)
        def _():
            st_ref[...] = jnp.zeros_like(st_ref)

        xs = xs_ref[0]
        st = st_ref[...]
        st_out_ref[0, 0, 0] = st
        q = _ssd_shared(xs, bm_ref[0], cm_ref[0], dtc_ref[0], dtr_ref[0, 0], plane_ref[0],
                        psub_ref[0], st)
        y = _ssd_ydiag(q) + q["yoff"] + xs * d_ref[...]
        st_ref[...] = q["tot_ch"] * st + _dot_tn(q["bm_b"], (q["w_ch"] * q["xdt"]).astype(BF16))
        zv = z_ref[0]
        gy = y * (zv * _sigmoid(zv))
        rstd = lax.rsqrt(jnp.mean(gy * gy, axis=-1, keepdims=True) + SSM_NORM_EPS)
        gn_ref[0] = ((gy * rstd) * nw_ref[...]).astype(BF16)

    return pl.pallas_call(
        body, name=name, grid=(bsz, g_n, nc),
        in_specs=[sp["z"], sp["dtc"], sp["xs"], sp["bm"], sp["cm"], sp["dtr"], sp["plane"],
                  sp["psub"], sp["chan"], sp["chan"]],
        out_specs=[sp["z"], sp["state"]],
        out_shape=[jax.ShapeDtypeStruct((bsz, s, d_inner), BF16),
                   jax.ShapeDtypeStruct((bsz, g_n, nc, SSM_STATE, SSM_GROUP_W), F32)],
        scratch_shapes=[pltpu.VMEM((SSM_STATE, SSM_GROUP_W), F32)],
        compiler_params=_params("parallel", "parallel", "arbitrary"),
    )(zx, zx, xbc, xbc, xbc, dtr_row, plane, psub, d_ch, nw)


def _ssd_bwd(zx, xbc, dtr_row, plane, psub, d_ch, nw, states, dgn, *, name):
    bsz, s, _ = zx.shape
    d_inner = d_ch.shape[1]
    g_n = d_inner // SSM_GROUP_W
    cl = SSM_CHUNK
    nc = s // cl
    sp = _ssd_specs(bsz, s, g_n, d_inner, True)
    acc_ch = pl.BlockSpec((1, 1, 8, SSM_GROUP_W), lambda b, g, c: (b, g, 0, 0))
    acc_ln = pl.BlockSpec((1, 1, 8, LANES), lambda b, g, c: (b, g, 0, 0))

    def body(z_ref, dtc_ref, xs_ref, bm_ref, cm_ref, dtr_ref, plane_ref, psub_ref, d_ref, nw_ref,
             st_in_ref, dgn_ref,
             dxs_ref, dbm_ref, dcm_ref, dz_ref, ddt_ref, ach_ref, aln_ref, dst_ref):
        first = pl.program_id(2) == 0

        @pl.when(first)
        def _():
            dst_ref[...] = jnp.zeros_like(dst_ref)
            ach_ref[...] = jnp.zeros_like(ach_ref)
            aln_ref[...] = jnp.zeros_like(aln_ref)

        xs = xs_ref[0]
        st = st_in_ref[0, 0, 0]
        q = _ssd_shared(xs, bm_ref[0], cm_ref[0], dtc_ref[0], dtr_ref[0, 0], plane_ref[0],
                        psub_ref[0], st)
        d_chv = d_ref[...]
        nwv = nw_ref[...]
        y = _ssd_ydiag(q) + q["yoff"] + xs * d_chv
        zv = z_ref[0]
        sz = _sigmoid(zv)
        silu_z = zv * sz
        gy = y * silu_z
        rstd = lax.rsqrt(jnp.mean(gy * gy, axis=-1, keepdims=True) + SSM_NORM_EPS)
        gyh = gy * rstd
        dgnv = dgn_ref[0]
        dgyh = dgnv * nwv
        dgy = rstd * (dgyh - gyh * jnp.mean(dgyh * gyh, axis=-1, keepdims=True))
        dy = dgy * silu_z
        dz_ref[0] = (dgy * y * (sz * (1.0 + zv * (1.0 - sz)))).astype(BF16)
        ach_ref[0, 0, 0:1, :] += jnp.sum(dgnv * gyh, axis=0, keepdims=True)
        ach_ref[0, 0, 1:2, :] += jnp.sum(dy * xs, axis=0, keepdims=True)
        st_b = st.astype(BF16)
        dyd = (dy * q["decay_ch"]).astype(BF16)
        dcm = _dot_nt(dyd, st_b)
        dstn = dst_ref[...]
        dstn_b = dstn.astype(BF16)
        bds = _dot(q["bm_b"], dstn_b)
        wx = q["w_ch"] * q["xdt"]
        dbm = _dot_nt(wx.astype(BF16), dstn_b)
        dst_ref[...] = q["tot_ch"] * dstn + _dot_tn(q["cm_b"], dyd)
        vterm = wx * bds
        cs_terms = dy * q["yoff"] - vterm
        last_ch = q["tot_ch"] * jnp.sum(dstn * st, axis=0, keepdims=True) + jnp.sum(vterm, axis=0, keepdims=True)
        lane = lax.broadcasted_iota(I32, (cl, LANES), 1)
        rowi = lax.broadcasted_iota(I32, (SSM_ROWS, cl), 0)
        dg_sum = jnp.zeros((cl, cl), F32)
        dcs_col = jnp.zeros((cl, LANES), F32)
        dcs_row = jnp.zeros((SSM_ROWS, cl), F32)
        dxdt_pairs = []
        for pr in range(SSM_HEADS_PER_GROUP // 2):
            xp_b = q["xdt"][:, pr * LANES:(pr + 1) * LANES].astype(BF16)
            dyp = dy[:, pr * LANES:(pr + 1) * LANES]
            acc = None
            for hh in range(2):
                r = 2 * pr + hh
                dm = _head_decay(q, r)
                mmat = q["gmat"] * dm
                dym = jnp.where(_half_mask(hh), dyp, 0.0).astype(BF16)
                dmat = jnp.where(q["tril"], _dot_nt(dym, xp_b), 0.0)
                part = _dot_tn(mmat.astype(BF16), dym)
                acc = part if acc is None else acc + part
                dg_sum = dg_sum + dmat * dm
                e = dmat * mmat
                dcs_col = dcs_col + jnp.where(lane == r, jnp.sum(e, axis=1, keepdims=True), 0.0)
                dcs_row = dcs_row + jnp.where(rowi == r, jnp.sum(e, axis=0, keepdims=True), 0.0)
            dxdt_pairs.append(acc)
        dg_b = dg_sum.astype(BF16)
        dcm_ref[0] = dcm + _dot(dg_b, q["bm_b"])
        dbm_ref[0] = dbm + _dot_tn(dg_b, q["cm_b"])
        dxdt = q["w_ch"] * bds + jnp.concatenate(dxdt_pairs, axis=1)
        dxs_ref[0] = dy * d_chv + dxdt * q["dt_ch"]
        seg = (lax.broadcasted_iota(I32, (SSM_GROUP_W, LANES), 0) // SSM_HEAD_DIM
               == lax.broadcasted_iota(I32, (SSM_GROUP_W, LANES), 1)).astype(BF16)
        row_as_col = jnp.transpose(jnp.concatenate(
            [dcs_row, jnp.zeros((cl - SSM_ROWS, cl), F32)], axis=0))
        dcs = dcs_col - row_as_col + _dot2(cs_terms, seg)
        last = _dot2(jnp.zeros((8, SSM_GROUP_W), F32) + last_ch, seg)[0:1, :]
        da = _dot2_left(q["up_incl"], dcs) + last
        ddt = _dot2(dxdt * xs, seg) + da * q["a_l"]
        ddtr = ddt * _sigmoid(dtc_ref[0] + q["bias_l"])
        ddt_ref[0] = ddtr.astype(BF16)
        aln_ref[0, 0, 0:1, :] += jnp.sum(ddtr, axis=0, keepdims=True)
        aln_ref[0, 0, 1:2, :] += jnp.sum(da * q["dt_c"], axis=0, keepdims=True)

    outs = pl.pallas_call(
        body, name=name, grid=(bsz, g_n, nc),
        in_specs=[sp["z"], sp["dtc"], sp["xs"], sp["bm"], sp["cm"], sp["dtr"], sp["plane"],
                  sp["psub"], sp["chan"], sp["chan"], sp["state"], sp["z"]],
        out_specs=[sp["z"], sp["bgrp"], sp["bgrp"], sp["z"], sp["bgrp"], acc_ch, acc_ln],
        out_shape=[jax.ShapeDtypeStruct((bsz, s, d_inner), F32),
                   jax.ShapeDtypeStruct((bsz, s, g_n * SSM_STATE), F32),
                   jax.ShapeDtypeStruct((bsz, s, g_n * SSM_STATE), F32),
                   jax.ShapeDtypeStruct((bsz, s, d_inner), BF16),
                   jax.ShapeDtypeStruct((bsz, s, g_n * LANES), BF16),
                   jax.ShapeDtypeStruct((bsz, g_n, 8, SSM_GROUP_W), F32),
                   jax.ShapeDtypeStruct((bsz, g_n, 8, LANES), F32)],
        scratch_shapes=[pltpu.VMEM((SSM_STATE, SSM_GROUP_W), F32)],
        compiler_params=_params("parallel", "parallel", "arbitrary"),
    )(zx, zx, xbc, xbc, xbc, dtr_row, plane, psub, d_ch, nw, states, dgn)
    return outs


def _sb_tile(qm, kb, masked):
    z = _dot_nt(qm, kb) * SB_SCALE
    e = jnp.exp(-jnp.abs(z))
    lb = jnp.minimum(z, 0.0) - jnp.log(1.0 + e)
    lk = lb - z
    if masked:
        ri = lax.broadcasted_iota(I32, z.shape, 0)
        ci = lax.broadcasted_iota(I32, z.shape, 1)
        causal = ri > ci
        lk = jnp.where(causal, lk, 0.0)
    else:
        causal = None
    return z, e, lb, lk, causal


def _lane_half(hh, shape):
    lane = lax.broadcasted_iota(I32, shape, 1)
    return (lane < SB_HEAD_DIM) if hh == 0 else (lane >= SB_HEAD_DIM)


def _sb_fwd(q, kv, *, name):
    bsz, s, w = q.shape
    blk = SB_BLOCK
    npair = w // LANES
    nq = s // blk

    def body(q_ref, k_ref, v_ref, o_ref, tot_ref):
        qi = pl.program_id(2)
        qv = q_ref[0]
        ri = lax.broadcasted_iota(I32, (blk, blk), 0)
        ci = lax.broadcasted_iota(I32, (blk, blk), 1)
        lower = (ri > ci).astype(BF16)
        lane = lax.broadcasted_iota(I32, (blk, LANES), 1)
        acc = jnp.zeros((blk, LANES), F32)
        tot = jnp.zeros((blk, LANES), F32)
        for hh in range(2):
            half = _lane_half(hh, (blk, LANES))
            qm = jnp.where(half, qv, jnp.zeros_like(qv))

            def tile(j, r, acc, masked):
                start = pl.multiple_of(j * blk, blk)
                kb = k_ref[0, pl.ds(start, blk), :]
                vb = v_ref[0, pl.ds(start, blk), :]
                _, _, lb, lk, causal = _sb_tile(qm, kb, masked)
                logw = lb + _dot2(lk, lower) + r
                wgt = jnp.exp(logw)
                if masked:
                    wgt = jnp.where(causal, wgt, 0.0)
                vm = jnp.where(half, vb, jnp.zeros_like(vb))
                acc = acc + _dot(wgt.astype(BF16), vm)
                return r + jnp.sum(lk, axis=1, keepdims=True), acc

            r, acc = tile(qi, jnp.zeros((blk, 1), F32), acc, True)

            def step(t, carry):
                return tile(qi - 1 - t, carry[0], carry[1], False)

            r, acc = lax.fori_loop(0, qi, step, (r, acc))
            tot = tot + jnp.where(lane == hh, r, 0.0)
        o_ref[0] = acc.astype(BF16)
        tot_ref[0, 0] = tot

    return pl.pallas_call(
        body, name=name, grid=(bsz, npair, nq),
        in_specs=[pl.BlockSpec((1, blk, LANES), lambda b, p, i: (b, i, p)),
                  pl.BlockSpec((1, s, LANES), lambda b, p, i: (b, 0, p)),
                  pl.BlockSpec((1, s, LANES), lambda b, p, i: (b, 0, npair + p))],
        out_specs=[pl.BlockSpec((1, blk, LANES), lambda b, p, i: (b, i, p)),
                   pl.BlockSpec((1, 1, blk, LANES), lambda b, p, i: (b, p, i, 0))],
        out_shape=[jax.ShapeDtypeStruct((bsz, s, w), BF16),
                   jax.ShapeDtypeStruct((bsz, npair, s, LANES), F32)],
        compiler_params=_params("parallel", "parallel", "arbitrary"),
    )(q, kv, kv)


def _sb_bwd(q, kv, do, tot, dk_in, dv_in, *, name):
    bsz, s, w = q.shape
    blk = SB_BLOCK
    npair = w // LANES
    nq = s // blk
    has_init = dk_in is not None

    def body(*refs):
        if has_init:
            q_ref, k_ref, v_ref, do_ref, tot_ref, dki_ref, dvi_ref, dq_ref, dk_ref, dv_ref = refs
        else:
            q_ref, k_ref, v_ref, do_ref, tot_ref, dq_ref, dk_ref, dv_ref = refs
        qi = pl.program_id(2)

        @pl.when(qi == 0)
        def _():
            if has_init:
                dk_ref[...] = dki_ref[...]
                dv_ref[...] = dvi_ref[...]
            else:
                dk_ref[...] = jnp.zeros_like(dk_ref)
                dv_ref[...] = jnp.zeros_like(dv_ref)

        qv = q_ref[0]
        dov = do_ref[0]
        totv = tot_ref[0, 0]
        ri = lax.broadcasted_iota(I32, (blk, blk), 0)
        ci = lax.broadcasted_iota(I32, (blk, blk), 1)
        upper = (ri < ci).astype(BF16)
        dq_acc = jnp.zeros((blk, LANES), F32)
        for hh in range(2):
            half = _lane_half(hh, (blk, LANES))
            qm = jnp.where(half, qv, jnp.zeros_like(qv))
            dom = jnp.where(half, dov, jnp.zeros_like(dov))
            tot_h = totv[:, hh:hh + 1]

            def tile(j, pre_lk, pre_d, dq_acc, masked):
                start = pl.multiple_of(j * blk, blk)
                kb = k_ref[0, pl.ds(start, blk), :]
                vb = v_ref[0, pl.ds(start, blk), :]
                z, e, lb, lk, causal = _sb_tile(qm, kb, masked)
                rest = tot_h - (pre_lk + _dot2(lk, upper) + lk)
                wgt = jnp.exp(lb + rest)
                if masked:
                    wgt = jnp.where(causal, wgt, 0.0)
                dlogit = _dot_nt(dom, vb) * wgt
                pfx = pre_d + _dot2(dlogit, upper)
                inv = 1.0 / (1.0 + e)
                pos = z >= 0.0
                sig = jnp.where(pos, 1.0, e) * inv
                one_m = jnp.where(pos, e, 1.0) * inv
                dz = (dlogit * one_m - pfx * sig) * SB_SCALE
                if masked:
                    dz = jnp.where(causal, dz, 0.0)
                dz_b = dz.astype(BF16)
                km = jnp.where(half, kb, jnp.zeros_like(kb))
                dq_acc = dq_acc + _dot(dz_b, km)
                dk_ref[0, pl.ds(start, blk), :] += _dot_tn(dz_b, qm)
                dv_ref[0, pl.ds(start, blk), :] += _dot_tn(wgt.astype(BF16), dom)
                return (pre_lk + jnp.sum(lk, axis=1, keepdims=True),
                        pre_d + jnp.sum(dlogit, axis=1, keepdims=True), dq_acc)

            def step(j, carry):
                return tile(j, carry[0], carry[1], carry[2], False)

            zero = jnp.zeros((blk, 1), F32)
            pre_lk, pre_d, dq_acc = lax.fori_loop(0, qi, step, (zero, zero, dq_acc))
            _, _, dq_acc = tile(qi, pre_lk, pre_d, dq_acc, True)
        dq_ref[0] = dq_acc.astype(BF16)

    qspec = pl.BlockSpec((1, blk, LANES), lambda b, p, i: (b, i, p))
    kspec = pl.BlockSpec((1, s, LANES), lambda b, p, i: (b, 0, p))
    vspec = pl.BlockSpec((1, s, LANES), lambda b, p, i: (b, 0, npair + p))
    tspec = pl.BlockSpec((1, 1, blk, LANES), lambda b, p, i: (b, p, i, 0))
    in_specs = [qspec, kspec, vspec, qspec, tspec] + ([kspec, kspec] if has_init else [])
    args = (q, kv, kv, do, tot) + ((dk_in, dv_in) if has_init else ())
    return pl.pallas_call(
        body, name=name, grid=(bsz, npair, nq), in_specs=in_specs,
        out_specs=[qspec, kspec, kspec],
        out_shape=[jax.ShapeDtypeStruct((bsz, s, w), BF16), jax.ShapeDtypeStruct((bsz, s, w), F32),
                   jax.ShapeDtypeStruct((bsz, s, w), F32)],
        compiler_params=_params("parallel", "parallel", "arbitrary"),
    )(*args)


def _adamw(w, g, m, v, *, name, rows=1024):
    r = w.shape[0]
    tr = _tile(r, rows)

    def body(w_ref, g_ref, m_ref, v_ref, d_ref, mo_ref, vo_ref):
        gv = g_ref[...]
        mn = ADAM_B1 * m_ref[...] + (1.0 - ADAM_B1) * gv
        vn = ADAM_B2 * v_ref[...] + (1.0 - ADAM_B2) * (gv * gv)
        m_hat = mn / (1.0 - ADAM_B1 ** ADAM_STEP)
        v_hat = vn / (1.0 - ADAM_B2 ** ADAM_STEP)
        d_ref[...] = -ADAM_LR * (m_hat / (jnp.sqrt(v_hat) + ADAM_EPS) + ADAM_WD * w_ref[...])
        mo_ref[...] = mn
        vo_ref[...] = vn

    row = pl.BlockSpec((tr, LANES), lambda i: (i, 0))
    return pl.pallas_call(
        body, name=name, grid=(r // tr,), in_specs=[row] * 4, out_specs=[row] * 3,
        out_shape=[jax.ShapeDtypeStruct((r, LANES), F32)] * 3, compiler_params=_params("parallel"),
    )(w, g, m, v)


def _add_own_half(idx, g, recv, *, rows=1024):
    _, _, rh, _ = g.shape
    tr = _tile(rh, rows)

    def body(idx_ref, a_ref, b_ref, o_ref):
        o_ref[...] = a_ref[0] + b_ref[...]

    return pl.pallas_call(
        body, name="grad_pair_sum",
        grid_spec=pltpu.PrefetchScalarGridSpec(
            num_scalar_prefetch=1, grid=(N_CHIPS, rh // tr),
            in_specs=[pl.BlockSpec((1, 1, tr, LANES), lambda k, i, idx: (k, idx[0], i, 0)),
                      pl.BlockSpec((1, tr, LANES), lambda k, i, idx: (k, i, 0))],
            out_specs=pl.BlockSpec((1, tr, LANES), lambda k, i, idx: (k, i, 0))),
        out_shape=jax.ShapeDtypeStruct((N_CHIPS, rh, LANES), F32),
        compiler_params=_params("parallel", "parallel"),
    )(idx, g, recv)


def _add_chips(idx, own, recv, *, rows=1024):
    _, rh, _ = own.shape
    tr = _tile(rh, rows)

    def body(idx_ref, a_ref, b_ref, o_ref):
        o_ref[...] = ((a_ref[0] + b_ref[0]) + b_ref[1]) + b_ref[2]

    return pl.pallas_call(
        body, name="grad_chip_sum",
        grid_spec=pltpu.PrefetchScalarGridSpec(
            num_scalar_prefetch=1, grid=(rh // tr,),
            in_specs=[pl.BlockSpec((1, tr, LANES), lambda i, idx: (idx[0], i, 0)),
                      pl.BlockSpec((3, tr, LANES), lambda i, idx: (0, i, 0))],
            out_specs=pl.BlockSpec((tr, LANES), lambda i, idx: (i, 0))),
        out_shape=jax.ShapeDtypeStruct((rh, LANES), F32),
        compiler_params=_params("parallel"),
    )(idx, own, recv)


def _sum_devices(parts):
    _, r, _ = parts.shape

    def body(p_ref, o_ref):
        acc = p_ref[0]
        for k in range(1, N_DEV):
            acc = acc + p_ref[k]
        o_ref[...] = acc

    return pl.pallas_call(
        body, name="small_grad_sum", grid=(1,),
        in_specs=[pl.BlockSpec((N_DEV, r, LANES), lambda i: (0, 0, 0))],
        out_specs=pl.BlockSpec((r, LANES), lambda i: (0, 0)),
        out_shape=jax.ShapeDtypeStruct((r, LANES), F32),
    )(parts)


def _place():
    return lax.axis_index("x"), lax.axis_index("y"), lax.axis_index("c")


def _rcopy(src, dst, send_sems, recv_sems, k, to):
    return pltpu.make_async_remote_copy(src_ref=src, dst_ref=dst, send_sem=send_sems.at[k],
                                        recv_sem=recv_sems.at[k], device_id=to, device_id_type=MESH)


def _gather_weights(wl):
    _, rh, _ = wl.shape

    def body(w_ref, out_ref, send_sems, recv_sems, local_sem):
        x, y, c = _place()
        sibling = (x, y, 1 - c)
        chips = [(1 - x, y), (x, 1 - y), (1 - x, 1 - y)]

        def piece(px, py, pc):
            return out_ref.at[2 * px + py, pc]

        mine = pltpu.make_async_copy(w_ref, out_ref.at[2 * x + y], local_sem)
        mine.start()
        first = [_rcopy(w_ref.at[c], piece(x, y, c), send_sems, recv_sems, j, (*chip, c))
                 for j, chip in enumerate(chips)]
        for cp in first:
            cp.start()
        passed = [_rcopy(piece(*chip, c), piece(*chip, c), send_sems, recv_sems, 3 + j, sibling)
                  for j, chip in enumerate(chips)]
        for j, chip in enumerate(chips):
            _rcopy(piece(*chip, c), piece(*chip, c), send_sems, recv_sems, j, (*chip, c)).wait_recv()
            passed[j].start()
        for j, chip in enumerate(chips):
            _rcopy(piece(*chip, 1 - c), piece(*chip, 1 - c), send_sems, recv_sems, 3 + j,
                   sibling).wait_recv()
        for cp in first + passed:
            cp.wait_send()
        mine.wait()

    return pl.pallas_call(
        body, name="gather_weights", in_specs=[ANY], out_specs=ANY,
        out_shape=jax.ShapeDtypeStruct((N_CHIPS, 2, rh, LANES), wl.dtype),
        scratch_shapes=[pltpu.SemaphoreType.DMA((6,)), pltpu.SemaphoreType.DMA((6,)),
                        pltpu.SemaphoreType.DMA],
    )(wl)


def _swap_halves(g):
    _, _, rh, _ = g.shape

    def body(g_ref, out_ref, send_sems, recv_sems):
        x, y, c = _place()
        sibling = (x, y, 1 - c)
        cps = [_rcopy(g_ref.at[k, 1 - c], out_ref.at[k], send_sems, recv_sems, k, sibling)
               for k in range(N_CHIPS)]
        for cp in cps:
            cp.start()
        for cp in cps:
            cp.wait()

    return pl.pallas_call(
        body, name="grad_swap_halves", in_specs=[ANY], out_specs=ANY,
        out_shape=jax.ShapeDtypeStruct((N_CHIPS, rh, LANES), g.dtype),
        scratch_shapes=[pltpu.SemaphoreType.DMA((N_CHIPS,)), pltpu.SemaphoreType.DMA((N_CHIPS,))],
    )(g)


def _scatter_chunks(s4):
    _, rh, _ = s4.shape

    def body(s_ref, out_ref, send_sems, recv_sems):
        x, y, c = _place()
        chips = [(1 - x, y), (x, 1 - y), (1 - x, 1 - y)]
        cps = [_rcopy(s_ref.at[2 * chip[0] + chip[1]], out_ref.at[j], send_sems, recv_sems, j,
                      (*chip, c)) for j, chip in enumerate(chips)]
        for cp in cps:
            cp.start()
        for cp in cps:
            cp.wait()

    return pl.pallas_call(
        body, name="grad_scatter_chunks", in_specs=[ANY], out_specs=ANY,
        out_shape=jax.ShapeDtypeStruct((3, rh, LANES), s4.dtype),
        scratch_shapes=[pltpu.SemaphoreType.DMA((3,)), pltpu.SemaphoreType.DMA((3,))],
    )(s4)


def _share_half(tot):
    rh, _ = tot.shape

    def body(t_ref, out_ref, send_sems, recv_sems, local_sem):
        x, y, c = _place()
        mine = pltpu.make_async_copy(t_ref, out_ref.at[c], local_sem)
        mine.start()
        cp = _rcopy(t_ref, out_ref.at[c], send_sems, recv_sems, 0, (x, y, 1 - c))
        cp.start()
        _rcopy(t_ref, out_ref.at[1 - c], send_sems, recv_sems, 0, (x, y, 1 - c)).wait_recv()
        cp.wait_send()
        mine.wait()

    return pl.pallas_call(
        body, name="grad_share_half", in_specs=[ANY], out_specs=ANY,
        out_shape=jax.ShapeDtypeStruct((2, rh, LANES), tot.dtype),
        scratch_shapes=[pltpu.SemaphoreType.DMA((1,)), pltpu.SemaphoreType.DMA((1,)),
                        pltpu.SemaphoreType.DMA],
    )(tot)


def _exchange_small(r):
    rr, _ = r.shape

    def body(r_ref, out_ref, send_sems, recv_sems, local_sem):
        x, y, c = _place()
        me = 4 * x + 2 * y + c
        mine = pltpu.make_async_copy(r_ref, out_ref.at[me], local_sem)
        mine.start()
        cps = []
        for k in range(N_DEV - 1):
            fx, fy, fc = ((k + 1) >> 2) & 1, ((k + 1) >> 1) & 1, (k + 1) & 1
            to = (x ^ fx, y ^ fy, c ^ fc)
            cps.append((_rcopy(r_ref, out_ref.at[me], send_sems, recv_sems, k, to), to))
        for cp, _ in cps:
            cp.start()
        for k, (cp, to) in enumerate(cps):
            src = 4 * to[0] + 2 * to[1] + to[2]
            _rcopy(r_ref, out_ref.at[src], send_sems, recv_sems, k, to).wait_recv()
        for cp, _ in cps:
            cp.wait_send()
        mine.wait()

    return pl.pallas_call(
        body, name="small_grad_exchange", in_specs=[ANY], out_specs=ANY,
        out_shape=jax.ShapeDtypeStruct((N_DEV, rr, LANES), r.dtype),
        scratch_shapes=[pltpu.SemaphoreType.DMA((N_DEV - 1,)), pltpu.SemaphoreType.DMA((N_DEV - 1,)),
                        pltpu.SemaphoreType.DMA],
    )(r)


def _pack_flat(arrs, dtype):
    flat = jnp.concatenate([a.reshape(-1).astype(dtype) for a in arrs])
    n = flat.shape[0]
    unit = PACK_ROWS * LANES
    pad = (-n) % unit
    if pad:
        flat = jnp.concatenate([flat, jnp.zeros((pad,), dtype)])
    return flat


def _unpack_flat(flat, shapes):
    out, off = [], 0
    for shp in shapes:
        n = 1
        for d in shp:
            n *= d
        out.append(flat[off:off + n].reshape(shp))
        off += n
    return out


def _ffn_fwd(h, bsz, s, gain, w_up, cw, cb, w_down, i):
    hf = _rmsnorm_fwd(h, gain, name=f"ffn_norm_{i}")
    up = _mm(hf, w_up, name=f"ffn_up_{i}")
    up3 = up.reshape(bsz, s, -1)
    f = _conv_glu_fwd(up3, cw, cb, name=f"ffn_glu_{i}").reshape(h.shape[0], -1)
    h2 = _mm(f, w_down, add=h, name=f"ffn_down_{i}")
    return h2, (h, hf, up3, f)


def _ffn_bwd(dh, saved, gain, w_up, cw, cb, w_down, i):
    h, hf, up3, f = saved
    t = h.shape[0]
    d_down = _mm(f, dh, ta=True, name=f"ffn_down_dw_{i}")
    df = _mm(dh, w_down, tb=True, name=f"ffn_down_dx_{i}")
    dug, duv, dcw, dcb = _conv_glu_bwd(up3, cw, cb, df.reshape(up3.shape[0], up3.shape[1], -1),
                                       name=f"ffn_glu_bwd_{i}")
    dup = jnp.concatenate([dug, duv], axis=-1).reshape(t, -1)
    d_up = _mm(hf, dup, ta=True, name=f"ffn_up_dw_{i}")
    dhf = _mm(dup, w_up, tb=True, name=f"ffn_up_dx_{i}")
    dh, dgain = _rmsnorm_bwd(h, gain, dhf, dh, name=f"ffn_norm_bwd_{i}")
    return dh, dgain, d_up, dcw, dcb, d_down


def _ple_layer_fwd(h, p_i, gain, w_gate, w_proj, i):
    hp = _rmsnorm_fwd(h, gain, name=f"ple_norm_{i}")
    a = _mm(hp, w_gate, name=f"ple_gate_{i}")
    pp = _mm(p_i, w_proj, name=f"ple_proj_{i}")
    return _ple_fwd(h, a, pp, name=f"ple_mix_{i}"), (h, hp, a, pp)


def _ple_layer_bwd(dh, saved, p_i, gain, w_gate, i):
    h, hp, a, pp = saved
    da, dpp = _ple_bwd(dh, a, pp, name=f"ple_mix_bwd_{i}")
    d_gate = _mm(hp, da, ta=True, name=f"ple_gate_dw_{i}")
    d_proj = _mm(p_i, dpp, ta=True, name=f"ple_proj_dw_{i}")
    dhp = _mm(da, w_gate, tb=True, name=f"ple_gate_dx_{i}")
    dh, dgain = _rmsnorm_bwd(h, gain, dhp, dh, name=f"ple_norm_bwd_{i}")
    return dh, dgain, d_gate, d_proj


def _ssm_consts(dt_bias, a_log, d_skip, g_n):
    hpg = SSM_HEADS_PER_GROUP
    a = -jnp.exp(a_log)
    rows = jnp.stack([dt_bias.reshape(g_n, hpg), a.reshape(g_n, hpg)], axis=1)
    plane = jnp.zeros((g_n, 8, LANES), F32).at[:, 0:2, 0:hpg].set(rows)
    psub = jnp.zeros((g_n, SSM_ROWS, LANES), F32).at[:, 0:hpg, 0:2].set(jnp.swapaxes(rows, 1, 2))
    d_ch = jnp.repeat(d_skip, SSM_HEAD_DIM).reshape(1, -1)
    return a, plane, psub, d_ch


def _ssm_in_big(w_in, d_inner, g_n):
    d = w_in.shape[0]
    cut = w_in.shape[1] - g_n * SSM_HEADS_PER_GROUP
    wdt = w_in[:, cut:].reshape(d, g_n, SSM_HEADS_PER_GROUP)
    wdt = jnp.pad(wdt, ((0, 0), (0, 0), (0, LANES - SSM_HEADS_PER_GROUP))).reshape(d, g_n * LANES)
    return jnp.concatenate([w_in[:, :cut], wdt], axis=1)


def _ssm_in_small(dw_big, g_n):
    d = dw_big.shape[0]
    cut = dw_big.shape[1] - g_n * LANES
    ddt = dw_big[:, cut:].reshape(d, g_n, LANES)[:, :, :SSM_HEADS_PER_GROUP].reshape(d, -1)
    return jnp.concatenate([dw_big[:, :cut], ddt], axis=1)


def _ssm_fwd(h, bsz, s, gain, w_in_big, cw, cb, plane, psub, d_ch, nw, w_out, i):
    d_inner = d_ch.shape[1]
    g_n = d_inner // SSM_GROUP_W
    conv_dim = cw.shape[1]
    hn = _rmsnorm_fwd(h, gain, name=f"attn_norm_{i}")
    zx = _mm(hn, w_in_big, name=f"ssm_in_{i}").reshape(bsz, s, -1)
    xbc = _conv_silu_fwd(zx, d_inner, cw, cb, name=f"ssm_conv_{i}")
    dtr = zx[:, :, d_inner + conv_dim:].reshape(bsz, s, g_n, LANES)[..., :SSM_HEADS_PER_GROUP]
    dtr_row = jnp.pad(jnp.transpose(dtr, (0, 2, 3, 1)),
                      ((0, 0), (0, 0), (0, SSM_ROWS - SSM_HEADS_PER_GROUP), (0, 0)))
    gn, states = _ssd_fwd(zx, xbc, dtr_row, plane, psub, d_ch, nw, name=f"ssd_{i}")
    gn2 = gn.reshape(h.shape[0], -1)
    h1 = _mm(gn2, w_out, add=h, name=f"ssm_out_{i}")
    return h1, (h, hn, zx, xbc, dtr_row, states, gn2)


def _ssm_bwd(dh, saved, gain, w_in_big, cw, cb, plane, psub, d_ch, nw, w_out, i):
    h, hn, zx, xbc, dtr_row, states, gn2 = saved
    t = h.shape[0]
    bsz, s, _ = zx.shape
    d_inner = d_ch.shape[1]
    d_out = _mm(gn2, dh, ta=True, name=f"ssm_out_dw_{i}")
    dgn = _mm(dh, w_out, tb=True, name=f"ssm_out_dx_{i}").reshape(bsz, s, -1)
    dxs, dbm, dcm, dz, ddtr, ach, aln = _ssd_bwd(zx, xbc, dtr_row, plane, psub, d_ch, nw, states, dgn,
                                                  name=f"ssd_bwd_{i}")
    dxbc_c = jnp.concatenate([dxs, dbm, dcm], axis=-1)
    dxbc, dcw, dcb = _conv_silu_bwd(zx, d_inner, cw, cb, dxbc_c, name=f"ssm_conv_bwd_{i}")
    dcat = jnp.concatenate([dz, dxbc, ddtr], axis=-1).reshape(t, -1)
    d_in_big = _mm(hn, dcat, ta=True, name=f"ssm_in_dw_{i}")
    dhn = _mm(dcat, w_in_big, tb=True, name=f"ssm_in_dx_{i}")
    dh, dgain = _rmsnorm_bwd(h, gain, dhn, dh, name=f"attn_norm_bwd_{i}")
    hpg = SSM_HEADS_PER_GROUP
    ach = jnp.sum(ach, axis=0)
    aln = jnp.sum(aln, axis=0)
    d_nw = ach[:, 0, :].reshape(-1)
    d_dskip = jnp.sum(ach[:, 1, :].reshape(-1, SSM_HEAD_DIM), axis=1)
    d_bias = aln[:, 0, :hpg].reshape(-1)
    d_a = aln[:, 1, :hpg].reshape(-1)
    return dh, dgain, d_in_big, dcw, dcb, d_bias, d_a, d_dskip, d_nw, d_out


def _sb_layer_fwd(h, bsz, s, gain, w_q, w_o, kv3, i):
    hn = _rmsnorm_fwd(h, gain, name=f"attn_norm_{i}")
    q3 = _mm(hn, w_q, out_dtype=BF16, name=f"sb_q_{i}").reshape(bsz, s, -1)
    o3, tot = _sb_fwd(q3, kv3, name=f"sb_attn_{i}")
    o2 = o3.reshape(h.shape[0], -1)
    h1 = _mm(o2, w_o, add=h, name=f"sb_o_{i}")
    return h1, (h, hn, q3, o2, tot)


def _sb_layer_bwd(dh, saved, gain, w_q, w_o, kv3, dk, dv, i):
    h, hn, q3, o2, tot = saved
    t = h.shape[0]
    d_o = _mm(o2, dh, ta=True, name=f"sb_o_dw_{i}")
    do3 = _mm(dh, w_o, tb=True, out_dtype=BF16, name=f"sb_o_dx_{i}").reshape(q3.shape)
    dq3, dk, dv = _sb_bwd(q3, kv3, do3, tot, dk, dv, name=f"sb_attn_bwd_{i}")
    dq = dq3.reshape(t, -1)
    d_q = _mm(hn, dq, ta=True, name=f"sb_q_dw_{i}")
    dhn = _mm(dq, w_q, tb=True, name=f"sb_q_dx_{i}")
    dh, dgain = _rmsnorm_bwd(h, gain, dhn, dh, name=f"attn_norm_bwd_{i}")
    return dh, dgain, d_q, d_o, dk, dv


def kernel(x, p, attn_norm, ffn_norm, ple_norm, ssm_in_proj, ssm_conv_w, ssm_conv_b, ssm_dt_bias, ssm_a_log, ssm_d, ssm_norm, ssm_out_proj, kv_norm, w_kv, w_q, w_o, ffn_up, ffn_conv_w, ffn_conv_b, ffn_down, ple_gate, ple_proj, final_norm, loss_target, m_attn_norm, m_ffn_norm, m_ple_norm, m_ssm_in_proj, m_ssm_conv_w, m_ssm_conv_b, m_ssm_dt_bias, m_ssm_a_log, m_ssm_d, m_ssm_norm, m_ssm_out_proj, m_kv_norm, m_w_kv, m_w_q, m_w_o, m_ffn_up, m_ffn_conv_w, m_ffn_conv_b, m_ffn_down, m_ple_gate, m_ple_proj, m_final_norm, v_attn_norm, v_ffn_norm, v_ple_norm, v_ssm_in_proj, v_ssm_conv_w, v_ssm_conv_b, v_ssm_dt_bias, v_ssm_a_log, v_ssm_d, v_ssm_norm, v_ssm_out_proj, v_kv_norm, v_w_kv, v_w_q, v_w_o, v_ffn_up, v_ffn_conv_w, v_ffn_conv_b, v_ffn_down, v_ple_gate, v_ple_proj, v_final_norm):
    given = dict(locals())
    wl = {n: given[n] for n in WEIGHTS}
    bsz, s, d = x.shape
    t = bsz * s
    depth = attn_norm.shape[0]
    n_a = ssm_in_proj.shape[0]
    d_inner = ssm_norm.shape[1] * N_CHIPS
    g_n = d_inner // SSM_GROUP_W
    cidx = lax.axis_index("c").astype(I32).reshape(1)
    chip_idx = (2 * lax.axis_index("x") + lax.axis_index("y")).astype(I32).reshape(1)

    local_shapes = [wl[n].shape for n in SHARDED]
    packed_w = _pack_flat([wl[n] for n in SHARDED], BF16)
    rows = packed_w.shape[0] // LANES
    rh = rows // 2
    gathered = _gather_weights(packed_w.reshape(2, rh, LANES)).reshape(N_CHIPS, rows * LANES)
    per_chip = [_unpack_flat(gathered[j], local_shapes) for j in range(N_CHIPS)]
    full = {}
    for k, n in enumerate(SHARDED):
        full[n] = jnp.concatenate([per_chip[j][k] for j in range(N_CHIPS)], axis=SHARD_AXIS[n])

    h = x.reshape(t, d)
    tgt = loss_target.reshape(t, d)
    saved = []
    kv3 = hkv = h_kv_in = None
    consts = []
    for i in range(depth):
        if i < n_a:
            a_neg, plane, psub, d_ch = _ssm_consts(ssm_dt_bias[i], ssm_a_log[i], ssm_d[i], g_n)
            w_in_big = _ssm_in_big(full["ssm_in_proj"][i], d_inner, g_n)
            cw = full["ssm_conv_w"][i].astype(F32)
            cb = full["ssm_conv_b"][i].astype(F32)
            nw = full["ssm_norm"][i].astype(F32).reshape(1, -1)
            consts.append((a_neg, plane, psub, d_ch, w_in_big, cw, cb, nw))
            h, sv_mix = _ssm_fwd(h, bsz, s, attn_norm[i], w_in_big, cw, cb, plane, psub, d_ch, nw,
                                 full["ssm_out_proj"][i], i)
        else:
            j = i - n_a
            h, sv_mix = _sb_layer_fwd(h, bsz, s, attn_norm[i], full["w_q"][j], full["w_o"][j], kv3, i)
        fcw = full["ffn_conv_w"][i].astype(F32)
        h, sv_ffn = _ffn_fwd(h, bsz, s, ffn_norm[i], full["ffn_up"][i], fcw, ffn_conv_b[i],
                             full["ffn_down"][i], i)
        p_i = p[i].reshape(t, -1)
        h, sv_ple = _ple_layer_fwd(h, p_i, ple_norm[i], full["ple_gate"][i], full["ple_proj"][i], i)
        saved.append((sv_mix, sv_ffn, sv_ple))
        if i == n_a - 1:
            h_kv_in = h
            hkv = _rmsnorm_fwd(h, kv_norm, name="kv_norm")
            kv3 = _mm(hkv, full["w_kv"], out_dtype=BF16, name="kv_proj").reshape(bsz, s, -1)

    loss_local, dh, g_final = _final_loss(h, final_norm, tgt)
    gr = {n: [None] * wl[n].shape[0] for n in WEIGHTS if n not in ("kv_norm", "w_kv", "final_norm")}
    gr["final_norm"] = g_final
    dk = dv = None
    for i in reversed(range(depth)):
        sv_mix, sv_ffn, sv_ple = saved[i]
        if i == n_a - 1:
            dkv = jnp.concatenate([dk, dv], axis=-1).reshape(t, -1)
            gr["w_kv"] = _mm(hkv, dkv, ta=True, name="kv_proj_dw")
            dhkv = _mm(dkv, full["w_kv"], tb=True, name="kv_proj_dx")
            dh, gr["kv_norm"] = _rmsnorm_bwd(h_kv_in, kv_norm, dhkv, dh, name="kv_norm_bwd")
        p_i = p[i].reshape(t, -1)
        dh, gr["ple_norm"][i], gr["ple_gate"][i], gr["ple_proj"][i] = _ple_layer_bwd(
            dh, sv_ple, p_i, ple_norm[i], full["ple_gate"][i], i)
        fcw = full["ffn_conv_w"][i].astype(F32)
        (dh, gr["ffn_norm"][i], gr["ffn_up"][i], gr["ffn_conv_w"][i], gr["ffn_conv_b"][i],
         gr["ffn_down"][i]) = _ffn_bwd(dh, sv_ffn, ffn_norm[i], full["ffn_up"][i], fcw, ffn_conv_b[i],
                                       full["ffn_down"][i], i)
        if i < n_a:
            a_neg, plane, psub, d_ch, w_in_big, cw, cb, nw = consts[i]
            (dh, gr["attn_norm"][i], d_in_big, gr["ssm_conv_w"][i], gr["ssm_conv_b"][i],
             gr["ssm_dt_bias"][i], d_a, gr["ssm_d"][i], gr["ssm_norm"][i],
             gr["ssm_out_proj"][i]) = _ssm_bwd(dh, sv_mix, attn_norm[i], w_in_big, cw, cb, plane, psub,
                                               d_ch, nw, full["ssm_out_proj"][i], i)
            gr["ssm_in_proj"][i] = _ssm_in_small(d_in_big, g_n)
            gr["ssm_a_log"][i] = d_a * a_neg
        else:
            j = i - n_a
            dh, gr["attn_norm"][i], gr["w_q"][j], gr["w_o"][j], dk, dv = _sb_layer_bwd(
                dh, sv_mix, attn_norm[i], full["w_q"][j], full["w_o"][j], kv3, dk, dv, i)
    grad_x = dh.reshape(bsz, s, d)
    gfull = {n: (jnp.stack(v) if isinstance(v, list) else v) for n, v in gr.items()}

    chunks = []
    for j in range(N_CHIPS):
        parts = [jnp.split(gfull[n], N_CHIPS, axis=SHARD_AXIS[n])[j] for n in SHARDED]
        chunks.append(_pack_flat(parts, F32))
    g4 = jnp.stack(chunks).reshape(N_CHIPS, 2, rh, LANES)
    from_sibling = _swap_halves(g4)
    chip_sums = _add_own_half(cidx, g4, from_sibling)
    from_chips = _scatter_chunks(chip_sums)
    my_half = _add_chips(chip_idx, chip_sums, from_chips)
    g_shard = _share_half(my_half).reshape(rows, LANES)

    rep_shapes = [wl[n].shape for n in REPLICATED]
    packed_r = _pack_flat([gfull[n] for n in REPLICATED], F32)
    rrows = packed_r.shape[0] // LANES
    g_rep = _sum_devices(_exchange_small(packed_r.reshape(rrows, LANES)))

    def adam(names, g_slab, shapes, tag):
        ws = _pack_flat([wl[n] for n in names], F32).reshape(-1, LANES)
        ms = _pack_flat([given["m_" + n] for n in names], F32).reshape(-1, LANES)
        vs = _pack_flat([given["v_" + n] for n in names], F32).reshape(-1, LANES)
        dl, mn, vn = _adamw(ws, g_slab, ms, vs, name="adamw_" + tag)
        un = lambda a: dict(zip(names, _unpack_flat(a.reshape(-1), shapes)))
        return un(g_slab), un(dl), un(mn), un(vn)

    sh = adam(SHARDED, g_shard, local_shapes, "sharded")
    rp = adam(REPLICATED, g_rep, rep_shapes, "replicated")
    pick = lambda k, n: (sh[k][n] if n in SHARD_AXIS else rp[k][n])
    loss = lax.psum(loss_local, ("x", "y", "c"))
    return (loss, grad_x, *[pick(0, n) for n in WEIGHTS], *[pick(1, n) for n in WEIGHTS],
            *[pick(2, n) for n in WEIGHTS], *[pick(3, n) for n in WEIGHTS])
```

```python
import functools

import jax
import jax.numpy as jnp
from jax import lax
from jax.experimental import pallas as pl
from jax.experimental.pallas import tpu as pltpu

F32 = jnp.float32
BF16 = jnp.bfloat16
I32 = jnp.int32

NORM_EPS = 1e-6
SSM_NORM_EPS = 1e-5
SSM_HEAD_DIM = 64
SSM_STATE = 128
SSM_CHUNK = 128
SSM_HEADS_PER_GROUP = 8
SSM_GROUP_W = SSM_HEADS_PER_GROUP * SSM_HEAD_DIM
SSM_CONV = 4
SSM_ROWS = 16
SB_HEAD_DIM = 64
SB_BLOCK = 128
SB_SCALE = SB_HEAD_DIM ** -0.5
LOG2E = 1.4426950408889634
FFN_CONV = 3
LANES = 128
N_CHIPS = 4
N_DEV = 8

ADAM_LR = 0.001
ADAM_B1 = 0.9
ADAM_B2 = 0.999
ADAM_EPS = 1e-08
ADAM_WD = 0.01
ADAM_STEP = 10

MESH = pl.DeviceIdType.MESH
ANY = pl.BlockSpec(memory_space=pl.ANY)

SHARD_AXIS = {
    "ssm_in_proj": 2, "ssm_conv_w": 2, "ssm_conv_b": 1, "ssm_norm": 1, "ssm_out_proj": 1,
    "w_kv": 1, "w_q": 1, "w_o": 1, "ffn_up": 2, "ffn_conv_w": 2, "ffn_down": 1,
    "ple_gate": 1, "ple_proj": 2,
}
REPLICATED = ["attn_norm", "ffn_norm", "ple_norm", "ssm_dt_bias", "ssm_a_log", "ssm_d",
              "kv_norm", "ffn_conv_b", "final_norm"]
WEIGHTS = ["attn_norm", "ffn_norm", "ple_norm", "ssm_in_proj", "ssm_conv_w", "ssm_conv_b",
           "ssm_dt_bias", "ssm_a_log", "ssm_d", "ssm_norm", "ssm_out_proj", "kv_norm", "w_kv",
           "w_q", "w_o", "ffn_up", "ffn_conv_w", "ffn_conv_b", "ffn_down", "ple_gate",
           "ple_proj", "final_norm"]
SHARDED = [n for n in WEIGHTS if n in SHARD_AXIS]
PACK_ROWS = 2048


def _tile(n, pref):
    t = (min(pref, n) // 128) * 128
    while t >= 128:
        if n % t == 0:
            return t
        t -= 128
    return n


def _dot(a, b):
    return jnp.dot(a, b, preferred_element_type=F32)


def _dot_nt(a, b):
    return lax.dot_general(a, b, (((1,), (1,)), ((), ())), preferred_element_type=F32)


def _dot_tn(a, b):
    return lax.dot_general(a, b, (((0,), (0,)), ((), ())), preferred_element_type=F32)


def _split2(x):
    hi = x.astype(BF16)
    lo = (x - hi.astype(F32)).astype(BF16)
    return hi, lo


def _dot2(x, m):
    hi, lo = _split2(x)
    return _dot(hi, m) + _dot(lo, m)


def _dot2_left(m, x):
    hi, lo = _split2(x)
    return _dot(m, hi) + _dot(m, lo)


def _softplus(x):
    return jnp.maximum(x, 0.0) + jnp.log(1.0 + jnp.exp(-jnp.abs(x)))


def _sigmoid(x):
    return jax.nn.sigmoid(x)


def _params(*sem):
    return pltpu.CompilerParams(dimension_semantics=sem)


MM_VMEM_BUDGET = 36 * 1024 * 1024
MM_FULL_K = 2816


def _mm_tiles(m, n, k, sa, sb, so, has_add):
    tk = k if k <= MM_FULL_K else _tile(k, 1024)
    tn = _tile(n, 1408)
    tm = _tile(m, 1408)

    def need(tm_):
        return (2 * tm_ * tk * sa + 2 * tk * tn * sb + tm_ * tn * 4 + 2 * tm_ * tn * so
                + (2 * tm_ * tn * 4 if has_add else 0))

    while need(tm) > MM_VMEM_BUDGET and tm % 256 == 0:
        tm //= 2
    return tm, tn, tk


def _mm(a, b, *, name, ta=False, tb=False, add=None, out_dtype=F32):
    m = a.shape[1] if ta else a.shape[0]
    k = a.shape[0] if ta else a.shape[1]
    n = b.shape[0] if tb else b.shape[1]
    assert (b.shape[1] if tb else b.shape[0]) == k, (a.shape, b.shape, ta, tb)
    tm, tn, tk = _mm_tiles(m, n, k, a.dtype.itemsize, b.dtype.itemsize,
                           jnp.dtype(out_dtype).itemsize, add is not None)
    nk = k // tk
    dims = (((0 if ta else 1,), (1 if tb else 0,)), ((), ()))
    has_add = add is not None

    def body(*refs):
        if has_add:
            a_ref, b_ref, add_ref, o_ref, acc_ref = refs
        else:
            a_ref, b_ref, o_ref, acc_ref = refs
        kk = pl.program_id(2)

        @pl.when(kk == 0)
        def _():
            acc_ref[...] = jnp.zeros_like(acc_ref)

        acc_ref[...] += lax.dot_general(a_ref[...].astype(BF16), b_ref[...].astype(BF16), dims,
                                        preferred_element_type=F32)

        @pl.when(kk == nk - 1)
        def _():
            r = acc_ref[...]
            if has_add:
                r = r + add_ref[...].astype(F32)
            o_ref[...] = r.astype(out_dtype)

    a_spec = (pl.BlockSpec((tk, tm), lambda i, j, kk: (kk, i)) if ta
              else pl.BlockSpec((tm, tk), lambda i, j, kk: (i, kk)))
    b_spec = (pl.BlockSpec((tn, tk), lambda i, j, kk: (j, kk)) if tb
              else pl.BlockSpec((tk, tn), lambda i, j, kk: (kk, j)))
    o_spec = pl.BlockSpec((tm, tn), lambda i, j, kk: (i, j))
    in_specs = [a_spec, b_spec] + ([o_spec] if has_add else [])
    args = (a, b) + ((add,) if has_add else ())
    return pl.pallas_call(
        body, name=name, grid=(m // tm, n // tn, nk), in_specs=in_specs, out_specs=o_spec,
        out_shape=jax.ShapeDtypeStruct((m, n), out_dtype),
        scratch_shapes=[pltpu.VMEM((tm, tn), F32)],
        compiler_params=_params("parallel", "parallel", "arbitrary"),
    )(*args)


def _rmsnorm_fwd(x, gain, *, name, rows=512):
    t, d = x.shape
    tr = _tile(t, rows)

    def body(x_ref, g_ref, o_ref):
        xv = x_ref[...]
        r = lax.rsqrt(jnp.mean(xv * xv, axis=-1, keepdims=True) + NORM_EPS)
        o_ref[...] = ((xv * r) * g_ref[...]).astype(BF16)

    return pl.pallas_call(
        body, name=name, grid=(t // tr,),
        in_specs=[pl.BlockSpec((tr, d), lambda i: (i, 0)), pl.BlockSpec((1, d), lambda i: (0, 0))],
        out_specs=pl.BlockSpec((tr, d), lambda i: (i, 0)),
        out_shape=jax.ShapeDtypeStruct((t, d), BF16),
        compiler_params=_params("parallel"),
    )(x, gain.reshape(1, d))


def _rmsnorm_bwd(x, gain, dy, dres, *, name, rows=512):
    t, d = x.shape
    tr = _tile(t, rows)

    def body(x_ref, g_ref, dy_ref, dres_ref, dx_ref, dg_ref):
        xv = x_ref[...]
        r = lax.rsqrt(jnp.mean(xv * xv, axis=-1, keepdims=True) + NORM_EPS)
        xh = xv * r
        dyv = dy_ref[...].astype(F32)
        dxh = dyv * g_ref[...]
        dx = r * (dxh - xh * jnp.mean(dxh * xh, axis=-1, keepdims=True))
        dx_ref[...] = dres_ref[...] + dx
        part = jnp.sum(dyv * xh, axis=0, keepdims=True)

        @pl.when(pl.program_id(0) == 0)
        def _():
            dg_ref[...] = part

        @pl.when(pl.program_id(0) > 0)
        def _():
            dg_ref[...] += part

    row = pl.BlockSpec((tr, d), lambda i: (i, 0))
    vec = pl.BlockSpec((1, d), lambda i: (0, 0))
    dx, dg = pl.pallas_call(
        body, name=name, grid=(t // tr,), in_specs=[row, vec, row, row], out_specs=[row, vec],
        out_shape=[jax.ShapeDtypeStruct((t, d), F32), jax.ShapeDtypeStruct((1, d), F32)],
        compiler_params=_params("arbitrary"),
    )(x, gain.reshape(1, d), dy, dres)
    return dx, dg.reshape(d)


def _final_loss(h, gain, target, *, rows=512):
    t, d = h.shape
    tr = _tile(t, rows)

    def body(x_ref, g_ref, tg_ref, dx_ref, dg_ref, loss_ref):
        xv = x_ref[...]
        g = g_ref[...]
        r = lax.rsqrt(jnp.mean(xv * xv, axis=-1, keepdims=True) + NORM_EPS)
        xh = xv * r
        err = xh * g - tg_ref[...]
        dyv = err * (1.0 / d)
        dxh = dyv * g
        dx_ref[...] = r * (dxh - xh * jnp.mean(dxh * xh, axis=-1, keepdims=True))
        part = jnp.sum(dyv * xh, axis=0, keepdims=True)
        lpart = jnp.zeros((1, LANES), F32) + (0.5 / d) * jnp.sum(err * err)

        @pl.when(pl.program_id(0) == 0)
        def _():
            dg_ref[...] = part
            loss_ref[...] = lpart

        @pl.when(pl.program_id(0) > 0)
        def _():
            dg_ref[...] += part
            loss_ref[...] += lpart

    row = pl.BlockSpec((tr, d), lambda i: (i, 0))
    vec = pl.BlockSpec((1, d), lambda i: (0, 0))
    dx, dg, loss = pl.pallas_call(
        body, name="final_loss", grid=(t // tr,), in_specs=[row, vec, row],
        out_specs=[row, vec, pl.BlockSpec((1, LANES), lambda i: (0, 0))],
        out_shape=[jax.ShapeDtypeStruct((t, d), F32), jax.ShapeDtypeStruct((1, d), F32),
                   jax.ShapeDtypeStruct((1, LANES), F32)],
        compiler_params=_params("arbitrary"),
    )(h, gain.reshape(1, d), target)
    return loss[0, 0], dx, dg.reshape(d)


def _ple_fwd(h, a, pp, *, name, rows=512):
    t, d = h.shape
    tr = _tile(t, rows)

    def body(h_ref, a_ref, p_ref, o_ref):
        o_ref[...] = h_ref[...] + _sigmoid(a_ref[...]) * p_ref[...]

    row = pl.BlockSpec((tr, d), lambda i: (i, 0))
    return pl.pallas_call(
        body, name=name, grid=(t // tr,), in_specs=[row, row, row], out_specs=row,
        out_shape=jax.ShapeDtypeStruct((t, d), F32), compiler_params=_params("parallel"),
    )(h, a, pp)


def _ple_bwd(dh, a, pp, *, name, rows=512):
    t, d = dh.shape
    tr = _tile(t, rows)

    def body(dh_ref, a_ref, p_ref, da_ref, dp_ref):
        s = _sigmoid(a_ref[...])
        dhv = dh_ref[...]
        da_ref[...] = (dhv * p_ref[...] * (s * (1.0 - s))).astype(BF16)
        dp_ref[...] = (dhv * s).astype(BF16)

    row = pl.BlockSpec((tr, d), lambda i: (i, 0))
    return pl.pallas_call(
        body, name=name, grid=(t // tr,), in_specs=[row, row, row], out_specs=[row, row],
        out_shape=[jax.ShapeDtypeStruct((t, d), BF16)] * 2, compiler_params=_params("parallel"),
    )(dh, a, pp)


def _shift_down(u, j):
    if j == 0:
        return u
    rows = lax.broadcasted_iota(I32, u.shape, 0)
    return jnp.where(rows >= j, pltpu.roll(u, j, 0), 0.0)


def _shift_up(u, j):
    if j == 0:
        return u
    s = u.shape[0]
    rows = lax.broadcasted_iota(I32, u.shape, 0)
    return jnp.where(rows < s - j, pltpu.roll(u, s - j, 0), 0.0)


def _conv_pre(u, wv, bv):
    kw = wv.shape[0]
    shifted = [_shift_down(u, kw - 1 - k) for k in range(kw)]
    pre = bv + wv[0:1, :] * shifted[0]
    for k in range(1, kw):
        pre = pre + wv[k:k + 1, :] * shifted[k]
    return pre, shifted


def _conv_back(dpre, wv, shifted, dw_ref, db_ref, first):
    kw = wv.shape[0]
    du = wv[kw - 1:kw, :] * dpre
    for k in range(kw - 1):
        du = du + wv[k:k + 1, :] * _shift_up(dpre, kw - 1 - k)
    dws = [jnp.sum(dpre * shifted[k], axis=0, keepdims=True) for k in range(kw)]
    dbs = jnp.sum(dpre, axis=0, keepdims=True)

    @pl.when(first)
    def _():
        for k in range(kw):
            dw_ref[k:k + 1, :] = dws[k]
        db_ref[...] = dbs

    @pl.when(jnp.logical_not(first))
    def _():
        for k in range(kw):
            dw_ref[k:k + 1, :] += dws[k]
        db_ref[...] += dbs

    return du


def _dsilu(pre):
    s = _sigmoid(pre)
    return s, s * (1.0 + pre * (1.0 - s))


def _conv_silu_fwd(zx, off, w, b, *, name, tc=128):
    bsz, s, _ = zx.shape
    kw, c = w.shape
    o0 = off // tc

    def body(u_ref, w_ref, b_ref, o_ref):
        pre, _ = _conv_pre(u_ref[0], w_ref[...], b_ref[...])
        o_ref[0] = pre * _sigmoid(pre)

    return pl.pallas_call(
        body, name=name, grid=(bsz, c // tc),
        in_specs=[pl.BlockSpec((1, s, tc), lambda i, j: (i, 0, o0 + j)),
                  pl.BlockSpec((kw, tc), lambda i, j: (0, j)),
                  pl.BlockSpec((1, tc), lambda i, j: (0, j))],
        out_specs=pl.BlockSpec((1, s, tc), lambda i, j: (i, 0, j)),
        out_shape=jax.ShapeDtypeStruct((bsz, s, c), F32),
        compiler_params=_params("parallel", "parallel"),
    )(zx, w, b.reshape(1, c))


def _conv_silu_bwd(zx, off, w, b, douts, *, name, tc=128):
    bsz, s, _ = zx.shape
    kw, c = w.shape
    o0 = off // tc
    counts = [d.shape[2] // tc for d in douts]
    starts = [sum(counts[:k]) for k in range(len(douts))]
    assert sum(counts) == c // tc

    def body(u_ref, w_ref, b_ref, *rest):
        dy_refs = rest[:len(douts)]
        du_ref, dw_ref, db_ref = rest[len(douts):]
        j = pl.program_id(0)
        dy = dy_refs[0][0]
        for k in range(1, len(douts)):
            dy = jnp.where(j >= starts[k], dy_refs[k][0], dy)
        wv = w_ref[...]
        pre, shifted = _conv_pre(u_ref[0], wv, b_ref[...])
        _, ds = _dsilu(pre)
        du = _conv_back(dy * ds, wv, shifted, dw_ref, db_ref, pl.program_id(1) == 0)
        du_ref[0] = du.astype(BF16)

    def part_spec(k):
        return pl.BlockSpec((1, s, tc), lambda j, i: (i, 0, jnp.clip(j - starts[k], 0, counts[k] - 1)))

    du, dw, db = pl.pallas_call(
        body, name=name, grid=(c // tc, bsz),
        in_specs=[pl.BlockSpec((1, s, tc), lambda j, i: (i, 0, o0 + j)),
                  pl.BlockSpec((kw, tc), lambda j, i: (0, j)),
                  pl.BlockSpec((1, tc), lambda j, i: (0, j))] + [part_spec(k) for k in range(len(douts))],
        out_specs=[pl.BlockSpec((1, s, tc), lambda j, i: (i, 0, j)),
                   pl.BlockSpec((kw, tc), lambda j, i: (0, j)),
                   pl.BlockSpec((1, tc), lambda j, i: (0, j))],
        out_shape=[jax.ShapeDtypeStruct((bsz, s, c), BF16), jax.ShapeDtypeStruct((kw, c), F32),
                   jax.ShapeDtypeStruct((1, c), F32)],
        compiler_params=_params("parallel", "arbitrary"),
    )(zx, w, b.reshape(1, c), *douts)
    return du, dw, db.reshape(c)


def _conv_glu_fwd(up, w, b, *, name, tc=128):
    bsz, s, c2 = up.shape
    kw = w.shape[0]
    f = c2 // 2
    nt = f // tc

    def body(ug_ref, uv_ref, wg_ref, wv_ref, bg_ref, bv_ref, o_ref):
        pg, _ = _conv_pre(ug_ref[0], wg_ref[...], bg_ref[...])
        pv, _ = _conv_pre(uv_ref[0], wv_ref[...], bv_ref[...])
        o_ref[0] = (pg * _sigmoid(pg) * pv).astype(BF16)

    b2 = b.reshape(1, c2)
    return pl.pallas_call(
        body, name=name, grid=(bsz, nt),
        in_specs=[pl.BlockSpec((1, s, tc), lambda i, j: (i, 0, j)),
                  pl.BlockSpec((1, s, tc), lambda i, j: (i, 0, nt + j)),
                  pl.BlockSpec((kw, tc), lambda i, j: (0, j)),
                  pl.BlockSpec((kw, tc), lambda i, j: (0, nt + j)),
                  pl.BlockSpec((1, tc), lambda i, j: (0, j)),
                  pl.BlockSpec((1, tc), lambda i, j: (0, nt + j))],
        out_specs=pl.BlockSpec((1, s, tc), lambda i, j: (i, 0, j)),
        out_shape=jax.ShapeDtypeStruct((bsz, s, f), BF16),
        compiler_params=_params("parallel", "parallel"),
    )(up, up, w, w, b2, b2)


def _conv_glu_bwd(up, w, b, df, *, name, tc=128):
    bsz, s, c2 = up.shape
    kw = w.shape[0]
    f = c2 // 2
    nt = f // tc

    def body(ug_ref, uv_ref, wg_ref, wv_ref, bg_ref, bv_ref, df_ref,
             dug_ref, duv_ref, dwg_ref, dwv_ref, dbg_ref, dbv_ref):
        first = pl.program_id(1) == 0
        wg = wg_ref[...]
        wv = wv_ref[...]
        pg, sg = _conv_pre(ug_ref[0], wg, bg_ref[...])
        pv, sv = _conv_pre(uv_ref[0], wv, bv_ref[...])
        sig, dsl = _dsilu(pg)
        dfv = df_ref[0]
        dug_ref[0] = _conv_back(dfv * pv * dsl, wg, sg, dwg_ref, dbg_ref, first).astype(BF16)
        duv_ref[0] = _conv_back(dfv * (pg * sig), wv, sv, dwv_ref, dbv_ref, first).astype(BF16)

    b2 = b.reshape(1, c2)
    act = lambda j, i: (i, 0, j)
    wsp = pl.BlockSpec((kw, tc), lambda j, i: (0, j))
    bsp = pl.BlockSpec((1, tc), lambda j, i: (0, j))
    dug, duv, dwg, dwv, dbg, dbv = pl.pallas_call(
        body, name=name, grid=(nt, bsz),
        in_specs=[pl.BlockSpec((1, s, tc), act),
                  pl.BlockSpec((1, s, tc), lambda j, i: (i, 0, nt + j)),
                  wsp, pl.BlockSpec((kw, tc), lambda j, i: (0, nt + j)),
                  bsp, pl.BlockSpec((1, tc), lambda j, i: (0, nt + j)),
                  pl.BlockSpec((1, s, tc), act)],
        out_specs=[pl.BlockSpec((1, s, tc), act), pl.BlockSpec((1, s, tc), act), wsp, wsp, bsp, bsp],
        out_shape=[jax.ShapeDtypeStruct((bsz, s, f), BF16)] * 2
        + [jax.ShapeDtypeStruct((kw, f), F32)] * 2 + [jax.ShapeDtypeStruct((1, f), F32)] * 2,
        compiler_params=_params("parallel", "arbitrary"),
    )(up, up, w, w, b2, b2, df)
    return (dug, duv, jnp.concatenate([dwg, dwv], axis=1),
            jnp.concatenate([dbg.reshape(f), dbv.reshape(f)]))


def _ssd_shared(xs, bm, cm, dtc_raw, dtr_raw, plane, psub, st):
    cl = SSM_CHUNK
    bias_l, a_l = plane[0:1, :], plane[1:2, :]
    bias_s, a_s = psub[:, 0:1], psub[:, 1:2]
    ri = lax.broadcasted_iota(I32, (cl, cl), 0)
    ci = lax.broadcasted_iota(I32, (cl, cl), 1)
    tril = ri >= ci
    low_incl = tril.astype(BF16)
    up_incl = (ri <= ci).astype(BF16)
    seg_t = (lax.broadcasted_iota(I32, (LANES, SSM_GROUP_W), 0)
             == lax.broadcasted_iota(I32, (LANES, SSM_GROUP_W), 1) // SSM_HEAD_DIM).astype(BF16)
    dt_c = _softplus(dtc_raw + bias_l)
    cs_c = _dot2_left(low_incl, dt_c * a_l)
    dt_r = _softplus(dtr_raw + bias_s)
    cs_r = _dot2(dt_r * a_s, up_incl)
    dt_ch = _dot2(dt_c, seg_t)
    cs_ch = _dot2(cs_c, seg_t)
    cs_last = cs_ch[cl - 1:cl, :]
    decay_ch = jnp.exp(cs_ch)
    w_ch = jnp.exp(cs_last - cs_ch)
    tot_ch = jnp.exp(cs_last)
    xdt = xs * dt_ch
    bm_b, cm_b = bm.astype(BF16), cm.astype(BF16)
    gmat = _dot_nt(cm_b, bm_b)
    cst = _dot(cm_b, st.astype(BF16))
    yoff = decay_ch * cst
    return dict(tril=tril, low_incl=low_incl, up_incl=up_incl, seg_t=seg_t, a_l=a_l, bias_l=bias_l,
                dt_c=dt_c, cs_c=cs_c, cs_r=cs_r, dt_ch=dt_ch, decay_ch=decay_ch, w_ch=w_ch,
                tot_ch=tot_ch, xdt=xdt, bm_b=bm_b, cm_b=cm_b, gmat=gmat, yoff=yoff)


def _head_decay(q, r):
    diff = q["cs_c"][:, r:r + 1] - q["cs_r"][r:r + 1, :]
    return jnp.where(q["tril"], jnp.exp(jnp.minimum(diff, 0.0)), 0.0)


def _half_mask(hh):
    lane = lax.broadcasted_iota(I32, (SSM_CHUNK, LANES), 1)
    return (lane < SSM_HEAD_DIM) if hh == 0 else (lane >= SSM_HEAD_DIM)


def _ssd_ydiag(q):
    pairs = []
    for pr in range(SSM_HEADS_PER_GROUP // 2):
        xp = q["xdt"][:, pr * LANES:(pr + 1) * LANES]
        acc = None
        for hh in range(2):
            mm_ = (q["gmat"] * _head_decay(q, 2 * pr + hh)).astype(BF16)
            part = _dot(mm_, jnp.where(_half_mask(hh), xp, 0.0).astype(BF16))
            acc = part if acc is None else acc + part
        pairs.append(acc)
    return jnp.concatenate(pairs, axis=1)


def _ssd_specs(bsz, s, g_n, d_inner, rev):
    cl = SSM_CHUNK
    nc = s // cl
    cc = (lambda c: nc - 1 - c) if rev else (lambda c: c)
    gb = d_inner // LANES
    dt0 = (d_inner + d_inner + 2 * g_n * SSM_STATE) // LANES
    return dict(
        z=pl.BlockSpec((1, cl, SSM_GROUP_W), lambda b, g, c: (b, cc(c), g)),
        dtc=pl.BlockSpec((1, cl, LANES), lambda b, g, c: (b, cc(c), dt0 + g)),
        xs=pl.BlockSpec((1, cl, SSM_GROUP_W), lambda b, g, c: (b, cc(c), g)),
        bm=pl.BlockSpec((1, cl, LANES), lambda b, g, c: (b, cc(c), gb + g)),
        cm=pl.BlockSpec((1, cl, LANES), lambda b, g, c: (b, cc(c), gb + g_n + g)),
        dtr=pl.BlockSpec((1, 1, SSM_ROWS, cl), lambda b, g, c: (b, g, 0, cc(c))),
        plane=pl.BlockSpec((1, 8, LANES), lambda b, g, c: (g, 0, 0)),
        psub=pl.BlockSpec((1, SSM_ROWS, LANES), lambda b, g, c: (g, 0, 0)),
        chan=pl.BlockSpec((1, SSM_GROUP_W), lambda b, g, c: (0, g)),
        state=pl.BlockSpec((1, 1, 1, SSM_STATE, SSM_GROUP_W), lambda b, g, c: (b, g, cc(c), 0, 0)),
        bgrp=pl.BlockSpec((1, cl, LANES), lambda b, g, c: (b, cc(c), g)),
    )


def _ssd_fwd(zx, xbc, dtr_row, plane, psub, d_ch, nw, *, name):
    bsz, s, _ = zx.shape
    d_inner = d_ch.shape[1]
    g_n = d_inner // SSM_GROUP_W
    nc = s // SSM_CHUNK
    sp = _ssd_specs(bsz, s, g_n, d_inner, False)

    def body(z_ref, dtc_ref, xs_ref, bm_ref, cm_ref, dtr_ref, plane_ref, psub_ref, d_ref, nw_ref,
             gn_ref, st_out_ref, st_ref):
        @pl.when(pl.program_id(2) == 0)
        def _():
            st_ref[...] = jnp.zeros_like(st_ref)

        xs = xs_ref[0]
        st = st_ref[...]
        st_out_ref[0, 0, 0] = st
        q = _ssd_shared(xs, bm_ref[0], cm_ref[0], dtc_ref[0], dtr_ref[0, 0], plane_ref[0],
                        psub_ref[0], st)
        y = _ssd_ydiag(q) + q["yoff"] + xs * d_ref[...]
        st_ref[...] = q["tot_ch"] * st + _dot_tn(q["bm_b"], (q["w_ch"] * q["xdt"]).astype(BF16))
        zv = z_ref[0]
        gy = y * (zv * _sigmoid(zv))
        rstd = lax.rsqrt(jnp.mean(gy * gy, axis=-1, keepdims=True) + SSM_NORM_EPS)
        gn_ref[0] = ((gy * rstd) * nw_ref[...]).astype(BF16)

    return pl.pallas_call(
        body, name=name, grid=(bsz, g_n, nc),
        in_specs=[sp["z"], sp["dtc"], sp["xs"], sp["bm"], sp["cm"], sp["dtr"], sp["plane"],
                  sp["psub"], sp["chan"], sp["chan"]],
        out_specs=[sp["z"], sp["state"]],
        out_shape=[jax.ShapeDtypeStruct((bsz, s, d_inner), BF16),
                   jax.ShapeDtypeStruct((bsz, g_n, nc, SSM_STATE, SSM_GROUP_W), F32)],
        scratch_shapes=[pltpu.VMEM((SSM_STATE, SSM_GROUP_W), F32)],
        compiler_params=_params("parallel", "parallel", "arbitrary"),
    )(zx, zx, xbc, xbc, xbc, dtr_row, plane, psub, d_ch, nw)


def _ssd_bwd(zx, xbc, dtr_row, plane, psub, d_ch, nw, states, dgn, *, name):
    bsz, s, _ = zx.shape
    d_inner = d_ch.shape[1]
    g_n = d_inner // SSM_GROUP_W
    cl = SSM_CHUNK
    nc = s // cl
    sp = _ssd_specs(bsz, s, g_n, d_inner, True)
    acc_ch = pl.BlockSpec((1, 1, 8, SSM_GROUP_W), lambda b, g, c: (b, g, 0, 0))
    acc_ln = pl.BlockSpec((1, 1, 8, LANES), lambda b, g, c: (b, g, 0, 0))

    def body(z_ref, dtc_ref, xs_ref, bm_ref, cm_ref, dtr_ref, plane_ref, psub_ref, d_ref, nw_ref,
             st_in_ref, dgn_ref,
             dxs_ref, dbm_ref, dcm_ref, dz_ref, ddt_ref, ach_ref, aln_ref, dst_ref):
        first = pl.program_id(2) == 0

        @pl.when(first)
        def _():
            dst_ref[...] = jnp.zeros_like(dst_ref)
            ach_ref[...] = jnp.zeros_like(ach_ref)
            aln_ref[...] = jnp.zeros_like(aln_ref)

        xs = xs_ref[0]
        st = st_in_ref[0, 0, 0]
        q = _ssd_shared(xs, bm_ref[0], cm_ref[0], dtc_ref[0], dtr_ref[0, 0], plane_ref[0],
                        psub_ref[0], st)
        d_chv = d_ref[...]
        nwv = nw_ref[...]
        y = _ssd_ydiag(q) + q["yoff"] + xs * d_chv
        zv = z_ref[0]
        sz = _sigmoid(zv)
        silu_z = zv * sz
        gy = y * silu_z
        rstd = lax.rsqrt(jnp.mean(gy * gy, axis=-1, keepdims=True) + SSM_NORM_EPS)
        gyh = gy * rstd
        dgnv = dgn_ref[0]
        dgyh = dgnv * nwv
        dgy = rstd * (dgyh - gyh * jnp.mean(dgyh * gyh, axis=-1, keepdims=True))
        dy = dgy * silu_z
        dz_ref[0] = (dgy * y * (sz * (1.0 + zv * (1.0 - sz)))).astype(BF16)
        ach_ref[0, 0, 0:1, :] += jnp.sum(dgnv * gyh, axis=0, keepdims=True)
        ach_ref[0, 0, 1:2, :] += jnp.sum(dy * xs, axis=0, keepdims=True)
        st_b = st.astype(BF16)
        dyd = (dy * q["decay_ch"]).astype(BF16)
        dcm = _dot_nt(dyd, st_b)
        dstn = dst_ref[...]
        dstn_b = dstn.astype(BF16)
        bds = _dot(q["bm_b"], dstn_b)
        wx = q["w_ch"] * q["xdt"]
        dbm = _dot_nt(wx.astype(BF16), dstn_b)
        dst_ref[...] = q["tot_ch"] * dstn + _dot_tn(q["cm_b"], dyd)
        vterm = wx * bds
        cs_terms = dy * q["yoff"] - vterm
        last_ch = q["tot_ch"] * jnp.sum(dstn * st, axis=0, keepdims=True) + jnp.sum(vterm, axis=0, keepdims=True)
        lane = lax.broadcasted_iota(I32, (cl, LANES), 1)
        rowi = lax.broadcasted_iota(I32, (SSM_ROWS, cl), 0)
        dg_sum = jnp.zeros((cl, cl), F32)
        dcs_col = jnp.zeros((cl, LANES), F32)
        dcs_row = jnp.zeros((SSM_ROWS, cl), F32)
        dxdt_pairs = []
        for pr in range(SSM_HEADS_PER_GROUP // 2):
            xp_b = q["xdt"][:, pr * LANES:(pr + 1) * LANES].astype(BF16)
            dyp = dy[:, pr * LANES:(pr + 1) * LANES]
            acc = None
            for hh in range(2):
                r = 2 * pr + hh
                dm = _head_decay(q, r)
                mmat = q["gmat"] * dm
                dym = jnp.where(_half_mask(hh), dyp, 0.0).astype(BF16)
                dmat = jnp.where(q["tril"], _dot_nt(dym, xp_b), 0.0)
                part = _dot_tn(mmat.astype(BF16), dym)
                acc = part if acc is None else acc + part
                dg_sum = dg_sum + dmat * dm
                e = dmat * mmat
                dcs_col = dcs_col + jnp.where(lane == r, jnp.sum(e, axis=1, keepdims=True), 0.0)
                dcs_row = dcs_row + jnp.where(rowi == r, jnp.sum(e, axis=0, keepdims=True), 0.0)
            dxdt_pairs.append(acc)
        dg_b = dg_sum.astype(BF16)
        dcm_ref[0] = dcm + _dot(dg_b, q["bm_b"])
        dbm_ref[0] = dbm + _dot_tn(dg_b, q["cm_b"])
        dxdt = q["w_ch"] * bds + jnp.concatenate(dxdt_pairs, axis=1)
        dxs_ref[0] = dy * d_chv + dxdt * q["dt_ch"]
        seg = (lax.broadcasted_iota(I32, (SSM_GROUP_W, LANES), 0) // SSM_HEAD_DIM
               == lax.broadcasted_iota(I32, (SSM_GROUP_W, LANES), 1)).astype(BF16)
        row_as_col = jnp.transpose(jnp.concatenate(
            [dcs_row, jnp.zeros((cl - SSM_ROWS, cl), F32)], axis=0))
        dcs = dcs_col - row_as_col + _dot2(cs_terms, seg)
        last = _dot2(jnp.zeros((8, SSM_GROUP_W), F32) + last_ch, seg)[0:1, :]
        da = _dot2_left(q["up_incl"], dcs) + last
        ddt = _dot2(dxdt * xs, seg) + da * q["a_l"]
        ddtr = ddt * _sigmoid(dtc_ref[0] + q["bias_l"])
        ddt_ref[0] = ddtr.astype(BF16)
        aln_ref[0, 0, 0:1, :] += jnp.sum(ddtr, axis=0, keepdims=True)
        aln_ref[0, 0, 1:2, :] += jnp.sum(da * q["dt_c"], axis=0, keepdims=True)

    outs = pl.pallas_call(
        body, name=name, grid=(bsz, g_n, nc),
        in_specs=[sp["z"], sp["dtc"], sp["xs"], sp["bm"], sp["cm"], sp["dtr"], sp["plane"],
                  sp["psub"], sp["chan"], sp["chan"], sp["state"], sp["z"]],
        out_specs=[sp["z"], sp["bgrp"], sp["bgrp"], sp["z"], sp["bgrp"], acc_ch, acc_ln],
        out_shape=[jax.ShapeDtypeStruct((bsz, s, d_inner), F32),
                   jax.ShapeDtypeStruct((bsz, s, g_n * SSM_STATE), F32),
                   jax.ShapeDtypeStruct((bsz, s, g_n * SSM_STATE), F32),
                   jax.ShapeDtypeStruct((bsz, s, d_inner), BF16),
                   jax.ShapeDtypeStruct((bsz, s, g_n * LANES), BF16),
                   jax.ShapeDtypeStruct((bsz, g_n, 8, SSM_GROUP_W), F32),
                   jax.ShapeDtypeStruct((bsz, g_n, 8, LANES), F32)],
        scratch_shapes=[pltpu.VMEM((SSM_STATE, SSM_GROUP_W), F32)],
        compiler_params=_params("parallel", "parallel", "arbitrary"),
    )(zx, zx, xbc, xbc, xbc, dtr_row, plane, psub, d_ch, nw, states, dgn)
    return outs


def _sb_stack(x):
    lane = lax.broadcasted_iota(I32, x.shape, 1)
    zero = jnp.zeros_like(x)
    return jnp.concatenate([jnp.where(lane < SB_HEAD_DIM, x, zero),
                            jnp.where(lane >= SB_HEAD_DIM, x, zero)], axis=0)


def _sb_unstack_t(acc_t):
    row = lax.broadcasted_iota(I32, (LANES, SB_BLOCK), 0)
    return jnp.transpose(jnp.where(row < SB_HEAD_DIM, acc_t[:, :SB_BLOCK], acc_t[:, SB_BLOCK:]))


def _sb_tile_blocks(nq):
    return 4 if nq % 4 == 0 else (2 if nq % 2 == 0 else 1)


def _sb_valid(u, qi, nb):
    shape = (nb * SB_BLOCK, 2 * SB_BLOCK)
    key = u * (nb * SB_BLOCK) + lax.broadcasted_iota(I32, shape, 0)
    qpos = qi * SB_BLOCK + lax.broadcasted_iota(I32, shape, 1) % SB_BLOCK
    return key < qpos


def _sb_logits(kb, qs, valid):
    z2 = _dot_nt(kb, qs) * LOG2E
    lb = jnp.minimum(z2, 0.0) - jnp.log2(1.0 + jnp.exp2(-jnp.abs(z2)))
    lk_all = lb - z2
    lk = lk_all if valid is None else jnp.where(valid, lk_all, 0.0)
    return lb, lk_all, lk


def _sb_scan(tri2, x, nb, reverse, exact=True):
    blk = SB_BLOCK
    edge = 0 if reverse else blk - 1
    carry = jnp.zeros((1, x.shape[1]), F32)
    res = [None] * nb
    for i in (reversed(range(nb)) if reverse else range(nb)):
        part = x[i * blk:(i + 1) * blk]
        if exact:
            hi, lo = _split2(part)
            raw = _dot(tri2, jnp.concatenate([hi, lo], axis=0))
        else:
            raw = _dot(tri2[:, :blk], part.astype(BF16))
        res[i] = raw + carry
        carry = carry + (raw[edge:edge + 1] + part[edge:edge + 1])
    return jnp.concatenate(res, axis=0), carry


def _sb_fwd(q, kv, kvt, *, name):
    bsz, s, w = q.shape
    blk = SB_BLOCK
    npair = w // LANES
    nq = s // blk
    nb = _sb_tile_blocks(nq)

    def body(q_ref, k_ref, vt_ref, o_ref, tot_ref):
        qi = pl.program_id(2)
        qs = _sb_stack(q_ref[0] * SB_SCALE)
        ri = lax.broadcasted_iota(I32, (blk, blk), 0)
        ci = lax.broadcasted_iota(I32, (blk, blk), 1)
        upper = (ri < ci).astype(BF16)
        tri2 = jnp.concatenate([upper, upper], axis=1)

        def tile(u, r, acc, masked):
            rows = pl.ds(pl.multiple_of(u * (nb * blk), nb * blk), nb * blk)
            valid = _sb_valid(u, qi, nb) if masked else None
            lb, _, lk = _sb_logits(k_ref[0, rows, :], qs, valid)
            sfx, total = _sb_scan(tri2, lk, nb, True)
            wgt = jnp.exp2(lb + sfx + r)
            if masked:
                wgt = jnp.where(valid, wgt, 0.0)
            wb = wgt.astype(BF16)
            for i in range(nb):
                acc = acc + _dot(vt_ref[0, 0, u * nb + i], wb[i * blk:(i + 1) * blk])
            return r + total, acc

        top = qi // nb
        r, acc = tile(top, jnp.zeros((1, 2 * blk), F32), jnp.zeros((LANES, 2 * blk), F32), True)
        r, acc = lax.fori_loop(0, top, lambda t, c: tile(top - 1 - t, c[0], c[1], False), (r, acc))
        o_ref[0] = _sb_unstack_t(acc).astype(BF16)
        tot_ref[0, 0, 0] = r

    return pl.pallas_call(
        body, name=name, grid=(bsz, npair, nq),
        in_specs=[pl.BlockSpec((1, blk, LANES), lambda b, p, i: (b, i, p)),
                  pl.BlockSpec((1, s, LANES), lambda b, p, i: (b, 0, p)),
                  pl.BlockSpec((1, 1, nq, LANES, blk), lambda b, p, i: (b, npair + p, 0, 0, 0))],
        out_specs=[pl.BlockSpec((1, blk, LANES), lambda b, p, i: (b, i, p)),
                   pl.BlockSpec((1, 1, 1, 1, 2 * blk), lambda b, p, i: (b, p, i, 0, 0))],
        out_shape=[jax.ShapeDtypeStruct((bsz, s, w), BF16),
                   jax.ShapeDtypeStruct((bsz, npair, nq, 1, 2 * blk), F32)],
        compiler_params=_params("parallel", "parallel", "arbitrary"),
    )(q, kv, kvt)


def _kv_blocks_t(kv3):
    bsz, s, w2 = kv3.shape
    x = kv3.reshape(bsz, s // SB_BLOCK, SB_BLOCK, w2 // LANES, LANES)
    return jnp.transpose(x, (0, 3, 1, 4, 2))


def _sb_bwd(q, kv, kvt, do, tot, dk_in, dv_in, *, name):
    bsz, s, w = q.shape
    blk = SB_BLOCK
    npair = w // LANES
    nq = s // blk
    nb = _sb_tile_blocks(nq)
    has_init = dk_in is not None

    def body(*refs):
        if has_init:
            q_ref, k_ref, v_ref, kt_ref, do_ref, tot_ref, dki_ref, dvi_ref, dq_ref, dk_ref, dv_ref = refs
        else:
            q_ref, k_ref, v_ref, kt_ref, do_ref, tot_ref, dq_ref, dk_ref, dv_ref = refs
        qi = pl.program_id(2)

        @pl.when(qi == 0)
        def _():
            if has_init:
                dk_ref[...] = dki_ref[...]
                dv_ref[...] = dvi_ref[...]
            else:
                dk_ref[...] = jnp.zeros_like(dk_ref)
                dv_ref[...] = jnp.zeros_like(dv_ref)

        qs = _sb_stack(q_ref[0] * SB_SCALE)
        dos = _sb_stack(do_ref[0])
        totv = tot_ref[0, 0, 0]
        ri = lax.broadcasted_iota(I32, (blk, blk), 0)
        ci = lax.broadcasted_iota(I32, (blk, blk), 1)
        lower = (ri > ci).astype(BF16)
        tri2 = jnp.concatenate([lower, lower], axis=1)

        def tile(u, pre_lk, pre_d, dqt, masked):
            rows = pl.ds(pl.multiple_of(u * (nb * blk), nb * blk), nb * blk)
            valid = _sb_valid(u, qi, nb) if masked else None
            lb, lk_all, lk = _sb_logits(k_ref[0, rows, :], qs, valid)
            before, tot_lk = _sb_scan(tri2, lk, nb, False)
            wgt = jnp.exp2(lb + (totv - (pre_lk + before + lk)))
            if masked:
                wgt = jnp.where(valid, wgt, 0.0)
            dlogit = _dot_nt(v_ref[0, rows, :], dos) * wgt
            dbefore, tot_d = _sb_scan(tri2, dlogit, nb, False)
            dz = dlogit * jnp.exp2(lk_all) - (pre_d + dbefore) * jnp.exp2(lb)
            if masked:
                dz = jnp.where(valid, dz, 0.0)
            dz_b = dz.astype(BF16)
            for i in range(nb):
                dqt = dqt + _dot(kt_ref[0, 0, u * nb + i], dz_b[i * blk:(i + 1) * blk])
            dk_ref[0, rows, :] += _dot(dz_b, qs)
            dv_ref[0, rows, :] += _dot(wgt.astype(BF16), dos)
            return pre_lk + tot_lk, pre_d + tot_d, dqt

        zero = jnp.zeros((1, 2 * blk), F32)
        top = qi // nb
        c = lax.fori_loop(0, top, lambda u, c: tile(u, c[0], c[1], c[2], False),
                          (zero, zero, jnp.zeros((LANES, 2 * blk), F32)))
        _, _, dqt = tile(top, c[0], c[1], c[2], True)
        dq_ref[0] = (_sb_unstack_t(dqt) * SB_SCALE).astype(BF16)

    qspec = pl.BlockSpec((1, blk, LANES), lambda b, p, i: (b, i, p))
    kspec = pl.BlockSpec((1, s, LANES), lambda b, p, i: (b, 0, p))
    vspec = pl.BlockSpec((1, s, LANES), lambda b, p, i: (b, 0, npair + p))
    ktspec = pl.BlockSpec((1, 1, nq, LANES, blk), lambda b, p, i: (b, p, 0, 0, 0))
    tspec = pl.BlockSpec((1, 1, 1, 1, 2 * blk), lambda b, p, i: (b, p, i, 0, 0))
    in_specs = [qspec, kspec, vspec, ktspec, qspec, tspec] + ([kspec, kspec] if has_init else [])
    args = (q, kv, kv, kvt, do, tot) + ((dk_in, dv_in) if has_init else ())
    return pl.pallas_call(
        body, name=name, grid=(bsz, npair, nq), in_specs=in_specs,
        out_specs=[qspec, kspec, kspec],
        out_shape=[jax.ShapeDtypeStruct((bsz, s, w), BF16), jax.ShapeDtypeStruct((bsz, s, w), F32),
                   jax.ShapeDtypeStruct((bsz, s, w), F32)],
        compiler_params=_params("parallel", "parallel", "arbitrary"),
    )(*args)


def _adamw(w, g, m, v, *, name, rows=1024):
    r = w.shape[0]
    tr = _tile(r, rows)

    def body(w_ref, g_ref, m_ref, v_ref, d_ref, mo_ref, vo_ref):
        gv = g_ref[...]
        mn = ADAM_B1 * m_ref[...] + (1.0 - ADAM_B1) * gv
        vn = ADAM_B2 * v_ref[...] + (1.0 - ADAM_B2) * (gv * gv)
        m_hat = mn / (1.0 - ADAM_B1 ** ADAM_STEP)
        v_hat = vn / (1.0 - ADAM_B2 ** ADAM_STEP)
        d_ref[...] = -ADAM_LR * (m_hat / (jnp.sqrt(v_hat) + ADAM_EPS) + ADAM_WD * w_ref[...])
        mo_ref[...] = mn
        vo_ref[...] = vn

    row = pl.BlockSpec((tr, LANES), lambda i: (i, 0))
    return pl.pallas_call(
        body, name=name, grid=(r // tr,), in_specs=[row] * 4, out_specs=[row] * 3,
        out_shape=[jax.ShapeDtypeStruct((r, LANES), F32)] * 3, compiler_params=_params("parallel"),
    )(w, g, m, v)


def _add_own_half(idx, g, recv, *, rows=1024):
    _, _, rh, _ = g.shape
    tr = _tile(rh, rows)

    def body(idx_ref, a_ref, b_ref, o_ref):
        o_ref[...] = a_ref[0] + b_ref[...]

    return pl.pallas_call(
        body, name="grad_pair_sum",
        grid_spec=pltpu.PrefetchScalarGridSpec(
            num_scalar_prefetch=1, grid=(N_CHIPS, rh // tr),
            in_specs=[pl.BlockSpec((1, 1, tr, LANES), lambda k, i, idx: (k, idx[0], i, 0)),
                      pl.BlockSpec((1, tr, LANES), lambda k, i, idx: (k, i, 0))],
            out_specs=pl.BlockSpec((1, tr, LANES), lambda k, i, idx: (k, i, 0))),
        out_shape=jax.ShapeDtypeStruct((N_CHIPS, rh, LANES), F32),
        compiler_params=_params("parallel", "parallel"),
    )(idx, g, recv)


def _add_chips(idx, own, recv, *, rows=1024):
    _, rh, _ = own.shape
    tr = _tile(rh, rows)

    def body(idx_ref, a_ref, b_ref, o_ref):
        o_ref[...] = ((a_ref[0] + b_ref[0]) + b_ref[1]) + b_ref[2]

    return pl.pallas_call(
        body, name="grad_chip_sum",
        grid_spec=pltpu.PrefetchScalarGridSpec(
            num_scalar_prefetch=1, grid=(rh // tr,),
            in_specs=[pl.BlockSpec((1, tr, LANES), lambda i, idx: (idx[0], i, 0)),
                      pl.BlockSpec((3, tr, LANES), lambda i, idx: (0, i, 0))],
            out_specs=pl.BlockSpec((tr, LANES), lambda i, idx: (i, 0))),
        out_shape=jax.ShapeDtypeStruct((rh, LANES), F32),
        compiler_params=_params("parallel"),
    )(idx, own, recv)


def _sum_devices(parts):
    _, r, _ = parts.shape

    def body(p_ref, o_ref):
        acc = p_ref[0]
        for k in range(1, N_DEV):
            acc = acc + p_ref[k]
        o_ref[...] = acc

    return pl.pallas_call(
        body, name="small_grad_sum", grid=(1,),
        in_specs=[pl.BlockSpec((N_DEV, r, LANES), lambda i: (0, 0, 0))],
        out_specs=pl.BlockSpec((r, LANES), lambda i: (0, 0)),
        out_shape=jax.ShapeDtypeStruct((r, LANES), F32),
    )(parts)


def _place():
    return lax.axis_index("x"), lax.axis_index("y"), lax.axis_index("c")


def _rcopy(src, dst, send_sems, recv_sems, k, to):
    return pltpu.make_async_remote_copy(src_ref=src, dst_ref=dst, send_sem=send_sems.at[k],
                                        recv_sem=recv_sems.at[k], device_id=to, device_id_type=MESH)


def _gather_weights(wl):
    _, rh, _ = wl.shape

    def body(w_ref, out_ref, send_sems, recv_sems, local_sem):
        x, y, c = _place()
        sibling = (x, y, 1 - c)
        chips = [(1 - x, y), (x, 1 - y), (1 - x, 1 - y)]

        def piece(px, py, pc):
            return out_ref.at[2 * px + py, pc]

        mine = pltpu.make_async_copy(w_ref, out_ref.at[2 * x + y], local_sem)
        mine.start()
        first = [_rcopy(w_ref.at[c], piece(x, y, c), send_sems, recv_sems, j, (*chip, c))
                 for j, chip in enumerate(chips)]
        for cp in first:
            cp.start()
        passed = [_rcopy(piece(*chip, c), piece(*chip, c), send_sems, recv_sems, 3 + j, sibling)
                  for j, chip in enumerate(chips)]
        for j, chip in enumerate(chips):
            _rcopy(piece(*chip, c), piece(*chip, c), send_sems, recv_sems, j, (*chip, c)).wait_recv()
            passed[j].start()
        for j, chip in enumerate(chips):
            _rcopy(piece(*chip, 1 - c), piece(*chip, 1 - c), send_sems, recv_sems, 3 + j,
                   sibling).wait_recv()
        for cp in first + passed:
            cp.wait_send()
        mine.wait()

    return pl.pallas_call(
        body, name="gather_weights", in_specs=[ANY], out_specs=ANY,
        out_shape=jax.ShapeDtypeStruct((N_CHIPS, 2, rh, LANES), wl.dtype),
        scratch_shapes=[pltpu.SemaphoreType.DMA((6,)), pltpu.SemaphoreType.DMA((6,)),
                        pltpu.SemaphoreType.DMA],
    )(wl)


def _swap_halves(g):
    _, _, rh, _ = g.shape

    def body(g_ref, out_ref, send_sems, recv_sems):
        x, y, c = _place()
        sibling = (x, y, 1 - c)
        cps = [_rcopy(g_ref.at[k, 1 - c], out_ref.at[k], send_sems, recv_sems, k, sibling)
               for k in range(N_CHIPS)]
        for cp in cps:
            cp.start()
        for cp in cps:
            cp.wait()

    return pl.pallas_call(
        body, name="grad_swap_halves", in_specs=[ANY], out_specs=ANY,
        out_shape=jax.ShapeDtypeStruct((N_CHIPS, rh, LANES), g.dtype),
        scratch_shapes=[pltpu.SemaphoreType.DMA((N_CHIPS,)), pltpu.SemaphoreType.DMA((N_CHIPS,))],
    )(g)


def _scatter_chunks(s4):
    _, rh, _ = s4.shape

    def body(s_ref, out_ref, send_sems, recv_sems):
        x, y, c = _place()
        chips = [(1 - x, y), (x, 1 - y), (1 - x, 1 - y)]
        cps = [_rcopy(s_ref.at[2 * chip[0] + chip[1]], out_ref.at[j], send_sems, recv_sems, j,
                      (*chip, c)) for j, chip in enumerate(chips)]
        for cp in cps:
            cp.start()
        for cp in cps:
            cp.wait()

    return pl.pallas_call(
        body, name="grad_scatter_chunks", in_specs=[ANY], out_specs=ANY,
        out_shape=jax.ShapeDtypeStruct((3, rh, LANES), s4.dtype),
        scratch_shapes=[pltpu.SemaphoreType.DMA((3,)), pltpu.SemaphoreType.DMA((3,))],
    )(s4)


def _share_half(tot):
    rh, _ = tot.shape

    def body(t_ref, out_ref, send_sems, recv_sems, local_sem):
        x, y, c = _place()
        mine = pltpu.make_async_copy(t_ref, out_ref.at[c], local_sem)
        mine.start()
        cp = _rcopy(t_ref, out_ref.at[c], send_sems, recv_sems, 0, (x, y, 1 - c))
        cp.start()
        _rcopy(t_ref, out_ref.at[1 - c], send_sems, recv_sems, 0, (x, y, 1 - c)).wait_recv()
        cp.wait_send()
        mine.wait()

    return pl.pallas_call(
        body, name="grad_share_half", in_specs=[ANY], out_specs=ANY,
        out_shape=jax.ShapeDtypeStruct((2, rh, LANES), tot.dtype),
        scratch_shapes=[pltpu.SemaphoreType.DMA((1,)), pltpu.SemaphoreType.DMA((1,)),
                        pltpu.SemaphoreType.DMA],
    )(tot)


def _exchange_small(r):
    rr, _ = r.shape

    def body(r_ref, out_ref, send_sems, recv_sems, local_sem):
        x, y, c = _place()
        me = 4 * x + 2 * y + c
        mine = pltpu.make_async_copy(r_ref, out_ref.at[me], local_sem)
        mine.start()
        cps = []
        for k in range(N_DEV - 1):
            fx, fy, fc = ((k + 1) >> 2) & 1, ((k + 1) >> 1) & 1, (k + 1) & 1
            to = (x ^ fx, y ^ fy, c ^ fc)
            cps.append((_rcopy(r_ref, out_ref.at[me], send_sems, recv_sems, k, to), to))
        for cp, _ in cps:
            cp.start()
        for k, (cp, to) in enumerate(cps):
            src = 4 * to[0] + 2 * to[1] + to[2]
            _rcopy(r_ref, out_ref.at[src], send_sems, recv_sems, k, to).wait_recv()
        for cp, _ in cps:
            cp.wait_send()
        mine.wait()

    return pl.pallas_call(
        body, name="small_grad_exchange", in_specs=[ANY], out_specs=ANY,
        out_shape=jax.ShapeDtypeStruct((N_DEV, rr, LANES), r.dtype),
        scratch_shapes=[pltpu.SemaphoreType.DMA((N_DEV - 1,)), pltpu.SemaphoreType.DMA((N_DEV - 1,)),
                        pltpu.SemaphoreType.DMA],
    )(r)


def _pack_flat(arrs, dtype):
    flat = jnp.concatenate([a.reshape(-1).astype(dtype) for a in arrs])
    n = flat.shape[0]
    unit = PACK_ROWS * LANES
    pad = (-n) % unit
    if pad:
        flat = jnp.concatenate([flat, jnp.zeros((pad,), dtype)])
    return flat


def _unpack_flat(flat, shapes):
    out, off = [], 0
    for shp in shapes:
        n = 1
        for d in shp:
            n *= d
        out.append(flat[off:off + n].reshape(shp))
        off += n
    return out


def _ffn_fwd(h, bsz, s, gain, w_up, cw, cb, w_down, i):
    hf = _rmsnorm_fwd(h, gain, name=f"ffn_norm_{i}")
    up = _mm(hf, w_up, name=f"ffn_up_{i}")
    up3 = up.reshape(bsz, s, -1)
    f = _conv_glu_fwd(up3, cw, cb, name=f"ffn_glu_{i}").reshape(h.shape[0], -1)
    h2 = _mm(f, w_down, add=h, name=f"ffn_down_{i}")
    return h2, (h, hf, up3, f)


def _ffn_bwd(dh, saved, gain, w_up, cw, cb, w_down, i):
    h, hf, up3, f = saved
    t = h.shape[0]
    d_down = _mm(f, dh, ta=True, name=f"ffn_down_dw_{i}")
    df = _mm(dh, w_down, tb=True, name=f"ffn_down_dx_{i}")
    dug, duv, dcw, dcb = _conv_glu_bwd(up3, cw, cb, df.reshape(up3.shape[0], up3.shape[1], -1),
                                       name=f"ffn_glu_bwd_{i}")
    dug, duv = dug.reshape(t, -1), duv.reshape(t, -1)
    fdim = dug.shape[1]
    d_up = jnp.concatenate([_mm(hf, dug, ta=True, name=f"ffn_up_dwg_{i}"),
                            _mm(hf, duv, ta=True, name=f"ffn_up_dwv_{i}")], axis=1)
    dhf = _mm(dug, w_up[:, :fdim], tb=True, name=f"ffn_up_dxg_{i}")
    dhf = _mm(duv, w_up[:, fdim:], tb=True, add=dhf, name=f"ffn_up_dxv_{i}")
    dh, dgain = _rmsnorm_bwd(h, gain, dhf, dh, name=f"ffn_norm_bwd_{i}")
    return dh, dgain, d_up, dcw, dcb, d_down


def _ple_layer_fwd(h, p_i, gain, w_gate, w_proj, i):
    hp = _rmsnorm_fwd(h, gain, name=f"ple_norm_{i}")
    a = _mm(hp, w_gate, name=f"ple_gate_{i}")
    pp = _mm(p_i, w_proj, name=f"ple_proj_{i}")
    return _ple_fwd(h, a, pp, name=f"ple_mix_{i}"), (h, hp, a, pp)


def _ple_layer_bwd(dh, saved, p_i, gain, w_gate, i):
    h, hp, a, pp = saved
    da, dpp = _ple_bwd(dh, a, pp, name=f"ple_mix_bwd_{i}")
    d_gate = _mm(hp, da, ta=True, name=f"ple_gate_dw_{i}")
    d_proj = _mm(p_i, dpp, ta=True, name=f"ple_proj_dw_{i}")
    dhp = _mm(da, w_gate, tb=True, name=f"ple_gate_dx_{i}")
    dh, dgain = _rmsnorm_bwd(h, gain, dhp, dh, name=f"ple_norm_bwd_{i}")
    return dh, dgain, d_gate, d_proj


def _ssm_consts(dt_bias, a_log, d_skip, g_n):
    hpg = SSM_HEADS_PER_GROUP
    a = -jnp.exp(a_log)
    rows = jnp.stack([dt_bias.reshape(g_n, hpg), a.reshape(g_n, hpg)], axis=1)
    plane = jnp.zeros((g_n, 8, LANES), F32).at[:, 0:2, 0:hpg].set(rows)
    psub = jnp.zeros((g_n, SSM_ROWS, LANES), F32).at[:, 0:hpg, 0:2].set(jnp.swapaxes(rows, 1, 2))
    d_ch = jnp.repeat(d_skip, SSM_HEAD_DIM).reshape(1, -1)
    return a, plane, psub, d_ch


def _ssm_in_big(w_in, d_inner, g_n):
    d = w_in.shape[0]
    cut = w_in.shape[1] - g_n * SSM_HEADS_PER_GROUP
    wdt = w_in[:, cut:].reshape(d, g_n, SSM_HEADS_PER_GROUP)
    wdt = jnp.pad(wdt, ((0, 0), (0, 0), (0, LANES - SSM_HEADS_PER_GROUP))).reshape(d, g_n * LANES)
    return jnp.concatenate([w_in[:, :cut], wdt], axis=1)


def _ssm_in_small(dw_big, g_n):
    d = dw_big.shape[0]
    cut = dw_big.shape[1] - g_n * LANES
    ddt = dw_big[:, cut:].reshape(d, g_n, LANES)[:, :, :SSM_HEADS_PER_GROUP].reshape(d, -1)
    return jnp.concatenate([dw_big[:, :cut], ddt], axis=1)


def _ssm_fwd(h, bsz, s, gain, w_in_big, cw, cb, plane, psub, d_ch, nw, w_out, i):
    d_inner = d_ch.shape[1]
    g_n = d_inner // SSM_GROUP_W
    conv_dim = cw.shape[1]
    hn = _rmsnorm_fwd(h, gain, name=f"attn_norm_{i}")
    zx = _mm(hn, w_in_big, name=f"ssm_in_{i}").reshape(bsz, s, -1)
    xbc = _conv_silu_fwd(zx, d_inner, cw, cb, name=f"ssm_conv_{i}")
    dtr = zx[:, :, d_inner + conv_dim:].reshape(bsz, s, g_n, LANES)[..., :SSM_HEADS_PER_GROUP]
    dtr_row = jnp.pad(jnp.transpose(dtr, (0, 2, 3, 1)),
                      ((0, 0), (0, 0), (0, SSM_ROWS - SSM_HEADS_PER_GROUP), (0, 0)))
    gn, states = _ssd_fwd(zx, xbc, dtr_row, plane, psub, d_ch, nw, name=f"ssd_{i}")
    gn2 = gn.reshape(h.shape[0], -1)
    h1 = _mm(gn2, w_out, add=h, name=f"ssm_out_{i}")
    return h1, (h, hn, zx, xbc, dtr_row, states, gn2)


def _ssm_bwd(dh, saved, gain, w_in_big, cw, cb, plane, psub, d_ch, nw, w_out, i):
    h, hn, zx, xbc, dtr_row, states, gn2 = saved
    t = h.shape[0]
    bsz, s, _ = zx.shape
    d_inner = d_ch.shape[1]
    d_out = _mm(gn2, dh, ta=True, name=f"ssm_out_dw_{i}")
    dgn = _mm(dh, w_out, tb=True, name=f"ssm_out_dx_{i}").reshape(bsz, s, -1)
    dxs, dbm, dcm, dz, ddtr, ach, aln = _ssd_bwd(zx, xbc, dtr_row, plane, psub, d_ch, nw, states, dgn,
                                                  name=f"ssd_bwd_{i}")
    dxbc, dcw, dcb = _conv_silu_bwd(zx, d_inner, cw, cb, [dxs, dbm, dcm], name=f"ssm_conv_bwd_{i}")
    d_in_parts, dhn, col = [], None, 0
    for tag, part in (("z", dz), ("xbc", dxbc), ("dt", ddtr)):
        part = part.reshape(t, -1)
        d_in_parts.append(_mm(hn, part, ta=True, name=f"ssm_in_dw_{tag}_{i}"))
        dhn = _mm(part, w_in_big[:, col:col + part.shape[1]], tb=True, add=dhn,
                  name=f"ssm_in_dx_{tag}_{i}")
        col += part.shape[1]
    d_in_big = jnp.concatenate(d_in_parts, axis=1)
    dh, dgain = _rmsnorm_bwd(h, gain, dhn, dh, name=f"attn_norm_bwd_{i}")
    hpg = SSM_HEADS_PER_GROUP
    ach = jnp.sum(ach, axis=0)
    aln = jnp.sum(aln, axis=0)
    d_nw = ach[:, 0, :].reshape(-1)
    d_dskip = jnp.sum(ach[:, 1, :].reshape(-1, SSM_HEAD_DIM), axis=1)
    d_bias = aln[:, 0, :hpg].reshape(-1)
    d_a = aln[:, 1, :hpg].reshape(-1)
    return dh, dgain, d_in_big, dcw, dcb, d_bias, d_a, d_dskip, d_nw, d_out


def _sb_layer_fwd(h, bsz, s, gain, w_q, w_o, kv3, kvt, i):
    hn = _rmsnorm_fwd(h, gain, name=f"attn_norm_{i}")
    q3 = _mm(hn, w_q, out_dtype=BF16, name=f"sb_q_{i}").reshape(bsz, s, -1)
    o3, tot = _sb_fwd(q3, kv3, kvt, name=f"sb_attn_{i}")
    o2 = o3.reshape(h.shape[0], -1)
    h1 = _mm(o2, w_o, add=h, name=f"sb_o_{i}")
    return h1, (h, hn, q3, o2, tot)


def _sb_layer_bwd(dh, saved, gain, w_q, w_o, kv3, kvt, dk, dv, i):
    h, hn, q3, o2, tot = saved
    t = h.shape[0]
    d_o = _mm(o2, dh, ta=True, name=f"sb_o_dw_{i}")
    do3 = _mm(dh, w_o, tb=True, out_dtype=BF16, name=f"sb_o_dx_{i}").reshape(q3.shape)
    dq3, dk, dv = _sb_bwd(q3, kv3, kvt, do3, tot, dk, dv, name=f"sb_attn_bwd_{i}")
    dq = dq3.reshape(t, -1)
    d_q = _mm(hn, dq, ta=True, name=f"sb_q_dw_{i}")
    dhn = _mm(dq, w_q, tb=True, name=f"sb_q_dx_{i}")
    dh, dgain = _rmsnorm_bwd(h, gain, dhn, dh, name=f"attn_norm_bwd_{i}")
    return dh, dgain, d_q, d_o, dk, dv


def kernel(x, p, attn_norm, ffn_norm, ple_norm, ssm_in_proj, ssm_conv_w, ssm_conv_b, ssm_dt_bias, ssm_a_log, ssm_d, ssm_norm, ssm_out_proj, kv_norm, w_kv, w_q, w_o, ffn_up, ffn_conv_w, ffn_conv_b, ffn_down, ple_gate, ple_proj, final_norm, loss_target, m_attn_norm, m_ffn_norm, m_ple_norm, m_ssm_in_proj, m_ssm_conv_w, m_ssm_conv_b, m_ssm_dt_bias, m_ssm_a_log, m_ssm_d, m_ssm_norm, m_ssm_out_proj, m_kv_norm, m_w_kv, m_w_q, m_w_o, m_ffn_up, m_ffn_conv_w, m_ffn_conv_b, m_ffn_down, m_ple_gate, m_ple_proj, m_final_norm, v_attn_norm, v_ffn_norm, v_ple_norm, v_ssm_in_proj, v_ssm_conv_w, v_ssm_conv_b, v_ssm_dt_bias, v_ssm_a_log, v_ssm_d, v_ssm_norm, v_ssm_out_proj, v_kv_norm, v_w_kv, v_w_q, v_w_o, v_ffn_up, v_ffn_conv_w, v_ffn_conv_b, v_ffn_down, v_ple_gate, v_ple_proj, v_final_norm):
    given = dict(locals())
    wl = {n: given[n] for n in WEIGHTS}
    bsz, s, d = x.shape
    t = bsz * s
    depth = attn_norm.shape[0]
    n_a = ssm_in_proj.shape[0]
    d_inner = ssm_norm.shape[1] * N_CHIPS
    g_n = d_inner // SSM_GROUP_W
    cidx = lax.axis_index("c").astype(I32).reshape(1)
    chip_idx = (2 * lax.axis_index("x") + lax.axis_index("y")).astype(I32).reshape(1)

    local_shapes = [wl[n].shape for n in SHARDED]
    packed_w = _pack_flat([wl[n] for n in SHARDED], BF16)
    rows = packed_w.shape[0] // LANES
    rh = rows // 2
    gathered = _gather_weights(packed_w.reshape(2, rh, LANES)).reshape(N_CHIPS, rows * LANES)
    per_chip = [_unpack_flat(gathered[j], local_shapes) for j in range(N_CHIPS)]
    full = {}
    for k, n in enumerate(SHARDED):
        full[n] = jnp.concatenate([per_chip[j][k] for j in range(N_CHIPS)], axis=SHARD_AXIS[n])

    h = x.reshape(t, d)
    tgt = loss_target.reshape(t, d)
    saved = []
    kv3 = kvt = hkv = h_kv_in = None
    consts = []
    for i in range(depth):
        if i < n_a:
            a_neg, plane, psub, d_ch = _ssm_consts(ssm_dt_bias[i], ssm_a_log[i], ssm_d[i], g_n)
            w_in_big = _ssm_in_big(full["ssm_in_proj"][i], d_inner, g_n)
            cw = full["ssm_conv_w"][i].astype(F32)
            cb = full["ssm_conv_b"][i].astype(F32)
            nw = full["ssm_norm"][i].astype(F32).reshape(1, -1)
            consts.append((a_neg, plane, psub, d_ch, w_in_big, cw, cb, nw))
            h, sv_mix = _ssm_fwd(h, bsz, s, attn_norm[i], w_in_big, cw, cb, plane, psub, d_ch, nw,
                                 full["ssm_out_proj"][i], i)
        else:
            j = i - n_a
            h, sv_mix = _sb_layer_fwd(h, bsz, s, attn_norm[i], full["w_q"][j], full["w_o"][j], kv3, kvt, i)
        fcw = full["ffn_conv_w"][i].astype(F32)
        h, sv_ffn = _ffn_fwd(h, bsz, s, ffn_norm[i], full["ffn_up"][i], fcw, ffn_conv_b[i],
                             full["ffn_down"][i], i)
        p_i = p[i].reshape(t, -1)
        h, sv_ple = _ple_layer_fwd(h, p_i, ple_norm[i], full["ple_gate"][i], full["ple_proj"][i], i)
        saved.append((sv_mix, sv_ffn, sv_ple))
        if i == n_a - 1:
            h_kv_in = h
            hkv = _rmsnorm_fwd(h, kv_norm, name="kv_norm")
            kv3 = _mm(hkv, full["w_kv"], out_dtype=BF16, name="kv_proj").reshape(bsz, s, -1)
            kvt = _kv_blocks_t(kv3)

    loss_local, dh, g_final = _final_loss(h, final_norm, tgt)
    gr = {n: [None] * wl[n].shape[0] for n in WEIGHTS if n not in ("kv_norm", "w_kv", "final_norm")}
    gr["final_norm"] = g_final
    dk = dv = None
    for i in reversed(range(depth)):
        sv_mix, sv_ffn, sv_ple = saved[i]
        if i == n_a - 1:
            dkv = jnp.concatenate([dk, dv], axis=-1).reshape(t, -1)
            gr["w_kv"] = _mm(hkv, dkv, ta=True, name="kv_proj_dw")
            dhkv = _mm(dkv, full["w_kv"], tb=True, name="kv_proj_dx")
            dh, gr["kv_norm"] = _rmsnorm_bwd(h_kv_in, kv_norm, dhkv, dh, name="kv_norm_bwd")
        p_i = p[i].reshape(t, -1)
        dh, gr["ple_norm"][i], gr["ple_gate"][i], gr["ple_proj"][i] = _ple_layer_bwd(
            dh, sv_ple, p_i, ple_norm[i], full["ple_gate"][i], i)
        fcw = full["ffn_conv_w"][i].astype(F32)
        (dh, gr["ffn_norm"][i], gr["ffn_up"][i], gr["ffn_conv_w"][i], gr["ffn_conv_b"][i],
         gr["ffn_down"][i]) = _ffn_bwd(dh, sv_ffn, ffn_norm[i], full["ffn_up"][i], fcw, ffn_conv_b[i],
                                       full["ffn_down"][i], i)
        if i < n_a:
            a_neg, plane, psub, d_ch, w_in_big, cw, cb, nw = consts[i]
            (dh, gr["attn_norm"][i], d_in_big, gr["ssm_conv_w"][i], gr["ssm_conv_b"][i],
             gr["ssm_dt_bias"][i], d_a, gr["ssm_d"][i], gr["ssm_norm"][i],
             gr["ssm_out_proj"][i]) = _ssm_bwd(dh, sv_mix, attn_norm[i], w_in_big, cw, cb, plane, psub,
                                               d_ch, nw, full["ssm_out_proj"][i], i)
            gr["ssm_in_proj"][i] = _ssm_in_small(d_in_big, g_n)
            gr["ssm_a_log"][i] = d_a * a_neg
        else:
            j = i - n_a
            dh, gr["attn_norm"][i], gr["w_q"][j], gr["w_o"][j], dk, dv = _sb_layer_bwd(
                dh, sv_mix, attn_norm[i], full["w_q"][j], full["w_o"][j], kv3, kvt, dk, dv, i)
    grad_x = dh.reshape(bsz, s, d)
    gfull = {n: (jnp.stack(v) if isinstance(v, list) else v) for n, v in gr.items()}

    chunks = []
    for j in range(N_CHIPS):
        parts = [jnp.split(gfull[n], N_CHIPS, axis=SHARD_AXIS[n])[j] for n in SHARDED]
        chunks.append(_pack_flat(parts, F32))
    g4 = jnp.stack(chunks).reshape(N_CHIPS, 2, rh, LANES)
    from_sibling = _swap_halves(g4)
    chip_sums = _add_own_half(cidx, g4, from_sibling)
    from_chips = _scatter_chunks(chip_sums)
    my_half = _add_chips(chip_idx, chip_sums, from_chips)
    g_shard = _share_half(my_half).reshape(rows, LANES)

    rep_shapes = [wl[n].shape for n in REPLICATED]
    packed_r = _pack_flat([gfull[n] for n in REPLICATED], F32)
    rrows = packed_r.shape[0] // LANES
    g_rep = _sum_devices(_exchange_small(packed_r.reshape(rrows, LANES)))

    def adam(names, g_slab, shapes, tag):
        ws = _pack_flat([wl[n] for n in names], F32).reshape(-1, LANES)
        ms = _pack_flat([given["m_" + n] for n in names], F32).reshape(-1, LANES)
        vs = _pack_flat([given["v_" + n] for n in names], F32).reshape(-1, LANES)
        dl, mn, vn = _adamw(ws, g_slab, ms, vs, name="adamw_" + tag)
        un = lambda a: dict(zip(names, _unpack_flat(a.reshape(-1), shapes)))
        return un(g_slab), un(dl), un(mn), un(vn)

    sh = adam(SHARDED, g_shard, local_shapes, "sharded")
    rp = adam(REPLICATED, g_rep, rep_shapes, "replicated")
    pick = lambda k, n: (sh[k][n] if n in SHARD_AXIS else rp[k][n])
    loss = lax.psum(loss_local, ("x", "y", "c"))
    return (loss, grad_x, *[pick(0, n) for n in WEIGHTS], *[pick(1, n) for n in WEIGHTS],
            *[pick(2, n) for n in WEIGHTS], *[pick(3, n) for n in WEIGHTS])
```

```python
import functools

import jax
import jax.numpy as jnp
from jax import lax
from jax.experimental import pallas as pl
from jax.experimental.pallas import tpu as pltpu

F32 = jnp.float32
BF16 = jnp.bfloat16
I32 = jnp.int32

NORM_EPS = 1e-6
SSM_NORM_EPS = 1e-5
SSM_HEAD_DIM = 64
SSM_STATE = 128
SSM_CHUNK = 128
SSM_HEADS_PER_GROUP = 8
SSM_GROUP_W = SSM_HEADS_PER_GROUP * SSM_HEAD_DIM
SSM_CONV = 4
SSM_ROWS = 16
SB_HEAD_DIM = 64
SB_BLOCK = 128
SB_SCALE = SB_HEAD_DIM ** -0.5
LOG2E = 1.4426950408889634
FFN_CONV = 3
LANES = 128
N_CHIPS = 4
N_DEV = 8

ADAM_LR = 0.001
ADAM_B1 = 0.9
ADAM_B2 = 0.999
ADAM_EPS = 1e-08
ADAM_WD = 0.01
ADAM_STEP = 10

MESH = pl.DeviceIdType.MESH
ANY = pl.BlockSpec(memory_space=pl.ANY)

SHARD_AXIS = {
    "ssm_in_proj": 2, "ssm_conv_w": 2, "ssm_conv_b": 1, "ssm_norm": 1, "ssm_out_proj": 1,
    "w_kv": 1, "w_q": 1, "w_o": 1, "ffn_up": 2, "ffn_conv_w": 2, "ffn_down": 1,
    "ple_gate": 1, "ple_proj": 2,
}
REPLICATED = ["attn_norm", "ffn_norm", "ple_norm", "ssm_dt_bias", "ssm_a_log", "ssm_d",
              "kv_norm", "ffn_conv_b", "final_norm"]
WEIGHTS = ["attn_norm", "ffn_norm", "ple_norm", "ssm_in_proj", "ssm_conv_w", "ssm_conv_b",
           "ssm_dt_bias", "ssm_a_log", "ssm_d", "ssm_norm", "ssm_out_proj", "kv_norm", "w_kv",
           "w_q", "w_o", "ffn_up", "ffn_conv_w", "ffn_conv_b", "ffn_down", "ple_gate",
           "ple_proj", "final_norm"]
SHARDED = [n for n in WEIGHTS if n in SHARD_AXIS]
PACK_ROWS = 2048
SLAB_ROW_ALIGN = 16


def _tile(n, pref):
    t = (min(pref, n) // 128) * 128
    while t >= 128:
        if n % t == 0:
            return t
        t -= 128
    return n


def _dot(a, b):
    return jnp.dot(a, b, preferred_element_type=F32)


def _dot_nt(a, b):
    return lax.dot_general(a, b, (((1,), (1,)), ((), ())), preferred_element_type=F32)


def _dot_tn(a, b):
    return lax.dot_general(a, b, (((0,), (0,)), ((), ())), preferred_element_type=F32)


def _split2(x):
    hi = x.astype(BF16)
    lo = (x - hi.astype(F32)).astype(BF16)
    return hi, lo


def _dot2(x, m):
    hi, lo = _split2(x)
    return _dot(hi, m) + _dot(lo, m)


def _dot2_left(m, x):
    hi, lo = _split2(x)
    return _dot(m, hi) + _dot(m, lo)


def _softplus(x):
    return jnp.maximum(x, 0.0) + jnp.log(1.0 + jnp.exp(-jnp.abs(x)))


def _sigmoid(x):
    return jax.nn.sigmoid(x)


def _params(*sem):
    return pltpu.CompilerParams(dimension_semantics=sem)


MM_VMEM_BUDGET = 36 * 1024 * 1024
MM_FULL_K = 2816


def _mm_tiles(m, n, k, sa, sb, so, has_add):
    tk = k if k <= MM_FULL_K else _tile(k, 1024)
    tn = _tile(n, 1408)
    tm = _tile(m, 1408)

    def need(tm_):
        return (2 * tm_ * tk * sa + 2 * tk * tn * sb + tm_ * tn * 4 + 2 * tm_ * tn * so
                + (2 * tm_ * tn * 4 if has_add else 0))

    while need(tm) > MM_VMEM_BUDGET and tm % 256 == 0:
        tm //= 2
    return tm, tn, tk


def _mm(a, b, *, name, ta=False, tb=False, add=None, out_dtype=F32):
    m = a.shape[1] if ta else a.shape[0]
    k = a.shape[0] if ta else a.shape[1]
    n = b.shape[0] if tb else b.shape[1]
    assert (b.shape[1] if tb else b.shape[0]) == k, (a.shape, b.shape, ta, tb)
    tm, tn, tk = _mm_tiles(m, n, k, a.dtype.itemsize, b.dtype.itemsize,
                           jnp.dtype(out_dtype).itemsize, add is not None)
    nk = k // tk
    dims = (((0 if ta else 1,), (1 if tb else 0,)), ((), ()))
    has_add = add is not None

    def body(*refs):
        if has_add:
            a_ref, b_ref, add_ref, o_ref, acc_ref = refs
        else:
            a_ref, b_ref, o_ref, acc_ref = refs
        kk = pl.program_id(2)

        @pl.when(kk == 0)
        def _():
            acc_ref[...] = jnp.zeros_like(acc_ref)

        acc_ref[...] += lax.dot_general(a_ref[...].astype(BF16), b_ref[...].astype(BF16), dims,
                                        preferred_element_type=F32)

        @pl.when(kk == nk - 1)
        def _():
            r = acc_ref[...]
            if has_add:
                r = r + add_ref[...].astype(F32)
            o_ref[...] = r.astype(out_dtype)

    a_spec = (pl.BlockSpec((tk, tm), lambda i, j, kk: (kk, i)) if ta
              else pl.BlockSpec((tm, tk), lambda i, j, kk: (i, kk)))
    b_spec = (pl.BlockSpec((tn, tk), lambda i, j, kk: (j, kk)) if tb
              else pl.BlockSpec((tk, tn), lambda i, j, kk: (kk, j)))
    o_spec = pl.BlockSpec((tm, tn), lambda i, j, kk: (i, j))
    in_specs = [a_spec, b_spec] + ([o_spec] if has_add else [])
    args = (a, b) + ((add,) if has_add else ())
    return pl.pallas_call(
        body, name=name, grid=(m // tm, n // tn, nk), in_specs=in_specs, out_specs=o_spec,
        out_shape=jax.ShapeDtypeStruct((m, n), out_dtype),
        scratch_shapes=[pltpu.VMEM((tm, tn), F32)],
        compiler_params=_params("parallel", "parallel", "arbitrary"),
    )(*args)


def _rmsnorm_fwd(x, gain, *, name, rows=512):
    t, d = x.shape
    tr = _tile(t, rows)

    def body(x_ref, g_ref, o_ref):
        xv = x_ref[...]
        r = lax.rsqrt(jnp.mean(xv * xv, axis=-1, keepdims=True) + NORM_EPS)
        o_ref[...] = ((xv * r) * g_ref[...]).astype(BF16)

    return pl.pallas_call(
        body, name=name, grid=(t // tr,),
        in_specs=[pl.BlockSpec((tr, d), lambda i: (i, 0)), pl.BlockSpec((1, d), lambda i: (0, 0))],
        out_specs=pl.BlockSpec((tr, d), lambda i: (i, 0)),
        out_shape=jax.ShapeDtypeStruct((t, d), BF16),
        compiler_params=_params("parallel"),
    )(x, gain.reshape(1, d))


def _rmsnorm_bwd(x, gain, dy, dres, *, name, rows=512):
    t, d = x.shape
    tr = _tile(t, rows)

    def body(x_ref, g_ref, dy_ref, dres_ref, dx_ref, dg_ref):
        xv = x_ref[...]
        r = lax.rsqrt(jnp.mean(xv * xv, axis=-1, keepdims=True) + NORM_EPS)
        xh = xv * r
        dyv = dy_ref[...].astype(F32)
        dxh = dyv * g_ref[...]
        dx = r * (dxh - xh * jnp.mean(dxh * xh, axis=-1, keepdims=True))
        dx_ref[...] = dres_ref[...] + dx
        part = jnp.sum(dyv * xh, axis=0, keepdims=True)

        @pl.when(pl.program_id(0) == 0)
        def _():
            dg_ref[...] = part

        @pl.when(pl.program_id(0) > 0)
        def _():
            dg_ref[...] += part

    row = pl.BlockSpec((tr, d), lambda i: (i, 0))
    vec = pl.BlockSpec((1, d), lambda i: (0, 0))
    dx, dg = pl.pallas_call(
        body, name=name, grid=(t // tr,), in_specs=[row, vec, row, row], out_specs=[row, vec],
        out_shape=[jax.ShapeDtypeStruct((t, d), F32), jax.ShapeDtypeStruct((1, d), F32)],
        compiler_params=_params("arbitrary"),
    )(x, gain.reshape(1, d), dy, dres)
    return dx, dg.reshape(d)


def _final_loss(h, gain, target, *, rows=512):
    t, d = h.shape
    tr = _tile(t, rows)

    def body(x_ref, g_ref, tg_ref, dx_ref, dg_ref, loss_ref):
        xv = x_ref[...]
        g = g_ref[...]
        r = lax.rsqrt(jnp.mean(xv * xv, axis=-1, keepdims=True) + NORM_EPS)
        xh = xv * r
        err = xh * g - tg_ref[...]
        dyv = err * (1.0 / d)
        dxh = dyv * g
        dx_ref[...] = r * (dxh - xh * jnp.mean(dxh * xh, axis=-1, keepdims=True))
        part = jnp.sum(dyv * xh, axis=0, keepdims=True)
        lpart = jnp.zeros((1, LANES), F32) + (0.5 / d) * jnp.sum(err * err)

        @pl.when(pl.program_id(0) == 0)
        def _():
            dg_ref[...] = part
            loss_ref[...] = lpart

        @pl.when(pl.program_id(0) > 0)
        def _():
            dg_ref[...] += part
            loss_ref[...] += lpart

    row = pl.BlockSpec((tr, d), lambda i: (i, 0))
    vec = pl.BlockSpec((1, d), lambda i: (0, 0))
    dx, dg, loss = pl.pallas_call(
        body, name="final_loss", grid=(t // tr,), in_specs=[row, vec, row],
        out_specs=[row, vec, pl.BlockSpec((1, LANES), lambda i: (0, 0))],
        out_shape=[jax.ShapeDtypeStruct((t, d), F32), jax.ShapeDtypeStruct((1, d), F32),
                   jax.ShapeDtypeStruct((1, LANES), F32)],
        compiler_params=_params("arbitrary"),
    )(h, gain.reshape(1, d), target)
    return loss[0, 0], dx, dg.reshape(d)


def _ple_fwd(h, a, pp, *, name, rows=512):
    t, d = h.shape
    tr = _tile(t, rows)

    def body(h_ref, a_ref, p_ref, o_ref):
        o_ref[...] = h_ref[...] + _sigmoid(a_ref[...]) * p_ref[...]

    row = pl.BlockSpec((tr, d), lambda i: (i, 0))
    return pl.pallas_call(
        body, name=name, grid=(t // tr,), in_specs=[row, row, row], out_specs=row,
        out_shape=jax.ShapeDtypeStruct((t, d), F32), compiler_params=_params("parallel"),
    )(h, a, pp)


def _ple_bwd(dh, a, pp, *, name, rows=512):
    t, d = dh.shape
    tr = _tile(t, rows)

    def body(dh_ref, a_ref, p_ref, da_ref, dp_ref):
        s = _sigmoid(a_ref[...])
        dhv = dh_ref[...]
        da_ref[...] = (dhv * p_ref[...] * (s * (1.0 - s))).astype(BF16)
        dp_ref[...] = (dhv * s).astype(BF16)

    row = pl.BlockSpec((tr, d), lambda i: (i, 0))
    return pl.pallas_call(
        body, name=name, grid=(t // tr,), in_specs=[row, row, row], out_specs=[row, row],
        out_shape=[jax.ShapeDtypeStruct((t, d), BF16)] * 2, compiler_params=_params("parallel"),
    )(dh, a, pp)


def _shift_down(u, j):
    if j == 0:
        return u
    rows = lax.broadcasted_iota(I32, u.shape, 0)
    return jnp.where(rows >= j, pltpu.roll(u, j, 0), 0.0)


def _shift_up(u, j):
    if j == 0:
        return u
    s = u.shape[0]
    rows = lax.broadcasted_iota(I32, u.shape, 0)
    return jnp.where(rows < s - j, pltpu.roll(u, s - j, 0), 0.0)


def _conv_pre(u, wv, bv):
    kw = wv.shape[0]
    shifted = [_shift_down(u, kw - 1 - k) for k in range(kw)]
    pre = bv + wv[0:1, :] * shifted[0]
    for k in range(1, kw):
        pre = pre + wv[k:k + 1, :] * shifted[k]
    return pre, shifted


def _conv_back(dpre, wv, shifted, dw_ref, db_ref, first):
    kw = wv.shape[0]
    du = wv[kw - 1:kw, :] * dpre
    for k in range(kw - 1):
        du = du + wv[k:k + 1, :] * _shift_up(dpre, kw - 1 - k)
    dws = [jnp.sum(dpre * shifted[k], axis=0, keepdims=True) for k in range(kw)]
    dbs = jnp.sum(dpre, axis=0, keepdims=True)

    @pl.when(first)
    def _():
        for k in range(kw):
            dw_ref[k:k + 1, :] = dws[k]
        db_ref[...] = dbs

    @pl.when(jnp.logical_not(first))
    def _():
        for k in range(kw):
            dw_ref[k:k + 1, :] += dws[k]
        db_ref[...] += dbs

    return du


def _dsilu(pre):
    s = _sigmoid(pre)
    return s, s * (1.0 + pre * (1.0 - s))


def _conv_silu_fwd(zx, off, w, b, *, name, tc=128):
    bsz, s, _ = zx.shape
    kw, c = w.shape
    o0 = off // tc

    def body(u_ref, w_ref, b_ref, o_ref):
        pre, _ = _conv_pre(u_ref[0], w_ref[...], b_ref[...])
        o_ref[0] = pre * _sigmoid(pre)

    return pl.pallas_call(
        body, name=name, grid=(bsz, c // tc),
        in_specs=[pl.BlockSpec((1, s, tc), lambda i, j: (i, 0, o0 + j)),
                  pl.BlockSpec((kw, tc), lambda i, j: (0, j)),
                  pl.BlockSpec((1, tc), lambda i, j: (0, j))],
        out_specs=pl.BlockSpec((1, s, tc), lambda i, j: (i, 0, j)),
        out_shape=jax.ShapeDtypeStruct((bsz, s, c), F32),
        compiler_params=_params("parallel", "parallel"),
    )(zx, w, b.reshape(1, c))


def _conv_silu_bwd(zx, off, w, b, douts, *, name, tc=128):
    bsz, s, _ = zx.shape
    kw, c = w.shape
    o0 = off // tc
    counts = [d.shape[2] // tc for d in douts]
    starts = [sum(counts[:k]) for k in range(len(douts))]
    assert sum(counts) == c // tc

    def body(u_ref, w_ref, b_ref, *rest):
        dy_refs = rest[:len(douts)]
        du_ref, dw_ref, db_ref = rest[len(douts):]
        j = pl.program_id(0)
        dy = dy_refs[0][0]
        for k in range(1, len(douts)):
            dy = jnp.where(j >= starts[k], dy_refs[k][0], dy)
        wv = w_ref[...]
        pre, shifted = _conv_pre(u_ref[0], wv, b_ref[...])
        _, ds = _dsilu(pre)
        du = _conv_back(dy * ds, wv, shifted, dw_ref, db_ref, pl.program_id(1) == 0)
        du_ref[0] = du.astype(BF16)

    def part_spec(k):
        return pl.BlockSpec((1, s, tc), lambda j, i: (i, 0, jnp.clip(j - starts[k], 0, counts[k] - 1)))

    du, dw, db = pl.pallas_call(
        body, name=name, grid=(c // tc, bsz),
        in_specs=[pl.BlockSpec((1, s, tc), lambda j, i: (i, 0, o0 + j)),
                  pl.BlockSpec((kw, tc), lambda j, i: (0, j)),
                  pl.BlockSpec((1, tc), lambda j, i: (0, j))] + [part_spec(k) for k in range(len(douts))],
        out_specs=[pl.BlockSpec((1, s, tc), lambda j, i: (i, 0, j)),
                   pl.BlockSpec((kw, tc), lambda j, i: (0, j)),
                   pl.BlockSpec((1, tc), lambda j, i: (0, j))],
        out_shape=[jax.ShapeDtypeStruct((bsz, s, c), BF16), jax.ShapeDtypeStruct((kw, c), F32),
                   jax.ShapeDtypeStruct((1, c), F32)],
        compiler_params=_params("parallel", "arbitrary"),
    )(zx, w, b.reshape(1, c), *douts)
    return du, dw, db.reshape(c)


def _conv_glu_fwd(up, w, b, *, name, tc=128):
    bsz, s, c2 = up.shape
    kw = w.shape[0]
    f = c2 // 2
    nt = f // tc

    def body(ug_ref, uv_ref, wg_ref, wv_ref, bg_ref, bv_ref, o_ref):
        pg, _ = _conv_pre(ug_ref[0], wg_ref[...], bg_ref[...])
        pv, _ = _conv_pre(uv_ref[0], wv_ref[...], bv_ref[...])
        o_ref[0] = (pg * _sigmoid(pg) * pv).astype(BF16)

    b2 = b.reshape(1, c2)
    return pl.pallas_call(
        body, name=name, grid=(bsz, nt),
        in_specs=[pl.BlockSpec((1, s, tc), lambda i, j: (i, 0, j)),
                  pl.BlockSpec((1, s, tc), lambda i, j: (i, 0, nt + j)),
                  pl.BlockSpec((kw, tc), lambda i, j: (0, j)),
                  pl.BlockSpec((kw, tc), lambda i, j: (0, nt + j)),
                  pl.BlockSpec((1, tc), lambda i, j: (0, j)),
                  pl.BlockSpec((1, tc), lambda i, j: (0, nt + j))],
        out_specs=pl.BlockSpec((1, s, tc), lambda i, j: (i, 0, j)),
        out_shape=jax.ShapeDtypeStruct((bsz, s, f), BF16),
        compiler_params=_params("parallel", "parallel"),
    )(up, up, w, w, b2, b2)


def _conv_glu_bwd(up, w, b, df, *, name, tc=128):
    bsz, s, c2 = up.shape
    kw = w.shape[0]
    f = c2 // 2
    nt = f // tc

    def body(ug_ref, uv_ref, wg_ref, wv_ref, bg_ref, bv_ref, df_ref,
             dug_ref, duv_ref, dwg_ref, dwv_ref, dbg_ref, dbv_ref):
        first = pl.program_id(1) == 0
        wg = wg_ref[...]
        wv = wv_ref[...]
        pg, sg = _conv_pre(ug_ref[0], wg, bg_ref[...])
        pv, sv = _conv_pre(uv_ref[0], wv, bv_ref[...])
        sig, dsl = _dsilu(pg)
        dfv = df_ref[0]
        dug_ref[0] = _conv_back(dfv * pv * dsl, wg, sg, dwg_ref, dbg_ref, first).astype(BF16)
        duv_ref[0] = _conv_back(dfv * (pg * sig), wv, sv, dwv_ref, dbv_ref, first).astype(BF16)

    b2 = b.reshape(1, c2)
    act = lambda j, i: (i, 0, j)
    wsp = pl.BlockSpec((kw, tc), lambda j, i: (0, j))
    bsp = pl.BlockSpec((1, tc), lambda j, i: (0, j))
    dug, duv, dwg, dwv, dbg, dbv = pl.pallas_call(
        body, name=name, grid=(nt, bsz),
        in_specs=[pl.BlockSpec((1, s, tc), act),
                  pl.BlockSpec((1, s, tc), lambda j, i: (i, 0, nt + j)),
                  wsp, pl.BlockSpec((kw, tc), lambda j, i: (0, nt + j)),
                  bsp, pl.BlockSpec((1, tc), lambda j, i: (0, nt + j)),
                  pl.BlockSpec((1, s, tc), act)],
        out_specs=[pl.BlockSpec((1, s, tc), act), pl.BlockSpec((1, s, tc), act), wsp, wsp, bsp, bsp],
        out_shape=[jax.ShapeDtypeStruct((bsz, s, f), BF16)] * 2
        + [jax.ShapeDtypeStruct((kw, f), F32)] * 2 + [jax.ShapeDtypeStruct((1, f), F32)] * 2,
        compiler_params=_params("parallel", "arbitrary"),
    )(up, up, w, w, b2, b2, df)
    return (dug, duv, jnp.concatenate([dwg, dwv], axis=1),
            jnp.concatenate([dbg.reshape(f), dbv.reshape(f)]))


def _ssd_shared(xs, bm, cm, dtc_raw, dtr_raw, plane, psub, st):
    cl = SSM_CHUNK
    bias_l, a_l = plane[0:1, :], plane[1:2, :]
    bias_s, a_s = psub[:, 0:1], psub[:, 1:2]
    ri = lax.broadcasted_iota(I32, (cl, cl), 0)
    ci = lax.broadcasted_iota(I32, (cl, cl), 1)
    tril = ri >= ci
    low_incl = tril.astype(BF16)
    up_incl = (ri <= ci).astype(BF16)
    seg_t = (lax.broadcasted_iota(I32, (LANES, SSM_GROUP_W), 0)
             == lax.broadcasted_iota(I32, (LANES, SSM_GROUP_W), 1) // SSM_HEAD_DIM).astype(BF16)
    dt_c = _softplus(dtc_raw + bias_l)
    cs_c = _dot2_left(low_incl, dt_c * a_l)
    dt_r = _softplus(dtr_raw + bias_s)
    cs_r = _dot2(dt_r * a_s, up_incl)
    dt_ch = _dot2(dt_c, seg_t)
    cs_ch = _dot2(cs_c, seg_t)
    cs_last = cs_ch[cl - 1:cl, :]
    decay_ch = jnp.exp(cs_ch)
    w_ch = jnp.exp(cs_last - cs_ch)
    tot_ch = jnp.exp(cs_last)
    xdt = xs * dt_ch
    bm_b, cm_b = bm.astype(BF16), cm.astype(BF16)
    gmat = _dot_nt(cm_b, bm_b)
    cst = _dot(cm_b, st.astype(BF16))
    yoff = decay_ch * cst
    return dict(tril=tril, low_incl=low_incl, up_incl=up_incl, seg_t=seg_t, a_l=a_l, bias_l=bias_l,
                dt_c=dt_c, cs_c=cs_c, cs_r=cs_r, dt_ch=dt_ch, decay_ch=decay_ch, w_ch=w_ch,
                tot_ch=tot_ch, xdt=xdt, bm_b=bm_b, cm_b=cm_b, gmat=gmat, yoff=yoff)


def _head_decay(q, r):
    diff = q["cs_c"][:, r:r + 1] - q["cs_r"][r:r + 1, :]
    return jnp.where(q["tril"], jnp.exp(jnp.minimum(diff, 0.0)), 0.0)


def _half_mask(hh):
    lane = lax.broadcasted_iota(I32, (SSM_CHUNK, LANES), 1)
    return (lane < SSM_HEAD_DIM) if hh == 0 else (lane >= SSM_HEAD_DIM)


def _ssd_ydiag(q):
    pairs = []
    for pr in range(SSM_HEADS_PER_GROUP // 2):
        xp = q["xdt"][:, pr * LANES:(pr + 1) * LANES]
        acc = None
        for hh in range(2):
            mm_ = (q["gmat"] * _head_decay(q, 2 * pr + hh)).astype(BF16)
            part = _dot(mm_, jnp.where(_half_mask(hh), xp, 0.0).astype(BF16))
            acc = part if acc is None else acc + part
        pairs.append(acc)
    return jnp.concatenate(pairs, axis=1)


def _ssd_specs(bsz, s, g_n, d_inner, rev):
    cl = SSM_CHUNK
    nc = s // cl
    cc = (lambda c: nc - 1 - c) if rev else (lambda c: c)
    gb = d_inner // LANES
    dt0 = (d_inner + d_inner + 2 * g_n * SSM_STATE) // LANES
    return dict(
        z=pl.BlockSpec((1, cl, SSM_GROUP_W), lambda b, g, c: (b, cc(c), g)),
        dtc=pl.BlockSpec((1, cl, LANES), lambda b, g, c: (b, cc(c), dt0 + g)),
        xs=pl.BlockSpec((1, cl, SSM_GROUP_W), lambda b, g, c: (b, cc(c), g)),
        bm=pl.BlockSpec((1, cl, LANES), lambda b, g, c: (b, cc(c), gb + g)),
        cm=pl.BlockSpec((1, cl, LANES), lambda b, g, c: (b, cc(c), gb + g_n + g)),
        dtr=pl.BlockSpec((1, 1, SSM_ROWS, cl), lambda b, g, c: (b, g, 0, cc(c))),
        plane=pl.BlockSpec((1, 8, LANES), lambda b, g, c: (g, 0, 0)),
        psub=pl.BlockSpec((1, SSM_ROWS, LANES), lambda b, g, c: (g, 0, 0)),
        chan=pl.BlockSpec((1, SSM_GROUP_W), lambda b, g, c: (0, g)),
        state=pl.BlockSpec((1, 1, 1, SSM_STATE, SSM_GROUP_W), lambda b, g, c: (b, g, cc(c), 0, 0)),
        bgrp=pl.BlockSpec((1, cl, LANES), lambda b, g, c: (b, cc(c), g)),
    )


def _ssd_fwd(zx, xbc, dtr_row, plane, psub, d_ch, nw, *, name):
    bsz, s, _ = zx.shape
    d_inner = d_ch.shape[1]
    g_n = d_inner // SSM_GROUP_W
    nc = s // SSM_CHUNK
    sp = _ssd_specs(bsz, s, g_n, d_inner, False)

    def body(z_ref, dtc_ref, xs_ref, bm_ref, cm_ref, dtr_ref, plane_ref, psub_ref, d_ref, nw_ref,
             gn_ref, st_out_ref, st_ref):
        @pl.when(pl.program_id(2) == 0)
        def _():
            st_ref[...] = jnp.zeros_like(st_ref)

        xs = xs_ref[0]
        st = st_ref[...]
        st_out_ref[0, 0, 0] = st
        q = _ssd_shared(xs, bm_ref[0], cm_ref[0], dtc_ref[0], dtr_ref[0, 0], plane_ref[0],
                        psub_ref[0], st)
        y = _ssd_ydiag(q) + q["yoff"] + xs * d_ref[...]
        st_ref[...] = q["tot_ch"] * st + _dot_tn(q["bm_b"], (q["w_ch"] * q["xdt"]).astype(BF16))
        zv = z_ref[0]
        gy = y * (zv * _sigmoid(zv))
        rstd = lax.rsqrt(jnp.mean(gy * gy, axis=-1, keepdims=True) + SSM_NORM_EPS)
        gn_ref[0] = ((gy * rstd) * nw_ref[...]).astype(BF16)

    return pl.pallas_call(
        body, name=name, grid=(bsz, g_n, nc),
        in_specs=[sp["z"], sp["dtc"], sp["xs"], sp["bm"], sp["cm"], sp["dtr"], sp["plane"],
                  sp["psub"], sp["chan"], sp["chan"]],
        out_specs=[sp["z"], sp["state"]],
        out_shape=[jax.ShapeDtypeStruct((bsz, s, d_inner), BF16),
                   jax.ShapeDtypeStruct((bsz, g_n, nc, SSM_STATE, SSM_GROUP_W), F32)],
        scratch_shapes=[pltpu.VMEM((SSM_STATE, SSM_GROUP_W), F32)],
        compiler_params=_params("parallel", "parallel", "arbitrary"),
    )(zx, zx, xbc, xbc, xbc, dtr_row, plane, psub, d_ch, nw)


def _ssd_bwd(zx, xbc, dtr_row, plane, psub, d_ch, nw, states, dgn, *, name):
    bsz, s, _ = zx.shape
    d_inner = d_ch.shape[1]
    g_n = d_inner // SSM_GROUP_W
    cl = SSM_CHUNK
    nc = s // cl
    sp = _ssd_specs(bsz, s, g_n, d_inner, True)
    acc_ch = pl.BlockSpec((1, 1, 8, SSM_GROUP_W), lambda b, g, c: (b, g, 0, 0))
    acc_ln = pl.BlockSpec((1, 1, 8, LANES), lambda b, g, c: (b, g, 0, 0))

    def body(z_ref, dtc_ref, xs_ref, bm_ref, cm_ref, dtr_ref, plane_ref, psub_ref, d_ref, nw_ref,
             st_in_ref, dgn_ref,
             dxs_ref, dbm_ref, dcm_ref, dz_ref, ddt_ref, ach_ref, aln_ref, dst_ref):
        first = pl.program_id(2) == 0

        @pl.when(first)
        def _():
            dst_ref[...] = jnp.zeros_like(dst_ref)
            ach_ref[...] = jnp.zeros_like(ach_ref)
            aln_ref[...] = jnp.zeros_like(aln_ref)

        xs = xs_ref[0]
        st = st_in_ref[0, 0, 0]
        q = _ssd_shared(xs, bm_ref[0], cm_ref[0], dtc_ref[0], dtr_ref[0, 0], plane_ref[0],
                        psub_ref[0], st)
        d_chv = d_ref[...]
        nwv = nw_ref[...]
        y = _ssd_ydiag(q) + q["yoff"] + xs * d_chv
        zv = z_ref[0]
        sz = _sigmoid(zv)
        silu_z = zv * sz
        gy = y * silu_z
        rstd = lax.rsqrt(jnp.mean(gy * gy, axis=-1, keepdims=True) + SSM_NORM_EPS)
        gyh = gy * rstd
        dgnv = dgn_ref[0]
        dgyh = dgnv * nwv
        dgy = rstd * (dgyh - gyh * jnp.mean(dgyh * gyh, axis=-1, keepdims=True))
        dy = dgy * silu_z
        dz_ref[0] = (dgy * y * (sz * (1.0 + zv * (1.0 - sz)))).astype(BF16)
        ach_ref[0, 0, 0:1, :] += jnp.sum(dgnv * gyh, axis=0, keepdims=True)
        ach_ref[0, 0, 1:2, :] += jnp.sum(dy * xs, axis=0, keepdims=True)
        st_b = st.astype(BF16)
        dyd = (dy * q["decay_ch"]).astype(BF16)
        dcm = _dot_nt(dyd, st_b)
        dstn = dst_ref[...]
        dstn_b = dstn.astype(BF16)
        bds = _dot(q["bm_b"], dstn_b)
        wx = q["w_ch"] * q["xdt"]
        dbm = _dot_nt(wx.astype(BF16), dstn_b)
        dst_ref[...] = q["tot_ch"] * dstn + _dot_tn(q["cm_b"], dyd)
        vterm = wx * bds
        cs_terms = dy * q["yoff"] - vterm
        last_ch = q["tot_ch"] * jnp.sum(dstn * st, axis=0, keepdims=True) + jnp.sum(vterm, axis=0, keepdims=True)
        lane = lax.broadcasted_iota(I32, (cl, LANES), 1)
        rowi = lax.broadcasted_iota(I32, (SSM_ROWS, cl), 0)
        dg_sum = jnp.zeros((cl, cl), F32)
        dcs_col = jnp.zeros((cl, LANES), F32)
        dcs_row = jnp.zeros((SSM_ROWS, cl), F32)
        dxdt_pairs = []
        for pr in range(SSM_HEADS_PER_GROUP // 2):
            xp_b = q["xdt"][:, pr * LANES:(pr + 1) * LANES].astype(BF16)
            dyp = dy[:, pr * LANES:(pr + 1) * LANES]
            acc = None
            for hh in range(2):
                r = 2 * pr + hh
                dm = _head_decay(q, r)
                mmat = q["gmat"] * dm
                dym = jnp.where(_half_mask(hh), dyp, 0.0).astype(BF16)
                dmat = jnp.where(q["tril"], _dot_nt(dym, xp_b), 0.0)
                part = _dot_tn(mmat.astype(BF16), dym)
                acc = part if acc is None else acc + part
                dg_sum = dg_sum + dmat * dm
                e = dmat * mmat
                dcs_col = dcs_col + jnp.where(lane == r, jnp.sum(e, axis=1, keepdims=True), 0.0)
                dcs_row = dcs_row + jnp.where(rowi == r, jnp.sum(e, axis=0, keepdims=True), 0.0)
            dxdt_pairs.append(acc)
        dg_b = dg_sum.astype(BF16)
        dcm_ref[0] = dcm + _dot(dg_b, q["bm_b"])
        dbm_ref[0] = dbm + _dot_tn(dg_b, q["cm_b"])
        dxdt = q["w_ch"] * bds + jnp.concatenate(dxdt_pairs, axis=1)
        dxs_ref[0] = dy * d_chv + dxdt * q["dt_ch"]
        seg = (lax.broadcasted_iota(I32, (SSM_GROUP_W, LANES), 0) // SSM_HEAD_DIM
               == lax.broadcasted_iota(I32, (SSM_GROUP_W, LANES), 1)).astype(BF16)
        row_as_col = jnp.transpose(jnp.concatenate(
            [dcs_row, jnp.zeros((cl - SSM_ROWS, cl), F32)], axis=0))
        dcs = dcs_col - row_as_col + _dot2(cs_terms, seg)
        last = _dot2(jnp.zeros((8, SSM_GROUP_W), F32) + last_ch, seg)[0:1, :]
        da = _dot2_left(q["up_incl"], dcs) + last
        ddt = _dot2(dxdt * xs, seg) + da * q["a_l"]
        ddtr = ddt * _sigmoid(dtc_ref[0] + q["bias_l"])
        ddt_ref[0] = ddtr.astype(BF16)
        aln_ref[0, 0, 0:1, :] += jnp.sum(ddtr, axis=0, keepdims=True)
        aln_ref[0, 0, 1:2, :] += jnp.sum(da * q["dt_c"], axis=0, keepdims=True)

    outs = pl.pallas_call(
        body, name=name, grid=(bsz, g_n, nc),
        in_specs=[sp["z"], sp["dtc"], sp["xs"], sp["bm"], sp["cm"], sp["dtr"], sp["plane"],
                  sp["psub"], sp["chan"], sp["chan"], sp["state"], sp["z"]],
        out_specs=[sp["z"], sp["bgrp"], sp["bgrp"], sp["z"], sp["bgrp"], acc_ch, acc_ln],
        out_shape=[jax.ShapeDtypeStruct((bsz, s, d_inner), F32),
                   jax.ShapeDtypeStruct((bsz, s, g_n * SSM_STATE), F32),
                   jax.ShapeDtypeStruct((bsz, s, g_n * SSM_STATE), F32),
                   jax.ShapeDtypeStruct((bsz, s, d_inner), BF16),
                   jax.ShapeDtypeStruct((bsz, s, g_n * LANES), BF16),
                   jax.ShapeDtypeStruct((bsz, g_n, 8, SSM_GROUP_W), F32),
                   jax.ShapeDtypeStruct((bsz, g_n, 8, LANES), F32)],
        scratch_shapes=[pltpu.VMEM((SSM_STATE, SSM_GROUP_W), F32)],
        compiler_params=_params("parallel", "parallel", "arbitrary"),
    )(zx, zx, xbc, xbc, xbc, dtr_row, plane, psub, d_ch, nw, states, dgn)
    return outs


def _sb_stack(x):
    lane = lax.broadcasted_iota(I32, x.shape, 1)
    zero = jnp.zeros_like(x)
    return jnp.concatenate([jnp.where(lane < SB_HEAD_DIM, x, zero),
                            jnp.where(lane >= SB_HEAD_DIM, x, zero)], axis=0)


def _sb_unstack_t(acc_t):
    row = lax.broadcasted_iota(I32, (LANES, SB_BLOCK), 0)
    return jnp.transpose(jnp.where(row < SB_HEAD_DIM, acc_t[:, :SB_BLOCK], acc_t[:, SB_BLOCK:]))


def _sb_tile_blocks(nq):
    return 4 if nq % 4 == 0 else (2 if nq % 2 == 0 else 1)


def _sb_valid(u, qi, nb):
    shape = (nb * SB_BLOCK, 2 * SB_BLOCK)
    key = u * (nb * SB_BLOCK) + lax.broadcasted_iota(I32, shape, 0)
    qpos = qi * SB_BLOCK + lax.broadcasted_iota(I32, shape, 1) % SB_BLOCK
    return key < qpos


def _sb_logits(kb, qs, valid):
    z2 = _dot_nt(kb, qs) * LOG2E
    lb = jnp.minimum(z2, 0.0) - jnp.log2(1.0 + jnp.exp2(-jnp.abs(z2)))
    lk_all = lb - z2
    lk = lk_all if valid is None else jnp.where(valid, lk_all, 0.0)
    return lb, lk_all, lk


def _sb_scan(tri2, x, nb, reverse, exact=True):
    blk = SB_BLOCK
    edge = 0 if reverse else blk - 1
    carry = jnp.zeros((1, x.shape[1]), F32)
    res = [None] * nb
    for i in (reversed(range(nb)) if reverse else range(nb)):
        part = x[i * blk:(i + 1) * blk]
        if exact:
            hi, lo = _split2(part)
            raw = _dot(tri2, jnp.concatenate([hi, lo], axis=0))
        else:
            raw = _dot(tri2[:, :blk], part.astype(BF16))
        res[i] = raw + carry
        carry = carry + (raw[edge:edge + 1] + part[edge:edge + 1])
    return jnp.concatenate(res, axis=0), carry


def _sb_fwd(q, kv, kvt, *, name):
    bsz, s, w = q.shape
    blk = SB_BLOCK
    npair = w // LANES
    nq = s // blk
    nb = _sb_tile_blocks(nq)

    def body(q_ref, k_ref, vt_ref, o_ref, tot_ref):
        qi = pl.program_id(2)
        qs = _sb_stack(q_ref[0] * SB_SCALE)
        ri = lax.broadcasted_iota(I32, (blk, blk), 0)
        ci = lax.broadcasted_iota(I32, (blk, blk), 1)
        upper = (ri < ci).astype(BF16)
        tri2 = jnp.concatenate([upper, upper], axis=1)

        def tile(u, r, acc, masked):
            rows = pl.ds(pl.multiple_of(u * (nb * blk), nb * blk), nb * blk)
            valid = _sb_valid(u, qi, nb) if masked else None
            lb, _, lk = _sb_logits(k_ref[0, rows, :], qs, valid)
            sfx, total = _sb_scan(tri2, lk, nb, True)
            wgt = jnp.exp2(lb + sfx + r)
            if masked:
                wgt = jnp.where(valid, wgt, 0.0)
            wb = wgt.astype(BF16)
            for i in range(nb):
                acc = acc + _dot(vt_ref[0, 0, u * nb + i], wb[i * blk:(i + 1) * blk])
            return r + total, acc

        top = qi // nb
        r, acc = tile(top, jnp.zeros((1, 2 * blk), F32), jnp.zeros((LANES, 2 * blk), F32), True)
        r, acc = lax.fori_loop(0, top, lambda t, c: tile(top - 1 - t, c[0], c[1], False), (r, acc))
        o_ref[0] = _sb_unstack_t(acc).astype(BF16)
        tot_ref[0, 0, 0] = r

    return pl.pallas_call(
        body, name=name, grid=(bsz, npair, nq),
        in_specs=[pl.BlockSpec((1, blk, LANES), lambda b, p, i: (b, i, p)),
                  pl.BlockSpec((1, s, LANES), lambda b, p, i: (b, 0, p)),
                  pl.BlockSpec((1, 1, nq, LANES, blk), lambda b, p, i: (b, npair + p, 0, 0, 0))],
        out_specs=[pl.BlockSpec((1, blk, LANES), lambda b, p, i: (b, i, p)),
                   pl.BlockSpec((1, 1, 1, 1, 2 * blk), lambda b, p, i: (b, p, i, 0, 0))],
        out_shape=[jax.ShapeDtypeStruct((bsz, s, w), BF16),
                   jax.ShapeDtypeStruct((bsz, npair, nq, 1, 2 * blk), F32)],
        compiler_params=_params("parallel", "parallel", "arbitrary"),
    )(q, kv, kvt)


def _kv_blocks_t(kv3):
    bsz, s, w2 = kv3.shape
    x = kv3.reshape(bsz, s // SB_BLOCK, SB_BLOCK, w2 // LANES, LANES)
    return jnp.transpose(x, (0, 3, 1, 4, 2))


def _sb_bwd(q, kv, kvt, do, tot, dk_in, dv_in, *, name):
    bsz, s, w = q.shape
    blk = SB_BLOCK
    npair = w // LANES
    nq = s // blk
    nb = _sb_tile_blocks(nq)
    has_init = dk_in is not None

    def body(*refs):
        if has_init:
            q_ref, k_ref, v_ref, kt_ref, do_ref, tot_ref, dki_ref, dvi_ref, dq_ref, dk_ref, dv_ref = refs
        else:
            q_ref, k_ref, v_ref, kt_ref, do_ref, tot_ref, dq_ref, dk_ref, dv_ref = refs
        qi = pl.program_id(2)

        @pl.when(qi == 0)
        def _():
            if has_init:
                dk_ref[...] = dki_ref[...]
                dv_ref[...] = dvi_ref[...]
            else:
                dk_ref[...] = jnp.zeros_like(dk_ref)
                dv_ref[...] = jnp.zeros_like(dv_ref)

        qs = _sb_stack(q_ref[0] * SB_SCALE)
        dos = _sb_stack(do_ref[0])
        totv = tot_ref[0, 0, 0]
        ri = lax.broadcasted_iota(I32, (blk, blk), 0)
        ci = lax.broadcasted_iota(I32, (blk, blk), 1)
        lower = (ri > ci).astype(BF16)
        tri2 = jnp.concatenate([lower, lower], axis=1)

        def tile(u, pre_lk, pre_d, dqt, masked):
            rows = pl.ds(pl.multiple_of(u * (nb * blk), nb * blk), nb * blk)
            valid = _sb_valid(u, qi, nb) if masked else None
            lb, lk_all, lk = _sb_logits(k_ref[0, rows, :], qs, valid)
            before, tot_lk = _sb_scan(tri2, lk, nb, False)
            wgt = jnp.exp2(lb + (totv - (pre_lk + before + lk)))
            if masked:
                wgt = jnp.where(valid, wgt, 0.0)
            dlogit = _dot_nt(v_ref[0, rows, :], dos) * wgt
            dbefore, tot_d = _sb_scan(tri2, dlogit, nb, False)
            dz = dlogit * jnp.exp2(lk_all) - (pre_d + dbefore) * jnp.exp2(lb)
            if masked:
                dz = jnp.where(valid, dz, 0.0)
            dz_b = dz.astype(BF16)
            for i in range(nb):
                dqt = dqt + _dot(kt_ref[0, 0, u * nb + i], dz_b[i * blk:(i + 1) * blk])
            dk_ref[0, rows, :] += _dot(dz_b, qs)
            dv_ref[0, rows, :] += _dot(wgt.astype(BF16), dos)
            return pre_lk + tot_lk, pre_d + tot_d, dqt

        zero = jnp.zeros((1, 2 * blk), F32)
        top = qi // nb
        c = lax.fori_loop(0, top, lambda u, c: tile(u, c[0], c[1], c[2], False),
                          (zero, zero, jnp.zeros((LANES, 2 * blk), F32)))
        _, _, dqt = tile(top, c[0], c[1], c[2], True)
        dq_ref[0] = (_sb_unstack_t(dqt) * SB_SCALE).astype(BF16)

    qspec = pl.BlockSpec((1, blk, LANES), lambda b, p, i: (b, i, p))
    kspec = pl.BlockSpec((1, s, LANES), lambda b, p, i: (b, 0, p))
    vspec = pl.BlockSpec((1, s, LANES), lambda b, p, i: (b, 0, npair + p))
    ktspec = pl.BlockSpec((1, 1, nq, LANES, blk), lambda b, p, i: (b, p, 0, 0, 0))
    tspec = pl.BlockSpec((1, 1, 1, 1, 2 * blk), lambda b, p, i: (b, p, i, 0, 0))
    in_specs = [qspec, kspec, vspec, ktspec, qspec, tspec] + ([kspec, kspec] if has_init else [])
    args = (q, kv, kv, kvt, do, tot) + ((dk_in, dv_in) if has_init else ())
    return pl.pallas_call(
        body, name=name, grid=(bsz, npair, nq), in_specs=in_specs,
        out_specs=[qspec, kspec, kspec],
        out_shape=[jax.ShapeDtypeStruct((bsz, s, w), BF16), jax.ShapeDtypeStruct((bsz, s, w), F32),
                   jax.ShapeDtypeStruct((bsz, s, w), F32)],
        compiler_params=_params("parallel", "parallel", "arbitrary"),
    )(*args)


ADAM_BLOCK_BYTES = 1 << 20


def _adamw(w, g, m, v, *, name):
    shape = w.shape
    r, c = shape[-2], shape[-1]
    lead = _size(shape[:-2])
    tr = r
    for cand in range(8, r, 8):
        if r % cand == 0 and cand * c * 4 <= ADAM_BLOCK_BYTES:
            tr = cand
    if r * c * 4 <= ADAM_BLOCK_BYTES:
        tr = r

    def body(w_ref, g_ref, m_ref, v_ref, d_ref, mo_ref, vo_ref):
        gv = g_ref[...]
        mn = ADAM_B1 * m_ref[...] + (1.0 - ADAM_B1) * gv
        vn = ADAM_B2 * v_ref[...] + (1.0 - ADAM_B2) * (gv * gv)
        m_hat = mn / (1.0 - ADAM_B1 ** ADAM_STEP)
        v_hat = vn / (1.0 - ADAM_B2 ** ADAM_STEP)
        d_ref[...] = -ADAM_LR * (m_hat / (jnp.sqrt(v_hat) + ADAM_EPS) + ADAM_WD * w_ref[...])
        mo_ref[...] = mn
        vo_ref[...] = vn

    blk = pl.BlockSpec((1, tr, c), lambda l, i: (l, i, 0))
    outs = pl.pallas_call(
        body, name=name, grid=(lead, r // tr), in_specs=[blk] * 4, out_specs=[blk] * 3,
        out_shape=[jax.ShapeDtypeStruct((lead, r, c), F32)] * 3,
        compiler_params=_params("parallel", "parallel"),
    )(*[a.reshape(lead, r, c) for a in (w, g, m, v)])
    return [o.reshape(shape) for o in outs]


def _add_own_half(idx, g, recv, *, rows=1024):
    _, _, rh, _ = g.shape
    tr = _tile(rh, rows)

    def body(idx_ref, a_ref, b_ref, o_ref):
        o_ref[...] = (a_ref[0].astype(F32) + b_ref[...].astype(F32)).astype(o_ref.dtype)

    return pl.pallas_call(
        body, name="grad_pair_sum",
        grid_spec=pltpu.PrefetchScalarGridSpec(
            num_scalar_prefetch=1, grid=(N_CHIPS, rh // tr),
            in_specs=[pl.BlockSpec((1, 1, tr, LANES), lambda k, i, idx: (k, idx[0], i, 0)),
                      pl.BlockSpec((1, tr, LANES), lambda k, i, idx: (k, i, 0))],
            out_specs=pl.BlockSpec((1, tr, LANES), lambda k, i, idx: (k, i, 0))),
        out_shape=jax.ShapeDtypeStruct((N_CHIPS, rh, LANES), g.dtype),
        compiler_params=_params("parallel", "parallel"),
    )(idx, g, recv)


def _add_chips(idx, own, recv, *, rows=1024):
    _, rh, _ = own.shape
    tr = _tile(rh, rows)

    def body(idx_ref, a_ref, b_ref, o_ref):
        f = lambda v: v.astype(F32)
        o_ref[...] = ((f(a_ref[0]) + f(b_ref[0])) + f(b_ref[1])) + f(b_ref[2])

    return pl.pallas_call(
        body, name="grad_chip_sum",
        grid_spec=pltpu.PrefetchScalarGridSpec(
            num_scalar_prefetch=1, grid=(rh // tr,),
            in_specs=[pl.BlockSpec((1, tr, LANES), lambda i, idx: (idx[0], i, 0)),
                      pl.BlockSpec((3, tr, LANES), lambda i, idx: (0, i, 0))],
            out_specs=pl.BlockSpec((tr, LANES), lambda i, idx: (i, 0))),
        out_shape=jax.ShapeDtypeStruct((rh, LANES), F32),
        compiler_params=_params("parallel"),
    )(idx, own, recv)


def _sum_devices(parts):
    _, r, _ = parts.shape

    def body(p_ref, o_ref):
        acc = p_ref[0]
        for k in range(1, N_DEV):
            acc = acc + p_ref[k]
        o_ref[...] = acc

    return pl.pallas_call(
        body, name="small_grad_sum", grid=(1,),
        in_specs=[pl.BlockSpec((N_DEV, r, LANES), lambda i: (0, 0, 0))],
        out_specs=pl.BlockSpec((r, LANES), lambda i: (0, 0)),
        out_shape=jax.ShapeDtypeStruct((r, LANES), F32),
    )(parts)


def _place():
    return lax.axis_index("x"), lax.axis_index("y"), lax.axis_index("c")


def _rcopy(src, dst, send_sems, recv_sems, k, to):
    return pltpu.make_async_remote_copy(src_ref=src, dst_ref=dst, send_sem=send_sems.at[k],
                                        recv_sem=recv_sems.at[k], device_id=to, device_id_type=MESH)


def _gather_weights(wl):
    _, rh, _ = wl.shape

    def body(w_ref, out_ref, send_sems, recv_sems):
        x, y, c = _place()
        sibling = (x, y, 1 - c)
        chips = [(1 - x, y), (x, 1 - y), (1 - x, 1 - y)]

        def piece(px, py, pc):
            return out_ref.at[2 * px + py, pc]

        first = [_rcopy(w_ref.at[c], piece(x, y, c), send_sems, recv_sems, j, (*chip, c))
                 for j, chip in enumerate(chips)]
        for cp in first:
            cp.start()
        passed = [_rcopy(piece(*chip, c), piece(*chip, c), send_sems, recv_sems, 3 + j, sibling)
                  for j, chip in enumerate(chips)]
        for j, chip in enumerate(chips):
            _rcopy(piece(*chip, c), piece(*chip, c), send_sems, recv_sems, j, (*chip, c)).wait_recv()
            passed[j].start()
        for j, chip in enumerate(chips):
            _rcopy(piece(*chip, 1 - c), piece(*chip, 1 - c), send_sems, recv_sems, 3 + j,
                   sibling).wait_recv()
        for cp in first + passed:
            cp.wait_send()

    return pl.pallas_call(
        body, name="gather_weights", in_specs=[ANY], out_specs=ANY,
        out_shape=jax.ShapeDtypeStruct((N_CHIPS, 2, rh, LANES), wl.dtype),
        scratch_shapes=[pltpu.SemaphoreType.DMA((6,)), pltpu.SemaphoreType.DMA((6,))],
    )(wl)


def _swap_halves(g):
    _, _, rh, _ = g.shape

    def body(g_ref, out_ref, send_sems, recv_sems):
        x, y, c = _place()
        sibling = (x, y, 1 - c)
        cps = [_rcopy(g_ref.at[k, 1 - c], out_ref.at[k], send_sems, recv_sems, k, sibling)
               for k in range(N_CHIPS)]
        for cp in cps:
            cp.start()
        for cp in cps:
            cp.wait()

    return pl.pallas_call(
        body, name="grad_swap_halves", in_specs=[ANY], out_specs=ANY,
        out_shape=jax.ShapeDtypeStruct((N_CHIPS, rh, LANES), g.dtype),
        scratch_shapes=[pltpu.SemaphoreType.DMA((N_CHIPS,)), pltpu.SemaphoreType.DMA((N_CHIPS,))],
    )(g)


def _scatter_chunks(s4):
    _, rh, _ = s4.shape

    def body(s_ref, out_ref, send_sems, recv_sems):
        x, y, c = _place()
        chips = [(1 - x, y), (x, 1 - y), (1 - x, 1 - y)]
        cps = [_rcopy(s_ref.at[2 * chip[0] + chip[1]], out_ref.at[j], send_sems, recv_sems, j,
                      (*chip, c)) for j, chip in enumerate(chips)]
        for cp in cps:
            cp.start()
        for cp in cps:
            cp.wait()

    return pl.pallas_call(
        body, name="grad_scatter_chunks", in_specs=[ANY], out_specs=ANY,
        out_shape=jax.ShapeDtypeStruct((3, rh, LANES), s4.dtype),
        scratch_shapes=[pltpu.SemaphoreType.DMA((3,)), pltpu.SemaphoreType.DMA((3,))],
    )(s4)


def _share_half(tot):
    rh, _ = tot.shape

    def body(t_ref, out_ref, send_sems, recv_sems):
        x, y, c = _place()
        cp = _rcopy(t_ref, out_ref, send_sems, recv_sems, 0, (x, y, 1 - c))
        cp.start()
        cp.wait()

    return pl.pallas_call(
        body, name="grad_share_half", in_specs=[ANY], out_specs=ANY,
        out_shape=jax.ShapeDtypeStruct((rh, LANES), tot.dtype),
        scratch_shapes=[pltpu.SemaphoreType.DMA((1,)), pltpu.SemaphoreType.DMA((1,))],
    )(tot)


def _exchange_small(r):
    rr, _ = r.shape

    def body(r_ref, out_ref, send_sems, recv_sems, local_sem):
        x, y, c = _place()
        me = 4 * x + 2 * y + c
        mine = pltpu.make_async_copy(r_ref, out_ref.at[me], local_sem)
        mine.start()
        cps = []
        for k in range(N_DEV - 1):
            fx, fy, fc = ((k + 1) >> 2) & 1, ((k + 1) >> 1) & 1, (k + 1) & 1
            to = (x ^ fx, y ^ fy, c ^ fc)
            cps.append((_rcopy(r_ref, out_ref.at[me], send_sems, recv_sems, k, to), to))
        for cp, _ in cps:
            cp.start()
        for k, (cp, to) in enumerate(cps):
            src = 4 * to[0] + 2 * to[1] + to[2]
            _rcopy(r_ref, out_ref.at[src], send_sems, recv_sems, k, to).wait_recv()
        for cp, _ in cps:
            cp.wait_send()
        mine.wait()

    return pl.pallas_call(
        body, name="small_grad_exchange", in_specs=[ANY], out_specs=ANY,
        out_shape=jax.ShapeDtypeStruct((N_DEV, rr, LANES), r.dtype),
        scratch_shapes=[pltpu.SemaphoreType.DMA((N_DEV - 1,)), pltpu.SemaphoreType.DMA((N_DEV - 1,)),
                        pltpu.SemaphoreType.DMA],
    )(r)


def _size(shape):
    n = 1
    for d in shape:
        n *= d
    return n


def _slab_rows(shape):
    rows = -(-_size(shape) // LANES)
    return -(-rows // SLAB_ROW_ALIGN) * SLAB_ROW_ALIGN


def _pack_rows(arrs, dtype, lead=0, unit=PACK_ROWS):
    parts, total = [], 0
    for a in arrs:
        front, shp = a.shape[:lead], a.shape[lead:]
        n, rows = _size(shp), _slab_rows(shp)
        nopad = [(0, 0)] * lead
        if n % LANES == 0:
            p = a.reshape(front + (n // LANES, LANES)).astype(dtype)
        else:
            p = jnp.pad(a.reshape(front + (n,)).astype(dtype), nopad + [(0, rows * LANES - n)])
            p = p.reshape(front + (rows, LANES))
        if p.shape[lead] != rows:
            p = jnp.pad(p, nopad + [(0, rows - p.shape[lead]), (0, 0)])
        parts.append(p)
        total += rows
    pad = (-total) % unit
    if pad:
        parts.append(jnp.zeros(parts[0].shape[:lead] + (pad, LANES), dtype))
    return jnp.concatenate(parts, axis=lead)


def _unpack_rows(slab, shapes):
    lead = slab.shape[:-2]
    out, off = [], 0
    for shp in shapes:
        n, rows = _size(shp), _slab_rows(shp)
        piece = slab[..., off:off + rows, :]
        if n % LANES == 0:
            piece = piece[..., :n // LANES, :].reshape(lead + tuple(shp))
        else:
            piece = piece.reshape(lead + (rows * LANES,))[..., :n].reshape(lead + tuple(shp))
        out.append(piece)
        off += rows
    return out


def _ffn_fwd(h, bsz, s, gain, w_up, cw, cb, w_down, i):
    hf = _rmsnorm_fwd(h, gain, name=f"ffn_norm_{i}")
    up = _mm(hf, w_up, name=f"ffn_up_{i}")
    up3 = up.reshape(bsz, s, -1)
    f = _conv_glu_fwd(up3, cw, cb, name=f"ffn_glu_{i}").reshape(h.shape[0], -1)
    h2 = _mm(f, w_down, add=h, name=f"ffn_down_{i}")
    return h2, (h, hf, up3, f)


def _ffn_bwd(dh, saved, gain, w_up, cw, cb, w_down, i):
    h, hf, up3, f = saved
    t = h.shape[0]
    d_down = _mm(f, dh, ta=True, name=f"ffn_down_dw_{i}")
    df = _mm(dh, w_down, tb=True, name=f"ffn_down_dx_{i}")
    dug, duv, dcw, dcb = _conv_glu_bwd(up3, cw, cb, df.reshape(up3.shape[0], up3.shape[1], -1),
                                       name=f"ffn_glu_bwd_{i}")
    dug, duv = dug.reshape(t, -1), duv.reshape(t, -1)
    fdim = dug.shape[1]
    d_up = jnp.concatenate([_mm(hf, dug, ta=True, name=f"ffn_up_dwg_{i}"),
                            _mm(hf, duv, ta=True, name=f"ffn_up_dwv_{i}")], axis=1)
    dhf = _mm(dug, w_up[:, :fdim], tb=True, name=f"ffn_up_dxg_{i}")
    dhf = _mm(duv, w_up[:, fdim:], tb=True, add=dhf, name=f"ffn_up_dxv_{i}")
    dh, dgain = _rmsnorm_bwd(h, gain, dhf, dh, name=f"ffn_norm_bwd_{i}")
    return dh, dgain, d_up, dcw, dcb, d_down


def _ple_layer_fwd(h, p_i, gain, w_gate, w_proj, i):
    hp = _rmsnorm_fwd(h, gain, name=f"ple_norm_{i}")
    a = _mm(hp, w_gate, name=f"ple_gate_{i}")
    pp = _mm(p_i, w_proj, name=f"ple_proj_{i}")
    return _ple_fwd(h, a, pp, name=f"ple_mix_{i}"), (h, hp, a, pp)


def _ple_layer_bwd(dh, saved, p_i, gain, w_gate, i):
    h, hp, a, pp = saved
    da, dpp = _ple_bwd(dh, a, pp, name=f"ple_mix_bwd_{i}")
    d_gate = _mm(hp, da, ta=True, name=f"ple_gate_dw_{i}")
    d_proj = _mm(p_i, dpp, ta=True, name=f"ple_proj_dw_{i}")
    dhp = _mm(da, w_gate, tb=True, name=f"ple_gate_dx_{i}")
    dh, dgain = _rmsnorm_bwd(h, gain, dhp, dh, name=f"ple_norm_bwd_{i}")
    return dh, dgain, d_gate, d_proj


def _ssm_consts(dt_bias, a_log, d_skip, g_n):
    hpg = SSM_HEADS_PER_GROUP
    a = -jnp.exp(a_log)
    rows = jnp.stack([dt_bias.reshape(g_n, hpg), a.reshape(g_n, hpg)], axis=1)
    plane = jnp.zeros((g_n, 8, LANES), F32).at[:, 0:2, 0:hpg].set(rows)
    psub = jnp.zeros((g_n, SSM_ROWS, LANES), F32).at[:, 0:hpg, 0:2].set(jnp.swapaxes(rows, 1, 2))
    d_ch = jnp.repeat(d_skip, SSM_HEAD_DIM).reshape(1, -1)
    return a, plane, psub, d_ch


def _ssm_in_big(w_in, d_inner, g_n):
    d = w_in.shape[0]
    cut = w_in.shape[1] - g_n * SSM_HEADS_PER_GROUP
    wdt = w_in[:, cut:].reshape(d, g_n, SSM_HEADS_PER_GROUP)
    wdt = jnp.pad(wdt, ((0, 0), (0, 0), (0, LANES - SSM_HEADS_PER_GROUP))).reshape(d, g_n * LANES)
    return jnp.concatenate([w_in[:, :cut], wdt], axis=1)


def _ssm_in_small(dw_big, g_n):
    d = dw_big.shape[0]
    cut = dw_big.shape[1] - g_n * LANES
    ddt = dw_big[:, cut:].reshape(d, g_n, LANES)[:, :, :SSM_HEADS_PER_GROUP].reshape(d, -1)
    return jnp.concatenate([dw_big[:, :cut], ddt], axis=1)


def _ssm_fwd(h, bsz, s, gain, w_in_big, cw, cb, plane, psub, d_ch, nw, w_out, i):
    d_inner = d_ch.shape[1]
    g_n = d_inner // SSM_GROUP_W
    conv_dim = cw.shape[1]
    hn = _rmsnorm_fwd(h, gain, name=f"attn_norm_{i}")
    zx = _mm(hn, w_in_big, name=f"ssm_in_{i}").reshape(bsz, s, -1)
    xbc = _conv_silu_fwd(zx, d_inner, cw, cb, name=f"ssm_conv_{i}")
    dtr = zx[:, :, d_inner + conv_dim:].reshape(bsz, s, g_n, LANES)[..., :SSM_HEADS_PER_GROUP]
    dtr_row = jnp.pad(jnp.transpose(dtr, (0, 2, 3, 1)),
                      ((0, 0), (0, 0), (0, SSM_ROWS - SSM_HEADS_PER_GROUP), (0, 0)))
    gn, states = _ssd_fwd(zx, xbc, dtr_row, plane, psub, d_ch, nw, name=f"ssd_{i}")
    gn2 = gn.reshape(h.shape[0], -1)
    h1 = _mm(gn2, w_out, add=h, name=f"ssm_out_{i}")
    return h1, (h, hn, zx, xbc, dtr_row, states, gn2)


def _ssm_bwd(dh, saved, gain, w_in_big, cw, cb, plane, psub, d_ch, nw, w_out, i):
    h, hn, zx, xbc, dtr_row, states, gn2 = saved
    t = h.shape[0]
    bsz, s, _ = zx.shape
    d_inner = d_ch.shape[1]
    d_out = _mm(gn2, dh, ta=True, name=f"ssm_out_dw_{i}")
    dgn = _mm(dh, w_out, tb=True, name=f"ssm_out_dx_{i}").reshape(bsz, s, -1)
    dxs, dbm, dcm, dz, ddtr, ach, aln = _ssd_bwd(zx, xbc, dtr_row, plane, psub, d_ch, nw, states, dgn,
                                                  name=f"ssd_bwd_{i}")
    dxbc, dcw, dcb = _conv_silu_bwd(zx, d_inner, cw, cb, [dxs, dbm, dcm], name=f"ssm_conv_bwd_{i}")
    d_in_parts, dhn, col = [], None, 0
    for tag, part in (("z", dz), ("xbc", dxbc), ("dt", ddtr)):
        part = part.reshape(t, -1)
        d_in_parts.append(_mm(hn, part, ta=True, name=f"ssm_in_dw_{tag}_{i}"))
        dhn = _mm(part, w_in_big[:, col:col + part.shape[1]], tb=True, add=dhn,
                  name=f"ssm_in_dx_{tag}_{i}")
        col += part.shape[1]
    d_in_big = jnp.concatenate(d_in_parts, axis=1)
    dh, dgain = _rmsnorm_bwd(h, gain, dhn, dh, name=f"attn_norm_bwd_{i}")
    hpg = SSM_HEADS_PER_GROUP
    ach = jnp.sum(ach, axis=0)
    aln = jnp.sum(aln, axis=0)
    d_nw = ach[:, 0, :].reshape(-1)
    d_dskip = jnp.sum(ach[:, 1, :].reshape(-1, SSM_HEAD_DIM), axis=1)
    d_bias = aln[:, 0, :hpg].reshape(-1)
    d_a = aln[:, 1, :hpg].reshape(-1)
    return dh, dgain, d_in_big, dcw, dcb, d_bias, d_a, d_dskip, d_nw, d_out


def _sb_layer_fwd(h, bsz, s, gain, w_q, w_o, kv3, kvt, i):
    hn = _rmsnorm_fwd(h, gain, name=f"attn_norm_{i}")
    q3 = _mm(hn, w_q, out_dtype=BF16, name=f"sb_q_{i}").reshape(bsz, s, -1)
    o3, tot = _sb_fwd(q3, kv3, kvt, name=f"sb_attn_{i}")
    o2 = o3.reshape(h.shape[0], -1)
    h1 = _mm(o2, w_o, add=h, name=f"sb_o_{i}")
    return h1, (h, hn, q3, o2, tot)


def _sb_layer_bwd(dh, saved, gain, w_q, w_o, kv3, kvt, dk, dv, i):
    h, hn, q3, o2, tot = saved
    t = h.shape[0]
    d_o = _mm(o2, dh, ta=True, name=f"sb_o_dw_{i}")
    do3 = _mm(dh, w_o, tb=True, out_dtype=BF16, name=f"sb_o_dx_{i}").reshape(q3.shape)
    dq3, dk, dv = _sb_bwd(q3, kv3, kvt, do3, tot, dk, dv, name=f"sb_attn_bwd_{i}")
    dq = dq3.reshape(t, -1)
    d_q = _mm(hn, dq, ta=True, name=f"sb_q_dw_{i}")
    dhn = _mm(dq, w_q, tb=True, name=f"sb_q_dx_{i}")
    dh, dgain = _rmsnorm_bwd(h, gain, dhn, dh, name=f"attn_norm_bwd_{i}")
    return dh, dgain, d_q, d_o, dk, dv


def kernel(x, p, attn_norm, ffn_norm, ple_norm, ssm_in_proj, ssm_conv_w, ssm_conv_b, ssm_dt_bias, ssm_a_log, ssm_d, ssm_norm, ssm_out_proj, kv_norm, w_kv, w_q, w_o, ffn_up, ffn_conv_w, ffn_conv_b, ffn_down, ple_gate, ple_proj, final_norm, loss_target, m_attn_norm, m_ffn_norm, m_ple_norm, m_ssm_in_proj, m_ssm_conv_w, m_ssm_conv_b, m_ssm_dt_bias, m_ssm_a_log, m_ssm_d, m_ssm_norm, m_ssm_out_proj, m_kv_norm, m_w_kv, m_w_q, m_w_o, m_ffn_up, m_ffn_conv_w, m_ffn_conv_b, m_ffn_down, m_ple_gate, m_ple_proj, m_final_norm, v_attn_norm, v_ffn_norm, v_ple_norm, v_ssm_in_proj, v_ssm_conv_w, v_ssm_conv_b, v_ssm_dt_bias, v_ssm_a_log, v_ssm_d, v_ssm_norm, v_ssm_out_proj, v_kv_norm, v_w_kv, v_w_q, v_w_o, v_ffn_up, v_ffn_conv_w, v_ffn_conv_b, v_ffn_down, v_ple_gate, v_ple_proj, v_final_norm):
    given = dict(locals())
    wl = {n: given[n] for n in WEIGHTS}
    bsz, s, d = x.shape
    t = bsz * s
    depth = attn_norm.shape[0]
    n_a = ssm_in_proj.shape[0]
    d_inner = ssm_norm.shape[1] * N_CHIPS
    g_n = d_inner // SSM_GROUP_W
    cidx = lax.axis_index("c").astype(I32).reshape(1)
    chip_idx = (2 * lax.axis_index("x") + lax.axis_index("y")).astype(I32).reshape(1)

    local_shapes = [wl[n].shape for n in SHARDED]
    packed_w = _pack_rows([wl[n] for n in SHARDED], BF16)
    rows = packed_w.shape[0]
    rh = rows // 2
    gathered = _gather_weights(packed_w.reshape(2, rh, LANES)).reshape(N_CHIPS, rows, LANES)
    gathered = lax.dynamic_update_index_in_dim(gathered, packed_w, chip_idx[0], 0)
    full = {}
    for n, piece in zip(SHARDED, _unpack_rows(gathered, local_shapes)):
        ax = SHARD_AXIS[n]
        merged = piece.shape[1:ax + 1] + (N_CHIPS * piece.shape[ax + 1],) + piece.shape[ax + 2:]
        full[n] = jnp.moveaxis(piece, 0, ax).reshape(merged)

    h = x.reshape(t, d)
    tgt = loss_target.reshape(t, d)
    saved = []
    kv3 = kvt = hkv = h_kv_in = None
    consts = []
    for i in range(depth):
        if i < n_a:
            a_neg, plane, psub, d_ch = _ssm_consts(ssm_dt_bias[i], ssm_a_log[i], ssm_d[i], g_n)
            w_in_big = _ssm_in_big(full["ssm_in_proj"][i], d_inner, g_n)
            cw = full["ssm_conv_w"][i].astype(F32)
            cb = full["ssm_conv_b"][i].astype(F32)
            nw = full["ssm_norm"][i].astype(F32).reshape(1, -1)
            consts.append((a_neg, plane, psub, d_ch, w_in_big, cw, cb, nw))
            h, sv_mix = _ssm_fwd(h, bsz, s, attn_norm[i], w_in_big, cw, cb, plane, psub, d_ch, nw,
                                 full["ssm_out_proj"][i], i)
        else:
            j = i - n_a
            h, sv_mix = _sb_layer_fwd(h, bsz, s, attn_norm[i], full["w_q"][j], full["w_o"][j], kv3, kvt, i)
        fcw = full["ffn_conv_w"][i].astype(F32)
        h, sv_ffn = _ffn_fwd(h, bsz, s, ffn_norm[i], full["ffn_up"][i], fcw, ffn_conv_b[i],
                             full["ffn_down"][i], i)
        p_i = p[i].reshape(t, -1)
        h, sv_ple = _ple_layer_fwd(h, p_i, ple_norm[i], full["ple_gate"][i], full["ple_proj"][i], i)
        saved.append((sv_mix, sv_ffn, sv_ple))
        if i == n_a - 1:
            h_kv_in = h
            hkv = _rmsnorm_fwd(h, kv_norm, name="kv_norm")
            kv3 = _mm(hkv, full["w_kv"], out_dtype=BF16, name="kv_proj").reshape(bsz, s, -1)
            kvt = _kv_blocks_t(kv3)

    loss_local, dh, g_final = _final_loss(h, final_norm, tgt)
    gr = {n: [None] * wl[n].shape[0] for n in WEIGHTS if n not in ("kv_norm", "w_kv", "final_norm")}
    gr["final_norm"] = g_final
    dk = dv = None
    for i in reversed(range(depth)):
        sv_mix, sv_ffn, sv_ple = saved[i]
        if i == n_a - 1:
            dkv = jnp.concatenate([dk, dv], axis=-1).reshape(t, -1)
            gr["w_kv"] = _mm(hkv, dkv, ta=True, name="kv_proj_dw")
            dhkv = _mm(dkv, full["w_kv"], tb=True, name="kv_proj_dx")
            dh, gr["kv_norm"] = _rmsnorm_bwd(h_kv_in, kv_norm, dhkv, dh, name="kv_norm_bwd")
        p_i = p[i].reshape(t, -1)
        dh, gr["ple_norm"][i], gr["ple_gate"][i], gr["ple_proj"][i] = _ple_layer_bwd(
            dh, sv_ple, p_i, ple_norm[i], full["ple_gate"][i], i)
        fcw = full["ffn_conv_w"][i].astype(F32)
        (dh, gr["ffn_norm"][i], gr["ffn_up"][i], gr["ffn_conv_w"][i], gr["ffn_conv_b"][i],
         gr["ffn_down"][i]) = _ffn_bwd(dh, sv_ffn, ffn_norm[i], full["ffn_up"][i], fcw, ffn_conv_b[i],
                                       full["ffn_down"][i], i)
        if i < n_a:
            a_neg, plane, psub, d_ch, w_in_big, cw, cb, nw = consts[i]
            (dh, gr["attn_norm"][i], d_in_big, gr["ssm_conv_w"][i], gr["ssm_conv_b"][i],
             gr["ssm_dt_bias"][i], d_a, gr["ssm_d"][i], gr["ssm_norm"][i],
             gr["ssm_out_proj"][i]) = _ssm_bwd(dh, sv_mix, attn_norm[i], w_in_big, cw, cb, plane, psub,
                                               d_ch, nw, full["ssm_out_proj"][i], i)
            gr["ssm_in_proj"][i] = _ssm_in_small(d_in_big, g_n)
            gr["ssm_a_log"][i] = d_a * a_neg
        else:
            j = i - n_a
            dh, gr["attn_norm"][i], gr["w_q"][j], gr["w_o"][j], dk, dv = _sb_layer_bwd(
                dh, sv_mix, attn_norm[i], full["w_q"][j], full["w_o"][j], kv3, kvt, dk, dv, i)
    grad_x = dh.reshape(bsz, s, d)
    gfull = {n: (jnp.stack(v) if isinstance(v, list) else v) for n, v in gr.items()}

    by_chip = []
    for n in SHARDED:
        ax, shp = SHARD_AXIS[n], gfull[n].shape
        split = gfull[n].reshape(shp[:ax] + (N_CHIPS, shp[ax] // N_CHIPS) + shp[ax + 1:])
        by_chip.append(jnp.moveaxis(split, ax, 0))
    g4 = _pack_rows(by_chip, BF16, lead=1).reshape(N_CHIPS, 2, rh, LANES)
    from_sibling = _swap_halves(g4)
    chip_sums = _add_own_half(cidx, g4, from_sibling)
    from_chips = _scatter_chunks(chip_sums)
    my_half = _add_chips(chip_idx, chip_sums, from_chips)
    other_half = _share_half(my_half)
    low_core = cidx[0] == 0
    g_shard = jnp.concatenate([jnp.where(low_core, my_half, other_half),
                               jnp.where(low_core, other_half, my_half)], axis=0)

    rep_shapes = [wl[n].shape for n in REPLICATED]
    packed_r = _pack_rows([gfull[n] for n in REPLICATED], F32, unit=SLAB_ROW_ALIGN)
    g_rep = _sum_devices(_exchange_small(packed_r))

    grads = dict(zip(SHARDED, _unpack_rows(g_shard, local_shapes)))
    delta, new_m, new_v = {}, {}, {}
    for n in SHARDED:
        delta[n], new_m[n], new_v[n] = _adamw(wl[n], grads[n], given["m_" + n], given["v_" + n],
                                              name="adamw_" + n)
    slabs = [_pack_rows([src[pre + n] for n in REPLICATED], F32, unit=SLAB_ROW_ALIGN)
             for src, pre in ((wl, ""), (given, "m_"), (given, "v_"))]
    rep_out = _adamw(slabs[0], g_rep, slabs[1], slabs[2], name="adamw_replicated")
    for dst, slab in zip((grads, delta, new_m, new_v), [g_rep] + list(rep_out)):
        dst.update(zip(REPLICATED, _unpack_rows(slab, rep_shapes)))
    loss = lax.psum(loss_local, ("x", "y", "c"))
    return (loss, grad_x, *[grads[n] for n in WEIGHTS], *[delta[n] for n in WEIGHTS],
            *[new_m[n] for n in WEIGHTS], *[new_v[n] for n in WEIGHTS])
```

```python
import functools

import jax
import jax.numpy as jnp
from jax import lax
from jax.experimental import pallas as pl
from jax.experimental.pallas import tpu as pltpu

F32 = jnp.float32
BF16 = jnp.bfloat16
I32 = jnp.int32

NORM_EPS = 1e-6
SSM_NORM_EPS = 1e-5
SSM_HEAD_DIM = 64
SSM_STATE = 128
SSM_CHUNK = 128
SSM_HEADS_PER_GROUP = 8
SSM_GROUP_W = SSM_HEADS_PER_GROUP * SSM_HEAD_DIM
SSM_CONV = 4
SSM_ROWS = 16
SB_HEAD_DIM = 64
SB_BLOCK = 128
SB_SCALE = SB_HEAD_DIM ** -0.5
SB_Q_BLOCKS_FWD = 4
SB_Q_BLOCKS_BWD = 2
FFN_CONV = 3
LANES = 128
N_CHIPS = 4
N_DEV = 8

ADAM_LR = 0.001
ADAM_B1 = 0.9
ADAM_B2 = 0.999
ADAM_EPS = 1e-08
ADAM_WD = 0.01
ADAM_STEP = 10

MESH = pl.DeviceIdType.MESH
ANY = pl.BlockSpec(memory_space=pl.ANY)

SHARD_AXIS = {
    "ssm_in_proj": 2, "ssm_conv_w": 2, "ssm_conv_b": 1, "ssm_norm": 1, "ssm_out_proj": 1,
    "w_kv": 1, "w_q": 1, "w_o": 1, "ffn_up": 2, "ffn_conv_w": 2, "ffn_down": 1,
    "ple_gate": 1, "ple_proj": 2,
}
REPLICATED = ["attn_norm", "ffn_norm", "ple_norm", "ssm_dt_bias", "ssm_a_log", "ssm_d",
              "kv_norm", "ffn_conv_b", "final_norm"]
WEIGHTS = ["attn_norm", "ffn_norm", "ple_norm", "ssm_in_proj", "ssm_conv_w", "ssm_conv_b",
           "ssm_dt_bias", "ssm_a_log", "ssm_d", "ssm_norm", "ssm_out_proj", "kv_norm", "w_kv",
           "w_q", "w_o", "ffn_up", "ffn_conv_w", "ffn_conv_b", "ffn_down", "ple_gate",
           "ple_proj", "final_norm"]
SHARDED = [n for n in WEIGHTS if n in SHARD_AXIS]
PACK_ROWS = 2048
SLAB_ROW_ALIGN = 16


def _tile(n, pref):
    t = (min(pref, n) // 128) * 128
    while t >= 128:
        if n % t == 0:
            return t
        t -= 128
    return n


def _dot(a, b):
    return jnp.dot(a, b, preferred_element_type=F32)


def _dot_nt(a, b):
    return lax.dot_general(a, b, (((1,), (1,)), ((), ())), preferred_element_type=F32)


def _dot_tn(a, b):
    return lax.dot_general(a, b, (((0,), (0,)), ((), ())), preferred_element_type=F32)


def _split2(x):
    hi = x.astype(BF16)
    lo = (x - hi.astype(F32)).astype(BF16)
    return hi, lo


def _dot2(x, m):
    hi, lo = _split2(x)
    return _dot(hi, m) + _dot(lo, m)


def _dot2_left(m, x):
    hi, lo = _split2(x)
    return _dot(m, hi) + _dot(m, lo)


def _softplus(x):
    return jnp.maximum(x, 0.0) + jnp.log(1.0 + jnp.exp(-jnp.abs(x)))


def _sigmoid(x):
    return jax.nn.sigmoid(x)


def _params(*sem):
    return pltpu.CompilerParams(dimension_semantics=sem)


MM_VMEM_BUDGET = 36 * 1024 * 1024
MM_FULL_K = 2816


def _mm_tiles(m, n, k, sa, sb, so, has_add):
    tk = k if k <= MM_FULL_K else _tile(k, 1024)
    tn = _tile(n, 1408)
    tm = _tile(m, 1408)

    def need(tm_):
        return (2 * tm_ * tk * sa + 2 * tk * tn * sb + tm_ * tn * 4 + 2 * tm_ * tn * so
                + (2 * tm_ * tn * 4 if has_add else 0))

    while need(tm) > MM_VMEM_BUDGET and tm % 256 == 0:
        tm //= 2
    return tm, tn, tk


def _mm(a, b, *, name, ta=False, tb=False, add=None, out_dtype=F32):
    m = a.shape[1] if ta else a.shape[0]
    k = a.shape[0] if ta else a.shape[1]
    n = b.shape[0] if tb else b.shape[1]
    assert (b.shape[1] if tb else b.shape[0]) == k, (a.shape, b.shape, ta, tb)
    tm, tn, tk = _mm_tiles(m, n, k, a.dtype.itemsize, b.dtype.itemsize,
                           jnp.dtype(out_dtype).itemsize, add is not None)
    nk = k // tk
    dims = (((0 if ta else 1,), (1 if tb else 0,)), ((), ()))
    has_add = add is not None

    def body(*refs):
        if has_add:
            a_ref, b_ref, add_ref, o_ref, acc_ref = refs
        else:
            a_ref, b_ref, o_ref, acc_ref = refs
        kk = pl.program_id(2)

        @pl.when(kk == 0)
        def _():
            acc_ref[...] = jnp.zeros_like(acc_ref)

        acc_ref[...] += lax.dot_general(a_ref[...].astype(BF16), b_ref[...].astype(BF16), dims,
                                        preferred_element_type=F32)

        @pl.when(kk == nk - 1)
        def _():
            r = acc_ref[...]
            if has_add:
                r = r + add_ref[...].astype(F32)
            o_ref[...] = r.astype(out_dtype)

    a_spec = (pl.BlockSpec((tk, tm), lambda i, j, kk: (kk, i)) if ta
              else pl.BlockSpec((tm, tk), lambda i, j, kk: (i, kk)))
    b_spec = (pl.BlockSpec((tn, tk), lambda i, j, kk: (j, kk)) if tb
              else pl.BlockSpec((tk, tn), lambda i, j, kk: (kk, j)))
    o_spec = pl.BlockSpec((tm, tn), lambda i, j, kk: (i, j))
    in_specs = [a_spec, b_spec] + ([o_spec] if has_add else [])
    args = (a, b) + ((add,) if has_add else ())
    return pl.pallas_call(
        body, name=name, grid=(m // tm, n // tn, nk), in_specs=in_specs, out_specs=o_spec,
        out_shape=jax.ShapeDtypeStruct((m, n), out_dtype),
        scratch_shapes=[pltpu.VMEM((tm, tn), F32)],
        compiler_params=_params("parallel", "parallel", "arbitrary"),
    )(*args)


def _rmsnorm_fwd(x, gain, *, name, rows=512):
    t, d = x.shape
    tr = _tile(t, rows)

    def body(x_ref, g_ref, o_ref):
        xv = x_ref[...]
        r = lax.rsqrt(jnp.mean(xv * xv, axis=-1, keepdims=True) + NORM_EPS)
        o_ref[...] = ((xv * r) * g_ref[...]).astype(BF16)

    return pl.pallas_call(
        body, name=name, grid=(t // tr,),
        in_specs=[pl.BlockSpec((tr, d), lambda i: (i, 0)), pl.BlockSpec((1, d), lambda i: (0, 0))],
        out_specs=pl.BlockSpec((tr, d), lambda i: (i, 0)),
        out_shape=jax.ShapeDtypeStruct((t, d), BF16),
        compiler_params=_params("parallel"),
    )(x, gain.reshape(1, d))


def _rmsnorm_bwd(x, gain, dy, dres, *, name, rows=512):
    t, d = x.shape
    tr = _tile(t, rows)

    def body(x_ref, g_ref, dy_ref, dres_ref, dx_ref, dg_ref):
        xv = x_ref[...]
        r = lax.rsqrt(jnp.mean(xv * xv, axis=-1, keepdims=True) + NORM_EPS)
        xh = xv * r
        dyv = dy_ref[...].astype(F32)
        dxh = dyv * g_ref[...]
        dx = r * (dxh - xh * jnp.mean(dxh * xh, axis=-1, keepdims=True))
        dx_ref[...] = dres_ref[...] + dx
        part = jnp.sum(dyv * xh, axis=0, keepdims=True)

        @pl.when(pl.program_id(0) == 0)
        def _():
            dg_ref[...] = part

        @pl.when(pl.program_id(0) > 0)
        def _():
            dg_ref[...] += part

    row = pl.BlockSpec((tr, d), lambda i: (i, 0))
    vec = pl.BlockSpec((1, d), lambda i: (0, 0))
    dx, dg = pl.pallas_call(
        body, name=name, grid=(t // tr,), in_specs=[row, vec, row, row], out_specs=[row, vec],
        out_shape=[jax.ShapeDtypeStruct((t, d), F32), jax.ShapeDtypeStruct((1, d), F32)],
        compiler_params=_params("arbitrary"),
    )(x, gain.reshape(1, d), dy, dres)
    return dx, dg.reshape(d)


def _final_loss(h, gain, target, *, rows=512):
    t, d = h.shape
    tr = _tile(t, rows)

    def body(x_ref, g_ref, tg_ref, dx_ref, dg_ref, loss_ref):
        xv = x_ref[...]
        g = g_ref[...]
        r = lax.rsqrt(jnp.mean(xv * xv, axis=-1, keepdims=True) + NORM_EPS)
        xh = xv * r
        err = xh * g - tg_ref[...]
        dyv = err * (1.0 / d)
        dxh = dyv * g
        dx_ref[...] = r * (dxh - xh * jnp.mean(dxh * xh, axis=-1, keepdims=True))
        part = jnp.sum(dyv * xh, axis=0, keepdims=True)
        lpart = jnp.zeros((1, LANES), F32) + (0.5 / d) * jnp.sum(err * err)

        @pl.when(pl.program_id(0) == 0)
        def _():
            dg_ref[...] = part
            loss_ref[...] = lpart

        @pl.when(pl.program_id(0) > 0)
        def _():
            dg_ref[...] += part
            loss_ref[...] += lpart

    row = pl.BlockSpec((tr, d), lambda i: (i, 0))
    vec = pl.BlockSpec((1, d), lambda i: (0, 0))
    dx, dg, loss = pl.pallas_call(
        body, name="final_loss", grid=(t // tr,), in_specs=[row, vec, row],
        out_specs=[row, vec, pl.BlockSpec((1, LANES), lambda i: (0, 0))],
        out_shape=[jax.ShapeDtypeStruct((t, d), F32), jax.ShapeDtypeStruct((1, d), F32),
                   jax.ShapeDtypeStruct((1, LANES), F32)],
        compiler_params=_params("arbitrary"),
    )(h, gain.reshape(1, d), target)
    return loss[0, 0], dx, dg.reshape(d)


def _ple_fwd(h, a, pp, *, name, rows=512):
    t, d = h.shape
    tr = _tile(t, rows)

    def body(h_ref, a_ref, p_ref, o_ref):
        o_ref[...] = h_ref[...] + _sigmoid(a_ref[...]) * p_ref[...]

    row = pl.BlockSpec((tr, d), lambda i: (i, 0))
    return pl.pallas_call(
        body, name=name, grid=(t // tr,), in_specs=[row, row, row], out_specs=row,
        out_shape=jax.ShapeDtypeStruct((t, d), F32), compiler_params=_params("parallel"),
    )(h, a, pp)


def _ple_bwd(dh, a, pp, *, name, rows=512):
    t, d = dh.shape
    tr = _tile(t, rows)

    def body(dh_ref, a_ref, p_ref, da_ref, dp_ref):
        s = _sigmoid(a_ref[...])
        dhv = dh_ref[...]
        da_ref[...] = (dhv * p_ref[...] * (s * (1.0 - s))).astype(BF16)
        dp_ref[...] = (dhv * s).astype(BF16)

    row = pl.BlockSpec((tr, d), lambda i: (i, 0))
    return pl.pallas_call(
        body, name=name, grid=(t // tr,), in_specs=[row, row, row], out_specs=[row, row],
        out_shape=[jax.ShapeDtypeStruct((t, d), BF16)] * 2, compiler_params=_params("parallel"),
    )(dh, a, pp)


def _shift_down(u, j):
    if j == 0:
        return u
    rows = lax.broadcasted_iota(I32, u.shape, 0)
    return jnp.where(rows >= j, pltpu.roll(u, j, 0), 0.0)


def _shift_up(u, j):
    if j == 0:
        return u
    s = u.shape[0]
    rows = lax.broadcasted_iota(I32, u.shape, 0)
    return jnp.where(rows < s - j, pltpu.roll(u, s - j, 0), 0.0)


def _conv_pre(u, wv, bv):
    kw = wv.shape[0]
    shifted = [_shift_down(u, kw - 1 - k) for k in range(kw)]
    pre = bv + wv[0:1, :] * shifted[0]
    for k in range(1, kw):
        pre = pre + wv[k:k + 1, :] * shifted[k]
    return pre, shifted


def _conv_back(dpre, wv, shifted, dw_ref, db_ref, first):
    kw = wv.shape[0]
    du = wv[kw - 1:kw, :] * dpre
    for k in range(kw - 1):
        du = du + wv[k:k + 1, :] * _shift_up(dpre, kw - 1 - k)
    dws = [jnp.sum(dpre * shifted[k], axis=0, keepdims=True) for k in range(kw)]
    dbs = jnp.sum(dpre, axis=0, keepdims=True)

    @pl.when(first)
    def _():
        for k in range(kw):
            dw_ref[k:k + 1, :] = dws[k]
        db_ref[...] = dbs

    @pl.when(jnp.logical_not(first))
    def _():
        for k in range(kw):
            dw_ref[k:k + 1, :] += dws[k]
        db_ref[...] += dbs

    return du


def _dsilu(pre):
    s = _sigmoid(pre)
    return s, s * (1.0 + pre * (1.0 - s))


def _conv_silu_fwd(zx, off, w, b, *, name, tc=128):
    bsz, s, _ = zx.shape
    kw, c = w.shape
    o0 = off // tc

    def body(u_ref, w_ref, b_ref, o_ref):
        pre, _ = _conv_pre(u_ref[0], w_ref[...], b_ref[...])
        o_ref[0] = pre * _sigmoid(pre)

    return pl.pallas_call(
        body, name=name, grid=(bsz, c // tc),
        in_specs=[pl.BlockSpec((1, s, tc), lambda i, j: (i, 0, o0 + j)),
                  pl.BlockSpec((kw, tc), lambda i, j: (0, j)),
                  pl.BlockSpec((1, tc), lambda i, j: (0, j))],
        out_specs=pl.BlockSpec((1, s, tc), lambda i, j: (i, 0, j)),
        out_shape=jax.ShapeDtypeStruct((bsz, s, c), F32),
        compiler_params=_params("parallel", "parallel"),
    )(zx, w, b.reshape(1, c))


def _conv_silu_bwd(zx, off, w, b, douts, *, name, tc=128):
    bsz, s, _ = zx.shape
    kw, c = w.shape
    o0 = off // tc
    counts = [d.shape[2] // tc for d in douts]
    starts = [sum(counts[:k]) for k in range(len(douts))]
    assert sum(counts) == c // tc

    def body(u_ref, w_ref, b_ref, *rest):
        dy_refs = rest[:len(douts)]
        du_ref, dw_ref, db_ref = rest[len(douts):]
        j = pl.program_id(0)
        dy = dy_refs[0][0]
        for k in range(1, len(douts)):
            dy = jnp.where(j >= starts[k], dy_refs[k][0], dy)
        wv = w_ref[...]
        pre, shifted = _conv_pre(u_ref[0], wv, b_ref[...])
        _, ds = _dsilu(pre)
        du = _conv_back(dy * ds, wv, shifted, dw_ref, db_ref, pl.program_id(1) == 0)
        du_ref[0] = du.astype(BF16)

    def part_spec(k):
        return pl.BlockSpec((1, s, tc), lambda j, i: (i, 0, jnp.clip(j - starts[k], 0, counts[k] - 1)))

    du, dw, db = pl.pallas_call(
        body, name=name, grid=(c // tc, bsz),
        in_specs=[pl.BlockSpec((1, s, tc), lambda j, i: (i, 0, o0 + j)),
                  pl.BlockSpec((kw, tc), lambda j, i: (0, j)),
                  pl.BlockSpec((1, tc), lambda j, i: (0, j))] + [part_spec(k) for k in range(len(douts))],
        out_specs=[pl.BlockSpec((1, s, tc), lambda j, i: (i, 0, j)),
                   pl.BlockSpec((kw, tc), lambda j, i: (0, j)),
                   pl.BlockSpec((1, tc), lambda j, i: (0, j))],
        out_shape=[jax.ShapeDtypeStruct((bsz, s, c), BF16), jax.ShapeDtypeStruct((kw, c), F32),
                   jax.ShapeDtypeStruct((1, c), F32)],
        compiler_params=_params("parallel", "arbitrary"),
    )(zx, w, b.reshape(1, c), *douts)
    return du, dw, db.reshape(c)


def _conv_glu_fwd(up, w, b, *, name, tc=128):
    bsz, s, c2 = up.shape
    kw = w.shape[0]
    f = c2 // 2
    nt = f // tc

    def body(ug_ref, uv_ref, wg_ref, wv_ref, bg_ref, bv_ref, o_ref):
        pg, _ = _conv_pre(ug_ref[0], wg_ref[...], bg_ref[...])
        pv, _ = _conv_pre(uv_ref[0], wv_ref[...], bv_ref[...])
        o_ref[0] = (pg * _sigmoid(pg) * pv).astype(BF16)

    b2 = b.reshape(1, c2)
    return pl.pallas_call(
        body, name=name, grid=(bsz, nt),
        in_specs=[pl.BlockSpec((1, s, tc), lambda i, j: (i, 0, j)),
                  pl.BlockSpec((1, s, tc), lambda i, j: (i, 0, nt + j)),
                  pl.BlockSpec((kw, tc), lambda i, j: (0, j)),
                  pl.BlockSpec((kw, tc), lambda i, j: (0, nt + j)),
                  pl.BlockSpec((1, tc), lambda i, j: (0, j)),
                  pl.BlockSpec((1, tc), lambda i, j: (0, nt + j))],
        out_specs=pl.BlockSpec((1, s, tc), lambda i, j: (i, 0, j)),
        out_shape=jax.ShapeDtypeStruct((bsz, s, f), BF16),
        compiler_params=_params("parallel", "parallel"),
    )(up, up, w, w, b2, b2)


def _conv_glu_bwd(up, w, b, df, *, name, tc=128):
    bsz, s, c2 = up.shape
    kw = w.shape[0]
    f = c2 // 2
    nt = f // tc

    def body(ug_ref, uv_ref, wg_ref, wv_ref, bg_ref, bv_ref, df_ref,
             dug_ref, duv_ref, dwg_ref, dwv_ref, dbg_ref, dbv_ref):
        first = pl.program_id(1) == 0
        wg = wg_ref[...]
        wv = wv_ref[...]
        pg, sg = _conv_pre(ug_ref[0], wg, bg_ref[...])
        pv, sv = _conv_pre(uv_ref[0], wv, bv_ref[...])
        sig, dsl = _dsilu(pg)
        dfv = df_ref[0]
        dug_ref[0] = _conv_back(dfv * pv * dsl, wg, sg, dwg_ref, dbg_ref, first).astype(BF16)
        duv_ref[0] = _conv_back(dfv * (pg * sig), wv, sv, dwv_ref, dbv_ref, first).astype(BF16)

    b2 = b.reshape(1, c2)
    act = lambda j, i: (i, 0, j)
    wsp = pl.BlockSpec((kw, tc), lambda j, i: (0, j))
    bsp = pl.BlockSpec((1, tc), lambda j, i: (0, j))
    dug, duv, dwg, dwv, dbg, dbv = pl.pallas_call(
        body, name=name, grid=(nt, bsz),
        in_specs=[pl.BlockSpec((1, s, tc), act),
                  pl.BlockSpec((1, s, tc), lambda j, i: (i, 0, nt + j)),
                  wsp, pl.BlockSpec((kw, tc), lambda j, i: (0, nt + j)),
                  bsp, pl.BlockSpec((1, tc), lambda j, i: (0, nt + j)),
                  pl.BlockSpec((1, s, tc), act)],
        out_specs=[pl.BlockSpec((1, s, tc), act), pl.BlockSpec((1, s, tc), act), wsp, wsp, bsp, bsp],
        out_shape=[jax.ShapeDtypeStruct((bsz, s, f), BF16)] * 2
        + [jax.ShapeDtypeStruct((kw, f), F32)] * 2 + [jax.ShapeDtypeStruct((1, f), F32)] * 2,
        compiler_params=_params("parallel", "arbitrary"),
    )(up, up, w, w, b2, b2, df)
    return (dug, duv, jnp.concatenate([dwg, dwv], axis=1),
            jnp.concatenate([dbg.reshape(f), dbv.reshape(f)]))


def _ssd_shared(xs, bm, cm, dtc_raw, dtr_raw, plane, psub, st):
    cl = SSM_CHUNK
    bias_l, a_l = plane[0:1, :], plane[1:2, :]
    bias_s, a_s = psub[:, 0:1], psub[:, 1:2]
    ri = lax.broadcasted_iota(I32, (cl, cl), 0)
    ci = lax.broadcasted_iota(I32, (cl, cl), 1)
    tril = ri >= ci
    low_incl = tril.astype(BF16)
    up_incl = (ri <= ci).astype(BF16)
    seg_t = (lax.broadcasted_iota(I32, (LANES, SSM_GROUP_W), 0)
             == lax.broadcasted_iota(I32, (LANES, SSM_GROUP_W), 1) // SSM_HEAD_DIM).astype(BF16)
    dt_c = _softplus(dtc_raw + bias_l)
    cs_c = _dot2_left(low_incl, dt_c * a_l)
    dt_r = _softplus(dtr_raw + bias_s)
    cs_r = _dot2(dt_r * a_s, up_incl)
    dt_ch = _dot2(dt_c, seg_t)
    cs_ch = _dot2(cs_c, seg_t)
    cs_last = cs_ch[cl - 1:cl, :]
    decay_ch = jnp.exp(cs_ch)
    w_ch = jnp.exp(cs_last - cs_ch)
    tot_ch = jnp.exp(cs_last)
    xdt = xs * dt_ch
    bm_b, cm_b = bm.astype(BF16), cm.astype(BF16)
    gmat = _dot_nt(cm_b, bm_b)
    cst = _dot(cm_b, st.astype(BF16))
    yoff = decay_ch * cst
    return dict(tril=tril, low_incl=low_incl, up_incl=up_incl, seg_t=seg_t, a_l=a_l, bias_l=bias_l,
                dt_c=dt_c, cs_c=cs_c, cs_r=cs_r, dt_ch=dt_ch, decay_ch=decay_ch, w_ch=w_ch,
                tot_ch=tot_ch, xdt=xdt, bm_b=bm_b, cm_b=cm_b, gmat=gmat, yoff=yoff)


def _head_decay(q, r):
    diff = q["cs_c"][:, r:r + 1] - q["cs_r"][r:r + 1, :]
    return jnp.where(q["tril"], jnp.exp(jnp.minimum(diff, 0.0)), 0.0)


def _half_mask(hh):
    lane = lax.broadcasted_iota(I32, (SSM_CHUNK, LANES), 1)
    return (lane < SSM_HEAD_DIM) if hh == 0 else (lane >= SSM_HEAD_DIM)


def _ssd_ydiag(q):
    pairs = []
    for pr in range(SSM_HEADS_PER_GROUP // 2):
        xp = q["xdt"][:, pr * LANES:(pr + 1) * LANES]
        acc = None
        for hh in range(2):
            mm_ = (q["gmat"] * _head_decay(q, 2 * pr + hh)).astype(BF16)
            part = _dot(mm_, jnp.where(_half_mask(hh), xp, 0.0).astype(BF16))
            acc = part if acc is None else acc + part
        pairs.append(acc)
    return jnp.concatenate(pairs, axis=1)


def _ssd_specs(bsz, s, g_n, d_inner, rev):
    cl = SSM_CHUNK
    nc = s // cl
    cc = (lambda c: nc - 1 - c) if rev else (lambda c: c)
    gb = d_inner // LANES
    dt0 = (d_inner + d_inner + 2 * g_n * SSM_STATE) // LANES
    return dict(
        z=pl.BlockSpec((1, cl, SSM_GROUP_W), lambda b, g, c: (b, cc(c), g)),
        dtc=pl.BlockSpec((1, cl, LANES), lambda b, g, c: (b, cc(c), dt0 + g)),
        xs=pl.BlockSpec((1, cl, SSM_GROUP_W), lambda b, g, c: (b, cc(c), g)),
        bm=pl.BlockSpec((1, cl, LANES), lambda b, g, c: (b, cc(c), gb + g)),
        cm=pl.BlockSpec((1, cl, LANES), lambda b, g, c: (b, cc(c), gb + g_n + g)),
        dtr=pl.BlockSpec((1, 1, SSM_ROWS, cl), lambda b, g, c: (b, g, 0, cc(c))),
        plane=pl.BlockSpec((1, 8, LANES), lambda b, g, c: (g, 0, 0)),
        psub=pl.BlockSpec((1, SSM_ROWS, LANES), lambda b, g, c: (g, 0, 0)),
        chan=pl.BlockSpec((1, SSM_GROUP_W), lambda b, g, c: (0, g)),
        state=pl.BlockSpec((1, 1, 1, SSM_STATE, SSM_GROUP_W), lambda b, g, c: (b, g, cc(c), 0, 0)),
        bgrp=pl.BlockSpec((1, cl, LANES), lambda b, g, c: (b, cc(c), g)),
    )


def _ssd_fwd(zx, xbc, dtr_row, plane, psub, d_ch, nw, *, name):
    bsz, s, _ = zx.shape
    d_inner = d_ch.shape[1]
    g_n = d_inner // SSM_GROUP_W
    nc = s // SSM_CHUNK
    sp = _ssd_specs(bsz, s, g_n, d_inner, False)

    def body(z_ref, dtc_ref, xs_ref, bm_ref, cm_ref, dtr_ref, plane_ref, psub_ref, d_ref, nw_ref,
             gn_ref, st_out_ref, st_ref):
        @pl.when(pl.program_id(2) == 0)
        def _():
            st_ref[...] = jnp.zeros_like(st_ref)

        xs = xs_ref[0]
        st = st_ref[...]
        st_out_ref[0, 0, 0] = st
        q = _ssd_shared(xs, bm_ref[0], cm_ref[0], dtc_ref[0], dtr_ref[0, 0], plane_ref[0],
                        psub_ref[0], st)
        y = _ssd_ydiag(q) + q["yoff"] + xs * d_ref[...]
        st_ref[...] = q["tot_ch"] * st + _dot_tn(q["bm_b"], (q["w_ch"] * q["xdt"]).astype(BF16))
        zv = z_ref[0]
        gy = y * (zv * _sigmoid(zv))
        rstd = lax.rsqrt(jnp.mean(gy * gy, axis=-1, keepdims=True) + SSM_NORM_EPS)
        gn_ref[0] = ((gy * rstd) * nw_ref[...]).astype(BF16)

    return pl.pallas_call(
        body, name=name, grid=(bsz, g_n, nc),
        in_specs=[sp["z"], sp["dtc"], sp["xs"], sp["bm"], sp["cm"], sp["dtr"], sp["plane"],
                  sp["psub"], sp["chan"], sp["chan"]],
        out_specs=[sp["z"], sp["state"]],
        out_shape=[jax.ShapeDtypeStruct((bsz, s, d_inner), BF16),
                   jax.ShapeDtypeStruct((bsz, g_n, nc, SSM_STATE, SSM_GROUP_W), F32)],
        scratch_shapes=[pltpu.VMEM((SSM_STATE, SSM_GROUP_W), F32)],
        compiler_params=_params("parallel", "parallel", "arbitrary"),
    )(zx, zx, xbc, xbc, xbc, dtr_row, plane, psub, d_ch, nw)


def _ssd_bwd(zx, xbc, dtr_row, plane, psub, d_ch, nw, states, dgn, *, name):
    bsz, s, _ = zx.shape
    d_inner = d_ch.shape[1]
    g_n = d_inner // SSM_GROUP_W
    cl = SSM_CHUNK
    nc = s // cl
    sp = _ssd_specs(bsz, s, g_n, d_inner, True)
    acc_ch = pl.BlockSpec((1, 1, 8, SSM_GROUP_W), lambda b, g, c: (b, g, 0, 0))
    acc_ln = pl.BlockSpec((1, 1, 8, LANES), lambda b, g, c: (b, g, 0, 0))

    def body(z_ref, dtc_ref, xs_ref, bm_ref, cm_ref, dtr_ref, plane_ref, psub_ref, d_ref, nw_ref,
             st_in_ref, dgn_ref,
             dxs_ref, dbm_ref, dcm_ref, dz_ref, ddt_ref, ach_ref, aln_ref, dst_ref):
        first = pl.program_id(2) == 0

        @pl.when(first)
        def _():
            dst_ref[...] = jnp.zeros_like(dst_ref)
            ach_ref[...] = jnp.zeros_like(ach_ref)
            aln_ref[...] = jnp.zeros_like(aln_ref)

        xs = xs_ref[0]
        st = st_in_ref[0, 0, 0]
        q = _ssd_shared(xs, bm_ref[0], cm_ref[0], dtc_ref[0], dtr_ref[0, 0], plane_ref[0],
                        psub_ref[0], st)
        d_chv = d_ref[...]
        nwv = nw_ref[...]
        y = _ssd_ydiag(q) + q["yoff"] + xs * d_chv
        zv = z_ref[0]
        sz = _sigmoid(zv)
        silu_z = zv * sz
        gy = y * silu_z
        rstd = lax.rsqrt(jnp.mean(gy * gy, axis=-1, keepdims=True) + SSM_NORM_EPS)
        gyh = gy * rstd
        dgnv = dgn_ref[0]
        dgyh = dgnv * nwv
        dgy = rstd * (dgyh - gyh * jnp.mean(dgyh * gyh, axis=-1, keepdims=True))
        dy = dgy * silu_z
        dz_ref[0] = (dgy * y * (sz * (1.0 + zv * (1.0 - sz)))).astype(BF16)
        ach_ref[0, 0, 0:1, :] += jnp.sum(dgnv * gyh, axis=0, keepdims=True)
        ach_ref[0, 0, 1:2, :] += jnp.sum(dy * xs, axis=0, keepdims=True)
        st_b = st.astype(BF16)
        dyd = (dy * q["decay_ch"]).astype(BF16)
        dcm = _dot_nt(dyd, st_b)
        dstn = dst_ref[...]
        dstn_b = dstn.astype(BF16)
        bds = _dot(q["bm_b"], dstn_b)
        wx = q["w_ch"] * q["xdt"]
        dbm = _dot_nt(wx.astype(BF16), dstn_b)
        dst_ref[...] = q["tot_ch"] * dstn + _dot_tn(q["cm_b"], dyd)
        vterm = wx * bds
        cs_terms = dy * q["yoff"] - vterm
        last_ch = q["tot_ch"] * jnp.sum(dstn * st, axis=0, keepdims=True) + jnp.sum(vterm, axis=0, keepdims=True)
        lane = lax.broadcasted_iota(I32, (cl, LANES), 1)
        rowi = lax.broadcasted_iota(I32, (SSM_ROWS, cl), 0)
        dg_sum = jnp.zeros((cl, cl), F32)
        dcs_col = jnp.zeros((cl, LANES), F32)
        dcs_row = jnp.zeros((SSM_ROWS, cl), F32)
        dxdt_pairs = []
        for pr in range(SSM_HEADS_PER_GROUP // 2):
            xp_b = q["xdt"][:, pr * LANES:(pr + 1) * LANES].astype(BF16)
            dyp = dy[:, pr * LANES:(pr + 1) * LANES]
            acc = None
            for hh in range(2):
                r = 2 * pr + hh
                dm = _head_decay(q, r)
                mmat = q["gmat"] * dm
                dym = jnp.where(_half_mask(hh), dyp, 0.0).astype(BF16)
                dmat = jnp.where(q["tril"], _dot_nt(dym, xp_b), 0.0)
                part = _dot_tn(mmat.astype(BF16), dym)
                acc = part if acc is None else acc + part
                dg_sum = dg_sum + dmat * dm
                e = dmat * mmat
                dcs_col = dcs_col + jnp.where(lane == r, jnp.sum(e, axis=1, keepdims=True), 0.0)
                dcs_row = dcs_row + jnp.where(rowi == r, jnp.sum(e, axis=0, keepdims=True), 0.0)
            dxdt_pairs.append(acc)
        dg_b = dg_sum.astype(BF16)
        dcm_ref[0] = dcm + _dot(dg_b, q["bm_b"])
        dbm_ref[0] = dbm + _dot_tn(dg_b, q["cm_b"])
        dxdt = q["w_ch"] * bds + jnp.concatenate(dxdt_pairs, axis=1)
        dxs_ref[0] = dy * d_chv + dxdt * q["dt_ch"]
        seg = (lax.broadcasted_iota(I32, (SSM_GROUP_W, LANES), 0) // SSM_HEAD_DIM
               == lax.broadcasted_iota(I32, (SSM_GROUP_W, LANES), 1)).astype(BF16)
        row_as_col = jnp.transpose(jnp.concatenate(
            [dcs_row, jnp.zeros((cl - SSM_ROWS, cl), F32)], axis=0))
        dcs = dcs_col - row_as_col + _dot2(cs_terms, seg)
        last = _dot2(jnp.zeros((8, SSM_GROUP_W), F32) + last_ch, seg)[0:1, :]
        da = _dot2_left(q["up_incl"], dcs) + last
        ddt = _dot2(dxdt * xs, seg) + da * q["a_l"]
        ddtr = ddt * _sigmoid(dtc_ref[0] + q["bias_l"])
        ddt_ref[0] = ddtr.astype(BF16)
        aln_ref[0, 0, 0:1, :] += jnp.sum(ddtr, axis=0, keepdims=True)
        aln_ref[0, 0, 1:2, :] += jnp.sum(da * q["dt_c"], axis=0, keepdims=True)

    outs = pl.pallas_call(
        body, name=name, grid=(bsz, g_n, nc),
        in_specs=[sp["z"], sp["dtc"], sp["xs"], sp["bm"], sp["cm"], sp["dtr"], sp["plane"],
                  sp["psub"], sp["chan"], sp["chan"], sp["state"], sp["z"]],
        out_specs=[sp["z"], sp["bgrp"], sp["bgrp"], sp["z"], sp["bgrp"], acc_ch, acc_ln],
        out_shape=[jax.ShapeDtypeStruct((bsz, s, d_inner), F32),
                   jax.ShapeDtypeStruct((bsz, s, g_n * SSM_STATE), F32),
                   jax.ShapeDtypeStruct((bsz, s, g_n * SSM_STATE), F32),
                   jax.ShapeDtypeStruct((bsz, s, d_inner), BF16),
                   jax.ShapeDtypeStruct((bsz, s, g_n * LANES), BF16),
                   jax.ShapeDtypeStruct((bsz, g_n, 8, SSM_GROUP_W), F32),
                   jax.ShapeDtypeStruct((bsz, g_n, 8, LANES), F32)],
        scratch_shapes=[pltpu.VMEM((SSM_STATE, SSM_GROUP_W), F32)],
        compiler_params=_params("parallel", "parallel", "arbitrary"),
    )(zx, zx, xbc, xbc, xbc, dtr_row, plane, psub, d_ch, nw, states, dgn)
    return outs


def _sb_stack(x):
    out = []
    for i in range(x.shape[0] // SB_BLOCK):
        xb = x[i * SB_BLOCK:(i + 1) * SB_BLOCK]
        lane = lax.broadcasted_iota(I32, xb.shape, 1)
        zero = jnp.zeros_like(xb)
        out += [jnp.where(lane < SB_HEAD_DIM, xb, zero), jnp.where(lane >= SB_HEAD_DIM, xb, zero)]
    return jnp.concatenate(out, axis=0)


def _sb_unstack_t(acc_t):
    row = lax.broadcasted_iota(I32, (LANES, SB_BLOCK), 0)
    out = []
    for i in range(acc_t.shape[1] // (2 * SB_BLOCK)):
        a = acc_t[:, 2 * i * SB_BLOCK:(2 * i + 1) * SB_BLOCK]
        b = acc_t[:, (2 * i + 1) * SB_BLOCK:(2 * i + 2) * SB_BLOCK]
        out.append(jnp.transpose(jnp.where(row < SB_HEAD_DIM, a, b)))
    return jnp.concatenate(out, axis=0)


def _sb_tile_blocks(nq, q_blocks):
    nb = 4 if nq % 4 == 0 else (2 if nq % 2 == 0 else 1)
    return nb, min(nb, q_blocks)


def _sb_valid(u, qi0, nb, nqb):
    shape = (nb * SB_BLOCK, nqb * 2 * SB_BLOCK)
    key = u * (nb * SB_BLOCK) + lax.broadcasted_iota(I32, shape, 0)
    col = lax.broadcasted_iota(I32, shape, 1)
    qpos = (qi0 + col // (2 * SB_BLOCK)) * SB_BLOCK + col % SB_BLOCK
    return key < qpos


def _sb_logits(kb, qs, valid):
    z = _dot_nt(kb, qs)
    lb = jnp.minimum(z, 0.0) - jnp.log(1.0 + jnp.exp(-jnp.abs(z)))
    lk_all = lb - z
    lk = lk_all if valid is None else jnp.where(valid, lk_all, 0.0)
    return z, lb, lk_all, lk


def _sb_scan(tri2, x, nb, reverse, exact=True):
    blk = SB_BLOCK
    edge = 0 if reverse else blk - 1
    carry = jnp.zeros((1, x.shape[1]), F32)
    res = [None] * nb
    for i in (reversed(range(nb)) if reverse else range(nb)):
        part = x[i * blk:(i + 1) * blk]
        if exact:
            hi, lo = _split2(part)
            raw = _dot(tri2, jnp.concatenate([hi, lo], axis=0))
        else:
            raw = _dot(tri2[:, :blk], part.astype(BF16))
        res[i] = raw + carry
        carry = carry + (raw[edge:edge + 1] + part[edge:edge + 1])
    return jnp.concatenate(res, axis=0), carry


def _sb_fwd(q, kv, kvt, *, name):
    bsz, s, w = q.shape
    blk = SB_BLOCK
    npair = w // LANES
    nq = s // blk
    nb, nqb = _sb_tile_blocks(nq, SB_Q_BLOCKS_FWD)
    width = nqb * 2 * blk

    def body(q_ref, k_ref, vt_ref, o_ref, tot_ref):
        qi0 = pl.program_id(2) * nqb
        qs = _sb_stack(q_ref[0] * SB_SCALE)
        ri = lax.broadcasted_iota(I32, (blk, blk), 0)
        ci = lax.broadcasted_iota(I32, (blk, blk), 1)
        upper = (ri < ci).astype(BF16)
        tri2 = jnp.concatenate([upper, upper], axis=1)

        def tile(u, r, acc, masked):
            rows = pl.ds(pl.multiple_of(u * (nb * blk), nb * blk), nb * blk)
            valid = _sb_valid(u, qi0, nb, nqb) if masked else None
            _, lb, _, lk = _sb_logits(k_ref[0, rows, :], qs, valid)
            sfx, total = _sb_scan(tri2, lk, nb, True)
            wgt = jnp.exp(lb + sfx + r)
            if masked:
                wgt = jnp.where(valid, wgt, 0.0)
            wb = wgt.astype(BF16)
            for i in range(nb):
                acc = acc + _dot(vt_ref[0, 0, u * nb + i], wb[i * blk:(i + 1) * blk])
            return r + total, acc

        top = qi0 // nb
        r, acc = tile(top, jnp.zeros((1, width), F32), jnp.zeros((LANES, width), F32), True)
        r, acc = lax.fori_loop(0, top, lambda t, c: tile(top - 1 - t, c[0], c[1], False), (r, acc))
        o_ref[0] = _sb_unstack_t(acc).astype(BF16)
        tot_ref[0, 0, 0] = r

    qspec = pl.BlockSpec((1, nqb * blk, LANES), lambda b, p, i: (b, i, p))
    return pl.pallas_call(
        body, name=name, grid=(bsz, npair, nq // nqb),
        in_specs=[qspec,
                  pl.BlockSpec((1, s, LANES), lambda b, p, i: (b, 0, p)),
                  pl.BlockSpec((1, 1, nq, LANES, blk), lambda b, p, i: (b, npair + p, 0, 0, 0))],
        out_specs=[qspec, pl.BlockSpec((1, 1, 1, 1, width), lambda b, p, i: (b, p, i, 0, 0))],
        out_shape=[jax.ShapeDtypeStruct((bsz, s, w), BF16),
                   jax.ShapeDtypeStruct((bsz, npair, nq // nqb, 1, width), F32)],
        compiler_params=_params("parallel", "parallel", "arbitrary"),
    )(q, kv, kvt)


def _kv_blocks_t(kv3):
    bsz, s, w2 = kv3.shape
    x = kv3.reshape(bsz, s // SB_BLOCK, SB_BLOCK, w2 // LANES, LANES)
    return jnp.transpose(x, (0, 3, 1, 4, 2))


def _sb_bwd(q, kv, kvt, do, tot, dk_in, dv_in, *, name):
    bsz, s, w = q.shape
    blk = SB_BLOCK
    npair = w // LANES
    nq = s // blk
    nb, nqb = _sb_tile_blocks(nq, SB_Q_BLOCKS_BWD)
    width = nqb * 2 * blk
    tot = tot.reshape(bsz, npair, nq // nqb, 1, width)
    has_init = dk_in is not None

    def body(*refs):
        if has_init:
            q_ref, k_ref, v_ref, kt_ref, do_ref, tot_ref, dki_ref, dvi_ref, dq_ref, dk_ref, dv_ref = refs
        else:
            q_ref, k_ref, v_ref, kt_ref, do_ref, tot_ref, dq_ref, dk_ref, dv_ref = refs
        qi0 = pl.program_id(2) * nqb

        @pl.when(qi0 == 0)
        def _():
            if has_init:
                dk_ref[...] = dki_ref[...]
                dv_ref[...] = dvi_ref[...]
            else:
                dk_ref[...] = jnp.zeros_like(dk_ref)
                dv_ref[...] = jnp.zeros_like(dv_ref)

        qs = _sb_stack(q_ref[0] * SB_SCALE)
        dos = _sb_stack(do_ref[0])
        totv = tot_ref[0, 0, 0]
        ri = lax.broadcasted_iota(I32, (blk, blk), 0)
        ci = lax.broadcasted_iota(I32, (blk, blk), 1)
        lower = (ri > ci).astype(BF16)
        tri2 = jnp.concatenate([lower, lower], axis=1)

        def tile(u, pre_lk, pre_d, dqt, masked):
            rows = pl.ds(pl.multiple_of(u * (nb * blk), nb * blk), nb * blk)
            valid = _sb_valid(u, qi0, nb, nqb) if masked else None
            z, lb, lk_all, lk = _sb_logits(k_ref[0, rows, :], qs, valid)
            before, tot_lk = _sb_scan(tri2, lk, nb, False)
            wgt = jnp.exp(z + ((totv - pre_lk) - before))
            if masked:
                wgt = jnp.where(valid, wgt, 0.0)
            dlogit = _dot_nt(v_ref[0, rows, :], dos) * wgt
            dbefore, tot_d = _sb_scan(tri2, dlogit, nb, False)
            dz = dlogit * jnp.exp(lk_all) - (pre_d + dbefore) * jnp.exp(lb)
            if masked:
                dz = jnp.where(valid, dz, 0.0)
            dz_b = dz.astype(BF16)
            for i in range(nb):
                dqt = dqt + _dot(kt_ref[0, 0, u * nb + i], dz_b[i * blk:(i + 1) * blk])
            dk_ref[0, rows, :] += _dot(dz_b, qs)
            dv_ref[0, rows, :] += _dot(wgt.astype(BF16), dos)
            return pre_lk + tot_lk, pre_d + tot_d, dqt

        zero = jnp.zeros((1, width), F32)
        top = qi0 // nb
        c = lax.fori_loop(0, top, lambda u, c: tile(u, c[0], c[1], c[2], False),
                          (zero, zero, jnp.zeros((LANES, width), F32)))
        _, _, dqt = tile(top, c[0], c[1], c[2], True)
        dq_ref[0] = (_sb_unstack_t(dqt) * SB_SCALE).astype(BF16)

    qspec = pl.BlockSpec((1, nqb * blk, LANES), lambda b, p, i: (b, i, p))
    kspec = pl.BlockSpec((1, s, LANES), lambda b, p, i: (b, 0, p))
    vspec = pl.BlockSpec((1, s, LANES), lambda b, p, i: (b, 0, npair + p))
    ktspec = pl.BlockSpec((1, 1, nq, LANES, blk), lambda b, p, i: (b, p, 0, 0, 0))
    tspec = pl.BlockSpec((1, 1, 1, 1, width), lambda b, p, i: (b, p, i, 0, 0))
    in_specs = [qspec, kspec, vspec, ktspec, qspec, tspec] + ([kspec, kspec] if has_init else [])
    args = (q, kv, kv, kvt, do, tot) + ((dk_in, dv_in) if has_init else ())
    return pl.pallas_call(
        body, name=name, grid=(bsz, npair, nq // nqb), in_specs=in_specs,
        out_specs=[qspec, kspec, kspec],
        out_shape=[jax.ShapeDtypeStruct((bsz, s, w), BF16), jax.ShapeDtypeStruct((bsz, s, w), F32),
                   jax.ShapeDtypeStruct((bsz, s, w), F32)],
        compiler_params=_params("parallel", "parallel", "arbitrary"),
    )(*args)


ADAM_BLOCK_BYTES = 1 << 20


def _adamw(w, g, m, v, *, name):
    shape = w.shape
    r, c = shape[-2], shape[-1]
    lead = _size(shape[:-2])
    tr = r
    for cand in range(8, r, 8):
        if r % cand == 0 and cand * c * 4 <= ADAM_BLOCK_BYTES:
            tr = cand
    if r * c * 4 <= ADAM_BLOCK_BYTES:
        tr = r

    def body(w_ref, g_ref, m_ref, v_ref, d_ref, mo_ref, vo_ref):
        gv = g_ref[...]
        mn = ADAM_B1 * m_ref[...] + (1.0 - ADAM_B1) * gv
        vn = ADAM_B2 * v_ref[...] + (1.0 - ADAM_B2) * (gv * gv)
        m_hat = mn / (1.0 - ADAM_B1 ** ADAM_STEP)
        v_hat = vn / (1.0 - ADAM_B2 ** ADAM_STEP)
        d_ref[...] = -ADAM_LR * (m_hat / (jnp.sqrt(v_hat) + ADAM_EPS) + ADAM_WD * w_ref[...])
        mo_ref[...] = mn
        vo_ref[...] = vn

    blk = pl.BlockSpec((1, tr, c), lambda l, i: (l, i, 0))
    outs = pl.pallas_call(
        body, name=name, grid=(lead, r // tr), in_specs=[blk] * 4, out_specs=[blk] * 3,
        out_shape=[jax.ShapeDtypeStruct((lead, r, c), F32)] * 3,
        compiler_params=_params("parallel", "parallel"),
    )(*[a.reshape(lead, r, c) for a in (w, g, m, v)])
    return [o.reshape(shape) for o in outs]


def _add_own_half(idx, g, recv, *, rows=1024):
    _, _, rh, _ = g.shape
    tr = _tile(rh, rows)

    def body(idx_ref, a_ref, b_ref, o_ref):
        o_ref[...] = (a_ref[0].astype(F32) + b_ref[...].astype(F32)).astype(o_ref.dtype)

    return pl.pallas_call(
        body, name="grad_pair_sum",
        grid_spec=pltpu.PrefetchScalarGridSpec(
            num_scalar_prefetch=1, grid=(N_CHIPS, rh // tr),
            in_specs=[pl.BlockSpec((1, 1, tr, LANES), lambda k, i, idx: (k, idx[0], i, 0)),
                      pl.BlockSpec((1, tr, LANES), lambda k, i, idx: (k, i, 0))],
            out_specs=pl.BlockSpec((1, tr, LANES), lambda k, i, idx: (k, i, 0))),
        out_shape=jax.ShapeDtypeStruct((N_CHIPS, rh, LANES), g.dtype),
        compiler_params=_params("parallel", "parallel"),
    )(idx, g, recv)


def _add_chips(idx, own, recv, *, rows=1024):
    _, rh, _ = own.shape
    tr = _tile(rh, rows)

    def body(idx_ref, a_ref, b_ref, o_ref):
        f = lambda v: v.astype(F32)
        o_ref[...] = ((f(a_ref[0]) + f(b_ref[0])) + f(b_ref[1])) + f(b_ref[2])

    return pl.pallas_call(
        body, name="grad_chip_sum",
        grid_spec=pltpu.PrefetchScalarGridSpec(
            num_scalar_prefetch=1, grid=(rh // tr,),
            in_specs=[pl.BlockSpec((1, tr, LANES), lambda i, idx: (idx[0], i, 0)),
                      pl.BlockSpec((3, tr, LANES), lambda i, idx: (0, i, 0))],
            out_specs=pl.BlockSpec((tr, LANES), lambda i, idx: (i, 0))),
        out_shape=jax.ShapeDtypeStruct((rh, LANES), F32),
        compiler_params=_params("parallel"),
    )(idx, own, recv)


def _sum_devices(parts):
    _, r, _ = parts.shape

    def body(p_ref, o_ref):
        acc = p_ref[0]
        for k in range(1, N_DEV):
            acc = acc + p_ref[k]
        o_ref[...] = acc

    return pl.pallas_call(
        body, name="small_grad_sum", grid=(1,),
        in_specs=[pl.BlockSpec((N_DEV, r, LANES), lambda i: (0, 0, 0))],
        out_specs=pl.BlockSpec((r, LANES), lambda i: (0, 0)),
        out_shape=jax.ShapeDtypeStruct((r, LANES), F32),
    )(parts)


def _place():
    return lax.axis_index("x"), lax.axis_index("y"), lax.axis_index("c")


def _rcopy(src, dst, send_sems, recv_sems, k, to):
    return pltpu.make_async_remote_copy(src_ref=src, dst_ref=dst, send_sem=send_sems.at[k],
                                        recv_sem=recv_sems.at[k], device_id=to, device_id_type=MESH)


def _gather_weights(wl):
    _, rh, _ = wl.shape

    def body(w_ref, out_ref, send_sems, recv_sems):
        x, y, c = _place()
        sibling = (x, y, 1 - c)
        chips = [(1 - x, y), (x, 1 - y), (1 - x, 1 - y)]

        def piece(px, py, pc):
            return out_ref.at[2 * px + py, pc]

        first = [_rcopy(w_ref.at[c], piece(x, y, c), send_sems, recv_sems, j, (*chip, c))
                 for j, chip in enumerate(chips)]
        for cp in first:
            cp.start()
        passed = [_rcopy(piece(*chip, c), piece(*chip, c), send_sems, recv_sems, 3 + j, sibling)
                  for j, chip in enumerate(chips)]
        for j, chip in enumerate(chips):
            _rcopy(piece(*chip, c), piece(*chip, c), send_sems, recv_sems, j, (*chip, c)).wait_recv()
            passed[j].start()
        for j, chip in enumerate(chips):
            _rcopy(piece(*chip, 1 - c), piece(*chip, 1 - c), send_sems, recv_sems, 3 + j,
                   sibling).wait_recv()
        for cp in first + passed:
            cp.wait_send()

    return pl.pallas_call(
        body, name="gather_weights", in_specs=[ANY], out_specs=ANY,
        out_shape=jax.ShapeDtypeStruct((N_CHIPS, 2, rh, LANES), wl.dtype),
        scratch_shapes=[pltpu.SemaphoreType.DMA((6,)), pltpu.SemaphoreType.DMA((6,))],
    )(wl)


def _swap_halves(g):
    _, _, rh, _ = g.shape

    def body(g_ref, out_ref, send_sems, recv_sems):
        x, y, c = _place()
        sibling = (x, y, 1 - c)
        cps = [_rcopy(g_ref.at[k, 1 - c], out_ref.at[k], send_sems, recv_sems, k, sibling)
               for k in range(N_CHIPS)]
        for cp in cps:
            cp.start()
        for cp in cps:
            cp.wait()

    return pl.pallas_call(
        body, name="grad_swap_halves", in_specs=[ANY], out_specs=ANY,
        out_shape=jax.ShapeDtypeStruct((N_CHIPS, rh, LANES), g.dtype),
        scratch_shapes=[pltpu.SemaphoreType.DMA((N_CHIPS,)), pltpu.SemaphoreType.DMA((N_CHIPS,))],
    )(g)


def _scatter_chunks(s4):
    _, rh, _ = s4.shape

    def body(s_ref, out_ref, send_sems, recv_sems):
        x, y, c = _place()
        chips = [(1 - x, y), (x, 1 - y), (1 - x, 1 - y)]
        cps = [_rcopy(s_ref.at[2 * chip[0] + chip[1]], out_ref.at[j], send_sems, recv_sems, j,
                      (*chip, c)) for j, chip in enumerate(chips)]
        for cp in cps:
            cp.start()
        for cp in cps:
            cp.wait()

    return pl.pallas_call(
        body, name="grad_scatter_chunks", in_specs=[ANY], out_specs=ANY,
        out_shape=jax.ShapeDtypeStruct((3, rh, LANES), s4.dtype),
        scratch_shapes=[pltpu.SemaphoreType.DMA((3,)), pltpu.SemaphoreType.DMA((3,))],
    )(s4)


def _share_half(tot):
    rh, _ = tot.shape

    def body(t_ref, out_ref, send_sems, recv_sems):
        x, y, c = _place()
        cp = _rcopy(t_ref, out_ref, send_sems, recv_sems, 0, (x, y, 1 - c))
        cp.start()
        cp.wait()

    return pl.pallas_call(
        body, name="grad_share_half", in_specs=[ANY], out_specs=ANY,
        out_shape=jax.ShapeDtypeStruct((rh, LANES), tot.dtype),
        scratch_shapes=[pltpu.SemaphoreType.DMA((1,)), pltpu.SemaphoreType.DMA((1,))],
    )(tot)


def _exchange_small(r):
    rr, _ = r.shape

    def body(r_ref, out_ref, send_sems, recv_sems, local_sem):
        x, y, c = _place()
        me = 4 * x + 2 * y + c
        mine = pltpu.make_async_copy(r_ref, out_ref.at[me], local_sem)
        mine.start()
        cps = []
        for k in range(N_DEV - 1):
            fx, fy, fc = ((k + 1) >> 2) & 1, ((k + 1) >> 1) & 1, (k + 1) & 1
            to = (x ^ fx, y ^ fy, c ^ fc)
            cps.append((_rcopy(r_ref, out_ref.at[me], send_sems, recv_sems, k, to), to))
        for cp, _ in cps:
            cp.start()
        for k, (cp, to) in enumerate(cps):
            src = 4 * to[0] + 2 * to[1] + to[2]
            _rcopy(r_ref, out_ref.at[src], send_sems, recv_sems, k, to).wait_recv()
        for cp, _ in cps:
            cp.wait_send()
        mine.wait()

    return pl.pallas_call(
        body, name="small_grad_exchange", in_specs=[ANY], out_specs=ANY,
        out_shape=jax.ShapeDtypeStruct((N_DEV, rr, LANES), r.dtype),
        scratch_shapes=[pltpu.SemaphoreType.DMA((N_DEV - 1,)), pltpu.SemaphoreType.DMA((N_DEV - 1,)),
                        pltpu.SemaphoreType.DMA],
    )(r)


def _size(shape):
    n = 1
    for d in shape:
        n *= d
    return n


def _slab_rows(shape):
    rows = -(-_size(shape) // LANES)
    return -(-rows // SLAB_ROW_ALIGN) * SLAB_ROW_ALIGN


def _pack_rows(arrs, dtype, lead=0, unit=PACK_ROWS):
    parts, total = [], 0
    for a in arrs:
        front, shp = a.shape[:lead], a.shape[lead:]
        n, rows = _size(shp), _slab_rows(shp)
        nopad = [(0, 0)] * lead
        if n % LANES == 0:
            p = a.reshape(front + (n // LANES, LANES)).astype(dtype)
        else:
            p = jnp.pad(a.reshape(front + (n,)).astype(dtype), nopad + [(0, rows * LANES - n)])
            p = p.reshape(front + (rows, LANES))
        if p.shape[lead] != rows:
            p = jnp.pad(p, nopad + [(0, rows - p.shape[lead]), (0, 0)])
        parts.append(p)
        total += rows
    pad = (-total) % unit
    if pad:
        parts.append(jnp.zeros(parts[0].shape[:lead] + (pad, LANES), dtype))
    return jnp.concatenate(parts, axis=lead)


def _unpack_rows(slab, shapes):
    lead = slab.shape[:-2]
    out, off = [], 0
    for shp in shapes:
        n, rows = _size(shp), _slab_rows(shp)
        piece = slab[..., off:off + rows, :]
        if n % LANES == 0:
            piece = piece[..., :n // LANES, :].reshape(lead + tuple(shp))
        else:
            piece = piece.reshape(lead + (rows * LANES,))[..., :n].reshape(lead + tuple(shp))
        out.append(piece)
        off += rows
    return out


def _ffn_fwd(h, bsz, s, gain, w_up, cw, cb, w_down, i):
    hf = _rmsnorm_fwd(h, gain, name=f"ffn_norm_{i}")
    up = _mm(hf, w_up, name=f"ffn_up_{i}")
    up3 = up.reshape(bsz, s, -1)
    f = _conv_glu_fwd(up3, cw, cb, name=f"ffn_glu_{i}").reshape(h.shape[0], -1)
    h2 = _mm(f, w_down, add=h, name=f"ffn_down_{i}")
    return h2, (h, hf, up3, f)


def _ffn_bwd(dh, saved, gain, w_up, cw, cb, w_down, i):
    h, hf, up3, f = saved
    t = h.shape[0]
    d_down = _mm(f, dh, ta=True, name=f"ffn_down_dw_{i}")
    df = _mm(dh, w_down, tb=True, name=f"ffn_down_dx_{i}")
    dug, duv, dcw, dcb = _conv_glu_bwd(up3, cw, cb, df.reshape(up3.shape[0], up3.shape[1], -1),
                                       name=f"ffn_glu_bwd_{i}")
    dug, duv = dug.reshape(t, -1), duv.reshape(t, -1)
    fdim = dug.shape[1]
    d_up = jnp.concatenate([_mm(hf, dug, ta=True, name=f"ffn_up_dwg_{i}"),
                            _mm(hf, duv, ta=True, name=f"ffn_up_dwv_{i}")], axis=1)
    dhf = _mm(dug, w_up[:, :fdim], tb=True, name=f"ffn_up_dxg_{i}")
    dhf = _mm(duv, w_up[:, fdim:], tb=True, add=dhf, name=f"ffn_up_dxv_{i}")
    dh, dgain = _rmsnorm_bwd(h, gain, dhf, dh, name=f"ffn_norm_bwd_{i}")
    return dh, dgain, d_up, dcw, dcb, d_down


def _ple_layer_fwd(h, p_i, gain, w_gate, w_proj, i):
    hp = _rmsnorm_fwd(h, gain, name=f"ple_norm_{i}")
    a = _mm(hp, w_gate, name=f"ple_gate_{i}")
    pp = _mm(p_i, w_proj, name=f"ple_proj_{i}")
    return _ple_fwd(h, a, pp, name=f"ple_mix_{i}"), (h, hp, a, pp)


def _ple_layer_bwd(dh, saved, p_i, gain, w_gate, i):
    h, hp, a, pp = saved
    da, dpp = _ple_bwd(dh, a, pp, name=f"ple_mix_bwd_{i}")
    d_gate = _mm(hp, da, ta=True, name=f"ple_gate_dw_{i}")
    d_proj = _mm(p_i, dpp, ta=True, name=f"ple_proj_dw_{i}")
    dhp = _mm(da, w_gate, tb=True, name=f"ple_gate_dx_{i}")
    dh, dgain = _rmsnorm_bwd(h, gain, dhp, dh, name=f"ple_norm_bwd_{i}")
    return dh, dgain, d_gate, d_proj


def _ssm_consts(dt_bias, a_log, d_skip, g_n):
    hpg = SSM_HEADS_PER_GROUP
    a = -jnp.exp(a_log)
    rows = jnp.stack([dt_bias.reshape(g_n, hpg), a.reshape(g_n, hpg)], axis=1)
    plane = jnp.zeros((g_n, 8, LANES), F32).at[:, 0:2, 0:hpg].set(rows)
    psub = jnp.zeros((g_n, SSM_ROWS, LANES), F32).at[:, 0:hpg, 0:2].set(jnp.swapaxes(rows, 1, 2))
    d_ch = jnp.repeat(d_skip, SSM_HEAD_DIM).reshape(1, -1)
    return a, plane, psub, d_ch


def _ssm_in_big(w_in, d_inner, g_n):
    d = w_in.shape[0]
    cut = w_in.shape[1] - g_n * SSM_HEADS_PER_GROUP
    wdt = w_in[:, cut:].reshape(d, g_n, SSM_HEADS_PER_GROUP)
    wdt = jnp.pad(wdt, ((0, 0), (0, 0), (0, LANES - SSM_HEADS_PER_GROUP))).reshape(d, g_n * LANES)
    return jnp.concatenate([w_in[:, :cut], wdt], axis=1)


def _ssm_in_small(dw_big, g_n):
    d = dw_big.shape[0]
    cut = dw_big.shape[1] - g_n * LANES
    ddt = dw_big[:, cut:].reshape(d, g_n, LANES)[:, :, :SSM_HEADS_PER_GROUP].reshape(d, -1)
    return jnp.concatenate([dw_big[:, :cut], ddt], axis=1)


def _ssm_fwd(h, bsz, s, gain, w_in_big, cw, cb, plane, psub, d_ch, nw, w_out, i):
    d_inner = d_ch.shape[1]
    g_n = d_inner // SSM_GROUP_W
    conv_dim = cw.shape[1]
    hn = _rmsnorm_fwd(h, gain, name=f"attn_norm_{i}")
    zx = _mm(hn, w_in_big, name=f"ssm_in_{i}").reshape(bsz, s, -1)
    xbc = _conv_silu_fwd(zx, d_inner, cw, cb, name=f"ssm_conv_{i}")
    dtr = zx[:, :, d_inner + conv_dim:].reshape(bsz, s, g_n, LANES)[..., :SSM_HEADS_PER_GROUP]
    dtr_row = jnp.pad(jnp.transpose(dtr, (0, 2, 3, 1)),
                      ((0, 0), (0, 0), (0, SSM_ROWS - SSM_HEADS_PER_GROUP), (0, 0)))
    gn, states = _ssd_fwd(zx, xbc, dtr_row, plane, psub, d_ch, nw, name=f"ssd_{i}")
    gn2 = gn.reshape(h.shape[0], -1)
    h1 = _mm(gn2, w_out, add=h, name=f"ssm_out_{i}")
    return h1, (h, hn, zx, xbc, dtr_row, states, gn2)


def _ssm_bwd(dh, saved, gain, w_in_big, cw, cb, plane, psub, d_ch, nw, w_out, i):
    h, hn, zx, xbc, dtr_row, states, gn2 = saved
    t = h.shape[0]
    bsz, s, _ = zx.shape
    d_inner = d_ch.shape[1]
    d_out = _mm(gn2, dh, ta=True, name=f"ssm_out_dw_{i}")
    dgn = _mm(dh, w_out, tb=True, name=f"ssm_out_dx_{i}").reshape(bsz, s, -1)
    dxs, dbm, dcm, dz, ddtr, ach, aln = _ssd_bwd(zx, xbc, dtr_row, plane, psub, d_ch, nw, states, dgn,
                                                  name=f"ssd_bwd_{i}")
    dxbc, dcw, dcb = _conv_silu_bwd(zx, d_inner, cw, cb, [dxs, dbm, dcm], name=f"ssm_conv_bwd_{i}")
    d_in_parts, dhn, col = [], None, 0
    for tag, part in (("z", dz), ("xbc", dxbc), ("dt", ddtr)):
        part = part.reshape(t, -1)
        d_in_parts.append(_mm(hn, part, ta=True, name=f"ssm_in_dw_{tag}_{i}"))
        dhn = _mm(part, w_in_big[:, col:col + part.shape[1]], tb=True, add=dhn,
                  name=f"ssm_in_dx_{tag}_{i}")
        col += part.shape[1]
    d_in_big = jnp.concatenate(d_in_parts, axis=1)
    dh, dgain = _rmsnorm_bwd(h, gain, dhn, dh, name=f"attn_norm_bwd_{i}")
    hpg = SSM_HEADS_PER_GROUP
    ach = jnp.sum(ach, axis=0)
    aln = jnp.sum(aln, axis=0)
    d_nw = ach[:, 0, :].reshape(-1)
    d_dskip = jnp.sum(ach[:, 1, :].reshape(-1, SSM_HEAD_DIM), axis=1)
    d_bias = aln[:, 0, :hpg].reshape(-1)
    d_a = aln[:, 1, :hpg].reshape(-1)
    return dh, dgain, d_in_big, dcw, dcb, d_bias, d_a, d_dskip, d_nw, d_out


def _sb_layer_fwd(h, bsz, s, gain, w_q, w_o, kv3, kvt, i):
    hn = _rmsnorm_fwd(h, gain, name=f"attn_norm_{i}")
    q3 = _mm(hn, w_q, out_dtype=BF16, name=f"sb_q_{i}").reshape(bsz, s, -1)
    o3, tot = _sb_fwd(q3, kv3, kvt, name=f"sb_attn_{i}")
    o2 = o3.reshape(h.shape[0], -1)
    h1 = _mm(o2, w_o, add=h, name=f"sb_o_{i}")
    return h1, (h, hn, q3, o2, tot)


def _sb_layer_bwd(dh, saved, gain, w_q, w_o, kv3, kvt, dk, dv, i):
    h, hn, q3, o2, tot = saved
    t = h.shape[0]
    d_o = _mm(o2, dh, ta=True, name=f"sb_o_dw_{i}")
    do3 = _mm(dh, w_o, tb=True, out_dtype=BF16, name=f"sb_o_dx_{i}").reshape(q3.shape)
    dq3, dk, dv = _sb_bwd(q3, kv3, kvt, do3, tot, dk, dv, name=f"sb_attn_bwd_{i}")
    dq = dq3.reshape(t, -1)
    d_q = _mm(hn, dq, ta=True, name=f"sb_q_dw_{i}")
    dhn = _mm(dq, w_q, tb=True, name=f"sb_q_dx_{i}")
    dh, dgain = _rmsnorm_bwd(h, gain, dhn, dh, name=f"attn_norm_bwd_{i}")
    return dh, dgain, d_q, d_o, dk, dv


def kernel(x, p, attn_norm, ffn_norm, ple_norm, ssm_in_proj, ssm_conv_w, ssm_conv_b, ssm_dt_bias, ssm_a_log, ssm_d, ssm_norm, ssm_out_proj, kv_norm, w_kv, w_q, w_o, ffn_up, ffn_conv_w, ffn_conv_b, ffn_down, ple_gate, ple_proj, final_norm, loss_target, m_attn_norm, m_ffn_norm, m_ple_norm, m_ssm_in_proj, m_ssm_conv_w, m_ssm_conv_b, m_ssm_dt_bias, m_ssm_a_log, m_ssm_d, m_ssm_norm, m_ssm_out_proj, m_kv_norm, m_w_kv, m_w_q, m_w_o, m_ffn_up, m_ffn_conv_w, m_ffn_conv_b, m_ffn_down, m_ple_gate, m_ple_proj, m_final_norm, v_attn_norm, v_ffn_norm, v_ple_norm, v_ssm_in_proj, v_ssm_conv_w, v_ssm_conv_b, v_ssm_dt_bias, v_ssm_a_log, v_ssm_d, v_ssm_norm, v_ssm_out_proj, v_kv_norm, v_w_kv, v_w_q, v_w_o, v_ffn_up, v_ffn_conv_w, v_ffn_conv_b, v_ffn_down, v_ple_gate, v_ple_proj, v_final_norm):
    given = dict(locals())
    wl = {n: given[n] for n in WEIGHTS}
    bsz, s, d = x.shape
    t = bsz * s
    depth = attn_norm.shape[0]
    n_a = ssm_in_proj.shape[0]
    d_inner = ssm_norm.shape[1] * N_CHIPS
    g_n = d_inner // SSM_GROUP_W
    cidx = lax.axis_index("c").astype(I32).reshape(1)
    chip_idx = (2 * lax.axis_index("x") + lax.axis_index("y")).astype(I32).reshape(1)

    local_shapes = [wl[n].shape for n in SHARDED]
    packed_w = _pack_rows([wl[n] for n in SHARDED], BF16)
    rows = packed_w.shape[0]
    rh = rows // 2
    gathered = _gather_weights(packed_w.reshape(2, rh, LANES)).reshape(N_CHIPS, rows, LANES)
    gathered = lax.dynamic_update_index_in_dim(gathered, packed_w, chip_idx[0], 0)
    full = {}
    for n, piece in zip(SHARDED, _unpack_rows(gathered, local_shapes)):
        ax = SHARD_AXIS[n]
        merged = piece.shape[1:ax + 1] + (N_CHIPS * piece.shape[ax + 1],) + piece.shape[ax + 2:]
        full[n] = jnp.moveaxis(piece, 0, ax).reshape(merged)

    h = x.reshape(t, d)
    tgt = loss_target.reshape(t, d)
    saved = []
    kv3 = kvt = hkv = h_kv_in = None
    consts = []
    for i in range(depth):
        if i < n_a:
            a_neg, plane, psub, d_ch = _ssm_consts(ssm_dt_bias[i], ssm_a_log[i], ssm_d[i], g_n)
            w_in_big = _ssm_in_big(full["ssm_in_proj"][i], d_inner, g_n)
            cw = full["ssm_conv_w"][i].astype(F32)
            cb = full["ssm_conv_b"][i].astype(F32)
            nw = full["ssm_norm"][i].astype(F32).reshape(1, -1)
            consts.append((a_neg, plane, psub, d_ch, w_in_big, cw, cb, nw))
            h, sv_mix = _ssm_fwd(h, bsz, s, attn_norm[i], w_in_big, cw, cb, plane, psub, d_ch, nw,
                                 full["ssm_out_proj"][i], i)
        else:
            j = i - n_a
            h, sv_mix = _sb_layer_fwd(h, bsz, s, attn_norm[i], full["w_q"][j], full["w_o"][j], kv3, kvt, i)
        fcw = full["ffn_conv_w"][i].astype(F32)
        h, sv_ffn = _ffn_fwd(h, bsz, s, ffn_norm[i], full["ffn_up"][i], fcw, ffn_conv_b[i],
                             full["ffn_down"][i], i)
        p_i = p[i].reshape(t, -1)
        h, sv_ple = _ple_layer_fwd(h, p_i, ple_norm[i], full["ple_gate"][i], full["ple_proj"][i], i)
        saved.append((sv_mix, sv_ffn, sv_ple))
        if i == n_a - 1:
            h_kv_in = h
            hkv = _rmsnorm_fwd(h, kv_norm, name="kv_norm")
            kv3 = _mm(hkv, full["w_kv"], out_dtype=BF16, name="kv_proj").reshape(bsz, s, -1)
            kvt = _kv_blocks_t(kv3)

    loss_local, dh, g_final = _final_loss(h, final_norm, tgt)
    gr = {n: [None] * wl[n].shape[0] for n in WEIGHTS if n not in ("kv_norm", "w_kv", "final_norm")}
    gr["final_norm"] = g_final
    dk = dv = None
    for i in reversed(range(depth)):
        sv_mix, sv_ffn, sv_ple = saved[i]
        if i == n_a - 1:
            dkv = jnp.concatenate([dk, dv], axis=-1).reshape(t, -1)
            gr["w_kv"] = _mm(hkv, dkv, ta=True, name="kv_proj_dw")
            dhkv = _mm(dkv, full["w_kv"], tb=True, name="kv_proj_dx")
            dh, gr["kv_norm"] = _rmsnorm_bwd(h_kv_in, kv_norm, dhkv, dh, name="kv_norm_bwd")
        p_i = p[i].reshape(t, -1)
        dh, gr["ple_norm"][i], gr["ple_gate"][i], gr["ple_proj"][i] = _ple_layer_bwd(
            dh, sv_ple, p_i, ple_norm[i], full["ple_gate"][i], i)
        fcw = full["ffn_conv_w"][i].astype(F32)
        (dh, gr["ffn_norm"][i], gr["ffn_up"][i], gr["ffn_conv_w"][i], gr["ffn_conv_b"][i],
         gr["ffn_down"][i]) = _ffn_bwd(dh, sv_ffn, ffn_norm[i], full["ffn_up"][i], fcw, ffn_conv_b[i],
                                       full["ffn_down"][i], i)
        if i < n_a:
            a_neg, plane, psub, d_ch, w_in_big, cw, cb, nw = consts[i]
            (dh, gr["attn_norm"][i], d_in_big, gr["ssm_conv_w"][i], gr["ssm_conv_b"][i],
             gr["ssm_dt_bias"][i], d_a, gr["ssm_d"][i], gr["ssm_norm"][i],
             gr["ssm_out_proj"][i]) = _ssm_bwd(dh, sv_mix, attn_norm[i], w_in_big, cw, cb, plane, psub,
                                               d_ch, nw, full["ssm_out_proj"][i], i)
            gr["ssm_in_proj"][i] = _ssm_in_small(d_in_big, g_n)
            gr["ssm_a_log"][i] = d_a * a_neg
        else:
            j = i - n_a
            dh, gr["attn_norm"][i], gr["w_q"][j], gr["w_o"][j], dk, dv = _sb_layer_bwd(
                dh, sv_mix, attn_norm[i], full["w_q"][j], full["w_o"][j], kv3, kvt, dk, dv, i)
    grad_x = dh.reshape(bsz, s, d)
    gfull = {n: (jnp.stack(v) if isinstance(v, list) else v) for n, v in gr.items()}

    by_chip = []
    for n in SHARDED:
        ax, shp = SHARD_AXIS[n], gfull[n].shape
        split = gfull[n].reshape(shp[:ax] + (N_CHIPS, shp[ax] // N_CHIPS) + shp[ax + 1:])
        by_chip.append(jnp.moveaxis(split, ax, 0))
    g4 = _pack_rows(by_chip, BF16, lead=1).reshape(N_CHIPS, 2, rh, LANES)
    from_sibling = _swap_halves(g4)
    chip_sums = _add_own_half(cidx, g4, from_sibling)
    from_chips = _scatter_chunks(chip_sums)
    my_half = _add_chips(chip_idx, chip_sums, from_chips)
    other_half = _share_half(my_half)
    low_core = cidx[0] == 0
    g_shard = jnp.concatenate([jnp.where(low_core, my_half, other_half),
                               jnp.where(low_core, other_half, my_half)], axis=0)

    rep_shapes = [wl[n].shape for n in REPLICATED]
    packed_r = _pack_rows([gfull[n] for n in REPLICATED], F32, unit=SLAB_ROW_ALIGN)
    g_rep = _sum_devices(_exchange_small(packed_r))

    grads = dict(zip(SHARDED, _unpack_rows(g_shard, local_shapes)))
    delta, new_m, new_v = {}, {}, {}
    for n in SHARDED:
        delta[n], new_m[n], new_v[n] = _adamw(wl[n], grads[n], given["m_" + n], given["v_" + n],
                                              name="adamw_" + n)
    slabs = [_pack_rows([src[pre + n] for n in REPLICATED], F32, unit=SLAB_ROW_ALIGN)
             for src, pre in ((wl, ""), (given, "m_"), (given, "v_"))]
    rep_out = _adamw(slabs[0], g_rep, slabs[1], slabs[2], name="adamw_replicated")
    for dst, slab in zip((grads, delta, new_m, new_v), [g_rep] + list(rep_out)):
        dst.update(zip(REPLICATED, _unpack_rows(slab, rep_shapes)))
    loss = lax.psum(loss_local, ("x", "y", "c"))
    return (loss, grad_x, *[grads[n] for n in WEIGHTS], *[delta[n] for n in WEIGHTS],
            *[new_m[n] for n in WEIGHTS], *[new_v[n] for n in WEIGHTS])
```

```python
import functools

import jax
import jax.numpy as jnp
from jax import lax
from jax.experimental import pallas as pl
from jax.experimental.pallas import tpu as pltpu

F32 = jnp.float32
BF16 = jnp.bfloat16
I32 = jnp.int32

NORM_EPS = 1e-6
SSM_NORM_EPS = 1e-5
SSM_HEAD_DIM = 64
SSM_STATE = 128
SSM_CHUNK = 128
SSM_HEADS_PER_GROUP = 8
SSM_GROUP_W = SSM_HEADS_PER_GROUP * SSM_HEAD_DIM
SSM_CONV = 4
SSM_ROWS = 16
SB_HEAD_DIM = 64
SB_BLOCK = 128
SB_SCALE = SB_HEAD_DIM ** -0.5
SB_Q_BLOCKS_FWD = 4
SB_Q_BLOCKS_BWD = 2
FFN_CONV = 3
LANES = 128
N_CHIPS = 4
N_DEV = 8

ADAM_LR = 0.001
ADAM_B1 = 0.9
ADAM_B2 = 0.999
ADAM_EPS = 1e-08
ADAM_WD = 0.01
ADAM_STEP = 10

MESH = pl.DeviceIdType.MESH
ANY = pl.BlockSpec(memory_space=pl.ANY)

SHARD_AXIS = {
    "ssm_in_proj": 2, "ssm_conv_w": 2, "ssm_conv_b": 1, "ssm_norm": 1, "ssm_out_proj": 1,
    "w_kv": 1, "w_q": 1, "w_o": 1, "ffn_up": 2, "ffn_conv_w": 2, "ffn_down": 1,
    "ple_gate": 1, "ple_proj": 2,
}
REPLICATED = ["attn_norm", "ffn_norm", "ple_norm", "ssm_dt_bias", "ssm_a_log", "ssm_d",
              "kv_norm", "ffn_conv_b", "final_norm"]
WEIGHTS = ["attn_norm", "ffn_norm", "ple_norm", "ssm_in_proj", "ssm_conv_w", "ssm_conv_b",
           "ssm_dt_bias", "ssm_a_log", "ssm_d", "ssm_norm", "ssm_out_proj", "kv_norm", "w_kv",
           "w_q", "w_o", "ffn_up", "ffn_conv_w", "ffn_conv_b", "ffn_down", "ple_gate",
           "ple_proj", "final_norm"]
SHARDED = [n for n in WEIGHTS if n in SHARD_AXIS]
PACK_ROWS = 2048
SLAB_ROW_ALIGN = 16


def _tile(n, pref):
    t = (min(pref, n) // 128) * 128
    while t >= 128:
        if n % t == 0:
            return t
        t -= 128
    return n


def _dot(a, b):
    return jnp.dot(a, b, preferred_element_type=F32)


def _dot_nt(a, b):
    return lax.dot_general(a, b, (((1,), (1,)), ((), ())), preferred_element_type=F32)


def _dot_tn(a, b):
    return lax.dot_general(a, b, (((0,), (0,)), ((), ())), preferred_element_type=F32)


def _split2(x):
    hi = x.astype(BF16)
    lo = (x - hi.astype(F32)).astype(BF16)
    return hi, lo


def _dot2(x, m):
    hi, lo = _split2(x)
    return _dot(hi, m) + _dot(lo, m)


def _dot2_left(m, x):
    hi, lo = _split2(x)
    return _dot(m, hi) + _dot(m, lo)


def _softplus(x):
    return jnp.maximum(x, 0.0) + jnp.log(1.0 + jnp.exp(-jnp.abs(x)))


def _sigmoid(x):
    return 0.5 * jnp.tanh(0.5 * x) + 0.5


def _params(*sem):
    return pltpu.CompilerParams(dimension_semantics=sem)


MM_VMEM_BUDGET = 36 * 1024 * 1024
MM_FULL_K = 2816


def _mm_tiles(m, n, k, sa, sb, so, has_add):
    tk = k if k <= MM_FULL_K else _tile(k, 1024)
    tn = _tile(n, 1408)
    tm = _tile(m, 1408)

    def need(tm_):
        return (2 * tm_ * tk * sa + 2 * tk * tn * sb + tm_ * tn * 4 + 2 * tm_ * tn * so
                + (2 * tm_ * tn * 4 if has_add else 0))

    while need(tm) > MM_VMEM_BUDGET and tm % 256 == 0:
        tm //= 2
    return tm, tn, tk


def _mm(a, b, *, name, ta=False, tb=False, add=None, out_dtype=F32):
    m = a.shape[1] if ta else a.shape[0]
    k = a.shape[0] if ta else a.shape[1]
    n = b.shape[0] if tb else b.shape[1]
    assert (b.shape[1] if tb else b.shape[0]) == k, (a.shape, b.shape, ta, tb)
    tm, tn, tk = _mm_tiles(m, n, k, a.dtype.itemsize, b.dtype.itemsize,
                           jnp.dtype(out_dtype).itemsize, add is not None)
    nk = k // tk
    dims = (((0 if ta else 1,), (1 if tb else 0,)), ((), ()))
    has_add = add is not None

    def body(*refs):
        if has_add:
            a_ref, b_ref, add_ref, o_ref, acc_ref = refs
        else:
            a_ref, b_ref, o_ref, acc_ref = refs
        kk = pl.program_id(2)

        @pl.when(kk == 0)
        def _():
            acc_ref[...] = jnp.zeros_like(acc_ref)

        acc_ref[...] += lax.dot_general(a_ref[...].astype(BF16), b_ref[...].astype(BF16), dims,
                                        preferred_element_type=F32)

        @pl.when(kk == nk - 1)
        def _():
            r = acc_ref[...]
            if has_add:
                r = r + add_ref[...].astype(F32)
            o_ref[...] = r.astype(out_dtype)

    a_spec = (pl.BlockSpec((tk, tm), lambda i, j, kk: (kk, i)) if ta
              else pl.BlockSpec((tm, tk), lambda i, j, kk: (i, kk)))
    b_spec = (pl.BlockSpec((tn, tk), lambda i, j, kk: (j, kk)) if tb
              else pl.BlockSpec((tk, tn), lambda i, j, kk: (kk, j)))
    o_spec = pl.BlockSpec((tm, tn), lambda i, j, kk: (i, j))
    in_specs = [a_spec, b_spec] + ([o_spec] if has_add else [])
    args = (a, b) + ((add,) if has_add else ())
    return pl.pallas_call(
        body, name=name, grid=(m // tm, n // tn, nk), in_specs=in_specs, out_specs=o_spec,
        out_shape=jax.ShapeDtypeStruct((m, n), out_dtype),
        scratch_shapes=[pltpu.VMEM((tm, tn), F32)],
        compiler_params=_params("parallel", "parallel", "arbitrary"),
    )(*args)


def _rmsnorm_fwd(x, gain, *, name, rows=512):
    t, d = x.shape
    tr = _tile(t, rows)

    def body(x_ref, g_ref, o_ref):
        xv = x_ref[...]
        r = lax.rsqrt(jnp.mean(xv * xv, axis=-1, keepdims=True) + NORM_EPS)
        o_ref[...] = ((xv * r) * g_ref[...]).astype(BF16)

    return pl.pallas_call(
        body, name=name, grid=(t // tr,),
        in_specs=[pl.BlockSpec((tr, d), lambda i: (i, 0)), pl.BlockSpec((1, d), lambda i: (0, 0))],
        out_specs=pl.BlockSpec((tr, d), lambda i: (i, 0)),
        out_shape=jax.ShapeDtypeStruct((t, d), BF16),
        compiler_params=_params("parallel"),
    )(x, gain.reshape(1, d))


def _rmsnorm_bwd(x, gain, dy, dres, *, name, rows=512):
    t, d = x.shape
    tr = _tile(t, rows)

    def body(x_ref, g_ref, dy_ref, dres_ref, dx_ref, dg_ref):
        xv = x_ref[...]
        r = lax.rsqrt(jnp.mean(xv * xv, axis=-1, keepdims=True) + NORM_EPS)
        xh = xv * r
        dyv = dy_ref[...].astype(F32)
        dxh = dyv * g_ref[...]
        dx = r * (dxh - xh * jnp.mean(dxh * xh, axis=-1, keepdims=True))
        dx_ref[...] = dres_ref[...] + dx
        part = jnp.sum(dyv * xh, axis=0, keepdims=True)

        @pl.when(pl.program_id(0) == 0)
        def _():
            dg_ref[...] = part

        @pl.when(pl.program_id(0) > 0)
        def _():
            dg_ref[...] += part

    row = pl.BlockSpec((tr, d), lambda i: (i, 0))
    vec = pl.BlockSpec((1, d), lambda i: (0, 0))
    dx, dg = pl.pallas_call(
        body, name=name, grid=(t // tr,), in_specs=[row, vec, row, row], out_specs=[row, vec],
        out_shape=[jax.ShapeDtypeStruct((t, d), F32), jax.ShapeDtypeStruct((1, d), F32)],
        compiler_params=_params("arbitrary"),
    )(x, gain.reshape(1, d), dy, dres)
    return dx, dg.reshape(d)


def _final_loss(h, gain, target, *, rows=512):
    t, d = h.shape
    tr = _tile(t, rows)

    def body(x_ref, g_ref, tg_ref, dx_ref, dg_ref, loss_ref):
        xv = x_ref[...]
        g = g_ref[...]
        r = lax.rsqrt(jnp.mean(xv * xv, axis=-1, keepdims=True) + NORM_EPS)
        xh = xv * r
        err = xh * g - tg_ref[...]
        dyv = err * (1.0 / d)
        dxh = dyv * g
        dx_ref[...] = r * (dxh - xh * jnp.mean(dxh * xh, axis=-1, keepdims=True))
        part = jnp.sum(dyv * xh, axis=0, keepdims=True)
        lpart = jnp.zeros((1, LANES), F32) + (0.5 / d) * jnp.sum(err * err)

        @pl.when(pl.program_id(0) == 0)
        def _():
            dg_ref[...] = part
            loss_ref[...] = lpart

        @pl.when(pl.program_id(0) > 0)
        def _():
            dg_ref[...] += part
            loss_ref[...] += lpart

    row = pl.BlockSpec((tr, d), lambda i: (i, 0))
    vec = pl.BlockSpec((1, d), lambda i: (0, 0))
    dx, dg, loss = pl.pallas_call(
        body, name="final_loss", grid=(t // tr,), in_specs=[row, vec, row],
        out_specs=[row, vec, pl.BlockSpec((1, LANES), lambda i: (0, 0))],
        out_shape=[jax.ShapeDtypeStruct((t, d), F32), jax.ShapeDtypeStruct((1, d), F32),
                   jax.ShapeDtypeStruct((1, LANES), F32)],
        compiler_params=_params("arbitrary"),
    )(h, gain.reshape(1, d), target)
    return loss[0, 0], dx, dg.reshape(d)


def _ple_fwd(h, a, pp, *, name, rows=512):
    t, d = h.shape
    tr = _tile(t, rows)

    def body(h_ref, a_ref, p_ref, o_ref):
        o_ref[...] = h_ref[...] + _sigmoid(a_ref[...]) * p_ref[...]

    row = pl.BlockSpec((tr, d), lambda i: (i, 0))
    return pl.pallas_call(
        body, name=name, grid=(t // tr,), in_specs=[row, row, row], out_specs=row,
        out_shape=jax.ShapeDtypeStruct((t, d), F32), compiler_params=_params("parallel"),
    )(h, a, pp)


def _ple_bwd(dh, a, pp, *, name, rows=512):
    t, d = dh.shape
    tr = _tile(t, rows)

    def body(dh_ref, a_ref, p_ref, da_ref, dp_ref):
        s = _sigmoid(a_ref[...])
        dhv = dh_ref[...]
        da_ref[...] = (dhv * p_ref[...] * (s * (1.0 - s))).astype(BF16)
        dp_ref[...] = (dhv * s).astype(BF16)

    row = pl.BlockSpec((tr, d), lambda i: (i, 0))
    return pl.pallas_call(
        body, name=name, grid=(t // tr,), in_specs=[row, row, row], out_specs=[row, row],
        out_shape=[jax.ShapeDtypeStruct((t, d), BF16)] * 2, compiler_params=_params("parallel"),
    )(dh, a, pp)


CONV_ROWS = 64
CONV_HALO = 8


def _conv_window(ref, r0, with_prev, with_next):
    s = ref.shape[1]
    parts = []
    if with_prev:
        prev = ref[0, pl.ds(pl.multiple_of(jnp.maximum(r0 - CONV_HALO, 0), CONV_HALO), CONV_HALO), :]
        parts.append(jnp.where(r0 > 0, prev, 0.0))
    parts.append(ref[0, pl.ds(r0, CONV_ROWS), :])
    if with_next:
        nxt = pl.multiple_of(jnp.minimum(r0 + CONV_ROWS, s - CONV_HALO), CONV_HALO)
        parts.append(ref[0, pl.ds(nxt, CONV_HALO), :])
    return jnp.concatenate(parts, axis=0)


def _conv_taps(win, kw, n):
    return [win[CONV_HALO - (kw - 1 - k):CONV_HALO - (kw - 1 - k) + n] for k in range(kw)]


def _conv_apply(taps, wv, bv):
    pre = bv + wv[0:1, :] * taps[0]
    for k in range(1, len(taps)):
        pre = pre + wv[k:k + 1, :] * taps[k]
    return pre


def _rows8(x):
    acc = x[0:8]
    for i in range(1, x.shape[0] // 8):
        acc = acc + x[8 * i:8 * i + 8]
    return acc


def _conv_grad_step(dpre_ext, taps, wv, is_last):
    kw = wv.shape[0]
    halo = jnp.where(is_last, 0.0, dpre_ext[CONV_ROWS:])
    dpre_ext = jnp.concatenate([dpre_ext[:CONV_ROWS], halo], axis=0)
    dpre = dpre_ext[:CONV_ROWS]
    du = wv[kw - 1:kw, :] * dpre
    for k in range(kw - 1):
        du = du + wv[k:k + 1, :] * dpre_ext[kw - 1 - k:kw - 1 - k + CONV_ROWS]
    sums = [_rows8(dpre * taps[k][:CONV_ROWS]) for k in range(kw)] + [_rows8(dpre)]
    return du, sums


def _conv_store_sums(sums, dw_ref, db_ref, first):
    kw = len(sums) - 1
    vals = [jnp.sum(s_, axis=0, keepdims=True) for s_ in sums]

    @pl.when(first)
    def _():
        for k in range(kw):
            dw_ref[k:k + 1, :] = vals[k]
        db_ref[...] = vals[kw]

    @pl.when(jnp.logical_not(first))
    def _():
        for k in range(kw):
            dw_ref[k:k + 1, :] += vals[k]
        db_ref[...] += vals[kw]


def _dsilu(pre):
    s = _sigmoid(pre)
    return s, s * (1.0 + pre * (1.0 - s))


def _conv_silu_fwd(zx, off, w, b, *, name, tc=128):
    bsz, s, _ = zx.shape
    kw, c = w.shape
    o0 = off // tc

    def body(u_ref, w_ref, b_ref, o_ref):
        wv, bv = w_ref[...], b_ref[...]

        def step(i, carry):
            r0 = pl.multiple_of(i * CONV_ROWS, CONV_ROWS)
            taps = _conv_taps(_conv_window(u_ref, r0, True, False), kw, CONV_ROWS)
            pre = _conv_apply(taps, wv, bv)
            o_ref[0, pl.ds(r0, CONV_ROWS), :] = pre * _sigmoid(pre)
            return carry

        lax.fori_loop(0, s // CONV_ROWS, step, 0)

    return pl.pallas_call(
        body, name=name, grid=(bsz, c // tc),
        in_specs=[pl.BlockSpec((1, s, tc), lambda i, j: (i, 0, o0 + j)),
                  pl.BlockSpec((kw, tc), lambda i, j: (0, j)),
                  pl.BlockSpec((1, tc), lambda i, j: (0, j))],
        out_specs=pl.BlockSpec((1, s, tc), lambda i, j: (i, 0, j)),
        out_shape=jax.ShapeDtypeStruct((bsz, s, c), F32),
        compiler_params=_params("parallel", "parallel"),
    )(zx, w, b.reshape(1, c))


def _conv_silu_bwd(zx, off, w, b, douts, *, name, tc=128):
    bsz, s, _ = zx.shape
    kw, c = w.shape
    o0 = off // tc
    counts = [d.shape[2] // tc for d in douts]
    starts = [sum(counts[:k]) for k in range(len(douts))]
    assert sum(counts) == c // tc

    def body(u_ref, w_ref, b_ref, *rest):
        dy_refs = rest[:len(douts)]
        du_ref, dw_ref, db_ref = rest[len(douts):]
        j = pl.program_id(0)
        wv, bv = w_ref[...], b_ref[...]
        n = CONV_ROWS + CONV_HALO

        def step(i, sums):
            r0 = pl.multiple_of(i * CONV_ROWS, CONV_ROWS)
            taps = _conv_taps(_conv_window(u_ref, r0, True, True), kw, n)
            _, ds = _dsilu(_conv_apply(taps, wv, bv))
            dy = _conv_window(dy_refs[0], r0, False, True)
            for k in range(1, len(douts)):
                dy = jnp.where(j >= starts[k], _conv_window(dy_refs[k], r0, False, True), dy)
            du, new = _conv_grad_step(dy * ds, taps, wv, r0 + CONV_ROWS >= s)
            du_ref[0, pl.ds(r0, CONV_ROWS), :] = du.astype(BF16)
            return tuple(a + b_ for a, b_ in zip(sums, new))

        zero = tuple(jnp.zeros((8, tc), F32) for _ in range(kw + 1))
        sums = lax.fori_loop(0, s // CONV_ROWS, step, zero)
        _conv_store_sums(sums, dw_ref, db_ref, pl.program_id(1) == 0)

    def part_spec(k):
        return pl.BlockSpec((1, s, tc), lambda j, i: (i, 0, jnp.clip(j - starts[k], 0, counts[k] - 1)))

    du, dw, db = pl.pallas_call(
        body, name=name, grid=(c // tc, bsz),
        in_specs=[pl.BlockSpec((1, s, tc), lambda j, i: (i, 0, o0 + j)),
                  pl.BlockSpec((kw, tc), lambda j, i: (0, j)),
                  pl.BlockSpec((1, tc), lambda j, i: (0, j))] + [part_spec(k) for k in range(len(douts))],
        out_specs=[pl.BlockSpec((1, s, tc), lambda j, i: (i, 0, j)),
                   pl.BlockSpec((kw, tc), lambda j, i: (0, j)),
                   pl.BlockSpec((1, tc), lambda j, i: (0, j))],
        out_shape=[jax.ShapeDtypeStruct((bsz, s, c), BF16), jax.ShapeDtypeStruct((kw, c), F32),
                   jax.ShapeDtypeStruct((1, c), F32)],
        compiler_params=_params("parallel", "arbitrary"),
    )(zx, w, b.reshape(1, c), *douts)
    return du, dw, db.reshape(c)


def _conv_glu_fwd(up, w, b, *, name, tc=128):
    bsz, s, c2 = up.shape
    kw = w.shape[0]
    f = c2 // 2
    nt = f // tc

    def body(ug_ref, uv_ref, wg_ref, wv_ref, bg_ref, bv_ref, o_ref):
        wg, wv, bg, bv = wg_ref[...], wv_ref[...], bg_ref[...], bv_ref[...]

        def step(i, carry):
            r0 = pl.multiple_of(i * CONV_ROWS, CONV_ROWS)
            pg = _conv_apply(_conv_taps(_conv_window(ug_ref, r0, True, False), kw, CONV_ROWS), wg, bg)
            pv = _conv_apply(_conv_taps(_conv_window(uv_ref, r0, True, False), kw, CONV_ROWS), wv, bv)
            o_ref[0, pl.ds(r0, CONV_ROWS), :] = (pg * _sigmoid(pg) * pv).astype(BF16)
            return carry

        lax.fori_loop(0, s // CONV_ROWS, step, 0)

    b2 = b.reshape(1, c2)
    return pl.pallas_call(
        body, name=name, grid=(bsz, nt),
        in_specs=[pl.BlockSpec((1, s, tc), lambda i, j: (i, 0, j)),
                  pl.BlockSpec((1, s, tc), lambda i, j: (i, 0, nt + j)),
                  pl.BlockSpec((kw, tc), lambda i, j: (0, j)),
                  pl.BlockSpec((kw, tc), lambda i, j: (0, nt + j)),
                  pl.BlockSpec((1, tc), lambda i, j: (0, j)),
                  pl.BlockSpec((1, tc), lambda i, j: (0, nt + j))],
        out_specs=pl.BlockSpec((1, s, tc), lambda i, j: (i, 0, j)),
        out_shape=jax.ShapeDtypeStruct((bsz, s, f), BF16),
        compiler_params=_params("parallel", "parallel"),
    )(up, up, w, w, b2, b2)


def _conv_glu_bwd(up, w, b, df, *, name, tc=128):
    bsz, s, c2 = up.shape
    kw = w.shape[0]
    f = c2 // 2
    nt = f // tc

    def body(ug_ref, uv_ref, wg_ref, wv_ref, bg_ref, bv_ref, df_ref,
             dug_ref, duv_ref, dwg_ref, dwv_ref, dbg_ref, dbv_ref):
        first = pl.program_id(1) == 0
        wg, wv, bg, bv = wg_ref[...], wv_ref[...], bg_ref[...], bv_ref[...]
        n = CONV_ROWS + CONV_HALO

        def step(i, sums):
            r0 = pl.multiple_of(i * CONV_ROWS, CONV_ROWS)
            is_last = r0 + CONV_ROWS >= s
            tg = _conv_taps(_conv_window(ug_ref, r0, True, True), kw, n)
            tv = _conv_taps(_conv_window(uv_ref, r0, True, True), kw, n)
            pg = _conv_apply(tg, wg, bg)
            pv = _conv_apply(tv, wv, bv)
            sig, dsl = _dsilu(pg)
            dfv = _conv_window(df_ref, r0, False, True)
            dug, new_g = _conv_grad_step(dfv * pv * dsl, tg, wg, is_last)
            duv, new_v = _conv_grad_step(dfv * (pg * sig), tv, wv, is_last)
            dug_ref[0, pl.ds(r0, CONV_ROWS), :] = dug.astype(BF16)
            duv_ref[0, pl.ds(r0, CONV_ROWS), :] = duv.astype(BF16)
            return tuple(a + b_ for a, b_ in zip(sums, new_g + new_v))

        zero = tuple(jnp.zeros((8, tc), F32) for _ in range(2 * (kw + 1)))
        sums = lax.fori_loop(0, s // CONV_ROWS, step, zero)
        _conv_store_sums(sums[:kw + 1], dwg_ref, dbg_ref, first)
        _conv_store_sums(sums[kw + 1:], dwv_ref, dbv_ref, first)

    b2 = b.reshape(1, c2)
    act = lambda j, i: (i, 0, j)
    wsp = pl.BlockSpec((kw, tc), lambda j, i: (0, j))
    bsp = pl.BlockSpec((1, tc), lambda j, i: (0, j))
    dug, duv, dwg, dwv, dbg, dbv = pl.pallas_call(
        body, name=name, grid=(nt, bsz),
        in_specs=[pl.BlockSpec((1, s, tc), act),
                  pl.BlockSpec((1, s, tc), lambda j, i: (i, 0, nt + j)),
                  wsp, pl.BlockSpec((kw, tc), lambda j, i: (0, nt + j)),
                  bsp, pl.BlockSpec((1, tc), lambda j, i: (0, nt + j)),
                  pl.BlockSpec((1, s, tc), act)],
        out_specs=[pl.BlockSpec((1, s, tc), act), pl.BlockSpec((1, s, tc), act), wsp, wsp, bsp, bsp],
        out_shape=[jax.ShapeDtypeStruct((bsz, s, f), BF16)] * 2
        + [jax.ShapeDtypeStruct((kw, f), F32)] * 2 + [jax.ShapeDtypeStruct((1, f), F32)] * 2,
        compiler_params=_params("parallel", "arbitrary"),
    )(up, up, w, w, b2, b2, df)
    return (dug, duv, jnp.concatenate([dwg, dwv], axis=1),
            jnp.concatenate([dbg.reshape(f), dbv.reshape(f)]))


def _ssd_shared(xs, bm, cm, dtc_raw, dtr_raw, plane, psub, st):
    cl = SSM_CHUNK
    bias_l, a_l = plane[0:1, :], plane[1:2, :]
    bias_s, a_s = psub[:, 0:1], psub[:, 1:2]
    ri = lax.broadcasted_iota(I32, (cl, cl), 0)
    ci = lax.broadcasted_iota(I32, (cl, cl), 1)
    tril = ri >= ci
    low_incl = tril.astype(BF16)
    up_incl = (ri <= ci).astype(BF16)
    seg_t = (lax.broadcasted_iota(I32, (LANES, SSM_GROUP_W), 0)
             == lax.broadcasted_iota(I32, (LANES, SSM_GROUP_W), 1) // SSM_HEAD_DIM).astype(BF16)
    dt_c = _softplus(dtc_raw + bias_l)
    cs_c = _dot2_left(low_incl, dt_c * a_l)
    dt_r = _softplus(dtr_raw + bias_s)
    cs_r = _dot2(dt_r * a_s, up_incl)
    dt_ch = _dot2(dt_c, seg_t)
    cs_ch = _dot2(cs_c, seg_t)
    cs_last = cs_ch[cl - 1:cl, :]
    decay_ch = jnp.exp(cs_ch)
    w_ch = jnp.exp(cs_last - cs_ch)
    tot_ch = jnp.exp(cs_last)
    xdt = xs * dt_ch
    bm_b, cm_b = bm.astype(BF16), cm.astype(BF16)
    gmat = _dot_nt(cm_b, bm_b)
    cst = _dot(cm_b, st.astype(BF16))
    yoff = decay_ch * cst
    return dict(tril=tril, low_incl=low_incl, up_incl=up_incl, seg_t=seg_t, a_l=a_l, bias_l=bias_l,
                dt_c=dt_c, cs_c=cs_c, cs_r=cs_r, dt_ch=dt_ch, decay_ch=decay_ch, w_ch=w_ch,
                tot_ch=tot_ch, xdt=xdt, bm_b=bm_b, cm_b=cm_b, gmat=gmat, yoff=yoff)


def _head_decay(q, r):
    diff = q["cs_c"][:, r:r + 1] - q["cs_r"][r:r + 1, :]
    return jnp.where(q["tril"], jnp.exp(jnp.minimum(diff, 0.0)), 0.0)


def _half_mask(hh):
    lane = lax.broadcasted_iota(I32, (SSM_CHUNK, LANES), 1)
    return (lane < SSM_HEAD_DIM) if hh == 0 else (lane >= SSM_HEAD_DIM)


def _ssd_ydiag(q):
    pairs = []
    for pr in range(SSM_HEADS_PER_GROUP // 2):
        xp = q["xdt"][:, pr * LANES:(pr + 1) * LANES]
        acc = None
        for hh in range(2):
            mm_ = (q["gmat"] * _head_decay(q, 2 * pr + hh)).astype(BF16)
            part = _dot(mm_, jnp.where(_half_mask(hh), xp, 0.0).astype(BF16))
            acc = part if acc is None else acc + part
        pairs.append(acc)
    return jnp.concatenate(pairs, axis=1)


def _ssd_specs(bsz, s, g_n, d_inner, rev):
    cl = SSM_CHUNK
    nc = s // cl
    cc = (lambda c: nc - 1 - c) if rev else (lambda c: c)
    gb = d_inner // LANES
    dt0 = (d_inner + d_inner + 2 * g_n * SSM_STATE) // LANES
    return dict(
        z=pl.BlockSpec((1, cl, SSM_GROUP_W), lambda b, g, c: (b, cc(c), g)),
        dtc=pl.BlockSpec((1, cl, LANES), lambda b, g, c: (b, cc(c), dt0 + g)),
        xs=pl.BlockSpec((1, cl, SSM_GROUP_W), lambda b, g, c: (b, cc(c), g)),
        bm=pl.BlockSpec((1, cl, LANES), lambda b, g, c: (b, cc(c), gb + g)),
        cm=pl.BlockSpec((1, cl, LANES), lambda b, g, c: (b, cc(c), gb + g_n + g)),
        dtr=pl.BlockSpec((1, 1, SSM_ROWS, cl), lambda b, g, c: (b, g, 0, cc(c))),
        plane=pl.BlockSpec((1, 8, LANES), lambda b, g, c: (g, 0, 0)),
        psub=pl.BlockSpec((1, SSM_ROWS, LANES), lambda b, g, c: (g, 0, 0)),
        chan=pl.BlockSpec((1, SSM_GROUP_W), lambda b, g, c: (0, g)),
        state=pl.BlockSpec((1, 1, 1, SSM_STATE, SSM_GROUP_W), lambda b, g, c: (b, g, cc(c), 0, 0)),
        bgrp=pl.BlockSpec((1, cl, LANES), lambda b, g, c: (b, cc(c), g)),
    )


def _ssd_fwd(zx, xbc, dtr_row, plane, psub, d_ch, nw, *, name):
    bsz, s, _ = zx.shape
    d_inner = d_ch.shape[1]
    g_n = d_inner // SSM_GROUP_W
    nc = s // SSM_CHUNK
    sp = _ssd_specs(bsz, s, g_n, d_inner, False)

    def body(z_ref, dtc_ref, xs_ref, bm_ref, cm_ref, dtr_ref, plane_ref, psub_ref, d_ref, nw_ref,
             gn_ref, st_out_ref, st_ref):
        @pl.when(pl.program_id(2) == 0)
        def _():
            st_ref[...] = jnp.zeros_like(st_ref)

        xs = xs_ref[0]
        st = st_ref[...]
        st_out_ref[0, 0, 0] = st
        q = _ssd_shared(xs, bm_ref[0], cm_ref[0], dtc_ref[0], dtr_ref[0, 0], plane_ref[0],
                        psub_ref[0], st)
        y = _ssd_ydiag(q) + q["yoff"] + xs * d_ref[...]
        st_ref[...] = q["tot_ch"] * st + _dot_tn(q["bm_b"], (q["w_ch"] * q["xdt"]).astype(BF16))
        zv = z_ref[0]
        gy = y * (zv * _sigmoid(zv))
        rstd = lax.rsqrt(jnp.mean(gy * gy, axis=-1, keepdims=True) + SSM_NORM_EPS)
        gn_ref[0] = ((gy * rstd) * nw_ref[...]).astype(BF16)

    return pl.pallas_call(
        body, name=name, grid=(bsz, g_n, nc),
        in_specs=[sp["z"], sp["dtc"], sp["xs"], sp["bm"], sp["cm"], sp["dtr"], sp["plane"],
                  sp["psub"], sp["chan"], sp["chan"]],
        out_specs=[sp["z"], sp["state"]],
        out_shape=[jax.ShapeDtypeStruct((bsz, s, d_inner), BF16),
                   jax.ShapeDtypeStruct((bsz, g_n, nc, SSM_STATE, SSM_GROUP_W), F32)],
        scratch_shapes=[pltpu.VMEM((SSM_STATE, SSM_GROUP_W), F32)],
        compiler_params=_params("parallel", "parallel", "arbitrary"),
    )(zx, zx, xbc, xbc, xbc, dtr_row, plane, psub, d_ch, nw)


def _ssd_bwd(zx, xbc, dtr_row, plane, psub, d_ch, nw, states, dgn, *, name):
    bsz, s, _ = zx.shape
    d_inner = d_ch.shape[1]
    g_n = d_inner // SSM_GROUP_W
    cl = SSM_CHUNK
    nc = s // cl
    sp = _ssd_specs(bsz, s, g_n, d_inner, True)
    acc_ch = pl.BlockSpec((1, 1, 8, SSM_GROUP_W), lambda b, g, c: (b, g, 0, 0))
    acc_ln = pl.BlockSpec((1, 1, 8, LANES), lambda b, g, c: (b, g, 0, 0))

    def body(z_ref, dtc_ref, xs_ref, bm_ref, cm_ref, dtr_ref, plane_ref, psub_ref, d_ref, nw_ref,
             st_in_ref, dgn_ref,
             dxs_ref, dbm_ref, dcm_ref, dz_ref, ddt_ref, ach_ref, aln_ref, dst_ref):
        first = pl.program_id(2) == 0

        @pl.when(first)
        def _():
            dst_ref[...] = jnp.zeros_like(dst_ref)
            ach_ref[...] = jnp.zeros_like(ach_ref)
            aln_ref[...] = jnp.zeros_like(aln_ref)

        xs = xs_ref[0]
        st = st_in_ref[0, 0, 0]
        q = _ssd_shared(xs, bm_ref[0], cm_ref[0], dtc_ref[0], dtr_ref[0, 0], plane_ref[0],
                        psub_ref[0], st)
        d_chv = d_ref[...]
        nwv = nw_ref[...]
        y = _ssd_ydiag(q) + q["yoff"] + xs * d_chv
        zv = z_ref[0]
        sz = _sigmoid(zv)
        silu_z = zv * sz
        gy = y * silu_z
        rstd = lax.rsqrt(jnp.mean(gy * gy, axis=-1, keepdims=True) + SSM_NORM_EPS)
        gyh = gy * rstd
        dgnv = dgn_ref[0]
        dgyh = dgnv * nwv
        dgy = rstd * (dgyh - gyh * jnp.mean(dgyh * gyh, axis=-1, keepdims=True))
        dy = dgy * silu_z
        dz_ref[0] = (dgy * y * (sz * (1.0 + zv * (1.0 - sz)))).astype(BF16)
        ach_ref[0, 0, 0:1, :] += jnp.sum(dgnv * gyh, axis=0, keepdims=True)
        ach_ref[0, 0, 1:2, :] += jnp.sum(dy * xs, axis=0, keepdims=True)
        st_b = st.astype(BF16)
        dyd = (dy * q["decay_ch"]).astype(BF16)
        dcm = _dot_nt(dyd, st_b)
        dstn = dst_ref[...]
        dstn_b = dstn.astype(BF16)
        bds = _dot(q["bm_b"], dstn_b)
        wx = q["w_ch"] * q["xdt"]
        dbm = _dot_nt(wx.astype(BF16), dstn_b)
        dst_ref[...] = q["tot_ch"] * dstn + _dot_tn(q["cm_b"], dyd)
        vterm = wx * bds
        cs_terms = dy * q["yoff"] - vterm
        last_ch = q["tot_ch"] * jnp.sum(dstn * st, axis=0, keepdims=True) + jnp.sum(vterm, axis=0, keepdims=True)
        lane = lax.broadcasted_iota(I32, (cl, LANES), 1)
        rowi = lax.broadcasted_iota(I32, (SSM_ROWS, cl), 0)
        dg_sum = jnp.zeros((cl, cl), F32)
        dcs_col = jnp.zeros((cl, LANES), F32)
        dcs_row = jnp.zeros((SSM_ROWS, cl), F32)
        dxdt_pairs = []
        for pr in range(SSM_HEADS_PER_GROUP // 2):
            xp_b = q["xdt"][:, pr * LANES:(pr + 1) * LANES].astype(BF16)
            dyp = dy[:, pr * LANES:(pr + 1) * LANES]
            acc = None
            for hh in range(2):
                r = 2 * pr + hh
                dm = _head_decay(q, r)
                mmat = q["gmat"] * dm
                dym = jnp.where(_half_mask(hh), dyp, 0.0).astype(BF16)
                dmat = jnp.where(q["tril"], _dot_nt(dym, xp_b), 0.0)
                part = _dot_tn(mmat.astype(BF16), dym)
                acc = part if acc is None else acc + part
                dg_sum = dg_sum + dmat * dm
                e = dmat * mmat
                dcs_col = dcs_col + jnp.where(lane == r, jnp.sum(e, axis=1, keepdims=True), 0.0)
                dcs_row = dcs_row + jnp.where(rowi == r, jnp.sum(e, axis=0, keepdims=True), 0.0)
            dxdt_pairs.append(acc)
        dg_b = dg_sum.astype(BF16)
        dcm_ref[0] = dcm + _dot(dg_b, q["bm_b"])
        dbm_ref[0] = dbm + _dot_tn(dg_b, q["cm_b"])
        dxdt = q["w_ch"] * bds + jnp.concatenate(dxdt_pairs, axis=1)
        dxs_ref[0] = dy * d_chv + dxdt * q["dt_ch"]
        seg = (lax.broadcasted_iota(I32, (SSM_GROUP_W, LANES), 0) // SSM_HEAD_DIM
               == lax.broadcasted_iota(I32, (SSM_GROUP_W, LANES), 1)).astype(BF16)
        row_as_col = jnp.transpose(jnp.concatenate(
            [dcs_row, jnp.zeros((cl - SSM_ROWS, cl), F32)], axis=0))
        dcs = dcs_col - row_as_col + _dot2(cs_terms, seg)
        last = _dot2(jnp.zeros((8, SSM_GROUP_W), F32) + last_ch, seg)[0:1, :]
        da = _dot2_left(q["up_incl"], dcs) + last
        ddt = _dot2(dxdt * xs, seg) + da * q["a_l"]
        ddtr = ddt * _sigmoid(dtc_ref[0] + q["bias_l"])
        ddt_ref[0] = ddtr.astype(BF16)
        aln_ref[0, 0, 0:1, :] += jnp.sum(ddtr, axis=0, keepdims=True)
        aln_ref[0, 0, 1:2, :] += jnp.sum(da * q["dt_c"], axis=0, keepdims=True)

    outs = pl.pallas_call(
        body, name=name, grid=(bsz, g_n, nc),
        in_specs=[sp["z"], sp["dtc"], sp["xs"], sp["bm"], sp["cm"], sp["dtr"], sp["plane"],
                  sp["psub"], sp["chan"], sp["chan"], sp["state"], sp["z"]],
        out_specs=[sp["z"], sp["bgrp"], sp["bgrp"], sp["z"], sp["bgrp"], acc_ch, acc_ln],
        out_shape=[jax.ShapeDtypeStruct((bsz, s, d_inner), F32),
                   jax.ShapeDtypeStruct((bsz, s, g_n * SSM_STATE), F32),
                   jax.ShapeDtypeStruct((bsz, s, g_n * SSM_STATE), F32),
                   jax.ShapeDtypeStruct((bsz, s, d_inner), BF16),
                   jax.ShapeDtypeStruct((bsz, s, g_n * LANES), BF16),
                   jax.ShapeDtypeStruct((bsz, g_n, 8, SSM_GROUP_W), F32),
                   jax.ShapeDtypeStruct((bsz, g_n, 8, LANES), F32)],
        scratch_shapes=[pltpu.VMEM((SSM_STATE, SSM_GROUP_W), F32)],
        compiler_params=_params("parallel", "parallel", "arbitrary"),
    )(zx, zx, xbc, xbc, xbc, dtr_row, plane, psub, d_ch, nw, states, dgn)
    return outs


def _sb_stack(x):
    out = []
    for i in range(x.shape[0] // SB_BLOCK):
        xb = x[i * SB_BLOCK:(i + 1) * SB_BLOCK]
        lane = lax.broadcasted_iota(I32, xb.shape, 1)
        zero = jnp.zeros_like(xb)
        out += [jnp.where(lane < SB_HEAD_DIM, xb, zero), jnp.where(lane >= SB_HEAD_DIM, xb, zero)]
    return jnp.concatenate(out, axis=0)


def _sb_unstack_t(acc_t):
    row = lax.broadcasted_iota(I32, (LANES, SB_BLOCK), 0)
    out = []
    for i in range(acc_t.shape[1] // (2 * SB_BLOCK)):
        a = acc_t[:, 2 * i * SB_BLOCK:(2 * i + 1) * SB_BLOCK]
        b = acc_t[:, (2 * i + 1) * SB_BLOCK:(2 * i + 2) * SB_BLOCK]
        out.append(jnp.transpose(jnp.where(row < SB_HEAD_DIM, a, b)))
    return jnp.concatenate(out, axis=0)


def _sb_tile_blocks(nq, q_blocks):
    nb = 4 if nq % 4 == 0 else (2 if nq % 2 == 0 else 1)
    return nb, min(nb, q_blocks)


def _sb_valid(u, qi0, nb, nqb):
    shape = (nb * SB_BLOCK, nqb * 2 * SB_BLOCK)
    key = u * (nb * SB_BLOCK) + lax.broadcasted_iota(I32, shape, 0)
    col = lax.broadcasted_iota(I32, shape, 1)
    qpos = (qi0 + col // (2 * SB_BLOCK)) * SB_BLOCK + col % SB_BLOCK
    return key < qpos


def _sb_logits(kb, qs, valid):
    z = _dot_nt(kb, qs)
    lb = jnp.minimum(z, 0.0) - jnp.log(1.0 + jnp.exp(-jnp.abs(z)))
    lk_all = lb - z
    lk = lk_all if valid is None else jnp.where(valid, lk_all, 0.0)
    return z, lb, lk_all, lk


def _sb_scan(tri2, x, nb, reverse, exact=True):
    blk = SB_BLOCK
    edge = 0 if reverse else blk - 1
    carry = jnp.zeros((1, x.shape[1]), F32)
    res = [None] * nb
    for i in (reversed(range(nb)) if reverse else range(nb)):
        part = x[i * blk:(i + 1) * blk]
        if exact:
            hi, lo = _split2(part)
            raw = _dot(tri2, jnp.concatenate([hi, lo], axis=0))
        else:
            raw = _dot(tri2[:, :blk], part.astype(BF16))
        res[i] = raw + carry
        carry = carry + (raw[edge:edge + 1] + part[edge:edge + 1])
    return jnp.concatenate(res, axis=0), carry


def _sb_fwd(q, kv, kvt, *, name):
    bsz, s, w = q.shape
    blk = SB_BLOCK
    npair = w // LANES
    nq = s // blk
    nb, nqb = _sb_tile_blocks(nq, SB_Q_BLOCKS_FWD)
    width = nqb * 2 * blk

    def body(q_ref, k_ref, vt_ref, o_ref, tot_ref):
        qi0 = pl.program_id(2) * nqb
        qs = _sb_stack(q_ref[0] * SB_SCALE)
        ri = lax.broadcasted_iota(I32, (blk, blk), 0)
        ci = lax.broadcasted_iota(I32, (blk, blk), 1)
        upper = (ri < ci).astype(BF16)
        tri2 = jnp.concatenate([upper, upper], axis=1)

        def tile(u, r, acc, masked):
            rows = pl.ds(pl.multiple_of(u * (nb * blk), nb * blk), nb * blk)
            valid = _sb_valid(u, qi0, nb, nqb) if masked else None
            _, lb, _, lk = _sb_logits(k_ref[0, rows, :], qs, valid)
            sfx, total = _sb_scan(tri2, lk, nb, True)
            wgt = jnp.exp(lb + sfx + r)
            if masked:
                wgt = jnp.where(valid, wgt, 0.0)
            wb = wgt.astype(BF16)
            for i in range(nb):
                acc = acc + _dot(vt_ref[0, 0, u * nb + i], wb[i * blk:(i + 1) * blk])
            return r + total, acc

        top = qi0 // nb
        r, acc = tile(top, jnp.zeros((1, width), F32), jnp.zeros((LANES, width), F32), True)
        r, acc = lax.fori_loop(0, top, lambda t, c: tile(top - 1 - t, c[0], c[1], False), (r, acc))
        o_ref[0] = _sb_unstack_t(acc).astype(BF16)
        tot_ref[0, 0, 0] = r

    qspec = pl.BlockSpec((1, nqb * blk, LANES), lambda b, p, i: (b, i, p))
    return pl.pallas_call(
        body, name=name, grid=(bsz, npair, nq // nqb),
        in_specs=[qspec,
                  pl.BlockSpec((1, s, LANES), lambda b, p, i: (b, 0, p)),
                  pl.BlockSpec((1, 1, nq, LANES, blk), lambda b, p, i: (b, npair + p, 0, 0, 0))],
        out_specs=[qspec, pl.BlockSpec((1, 1, 1, 1, width), lambda b, p, i: (b, p, i, 0, 0))],
        out_shape=[jax.ShapeDtypeStruct((bsz, s, w), BF16),
                   jax.ShapeDtypeStruct((bsz, npair, nq // nqb, 1, width), F32)],
        compiler_params=_params("parallel", "parallel", "arbitrary"),
    )(q, kv, kvt)


def _kv_blocks_t(kv3):
    bsz, s, w2 = kv3.shape
    x = kv3.reshape(bsz, s // SB_BLOCK, SB_BLOCK, w2 // LANES, LANES)
    return jnp.transpose(x, (0, 3, 1, 4, 2))


def _sb_bwd(q, kv, kvt, do, tot, dk_in, dv_in, *, name):
    bsz, s, w = q.shape
    blk = SB_BLOCK
    npair = w // LANES
    nq = s // blk
    nb, nqb = _sb_tile_blocks(nq, SB_Q_BLOCKS_BWD)
    width = nqb * 2 * blk
    tot = tot.reshape(bsz, npair, nq // nqb, 1, width)
    has_init = dk_in is not None

    def body(*refs):
        if has_init:
            q_ref, k_ref, v_ref, kt_ref, do_ref, tot_ref, dki_ref, dvi_ref, dq_ref, dk_ref, dv_ref = refs
        else:
            q_ref, k_ref, v_ref, kt_ref, do_ref, tot_ref, dq_ref, dk_ref, dv_ref = refs
        qi0 = pl.program_id(2) * nqb

        @pl.when(qi0 == 0)
        def _():
            if has_init:
                dk_ref[...] = dki_ref[...]
                dv_ref[...] = dvi_ref[...]
            else:
                dk_ref[...] = jnp.zeros_like(dk_ref)
                dv_ref[...] = jnp.zeros_like(dv_ref)

        qs = _sb_stack(q_ref[0] * SB_SCALE)
        dos = _sb_stack(do_ref[0])
        totv = tot_ref[0, 0, 0]
        ri = lax.broadcasted_iota(I32, (blk, blk), 0)
        ci = lax.broadcasted_iota(I32, (blk, blk), 1)
        lower = (ri > ci).astype(BF16)
        tri2 = jnp.concatenate([lower, lower], axis=1)

        def tile(u, pre_lk, pre_d, dqt, masked):
            rows = pl.ds(pl.multiple_of(u * (nb * blk), nb * blk), nb * blk)
            valid = _sb_valid(u, qi0, nb, nqb) if masked else None
            z, lb, lk_all, lk = _sb_logits(k_ref[0, rows, :], qs, valid)
            before, tot_lk = _sb_scan(tri2, lk, nb, False)
            wgt = jnp.exp(z + ((totv - pre_lk) - before))
            if masked:
                wgt = jnp.where(valid, wgt, 0.0)
            dlogit = _dot_nt(v_ref[0, rows, :], dos) * wgt
            dbefore, tot_d = _sb_scan(tri2, dlogit, nb, False, exact=False)
            sig = jnp.exp(lb)
            dz = dlogit * (1.0 - sig) - (pre_d + dbefore) * sig
            if masked:
                dz = jnp.where(valid, dz, 0.0)
            dz_b = dz.astype(BF16)
            for i in range(nb):
                dqt = dqt + _dot(kt_ref[0, 0, u * nb + i], dz_b[i * blk:(i + 1) * blk])
            dk_ref[0, rows, :] += _dot(dz_b, qs)
            dv_ref[0, rows, :] += _dot(wgt.astype(BF16), dos)
            return pre_lk + tot_lk, pre_d + tot_d, dqt

        zero = jnp.zeros((1, width), F32)
        top = qi0 // nb
        c = lax.fori_loop(0, top, lambda u, c: tile(u, c[0], c[1], c[2], False),
                          (zero, zero, jnp.zeros((LANES, width), F32)))
        _, _, dqt = tile(top, c[0], c[1], c[2], True)
        dq_ref[0] = (_sb_unstack_t(dqt) * SB_SCALE).astype(BF16)

    qspec = pl.BlockSpec((1, nqb * blk, LANES), lambda b, p, i: (b, i, p))
    kspec = pl.BlockSpec((1, s, LANES), lambda b, p, i: (b, 0, p))
    vspec = pl.BlockSpec((1, s, LANES), lambda b, p, i: (b, 0, npair + p))
    ktspec = pl.BlockSpec((1, 1, nq, LANES, blk), lambda b, p, i: (b, p, 0, 0, 0))
    tspec = pl.BlockSpec((1, 1, 1, 1, width), lambda b, p, i: (b, p, i, 0, 0))
    in_specs = [qspec, kspec, vspec, ktspec, qspec, tspec] + ([kspec, kspec] if has_init else [])
    args = (q, kv, kv, kvt, do, tot) + ((dk_in, dv_in) if has_init else ())
    return pl.pallas_call(
        body, name=name, grid=(bsz, npair, nq // nqb), in_specs=in_specs,
        out_specs=[qspec, kspec, kspec],
        out_shape=[jax.ShapeDtypeStruct((bsz, s, w), BF16), jax.ShapeDtypeStruct((bsz, s, w), F32),
                   jax.ShapeDtypeStruct((bsz, s, w), F32)],
        compiler_params=_params("parallel", "parallel", "arbitrary"),
    )(*args)


ADAM_BLOCK_BYTES = 1 << 20


def _adamw(w, g, m, v, *, name):
    shape = w.shape
    r, c = shape[-2], shape[-1]
    lead = _size(shape[:-2])
    tr = r
    for cand in range(8, r, 8):
        if r % cand == 0 and cand * c * 4 <= ADAM_BLOCK_BYTES:
            tr = cand
    if r * c * 4 <= ADAM_BLOCK_BYTES:
        tr = r

    def body(w_ref, g_ref, m_ref, v_ref, d_ref, mo_ref, vo_ref):
        gv = g_ref[...]
        mn = ADAM_B1 * m_ref[...] + (1.0 - ADAM_B1) * gv
        vn = ADAM_B2 * v_ref[...] + (1.0 - ADAM_B2) * (gv * gv)
        m_hat = mn / (1.0 - ADAM_B1 ** ADAM_STEP)
        v_hat = vn / (1.0 - ADAM_B2 ** ADAM_STEP)
        d_ref[...] = -ADAM_LR * (m_hat / (jnp.sqrt(v_hat) + ADAM_EPS) + ADAM_WD * w_ref[...])
        mo_ref[...] = mn
        vo_ref[...] = vn

    blk = pl.BlockSpec((1, tr, c), lambda l, i: (l, i, 0))
    outs = pl.pallas_call(
        body, name=name, grid=(lead, r // tr), in_specs=[blk] * 4, out_specs=[blk] * 3,
        out_shape=[jax.ShapeDtypeStruct((lead, r, c), F32)] * 3,
        compiler_params=_params("parallel", "parallel"),
    )(*[a.reshape(lead, r, c) for a in (w, g, m, v)])
    return [o.reshape(shape) for o in outs]


def _add_own_half(idx, g, recv, *, rows=1024):
    _, _, rh, _ = g.shape
    tr = _tile(rh, rows)

    def body(idx_ref, a_ref, b_ref, o_ref):
        o_ref[...] = (a_ref[0].astype(F32) + b_ref[...].astype(F32)).astype(o_ref.dtype)

    return pl.pallas_call(
        body, name="grad_pair_sum",
        grid_spec=pltpu.PrefetchScalarGridSpec(
            num_scalar_prefetch=1, grid=(N_CHIPS, rh // tr),
            in_specs=[pl.BlockSpec((1, 1, tr, LANES), lambda k, i, idx: (k, idx[0], i, 0)),
                      pl.BlockSpec((1, tr, LANES), lambda k, i, idx: (k, i, 0))],
            out_specs=pl.BlockSpec((1, tr, LANES), lambda k, i, idx: (k, i, 0))),
        out_shape=jax.ShapeDtypeStruct((N_CHIPS, rh, LANES), g.dtype),
        compiler_params=_params("parallel", "parallel"),
    )(idx, g, recv)


def _add_chips(idx, own, recv, *, rows=1024):
    _, rh, _ = own.shape
    tr = _tile(rh, rows)

    def body(idx_ref, a_ref, b_ref, o_ref):
        f = lambda v: v.astype(F32)
        o_ref[...] = ((f(a_ref[0]) + f(b_ref[0])) + f(b_ref[1])) + f(b_ref[2])

    return pl.pallas_call(
        body, name="grad_chip_sum",
        grid_spec=pltpu.PrefetchScalarGridSpec(
            num_scalar_prefetch=1, grid=(rh // tr,),
            in_specs=[pl.BlockSpec((1, tr, LANES), lambda i, idx: (idx[0], i, 0)),
                      pl.BlockSpec((3, tr, LANES), lambda i, idx: (0, i, 0))],
            out_specs=pl.BlockSpec((tr, LANES), lambda i, idx: (i, 0))),
        out_shape=jax.ShapeDtypeStruct((rh, LANES), F32),
        compiler_params=_params("parallel"),
    )(idx, own, recv)


def _sum_devices(parts):
    _, r, _ = parts.shape

    def body(p_ref, o_ref):
        acc = p_ref[0]
        for k in range(1, N_DEV):
            acc = acc + p_ref[k]
        o_ref[...] = acc

    return pl.pallas_call(
        body, name="small_grad_sum", grid=(1,),
        in_specs=[pl.BlockSpec((N_DEV, r, LANES), lambda i: (0, 0, 0))],
        out_specs=pl.BlockSpec((r, LANES), lambda i: (0, 0)),
        out_shape=jax.ShapeDtypeStruct((r, LANES), F32),
    )(parts)


def _place():
    return lax.axis_index("x"), lax.axis_index("y"), lax.axis_index("c")


def _rcopy(src, dst, send_sems, recv_sems, k, to):
    return pltpu.make_async_remote_copy(src_ref=src, dst_ref=dst, send_sem=send_sems.at[k],
                                        recv_sem=recv_sems.at[k], device_id=to, device_id_type=MESH)


def _gather_weights(wl):
    _, rh, _ = wl.shape

    def body(w_ref, out_ref, send_sems, recv_sems):
        x, y, c = _place()
        sibling = (x, y, 1 - c)
        chips = [(1 - x, y), (x, 1 - y), (1 - x, 1 - y)]

        def piece(px, py, pc):
            return out_ref.at[2 * px + py, pc]

        first = [_rcopy(w_ref.at[c], piece(x, y, c), send_sems, recv_sems, j, (*chip, c))
                 for j, chip in enumerate(chips)]
        for cp in first:
            cp.start()
        passed = [_rcopy(piece(*chip, c), piece(*chip, c), send_sems, recv_sems, 3 + j, sibling)
                  for j, chip in enumerate(chips)]
        for j, chip in enumerate(chips):
            _rcopy(piece(*chip, c), piece(*chip, c), send_sems, recv_sems, j, (*chip, c)).wait_recv()
            passed[j].start()
        for j, chip in enumerate(chips):
            _rcopy(piece(*chip, 1 - c), piece(*chip, 1 - c), send_sems, recv_sems, 3 + j,
                   sibling).wait_recv()
        for cp in first + passed:
            cp.wait_send()

    return pl.pallas_call(
        body, name="gather_weights", in_specs=[ANY], out_specs=ANY,
        out_shape=jax.ShapeDtypeStruct((N_CHIPS, 2, rh, LANES), wl.dtype),
        scratch_shapes=[pltpu.SemaphoreType.DMA((6,)), pltpu.SemaphoreType.DMA((6,))],
    )(wl)


def _swap_halves(g):
    _, _, rh, _ = g.shape

    def body(g_ref, out_ref, send_sems, recv_sems):
        x, y, c = _place()
        sibling = (x, y, 1 - c)
        cps = [_rcopy(g_ref.at[k, 1 - c], out_ref.at[k], send_sems, recv_sems, k, sibling)
               for k in range(N_CHIPS)]
        for cp in cps:
            cp.start()
        for cp in cps:
            cp.wait()

    return pl.pallas_call(
        body, name="grad_swap_halves", in_specs=[ANY], out_specs=ANY,
        out_shape=jax.ShapeDtypeStruct((N_CHIPS, rh, LANES), g.dtype),
        scratch_shapes=[pltpu.SemaphoreType.DMA((N_CHIPS,)), pltpu.SemaphoreType.DMA((N_CHIPS,))],
    )(g)


def _scatter_chunks(s4):
    _, rh, _ = s4.shape

    def body(s_ref, out_ref, send_sems, recv_sems):
        x, y, c = _place()
        chips = [(1 - x, y), (x, 1 - y), (1 - x, 1 - y)]
        cps = [_rcopy(s_ref.at[2 * chip[0] + chip[1]], out_ref.at[j], send_sems, recv_sems, j,
                      (*chip, c)) for j, chip in enumerate(chips)]
        for cp in cps:
            cp.start()
        for cp in cps:
            cp.wait()

    return pl.pallas_call(
        body, name="grad_scatter_chunks", in_specs=[ANY], out_specs=ANY,
        out_shape=jax.ShapeDtypeStruct((3, rh, LANES), s4.dtype),
        scratch_shapes=[pltpu.SemaphoreType.DMA((3,)), pltpu.SemaphoreType.DMA((3,))],
    )(s4)


def _share_half(tot):
    rh, _ = tot.shape

    def body(t_ref, out_ref, send_sems, recv_sems):
        x, y, c = _place()
        cp = _rcopy(t_ref, out_ref, send_sems, recv_sems, 0, (x, y, 1 - c))
        cp.start()
        cp.wait()

    return pl.pallas_call(
        body, name="grad_share_half", in_specs=[ANY], out_specs=ANY,
        out_shape=jax.ShapeDtypeStruct((rh, LANES), tot.dtype),
        scratch_shapes=[pltpu.SemaphoreType.DMA((1,)), pltpu.SemaphoreType.DMA((1,))],
    )(tot)


def _exchange_small(r):
    rr, _ = r.shape

    def body(r_ref, out_ref, send_sems, recv_sems, local_sem):
        x, y, c = _place()
        me = 4 * x + 2 * y + c
        mine = pltpu.make_async_copy(r_ref, out_ref.at[me], local_sem)
        mine.start()
        cps = []
        for k in range(N_DEV - 1):
            fx, fy, fc = ((k + 1) >> 2) & 1, ((k + 1) >> 1) & 1, (k + 1) & 1
            to = (x ^ fx, y ^ fy, c ^ fc)
            cps.append((_rcopy(r_ref, out_ref.at[me], send_sems, recv_sems, k, to), to))
        for cp, _ in cps:
            cp.start()
        for k, (cp, to) in enumerate(cps):
            src = 4 * to[0] + 2 * to[1] + to[2]
            _rcopy(r_ref, out_ref.at[src], send_sems, recv_sems, k, to).wait_recv()
        for cp, _ in cps:
            cp.wait_send()
        mine.wait()

    return pl.pallas_call(
        body, name="small_grad_exchange", in_specs=[ANY], out_specs=ANY,
        out_shape=jax.ShapeDtypeStruct((N_DEV, rr, LANES), r.dtype),
        scratch_shapes=[pltpu.SemaphoreType.DMA((N_DEV - 1,)), pltpu.SemaphoreType.DMA((N_DEV - 1,)),
                        pltpu.SemaphoreType.DMA],
    )(r)


def _size(shape):
    n = 1
    for d in shape:
        n *= d
    return n


def _slab_rows(shape):
    rows = -(-_size(shape) // LANES)
    return -(-rows // SLAB_ROW_ALIGN) * SLAB_ROW_ALIGN


def _pack_rows(arrs, dtype, lead=0, unit=PACK_ROWS):
    parts, total = [], 0
    for a in arrs:
        front, shp = a.shape[:lead], a.shape[lead:]
        n, rows = _size(shp), _slab_rows(shp)
        nopad = [(0, 0)] * lead
        if n % LANES == 0:
            p = a.reshape(front + (n // LANES, LANES)).astype(dtype)
        else:
            p = jnp.pad(a.reshape(front + (n,)).astype(dtype), nopad + [(0, rows * LANES - n)])
            p = p.reshape(front + (rows, LANES))
        if p.shape[lead] != rows:
            p = jnp.pad(p, nopad + [(0, rows - p.shape[lead]), (0, 0)])
        parts.append(p)
        total += rows
    pad = (-total) % unit
    if pad:
        parts.append(jnp.zeros(parts[0].shape[:lead] + (pad, LANES), dtype))
    return jnp.concatenate(parts, axis=lead)


def _unpack_rows(slab, shapes):
    lead = slab.shape[:-2]
    out, off = [], 0
    for shp in shapes:
        n, rows = _size(shp), _slab_rows(shp)
        piece = slab[..., off:off + rows, :]
        if n % LANES == 0:
            piece = piece[..., :n // LANES, :].reshape(lead + tuple(shp))
        else:
            piece = piece.reshape(lead + (rows * LANES,))[..., :n].reshape(lead + tuple(shp))
        out.append(piece)
        off += rows
    return out


def _ffn_fwd(h, bsz, s, gain, w_up, cw, cb, w_down, i):
    hf = _rmsnorm_fwd(h, gain, name=f"ffn_norm_{i}")
    up = _mm(hf, w_up, name=f"ffn_up_{i}")
    up3 = up.reshape(bsz, s, -1)
    f = _conv_glu_fwd(up3, cw, cb, name=f"ffn_glu_{i}").reshape(h.shape[0], -1)
    h2 = _mm(f, w_down, add=h, name=f"ffn_down_{i}")
    return h2, (h, hf, up3, f)


def _ffn_bwd(dh, saved, gain, w_up, cw, cb, w_down, i):
    h, hf, up3, f = saved
    t = h.shape[0]
    d_down = _mm(f, dh, ta=True, name=f"ffn_down_dw_{i}")
    df = _mm(dh, w_down, tb=True, name=f"ffn_down_dx_{i}")
    dug, duv, dcw, dcb = _conv_glu_bwd(up3, cw, cb, df.reshape(up3.shape[0], up3.shape[1], -1),
                                       name=f"ffn_glu_bwd_{i}")
    dug, duv = dug.reshape(t, -1), duv.reshape(t, -1)
    fdim = dug.shape[1]
    d_up = jnp.concatenate([_mm(hf, dug, ta=True, name=f"ffn_up_dwg_{i}"),
                            _mm(hf, duv, ta=True, name=f"ffn_up_dwv_{i}")], axis=1)
    dhf = _mm(dug, w_up[:, :fdim], tb=True, name=f"ffn_up_dxg_{i}")
    dhf = _mm(duv, w_up[:, fdim:], tb=True, add=dhf, name=f"ffn_up_dxv_{i}")
    dh, dgain = _rmsnorm_bwd(h, gain, dhf, dh, name=f"ffn_norm_bwd_{i}")
    return dh, dgain, d_up, dcw, dcb, d_down


def _ple_layer_fwd(h, p_i, gain, w_gate, w_proj, i):
    hp = _rmsnorm_fwd(h, gain, name=f"ple_norm_{i}")
    a = _mm(hp, w_gate, name=f"ple_gate_{i}")
    pp = _mm(p_i, w_proj, name=f"ple_proj_{i}")
    return _ple_fwd(h, a, pp, name=f"ple_mix_{i}"), (h, hp, a, pp)


def _ple_layer_bwd(dh, saved, p_i, gain, w_gate, i):
    h, hp, a, pp = saved
    da, dpp = _ple_bwd(dh, a, pp, name=f"ple_mix_bwd_{i}")
    d_gate = _mm(hp, da, ta=True, name=f"ple_gate_dw_{i}")
    d_proj = _mm(p_i, dpp, ta=True, name=f"ple_proj_dw_{i}")
    dhp = _mm(da, w_gate, tb=True, name=f"ple_gate_dx_{i}")
    dh, dgain = _rmsnorm_bwd(h, gain, dhp, dh, name=f"ple_norm_bwd_{i}")
    return dh, dgain, d_gate, d_proj


def _ssm_consts(dt_bias, a_log, d_skip, g_n):
    hpg = SSM_HEADS_PER_GROUP
    a = -jnp.exp(a_log)
    rows = jnp.stack([dt_bias.reshape(g_n, hpg), a.reshape(g_n, hpg)], axis=1)
    plane = jnp.zeros((g_n, 8, LANES), F32).at[:, 0:2, 0:hpg].set(rows)
    psub = jnp.zeros((g_n, SSM_ROWS, LANES), F32).at[:, 0:hpg, 0:2].set(jnp.swapaxes(rows, 1, 2))
    d_ch = jnp.repeat(d_skip, SSM_HEAD_DIM).reshape(1, -1)
    return a, plane, psub, d_ch


def _ssm_in_big(w_in, d_inner, g_n):
    d = w_in.shape[0]
    cut = w_in.shape[1] - g_n * SSM_HEADS_PER_GROUP
    wdt = w_in[:, cut:].reshape(d, g_n, SSM_HEADS_PER_GROUP)
    wdt = jnp.pad(wdt, ((0, 0), (0, 0), (0, LANES - SSM_HEADS_PER_GROUP))).reshape(d, g_n * LANES)
    return jnp.concatenate([w_in[:, :cut], wdt], axis=1)


def _ssm_in_small(dw_big, g_n):
    d = dw_big.shape[0]
    cut = dw_big.shape[1] - g_n * LANES
    ddt = dw_big[:, cut:].reshape(d, g_n, LANES)[:, :, :SSM_HEADS_PER_GROUP].reshape(d, -1)
    return jnp.concatenate([dw_big[:, :cut], ddt], axis=1)


def _ssm_fwd(h, bsz, s, gain, w_in_big, cw, cb, plane, psub, d_ch, nw, w_out, i):
    d_inner = d_ch.shape[1]
    g_n = d_inner // SSM_GROUP_W
    conv_dim = cw.shape[1]
    hn = _rmsnorm_fwd(h, gain, name=f"attn_norm_{i}")
    zx = _mm(hn, w_in_big, name=f"ssm_in_{i}").reshape(bsz, s, -1)
    xbc = _conv_silu_fwd(zx, d_inner, cw, cb, name=f"ssm_conv_{i}")
    dtr = zx[:, :, d_inner + conv_dim:].reshape(bsz, s, g_n, LANES)[..., :SSM_HEADS_PER_GROUP]
    dtr_row = jnp.pad(jnp.transpose(dtr, (0, 2, 3, 1)),
                      ((0, 0), (0, 0), (0, SSM_ROWS - SSM_HEADS_PER_GROUP), (0, 0)))
    gn, states = _ssd_fwd(zx, xbc, dtr_row, plane, psub, d_ch, nw, name=f"ssd_{i}")
    gn2 = gn.reshape(h.shape[0], -1)
    h1 = _mm(gn2, w_out, add=h, name=f"ssm_out_{i}")
    return h1, (h, hn, zx, xbc, dtr_row, states, gn2)


def _ssm_bwd(dh, saved, gain, w_in_big, cw, cb, plane, psub, d_ch, nw, w_out, i):
    h, hn, zx, xbc, dtr_row, states, gn2 = saved
    t = h.shape[0]
    bsz, s, _ = zx.shape
    d_inner = d_ch.shape[1]
    d_out = _mm(gn2, dh, ta=True, name=f"ssm_out_dw_{i}")
    dgn = _mm(dh, w_out, tb=True, name=f"ssm_out_dx_{i}").reshape(bsz, s, -1)
    dxs, dbm, dcm, dz, ddtr, ach, aln = _ssd_bwd(zx, xbc, dtr_row, plane, psub, d_ch, nw, states, dgn,
                                                  name=f"ssd_bwd_{i}")
    dxbc, dcw, dcb = _conv_silu_bwd(zx, d_inner, cw, cb, [dxs, dbm, dcm], name=f"ssm_conv_bwd_{i}")
    d_in_parts, dhn, col = [], None, 0
    for tag, part in (("z", dz), ("xbc", dxbc), ("dt", ddtr)):
        part = part.reshape(t, -1)
        d_in_parts.append(_mm(hn, part, ta=True, name=f"ssm_in_dw_{tag}_{i}"))
        dhn = _mm(part, w_in_big[:, col:col + part.shape[1]], tb=True, add=dhn,
                  name=f"ssm_in_dx_{tag}_{i}")
        col += part.shape[1]
    d_in_big = jnp.concatenate(d_in_parts, axis=1)
    dh, dgain = _rmsnorm_bwd(h, gain, dhn, dh, name=f"attn_norm_bwd_{i}")
    hpg = SSM_HEADS_PER_GROUP
    ach = jnp.sum(ach, axis=0)
    aln = jnp.sum(aln, axis=0)
    d_nw = ach[:, 0, :].reshape(-1)
    d_dskip = jnp.sum(ach[:, 1, :].reshape(-1, SSM_HEAD_DIM), axis=1)
    d_bias = aln[:, 0, :hpg].reshape(-1)
    d_a = aln[:, 1, :hpg].reshape(-1)
    return dh, dgain, d_in_big, dcw, dcb, d_bias, d_a, d_dskip, d_nw, d_out


def _sb_layer_fwd(h, bsz, s, gain, w_q, w_o, kv3, kvt, i):
    hn = _rmsnorm_fwd(h, gain, name=f"attn_norm_{i}")
    q3 = _mm(hn, w_q, out_dtype=BF16, name=f"sb_q_{i}").reshape(bsz, s, -1)
    o3, tot = _sb_fwd(q3, kv3, kvt, name=f"sb_attn_{i}")
    o2 = o3.reshape(h.shape[0], -1)
    h1 = _mm(o2, w_o, add=h, name=f"sb_o_{i}")
    return h1, (h, hn, q3, o2, tot)


def _sb_layer_bwd(dh, saved, gain, w_q, w_o, kv3, kvt, dk, dv, i):
    h, hn, q3, o2, tot = saved
    t = h.shape[0]
    d_o = _mm(o2, dh, ta=True, name=f"sb_o_dw_{i}")
    do3 = _mm(dh, w_o, tb=True, out_dtype=BF16, name=f"sb_o_dx_{i}").reshape(q3.shape)
    dq3, dk, dv = _sb_bwd(q3, kv3, kvt, do3, tot, dk, dv, name=f"sb_attn_bwd_{i}")
    dq = dq3.reshape(t, -1)
    d_q = _mm(hn, dq, ta=True, name=f"sb_q_dw_{i}")
    dhn = _mm(dq, w_q, tb=True, name=f"sb_q_dx_{i}")
    dh, dgain = _rmsnorm_bwd(h, gain, dhn, dh, name=f"attn_norm_bwd_{i}")
    return dh, dgain, d_q, d_o, dk, dv


def kernel(x, p, attn_norm, ffn_norm, ple_norm, ssm_in_proj, ssm_conv_w, ssm_conv_b, ssm_dt_bias, ssm_a_log, ssm_d, ssm_norm, ssm_out_proj, kv_norm, w_kv, w_q, w_o, ffn_up, ffn_conv_w, ffn_conv_b, ffn_down, ple_gate, ple_proj, final_norm, loss_target, m_attn_norm, m_ffn_norm, m_ple_norm, m_ssm_in_proj, m_ssm_conv_w, m_ssm_conv_b, m_ssm_dt_bias, m_ssm_a_log, m_ssm_d, m_ssm_norm, m_ssm_out_proj, m_kv_norm, m_w_kv, m_w_q, m_w_o, m_ffn_up, m_ffn_conv_w, m_ffn_conv_b, m_ffn_down, m_ple_gate, m_ple_proj, m_final_norm, v_attn_norm, v_ffn_norm, v_ple_norm, v_ssm_in_proj, v_ssm_conv_w, v_ssm_conv_b, v_ssm_dt_bias, v_ssm_a_log, v_ssm_d, v_ssm_norm, v_ssm_out_proj, v_kv_norm, v_w_kv, v_w_q, v_w_o, v_ffn_up, v_ffn_conv_w, v_ffn_conv_b, v_ffn_down, v_ple_gate, v_ple_proj, v_final_norm):
    given = dict(locals())
    wl = {n: given[n] for n in WEIGHTS}
    bsz, s, d = x.shape
    t = bsz * s
    depth = attn_norm.shape[0]
    n_a = ssm_in_proj.shape[0]
    d_inner = ssm_norm.shape[1] * N_CHIPS
    g_n = d_inner // SSM_GROUP_W
    cidx = lax.axis_index("c").astype(I32).reshape(1)
    chip_idx = (2 * lax.axis_index("x") + lax.axis_index("y")).astype(I32).reshape(1)

    local_shapes = [wl[n].shape for n in SHARDED]
    packed_w = _pack_rows([wl[n] for n in SHARDED], BF16)
    rows = packed_w.shape[0]
    rh = rows // 2
    gathered = _gather_weights(packed_w.reshape(2, rh, LANES)).reshape(N_CHIPS, rows, LANES)
    gathered = lax.dynamic_update_index_in_dim(gathered, packed_w, chip_idx[0], 0)
    full = {}
    for n, piece in zip(SHARDED, _unpack_rows(gathered, local_shapes)):
        ax = SHARD_AXIS[n]
        merged = piece.shape[1:ax + 1] + (N_CHIPS * piece.shape[ax + 1],) + piece.shape[ax + 2:]
        full[n] = jnp.moveaxis(piece, 0, ax).reshape(merged)

    h = x.reshape(t, d)
    tgt = loss_target.reshape(t, d)
    saved = []
    kv3 = kvt = hkv = h_kv_in = None
    consts = []
    for i in range(depth):
        if i < n_a:
            a_neg, plane, psub, d_ch = _ssm_consts(ssm_dt_bias[i], ssm_a_log[i], ssm_d[i], g_n)
            w_in_big = _ssm_in_big(full["ssm_in_proj"][i], d_inner, g_n)
            cw = full["ssm_conv_w"][i].astype(F32)
            cb = full["ssm_conv_b"][i].astype(F32)
            nw = full["ssm_norm"][i].astype(F32).reshape(1, -1)
            consts.append((a_neg, plane, psub, d_ch, w_in_big, cw, cb, nw))
            h, sv_mix = _ssm_fwd(h, bsz, s, attn_norm[i], w_in_big, cw, cb, plane, psub, d_ch, nw,
                                 full["ssm_out_proj"][i], i)
        else:
            j = i - n_a
            h, sv_mix = _sb_layer_fwd(h, bsz, s, attn_norm[i], full["w_q"][j], full["w_o"][j], kv3, kvt, i)
        fcw = full["ffn_conv_w"][i].astype(F32)
        h, sv_ffn = _ffn_fwd(h, bsz, s, ffn_norm[i], full["ffn_up"][i], fcw, ffn_conv_b[i],
                             full["ffn_down"][i], i)
        p_i = p[i].reshape(t, -1)
        h, sv_ple = _ple_layer_fwd(h, p_i, ple_norm[i], full["ple_gate"][i], full["ple_proj"][i], i)
        saved.append((sv_mix, sv_ffn, sv_ple))
        if i == n_a - 1:
            h_kv_in = h
            hkv = _rmsnorm_fwd(h, kv_norm, name="kv_norm")
            kv3 = _mm(hkv, full["w_kv"], out_dtype=BF16, name="kv_proj").reshape(bsz, s, -1)
            kvt = _kv_blocks_t(kv3)

    loss_local, dh, g_final = _final_loss(h, final_norm, tgt)
    gr = {n: [None] * wl[n].shape[0] for n in WEIGHTS if n not in ("kv_norm", "w_kv", "final_norm")}
    gr["final_norm"] = g_final
    dk = dv = None
    for i in reversed(range(depth)):
        sv_mix, sv_ffn, sv_ple = saved[i]
        if i == n_a - 1:
            dkv = jnp.concatenate([dk, dv], axis=-1).reshape(t, -1)
            gr["w_kv"] = _mm(hkv, dkv, ta=True, name="kv_proj_dw")
            dhkv = _mm(dkv, full["w_kv"], tb=True, name="kv_proj_dx")
            dh, gr["kv_norm"] = _rmsnorm_bwd(h_kv_in, kv_norm, dhkv, dh, name="kv_norm_bwd")
        p_i = p[i].reshape(t, -1)
        dh, gr["ple_norm"][i], gr["ple_gate"][i], gr["ple_proj"][i] = _ple_layer_bwd(
            dh, sv_ple, p_i, ple_norm[i], full["ple_gate"][i], i)
        fcw = full["ffn_conv_w"][i].astype(F32)
        (dh, gr["ffn_norm"][i], gr["ffn_up"][i], gr["ffn_conv_w"][i], gr["ffn_conv_b"][i],
         gr["ffn_down"][i]) = _ffn_bwd(dh, sv_ffn, ffn_norm[i], full["ffn_up"][i], fcw, ffn_conv_b[i],
                                       full["ffn_down"][i], i)
        if i < n_a:
            a_neg, plane, psub, d_ch, w_in_big, cw, cb, nw = consts[i]
            (dh, gr["attn_norm"][i], d_in_big, gr["ssm_conv_w"][i], gr["ssm_conv_b"][i],
             gr["ssm_dt_bias"][i], d_a, gr["ssm_d"][i], gr["ssm_norm"][i],
             gr["ssm_out_proj"][i]) = _ssm_bwd(dh, sv_mix, attn_norm[i], w_in_big, cw, cb, plane, psub,
                                               d_ch, nw, full["ssm_out_proj"][i], i)
            gr["ssm_in_proj"][i] = _ssm_in_small(d_in_big, g_n)
            gr["ssm_a_log"][i] = d_a * a_neg
        else:
            j = i - n_a
            dh, gr["attn_norm"][i], gr["w_q"][j], gr["w_o"][j], dk, dv = _sb_layer_bwd(
                dh, sv_mix, attn_norm[i], full["w_q"][j], full["w_o"][j], kv3, kvt, dk, dv, i)
    grad_x = dh.reshape(bsz, s, d)
    gfull = {n: (jnp.stack(v) if isinstance(v, list) else v) for n, v in gr.items()}

    by_chip = []
    for n in SHARDED:
        ax, shp = SHARD_AXIS[n], gfull[n].shape
        split = gfull[n].reshape(shp[:ax] + (N_CHIPS, shp[ax] // N_CHIPS) + shp[ax + 1:])
        by_chip.append(jnp.moveaxis(split, ax, 0))
    g4 = _pack_rows(by_chip, BF16, lead=1).reshape(N_CHIPS, 2, rh, LANES)
    from_sibling = _swap_halves(g4)
    chip_sums = _add_own_half(cidx, g4, from_sibling)
    from_chips = _scatter_chunks(chip_sums)
    my_half = _add_chips(chip_idx, chip_sums, from_chips)
    other_half = _share_half(my_half)
    low_core = cidx[0] == 0
    g_shard = jnp.concatenate([jnp.where(low_core, my_half, other_half),
                               jnp.where(low_core, other_half, my_half)], axis=0)

    rep_shapes = [wl[n].shape for n in REPLICATED]
    packed_r = _pack_rows([gfull[n] for n in REPLICATED], F32, unit=SLAB_ROW_ALIGN)
    g_rep = _sum_devices(_exchange_small(packed_r))

    grads = dict(zip(SHARDED, _unpack_rows(g_shard, local_shapes)))
    delta, new_m, new_v = {}, {}, {}
    for n in SHARDED:
        delta[n], new_m[n], new_v[n] = _adamw(wl[n], grads[n], given["m_" + n], given["v_" + n],
                                              name="adamw_" + n)
    slabs = [_pack_rows([src[pre + n] for n in REPLICATED], F32, unit=SLAB_ROW_ALIGN)
             for src, pre in ((wl, ""), (given, "m_"), (given, "v_"))]
    rep_out = _adamw(slabs[0], g_rep, slabs[1], slabs[2], name="adamw_replicated")
    for dst, slab in zip((grads, delta, new_m, new_v), [g_rep] + list(rep_out)):
        dst.update(zip(REPLICATED, _unpack_rows(slab, rep_shapes)))
    loss = lax.psum(loss_local, ("x", "y", "c"))
    return (loss, grad_x, *[grads[n] for n in WEIGHTS], *[delta[n] for n in WEIGHTS],
            *[new_m[n] for n in WEIGHTS], *[new_v[n] for n in WEIGHTS])
```

```python
import functools

import jax
import jax.numpy as jnp
from jax import lax
from jax.experimental import pallas as pl
from jax.experimental.pallas import tpu as pltpu

F32 = jnp.float32
BF16 = jnp.bfloat16
I32 = jnp.int32

NORM_EPS = 1e-6
SSM_NORM_EPS = 1e-5
SSM_HEAD_DIM = 64
SSM_STATE = 128
SSM_CHUNK = 128
SSM_HEADS_PER_GROUP = 8
SSM_GROUP_W = SSM_HEADS_PER_GROUP * SSM_HEAD_DIM
SSM_CONV = 4
SSM_ROWS = 16
SB_HEAD_DIM = 64
SB_BLOCK = 128
SB_SCALE = SB_HEAD_DIM ** -0.5
SB_Q_BLOCKS_FWD = 4
SB_Q_BLOCKS_BWD = 2
FFN_CONV = 3
LANES = 128
N_CHIPS = 4
N_DEV = 8

ADAM_LR = 0.001
ADAM_B1 = 0.9
ADAM_B2 = 0.999
ADAM_EPS = 1e-08
ADAM_WD = 0.01
ADAM_STEP = 10

MESH = pl.DeviceIdType.MESH
ANY = pl.BlockSpec(memory_space=pl.ANY)

SHARD_AXIS = {
    "ssm_in_proj": 2, "ssm_conv_w": 2, "ssm_conv_b": 1, "ssm_norm": 1, "ssm_out_proj": 1,
    "w_kv": 1, "w_q": 1, "w_o": 1, "ffn_up": 2, "ffn_conv_w": 2, "ffn_down": 1,
    "ple_gate": 1, "ple_proj": 2,
}
REPLICATED = ["attn_norm", "ffn_norm", "ple_norm", "ssm_dt_bias", "ssm_a_log", "ssm_d",
              "kv_norm", "ffn_conv_b", "final_norm"]
WEIGHTS = ["attn_norm", "ffn_norm", "ple_norm", "ssm_in_proj", "ssm_conv_w", "ssm_conv_b",
           "ssm_dt_bias", "ssm_a_log", "ssm_d", "ssm_norm", "ssm_out_proj", "kv_norm", "w_kv",
           "w_q", "w_o", "ffn_up", "ffn_conv_w", "ffn_conv_b", "ffn_down", "ple_gate",
           "ple_proj", "final_norm"]
SHARDED = [n for n in WEIGHTS if n in SHARD_AXIS]
PACK_ROWS = 2048
SLAB_ROW_ALIGN = 16
BIG_WEIGHT = 1 << 17


def _tile(n, pref):
    t = (min(pref, n) // 128) * 128
    while t >= 128:
        if n % t == 0:
            return t
        t -= 128
    return n


def _dot(a, b):
    return jnp.dot(a, b, preferred_element_type=F32)


def _dot_nt(a, b):
    return lax.dot_general(a, b, (((1,), (1,)), ((), ())), preferred_element_type=F32)


def _dot_tn(a, b):
    return lax.dot_general(a, b, (((0,), (0,)), ((), ())), preferred_element_type=F32)


def _split2(x):
    hi = x.astype(BF16)
    lo = (x - hi.astype(F32)).astype(BF16)
    return hi, lo


def _dot2(x, m):
    hi, lo = _split2(x)
    return _dot(hi, m) + _dot(lo, m)


def _dot2_left(m, x):
    hi, lo = _split2(x)
    return _dot(m, hi) + _dot(m, lo)


def _softplus(x):
    return jnp.maximum(x, 0.0) + jnp.log(1.0 + jnp.exp(-jnp.abs(x)))


def _sigmoid(x):
    return 0.5 * jnp.tanh(0.5 * x) + 0.5


def _params(*sem):
    return pltpu.CompilerParams(dimension_semantics=sem)


MM_VMEM_BUDGET = 36 * 1024 * 1024
MM_FULL_K = 2816


def _mm_tiles(m, n, k, sa, sb, so, has_add):
    tk = k if k <= MM_FULL_K else _tile(k, 1024)
    tn = _tile(n, 1408)
    tm = _tile(m, 1408)

    def need(tm_):
        return (2 * tm_ * tk * sa + 2 * tk * tn * sb + tm_ * tn * 4 + 2 * tm_ * tn * so
                + (2 * tm_ * tn * 4 if has_add else 0))

    while need(tm) > MM_VMEM_BUDGET and tm % 256 == 0:
        tm //= 2
    return tm, tn, tk


def _mm(a, b, *, name, ta=False, tb=False, add=None, out_dtype=F32):
    m = a.shape[1] if ta else a.shape[0]
    k = a.shape[0] if ta else a.shape[1]
    n = b.shape[0] if tb else b.shape[1]
    assert (b.shape[1] if tb else b.shape[0]) == k, (a.shape, b.shape, ta, tb)
    tm, tn, tk = _mm_tiles(m, n, k, a.dtype.itemsize, b.dtype.itemsize,
                           jnp.dtype(out_dtype).itemsize, add is not None)
    nk = k // tk
    dims = (((0 if ta else 1,), (1 if tb else 0,)), ((), ()))
    has_add = add is not None

    def body(*refs):
        if has_add:
            a_ref, b_ref, add_ref, o_ref, acc_ref = refs
        else:
            a_ref, b_ref, o_ref, acc_ref = refs
        kk = pl.program_id(2)

        @pl.when(kk == 0)
        def _():
            acc_ref[...] = jnp.zeros_like(acc_ref)

        acc_ref[...] += lax.dot_general(a_ref[...].astype(BF16), b_ref[...].astype(BF16), dims,
                                        preferred_element_type=F32)

        @pl.when(kk == nk - 1)
        def _():
            r = acc_ref[...]
            if has_add:
                r = r + add_ref[...].astype(F32)
            o_ref[...] = r.astype(out_dtype)

    a_spec = (pl.BlockSpec((tk, tm), lambda i, j, kk: (kk, i)) if ta
              else pl.BlockSpec((tm, tk), lambda i, j, kk: (i, kk)))
    b_spec = (pl.BlockSpec((tn, tk), lambda i, j, kk: (j, kk)) if tb
              else pl.BlockSpec((tk, tn), lambda i, j, kk: (kk, j)))
    o_spec = pl.BlockSpec((tm, tn), lambda i, j, kk: (i, j))
    in_specs = [a_spec, b_spec] + ([o_spec] if has_add else [])
    args = (a, b) + ((add,) if has_add else ())
    return pl.pallas_call(
        body, name=name, grid=(m // tm, n // tn, nk), in_specs=in_specs, out_specs=o_spec,
        out_shape=jax.ShapeDtypeStruct((m, n), out_dtype),
        scratch_shapes=[pltpu.VMEM((tm, tn), F32)],
        compiler_params=_params("parallel", "parallel", "arbitrary"),
    )(*args)


def _rmsnorm_fwd(x, gain, *, name, rows=512):
    t, d = x.shape
    tr = _tile(t, rows)

    def body(x_ref, g_ref, o_ref):
        xv = x_ref[...]
        r = lax.rsqrt(jnp.mean(xv * xv, axis=-1, keepdims=True) + NORM_EPS)
        o_ref[...] = ((xv * r) * g_ref[...]).astype(BF16)

    return pl.pallas_call(
        body, name=name, grid=(t // tr,),
        in_specs=[pl.BlockSpec((tr, d), lambda i: (i, 0)), pl.BlockSpec((1, d), lambda i: (0, 0))],
        out_specs=pl.BlockSpec((tr, d), lambda i: (i, 0)),
        out_shape=jax.ShapeDtypeStruct((t, d), BF16),
        compiler_params=_params("parallel"),
    )(x, gain.reshape(1, d))


def _rmsnorm_bwd(x, gain, dy, dres, *, name, rows=512):
    t, d = x.shape
    tr = _tile(t, rows)

    def body(x_ref, g_ref, dy_ref, dres_ref, dx_ref, dg_ref):
        xv = x_ref[...]
        r = lax.rsqrt(jnp.mean(xv * xv, axis=-1, keepdims=True) + NORM_EPS)
        xh = xv * r
        dyv = dy_ref[...].astype(F32)
        dxh = dyv * g_ref[...]
        dx = r * (dxh - xh * jnp.mean(dxh * xh, axis=-1, keepdims=True))
        dx_ref[...] = dres_ref[...] + dx
        part = jnp.sum(dyv * xh, axis=0, keepdims=True)

        @pl.when(pl.program_id(0) == 0)
        def _():
            dg_ref[...] = part

        @pl.when(pl.program_id(0) > 0)
        def _():
            dg_ref[...] += part

    row = pl.BlockSpec((tr, d), lambda i: (i, 0))
    vec = pl.BlockSpec((1, d), lambda i: (0, 0))
    dx, dg = pl.pallas_call(
        body, name=name, grid=(t // tr,), in_specs=[row, vec, row, row], out_specs=[row, vec],
        out_shape=[jax.ShapeDtypeStruct((t, d), F32), jax.ShapeDtypeStruct((1, d), F32)],
        compiler_params=_params("arbitrary"),
    )(x, gain.reshape(1, d), dy, dres)
    return dx, dg.reshape(d)


def _final_loss(h, gain, target, *, rows=512):
    t, d = h.shape
    tr = _tile(t, rows)

    def body(x_ref, g_ref, tg_ref, dx_ref, dg_ref, loss_ref):
        xv = x_ref[...]
        g = g_ref[...]
        r = lax.rsqrt(jnp.mean(xv * xv, axis=-1, keepdims=True) + NORM_EPS)
        xh = xv * r
        err = xh * g - tg_ref[...]
        dyv = err * (1.0 / d)
        dxh = dyv * g
        dx_ref[...] = r * (dxh - xh * jnp.mean(dxh * xh, axis=-1, keepdims=True))
        part = jnp.sum(dyv * xh, axis=0, keepdims=True)
        lpart = jnp.zeros((1, LANES), F32) + (0.5 / d) * jnp.sum(err * err)

        @pl.when(pl.program_id(0) == 0)
        def _():
            dg_ref[...] = part
            loss_ref[...] = lpart

        @pl.when(pl.program_id(0) > 0)
        def _():
            dg_ref[...] += part
            loss_ref[...] += lpart

    row = pl.BlockSpec((tr, d), lambda i: (i, 0))
    vec = pl.BlockSpec((1, d), lambda i: (0, 0))
    dx, dg, loss = pl.pallas_call(
        body, name="final_loss", grid=(t // tr,), in_specs=[row, vec, row],
        out_specs=[row, vec, pl.BlockSpec((1, LANES), lambda i: (0, 0))],
        out_shape=[jax.ShapeDtypeStruct((t, d), F32), jax.ShapeDtypeStruct((1, d), F32),
                   jax.ShapeDtypeStruct((1, LANES), F32)],
        compiler_params=_params("arbitrary"),
    )(h, gain.reshape(1, d), target)
    return loss[0, 0], dx, dg.reshape(d)


def _ple_fwd(h, a, pp, *, name, rows=512):
    t, d = h.shape
    tr = _tile(t, rows)

    def body(h_ref, a_ref, p_ref, o_ref):
        o_ref[...] = h_ref[...] + _sigmoid(a_ref[...]) * p_ref[...]

    row = pl.BlockSpec((tr, d), lambda i: (i, 0))
    return pl.pallas_call(
        body, name=name, grid=(t // tr,), in_specs=[row, row, row], out_specs=row,
        out_shape=jax.ShapeDtypeStruct((t, d), F32), compiler_params=_params("parallel"),
    )(h, a, pp)


def _ple_bwd(dh, a, pp, *, name, rows=512):
    t, d = dh.shape
    tr = _tile(t, rows)

    def body(dh_ref, a_ref, p_ref, da_ref, dp_ref):
        s = _sigmoid(a_ref[...])
        dhv = dh_ref[...]
        da_ref[...] = (dhv * p_ref[...] * (s * (1.0 - s))).astype(BF16)
        dp_ref[...] = (dhv * s).astype(BF16)

    row = pl.BlockSpec((tr, d), lambda i: (i, 0))
    return pl.pallas_call(
        body, name=name, grid=(t // tr,), in_specs=[row, row, row], out_specs=[row, row],
        out_shape=[jax.ShapeDtypeStruct((t, d), BF16)] * 2, compiler_params=_params("parallel"),
    )(dh, a, pp)


CONV_ROWS = 64
CONV_HALO = 8


def _conv_window(ref, r0, with_prev, with_next):
    s = ref.shape[1]
    parts = []
    if with_prev:
        prev = ref[0, pl.ds(pl.multiple_of(jnp.maximum(r0 - CONV_HALO, 0), CONV_HALO), CONV_HALO), :]
        parts.append(jnp.where(r0 > 0, prev, 0.0))
    parts.append(ref[0, pl.ds(r0, CONV_ROWS), :])
    if with_next:
        nxt = pl.multiple_of(jnp.minimum(r0 + CONV_ROWS, s - CONV_HALO), CONV_HALO)
        parts.append(ref[0, pl.ds(nxt, CONV_HALO), :])
    return jnp.concatenate(parts, axis=0)


def _conv_taps(win, kw, n):
    return [win[CONV_HALO - (kw - 1 - k):CONV_HALO - (kw - 1 - k) + n] for k in range(kw)]


def _conv_apply(taps, wv, bv):
    pre = bv + wv[0:1, :] * taps[0]
    for k in range(1, len(taps)):
        pre = pre + wv[k:k + 1, :] * taps[k]
    return pre


def _rows8(x):
    acc = x[0:8]
    for i in range(1, x.shape[0] // 8):
        acc = acc + x[8 * i:8 * i + 8]
    return acc


def _conv_grad_step(dpre_ext, taps, wv, is_last):
    kw = wv.shape[0]
    halo = jnp.where(is_last, 0.0, dpre_ext[CONV_ROWS:])
    dpre_ext = jnp.concatenate([dpre_ext[:CONV_ROWS], halo], axis=0)
    dpre = dpre_ext[:CONV_ROWS]
    du = wv[kw - 1:kw, :] * dpre
    for k in range(kw - 1):
        du = du + wv[k:k + 1, :] * dpre_ext[kw - 1 - k:kw - 1 - k + CONV_ROWS]
    sums = [_rows8(dpre * taps[k][:CONV_ROWS]) for k in range(kw)] + [_rows8(dpre)]
    return du, sums


def _conv_store_sums(sums, dw_ref, db_ref, first):
    kw = len(sums) - 1
    vals = [jnp.sum(s_, axis=0, keepdims=True) for s_ in sums]

    @pl.when(first)
    def _():
        for k in range(kw):
            dw_ref[k:k + 1, :] = vals[k]
        db_ref[...] = vals[kw]

    @pl.when(jnp.logical_not(first))
    def _():
        for k in range(kw):
            dw_ref[k:k + 1, :] += vals[k]
        db_ref[...] += vals[kw]


def _dsilu(pre):
    s = _sigmoid(pre)
    return s, s * (1.0 + pre * (1.0 - s))


def _conv_silu_fwd(zx, off, w, b, *, name, tc=128):
    bsz, s, _ = zx.shape
    kw, c = w.shape
    o0 = off // tc

    def body(u_ref, w_ref, b_ref, o_ref):
        wv, bv = w_ref[...], b_ref[...]

        def step(i, carry):
            r0 = pl.multiple_of(i * CONV_ROWS, CONV_ROWS)
            taps = _conv_taps(_conv_window(u_ref, r0, True, False), kw, CONV_ROWS)
            pre = _conv_apply(taps, wv, bv)
            o_ref[0, pl.ds(r0, CONV_ROWS), :] = pre * _sigmoid(pre)
            return carry

        lax.fori_loop(0, s // CONV_ROWS, step, 0)

    return pl.pallas_call(
        body, name=name, grid=(bsz, c // tc),
        in_specs=[pl.BlockSpec((1, s, tc), lambda i, j: (i, 0, o0 + j)),
                  pl.BlockSpec((kw, tc), lambda i, j: (0, j)),
                  pl.BlockSpec((1, tc), lambda i, j: (0, j))],
        out_specs=pl.BlockSpec((1, s, tc), lambda i, j: (i, 0, j)),
        out_shape=jax.ShapeDtypeStruct((bsz, s, c), F32),
        compiler_params=_params("parallel", "parallel"),
    )(zx, w, b.reshape(1, c))


def _conv_silu_bwd(zx, off, w, b, douts, *, name, tc=128):
    bsz, s, _ = zx.shape
    kw, c = w.shape
    o0 = off // tc
    counts = [d.shape[2] // tc for d in douts]
    starts = [sum(counts[:k]) for k in range(len(douts))]
    assert sum(counts) == c // tc

    def body(u_ref, w_ref, b_ref, *rest):
        dy_refs = rest[:len(douts)]
        du_ref, dw_ref, db_ref = rest[len(douts):]
        j = pl.program_id(0)
        wv, bv = w_ref[...], b_ref[...]
        n = CONV_ROWS + CONV_HALO

        def step(i, sums):
            r0 = pl.multiple_of(i * CONV_ROWS, CONV_ROWS)
            taps = _conv_taps(_conv_window(u_ref, r0, True, True), kw, n)
            _, ds = _dsilu(_conv_apply(taps, wv, bv))
            dy = _conv_window(dy_refs[0], r0, False, True)
            for k in range(1, len(douts)):
                dy = jnp.where(j >= starts[k], _conv_window(dy_refs[k], r0, False, True), dy)
            du, new = _conv_grad_step(dy * ds, taps, wv, r0 + CONV_ROWS >= s)
            du_ref[0, pl.ds(r0, CONV_ROWS), :] = du.astype(BF16)
            return tuple(a + b_ for a, b_ in zip(sums, new))

        zero = tuple(jnp.zeros((8, tc), F32) for _ in range(kw + 1))
        sums = lax.fori_loop(0, s // CONV_ROWS, step, zero)
        _conv_store_sums(sums, dw_ref, db_ref, pl.program_id(1) == 0)

    def part_spec(k):
        return pl.BlockSpec((1, s, tc), lambda j, i: (i, 0, jnp.clip(j - starts[k], 0, counts[k] - 1)))

    du, dw, db = pl.pallas_call(
        body, name=name, grid=(c // tc, bsz),
        in_specs=[pl.BlockSpec((1, s, tc), lambda j, i: (i, 0, o0 + j)),
                  pl.BlockSpec((kw, tc), lambda j, i: (0, j)),
                  pl.BlockSpec((1, tc), lambda j, i: (0, j))] + [part_spec(k) for k in range(len(douts))],
        out_specs=[pl.BlockSpec((1, s, tc), lambda j, i: (i, 0, j)),
                   pl.BlockSpec((kw, tc), lambda j, i: (0, j)),
                   pl.BlockSpec((1, tc), lambda j, i: (0, j))],
        out_shape=[jax.ShapeDtypeStruct((bsz, s, c), BF16), jax.ShapeDtypeStruct((kw, c), F32),
                   jax.ShapeDtypeStruct((1, c), F32)],
        compiler_params=_params("parallel", "arbitrary"),
    )(zx, w, b.reshape(1, c), *douts)
    return du, dw, db.reshape(c)


def _conv_glu_fwd(up, w, b, *, name, tc=128):
    bsz, s, c2 = up.shape
    kw = w.shape[0]
    f = c2 // 2
    nt = f // tc

    def body(ug_ref, uv_ref, wg_ref, wv_ref, bg_ref, bv_ref, o_ref):
        wg, wv, bg, bv = wg_ref[...], wv_ref[...], bg_ref[...], bv_ref[...]

        def step(i, carry):
            r0 = pl.multiple_of(i * CONV_ROWS, CONV_ROWS)
            pg = _conv_apply(_conv_taps(_conv_window(ug_ref, r0, True, False), kw, CONV_ROWS), wg, bg)
            pv = _conv_apply(_conv_taps(_conv_window(uv_ref, r0, True, False), kw, CONV_ROWS), wv, bv)
            o_ref[0, pl.ds(r0, CONV_ROWS), :] = (pg * _sigmoid(pg) * pv).astype(BF16)
            return carry

        lax.fori_loop(0, s // CONV_ROWS, step, 0)

    b2 = b.reshape(1, c2)
    return pl.pallas_call(
        body, name=name, grid=(bsz, nt),
        in_specs=[pl.BlockSpec((1, s, tc), lambda i, j: (i, 0, j)),
                  pl.BlockSpec((1, s, tc), lambda i, j: (i, 0, nt + j)),
                  pl.BlockSpec((kw, tc), lambda i, j: (0, j)),
                  pl.BlockSpec((kw, tc), lambda i, j: (0, nt + j)),
                  pl.BlockSpec((1, tc), lambda i, j: (0, j)),
                  pl.BlockSpec((1, tc), lambda i, j: (0, nt + j))],
        out_specs=pl.BlockSpec((1, s, tc), lambda i, j: (i, 0, j)),
        out_shape=jax.ShapeDtypeStruct((bsz, s, f), BF16),
        compiler_params=_params("parallel", "parallel"),
    )(up, up, w, w, b2, b2)


def _conv_glu_bwd(up, w, b, df, *, name, tc=128):
    bsz, s, c2 = up.shape
    kw = w.shape[0]
    f = c2 // 2
    nt = f // tc

    def body(ug_ref, uv_ref, wg_ref, wv_ref, bg_ref, bv_ref, df_ref,
             dug_ref, duv_ref, dwg_ref, dwv_ref, dbg_ref, dbv_ref):
        first = pl.program_id(1) == 0
        wg, wv, bg, bv = wg_ref[...], wv_ref[...], bg_ref[...], bv_ref[...]
        n = CONV_ROWS + CONV_HALO

        def step(i, sums):
            r0 = pl.multiple_of(i * CONV_ROWS, CONV_ROWS)
            is_last = r0 + CONV_ROWS >= s
            tg = _conv_taps(_conv_window(ug_ref, r0, True, True), kw, n)
            tv = _conv_taps(_conv_window(uv_ref, r0, True, True), kw, n)
            pg = _conv_apply(tg, wg, bg)
            pv = _conv_apply(tv, wv, bv)
            sig, dsl = _dsilu(pg)
            dfv = _conv_window(df_ref, r0, False, True)
            dug, new_g = _conv_grad_step(dfv * pv * dsl, tg, wg, is_last)
            duv, new_v = _conv_grad_step(dfv * (pg * sig), tv, wv, is_last)
            dug_ref[0, pl.ds(r0, CONV_ROWS), :] = dug.astype(BF16)
            duv_ref[0, pl.ds(r0, CONV_ROWS), :] = duv.astype(BF16)
            return tuple(a + b_ for a, b_ in zip(sums, new_g + new_v))

        zero = tuple(jnp.zeros((8, tc), F32) for _ in range(2 * (kw + 1)))
        sums = lax.fori_loop(0, s // CONV_ROWS, step, zero)
        _conv_store_sums(sums[:kw + 1], dwg_ref, dbg_ref, first)
        _conv_store_sums(sums[kw + 1:], dwv_ref, dbv_ref, first)

    b2 = b.reshape(1, c2)
    act = lambda j, i: (i, 0, j)
    wsp = pl.BlockSpec((kw, tc), lambda j, i: (0, j))
    bsp = pl.BlockSpec((1, tc), lambda j, i: (0, j))
    dug, duv, dwg, dwv, dbg, dbv = pl.pallas_call(
        body, name=name, grid=(nt, bsz),
        in_specs=[pl.BlockSpec((1, s, tc), act),
                  pl.BlockSpec((1, s, tc), lambda j, i: (i, 0, nt + j)),
                  wsp, pl.BlockSpec((kw, tc), lambda j, i: (0, nt + j)),
                  bsp, pl.BlockSpec((1, tc), lambda j, i: (0, nt + j)),
                  pl.BlockSpec((1, s, tc), act)],
        out_specs=[pl.BlockSpec((1, s, tc), act), pl.BlockSpec((1, s, tc), act), wsp, wsp, bsp, bsp],
        out_shape=[jax.ShapeDtypeStruct((bsz, s, f), BF16)] * 2
        + [jax.ShapeDtypeStruct((kw, f), F32)] * 2 + [jax.ShapeDtypeStruct((1, f), F32)] * 2,
        compiler_params=_params("parallel", "arbitrary"),
    )(up, up, w, w, b2, b2, df)
    return (dug, duv, jnp.concatenate([dwg, dwv], axis=1),
            jnp.concatenate([dbg.reshape(f), dbv.reshape(f)]))


def _ssd_shared(xs, bm, cm, dtc_raw, dtr_raw, plane, psub, st):
    cl = SSM_CHUNK
    bias_l, a_l = plane[0:1, :], plane[1:2, :]
    bias_s, a_s = psub[:, 0:1], psub[:, 1:2]
    ri = lax.broadcasted_iota(I32, (cl, cl), 0)
    ci = lax.broadcasted_iota(I32, (cl, cl), 1)
    tril = ri >= ci
    low_incl = tril.astype(BF16)
    up_incl = (ri <= ci).astype(BF16)
    seg_t = (lax.broadcasted_iota(I32, (LANES, SSM_GROUP_W), 0)
             == lax.broadcasted_iota(I32, (LANES, SSM_GROUP_W), 1) // SSM_HEAD_DIM).astype(BF16)
    dt_c = _softplus(dtc_raw + bias_l)
    cs_c = _dot2_left(low_incl, dt_c * a_l)
    dt_r = _softplus(dtr_raw + bias_s)
    cs_r = _dot2(dt_r * a_s, up_incl)
    dt_ch = _dot2(dt_c, seg_t)
    cs_ch = _dot2(cs_c, seg_t)
    cs_last = cs_ch[cl - 1:cl, :]
    decay_ch = jnp.exp(cs_ch)
    w_ch = jnp.exp(cs_last - cs_ch)
    tot_ch = jnp.exp(cs_last)
    xdt = xs * dt_ch
    bm_b, cm_b = bm.astype(BF16), cm.astype(BF16)
    gmat = _dot_nt(cm_b, bm_b)
    cst = _dot(cm_b, st.astype(BF16))
    yoff = decay_ch * cst
    return dict(tril=tril, low_incl=low_incl, up_incl=up_incl, seg_t=seg_t, a_l=a_l, bias_l=bias_l,
                dt_c=dt_c, cs_c=cs_c, cs_r=cs_r, dt_ch=dt_ch, decay_ch=decay_ch, w_ch=w_ch,
                tot_ch=tot_ch, xdt=xdt, bm_b=bm_b, cm_b=cm_b, gmat=gmat, yoff=yoff)


def _head_decay(q, r):
    diff = q["cs_c"][:, r:r + 1] - q["cs_r"][r:r + 1, :]
    return jnp.where(q["tril"], jnp.exp(jnp.minimum(diff, 0.0)), 0.0)


def _half_mask(hh):
    lane = lax.broadcasted_iota(I32, (SSM_CHUNK, LANES), 1)
    return (lane < SSM_HEAD_DIM) if hh == 0 else (lane >= SSM_HEAD_DIM)


def _ssd_ydiag(q):
    pairs = []
    for pr in range(SSM_HEADS_PER_GROUP // 2):
        xp = q["xdt"][:, pr * LANES:(pr + 1) * LANES]
        acc = None
        for hh in range(2):
            mm_ = (q["gmat"] * _head_decay(q, 2 * pr + hh)).astype(BF16)
            part = _dot(mm_, jnp.where(_half_mask(hh), xp, 0.0).astype(BF16))
            acc = part if acc is None else acc + part
        pairs.append(acc)
    return jnp.concatenate(pairs, axis=1)


def _ssd_specs(bsz, s, g_n, d_inner, rev):
    cl = SSM_CHUNK
    nc = s // cl
    cc = (lambda c: nc - 1 - c) if rev else (lambda c: c)
    gb = d_inner // LANES
    dt0 = (d_inner + d_inner + 2 * g_n * SSM_STATE) // LANES
    return dict(
        z=pl.BlockSpec((1, cl, SSM_GROUP_W), lambda b, g, c: (b, cc(c), g)),
        dtc=pl.BlockSpec((1, cl, LANES), lambda b, g, c: (b, cc(c), dt0 + g)),
        xs=pl.BlockSpec((1, cl, SSM_GROUP_W), lambda b, g, c: (b, cc(c), g)),
        bm=pl.BlockSpec((1, cl, LANES), lambda b, g, c: (b, cc(c), gb + g)),
        cm=pl.BlockSpec((1, cl, LANES), lambda b, g, c: (b, cc(c), gb + g_n + g)),
        dtr=pl.BlockSpec((1, 1, SSM_ROWS, cl), lambda b, g, c: (b, g, 0, cc(c))),
        plane=pl.BlockSpec((1, 8, LANES), lambda b, g, c: (g, 0, 0)),
        psub=pl.BlockSpec((1, SSM_ROWS, LANES), lambda b, g, c: (g, 0, 0)),
        chan=pl.BlockSpec((1, SSM_GROUP_W), lambda b, g, c: (0, g)),
        state=pl.BlockSpec((1, 1, 1, SSM_STATE, SSM_GROUP_W), lambda b, g, c: (b, g, cc(c), 0, 0)),
        bgrp=pl.BlockSpec((1, cl, LANES), lambda b, g, c: (b, cc(c), g)),
    )


def _ssd_fwd(zx, xbc, dtr_row, plane, psub, d_ch, nw, *, name):
    bsz, s, _ = zx.shape
    d_inner = d_ch.shape[1]
    g_n = d_inner // SSM_GROUP_W
    nc = s // SSM_CHUNK
    sp = _ssd_specs(bsz, s, g_n, d_inner, False)

    def body(z_ref, dtc_ref, xs_ref, bm_ref, cm_ref, dtr_ref, plane_ref, psub_ref, d_ref, nw_ref,
             gn_ref, st_out_ref, st_ref):
        @pl.when(pl.program_id(2) == 0)
        def _():
            st_ref[...] = jnp.zeros_like(st_ref)

        xs = xs_ref[0]
        st = st_ref[...]
        st_out_ref[0, 0, 0] = st
        q = _ssd_shared(xs, bm_ref[0], cm_ref[0], dtc_ref[0], dtr_ref[0, 0], plane_ref[0],
                        psub_ref[0], st)
        y = _ssd_ydiag(q) + q["yoff"] + xs * d_ref[...]
        st_ref[...] = q["tot_ch"] * st + _dot_tn(q["bm_b"], (q["w_ch"] * q["xdt"]).astype(BF16))
        zv = z_ref[0]
        gy = y * (zv * _sigmoid(zv))
        rstd = lax.rsqrt(jnp.mean(gy * gy, axis=-1, keepdims=True) + SSM_NORM_EPS)
        gn_ref[0] = ((gy * rstd) * nw_ref[...]).astype(BF16)

    return pl.pallas_call(
        body, name=name, grid=(bsz, g_n, nc),
        in_specs=[sp["z"], sp["dtc"], sp["xs"], sp["bm"], sp["cm"], sp["dtr"], sp["plane"],
                  sp["psub"], sp["chan"], sp["chan"]],
        out_specs=[sp["z"], sp["state"]],
        out_shape=[jax.ShapeDtypeStruct((bsz, s, d_inner), BF16),
                   jax.ShapeDtypeStruct((bsz, g_n, nc, SSM_STATE, SSM_GROUP_W), F32)],
        scratch_shapes=[pltpu.VMEM((SSM_STATE, SSM_GROUP_W), F32)],
        compiler_params=_params("parallel", "parallel", "arbitrary"),
    )(zx, zx, xbc, xbc, xbc, dtr_row, plane, psub, d_ch, nw)


def _ssd_bwd(zx, xbc, dtr_row, plane, psub, d_ch, nw, states, dgn, *, name):
    bsz, s, _ = zx.shape
    d_inner = d_ch.shape[1]
    g_n = d_inner // SSM_GROUP_W
    cl = SSM_CHUNK
    nc = s // cl
    sp = _ssd_specs(bsz, s, g_n, d_inner, True)
    acc_ch = pl.BlockSpec((1, 1, 8, SSM_GROUP_W), lambda b, g, c: (b, g, 0, 0))
    acc_ln = pl.BlockSpec((1, 1, 8, LANES), lambda b, g, c: (b, g, 0, 0))

    def body(z_ref, dtc_ref, xs_ref, bm_ref, cm_ref, dtr_ref, plane_ref, psub_ref, d_ref, nw_ref,
             st_in_ref, dgn_ref,
             dxs_ref, dbm_ref, dcm_ref, dz_ref, ddt_ref, ach_ref, aln_ref, dst_ref):
        first = pl.program_id(2) == 0

        @pl.when(first)
        def _():
            dst_ref[...] = jnp.zeros_like(dst_ref)
            ach_ref[...] = jnp.zeros_like(ach_ref)
            aln_ref[...] = jnp.zeros_like(aln_ref)

        xs = xs_ref[0]
        st = st_in_ref[0, 0, 0]
        q = _ssd_shared(xs, bm_ref[0], cm_ref[0], dtc_ref[0], dtr_ref[0, 0], plane_ref[0],
                        psub_ref[0], st)
        d_chv = d_ref[...]
        nwv = nw_ref[...]
        y = _ssd_ydiag(q) + q["yoff"] + xs * d_chv
        zv = z_ref[0]
        sz = _sigmoid(zv)
        silu_z = zv * sz
        gy = y * silu_z
        rstd = lax.rsqrt(jnp.mean(gy * gy, axis=-1, keepdims=True) + SSM_NORM_EPS)
        gyh = gy * rstd
        dgnv = dgn_ref[0]
        dgyh = dgnv * nwv
        dgy = rstd * (dgyh - gyh * jnp.mean(dgyh * gyh, axis=-1, keepdims=True))
        dy = dgy * silu_z
        dz_ref[0] = (dgy * y * (sz * (1.0 + zv * (1.0 - sz)))).astype(BF16)
        ach_ref[0, 0, 0:1, :] += jnp.sum(dgnv * gyh, axis=0, keepdims=True)
        ach_ref[0, 0, 1:2, :] += jnp.sum(dy * xs, axis=0, keepdims=True)
        st_b = st.astype(BF16)
        dyd = (dy * q["decay_ch"]).astype(BF16)
        dcm = _dot_nt(dyd, st_b)
        dstn = dst_ref[...]
        dstn_b = dstn.astype(BF16)
        bds = _dot(q["bm_b"], dstn_b)
        wx = q["w_ch"] * q["xdt"]
        dbm = _dot_nt(wx.astype(BF16), dstn_b)
        dst_ref[...] = q["tot_ch"] * dstn + _dot_tn(q["cm_b"], dyd)
        vterm = wx * bds
        cs_terms = dy * q["yoff"] - vterm
        last_ch = q["tot_ch"] * jnp.sum(dstn * st, axis=0, keepdims=True) + jnp.sum(vterm, axis=0, keepdims=True)
        lane = lax.broadcasted_iota(I32, (cl, LANES), 1)
        rowi = lax.broadcasted_iota(I32, (SSM_ROWS, cl), 0)
        dg_sum = jnp.zeros((cl, cl), F32)
        dcs_col = jnp.zeros((cl, LANES), F32)
        dcs_row = jnp.zeros((SSM_ROWS, cl), F32)
        dxdt_pairs = []
        for pr in range(SSM_HEADS_PER_GROUP // 2):
            xp_b = q["xdt"][:, pr * LANES:(pr + 1) * LANES].astype(BF16)
            dyp = dy[:, pr * LANES:(pr + 1) * LANES]
            acc = None
            for hh in range(2):
                r = 2 * pr + hh
                dm = _head_decay(q, r)
                mmat = q["gmat"] * dm
                dym = jnp.where(_half_mask(hh), dyp, 0.0).astype(BF16)
                dmat = jnp.where(q["tril"], _dot_nt(dym, xp_b), 0.0)
                part = _dot_tn(mmat.astype(BF16), dym)
                acc = part if acc is None else acc + part
                dg_sum = dg_sum + dmat * dm
                e = dmat * mmat
                dcs_col = dcs_col + jnp.where(lane == r, jnp.sum(e, axis=1, keepdims=True), 0.0)
                dcs_row = dcs_row + jnp.where(rowi == r, jnp.sum(e, axis=0, keepdims=True), 0.0)
            dxdt_pairs.append(acc)
        dg_b = dg_sum.astype(BF16)
        dcm_ref[0] = dcm + _dot(dg_b, q["bm_b"])
        dbm_ref[0] = dbm + _dot_tn(dg_b, q["cm_b"])
        dxdt = q["w_ch"] * bds + jnp.concatenate(dxdt_pairs, axis=1)
        dxs_ref[0] = dy * d_chv + dxdt * q["dt_ch"]
        seg = (lax.broadcasted_iota(I32, (SSM_GROUP_W, LANES), 0) // SSM_HEAD_DIM
               == lax.broadcasted_iota(I32, (SSM_GROUP_W, LANES), 1)).astype(BF16)
        row_as_col = jnp.transpose(jnp.concatenate(
            [dcs_row, jnp.zeros((cl - SSM_ROWS, cl), F32)], axis=0))
        dcs = dcs_col - row_as_col + _dot2(cs_terms, seg)
        last = _dot2(jnp.zeros((8, SSM_GROUP_W), F32) + last_ch, seg)[0:1, :]
        da = _dot2_left(q["up_incl"], dcs) + last
        ddt = _dot2(dxdt * xs, seg) + da * q["a_l"]
        ddtr = ddt * _sigmoid(dtc_ref[0] + q["bias_l"])
        ddt_ref[0] = ddtr.astype(BF16)
        aln_ref[0, 0, 0:1, :] += jnp.sum(ddtr, axis=0, keepdims=True)
        aln_ref[0, 0, 1:2, :] += jnp.sum(da * q["dt_c"], axis=0, keepdims=True)

    outs = pl.pallas_call(
        body, name=name, grid=(bsz, g_n, nc),
        in_specs=[sp["z"], sp["dtc"], sp["xs"], sp["bm"], sp["cm"], sp["dtr"], sp["plane"],
                  sp["psub"], sp["chan"], sp["chan"], sp["state"], sp["z"]],
        out_specs=[sp["z"], sp["bgrp"], sp["bgrp"], sp["z"], sp["bgrp"], acc_ch, acc_ln],
        out_shape=[jax.ShapeDtypeStruct((bsz, s, d_inner), F32),
                   jax.ShapeDtypeStruct((bsz, s, g_n * SSM_STATE), F32),
                   jax.ShapeDtypeStruct((bsz, s, g_n * SSM_STATE), F32),
                   jax.ShapeDtypeStruct((bsz, s, d_inner), BF16),
                   jax.ShapeDtypeStruct((bsz, s, g_n * LANES), BF16),
                   jax.ShapeDtypeStruct((bsz, g_n, 8, SSM_GROUP_W), F32),
                   jax.ShapeDtypeStruct((bsz, g_n, 8, LANES), F32)],
        scratch_shapes=[pltpu.VMEM((SSM_STATE, SSM_GROUP_W), F32)],
        compiler_params=_params("parallel", "parallel", "arbitrary"),
    )(zx, zx, xbc, xbc, xbc, dtr_row, plane, psub, d_ch, nw, states, dgn)
    return outs


def _sb_stack(x):
    out = []
    for i in range(x.shape[0] // SB_BLOCK):
        xb = x[i * SB_BLOCK:(i + 1) * SB_BLOCK]
        lane = lax.broadcasted_iota(I32, xb.shape, 1)
        zero = jnp.zeros_like(xb)
        out += [jnp.where(lane < SB_HEAD_DIM, xb, zero), jnp.where(lane >= SB_HEAD_DIM, xb, zero)]
    return jnp.concatenate(out, axis=0)


def _sb_unstack_t(acc_t):
    row = lax.broadcasted_iota(I32, (LANES, SB_BLOCK), 0)
    out = []
    for i in range(acc_t.shape[1] // (2 * SB_BLOCK)):
        a = acc_t[:, 2 * i * SB_BLOCK:(2 * i + 1) * SB_BLOCK]
        b = acc_t[:, (2 * i + 1) * SB_BLOCK:(2 * i + 2) * SB_BLOCK]
        out.append(jnp.transpose(jnp.where(row < SB_HEAD_DIM, a, b)))
    return jnp.concatenate(out, axis=0)


def _sb_tile_blocks(nq, q_blocks):
    nb = 4 if nq % 4 == 0 else (2 if nq % 2 == 0 else 1)
    return nb, min(nb, q_blocks)


def _sb_valid(u, qi0, nb, nqb):
    shape = (nb * SB_BLOCK, nqb * 2 * SB_BLOCK)
    key = u * (nb * SB_BLOCK) + lax.broadcasted_iota(I32, shape, 0)
    col = lax.broadcasted_iota(I32, shape, 1)
    qpos = (qi0 + col // (2 * SB_BLOCK)) * SB_BLOCK + col % SB_BLOCK
    return key < qpos


def _sb_logits(kb, qs, valid):
    z = _dot_nt(kb, qs)
    lb = jnp.minimum(z, 0.0) - jnp.log(1.0 + jnp.exp(-jnp.abs(z)))
    lk_all = lb - z
    lk = lk_all if valid is None else jnp.where(valid, lk_all, 0.0)
    return z, lb, lk_all, lk


def _sb_scan(tri2, x, nb, reverse, exact=True):
    blk = SB_BLOCK
    edge = 0 if reverse else blk - 1
    carry = jnp.zeros((1, x.shape[1]), F32)
    res = [None] * nb
    for i in (reversed(range(nb)) if reverse else range(nb)):
        part = x[i * blk:(i + 1) * blk]
        if exact:
            hi, lo = _split2(part)
            raw = _dot(tri2, jnp.concatenate([hi, lo], axis=0))
        else:
            raw = _dot(tri2[:, :blk], part.astype(BF16))
        res[i] = raw + carry
        carry = carry + (raw[edge:edge + 1] + part[edge:edge + 1])
    return jnp.concatenate(res, axis=0), carry


def _sb_fwd(q, kv, kvt, *, name):
    bsz, s, w = q.shape
    blk = SB_BLOCK
    npair = w // LANES
    nq = s // blk
    nb, nqb = _sb_tile_blocks(nq, SB_Q_BLOCKS_FWD)
    width = nqb * 2 * blk

    def body(q_ref, k_ref, vt_ref, o_ref, tot_ref):
        qi0 = pl.program_id(2) * nqb
        qs = _sb_stack(q_ref[0] * SB_SCALE)
        ri = lax.broadcasted_iota(I32, (blk, blk), 0)
        ci = lax.broadcasted_iota(I32, (blk, blk), 1)
        upper = (ri < ci).astype(BF16)
        tri2 = jnp.concatenate([upper, upper], axis=1)

        def tile(u, r, acc, masked):
            rows = pl.ds(pl.multiple_of(u * (nb * blk), nb * blk), nb * blk)
            valid = _sb_valid(u, qi0, nb, nqb) if masked else None
            _, lb, _, lk = _sb_logits(k_ref[0, rows, :], qs, valid)
            sfx, total = _sb_scan(tri2, lk, nb, True)
            wgt = jnp.exp(lb + sfx + r)
            if masked:
                wgt = jnp.where(valid, wgt, 0.0)
            wb = wgt.astype(BF16)
            for i in range(nb):
                acc = acc + _dot(vt_ref[0, 0, u * nb + i], wb[i * blk:(i + 1) * blk])
            return r + total, acc

        top = qi0 // nb
        r, acc = tile(top, jnp.zeros((1, width), F32), jnp.zeros((LANES, width), F32), True)
        r, acc = lax.fori_loop(0, top, lambda t, c: tile(top - 1 - t, c[0], c[1], False), (r, acc))
        o_ref[0] = _sb_unstack_t(acc).astype(BF16)
        tot_ref[0, 0, 0] = r

    qspec = pl.BlockSpec((1, nqb * blk, LANES), lambda b, p, i: (b, i, p))
    return pl.pallas_call(
        body, name=name, grid=(bsz, npair, nq // nqb),
        in_specs=[qspec,
                  pl.BlockSpec((1, s, LANES), lambda b, p, i: (b, 0, p)),
                  pl.BlockSpec((1, 1, nq, LANES, blk), lambda b, p, i: (b, npair + p, 0, 0, 0))],
        out_specs=[qspec, pl.BlockSpec((1, 1, 1, 1, width), lambda b, p, i: (b, p, i, 0, 0))],
        out_shape=[jax.ShapeDtypeStruct((bsz, s, w), BF16),
                   jax.ShapeDtypeStruct((bsz, npair, nq // nqb, 1, width), F32)],
        compiler_params=_params("parallel", "parallel", "arbitrary"),
    )(q, kv, kvt)


def _kv_blocks_t(kv3):
    bsz, s, w2 = kv3.shape
    x = kv3.reshape(bsz, s // SB_BLOCK, SB_BLOCK, w2 // LANES, LANES)
    return jnp.transpose(x, (0, 3, 1, 4, 2))


def _sb_bwd(q, kv, kvt, do, tot, dk_in, dv_in, *, name):
    bsz, s, w = q.shape
    blk = SB_BLOCK
    npair = w // LANES
    nq = s // blk
    nb, nqb = _sb_tile_blocks(nq, SB_Q_BLOCKS_BWD)
    width = nqb * 2 * blk
    tot = tot.reshape(bsz, npair, nq // nqb, 1, width)
    has_init = dk_in is not None

    def body(*refs):
        if has_init:
            q_ref, k_ref, v_ref, kt_ref, do_ref, tot_ref, dki_ref, dvi_ref, dq_ref, dk_ref, dv_ref = refs
        else:
            q_ref, k_ref, v_ref, kt_ref, do_ref, tot_ref, dq_ref, dk_ref, dv_ref = refs
        qi0 = pl.program_id(2) * nqb

        @pl.when(qi0 == 0)
        def _():
            if has_init:
                dk_ref[...] = dki_ref[...]
                dv_ref[...] = dvi_ref[...]
            else:
                dk_ref[...] = jnp.zeros_like(dk_ref)
                dv_ref[...] = jnp.zeros_like(dv_ref)

        qs = _sb_stack(q_ref[0] * SB_SCALE)
        dos = _sb_stack(do_ref[0])
        totv = tot_ref[0, 0, 0]
        ri = lax.broadcasted_iota(I32, (blk, blk), 0)
        ci = lax.broadcasted_iota(I32, (blk, blk), 1)
        lower = (ri > ci).astype(BF16)
        tri2 = jnp.concatenate([lower, lower], axis=1)

        def tile(u, pre_lk, pre_d, dqt, masked):
            rows = pl.ds(pl.multiple_of(u * (nb * blk), nb * blk), nb * blk)
            valid = _sb_valid(u, qi0, nb, nqb) if masked else None
            z, lb, lk_all, lk = _sb_logits(k_ref[0, rows, :], qs, valid)
            before, tot_lk = _sb_scan(tri2, lk, nb, False)
            wgt = jnp.exp(z + ((totv - pre_lk) - before))
            if masked:
                wgt = jnp.where(valid, wgt, 0.0)
            dlogit = _dot_nt(v_ref[0, rows, :], dos) * wgt
            dbefore, tot_d = _sb_scan(tri2, dlogit, nb, False, exact=False)
            sig = jnp.exp(lb)
            dz = dlogit * (1.0 - sig) - (pre_d + dbefore) * sig
            if masked:
                dz = jnp.where(valid, dz, 0.0)
            dz_b = dz.astype(BF16)
            for i in range(nb):
                dqt = dqt + _dot(kt_ref[0, 0, u * nb + i], dz_b[i * blk:(i + 1) * blk])
            dk_ref[0, rows, :] += _dot(dz_b, qs)
            dv_ref[0, rows, :] += _dot(wgt.astype(BF16), dos)
            return pre_lk + tot_lk, pre_d + tot_d, dqt

        zero = jnp.zeros((1, width), F32)
        top = qi0 // nb
        c = lax.fori_loop(0, top, lambda u, c: tile(u, c[0], c[1], c[2], False),
                          (zero, zero, jnp.zeros((LANES, width), F32)))
        _, _, dqt = tile(top, c[0], c[1], c[2], True)
        dq_ref[0] = (_sb_unstack_t(dqt) * SB_SCALE).astype(BF16)

    qspec = pl.BlockSpec((1, nqb * blk, LANES), lambda b, p, i: (b, i, p))
    kspec = pl.BlockSpec((1, s, LANES), lambda b, p, i: (b, 0, p))
    vspec = pl.BlockSpec((1, s, LANES), lambda b, p, i: (b, 0, npair + p))
    ktspec = pl.BlockSpec((1, 1, nq, LANES, blk), lambda b, p, i: (b, p, 0, 0, 0))
    tspec = pl.BlockSpec((1, 1, 1, 1, width), lambda b, p, i: (b, p, i, 0, 0))
    in_specs = [qspec, kspec, vspec, ktspec, qspec, tspec] + ([kspec, kspec] if has_init else [])
    args = (q, kv, kv, kvt, do, tot) + ((dk_in, dv_in) if has_init else ())
    return pl.pallas_call(
        body, name=name, grid=(bsz, npair, nq // nqb), in_specs=in_specs,
        out_specs=[qspec, kspec, kspec],
        out_shape=[jax.ShapeDtypeStruct((bsz, s, w), BF16), jax.ShapeDtypeStruct((bsz, s, w), F32),
                   jax.ShapeDtypeStruct((bsz, s, w), F32)],
        compiler_params=_params("parallel", "parallel", "arbitrary"),
    )(*args)


ADAM_BLOCK_BYTES = 1 << 20


def _adamw(w, g, m, v, *, name):
    shape = w.shape
    r, c = shape[-2], shape[-1]
    lead = _size(shape[:-2])
    tr = r
    for cand in range(8, r, 8):
        if r % cand == 0 and cand * c * 4 <= ADAM_BLOCK_BYTES:
            tr = cand
    if r * c * 4 <= ADAM_BLOCK_BYTES:
        tr = r

    def body(w_ref, g_ref, m_ref, v_ref, d_ref, mo_ref, vo_ref):
        gv = g_ref[...]
        mn = ADAM_B1 * m_ref[...] + (1.0 - ADAM_B1) * gv
        vn = ADAM_B2 * v_ref[...] + (1.0 - ADAM_B2) * (gv * gv)
        m_hat = mn / (1.0 - ADAM_B1 ** ADAM_STEP)
        v_hat = vn / (1.0 - ADAM_B2 ** ADAM_STEP)
        d_ref[...] = -ADAM_LR * (m_hat / (jnp.sqrt(v_hat) + ADAM_EPS) + ADAM_WD * w_ref[...])
        mo_ref[...] = mn
        vo_ref[...] = vn

    blk = pl.BlockSpec((1, tr, c), lambda l, i: (l, i, 0))
    outs = pl.pallas_call(
        body, name=name, grid=(lead, r // tr), in_specs=[blk] * 4, out_specs=[blk] * 3,
        out_shape=[jax.ShapeDtypeStruct((lead, r, c), F32)] * 3,
        compiler_params=_params("parallel", "parallel"),
    )(*[a.reshape(lead, r, c) for a in (w, g, m, v)])
    return [o.reshape(shape) for o in outs]


def _row_tile(r, c, itemsize):
    if r * c * 4 <= ADAM_BLOCK_BYTES:
        return r
    step = 32 // itemsize
    tr = r
    for cand in range(step, r, step):
        if r % cand == 0 and cand * c * 4 <= ADAM_BLOCK_BYTES:
            tr = cand
    return tr


def _add_own_half(idx, g, recv, *, name):
    _, lh, r, c = recv.shape
    tr = _row_tile(r, c, g.dtype.itemsize)

    def body(idx_ref, a_ref, b_ref, o_ref):
        o_ref[...] = (a_ref[...].astype(F32) + b_ref[...].astype(F32)).astype(o_ref.dtype)

    blk = pl.BlockSpec((1, 1, tr, c), lambda k, l, i, idx: (k, l, i, 0))
    return pl.pallas_call(
        body, name=name,
        grid_spec=pltpu.PrefetchScalarGridSpec(
            num_scalar_prefetch=1, grid=(N_CHIPS, lh, r // tr),
            in_specs=[pl.BlockSpec((1, 1, tr, c), lambda k, l, i, idx: (k, idx[0] * lh + l, i, 0)), blk],
            out_specs=blk),
        out_shape=jax.ShapeDtypeStruct(recv.shape, g.dtype),
        compiler_params=_params("parallel", "parallel", "parallel"),
    )(idx, g, recv)


def _add_chips(idx, own, recv, *, name):
    _, lh, r, c = own.shape
    tr = _row_tile(r, c, own.dtype.itemsize)

    def body(idx_ref, a_ref, b_ref, o_ref):
        f = lambda v: v.astype(F32)
        o_ref[0] = ((f(a_ref[0, 0]) + f(b_ref[0, 0])) + f(b_ref[1, 0])) + f(b_ref[2, 0])

    return pl.pallas_call(
        body, name=name,
        grid_spec=pltpu.PrefetchScalarGridSpec(
            num_scalar_prefetch=1, grid=(lh, r // tr),
            in_specs=[pl.BlockSpec((1, 1, tr, c), lambda l, i, idx: (idx[0], l, i, 0)),
                      pl.BlockSpec((3, 1, tr, c), lambda l, i, idx: (0, l, i, 0))],
            out_specs=pl.BlockSpec((1, tr, c), lambda l, i, idx: (l, i, 0))),
        out_shape=jax.ShapeDtypeStruct((lh, r, c), F32),
        compiler_params=_params("parallel", "parallel"),
    )(idx, own, recv)


def _sum_devices(parts):
    _, r, _ = parts.shape

    def body(p_ref, o_ref):
        acc = p_ref[0]
        for k in range(1, N_DEV):
            acc = acc + p_ref[k]
        o_ref[...] = acc

    return pl.pallas_call(
        body, name="small_grad_sum", grid=(1,),
        in_specs=[pl.BlockSpec((N_DEV, r, LANES), lambda i: (0, 0, 0))],
        out_specs=pl.BlockSpec((r, LANES), lambda i: (0, 0)),
        out_shape=jax.ShapeDtypeStruct((r, LANES), F32),
    )(parts)


def _place():
    return lax.axis_index("x"), lax.axis_index("y"), lax.axis_index("c")


def _rcopy(src, dst, send_sems, recv_sems, k, to):
    return pltpu.make_async_remote_copy(src_ref=src, dst_ref=dst, send_sem=send_sems.at[k],
                                        recv_sem=recv_sems.at[k], device_id=to, device_id_type=MESH)


def _exchange_call(body, name, ins, out_shapes, n_sems):
    return pl.pallas_call(
        body, name=name, in_specs=[ANY] * len(ins), out_specs=[ANY] * len(out_shapes),
        out_shape=out_shapes,
        scratch_shapes=[pltpu.SemaphoreType.DMA((n_sems,)), pltpu.SemaphoreType.DMA((n_sems,))],
    )(*ins)


def _gather_weights(shards):
    n = len(shards)

    def body(*refs):
        ins, outs, send_sems, recv_sems = refs[:n], refs[n:2 * n], refs[2 * n], refs[2 * n + 1]
        x, y, c = _place()
        sibling = (x, y, 1 - c)
        chips = [(1 - x, y), (x, 1 - y), (1 - x, 1 - y)]

        def piece(i, px, py, pc):
            lh = ins[i].shape[0] // 2
            return outs[i].at[2 * px + py, pl.ds(pc * lh, lh)]

        def mine(i):
            lh = ins[i].shape[0] // 2
            return ins[i].at[pl.ds(c * lh, lh)]

        first = [_rcopy(mine(i), piece(i, x, y, c), send_sems, recv_sems, 6 * i + j, (*chip, c))
                 for i in range(n) for j, chip in enumerate(chips)]
        for cp in first:
            cp.start()
        passed = []
        for i in range(n):
            for j, chip in enumerate(chips):
                landed = piece(i, *chip, c)
                _rcopy(landed, landed, send_sems, recv_sems, 6 * i + j, (*chip, c)).wait_recv()
                passed.append(_rcopy(landed, landed, send_sems, recv_sems, 6 * i + 3 + j, sibling))
                passed[-1].start()
        for i in range(n):
            for j, chip in enumerate(chips):
                theirs = piece(i, *chip, 1 - c)
                _rcopy(theirs, theirs, send_sems, recv_sems, 6 * i + 3 + j, sibling).wait_recv()
        for cp in first + passed:
            cp.wait_send()

    shapes = [jax.ShapeDtypeStruct((N_CHIPS,) + s_.shape, s_.dtype) for s_ in shards]
    return _exchange_call(body, "gather_weights", shards, shapes, 6 * n)


def _swap_halves(gs):
    n = len(gs)

    def body(*refs):
        ins, outs, send_sems, recv_sems = refs[:n], refs[n:2 * n], refs[2 * n], refs[2 * n + 1]
        x, y, c = _place()
        cps = []
        for i in range(n):
            lh = ins[i].shape[1] // 2
            src = ins[i].at[pl.ds(0, N_CHIPS), pl.ds((1 - c) * lh, lh)]
            cps.append(_rcopy(src, outs[i], send_sems, recv_sems, i, (x, y, 1 - c)))
        for cp in cps:
            cp.start()
        for cp in cps:
            cp.wait()

    shapes = [jax.ShapeDtypeStruct((N_CHIPS, g.shape[1] // 2) + g.shape[2:], g.dtype) for g in gs]
    return _exchange_call(body, "grad_swap_halves", gs, shapes, n)


def _scatter_chunks(sums):
    n = len(sums)

    def body(*refs):
        ins, outs, send_sems, recv_sems = refs[:n], refs[n:2 * n], refs[2 * n], refs[2 * n + 1]
        x, y, c = _place()
        chips = [(1 - x, y), (x, 1 - y), (1 - x, 1 - y)]
        cps = [_rcopy(ins[i].at[2 * chip[0] + chip[1]], outs[i].at[j], send_sems, recv_sems, 3 * i + j,
                      (*chip, c)) for i in range(n) for j, chip in enumerate(chips)]
        for cp in cps:
            cp.start()
        for cp in cps:
            cp.wait()

    shapes = [jax.ShapeDtypeStruct((3,) + s_.shape[1:], s_.dtype) for s_ in sums]
    return _exchange_call(body, "grad_scatter_chunks", sums, shapes, 3 * n)


def _share_half(tots):
    n = len(tots)

    def body(*refs):
        ins, outs, send_sems, recv_sems = refs[:n], refs[n:2 * n], refs[2 * n], refs[2 * n + 1]
        x, y, c = _place()
        cps = [_rcopy(ins[i], outs[i], send_sems, recv_sems, i, (x, y, 1 - c)) for i in range(n)]
        for cp in cps:
            cp.start()
        for cp in cps:
            cp.wait()

    shapes = [jax.ShapeDtypeStruct(t_.shape, t_.dtype) for t_ in tots]
    return _exchange_call(body, "grad_share_half", tots, shapes, n)


def _exchange_small(r):
    rr, _ = r.shape

    def body(r_ref, out_ref, send_sems, recv_sems, local_sem):
        x, y, c = _place()
        me = 4 * x + 2 * y + c
        mine = pltpu.make_async_copy(r_ref, out_ref.at[me], local_sem)
        mine.start()
        cps = []
        for k in range(N_DEV - 1):
            fx, fy, fc = ((k + 1) >> 2) & 1, ((k + 1) >> 1) & 1, (k + 1) & 1
            to = (x ^ fx, y ^ fy, c ^ fc)
            cps.append((_rcopy(r_ref, out_ref.at[me], send_sems, recv_sems, k, to), to))
        for cp, _ in cps:
            cp.start()
        for k, (cp, to) in enumerate(cps):
            src = 4 * to[0] + 2 * to[1] + to[2]
            _rcopy(r_ref, out_ref.at[src], send_sems, recv_sems, k, to).wait_recv()
        for cp, _ in cps:
            cp.wait_send()
        mine.wait()

    return pl.pallas_call(
        body, name="small_grad_exchange", in_specs=[ANY], out_specs=ANY,
        out_shape=jax.ShapeDtypeStruct((N_DEV, rr, LANES), r.dtype),
        scratch_shapes=[pltpu.SemaphoreType.DMA((N_DEV - 1,)), pltpu.SemaphoreType.DMA((N_DEV - 1,)),
                        pltpu.SemaphoreType.DMA],
    )(r)


def _size(shape):
    n = 1
    for d in shape:
        n *= d
    return n


def _slab_rows(shape):
    rows = -(-_size(shape) // LANES)
    return -(-rows // SLAB_ROW_ALIGN) * SLAB_ROW_ALIGN


def _pack_rows(arrs, dtype, lead=0, unit=PACK_ROWS):
    parts, total = [], 0
    for a in arrs:
        front, shp = a.shape[:lead], a.shape[lead:]
        n, rows = _size(shp), _slab_rows(shp)
        nopad = [(0, 0)] * lead
        if n % LANES == 0:
            p = a.reshape(front + (n // LANES, LANES)).astype(dtype)
        else:
            p = jnp.pad(a.reshape(front + (n,)).astype(dtype), nopad + [(0, rows * LANES - n)])
            p = p.reshape(front + (rows, LANES))
        if p.shape[lead] != rows:
            p = jnp.pad(p, nopad + [(0, rows - p.shape[lead]), (0, 0)])
        parts.append(p)
        total += rows
    pad = (-total) % unit
    if pad:
        parts.append(jnp.zeros(parts[0].shape[:lead] + (pad, LANES), dtype))
    return jnp.concatenate(parts, axis=lead)


def _unpack_rows(slab, shapes):
    lead = slab.shape[:-2]
    out, off = [], 0
    for shp in shapes:
        n, rows = _size(shp), _slab_rows(shp)
        piece = slab[..., off:off + rows, :]
        if n % LANES == 0:
            piece = piece[..., :n // LANES, :].reshape(lead + tuple(shp))
        else:
            piece = piece.reshape(lead + (rows * LANES,))[..., :n].reshape(lead + tuple(shp))
        out.append(piece)
        off += rows
    return out


def _ffn_fwd(h, bsz, s, gain, w_up, cw, cb, w_down, i):
    hf = _rmsnorm_fwd(h, gain, name=f"ffn_norm_{i}")
    up = _mm(hf, w_up, name=f"ffn_up_{i}")
    up3 = up.reshape(bsz, s, -1)
    f = _conv_glu_fwd(up3, cw, cb, name=f"ffn_glu_{i}").reshape(h.shape[0], -1)
    h2 = _mm(f, w_down, add=h, name=f"ffn_down_{i}")
    return h2, (h, hf, up3, f)


def _ffn_bwd(dh, saved, gain, w_up, cw, cb, w_down, i):
    h, hf, up3, f = saved
    t = h.shape[0]
    d_down = _mm(f, dh, ta=True, name=f"ffn_down_dw_{i}")
    df = _mm(dh, w_down, tb=True, name=f"ffn_down_dx_{i}")
    dug, duv, dcw, dcb = _conv_glu_bwd(up3, cw, cb, df.reshape(up3.shape[0], up3.shape[1], -1),
                                       name=f"ffn_glu_bwd_{i}")
    dug, duv = dug.reshape(t, -1), duv.reshape(t, -1)
    fdim = dug.shape[1]
    d_up = jnp.concatenate([_mm(hf, dug, ta=True, name=f"ffn_up_dwg_{i}"),
                            _mm(hf, duv, ta=True, name=f"ffn_up_dwv_{i}")], axis=1)
    dhf = _mm(dug, w_up[:, :fdim], tb=True, name=f"ffn_up_dxg_{i}")
    dhf = _mm(duv, w_up[:, fdim:], tb=True, add=dhf, name=f"ffn_up_dxv_{i}")
    dh, dgain = _rmsnorm_bwd(h, gain, dhf, dh, name=f"ffn_norm_bwd_{i}")
    return dh, dgain, d_up, dcw, dcb, d_down


def _ple_layer_fwd(h, p_i, gain, w_gate, w_proj, i):
    hp = _rmsnorm_fwd(h, gain, name=f"ple_norm_{i}")
    a = _mm(hp, w_gate, name=f"ple_gate_{i}")
    pp = _mm(p_i, w_proj, name=f"ple_proj_{i}")
    return _ple_fwd(h, a, pp, name=f"ple_mix_{i}"), (h, hp, a, pp)


def _ple_layer_bwd(dh, saved, p_i, gain, w_gate, i):
    h, hp, a, pp = saved
    da, dpp = _ple_bwd(dh, a, pp, name=f"ple_mix_bwd_{i}")
    d_gate = _mm(hp, da, ta=True, name=f"ple_gate_dw_{i}")
    d_proj = _mm(p_i, dpp, ta=True, name=f"ple_proj_dw_{i}")
    dhp = _mm(da, w_gate, tb=True, name=f"ple_gate_dx_{i}")
    dh, dgain = _rmsnorm_bwd(h, gain, dhp, dh, name=f"ple_norm_bwd_{i}")
    return dh, dgain, d_gate, d_proj


def _ssm_consts(dt_bias, a_log, d_skip, g_n):
    hpg = SSM_HEADS_PER_GROUP
    a = -jnp.exp(a_log)
    rows = jnp.stack([dt_bias.reshape(g_n, hpg), a.reshape(g_n, hpg)], axis=1)
    plane = jnp.zeros((g_n, 8, LANES), F32).at[:, 0:2, 0:hpg].set(rows)
    psub = jnp.zeros((g_n, SSM_ROWS, LANES), F32).at[:, 0:hpg, 0:2].set(jnp.swapaxes(rows, 1, 2))
    d_ch = jnp.repeat(d_skip, SSM_HEAD_DIM).reshape(1, -1)
    return a, plane, psub, d_ch


def _ssm_in_big(w_in, d_inner, g_n):
    d = w_in.shape[0]
    cut = w_in.shape[1] - g_n * SSM_HEADS_PER_GROUP
    wdt = w_in[:, cut:].reshape(d, g_n, SSM_HEADS_PER_GROUP)
    wdt = jnp.pad(wdt, ((0, 0), (0, 0), (0, LANES - SSM_HEADS_PER_GROUP))).reshape(d, g_n * LANES)
    return jnp.concatenate([w_in[:, :cut], wdt], axis=1)


def _ssm_in_small(dw_big, g_n):
    d = dw_big.shape[0]
    cut = dw_big.shape[1] - g_n * LANES
    ddt = dw_big[:, cut:].reshape(d, g_n, LANES)[:, :, :SSM_HEADS_PER_GROUP].reshape(d, -1)
    return jnp.concatenate([dw_big[:, :cut], ddt], axis=1)


def _ssm_fwd(h, bsz, s, gain, w_in_big, cw, cb, plane, psub, d_ch, nw, w_out, i):
    d_inner = d_ch.shape[1]
    g_n = d_inner // SSM_GROUP_W
    conv_dim = cw.shape[1]
    hn = _rmsnorm_fwd(h, gain, name=f"attn_norm_{i}")
    zx = _mm(hn, w_in_big, name=f"ssm_in_{i}").reshape(bsz, s, -1)
    xbc = _conv_silu_fwd(zx, d_inner, cw, cb, name=f"ssm_conv_{i}")
    dtr = zx[:, :, d_inner + conv_dim:].reshape(bsz, s, g_n, LANES)[..., :SSM_HEADS_PER_GROUP]
    dtr_row = jnp.pad(jnp.transpose(dtr, (0, 2, 3, 1)),
                      ((0, 0), (0, 0), (0, SSM_ROWS - SSM_HEADS_PER_GROUP), (0, 0)))
    gn, states = _ssd_fwd(zx, xbc, dtr_row, plane, psub, d_ch, nw, name=f"ssd_{i}")
    gn2 = gn.reshape(h.shape[0], -1)
    h1 = _mm(gn2, w_out, add=h, name=f"ssm_out_{i}")
    return h1, (h, hn, zx, xbc, dtr_row, states, gn2)


def _ssm_bwd(dh, saved, gain, w_in_big, cw, cb, plane, psub, d_ch, nw, w_out, i):
    h, hn, zx, xbc, dtr_row, states, gn2 = saved
    t = h.shape[0]
    bsz, s, _ = zx.shape
    d_inner = d_ch.shape[1]
    d_out = _mm(gn2, dh, ta=True, name=f"ssm_out_dw_{i}")
    dgn = _mm(dh, w_out, tb=True, name=f"ssm_out_dx_{i}").reshape(bsz, s, -1)
    dxs, dbm, dcm, dz, ddtr, ach, aln = _ssd_bwd(zx, xbc, dtr_row, plane, psub, d_ch, nw, states, dgn,
                                                  name=f"ssd_bwd_{i}")
    dxbc, dcw, dcb = _conv_silu_bwd(zx, d_inner, cw, cb, [dxs, dbm, dcm], name=f"ssm_conv_bwd_{i}")
    d_in_parts, dhn, col = [], None, 0
    for tag, part in (("z", dz), ("xbc", dxbc), ("dt", ddtr)):
        part = part.reshape(t, -1)
        d_in_parts.append(_mm(hn, part, ta=True, name=f"ssm_in_dw_{tag}_{i}"))
        dhn = _mm(part, w_in_big[:, col:col + part.shape[1]], tb=True, add=dhn,
                  name=f"ssm_in_dx_{tag}_{i}")
        col += part.shape[1]
    d_in_big = jnp.concatenate(d_in_parts, axis=1)
    dh, dgain = _rmsnorm_bwd(h, gain, dhn, dh, name=f"attn_norm_bwd_{i}")
    hpg = SSM_HEADS_PER_GROUP
    ach = jnp.sum(ach, axis=0)
    aln = jnp.sum(aln, axis=0)
    d_nw = ach[:, 0, :].reshape(-1)
    d_dskip = jnp.sum(ach[:, 1, :].reshape(-1, SSM_HEAD_DIM), axis=1)
    d_bias = aln[:, 0, :hpg].reshape(-1)
    d_a = aln[:, 1, :hpg].reshape(-1)
    return dh, dgain, d_in_big, dcw, dcb, d_bias, d_a, d_dskip, d_nw, d_out


def _sb_layer_fwd(h, bsz, s, gain, w_q, w_o, kv3, kvt, i):
    hn = _rmsnorm_fwd(h, gain, name=f"attn_norm_{i}")
    q3 = _mm(hn, w_q, out_dtype=BF16, name=f"sb_q_{i}").reshape(bsz, s, -1)
    o3, tot = _sb_fwd(q3, kv3, kvt, name=f"sb_attn_{i}")
    o2 = o3.reshape(h.shape[0], -1)
    h1 = _mm(o2, w_o, add=h, name=f"sb_o_{i}")
    return h1, (h, hn, q3, o2, tot)


def _sb_layer_bwd(dh, saved, gain, w_q, w_o, kv3, kvt, dk, dv, i):
    h, hn, q3, o2, tot = saved
    t = h.shape[0]
    d_o = _mm(o2, dh, ta=True, name=f"sb_o_dw_{i}")
    do3 = _mm(dh, w_o, tb=True, out_dtype=BF16, name=f"sb_o_dx_{i}").reshape(q3.shape)
    dq3, dk, dv = _sb_bwd(q3, kv3, kvt, do3, tot, dk, dv, name=f"sb_attn_bwd_{i}")
    dq = dq3.reshape(t, -1)
    d_q = _mm(hn, dq, ta=True, name=f"sb_q_dw_{i}")
    dhn = _mm(dq, w_q, tb=True, name=f"sb_q_dx_{i}")
    dh, dgain = _rmsnorm_bwd(h, gain, dhn, dh, name=f"attn_norm_bwd_{i}")
    return dh, dgain, d_q, d_o, dk, dv


def kernel(x, p, attn_norm, ffn_norm, ple_norm, ssm_in_proj, ssm_conv_w, ssm_conv_b, ssm_dt_bias, ssm_a_log, ssm_d, ssm_norm, ssm_out_proj, kv_norm, w_kv, w_q, w_o, ffn_up, ffn_conv_w, ffn_conv_b, ffn_down, ple_gate, ple_proj, final_norm, loss_target, m_attn_norm, m_ffn_norm, m_ple_norm, m_ssm_in_proj, m_ssm_conv_w, m_ssm_conv_b, m_ssm_dt_bias, m_ssm_a_log, m_ssm_d, m_ssm_norm, m_ssm_out_proj, m_kv_norm, m_w_kv, m_w_q, m_w_o, m_ffn_up, m_ffn_conv_w, m_ffn_conv_b, m_ffn_down, m_ple_gate, m_ple_proj, m_final_norm, v_attn_norm, v_ffn_norm, v_ple_norm, v_ssm_in_proj, v_ssm_conv_w, v_ssm_conv_b, v_ssm_dt_bias, v_ssm_a_log, v_ssm_d, v_ssm_norm, v_ssm_out_proj, v_kv_norm, v_w_kv, v_w_q, v_w_o, v_ffn_up, v_ffn_conv_w, v_ffn_conv_b, v_ffn_down, v_ple_gate, v_ple_proj, v_final_norm):
    given = dict(locals())
    wl = {n: given[n] for n in WEIGHTS}
    bsz, s, d = x.shape
    t = bsz * s
    depth = attn_norm.shape[0]
    n_a = ssm_in_proj.shape[0]
    d_inner = ssm_norm.shape[1] * N_CHIPS
    g_n = d_inner // SSM_GROUP_W
    cidx = lax.axis_index("c").astype(I32).reshape(1)
    chip_idx = (2 * lax.axis_index("x") + lax.axis_index("y")).astype(I32).reshape(1)

    big = [n for n in SHARDED if _size(wl[n].shape) >= BIG_WEIGHT]
    small = [n for n in SHARDED if n not in big]
    small_shapes = [wl[n].shape for n in small]
    halves = lambda shp: shp if len(shp) == 3 else (2, shp[0] // 2, shp[1])
    small_slab = _pack_rows([wl[n] for n in small], BF16, unit=2 * SLAB_ROW_ALIGN)
    small_rows = small_slab.shape[0]
    mine = [wl[n].astype(BF16).reshape(halves(wl[n].shape)) for n in big]
    mine.append(small_slab.reshape(2, small_rows // 2, LANES))
    gathered = [lax.dynamic_update_index_in_dim(g, m_, chip_idx[0], 0)
                for g, m_ in zip(_gather_weights(mine), mine)]
    per_chip = {n: g.reshape((N_CHIPS,) + wl[n].shape) for n, g in zip(big, gathered)}
    per_chip.update(zip(small, _unpack_rows(gathered[-1].reshape(N_CHIPS, small_rows, LANES), small_shapes)))
    full = {}
    for n in SHARDED:
        ax, piece = SHARD_AXIS[n], per_chip[n]
        merged = piece.shape[1:ax + 1] + (N_CHIPS * piece.shape[ax + 1],) + piece.shape[ax + 2:]
        full[n] = jnp.moveaxis(piece, 0, ax).reshape(merged)

    h = x.reshape(t, d)
    tgt = loss_target.reshape(t, d)
    saved = []
    kv3 = kvt = hkv = h_kv_in = None
    consts = []
    for i in range(depth):
        if i < n_a:
            a_neg, plane, psub, d_ch = _ssm_consts(ssm_dt_bias[i], ssm_a_log[i], ssm_d[i], g_n)
            w_in_big = _ssm_in_big(full["ssm_in_proj"][i], d_inner, g_n)
            cw = full["ssm_conv_w"][i].astype(F32)
            cb = full["ssm_conv_b"][i].astype(F32)
            nw = full["ssm_norm"][i].astype(F32).reshape(1, -1)
            consts.append((a_neg, plane, psub, d_ch, w_in_big, cw, cb, nw))
            h, sv_mix = _ssm_fwd(h, bsz, s, attn_norm[i], w_in_big, cw, cb, plane, psub, d_ch, nw,
                                 full["ssm_out_proj"][i], i)
        else:
            j = i - n_a
            h, sv_mix = _sb_layer_fwd(h, bsz, s, attn_norm[i], full["w_q"][j], full["w_o"][j], kv3, kvt, i)
        fcw = full["ffn_conv_w"][i].astype(F32)
        h, sv_ffn = _ffn_fwd(h, bsz, s, ffn_norm[i], full["ffn_up"][i], fcw, ffn_conv_b[i],
                             full["ffn_down"][i], i)
        p_i = p[i].reshape(t, -1)
        h, sv_ple = _ple_layer_fwd(h, p_i, ple_norm[i], full["ple_gate"][i], full["ple_proj"][i], i)
        saved.append((sv_mix, sv_ffn, sv_ple))
        if i == n_a - 1:
            h_kv_in = h
            hkv = _rmsnorm_fwd(h, kv_norm, name="kv_norm")
            kv3 = _mm(hkv, full["w_kv"], out_dtype=BF16, name="kv_proj").reshape(bsz, s, -1)
            kvt = _kv_blocks_t(kv3)

    loss_local, dh, g_final = _final_loss(h, final_norm, tgt)
    gr = {n: [None] * wl[n].shape[0] for n in WEIGHTS if n not in ("kv_norm", "w_kv", "final_norm")}
    gr["final_norm"] = g_final
    dk = dv = None
    for i in reversed(range(depth)):
        sv_mix, sv_ffn, sv_ple = saved[i]
        if i == n_a - 1:
            dkv = jnp.concatenate([dk, dv], axis=-1).reshape(t, -1)
            gr["w_kv"] = _mm(hkv, dkv, ta=True, name="kv_proj_dw")
            dhkv = _mm(dkv, full["w_kv"], tb=True, name="kv_proj_dx")
            dh, gr["kv_norm"] = _rmsnorm_bwd(h_kv_in, kv_norm, dhkv, dh, name="kv_norm_bwd")
        p_i = p[i].reshape(t, -1)
        dh, gr["ple_norm"][i], gr["ple_gate"][i], gr["ple_proj"][i] = _ple_layer_bwd(
            dh, sv_ple, p_i, ple_norm[i], full["ple_gate"][i], i)
        fcw = full["ffn_conv_w"][i].astype(F32)
        (dh, gr["ffn_norm"][i], gr["ffn_up"][i], gr["ffn_conv_w"][i], gr["ffn_conv_b"][i],
         gr["ffn_down"][i]) = _ffn_bwd(dh, sv_ffn, ffn_norm[i], full["ffn_up"][i], fcw, ffn_conv_b[i],
                                       full["ffn_down"][i], i)
        if i < n_a:
            a_neg, plane, psub, d_ch, w_in_big, cw, cb, nw = consts[i]
            (dh, gr["attn_norm"][i], d_in_big, gr["ssm_conv_w"][i], gr["ssm_conv_b"][i],
             gr["ssm_dt_bias"][i], d_a, gr["ssm_d"][i], gr["ssm_norm"][i],
             gr["ssm_out_proj"][i]) = _ssm_bwd(dh, sv_mix, attn_norm[i], w_in_big, cw, cb, plane, psub,
                                               d_ch, nw, full["ssm_out_proj"][i], i)
            gr["ssm_in_proj"][i] = _ssm_in_small(d_in_big, g_n)
            gr["ssm_a_log"][i] = d_a * a_neg
        else:
            j = i - n_a
            dh, gr["attn_norm"][i], gr["w_q"][j], gr["w_o"][j], dk, dv = _sb_layer_bwd(
                dh, sv_mix, attn_norm[i], full["w_q"][j], full["w_o"][j], kv3, kvt, dk, dv, i)
    grad_x = dh.reshape(bsz, s, d)
    gfull = {n: (jnp.stack(v) if isinstance(v, list) else v) for n, v in gr.items()}

    by_chip = {}
    for n in SHARDED:
        ax, shp = SHARD_AXIS[n], gfull[n].shape
        split = gfull[n].reshape(shp[:ax] + (N_CHIPS, shp[ax] // N_CHIPS) + shp[ax + 1:])
        by_chip[n] = jnp.moveaxis(split, ax, 0)
    g4 = [by_chip[n].astype(BF16).reshape((N_CHIPS,) + halves(wl[n].shape)) for n in big]
    g4.append(_pack_rows([by_chip[n] for n in small], BF16, lead=1, unit=2 * SLAB_ROW_ALIGN)
              .reshape(N_CHIPS, 2, small_rows // 2, LANES))
    tags = big + ["small"]
    from_sibling = _swap_halves(g4)
    chip_sums = [_add_own_half(cidx, g, r_, name="grad_pair_sum_" + tg)
                 for g, r_, tg in zip(g4, from_sibling, tags)]
    from_chips = _scatter_chunks(chip_sums)
    my_half = [_add_chips(chip_idx, s_, r_, name="grad_chip_sum_" + tg)
               for s_, r_, tg in zip(chip_sums, from_chips, tags)]
    other_half = _share_half(my_half)
    low_core = cidx[0] == 0
    reduced = [jnp.concatenate([jnp.where(low_core, a, b_), jnp.where(low_core, b_, a)], axis=0)
               for a, b_ in zip(my_half, other_half)]
    grads = {n: g.reshape(wl[n].shape) for n, g in zip(big, reduced)}
    grads.update(zip(small, _unpack_rows(reduced[-1].reshape(small_rows, LANES), small_shapes)))

    rep_shapes = [wl[n].shape for n in REPLICATED]
    packed_r = _pack_rows([gfull[n] for n in REPLICATED], F32, unit=SLAB_ROW_ALIGN)
    g_rep = _sum_devices(_exchange_small(packed_r))

    delta, new_m, new_v = {}, {}, {}
    for n in SHARDED:
        delta[n], new_m[n], new_v[n] = _adamw(wl[n], grads[n], given["m_" + n], given["v_" + n],
                                              name="adamw_" + n)
    slabs = [_pack_rows([src[pre + n] for n in REPLICATED], F32, unit=SLAB_ROW_ALIGN)
             for src, pre in ((wl, ""), (given, "m_"), (given, "v_"))]
    rep_out = _adamw(slabs[0], g_rep, slabs[1], slabs[2], name="adamw_replicated")
    for dst, slab in zip((grads, delta, new_m, new_v), [g_rep] + list(rep_out)):
        dst.update(zip(REPLICATED, _unpack_rows(slab, rep_shapes)))
    loss = lax.psum(loss_local, ("x", "y", "c"))
    return (loss, grad_x, *[grads[n] for n in WEIGHTS], *[delta[n] for n in WEIGHTS],
            *[new_m[n] for n in WEIGHTS], *[new_v[n] for n in WEIGHTS])
```

```python
import functools

import jax
import jax.numpy as jnp
from jax import lax
from jax.experimental import pallas as pl
from jax.experimental.pallas import tpu as pltpu

F32 = jnp.float32
BF16 = jnp.bfloat16
I32 = jnp.int32

NORM_EPS = 1e-6
SSM_NORM_EPS = 1e-5
SSM_HEAD_DIM = 64
SSM_STATE = 128
SSM_CHUNK = 128
SSM_HEADS_PER_GROUP = 8
SSM_GROUP_W = SSM_HEADS_PER_GROUP * SSM_HEAD_DIM
SSM_CONV = 4
SSM_ROWS = 16
SB_HEAD_DIM = 64
SB_BLOCK = 128
SB_SCALE = SB_HEAD_DIM ** -0.5
SB_Q_BLOCKS_FWD = 4
SB_Q_BLOCKS_BWD = 4
FFN_CONV = 3
LANES = 128
N_CHIPS = 4
N_DEV = 8

ADAM_LR = 0.001
ADAM_B1 = 0.9
ADAM_B2 = 0.999
ADAM_EPS = 1e-08
ADAM_WD = 0.01
ADAM_STEP = 10

MESH = pl.DeviceIdType.MESH
ANY = pl.BlockSpec(memory_space=pl.ANY)

SHARD_AXIS = {
    "ssm_in_proj": 2, "ssm_conv_w": 2, "ssm_conv_b": 1, "ssm_norm": 1, "ssm_out_proj": 1,
    "w_kv": 1, "w_q": 1, "w_o": 1, "ffn_up": 2, "ffn_conv_w": 2, "ffn_down": 1,
    "ple_gate": 1, "ple_proj": 2,
}
REPLICATED = ["attn_norm", "ffn_norm", "ple_norm", "ssm_dt_bias", "ssm_a_log", "ssm_d",
              "kv_norm", "ffn_conv_b", "final_norm"]
WEIGHTS = ["attn_norm", "ffn_norm", "ple_norm", "ssm_in_proj", "ssm_conv_w", "ssm_conv_b",
           "ssm_dt_bias", "ssm_a_log", "ssm_d", "ssm_norm", "ssm_out_proj", "kv_norm", "w_kv",
           "w_q", "w_o", "ffn_up", "ffn_conv_w", "ffn_conv_b", "ffn_down", "ple_gate",
           "ple_proj", "final_norm"]
SHARDED = [n for n in WEIGHTS if n in SHARD_AXIS]
PACK_ROWS = 2048
SLAB_ROW_ALIGN = 16
BIG_WEIGHT = 1 << 17


def _tile(n, pref):
    t = (min(pref, n) // 128) * 128
    while t >= 128:
        if n % t == 0:
            return t
        t -= 128
    return n


def _dot(a, b):
    return jnp.dot(a, b, preferred_element_type=F32)


def _dot_nt(a, b):
    return lax.dot_general(a, b, (((1,), (1,)), ((), ())), preferred_element_type=F32)


def _dot_tn(a, b):
    return lax.dot_general(a, b, (((0,), (0,)), ((), ())), preferred_element_type=F32)


def _split2(x):
    hi = x.astype(BF16)
    lo = (x - hi.astype(F32)).astype(BF16)
    return hi, lo


def _dot2(x, m):
    hi, lo = _split2(x)
    return _dot(hi, m) + _dot(lo, m)


def _dot2_left(m, x):
    hi, lo = _split2(x)
    return _dot(m, hi) + _dot(m, lo)


def _softplus(x):
    return jnp.maximum(x, 0.0) + jnp.log(1.0 + jnp.exp(-jnp.abs(x)))


def _sigmoid(x):
    return 0.5 * jnp.tanh(0.5 * x) + 0.5


def _params(*sem):
    return pltpu.CompilerParams(dimension_semantics=sem)


MM_VMEM_BUDGET = 36 * 1024 * 1024
MM_FULL_K = 2816


def _mm_tiles(m, n, k, sa, sb, so, has_add):
    tk = k if k <= MM_FULL_K else _tile(k, 1024)
    tn = _tile(n, 1408)
    tm = _tile(m, 1408)

    def need(tm_):
        return (2 * tm_ * tk * sa + 2 * tk * tn * sb + tm_ * tn * 4 + 2 * tm_ * tn * so
                + (2 * tm_ * tn * 4 if has_add else 0))

    while need(tm) > MM_VMEM_BUDGET and tm % 256 == 0:
        tm //= 2
    return tm, tn, tk


def _mm(a, b, *, name, ta=False, tb=False, add=None, out_dtype=F32):
    m = a.shape[1] if ta else a.shape[0]
    k = a.shape[0] if ta else a.shape[1]
    n = b.shape[0] if tb else b.shape[1]
    assert (b.shape[1] if tb else b.shape[0]) == k, (a.shape, b.shape, ta, tb)
    tm, tn, tk = _mm_tiles(m, n, k, a.dtype.itemsize, b.dtype.itemsize,
                           jnp.dtype(out_dtype).itemsize, add is not None)
    nk = k // tk
    dims = (((0 if ta else 1,), (1 if tb else 0,)), ((), ()))
    has_add = add is not None

    def body(*refs):
        if has_add:
            a_ref, b_ref, add_ref, o_ref, acc_ref = refs
        else:
            a_ref, b_ref, o_ref, acc_ref = refs
        kk = pl.program_id(2)

        @pl.when(kk == 0)
        def _():
            acc_ref[...] = jnp.zeros_like(acc_ref)

        acc_ref[...] += lax.dot_general(a_ref[...].astype(BF16), b_ref[...].astype(BF16), dims,
                                        preferred_element_type=F32)

        @pl.when(kk == nk - 1)
        def _():
            r = acc_ref[...]
            if has_add:
                r = r + add_ref[...].astype(F32)
            o_ref[...] = r.astype(out_dtype)

    a_spec = (pl.BlockSpec((tk, tm), lambda i, j, kk: (kk, i)) if ta
              else pl.BlockSpec((tm, tk), lambda i, j, kk: (i, kk)))
    b_spec = (pl.BlockSpec((tn, tk), lambda i, j, kk: (j, kk)) if tb
              else pl.BlockSpec((tk, tn), lambda i, j, kk: (kk, j)))
    o_spec = pl.BlockSpec((tm, tn), lambda i, j, kk: (i, j))
    in_specs = [a_spec, b_spec] + ([o_spec] if has_add else [])
    args = (a, b) + ((add,) if has_add else ())
    return pl.pallas_call(
        body, name=name, grid=(m // tm, n // tn, nk), in_specs=in_specs, out_specs=o_spec,
        out_shape=jax.ShapeDtypeStruct((m, n), out_dtype),
        scratch_shapes=[pltpu.VMEM((tm, tn), F32)],
        compiler_params=_params("parallel", "parallel", "arbitrary"),
    )(*args)


def _rmsnorm_fwd(x, gain, *, name, rows=512):
    t, d = x.shape
    tr = _tile(t, rows)

    def body(x_ref, g_ref, o_ref):
        xv = x_ref[...]
        r = lax.rsqrt(jnp.mean(xv * xv, axis=-1, keepdims=True) + NORM_EPS)
        o_ref[...] = ((xv * r) * g_ref[...]).astype(BF16)

    return pl.pallas_call(
        body, name=name, grid=(t // tr,),
        in_specs=[pl.BlockSpec((tr, d), lambda i: (i, 0)), pl.BlockSpec((1, d), lambda i: (0, 0))],
        out_specs=pl.BlockSpec((tr, d), lambda i: (i, 0)),
        out_shape=jax.ShapeDtypeStruct((t, d), BF16),
        compiler_params=_params("parallel"),
    )(x, gain.reshape(1, d))


def _rmsnorm_bwd(x, gain, dy, dres, *, name, rows=512):
    t, d = x.shape
    tr = _tile(t, rows)

    def body(x_ref, g_ref, dy_ref, dres_ref, dx_ref, dg_ref):
        xv = x_ref[...]
        r = lax.rsqrt(jnp.mean(xv * xv, axis=-1, keepdims=True) + NORM_EPS)
        xh = xv * r
        dyv = dy_ref[...].astype(F32)
        dxh = dyv * g_ref[...]
        dx = r * (dxh - xh * jnp.mean(dxh * xh, axis=-1, keepdims=True))
        dx_ref[...] = dres_ref[...] + dx
        part = jnp.sum(dyv * xh, axis=0, keepdims=True)

        @pl.when(pl.program_id(0) == 0)
        def _():
            dg_ref[...] = part

        @pl.when(pl.program_id(0) > 0)
        def _():
            dg_ref[...] += part

    row = pl.BlockSpec((tr, d), lambda i: (i, 0))
    vec = pl.BlockSpec((1, d), lambda i: (0, 0))
    dx, dg = pl.pallas_call(
        body, name=name, grid=(t // tr,), in_specs=[row, vec, row, row], out_specs=[row, vec],
        out_shape=[jax.ShapeDtypeStruct((t, d), F32), jax.ShapeDtypeStruct((1, d), F32)],
        compiler_params=_params("arbitrary"),
    )(x, gain.reshape(1, d), dy, dres)
    return dx, dg.reshape(d)


def _final_loss(h, gain, target, *, rows=512):
    t, d = h.shape
    tr = _tile(t, rows)

    def body(x_ref, g_ref, tg_ref, dx_ref, dg_ref, loss_ref):
        xv = x_ref[...]
        g = g_ref[...]
        r = lax.rsqrt(jnp.mean(xv * xv, axis=-1, keepdims=True) + NORM_EPS)
        xh = xv * r
        err = xh * g - tg_ref[...]
        dyv = err * (1.0 / d)
        dxh = dyv * g
        dx_ref[...] = r * (dxh - xh * jnp.mean(dxh * xh, axis=-1, keepdims=True))
        part = jnp.sum(dyv * xh, axis=0, keepdims=True)
        lpart = jnp.zeros((1, LANES), F32) + (0.5 / d) * jnp.sum(err * err)

        @pl.when(pl.program_id(0) == 0)
        def _():
            dg_ref[...] = part
            loss_ref[...] = lpart

        @pl.when(pl.program_id(0) > 0)
        def _():
            dg_ref[...] += part
            loss_ref[...] += lpart

    row = pl.BlockSpec((tr, d), lambda i: (i, 0))
    vec = pl.BlockSpec((1, d), lambda i: (0, 0))
    dx, dg, loss = pl.pallas_call(
        body, name="final_loss", grid=(t // tr,), in_specs=[row, vec, row],
        out_specs=[row, vec, pl.BlockSpec((1, LANES), lambda i: (0, 0))],
        out_shape=[jax.ShapeDtypeStruct((t, d), F32), jax.ShapeDtypeStruct((1, d), F32),
                   jax.ShapeDtypeStruct((1, LANES), F32)],
        compiler_params=_params("arbitrary"),
    )(h, gain.reshape(1, d), target)
    return loss[0, 0], dx, dg.reshape(d)


def _ple_fwd(h, a, pp, *, name, rows=512):
    t, d = h.shape
    tr = _tile(t, rows)

    def body(h_ref, a_ref, p_ref, o_ref):
        o_ref[...] = h_ref[...] + _sigmoid(a_ref[...]) * p_ref[...]

    row = pl.BlockSpec((tr, d), lambda i: (i, 0))
    return pl.pallas_call(
        body, name=name, grid=(t // tr,), in_specs=[row, row, row], out_specs=row,
        out_shape=jax.ShapeDtypeStruct((t, d), F32), compiler_params=_params("parallel"),
    )(h, a, pp)


def _ple_bwd(dh, a, pp, *, name, rows=512):
    t, d = dh.shape
    tr = _tile(t, rows)

    def body(dh_ref, a_ref, p_ref, da_ref, dp_ref):
        s = _sigmoid(a_ref[...])
        dhv = dh_ref[...]
        da_ref[...] = (dhv * p_ref[...] * (s * (1.0 - s))).astype(BF16)
        dp_ref[...] = (dhv * s).astype(BF16)

    row = pl.BlockSpec((tr, d), lambda i: (i, 0))
    return pl.pallas_call(
        body, name=name, grid=(t // tr,), in_specs=[row, row, row], out_specs=[row, row],
        out_shape=[jax.ShapeDtypeStruct((t, d), BF16)] * 2, compiler_params=_params("parallel"),
    )(dh, a, pp)


CONV_ROWS = 64
CONV_HALO = 8


def _conv_window(ref, r0, with_prev, with_next):
    s = ref.shape[1]
    parts = []
    if with_prev:
        prev = ref[0, pl.ds(pl.multiple_of(jnp.maximum(r0 - CONV_HALO, 0), CONV_HALO), CONV_HALO), :]
        parts.append(jnp.where(r0 > 0, prev, 0.0))
    parts.append(ref[0, pl.ds(r0, CONV_ROWS), :])
    if with_next:
        nxt = pl.multiple_of(jnp.minimum(r0 + CONV_ROWS, s - CONV_HALO), CONV_HALO)
        parts.append(ref[0, pl.ds(nxt, CONV_HALO), :])
    return jnp.concatenate(parts, axis=0)


def _conv_taps(win, kw, n):
    return [win[CONV_HALO - (kw - 1 - k):CONV_HALO - (kw - 1 - k) + n] for k in range(kw)]


def _conv_apply(taps, wv, bv):
    pre = bv + wv[0:1, :] * taps[0]
    for k in range(1, len(taps)):
        pre = pre + wv[k:k + 1, :] * taps[k]
    return pre


def _rows8(x):
    acc = x[0:8]
    for i in range(1, x.shape[0] // 8):
        acc = acc + x[8 * i:8 * i + 8]
    return acc


def _conv_grad_step(dpre_ext, taps, wv, is_last):
    kw = wv.shape[0]
    halo = jnp.where(is_last, 0.0, dpre_ext[CONV_ROWS:])
    dpre_ext = jnp.concatenate([dpre_ext[:CONV_ROWS], halo], axis=0)
    dpre = dpre_ext[:CONV_ROWS]
    du = wv[kw - 1:kw, :] * dpre
    for k in range(kw - 1):
        du = du + wv[k:k + 1, :] * dpre_ext[kw - 1 - k:kw - 1 - k + CONV_ROWS]
    sums = [_rows8(dpre * taps[k][:CONV_ROWS]) for k in range(kw)] + [_rows8(dpre)]
    return du, sums


def _conv_store_sums(sums, dw_ref, db_ref, first):
    kw = len(sums) - 1
    vals = [jnp.sum(s_, axis=0, keepdims=True) for s_ in sums]

    @pl.when(first)
    def _():
        for k in range(kw):
            dw_ref[k:k + 1, :] = vals[k]
        db_ref[...] = vals[kw]

    @pl.when(jnp.logical_not(first))
    def _():
        for k in range(kw):
            dw_ref[k:k + 1, :] += vals[k]
        db_ref[...] += vals[kw]


def _dsilu(pre):
    s = _sigmoid(pre)
    return s, s * (1.0 + pre * (1.0 - s))


def _conv_silu_fwd(zx, off, w, b, *, name, tc=128):
    bsz, s, _ = zx.shape
    kw, c = w.shape
    o0 = off // tc

    def body(u_ref, w_ref, b_ref, o_ref):
        wv, bv = w_ref[...], b_ref[...]

        def step(i, carry):
            r0 = pl.multiple_of(i * CONV_ROWS, CONV_ROWS)
            taps = _conv_taps(_conv_window(u_ref, r0, True, False), kw, CONV_ROWS)
            pre = _conv_apply(taps, wv, bv)
            o_ref[0, pl.ds(r0, CONV_ROWS), :] = pre * _sigmoid(pre)
            return carry

        lax.fori_loop(0, s // CONV_ROWS, step, 0)

    return pl.pallas_call(
        body, name=name, grid=(bsz, c // tc),
        in_specs=[pl.BlockSpec((1, s, tc), lambda i, j: (i, 0, o0 + j)),
                  pl.BlockSpec((kw, tc), lambda i, j: (0, j)),
                  pl.BlockSpec((1, tc), lambda i, j: (0, j))],
        out_specs=pl.BlockSpec((1, s, tc), lambda i, j: (i, 0, j)),
        out_shape=jax.ShapeDtypeStruct((bsz, s, c), F32),
        compiler_params=_params("parallel", "parallel"),
    )(zx, w, b.reshape(1, c))


def _conv_silu_bwd(zx, off, w, b, douts, *, name, tc=128):
    bsz, s, _ = zx.shape
    kw, c = w.shape
    o0 = off // tc
    counts = [d.shape[2] // tc for d in douts]
    starts = [sum(counts[:k]) for k in range(len(douts))]
    assert sum(counts) == c // tc

    def body(u_ref, w_ref, b_ref, *rest):
        dy_refs = rest[:len(douts)]
        du_ref, dw_ref, db_ref = rest[len(douts):]
        j = pl.program_id(0)
        wv, bv = w_ref[...], b_ref[...]
        n = CONV_ROWS + CONV_HALO

        def step(i, sums):
            r0 = pl.multiple_of(i * CONV_ROWS, CONV_ROWS)
            taps = _conv_taps(_conv_window(u_ref, r0, True, True), kw, n)
            _, ds = _dsilu(_conv_apply(taps, wv, bv))
            dy = _conv_window(dy_refs[0], r0, False, True)
            for k in range(1, len(douts)):
                dy = jnp.where(j >= starts[k], _conv_window(dy_refs[k], r0, False, True), dy)
            du, new = _conv_grad_step(dy * ds, taps, wv, r0 + CONV_ROWS >= s)
            du_ref[0, pl.ds(r0, CONV_ROWS), :] = du.astype(BF16)
            return tuple(a + b_ for a, b_ in zip(sums, new))

        zero = tuple(jnp.zeros((8, tc), F32) for _ in range(kw + 1))
        sums = lax.fori_loop(0, s // CONV_ROWS, step, zero)
        _conv_store_sums(sums, dw_ref, db_ref, pl.program_id(1) == 0)

    def part_spec(k):
        return pl.BlockSpec((1, s, tc), lambda j, i: (i, 0, jnp.clip(j - starts[k], 0, counts[k] - 1)))

    du, dw, db = pl.pallas_call(
        body, name=name, grid=(c // tc, bsz),
        in_specs=[pl.BlockSpec((1, s, tc), lambda j, i: (i, 0, o0 + j)),
                  pl.BlockSpec((kw, tc), lambda j, i: (0, j)),
                  pl.BlockSpec((1, tc), lambda j, i: (0, j))] + [part_spec(k) for k in range(len(douts))],
        out_specs=[pl.BlockSpec((1, s, tc), lambda j, i: (i, 0, j)),
                   pl.BlockSpec((kw, tc), lambda j, i: (0, j)),
                   pl.BlockSpec((1, tc), lambda j, i: (0, j))],
        out_shape=[jax.ShapeDtypeStruct((bsz, s, c), BF16), jax.ShapeDtypeStruct((kw, c), F32),
                   jax.ShapeDtypeStruct((1, c), F32)],
        compiler_params=_params("parallel", "arbitrary"),
    )(zx, w, b.reshape(1, c), *douts)
    return du, dw, db.reshape(c)


def _conv_glu_fwd(up, w, b, *, name, tc=128):
    bsz, s, c2 = up.shape
    kw = w.shape[0]
    f = c2 // 2
    nt = f // tc

    def body(ug_ref, uv_ref, wg_ref, wv_ref, bg_ref, bv_ref, o_ref):
        wg, wv, bg, bv = wg_ref[...], wv_ref[...], bg_ref[...], bv_ref[...]

        def step(i, carry):
            r0 = pl.multiple_of(i * CONV_ROWS, CONV_ROWS)
            pg = _conv_apply(_conv_taps(_conv_window(ug_ref, r0, True, False), kw, CONV_ROWS), wg, bg)
            pv = _conv_apply(_conv_taps(_conv_window(uv_ref, r0, True, False), kw, CONV_ROWS), wv, bv)
            o_ref[0, pl.ds(r0, CONV_ROWS), :] = (pg * _sigmoid(pg) * pv).astype(BF16)
            return carry

        lax.fori_loop(0, s // CONV_ROWS, step, 0)

    b2 = b.reshape(1, c2)
    return pl.pallas_call(
        body, name=name, grid=(bsz, nt),
        in_specs=[pl.BlockSpec((1, s, tc), lambda i, j: (i, 0, j)),
                  pl.BlockSpec((1, s, tc), lambda i, j: (i, 0, nt + j)),
                  pl.BlockSpec((kw, tc), lambda i, j: (0, j)),
                  pl.BlockSpec((kw, tc), lambda i, j: (0, nt + j)),
                  pl.BlockSpec((1, tc), lambda i, j: (0, j)),
                  pl.BlockSpec((1, tc), lambda i, j: (0, nt + j))],
        out_specs=pl.BlockSpec((1, s, tc), lambda i, j: (i, 0, j)),
        out_shape=jax.ShapeDtypeStruct((bsz, s, f), BF16),
        compiler_params=_params("parallel", "parallel"),
    )(up, up, w, w, b2, b2)


def _conv_glu_bwd(up, w, b, df, *, name, tc=128):
    bsz, s, c2 = up.shape
    kw = w.shape[0]
    f = c2 // 2
    nt = f // tc

    def body(ug_ref, uv_ref, wg_ref, wv_ref, bg_ref, bv_ref, df_ref,
             dug_ref, duv_ref, dwg_ref, dwv_ref, dbg_ref, dbv_ref):
        first = pl.program_id(1) == 0
        wg, wv, bg, bv = wg_ref[...], wv_ref[...], bg_ref[...], bv_ref[...]
        n = CONV_ROWS + CONV_HALO

        def step(i, sums):
            r0 = pl.multiple_of(i * CONV_ROWS, CONV_ROWS)
            is_last = r0 + CONV_ROWS >= s
            tg = _conv_taps(_conv_window(ug_ref, r0, True, True), kw, n)
            tv = _conv_taps(_conv_window(uv_ref, r0, True, True), kw, n)
            pg = _conv_apply(tg, wg, bg)
            pv = _conv_apply(tv, wv, bv)
            sig, dsl = _dsilu(pg)
            dfv = _conv_window(df_ref, r0, False, True)
            dug, new_g = _conv_grad_step(dfv * pv * dsl, tg, wg, is_last)
            duv, new_v = _conv_grad_step(dfv * (pg * sig), tv, wv, is_last)
            dug_ref[0, pl.ds(r0, CONV_ROWS), :] = dug.astype(BF16)
            duv_ref[0, pl.ds(r0, CONV_ROWS), :] = duv.astype(BF16)
            return tuple(a + b_ for a, b_ in zip(sums, new_g + new_v))

        zero = tuple(jnp.zeros((8, tc), F32) for _ in range(2 * (kw + 1)))
        sums = lax.fori_loop(0, s // CONV_ROWS, step, zero)
        _conv_store_sums(sums[:kw + 1], dwg_ref, dbg_ref, first)
        _conv_store_sums(sums[kw + 1:], dwv_ref, dbv_ref, first)

    b2 = b.reshape(1, c2)
    act = lambda j, i: (i, 0, j)
    wsp = pl.BlockSpec((kw, tc), lambda j, i: (0, j))
    bsp = pl.BlockSpec((1, tc), lambda j, i: (0, j))
    dug, duv, dwg, dwv, dbg, dbv = pl.pallas_call(
        body, name=name, grid=(nt, bsz),
        in_specs=[pl.BlockSpec((1, s, tc), act),
                  pl.BlockSpec((1, s, tc), lambda j, i: (i, 0, nt + j)),
                  wsp, pl.BlockSpec((kw, tc), lambda j, i: (0, nt + j)),
                  bsp, pl.BlockSpec((1, tc), lambda j, i: (0, nt + j)),
                  pl.BlockSpec((1, s, tc), act)],
        out_specs=[pl.BlockSpec((1, s, tc), act), pl.BlockSpec((1, s, tc), act), wsp, wsp, bsp, bsp],
        out_shape=[jax.ShapeDtypeStruct((bsz, s, f), BF16)] * 2
        + [jax.ShapeDtypeStruct((kw, f), F32)] * 2 + [jax.ShapeDtypeStruct((1, f), F32)] * 2,
        compiler_params=_params("parallel", "arbitrary"),
    )(up, up, w, w, b2, b2, df)
    return (dug, duv, jnp.concatenate([dwg, dwv], axis=1),
            jnp.concatenate([dbg.reshape(f), dbv.reshape(f)]))


def _ssd_shared(xs, bm, cm, dtc_raw, dtr_raw, plane, psub, st):
    cl = SSM_CHUNK
    bias_l, a_l = plane[0:1, :], plane[1:2, :]
    bias_s, a_s = psub[:, 0:1], psub[:, 1:2]
    ri = lax.broadcasted_iota(I32, (cl, cl), 0)
    ci = lax.broadcasted_iota(I32, (cl, cl), 1)
    tril = ri >= ci
    low_incl = tril.astype(BF16)
    up_incl = (ri <= ci).astype(BF16)
    seg_t = (lax.broadcasted_iota(I32, (LANES, SSM_GROUP_W), 0)
             == lax.broadcasted_iota(I32, (LANES, SSM_GROUP_W), 1) // SSM_HEAD_DIM).astype(BF16)
    dt_c = _softplus(dtc_raw + bias_l)
    cs_c = _dot2_left(low_incl, dt_c * a_l)
    dt_r = _softplus(dtr_raw + bias_s)
    cs_r = _dot2(dt_r * a_s, up_incl)
    dt_ch = _dot2(dt_c, seg_t)
    cs_ch = _dot2(cs_c, seg_t)
    cs_last = cs_ch[cl - 1:cl, :]
    decay_ch = jnp.exp(cs_ch)
    w_ch = jnp.exp(cs_last - cs_ch)
    tot_ch = jnp.exp(cs_last)
    xdt = xs * dt_ch
    bm_b, cm_b = bm.astype(BF16), cm.astype(BF16)
    gmat = _dot_nt(cm_b, bm_b)
    cst = _dot(cm_b, st.astype(BF16))
    yoff = decay_ch * cst
    return dict(tril=tril, low_incl=low_incl, up_incl=up_incl, seg_t=seg_t, a_l=a_l, bias_l=bias_l,
                dt_c=dt_c, cs_c=cs_c, cs_r=cs_r, dt_ch=dt_ch, decay_ch=decay_ch, w_ch=w_ch,
                tot_ch=tot_ch, xdt=xdt, bm_b=bm_b, cm_b=cm_b, gmat=gmat, yoff=yoff)


def _head_decay(q, r):
    diff = q["cs_c"][:, r:r + 1] - q["cs_r"][r:r + 1, :]
    return jnp.where(q["tril"], jnp.exp(jnp.minimum(diff, 0.0)), 0.0)


def _half_mask(hh):
    lane = lax.broadcasted_iota(I32, (SSM_CHUNK, LANES), 1)
    return (lane < SSM_HEAD_DIM) if hh == 0 else (lane >= SSM_HEAD_DIM)


def _ssd_ydiag(q):
    pairs = []
    for pr in range(SSM_HEADS_PER_GROUP // 2):
        xp = q["xdt"][:, pr * LANES:(pr + 1) * LANES]
        acc = None
        for hh in range(2):
            mm_ = (q["gmat"] * _head_decay(q, 2 * pr + hh)).astype(BF16)
            part = _dot(mm_, jnp.where(_half_mask(hh), xp, 0.0).astype(BF16))
            acc = part if acc is None else acc + part
        pairs.append(acc)
    return jnp.concatenate(pairs, axis=1)


def _ssd_specs(bsz, s, g_n, d_inner, rev):
    cl = SSM_CHUNK
    nc = s // cl
    cc = (lambda c: nc - 1 - c) if rev else (lambda c: c)
    gb = d_inner // LANES
    dt0 = (d_inner + d_inner + 2 * g_n * SSM_STATE) // LANES
    return dict(
        z=pl.BlockSpec((1, cl, SSM_GROUP_W), lambda b, g, c: (b, cc(c), g)),
        dtc=pl.BlockSpec((1, cl, LANES), lambda b, g, c: (b, cc(c), dt0 + g)),
        xs=pl.BlockSpec((1, cl, SSM_GROUP_W), lambda b, g, c: (b, cc(c), g)),
        bm=pl.BlockSpec((1, cl, LANES), lambda b, g, c: (b, cc(c), gb + g)),
        cm=pl.BlockSpec((1, cl, LANES), lambda b, g, c: (b, cc(c), gb + g_n + g)),
        dtr=pl.BlockSpec((1, 1, SSM_ROWS, cl), lambda b, g, c: (b, g, 0, cc(c))),
        plane=pl.BlockSpec((1, 8, LANES), lambda b, g, c: (g, 0, 0)),
        psub=pl.BlockSpec((1, SSM_ROWS, LANES), lambda b, g, c: (g, 0, 0)),
        chan=pl.BlockSpec((1, SSM_GROUP_W), lambda b, g, c: (0, g)),
        state=pl.BlockSpec((1, 1, 1, SSM_STATE, SSM_GROUP_W), lambda b, g, c: (b, g, cc(c), 0, 0)),
        bgrp=pl.BlockSpec((1, cl, LANES), lambda b, g, c: (b, cc(c), g)),
    )


def _ssd_fwd(zx, xbc, dtr_row, plane, psub, d_ch, nw, *, name):
    bsz, s, _ = zx.shape
    d_inner = d_ch.shape[1]
    g_n = d_inner // SSM_GROUP_W
    nc = s // SSM_CHUNK
    sp = _ssd_specs(bsz, s, g_n, d_inner, False)

    def body(z_ref, dtc_ref, xs_ref, bm_ref, cm_ref, dtr_ref, plane_ref, psub_ref, d_ref, nw_ref,
             gn_ref, st_out_ref, st_ref):
        @pl.when(pl.program_id(2) == 0)
        def _():
            st_ref[...] = jnp.zeros_like(st_ref)

        xs = xs_ref[0]
        st = st_ref[...]
        st_out_ref[0, 0, 0] = st
        q = _ssd_shared(xs, bm_ref[0], cm_ref[0], dtc_ref[0], dtr_ref[0, 0], plane_ref[0],
                        psub_ref[0], st)
        y = _ssd_ydiag(q) + q["yoff"] + xs * d_ref[...]
        st_ref[...] = q["tot_ch"] * st + _dot_tn(q["bm_b"], (q["w_ch"] * q["xdt"]).astype(BF16))
        zv = z_ref[0]
        gy = y * (zv * _sigmoid(zv))
        rstd = lax.rsqrt(jnp.mean(gy * gy, axis=-1, keepdims=True) + SSM_NORM_EPS)
        gn_ref[0] = ((gy * rstd) * nw_ref[...]).astype(BF16)

    return pl.pallas_call(
        body, name=name, grid=(bsz, g_n, nc),
        in_specs=[sp["z"], sp["dtc"], sp["xs"], sp["bm"], sp["cm"], sp["dtr"], sp["plane"],
                  sp["psub"], sp["chan"], sp["chan"]],
        out_specs=[sp["z"], sp["state"]],
        out_shape=[jax.ShapeDtypeStruct((bsz, s, d_inner), BF16),
                   jax.ShapeDtypeStruct((bsz, g_n, nc, SSM_STATE, SSM_GROUP_W), F32)],
        scratch_shapes=[pltpu.VMEM((SSM_STATE, SSM_GROUP_W), F32)],
        compiler_params=_params("parallel", "parallel", "arbitrary"),
    )(zx, zx, xbc, xbc, xbc, dtr_row, plane, psub, d_ch, nw)


def _ssd_bwd(zx, xbc, dtr_row, plane, psub, d_ch, nw, states, dgn, *, name):
    bsz, s, _ = zx.shape
    d_inner = d_ch.shape[1]
    g_n = d_inner // SSM_GROUP_W
    cl = SSM_CHUNK
    nc = s // cl
    sp = _ssd_specs(bsz, s, g_n, d_inner, True)
    acc_ch = pl.BlockSpec((1, 1, 8, SSM_GROUP_W), lambda b, g, c: (b, g, 0, 0))
    acc_ln = pl.BlockSpec((1, 1, 8, LANES), lambda b, g, c: (b, g, 0, 0))

    def body(z_ref, dtc_ref, xs_ref, bm_ref, cm_ref, dtr_ref, plane_ref, psub_ref, d_ref, nw_ref,
             st_in_ref, dgn_ref,
             dxs_ref, dbm_ref, dcm_ref, dz_ref, ddt_ref, ach_ref, aln_ref, dst_ref):
        first = pl.program_id(2) == 0

        @pl.when(first)
        def _():
            dst_ref[...] = jnp.zeros_like(dst_ref)
            ach_ref[...] = jnp.zeros_like(ach_ref)
            aln_ref[...] = jnp.zeros_like(aln_ref)

        xs = xs_ref[0]
        st = st_in_ref[0, 0, 0]
        q = _ssd_shared(xs, bm_ref[0], cm_ref[0], dtc_ref[0], dtr_ref[0, 0], plane_ref[0],
                        psub_ref[0], st)
        d_chv = d_ref[...]
        nwv = nw_ref[...]
        y = _ssd_ydiag(q) + q["yoff"] + xs * d_chv
        zv = z_ref[0]
        sz = _sigmoid(zv)
        silu_z = zv * sz
        gy = y * silu_z
        rstd = lax.rsqrt(jnp.mean(gy * gy, axis=-1, keepdims=True) + SSM_NORM_EPS)
        gyh = gy * rstd
        dgnv = dgn_ref[0]
        dgyh = dgnv * nwv
        dgy = rstd * (dgyh - gyh * jnp.mean(dgyh * gyh, axis=-1, keepdims=True))
        dy = dgy * silu_z
        dz_ref[0] = (dgy * y * (sz * (1.0 + zv * (1.0 - sz)))).astype(BF16)
        ach_ref[0, 0, 0:1, :] += jnp.sum(dgnv * gyh, axis=0, keepdims=True)
        ach_ref[0, 0, 1:2, :] += jnp.sum(dy * xs, axis=0, keepdims=True)
        st_b = st.astype(BF16)
        dyd = (dy * q["decay_ch"]).astype(BF16)
        dcm = _dot_nt(dyd, st_b)
        dstn = dst_ref[...]
        dstn_b = dstn.astype(BF16)
        bds = _dot(q["bm_b"], dstn_b)
        wx = q["w_ch"] * q["xdt"]
        dbm = _dot_nt(wx.astype(BF16), dstn_b)
        dst_ref[...] = q["tot_ch"] * dstn + _dot_tn(q["cm_b"], dyd)
        vterm = wx * bds
        cs_terms = dy * q["yoff"] - vterm
        last_ch = q["tot_ch"] * jnp.sum(dstn * st, axis=0, keepdims=True) + jnp.sum(vterm, axis=0, keepdims=True)
        lane = lax.broadcasted_iota(I32, (cl, LANES), 1)
        rowi = lax.broadcasted_iota(I32, (SSM_ROWS, cl), 0)
        dg_sum = jnp.zeros((cl, cl), F32)
        dcs_col = jnp.zeros((cl, LANES), F32)
        dcs_row = jnp.zeros((SSM_ROWS, cl), F32)
        dxdt_pairs = []
        for pr in range(SSM_HEADS_PER_GROUP // 2):
            xp_b = q["xdt"][:, pr * LANES:(pr + 1) * LANES].astype(BF16)
            dyp = dy[:, pr * LANES:(pr + 1) * LANES]
            acc = None
            for hh in range(2):
                r = 2 * pr + hh
                dm = _head_decay(q, r)
                mmat = q["gmat"] * dm
                dym = jnp.where(_half_mask(hh), dyp, 0.0).astype(BF16)
                dmat = jnp.where(q["tril"], _dot_nt(dym, xp_b), 0.0)
                part = _dot_tn(mmat.astype(BF16), dym)
                acc = part if acc is None else acc + part
                dg_sum = dg_sum + dmat * dm
                e = dmat * mmat
                dcs_col = dcs_col + jnp.where(lane == r, jnp.sum(e, axis=1, keepdims=True), 0.0)
                dcs_row = dcs_row + jnp.where(rowi == r, jnp.sum(e, axis=0, keepdims=True), 0.0)
            dxdt_pairs.append(acc)
        dg_b = dg_sum.astype(BF16)
        dcm_ref[0] = dcm + _dot(dg_b, q["bm_b"])
        dbm_ref[0] = dbm + _dot_tn(dg_b, q["cm_b"])
        dxdt = q["w_ch"] * bds + jnp.concatenate(dxdt_pairs, axis=1)
        dxs_ref[0] = dy * d_chv + dxdt * q["dt_ch"]
        seg = (lax.broadcasted_iota(I32, (SSM_GROUP_W, LANES), 0) // SSM_HEAD_DIM
               == lax.broadcasted_iota(I32, (SSM_GROUP_W, LANES), 1)).astype(BF16)
        row_as_col = jnp.transpose(jnp.concatenate(
            [dcs_row, jnp.zeros((cl - SSM_ROWS, cl), F32)], axis=0))
        dcs = dcs_col - row_as_col + _dot2(cs_terms, seg)
        last = _dot2(jnp.zeros((8, SSM_GROUP_W), F32) + last_ch, seg)[0:1, :]
        da = _dot2_left(q["up_incl"], dcs) + last
        ddt = _dot2(dxdt * xs, seg) + da * q["a_l"]
        ddtr = ddt * _sigmoid(dtc_ref[0] + q["bias_l"])
        ddt_ref[0] = ddtr.astype(BF16)
        aln_ref[0, 0, 0:1, :] += jnp.sum(ddtr, axis=0, keepdims=True)
        aln_ref[0, 0, 1:2, :] += jnp.sum(da * q["dt_c"], axis=0, keepdims=True)

    outs = pl.pallas_call(
        body, name=name, grid=(bsz, g_n, nc),
        in_specs=[sp["z"], sp["dtc"], sp["xs"], sp["bm"], sp["cm"], sp["dtr"], sp["plane"],
                  sp["psub"], sp["chan"], sp["chan"], sp["state"], sp["z"]],
        out_specs=[sp["z"], sp["bgrp"], sp["bgrp"], sp["z"], sp["bgrp"], acc_ch, acc_ln],
        out_shape=[jax.ShapeDtypeStruct((bsz, s, d_inner), F32),
                   jax.ShapeDtypeStruct((bsz, s, g_n * SSM_STATE), F32),
                   jax.ShapeDtypeStruct((bsz, s, g_n * SSM_STATE), F32),
                   jax.ShapeDtypeStruct((bsz, s, d_inner), BF16),
                   jax.ShapeDtypeStruct((bsz, s, g_n * LANES), BF16),
                   jax.ShapeDtypeStruct((bsz, g_n, 8, SSM_GROUP_W), F32),
                   jax.ShapeDtypeStruct((bsz, g_n, 8, LANES), F32)],
        scratch_shapes=[pltpu.VMEM((SSM_STATE, SSM_GROUP_W), F32)],
        compiler_params=_params("parallel", "parallel", "arbitrary"),
    )(zx, zx, xbc, xbc, xbc, dtr_row, plane, psub, d_ch, nw, states, dgn)
    return outs


def _sb_stack(x):
    out = []
    for i in range(x.shape[0] // SB_BLOCK):
        xb = x[i * SB_BLOCK:(i + 1) * SB_BLOCK]
        lane = lax.broadcasted_iota(I32, xb.shape, 1)
        zero = jnp.zeros_like(xb)
        out += [jnp.where(lane < SB_HEAD_DIM, xb, zero), jnp.where(lane >= SB_HEAD_DIM, xb, zero)]
    return jnp.concatenate(out, axis=0)


def _sb_unstack_t(acc_t):
    row = lax.broadcasted_iota(I32, (LANES, SB_BLOCK), 0)
    out = []
    for i in range(acc_t.shape[1] // (2 * SB_BLOCK)):
        a = acc_t[:, 2 * i * SB_BLOCK:(2 * i + 1) * SB_BLOCK]
        b = acc_t[:, (2 * i + 1) * SB_BLOCK:(2 * i + 2) * SB_BLOCK]
        out.append(jnp.transpose(jnp.where(row < SB_HEAD_DIM, a, b)))
    return jnp.concatenate(out, axis=0)


def _sb_tile_blocks(nq, q_blocks):
    nb = 4 if nq % 4 == 0 else (2 if nq % 2 == 0 else 1)
    return nb, min(nb, q_blocks)


def _sb_valid(u, qi0, nb, nqb):
    shape = (nb * SB_BLOCK, nqb * 2 * SB_BLOCK)
    key = u * (nb * SB_BLOCK) + lax.broadcasted_iota(I32, shape, 0)
    col = lax.broadcasted_iota(I32, shape, 1)
    qpos = (qi0 + col // (2 * SB_BLOCK)) * SB_BLOCK + col % SB_BLOCK
    return key < qpos


def _sb_logits(kb, qs, valid):
    z = _dot_nt(kb, qs)
    lb = jnp.minimum(z, 0.0) - jnp.log(1.0 + jnp.exp(-jnp.abs(z)))
    lk_all = lb - z
    lk = lk_all if valid is None else jnp.where(valid, lk_all, 0.0)
    return z, lb, lk_all, lk


def _sb_diag(x):
    w2 = 2 * SB_BLOCK
    ri = lax.broadcasted_iota(I32, (SB_BLOCK, w2), 0)
    ci = lax.broadcasted_iota(I32, (SB_BLOCK, w2), 1) % SB_BLOCK
    first = jnp.where(ri < ci, x[:, :w2], 0.0)
    return first if x.shape[1] == w2 else jnp.concatenate([first, x[:, w2:]], axis=1)


def _sb_add_from(full, part, lo):
    if lo == 0:
        return full + part
    return jnp.concatenate([full[:, :lo], full[:, lo:] + part], axis=1)


def _sb_scan(tri2, x, nb, reverse, exact=True):
    blk = SB_BLOCK
    edge = 0 if reverse else blk - 1
    carry = jnp.zeros((1, x.shape[1]), F32)
    res = [None] * nb
    for i in (reversed(range(nb)) if reverse else range(nb)):
        part = x[i * blk:(i + 1) * blk]
        if exact:
            hi, lo = _split2(part)
            raw = _dot(tri2, jnp.concatenate([hi, lo], axis=0))
        else:
            raw = _dot(tri2[:, :blk], part.astype(BF16))
        res[i] = raw + carry
        carry = carry + (raw[edge:edge + 1] + part[edge:edge + 1])
    return jnp.concatenate(res, axis=0), carry


def _sb_fwd(q, kv, kvt, *, name):
    bsz, s, w = q.shape
    blk = SB_BLOCK
    npair = w // LANES
    nq = s // blk
    nb, nqb = _sb_tile_blocks(nq, SB_Q_BLOCKS_FWD)
    width = nqb * 2 * blk

    def body(q_ref, k_ref, vt_ref, o_ref, tot_ref):
        qi0 = pl.program_id(2) * nqb
        qs = _sb_stack(q_ref[0] * SB_SCALE)
        ri = lax.broadcasted_iota(I32, (blk, blk), 0)
        ci = lax.broadcasted_iota(I32, (blk, blk), 1)
        upper = (ri < ci).astype(BF16)
        tri2 = jnp.concatenate([upper, upper], axis=1)

        def tile(u, r, acc, masked):
            rows = pl.ds(pl.multiple_of(u * (nb * blk), nb * blk), nb * blk)
            valid = _sb_valid(u, qi0, nb, nqb) if masked else None
            _, lb, _, lk = _sb_logits(k_ref[0, rows, :], qs, valid)
            sfx, total = _sb_scan(tri2, lk, nb, True)
            wgt = jnp.exp(lb + sfx + r)
            if masked:
                wgt = jnp.where(valid, wgt, 0.0)
            wb = wgt.astype(BF16)
            for i in range(nb):
                acc = acc + _dot(vt_ref[0, 0, u * nb + i], wb[i * blk:(i + 1) * blk])
            return r + total, acc

        def top_tile(u):
            r = jnp.zeros((1, width), F32)
            acc = jnp.zeros((LANES, width), F32)
            for kb in reversed(range(nb)):
                lo = kb * 2 * blk
                rows = pl.ds(pl.multiple_of((u * nb + kb) * blk, blk), blk)
                _, lb, lk_all, _ = _sb_logits(k_ref[0, rows, :], qs[lo:], None)
                lk = _sb_diag(lk_all)
                hi, lo_part = _split2(lk)
                raw = _dot(tri2, jnp.concatenate([hi, lo_part], axis=0))
                wgt = _sb_diag(jnp.exp(lb + raw + r[:, lo:]))
                add_acc = _dot(vt_ref[0, 0, u * nb + kb], wgt.astype(BF16))
                add_r = raw[0:1] + lk[0:1]
                acc = _sb_add_from(acc, add_acc, lo)
                r = _sb_add_from(r, add_r, lo)
            return r, acc

        top = qi0 // nb
        zero_r, zero_acc = jnp.zeros((1, width), F32), jnp.zeros((LANES, width), F32)
        r, acc = top_tile(top) if nb == nqb else tile(top, zero_r, zero_acc, True)
        r, acc = lax.fori_loop(0, top, lambda t, c: tile(top - 1 - t, c[0], c[1], False), (r, acc))
        o_ref[0] = _sb_unstack_t(acc).astype(BF16)
        tot_ref[0, 0, 0] = r

    qspec = pl.BlockSpec((1, nqb * blk, LANES), lambda b, p, i: (b, i, p))
    return pl.pallas_call(
        body, name=name, grid=(bsz, npair, nq // nqb),
        in_specs=[qspec,
                  pl.BlockSpec((1, s, LANES), lambda b, p, i: (b, 0, p)),
                  pl.BlockSpec((1, 1, nq, LANES, blk), lambda b, p, i: (b, npair + p, 0, 0, 0))],
        out_specs=[qspec, pl.BlockSpec((1, 1, 1, 1, width), lambda b, p, i: (b, p, i, 0, 0))],
        out_shape=[jax.ShapeDtypeStruct((bsz, s, w), BF16),
                   jax.ShapeDtypeStruct((bsz, npair, nq // nqb, 1, width), F32)],
        compiler_params=_params("parallel", "parallel", "arbitrary"),
    )(q, kv, kvt)


def _kv_blocks_t(kv3):
    bsz, s, w2 = kv3.shape
    x = kv3.reshape(bsz, s // SB_BLOCK, SB_BLOCK, w2 // LANES, LANES)
    return jnp.transpose(x, (0, 3, 1, 4, 2))


def _sb_bwd(q, kv, kvt, do, tot, dk_in, dv_in, *, name):
    bsz, s, w = q.shape
    blk = SB_BLOCK
    npair = w // LANES
    nq = s // blk
    nb, nqb = _sb_tile_blocks(nq, SB_Q_BLOCKS_BWD)
    width = nqb * 2 * blk
    tot = tot.reshape(bsz, npair, nq // nqb, 1, width)
    has_init = dk_in is not None

    def body(*refs):
        if has_init:
            q_ref, k_ref, v_ref, kt_ref, do_ref, tot_ref, dki_ref, dvi_ref, dq_ref, dk_ref, dv_ref = refs
        else:
            q_ref, k_ref, v_ref, kt_ref, do_ref, tot_ref, dq_ref, dk_ref, dv_ref = refs
        qi0 = pl.program_id(2) * nqb

        @pl.when(qi0 == 0)
        def _():
            if has_init:
                dk_ref[...] = dki_ref[...]
                dv_ref[...] = dvi_ref[...]
            else:
                dk_ref[...] = jnp.zeros_like(dk_ref)
                dv_ref[...] = jnp.zeros_like(dv_ref)

        qs = _sb_stack(q_ref[0] * SB_SCALE)
        dos = _sb_stack(do_ref[0])
        totv = tot_ref[0, 0, 0]
        ri = lax.broadcasted_iota(I32, (blk, blk), 0)
        ci = lax.broadcasted_iota(I32, (blk, blk), 1)
        lower = (ri > ci).astype(BF16)
        tri2 = jnp.concatenate([lower, lower], axis=1)

        def tile(u, pre_lk, pre_d, dqt, masked):
            rows = pl.ds(pl.multiple_of(u * (nb * blk), nb * blk), nb * blk)
            valid = _sb_valid(u, qi0, nb, nqb) if masked else None
            z, lb, lk_all, lk = _sb_logits(k_ref[0, rows, :], qs, valid)
            before, tot_lk = _sb_scan(tri2, lk, nb, False)
            wgt = jnp.exp(z + ((totv - pre_lk) - before))
            if masked:
                wgt = jnp.where(valid, wgt, 0.0)
            dlogit = _dot_nt(v_ref[0, rows, :], dos) * wgt
            dbefore, tot_d = _sb_scan(tri2, dlogit, nb, False, exact=False)
            sig = jnp.exp(lb)
            dz = dlogit * (1.0 - sig) - (pre_d + dbefore) * sig
            if masked:
                dz = jnp.where(valid, dz, 0.0)
            dz_b = dz.astype(BF16)
            for i in range(nb):
                dqt = dqt + _dot(kt_ref[0, 0, u * nb + i], dz_b[i * blk:(i + 1) * blk])
            dk_ref[0, rows, :] += _dot(dz_b, qs)
            dv_ref[0, rows, :] += _dot(wgt.astype(BF16), dos)
            return pre_lk + tot_lk, pre_d + tot_d, dqt

        def top_tile(u, pre_lk, pre_d, dqt):
            rest = totv - pre_lk
            for kb in range(nb):
                lo = kb * 2 * blk
                rows = pl.ds(pl.multiple_of((u * nb + kb) * blk, blk), blk)
                qs_k, dos_k = qs[lo:], dos[lo:]
                z, lb, lk_all, _ = _sb_logits(k_ref[0, rows, :], qs_k, None)
                lk = _sb_diag(lk_all)
                hi, lo_part = _split2(lk)
                raw = _dot(tri2, jnp.concatenate([hi, lo_part], axis=0))
                wgt = _sb_diag(jnp.exp(z + (rest[:, lo:] - raw)))
                dlogit = _dot_nt(v_ref[0, rows, :], dos_k) * wgt
                draw = _dot(tri2[:, :blk], dlogit.astype(BF16))
                sig = jnp.exp(lb)
                dz_b = _sb_diag(dlogit * (1.0 - sig) - (pre_d[:, lo:] + draw) * sig).astype(BF16)
                dqt = _sb_add_from(dqt, _dot(kt_ref[0, 0, u * nb + kb], dz_b), lo)
                dk_ref[0, rows, :] += _dot(dz_b, qs_k)
                dv_ref[0, rows, :] += _dot(wgt.astype(BF16), dos_k)
                rest = _sb_add_from(rest, -(raw[blk - 1:blk] + lk[blk - 1:blk]), lo)
                pre_d = _sb_add_from(pre_d, draw[blk - 1:blk] + dlogit[blk - 1:blk], lo)
            return dqt

        zero = jnp.zeros((1, width), F32)
        top = qi0 // nb
        c = lax.fori_loop(0, top, lambda u, c: tile(u, c[0], c[1], c[2], False),
                          (zero, zero, jnp.zeros((LANES, width), F32)))
        dqt = top_tile(top, *c) if nb == nqb else tile(top, c[0], c[1], c[2], True)[2]
        dq_ref[0] = (_sb_unstack_t(dqt) * SB_SCALE).astype(BF16)

    qspec = pl.BlockSpec((1, nqb * blk, LANES), lambda b, p, i: (b, i, p))
    kspec = pl.BlockSpec((1, s, LANES), lambda b, p, i: (b, 0, p))
    vspec = pl.BlockSpec((1, s, LANES), lambda b, p, i: (b, 0, npair + p))
    ktspec = pl.BlockSpec((1, 1, nq, LANES, blk), lambda b, p, i: (b, p, 0, 0, 0))
    tspec = pl.BlockSpec((1, 1, 1, 1, width), lambda b, p, i: (b, p, i, 0, 0))
    in_specs = [qspec, kspec, vspec, ktspec, qspec, tspec] + ([kspec, kspec] if has_init else [])
    args = (q, kv, kv, kvt, do, tot) + ((dk_in, dv_in) if has_init else ())
    return pl.pallas_call(
        body, name=name, grid=(bsz, npair, nq // nqb), in_specs=in_specs,
        out_specs=[qspec, kspec, kspec],
        out_shape=[jax.ShapeDtypeStruct((bsz, s, w), BF16), jax.ShapeDtypeStruct((bsz, s, w), F32),
                   jax.ShapeDtypeStruct((bsz, s, w), F32)],
        compiler_params=_params("parallel", "parallel", "arbitrary"),
    )(*args)


ADAM_BLOCK_BYTES = 1 << 20


def _adamw(w, g, m, v, *, name):
    shape = w.shape
    r, c = shape[-2], shape[-1]
    lead = _size(shape[:-2])
    tr = r
    for cand in range(8, r, 8):
        if r % cand == 0 and cand * c * 4 <= ADAM_BLOCK_BYTES:
            tr = cand
    if r * c * 4 <= ADAM_BLOCK_BYTES:
        tr = r

    def body(w_ref, g_ref, m_ref, v_ref, d_ref, mo_ref, vo_ref):
        gv = g_ref[...]
        mn = ADAM_B1 * m_ref[...] + (1.0 - ADAM_B1) * gv
        vn = ADAM_B2 * v_ref[...] + (1.0 - ADAM_B2) * (gv * gv)
        m_hat = mn / (1.0 - ADAM_B1 ** ADAM_STEP)
        v_hat = vn / (1.0 - ADAM_B2 ** ADAM_STEP)
        d_ref[...] = -ADAM_LR * (m_hat / (jnp.sqrt(v_hat) + ADAM_EPS) + ADAM_WD * w_ref[...])
        mo_ref[...] = mn
        vo_ref[...] = vn

    blk = pl.BlockSpec((1, tr, c), lambda l, i: (l, i, 0))
    outs = pl.pallas_call(
        body, name=name, grid=(lead, r // tr), in_specs=[blk] * 4, out_specs=[blk] * 3,
        out_shape=[jax.ShapeDtypeStruct((lead, r, c), F32)] * 3,
        compiler_params=_params("parallel", "parallel"),
    )(*[a.reshape(lead, r, c) for a in (w, g, m, v)])
    return [o.reshape(shape) for o in outs]


def _row_tile(r, c, itemsize):
    if r * c * 4 <= ADAM_BLOCK_BYTES:
        return r
    step = 32 // itemsize
    tr = r
    for cand in range(step, r, step):
        if r % cand == 0 and cand * c * 4 <= ADAM_BLOCK_BYTES:
            tr = cand
    return tr


def _add_own_half(idx, g, recv, *, name):
    _, lh, r, c = recv.shape
    tr = _row_tile(r, c, g.dtype.itemsize)

    def body(idx_ref, a_ref, b_ref, o_ref):
        o_ref[...] = (a_ref[...].astype(F32) + b_ref[...].astype(F32)).astype(o_ref.dtype)

    blk = pl.BlockSpec((1, 1, tr, c), lambda k, l, i, idx: (k, l, i, 0))
    return pl.pallas_call(
        body, name=name,
        grid_spec=pltpu.PrefetchScalarGridSpec(
            num_scalar_prefetch=1, grid=(N_CHIPS, lh, r // tr),
            in_specs=[pl.BlockSpec((1, 1, tr, c), lambda k, l, i, idx: (k, idx[0] * lh + l, i, 0)), blk],
            out_specs=blk),
        out_shape=jax.ShapeDtypeStruct(recv.shape, g.dtype),
        compiler_params=_params("parallel", "parallel", "parallel"),
    )(idx, g, recv)


def _add_chips(idx, own, recv, *, name):
    _, lh, r, c = own.shape
    tr = _row_tile(r, c, own.dtype.itemsize)

    def body(idx_ref, a_ref, b_ref, o_ref):
        f = lambda v: v.astype(F32)
        o_ref[0] = ((f(a_ref[0, 0]) + f(b_ref[0, 0])) + f(b_ref[1, 0])) + f(b_ref[2, 0])

    return pl.pallas_call(
        body, name=name,
        grid_spec=pltpu.PrefetchScalarGridSpec(
            num_scalar_prefetch=1, grid=(lh, r // tr),
            in_specs=[pl.BlockSpec((1, 1, tr, c), lambda l, i, idx: (idx[0], l, i, 0)),
                      pl.BlockSpec((3, 1, tr, c), lambda l, i, idx: (0, l, i, 0))],
            out_specs=pl.BlockSpec((1, tr, c), lambda l, i, idx: (l, i, 0))),
        out_shape=jax.ShapeDtypeStruct((lh, r, c), F32),
        compiler_params=_params("parallel", "parallel"),
    )(idx, own, recv)


def _sum_devices(parts):
    _, r, _ = parts.shape

    def body(p_ref, o_ref):
        acc = p_ref[0]
        for k in range(1, N_DEV):
            acc = acc + p_ref[k]
        o_ref[...] = acc

    return pl.pallas_call(
        body, name="small_grad_sum", grid=(1,),
        in_specs=[pl.BlockSpec((N_DEV, r, LANES), lambda i: (0, 0, 0))],
        out_specs=pl.BlockSpec((r, LANES), lambda i: (0, 0)),
        out_shape=jax.ShapeDtypeStruct((r, LANES), F32),
    )(parts)


def _place():
    return lax.axis_index("x"), lax.axis_index("y"), lax.axis_index("c")


def _rcopy(src, dst, send_sems, recv_sems, k, to):
    return pltpu.make_async_remote_copy(src_ref=src, dst_ref=dst, send_sem=send_sems.at[k],
                                        recv_sem=recv_sems.at[k], device_id=to, device_id_type=MESH)


def _exchange_call(body, name, ins, out_shapes, n_sems):
    return pl.pallas_call(
        body, name=name, in_specs=[ANY] * len(ins), out_specs=[ANY] * len(out_shapes),
        out_shape=out_shapes,
        scratch_shapes=[pltpu.SemaphoreType.DMA((n_sems,)), pltpu.SemaphoreType.DMA((n_sems,))],
    )(*ins)


def _gather_weights(shards):
    n = len(shards)

    def body(*refs):
        ins, outs, send_sems, recv_sems = refs[:n], refs[n:2 * n], refs[2 * n], refs[2 * n + 1]
        x, y, c = _place()
        sibling = (x, y, 1 - c)
        chips = [(1 - x, y), (x, 1 - y), (1 - x, 1 - y)]

        def piece(i, px, py, pc):
            lh = ins[i].shape[0] // 2
            return outs[i].at[2 * px + py, pl.ds(pc * lh, lh)]

        def mine(i):
            lh = ins[i].shape[0] // 2
            return ins[i].at[pl.ds(c * lh, lh)]

        first = [_rcopy(mine(i), piece(i, x, y, c), send_sems, recv_sems, 6 * i + j, (*chip, c))
                 for i in range(n) for j, chip in enumerate(chips)]
        for cp in first:
            cp.start()
        passed = []
        for i in range(n):
            for j, chip in enumerate(chips):
                landed = piece(i, *chip, c)
                _rcopy(landed, landed, send_sems, recv_sems, 6 * i + j, (*chip, c)).wait_recv()
                passed.append(_rcopy(landed, landed, send_sems, recv_sems, 6 * i + 3 + j, sibling))
                passed[-1].start()
        for i in range(n):
            for j, chip in enumerate(chips):
                theirs = piece(i, *chip, 1 - c)
                _rcopy(theirs, theirs, send_sems, recv_sems, 6 * i + 3 + j, sibling).wait_recv()
        for cp in first + passed:
            cp.wait_send()

    shapes = [jax.ShapeDtypeStruct((N_CHIPS,) + s_.shape, s_.dtype) for s_ in shards]
    return _exchange_call(body, "gather_weights", shards, shapes, 6 * n)


def _swap_halves(gs):
    n = len(gs)

    def body(*refs):
        ins, outs, send_sems, recv_sems = refs[:n], refs[n:2 * n], refs[2 * n], refs[2 * n + 1]
        x, y, c = _place()
        cps = []
        for i in range(n):
            lh = ins[i].shape[1] // 2
            src = ins[i].at[pl.ds(0, N_CHIPS), pl.ds((1 - c) * lh, lh)]
            cps.append(_rcopy(src, outs[i], send_sems, recv_sems, i, (x, y, 1 - c)))
        for cp in cps:
            cp.start()
        for cp in cps:
            cp.wait()

    shapes = [jax.ShapeDtypeStruct((N_CHIPS, g.shape[1] // 2) + g.shape[2:], g.dtype) for g in gs]
    return _exchange_call(body, "grad_swap_halves", gs, shapes, n)


def _scatter_chunks(sums):
    n = len(sums)

    def body(*refs):
        ins, outs, send_sems, recv_sems = refs[:n], refs[n:2 * n], refs[2 * n], refs[2 * n + 1]
        x, y, c = _place()
        chips = [(1 - x, y), (x, 1 - y), (1 - x, 1 - y)]
        cps = [_rcopy(ins[i].at[2 * chip[0] + chip[1]], outs[i].at[j], send_sems, recv_sems, 3 * i + j,
                      (*chip, c)) for i in range(n) for j, chip in enumerate(chips)]
        for cp in cps:
            cp.start()
        for cp in cps:
            cp.wait()

    shapes = [jax.ShapeDtypeStruct((3,) + s_.shape[1:], s_.dtype) for s_ in sums]
    return _exchange_call(body, "grad_scatter_chunks", sums, shapes, 3 * n)


def _share_half(tots):
    n = len(tots)

    def body(*refs):
        ins, outs, send_sems, recv_sems = refs[:n], refs[n:2 * n], refs[2 * n], refs[2 * n + 1]
        x, y, c = _place()
        cps = [_rcopy(ins[i], outs[i], send_sems, recv_sems, i, (x, y, 1 - c)) for i in range(n)]
        for cp in cps:
            cp.start()
        for cp in cps:
            cp.wait()

    shapes = [jax.ShapeDtypeStruct(t_.shape, t_.dtype) for t_ in tots]
    return _exchange_call(body, "grad_share_half", tots, shapes, n)


def _exchange_small(r):
    rr, _ = r.shape

    def body(r_ref, out_ref, send_sems, recv_sems, local_sem):
        x, y, c = _place()
        me = 4 * x + 2 * y + c
        mine = pltpu.make_async_copy(r_ref, out_ref.at[me], local_sem)
        mine.start()
        cps = []
        for k in range(N_DEV - 1):
            fx, fy, fc = ((k + 1) >> 2) & 1, ((k + 1) >> 1) & 1, (k + 1) & 1
            to = (x ^ fx, y ^ fy, c ^ fc)
            cps.append((_rcopy(r_ref, out_ref.at[me], send_sems, recv_sems, k, to), to))
        for cp, _ in cps:
            cp.start()
        for k, (cp, to) in enumerate(cps):
            src = 4 * to[0] + 2 * to[1] + to[2]
            _rcopy(r_ref, out_ref.at[src], send_sems, recv_sems, k, to).wait_recv()
        for cp, _ in cps:
            cp.wait_send()
        mine.wait()

    return pl.pallas_call(
        body, name="small_grad_exchange", in_specs=[ANY], out_specs=ANY,
        out_shape=jax.ShapeDtypeStruct((N_DEV, rr, LANES), r.dtype),
        scratch_shapes=[pltpu.SemaphoreType.DMA((N_DEV - 1,)), pltpu.SemaphoreType.DMA((N_DEV - 1,)),
                        pltpu.SemaphoreType.DMA],
    )(r)


def _size(shape):
    n = 1
    for d in shape:
        n *= d
    return n


def _slab_rows(shape):
    rows = -(-_size(shape) // LANES)
    return -(-rows // SLAB_ROW_ALIGN) * SLAB_ROW_ALIGN


def _pack_rows(arrs, dtype, lead=0, unit=PACK_ROWS):
    parts, total = [], 0
    for a in arrs:
        front, shp = a.shape[:lead], a.shape[lead:]
        n, rows = _size(shp), _slab_rows(shp)
        nopad = [(0, 0)] * lead
        if n % LANES == 0:
            p = a.reshape(front + (n // LANES, LANES)).astype(dtype)
        else:
            p = jnp.pad(a.reshape(front + (n,)).astype(dtype), nopad + [(0, rows * LANES - n)])
            p = p.reshape(front + (rows, LANES))
        if p.shape[lead] != rows:
            p = jnp.pad(p, nopad + [(0, rows - p.shape[lead]), (0, 0)])
        parts.append(p)
        total += rows
    pad = (-total) % unit
    if pad:
        parts.append(jnp.zeros(parts[0].shape[:lead] + (pad, LANES), dtype))
    return jnp.concatenate(parts, axis=lead)


def _unpack_rows(slab, shapes):
    lead = slab.shape[:-2]
    out, off = [], 0
    for shp in shapes:
        n, rows = _size(shp), _slab_rows(shp)
        piece = slab[..., off:off + rows, :]
        if n % LANES == 0:
            piece = piece[..., :n // LANES, :].reshape(lead + tuple(shp))
        else:
            piece = piece.reshape(lead + (rows * LANES,))[..., :n].reshape(lead + tuple(shp))
        out.append(piece)
        off += rows
    return out


def _ffn_fwd(h, bsz, s, gain, w_up, cw, cb, w_down, i):
    hf = _rmsnorm_fwd(h, gain, name=f"ffn_norm_{i}")
    up = _mm(hf, w_up, name=f"ffn_up_{i}")
    up3 = up.reshape(bsz, s, -1)
    f = _conv_glu_fwd(up3, cw, cb, name=f"ffn_glu_{i}").reshape(h.shape[0], -1)
    h2 = _mm(f, w_down, add=h, name=f"ffn_down_{i}")
    return h2, (h, hf, up3, f)


def _ffn_bwd(dh, saved, gain, w_up, cw, cb, w_down, i):
    h, hf, up3, f = saved
    t = h.shape[0]
    d_down = _mm(f, dh, ta=True, name=f"ffn_down_dw_{i}")
    df = _mm(dh, w_down, tb=True, name=f"ffn_down_dx_{i}")
    dug, duv, dcw, dcb = _conv_glu_bwd(up3, cw, cb, df.reshape(up3.shape[0], up3.shape[1], -1),
                                       name=f"ffn_glu_bwd_{i}")
    dug, duv = dug.reshape(t, -1), duv.reshape(t, -1)
    fdim = dug.shape[1]
    d_up = jnp.concatenate([_mm(hf, dug, ta=True, name=f"ffn_up_dwg_{i}"),
                            _mm(hf, duv, ta=True, name=f"ffn_up_dwv_{i}")], axis=1)
    dhf = _mm(dug, w_up[:, :fdim], tb=True, name=f"ffn_up_dxg_{i}")
    dhf = _mm(duv, w_up[:, fdim:], tb=True, add=dhf, name=f"ffn_up_dxv_{i}")
    dh, dgain = _rmsnorm_bwd(h, gain, dhf, dh, name=f"ffn_norm_bwd_{i}")
    return dh, dgain, d_up, dcw, dcb, d_down


def _ple_layer_fwd(h, p_i, gain, w_gate, w_proj, i):
    hp = _rmsnorm_fwd(h, gain, name=f"ple_norm_{i}")
    a = _mm(hp, w_gate, name=f"ple_gate_{i}")
    pp = _mm(p_i, w_proj, name=f"ple_proj_{i}")
    return _ple_fwd(h, a, pp, name=f"ple_mix_{i}"), (h, hp, a, pp)


def _ple_layer_bwd(dh, saved, p_i, gain, w_gate, i):
    h, hp, a, pp = saved
    da, dpp = _ple_bwd(dh, a, pp, name=f"ple_mix_bwd_{i}")
    d_gate = _mm(hp, da, ta=True, name=f"ple_gate_dw_{i}")
    d_proj = _mm(p_i, dpp, ta=True, name=f"ple_proj_dw_{i}")
    dhp = _mm(da, w_gate, tb=True, name=f"ple_gate_dx_{i}")
    dh, dgain = _rmsnorm_bwd(h, gain, dhp, dh, name=f"ple_norm_bwd_{i}")
    return dh, dgain, d_gate, d_proj


def _ssm_consts(dt_bias, a_log, d_skip, g_n):
    hpg = SSM_HEADS_PER_GROUP
    a = -jnp.exp(a_log)
    rows = jnp.stack([dt_bias.reshape(g_n, hpg), a.reshape(g_n, hpg)], axis=1)
    plane = jnp.zeros((g_n, 8, LANES), F32).at[:, 0:2, 0:hpg].set(rows)
    psub = jnp.zeros((g_n, SSM_ROWS, LANES), F32).at[:, 0:hpg, 0:2].set(jnp.swapaxes(rows, 1, 2))
    d_ch = jnp.repeat(d_skip, SSM_HEAD_DIM).reshape(1, -1)
    return a, plane, psub, d_ch


def _ssm_in_big(w_in, d_inner, g_n):
    d = w_in.shape[0]
    cut = w_in.shape[1] - g_n * SSM_HEADS_PER_GROUP
    wdt = w_in[:, cut:].reshape(d, g_n, SSM_HEADS_PER_GROUP)
    wdt = jnp.pad(wdt, ((0, 0), (0, 0), (0, LANES - SSM_HEADS_PER_GROUP))).reshape(d, g_n * LANES)
    return jnp.concatenate([w_in[:, :cut], wdt], axis=1)


def _ssm_in_small(dw_big, g_n):
    d = dw_big.shape[0]
    cut = dw_big.shape[1] - g_n * LANES
    ddt = dw_big[:, cut:].reshape(d, g_n, LANES)[:, :, :SSM_HEADS_PER_GROUP].reshape(d, -1)
    return jnp.concatenate([dw_big[:, :cut], ddt], axis=1)


def _ssm_fwd(h, bsz, s, gain, w_in_big, cw, cb, plane, psub, d_ch, nw, w_out, i):
    d_inner = d_ch.shape[1]
    g_n = d_inner // SSM_GROUP_W
    conv_dim = cw.shape[1]
    hn = _rmsnorm_fwd(h, gain, name=f"attn_norm_{i}")
    zx = _mm(hn, w_in_big, name=f"ssm_in_{i}").reshape(bsz, s, -1)
    xbc = _conv_silu_fwd(zx, d_inner, cw, cb, name=f"ssm_conv_{i}")
    dtr = zx[:, :, d_inner + conv_dim:].reshape(bsz, s, g_n, LANES)[..., :SSM_HEADS_PER_GROUP]
    dtr_row = jnp.pad(jnp.transpose(dtr, (0, 2, 3, 1)),
                      ((0, 0), (0, 0), (0, SSM_ROWS - SSM_HEADS_PER_GROUP), (0, 0)))
    gn, states = _ssd_fwd(zx, xbc, dtr_row, plane, psub, d_ch, nw, name=f"ssd_{i}")
    gn2 = gn.reshape(h.shape[0], -1)
    h1 = _mm(gn2, w_out, add=h, name=f"ssm_out_{i}")
    return h1, (h, hn, zx, xbc, dtr_row, states, gn2)


def _ssm_bwd(dh, saved, gain, w_in_big, cw, cb, plane, psub, d_ch, nw, w_out, i):
    h, hn, zx, xbc, dtr_row, states, gn2 = saved
    t = h.shape[0]
    bsz, s, _ = zx.shape
    d_inner = d_ch.shape[1]
    d_out = _mm(gn2, dh, ta=True, name=f"ssm_out_dw_{i}")
    dgn = _mm(dh, w_out, tb=True, name=f"ssm_out_dx_{i}").reshape(bsz, s, -1)
    dxs, dbm, dcm, dz, ddtr, ach, aln = _ssd_bwd(zx, xbc, dtr_row, plane, psub, d_ch, nw, states, dgn,
                                                  name=f"ssd_bwd_{i}")
    dxbc, dcw, dcb = _conv_silu_bwd(zx, d_inner, cw, cb, [dxs, dbm, dcm], name=f"ssm_conv_bwd_{i}")
    d_in_parts, dhn, col = [], None, 0
    for tag, part in (("z", dz), ("xbc", dxbc), ("dt", ddtr)):
        part = part.reshape(t, -1)
        d_in_parts.append(_mm(hn, part, ta=True, name=f"ssm_in_dw_{tag}_{i}"))
        dhn = _mm(part, w_in_big[:, col:col + part.shape[1]], tb=True, add=dhn,
                  name=f"ssm_in_dx_{tag}_{i}")
        col += part.shape[1]
    d_in_big = jnp.concatenate(d_in_parts, axis=1)
    dh, dgain = _rmsnorm_bwd(h, gain, dhn, dh, name=f"attn_norm_bwd_{i}")
    hpg = SSM_HEADS_PER_GROUP
    ach = jnp.sum(ach, axis=0)
    aln = jnp.sum(aln, axis=0)
    d_nw = ach[:, 0, :].reshape(-1)
    d_dskip = jnp.sum(ach[:, 1, :].reshape(-1, SSM_HEAD_DIM), axis=1)
    d_bias = aln[:, 0, :hpg].reshape(-1)
    d_a = aln[:, 1, :hpg].reshape(-1)
    return dh, dgain, d_in_big, dcw, dcb, d_bias, d_a, d_dskip, d_nw, d_out


def _sb_layer_fwd(h, bsz, s, gain, w_q, w_o, kv3, kvt, i):
    hn = _rmsnorm_fwd(h, gain, name=f"attn_norm_{i}")
    q3 = _mm(hn, w_q, out_dtype=BF16, name=f"sb_q_{i}").reshape(bsz, s, -1)
    o3, tot = _sb_fwd(q3, kv3, kvt, name=f"sb_attn_{i}")
    o2 = o3.reshape(h.shape[0], -1)
    h1 = _mm(o2, w_o, add=h, name=f"sb_o_{i}")
    return h1, (h, hn, q3, o2, tot)


def _sb_layer_bwd(dh, saved, gain, w_q, w_o, kv3, kvt, dk, dv, i):
    h, hn, q3, o2, tot = saved
    t = h.shape[0]
    d_o = _mm(o2, dh, ta=True, name=f"sb_o_dw_{i}")
    do3 = _mm(dh, w_o, tb=True, out_dtype=BF16, name=f"sb_o_dx_{i}").reshape(q3.shape)
    dq3, dk, dv = _sb_bwd(q3, kv3, kvt, do3, tot, dk, dv, name=f"sb_attn_bwd_{i}")
    dq = dq3.reshape(t, -1)
    d_q = _mm(hn, dq, ta=True, name=f"sb_q_dw_{i}")
    dhn = _mm(dq, w_q, tb=True, name=f"sb_q_dx_{i}")
    dh, dgain = _rmsnorm_bwd(h, gain, dhn, dh, name=f"attn_norm_bwd_{i}")
    return dh, dgain, d_q, d_o, dk, dv


def kernel(x, p, attn_norm, ffn_norm, ple_norm, ssm_in_proj, ssm_conv_w, ssm_conv_b, ssm_dt_bias, ssm_a_log, ssm_d, ssm_norm, ssm_out_proj, kv_norm, w_kv, w_q, w_o, ffn_up, ffn_conv_w, ffn_conv_b, ffn_down, ple_gate, ple_proj, final_norm, loss_target, m_attn_norm, m_ffn_norm, m_ple_norm, m_ssm_in_proj, m_ssm_conv_w, m_ssm_conv_b, m_ssm_dt_bias, m_ssm_a_log, m_ssm_d, m_ssm_norm, m_ssm_out_proj, m_kv_norm, m_w_kv, m_w_q, m_w_o, m_ffn_up, m_ffn_conv_w, m_ffn_conv_b, m_ffn_down, m_ple_gate, m_ple_proj, m_final_norm, v_attn_norm, v_ffn_norm, v_ple_norm, v_ssm_in_proj, v_ssm_conv_w, v_ssm_conv_b, v_ssm_dt_bias, v_ssm_a_log, v_ssm_d, v_ssm_norm, v_ssm_out_proj, v_kv_norm, v_w_kv, v_w_q, v_w_o, v_ffn_up, v_ffn_conv_w, v_ffn_conv_b, v_ffn_down, v_ple_gate, v_ple_proj, v_final_norm):
    given = dict(locals())
    wl = {n: given[n] for n in WEIGHTS}
    bsz, s, d = x.shape
    t = bsz * s
    depth = attn_norm.shape[0]
    n_a = ssm_in_proj.shape[0]
    d_inner = ssm_norm.shape[1] * N_CHIPS
    g_n = d_inner // SSM_GROUP_W
    cidx = lax.axis_index("c").astype(I32).reshape(1)
    chip_idx = (2 * lax.axis_index("x") + lax.axis_index("y")).astype(I32).reshape(1)

    big = [n for n in SHARDED if _size(wl[n].shape) >= BIG_WEIGHT]
    small = [n for n in SHARDED if n not in big]
    small_shapes = [wl[n].shape for n in small]
    halves = lambda shp: shp if len(shp) == 3 else (2, shp[0] // 2, shp[1])
    small_slab = _pack_rows([wl[n] for n in small], BF16, unit=2 * SLAB_ROW_ALIGN)
    small_rows = small_slab.shape[0]
    mine = [wl[n].astype(BF16).reshape(halves(wl[n].shape)) for n in big]
    mine.append(small_slab.reshape(2, small_rows // 2, LANES))
    gathered = [lax.dynamic_update_index_in_dim(g, m_, chip_idx[0], 0)
                for g, m_ in zip(_gather_weights(mine), mine)]
    per_chip = {n: g.reshape((N_CHIPS,) + wl[n].shape) for n, g in zip(big, gathered)}
    per_chip.update(zip(small, _unpack_rows(gathered[-1].reshape(N_CHIPS, small_rows, LANES), small_shapes)))
    full = {}
    for n in SHARDED:
        ax, piece = SHARD_AXIS[n], per_chip[n]
        merged = piece.shape[1:ax + 1] + (N_CHIPS * piece.shape[ax + 1],) + piece.shape[ax + 2:]
        full[n] = jnp.moveaxis(piece, 0, ax).reshape(merged)

    h = x.reshape(t, d)
    tgt = loss_target.reshape(t, d)
    saved = []
    kv3 = kvt = hkv = h_kv_in = None
    consts = []
    for i in range(depth):
        if i < n_a:
            a_neg, plane, psub, d_ch = _ssm_consts(ssm_dt_bias[i], ssm_a_log[i], ssm_d[i], g_n)
            w_in_big = _ssm_in_big(full["ssm_in_proj"][i], d_inner, g_n)
            cw = full["ssm_conv_w"][i].astype(F32)
            cb = full["ssm_conv_b"][i].astype(F32)
            nw = full["ssm_norm"][i].astype(F32).reshape(1, -1)
            consts.append((a_neg, plane, psub, d_ch, w_in_big, cw, cb, nw))
            h, sv_mix = _ssm_fwd(h, bsz, s, attn_norm[i], w_in_big, cw, cb, plane, psub, d_ch, nw,
                                 full["ssm_out_proj"][i], i)
        else:
            j = i - n_a
            h, sv_mix = _sb_layer_fwd(h, bsz, s, attn_norm[i], full["w_q"][j], full["w_o"][j], kv3, kvt, i)
        fcw = full["ffn_conv_w"][i].astype(F32)
        h, sv_ffn = _ffn_fwd(h, bsz, s, ffn_norm[i], full["ffn_up"][i], fcw, ffn_conv_b[i],
                             full["ffn_down"][i], i)
        p_i = p[i].reshape(t, -1)
        h, sv_ple = _ple_layer_fwd(h, p_i, ple_norm[i], full["ple_gate"][i], full["ple_proj"][i], i)
        saved.append((sv_mix, sv_ffn, sv_ple))
        if i == n_a - 1:
            h_kv_in = h
            hkv = _rmsnorm_fwd(h, kv_norm, name="kv_norm")
            kv3 = _mm(hkv, full["w_kv"], out_dtype=BF16, name="kv_proj").reshape(bsz, s, -1)
            kvt = _kv_blocks_t(kv3)

    loss_local, dh, g_final = _final_loss(h, final_norm, tgt)
    gr = {n: [None] * wl[n].shape[0] for n in WEIGHTS if n not in ("kv_norm", "w_kv", "final_norm")}
    gr["final_norm"] = g_final
    dk = dv = None
    for i in reversed(range(depth)):
        sv_mix, sv_ffn, sv_ple = saved[i]
        if i == n_a - 1:
            dkv = jnp.concatenate([dk, dv], axis=-1).reshape(t, -1)
            gr["w_kv"] = _mm(hkv, dkv, ta=True, name="kv_proj_dw")
            dhkv = _mm(dkv, full["w_kv"], tb=True, name="kv_proj_dx")
            dh, gr["kv_norm"] = _rmsnorm_bwd(h_kv_in, kv_norm, dhkv, dh, name="kv_norm_bwd")
        p_i = p[i].reshape(t, -1)
        dh, gr["ple_norm"][i], gr["ple_gate"][i], gr["ple_proj"][i] = _ple_layer_bwd(
            dh, sv_ple, p_i, ple_norm[i], full["ple_gate"][i], i)
        fcw = full["ffn_conv_w"][i].astype(F32)
        (dh, gr["ffn_norm"][i], gr["ffn_up"][i], gr["ffn_conv_w"][i], gr["ffn_conv_b"][i],
         gr["ffn_down"][i]) = _ffn_bwd(dh, sv_ffn, ffn_norm[i], full["ffn_up"][i], fcw, ffn_conv_b[i],
                                       full["ffn_down"][i], i)
        if i < n_a:
            a_neg, plane, psub, d_ch, w_in_big, cw, cb, nw = consts[i]
            (dh, gr["attn_norm"][i], d_in_big, gr["ssm_conv_w"][i], gr["ssm_conv_b"][i],
             gr["ssm_dt_bias"][i], d_a, gr["ssm_d"][i], gr["ssm_norm"][i],
             gr["ssm_out_proj"][i]) = _ssm_bwd(dh, sv_mix, attn_norm[i], w_in_big, cw, cb, plane, psub,
                                               d_ch, nw, full["ssm_out_proj"][i], i)
            gr["ssm_in_proj"][i] = _ssm_in_small(d_in_big, g_n)
            gr["ssm_a_log"][i] = d_a * a_neg
        else:
            j = i - n_a
            dh, gr["attn_norm"][i], gr["w_q"][j], gr["w_o"][j], dk, dv = _sb_layer_bwd(
                dh, sv_mix, attn_norm[i], full["w_q"][j], full["w_o"][j], kv3, kvt, dk, dv, i)
    grad_x = dh.reshape(bsz, s, d)
    gfull = {n: (jnp.stack(v) if isinstance(v, list) else v) for n, v in gr.items()}

    by_chip = {}
    for n in SHARDED:
        ax, shp = SHARD_AXIS[n], gfull[n].shape
        split = gfull[n].reshape(shp[:ax] + (N_CHIPS, shp[ax] // N_CHIPS) + shp[ax + 1:])
        by_chip[n] = jnp.moveaxis(split, ax, 0)
    g4 = [by_chip[n].astype(BF16).reshape((N_CHIPS,) + halves(wl[n].shape)) for n in big]
    g4.append(_pack_rows([by_chip[n] for n in small], BF16, lead=1, unit=2 * SLAB_ROW_ALIGN)
              .reshape(N_CHIPS, 2, small_rows // 2, LANES))
    tags = big + ["small"]
    from_sibling = _swap_halves(g4)
    chip_sums = [_add_own_half(cidx, g, r_, name="grad_pair_sum_" + tg)
                 for g, r_, tg in zip(g4, from_sibling, tags)]
    from_chips = _scatter_chunks(chip_sums)
    my_half = [_add_chips(chip_idx, s_, r_, name="grad_chip_sum_" + tg)
               for s_, r_, tg in zip(chip_sums, from_chips, tags)]
    other_half = _share_half(my_half)
    low_core = cidx[0] == 0
    reduced = [jnp.concatenate([jnp.where(low_core, a, b_), jnp.where(low_core, b_, a)], axis=0)
               for a, b_ in zip(my_half, other_half)]
    grads = {n: g.reshape(wl[n].shape) for n, g in zip(big, reduced)}
    grads.update(zip(small, _unpack_rows(reduced[-1].reshape(small_rows, LANES), small_shapes)))

    rep_shapes = [wl[n].shape for n in REPLICATED]
    packed_r = _pack_rows([gfull[n] for n in REPLICATED], F32, unit=SLAB_ROW_ALIGN)
    g_rep = _sum_devices(_exchange_small(packed_r))

    delta, new_m, new_v = {}, {}, {}
    for n in SHARDED:
        delta[n], new_m[n], new_v[n] = _adamw(wl[n], grads[n], given["m_" + n], given["v_" + n],
                                              name="adamw_" + n)
    slabs = [_pack_rows([src[pre + n] for n in REPLICATED], F32, unit=SLAB_ROW_ALIGN)
             for src, pre in ((wl, ""), (given, "m_"), (given, "v_"))]
    rep_out = _adamw(slabs[0], g_rep, slabs[1], slabs[2], name="adamw_replicated")
    for dst, slab in zip((grads, delta, new_m, new_v), [g_rep] + list(rep_out)):
        dst.update(zip(REPLICATED, _unpack_rows(slab, rep_shapes)))
    loss = lax.psum(loss_local, ("x", "y", "c"))
    return (loss, grad_x, *[grads[n] for n in WEIGHTS], *[delta[n] for n in WEIGHTS],
            *[new_m[n] for n in WEIGHTS], *[new_v[n] for n in WEIGHTS])
```

```python
import functools

import jax
import jax.numpy as jnp
from jax import lax
from jax.experimental import pallas as pl
from jax.experimental.pallas import tpu as pltpu

F32 = jnp.float32
BF16 = jnp.bfloat16
I32 = jnp.int32

NORM_EPS = 1e-6
SSM_NORM_EPS = 1e-5
SSM_HEAD_DIM = 64
SSM_STATE = 128
SSM_CHUNK = 128
SSM_HEADS_PER_GROUP = 8
SSM_GROUP_W = SSM_HEADS_PER_GROUP * SSM_HEAD_DIM
SSM_CONV = 4
SSM_ROWS = 16
SB_HEAD_DIM = 64
SB_BLOCK = 128
SB_SCALE = SB_HEAD_DIM ** -0.5
SB_Q_BLOCKS_FWD = 4
SB_Q_BLOCKS_BWD = 4
FFN_CONV = 3
LANES = 128
N_CHIPS = 4
N_DEV = 8

ADAM_LR = 0.001
ADAM_B1 = 0.9
ADAM_B2 = 0.999
ADAM_EPS = 1e-08
ADAM_WD = 0.01
ADAM_STEP = 10

MESH = pl.DeviceIdType.MESH
ANY = pl.BlockSpec(memory_space=pl.ANY)

SHARD_AXIS = {
    "ssm_in_proj": 2, "ssm_conv_w": 2, "ssm_conv_b": 1, "ssm_norm": 1, "ssm_out_proj": 1,
    "w_kv": 1, "w_q": 1, "w_o": 1, "ffn_up": 2, "ffn_conv_w": 2, "ffn_down": 1,
    "ple_gate": 1, "ple_proj": 2,
}
REPLICATED = ["attn_norm", "ffn_norm", "ple_norm", "ssm_dt_bias", "ssm_a_log", "ssm_d",
              "kv_norm", "ffn_conv_b", "final_norm"]
WEIGHTS = ["attn_norm", "ffn_norm", "ple_norm", "ssm_in_proj", "ssm_conv_w", "ssm_conv_b",
           "ssm_dt_bias", "ssm_a_log", "ssm_d", "ssm_norm", "ssm_out_proj", "kv_norm", "w_kv",
           "w_q", "w_o", "ffn_up", "ffn_conv_w", "ffn_conv_b", "ffn_down", "ple_gate",
           "ple_proj", "final_norm"]
SHARDED = [n for n in WEIGHTS if n in SHARD_AXIS]
PACK_ROWS = 2048
SLAB_ROW_ALIGN = 16
BIG_WEIGHT = 1 << 17


def _tile(n, pref):
    t = (min(pref, n) // 128) * 128
    while t >= 128:
        if n % t == 0:
            return t
        t -= 128
    return n


def _dot(a, b):
    return jnp.dot(a, b, preferred_element_type=F32)


def _dot_nt(a, b):
    return lax.dot_general(a, b, (((1,), (1,)), ((), ())), preferred_element_type=F32)


def _dot_tn(a, b):
    return lax.dot_general(a, b, (((0,), (0,)), ((), ())), preferred_element_type=F32)


def _split2(x):
    hi = x.astype(BF16)
    lo = (x - hi.astype(F32)).astype(BF16)
    return hi, lo


def _dot2(x, m):
    hi, lo = _split2(x)
    return _dot(hi, m) + _dot(lo, m)


def _dot2_left(m, x):
    hi, lo = _split2(x)
    return _dot(m, hi) + _dot(m, lo)


def _softplus(x):
    return jnp.maximum(x, 0.0) + jnp.log(1.0 + jnp.exp(-jnp.abs(x)))


def _sigmoid(x):
    return 0.5 * jnp.tanh(0.5 * x) + 0.5


def _params(*sem):
    return pltpu.CompilerParams(dimension_semantics=sem)


MM_VMEM_BUDGET = 36 * 1024 * 1024
MM_FULL_K = 2816


def _mm_tiles(m, n, k, sa, sb, so, has_add, extra=0):
    tk = k if k <= MM_FULL_K else _tile(k, 1024)
    tn = _tile(n, 1408)
    tm = _tile(m, 1408)

    def need(tm_):
        return (2 * tm_ * tk * sa + 2 * tk * tn * sb + tm_ * tn * 4 + 2 * tm_ * tn * so
                + 2 * tm_ * tn * 4 * (extra + (1 if has_add else 0)))

    while need(tm) > MM_VMEM_BUDGET and tm % 256 == 0:
        tm //= 2
    return tm, tn, tk


def _mm(a, b, *, name, ta=False, tb=False, add=None, out_dtype=F32):
    m = a.shape[1] if ta else a.shape[0]
    k = a.shape[0] if ta else a.shape[1]
    n = b.shape[0] if tb else b.shape[1]
    assert (b.shape[1] if tb else b.shape[0]) == k, (a.shape, b.shape, ta, tb)
    tm, tn, tk = _mm_tiles(m, n, k, a.dtype.itemsize, b.dtype.itemsize,
                           jnp.dtype(out_dtype).itemsize, add is not None)
    nk = k // tk
    dims = (((0 if ta else 1,), (1 if tb else 0,)), ((), ()))
    has_add = add is not None

    def body(*refs):
        if has_add:
            a_ref, b_ref, add_ref, o_ref, acc_ref = refs
        else:
            a_ref, b_ref, o_ref, acc_ref = refs
        kk = pl.program_id(2)

        @pl.when(kk == 0)
        def _():
            acc_ref[...] = jnp.zeros_like(acc_ref)

        acc_ref[...] += lax.dot_general(a_ref[...].astype(BF16), b_ref[...].astype(BF16), dims,
                                        preferred_element_type=F32)

        @pl.when(kk == nk - 1)
        def _():
            r = acc_ref[...]
            if has_add:
                r = r + add_ref[...].astype(F32)
            o_ref[...] = r.astype(out_dtype)

    a_spec = (pl.BlockSpec((tk, tm), lambda i, j, kk: (kk, i)) if ta
              else pl.BlockSpec((tm, tk), lambda i, j, kk: (i, kk)))
    b_spec = (pl.BlockSpec((tn, tk), lambda i, j, kk: (j, kk)) if tb
              else pl.BlockSpec((tk, tn), lambda i, j, kk: (kk, j)))
    o_spec = pl.BlockSpec((tm, tn), lambda i, j, kk: (i, j))
    in_specs = [a_spec, b_spec] + ([o_spec] if has_add else [])
    args = (a, b) + ((add,) if has_add else ())
    return pl.pallas_call(
        body, name=name, grid=(m // tm, n // tn, nk), in_specs=in_specs, out_specs=o_spec,
        out_shape=jax.ShapeDtypeStruct((m, n), out_dtype),
        scratch_shapes=[pltpu.VMEM((tm, tn), F32)],
        compiler_params=_params("parallel", "parallel", "arbitrary"),
    )(*args)


def _mm_norm_bwd(a, b, x, gain, dres, *, name, add=None):
    m, k = a.shape
    n = b.shape[0]
    has_add = add is not None
    tm, tn, tk = _mm_tiles(m, n, k, a.dtype.itemsize, b.dtype.itemsize, 4, has_add, extra=2)
    assert tn == n, (tn, n)
    nk = k // tk

    def body(*refs):
        if has_add:
            a_ref, b_ref, add_ref, x_ref, g_ref, dres_ref, dx_ref, dg_ref, acc_ref = refs
        else:
            a_ref, b_ref, x_ref, g_ref, dres_ref, dx_ref, dg_ref, acc_ref = refs
        i, kk = pl.program_id(0), pl.program_id(1)

        @pl.when(kk == 0)
        def _():
            acc_ref[...] = jnp.zeros_like(acc_ref)

        acc_ref[...] += _dot_nt(a_ref[...].astype(BF16), b_ref[...].astype(BF16))

        @pl.when(kk == nk - 1)
        def _():
            dyv = acc_ref[...]
            if has_add:
                dyv = dyv + add_ref[...]
            xv = x_ref[...]
            r = lax.rsqrt(jnp.mean(xv * xv, axis=-1, keepdims=True) + NORM_EPS)
            xh = xv * r
            dxh = dyv * g_ref[...]
            dx_ref[...] = dres_ref[...] + r * (dxh - xh * jnp.mean(dxh * xh, axis=-1, keepdims=True))
            part = jnp.sum(dyv * xh, axis=0, keepdims=True)

            @pl.when(i == 0)
            def _():
                dg_ref[...] = part

            @pl.when(i > 0)
            def _():
                dg_ref[...] += part

    row = pl.BlockSpec((tm, n), lambda i, kk: (i, 0))
    vec = pl.BlockSpec((1, n), lambda i, kk: (0, 0))
    in_specs = ([pl.BlockSpec((tm, tk), lambda i, kk: (i, kk)), pl.BlockSpec((n, tk), lambda i, kk: (0, kk))]
                + ([row] if has_add else []) + [row, vec, row])
    args = (a, b) + ((add,) if has_add else ()) + (x, gain.reshape(1, n), dres)
    dx, dg = pl.pallas_call(
        body, name=name, grid=(m // tm, nk), in_specs=in_specs, out_specs=[row, vec],
        out_shape=[jax.ShapeDtypeStruct((m, n), F32), jax.ShapeDtypeStruct((1, n), F32)],
        scratch_shapes=[pltpu.VMEM((tm, n), F32)],
        compiler_params=_params("arbitrary", "arbitrary"),
    )(*args)
    return dx, dg.reshape(n)


def _rmsnorm_fwd(x, gain, *, name, rows=512):
    t, d = x.shape
    tr = _tile(t, rows)

    def body(x_ref, g_ref, o_ref):
        xv = x_ref[...]
        r = lax.rsqrt(jnp.mean(xv * xv, axis=-1, keepdims=True) + NORM_EPS)
        o_ref[...] = ((xv * r) * g_ref[...]).astype(BF16)

    return pl.pallas_call(
        body, name=name, grid=(t // tr,),
        in_specs=[pl.BlockSpec((tr, d), lambda i: (i, 0)), pl.BlockSpec((1, d), lambda i: (0, 0))],
        out_specs=pl.BlockSpec((tr, d), lambda i: (i, 0)),
        out_shape=jax.ShapeDtypeStruct((t, d), BF16),
        compiler_params=_params("parallel"),
    )(x, gain.reshape(1, d))


def _rmsnorm_bwd(x, gain, dy, dres, *, name, rows=512):
    t, d = x.shape
    tr = _tile(t, rows)

    def body(x_ref, g_ref, dy_ref, dres_ref, dx_ref, dg_ref):
        xv = x_ref[...]
        r = lax.rsqrt(jnp.mean(xv * xv, axis=-1, keepdims=True) + NORM_EPS)
        xh = xv * r
        dyv = dy_ref[...].astype(F32)
        dxh = dyv * g_ref[...]
        dx = r * (dxh - xh * jnp.mean(dxh * xh, axis=-1, keepdims=True))
        dx_ref[...] = dres_ref[...] + dx
        part = jnp.sum(dyv * xh, axis=0, keepdims=True)

        @pl.when(pl.program_id(0) == 0)
        def _():
            dg_ref[...] = part

        @pl.when(pl.program_id(0) > 0)
        def _():
            dg_ref[...] += part

    row = pl.BlockSpec((tr, d), lambda i: (i, 0))
    vec = pl.BlockSpec((1, d), lambda i: (0, 0))
    dx, dg = pl.pallas_call(
        body, name=name, grid=(t // tr,), in_specs=[row, vec, row, row], out_specs=[row, vec],
        out_shape=[jax.ShapeDtypeStruct((t, d), F32), jax.ShapeDtypeStruct((1, d), F32)],
        compiler_params=_params("arbitrary"),
    )(x, gain.reshape(1, d), dy, dres)
    return dx, dg.reshape(d)


def _final_loss(h, gain, target, *, rows=512):
    t, d = h.shape
    tr = _tile(t, rows)

    def body(x_ref, g_ref, tg_ref, dx_ref, dg_ref, loss_ref):
        xv = x_ref[...]
        g = g_ref[...]
        r = lax.rsqrt(jnp.mean(xv * xv, axis=-1, keepdims=True) + NORM_EPS)
        xh = xv * r
        err = xh * g - tg_ref[...]
        dyv = err * (1.0 / d)
        dxh = dyv * g
        dx_ref[...] = r * (dxh - xh * jnp.mean(dxh * xh, axis=-1, keepdims=True))
        part = jnp.sum(dyv * xh, axis=0, keepdims=True)
        lpart = jnp.zeros((1, LANES), F32) + (0.5 / d) * jnp.sum(err * err)

        @pl.when(pl.program_id(0) == 0)
        def _():
            dg_ref[...] = part
            loss_ref[...] = lpart

        @pl.when(pl.program_id(0) > 0)
        def _():
            dg_ref[...] += part
            loss_ref[...] += lpart

    row = pl.BlockSpec((tr, d), lambda i: (i, 0))
    vec = pl.BlockSpec((1, d), lambda i: (0, 0))
    dx, dg, loss = pl.pallas_call(
        body, name="final_loss", grid=(t // tr,), in_specs=[row, vec, row],
        out_specs=[row, vec, pl.BlockSpec((1, LANES), lambda i: (0, 0))],
        out_shape=[jax.ShapeDtypeStruct((t, d), F32), jax.ShapeDtypeStruct((1, d), F32),
                   jax.ShapeDtypeStruct((1, LANES), F32)],
        compiler_params=_params("arbitrary"),
    )(h, gain.reshape(1, d), target)
    return loss[0, 0], dx, dg.reshape(d)


def _ple_fwd(h, a, pp, *, name, rows=512):
    t, d = h.shape
    tr = _tile(t, rows)

    def body(h_ref, a_ref, p_ref, o_ref):
        o_ref[...] = h_ref[...] + _sigmoid(a_ref[...]) * p_ref[...]

    row = pl.BlockSpec((tr, d), lambda i: (i, 0))
    return pl.pallas_call(
        body, name=name, grid=(t // tr,), in_specs=[row, row, row], out_specs=row,
        out_shape=jax.ShapeDtypeStruct((t, d), F32), compiler_params=_params("parallel"),
    )(h, a, pp)


def _ple_bwd(dh, a, pp, *, name, rows=512):
    t, d = dh.shape
    tr = _tile(t, rows)

    def body(dh_ref, a_ref, p_ref, da_ref, dp_ref):
        s = _sigmoid(a_ref[...])
        dhv = dh_ref[...]
        da_ref[...] = (dhv * p_ref[...] * (s * (1.0 - s))).astype(BF16)
        dp_ref[...] = (dhv * s).astype(BF16)

    row = pl.BlockSpec((tr, d), lambda i: (i, 0))
    return pl.pallas_call(
        body, name=name, grid=(t // tr,), in_specs=[row, row, row], out_specs=[row, row],
        out_shape=[jax.ShapeDtypeStruct((t, d), BF16)] * 2, compiler_params=_params("parallel"),
    )(dh, a, pp)


CONV_ROWS = 64
CONV_HALO = 8


def _conv_window(ref, r0, with_prev, with_next):
    s = ref.shape[1]
    parts = []
    if with_prev:
        prev = ref[0, pl.ds(pl.multiple_of(jnp.maximum(r0 - CONV_HALO, 0), CONV_HALO), CONV_HALO), :]
        parts.append(jnp.where(r0 > 0, prev, 0.0))
    parts.append(ref[0, pl.ds(r0, CONV_ROWS), :])
    if with_next:
        nxt = pl.multiple_of(jnp.minimum(r0 + CONV_ROWS, s - CONV_HALO), CONV_HALO)
        parts.append(ref[0, pl.ds(nxt, CONV_HALO), :])
    return jnp.concatenate(parts, axis=0)


def _conv_taps(win, kw, n):
    return [win[CONV_HALO - (kw - 1 - k):CONV_HALO - (kw - 1 - k) + n] for k in range(kw)]


def _conv_apply(taps, wv, bv):
    pre = bv + wv[0:1, :] * taps[0]
    for k in range(1, len(taps)):
        pre = pre + wv[k:k + 1, :] * taps[k]
    return pre


def _rows8(x):
    acc = x[0:8]
    for i in range(1, x.shape[0] // 8):
        acc = acc + x[8 * i:8 * i + 8]
    return acc


def _conv_grad_step(dpre_ext, taps, wv, is_last):
    kw = wv.shape[0]
    halo = jnp.where(is_last, 0.0, dpre_ext[CONV_ROWS:])
    dpre_ext = jnp.concatenate([dpre_ext[:CONV_ROWS], halo], axis=0)
    dpre = dpre_ext[:CONV_ROWS]
    du = wv[kw - 1:kw, :] * dpre
    for k in range(kw - 1):
        du = du + wv[k:k + 1, :] * dpre_ext[kw - 1 - k:kw - 1 - k + CONV_ROWS]
    sums = [_rows8(dpre * taps[k][:CONV_ROWS]) for k in range(kw)] + [_rows8(dpre)]
    return du, sums


def _conv_store_sums(sums, dw_ref, db_ref, first):
    kw = len(sums) - 1
    vals = [jnp.sum(s_, axis=0, keepdims=True) for s_ in sums]

    @pl.when(first)
    def _():
        for k in range(kw):
            dw_ref[k:k + 1, :] = vals[k]
        db_ref[...] = vals[kw]

    @pl.when(jnp.logical_not(first))
    def _():
        for k in range(kw):
            dw_ref[k:k + 1, :] += vals[k]
        db_ref[...] += vals[kw]


def _dsilu(pre):
    s = _sigmoid(pre)
    return s, s * (1.0 + pre * (1.0 - s))


def _conv_silu_fwd(zx, off, w, b, *, name, tc=128):
    bsz, s, _ = zx.shape
    kw, c = w.shape
    o0 = off // tc

    def body(u_ref, w_ref, b_ref, o_ref):
        wv, bv = w_ref[...], b_ref[...]

        def step(i, carry):
            r0 = pl.multiple_of(i * CONV_ROWS, CONV_ROWS)
            taps = _conv_taps(_conv_window(u_ref, r0, True, False), kw, CONV_ROWS)
            pre = _conv_apply(taps, wv, bv)
            o_ref[0, pl.ds(r0, CONV_ROWS), :] = pre * _sigmoid(pre)
            return carry

        lax.fori_loop(0, s // CONV_ROWS, step, 0)

    return pl.pallas_call(
        body, name=name, grid=(bsz, c // tc),
        in_specs=[pl.BlockSpec((1, s, tc), lambda i, j: (i, 0, o0 + j)),
                  pl.BlockSpec((kw, tc), lambda i, j: (0, j)),
                  pl.BlockSpec((1, tc), lambda i, j: (0, j))],
        out_specs=pl.BlockSpec((1, s, tc), lambda i, j: (i, 0, j)),
        out_shape=jax.ShapeDtypeStruct((bsz, s, c), F32),
        compiler_params=_params("parallel", "parallel"),
    )(zx, w, b.reshape(1, c))


def _conv_silu_bwd(zx, off, w, b, douts, *, name, tc=128):
    bsz, s, _ = zx.shape
    kw, c = w.shape
    o0 = off // tc
    counts = [d.shape[2] // tc for d in douts]
    starts = [sum(counts[:k]) for k in range(len(douts))]
    assert sum(counts) == c // tc

    def body(u_ref, w_ref, b_ref, *rest):
        dy_refs = rest[:len(douts)]
        du_ref, dw_ref, db_ref = rest[len(douts):]
        j = pl.program_id(0)
        wv, bv = w_ref[...], b_ref[...]
        n = CONV_ROWS + CONV_HALO

        def step(i, sums):
            r0 = pl.multiple_of(i * CONV_ROWS, CONV_ROWS)
            taps = _conv_taps(_conv_window(u_ref, r0, True, True), kw, n)
            _, ds = _dsilu(_conv_apply(taps, wv, bv))
            dy = _conv_window(dy_refs[0], r0, False, True)
            for k in range(1, len(douts)):
                dy = jnp.where(j >= starts[k], _conv_window(dy_refs[k], r0, False, True), dy)
            du, new = _conv_grad_step(dy * ds, taps, wv, r0 + CONV_ROWS >= s)
            du_ref[0, pl.ds(r0, CONV_ROWS), :] = du.astype(BF16)
            return tuple(a + b_ for a, b_ in zip(sums, new))

        zero = tuple(jnp.zeros((8, tc), F32) for _ in range(kw + 1))
        sums = lax.fori_loop(0, s // CONV_ROWS, step, zero)
        _conv_store_sums(sums, dw_ref, db_ref, pl.program_id(1) == 0)

    def part_spec(k):
        return pl.BlockSpec((1, s, tc), lambda j, i: (i, 0, jnp.clip(j - starts[k], 0, counts[k] - 1)))

    du, dw, db = pl.pallas_call(
        body, name=name, grid=(c // tc, bsz),
        in_specs=[pl.BlockSpec((1, s, tc), lambda j, i: (i, 0, o0 + j)),
                  pl.BlockSpec((kw, tc), lambda j, i: (0, j)),
                  pl.BlockSpec((1, tc), lambda j, i: (0, j))] + [part_spec(k) for k in range(len(douts))],
        out_specs=[pl.BlockSpec((1, s, tc), lambda j, i: (i, 0, j)),
                   pl.BlockSpec((kw, tc), lambda j, i: (0, j)),
                   pl.BlockSpec((1, tc), lambda j, i: (0, j))],
        out_shape=[jax.ShapeDtypeStruct((bsz, s, c), BF16), jax.ShapeDtypeStruct((kw, c), F32),
                   jax.ShapeDtypeStruct((1, c), F32)],
        compiler_params=_params("parallel", "arbitrary"),
    )(zx, w, b.reshape(1, c), *douts)
    return du, dw, db.reshape(c)


def _conv_glu_fwd(up, w, b, *, name, tc=128):
    bsz, s, c2 = up.shape
    kw = w.shape[0]
    f = c2 // 2
    nt = f // tc

    def body(ug_ref, uv_ref, wg_ref, wv_ref, bg_ref, bv_ref, o_ref):
        wg, wv, bg, bv = wg_ref[...], wv_ref[...], bg_ref[...], bv_ref[...]

        def step(i, carry):
            r0 = pl.multiple_of(i * CONV_ROWS, CONV_ROWS)
            pg = _conv_apply(_conv_taps(_conv_window(ug_ref, r0, True, False), kw, CONV_ROWS), wg, bg)
            pv = _conv_apply(_conv_taps(_conv_window(uv_ref, r0, True, False), kw, CONV_ROWS), wv, bv)
            o_ref[0, pl.ds(r0, CONV_ROWS), :] = (pg * _sigmoid(pg) * pv).astype(BF16)
            return carry

        lax.fori_loop(0, s // CONV_ROWS, step, 0)

    b2 = b.reshape(1, c2)
    return pl.pallas_call(
        body, name=name, grid=(bsz, nt),
        in_specs=[pl.BlockSpec((1, s, tc), lambda i, j: (i, 0, j)),
                  pl.BlockSpec((1, s, tc), lambda i, j: (i, 0, nt + j)),
                  pl.BlockSpec((kw, tc), lambda i, j: (0, j)),
                  pl.BlockSpec((kw, tc), lambda i, j: (0, nt + j)),
                  pl.BlockSpec((1, tc), lambda i, j: (0, j)),
                  pl.BlockSpec((1, tc), lambda i, j: (0, nt + j))],
        out_specs=pl.BlockSpec((1, s, tc), lambda i, j: (i, 0, j)),
        out_shape=jax.ShapeDtypeStruct((bsz, s, f), BF16),
        compiler_params=_params("parallel", "parallel"),
    )(up, up, w, w, b2, b2)


def _conv_glu_bwd(up, w, b, df, *, name, tc=128):
    bsz, s, c2 = up.shape
    kw = w.shape[0]
    f = c2 // 2
    nt = f // tc

    def body(ug_ref, uv_ref, wg_ref, wv_ref, bg_ref, bv_ref, df_ref,
             dug_ref, duv_ref, dwg_ref, dwv_ref, dbg_ref, dbv_ref):
        first = pl.program_id(1) == 0
        wg, wv, bg, bv = wg_ref[...], wv_ref[...], bg_ref[...], bv_ref[...]
        n = CONV_ROWS + CONV_HALO

        def step(i, sums):
            r0 = pl.multiple_of(i * CONV_ROWS, CONV_ROWS)
            is_last = r0 + CONV_ROWS >= s
            tg = _conv_taps(_conv_window(ug_ref, r0, True, True), kw, n)
            tv = _conv_taps(_conv_window(uv_ref, r0, True, True), kw, n)
            pg = _conv_apply(tg, wg, bg)
            pv = _conv_apply(tv, wv, bv)
            sig, dsl = _dsilu(pg)
            dfv = _conv_window(df_ref, r0, False, True)
            dug, new_g = _conv_grad_step(dfv * pv * dsl, tg, wg, is_last)
            duv, new_v = _conv_grad_step(dfv * (pg * sig), tv, wv, is_last)
            dug_ref[0, pl.ds(r0, CONV_ROWS), :] = dug.astype(BF16)
            duv_ref[0, pl.ds(r0, CONV_ROWS), :] = duv.astype(BF16)
            return tuple(a + b_ for a, b_ in zip(sums, new_g + new_v))

        zero = tuple(jnp.zeros((8, tc), F32) for _ in range(2 * (kw + 1)))
        sums = lax.fori_loop(0, s // CONV_ROWS, step, zero)
        _conv_store_sums(sums[:kw + 1], dwg_ref, dbg_ref, first)
        _conv_store_sums(sums[kw + 1:], dwv_ref, dbv_ref, first)

    b2 = b.reshape(1, c2)
    act = lambda j, i: (i, 0, j)
    wsp = pl.BlockSpec((kw, tc), lambda j, i: (0, j))
    bsp = pl.BlockSpec((1, tc), lambda j, i: (0, j))
    dug, duv, dwg, dwv, dbg, dbv = pl.pallas_call(
        body, name=name, grid=(nt, bsz),
        in_specs=[pl.BlockSpec((1, s, tc), act),
                  pl.BlockSpec((1, s, tc), lambda j, i: (i, 0, nt + j)),
                  wsp, pl.BlockSpec((kw, tc), lambda j, i: (0, nt + j)),
                  bsp, pl.BlockSpec((1, tc), lambda j, i: (0, nt + j)),
                  pl.BlockSpec((1, s, tc), act)],
        out_specs=[pl.BlockSpec((1, s, tc), act), pl.BlockSpec((1, s, tc), act), wsp, wsp, bsp, bsp],
        out_shape=[jax.ShapeDtypeStruct((bsz, s, f), BF16)] * 2
        + [jax.ShapeDtypeStruct((kw, f), F32)] * 2 + [jax.ShapeDtypeStruct((1, f), F32)] * 2,
        compiler_params=_params("parallel", "arbitrary"),
    )(up, up, w, w, b2, b2, df)
    return (dug, duv, jnp.concatenate([dwg, dwv], axis=1),
            jnp.concatenate([dbg.reshape(f), dbv.reshape(f)]))


def _ssd_shared(xs, bm, cm, dtc_raw, dtr_raw, plane, psub, st):
    cl = SSM_CHUNK
    bias_l, a_l = plane[0:1, :], plane[1:2, :]
    bias_s, a_s = psub[:, 0:1], psub[:, 1:2]
    ri = lax.broadcasted_iota(I32, (cl, cl), 0)
    ci = lax.broadcasted_iota(I32, (cl, cl), 1)
    tril = ri >= ci
    low_incl = tril.astype(BF16)
    up_incl = (ri <= ci).astype(BF16)
    seg_t = (lax.broadcasted_iota(I32, (LANES, SSM_GROUP_W), 0)
             == lax.broadcasted_iota(I32, (LANES, SSM_GROUP_W), 1) // SSM_HEAD_DIM).astype(BF16)
    dt_c = _softplus(dtc_raw + bias_l)
    cs_c = _dot2_left(low_incl, dt_c * a_l)
    dt_r = _softplus(dtr_raw + bias_s)
    cs_r = _dot2(dt_r * a_s, up_incl)
    dt_ch = _dot2(dt_c, seg_t)
    cs_ch = _dot2(cs_c, seg_t)
    cs_last = cs_ch[cl - 1:cl, :]
    decay_ch = jnp.exp(cs_ch)
    w_ch = jnp.exp(cs_last - cs_ch)
    tot_ch = jnp.exp(cs_last)
    xdt = xs * dt_ch
    bm_b, cm_b = bm.astype(BF16), cm.astype(BF16)
    gmat = _dot_nt(cm_b, bm_b)
    cst = _dot(cm_b, st.astype(BF16))
    yoff = decay_ch * cst
    return dict(tril=tril, low_incl=low_incl, up_incl=up_incl, seg_t=seg_t, a_l=a_l, bias_l=bias_l,
                dt_c=dt_c, cs_c=cs_c, cs_r=cs_r, dt_ch=dt_ch, decay_ch=decay_ch, w_ch=w_ch,
                tot_ch=tot_ch, xdt=xdt, bm_b=bm_b, cm_b=cm_b, gmat=gmat, yoff=yoff)


def _head_decay(q, r):
    diff = q["cs_c"][:, r:r + 1] - q["cs_r"][r:r + 1, :]
    return jnp.where(q["tril"], jnp.exp(jnp.minimum(diff, 0.0)), 0.0)


def _half_mask(hh):
    lane = lax.broadcasted_iota(I32, (SSM_CHUNK, LANES), 1)
    return (lane < SSM_HEAD_DIM) if hh == 0 else (lane >= SSM_HEAD_DIM)


def _ssd_ydiag(q):
    pairs = []
    for pr in range(SSM_HEADS_PER_GROUP // 2):
        xp = q["xdt"][:, pr * LANES:(pr + 1) * LANES]
        acc = None
        for hh in range(2):
            mm_ = (q["gmat"] * _head_decay(q, 2 * pr + hh)).astype(BF16)
            part = _dot(mm_, jnp.where(_half_mask(hh), xp, 0.0).astype(BF16))
            acc = part if acc is None else acc + part
        pairs.append(acc)
    return jnp.concatenate(pairs, axis=1)


def _ssd_specs(bsz, s, g_n, d_inner, rev):
    cl = SSM_CHUNK
    nc = s // cl
    cc = (lambda c: nc - 1 - c) if rev else (lambda c: c)
    gb = d_inner // LANES
    dt0 = (d_inner + d_inner + 2 * g_n * SSM_STATE) // LANES
    return dict(
        z=pl.BlockSpec((1, cl, SSM_GROUP_W), lambda b, g, c: (b, cc(c), g)),
        dtc=pl.BlockSpec((1, cl, LANES), lambda b, g, c: (b, cc(c), dt0 + g)),
        xs=pl.BlockSpec((1, cl, SSM_GROUP_W), lambda b, g, c: (b, cc(c), g)),
        bm=pl.BlockSpec((1, cl, LANES), lambda b, g, c: (b, cc(c), gb + g)),
        cm=pl.BlockSpec((1, cl, LANES), lambda b, g, c: (b, cc(c), gb + g_n + g)),
        dtr=pl.BlockSpec((1, 1, SSM_ROWS, cl), lambda b, g, c: (b, g, 0, cc(c))),
        plane=pl.BlockSpec((1, 8, LANES), lambda b, g, c: (g, 0, 0)),
        psub=pl.BlockSpec((1, SSM_ROWS, LANES), lambda b, g, c: (g, 0, 0)),
        chan=pl.BlockSpec((1, SSM_GROUP_W), lambda b, g, c: (0, g)),
        state=pl.BlockSpec((1, 1, 1, SSM_STATE, SSM_GROUP_W), lambda b, g, c: (b, g, cc(c), 0, 0)),
        bgrp=pl.BlockSpec((1, cl, LANES), lambda b, g, c: (b, cc(c), g)),
    )


def _ssd_fwd(zx, xbc, dtr_row, plane, psub, d_ch, nw, *, name):
    bsz, s, _ = zx.shape
    d_inner = d_ch.shape[1]
    g_n = d_inner // SSM_GROUP_W
    nc = s // SSM_CHUNK
    sp = _ssd_specs(bsz, s, g_n, d_inner, False)

    def body(z_ref, dtc_ref, xs_ref, bm_ref, cm_ref, dtr_ref, plane_ref, psub_ref, d_ref, nw_ref,
             gn_ref, st_out_ref, st_ref):
        @pl.when(pl.program_id(2) == 0)
        def _():
            st_ref[...] = jnp.zeros_like(st_ref)

        xs = xs_ref[0]
        st = st_ref[...]
        st_out_ref[0, 0, 0] = st
        q = _ssd_shared(xs, bm_ref[0], cm_ref[0], dtc_ref[0], dtr_ref[0, 0], plane_ref[0],
                        psub_ref[0], st)
        y = _ssd_ydiag(q) + q["yoff"] + xs * d_ref[...]
        st_ref[...] = q["tot_ch"] * st + _dot_tn(q["bm_b"], (q["w_ch"] * q["xdt"]).astype(BF16))
        zv = z_ref[0]
        gy = y * (zv * _sigmoid(zv))
        rstd = lax.rsqrt(jnp.mean(gy * gy, axis=-1, keepdims=True) + SSM_NORM_EPS)
        gn_ref[0] = ((gy * rstd) * nw_ref[...]).astype(BF16)

    return pl.pallas_call(
        body, name=name, grid=(bsz, g_n, nc),
        in_specs=[sp["z"], sp["dtc"], sp["xs"], sp["bm"], sp["cm"], sp["dtr"], sp["plane"],
                  sp["psub"], sp["chan"], sp["chan"]],
        out_specs=[sp["z"], sp["state"]],
        out_shape=[jax.ShapeDtypeStruct((bsz, s, d_inner), BF16),
                   jax.ShapeDtypeStruct((bsz, g_n, nc, SSM_STATE, SSM_GROUP_W), F32)],
        scratch_shapes=[pltpu.VMEM((SSM_STATE, SSM_GROUP_W), F32)],
        compiler_params=_params("parallel", "parallel", "arbitrary"),
    )(zx, zx, xbc, xbc, xbc, dtr_row, plane, psub, d_ch, nw)


def _ssd_bwd(zx, xbc, dtr_row, plane, psub, d_ch, nw, states, dgn, *, name):
    bsz, s, _ = zx.shape
    d_inner = d_ch.shape[1]
    g_n = d_inner // SSM_GROUP_W
    cl = SSM_CHUNK
    nc = s // cl
    sp = _ssd_specs(bsz, s, g_n, d_inner, True)
    acc_ch = pl.BlockSpec((1, 1, 8, SSM_GROUP_W), lambda b, g, c: (b, g, 0, 0))
    acc_ln = pl.BlockSpec((1, 1, 8, LANES), lambda b, g, c: (b, g, 0, 0))

    def body(z_ref, dtc_ref, xs_ref, bm_ref, cm_ref, dtr_ref, plane_ref, psub_ref, d_ref, nw_ref,
             st_in_ref, dgn_ref,
             dxs_ref, dbm_ref, dcm_ref, dz_ref, ddt_ref, ach_ref, aln_ref, dst_ref):
        first = pl.program_id(2) == 0

        @pl.when(first)
        def _():
            dst_ref[...] = jnp.zeros_like(dst_ref)
            ach_ref[...] = jnp.zeros_like(ach_ref)
            aln_ref[...] = jnp.zeros_like(aln_ref)

        xs = xs_ref[0]
        st = st_in_ref[0, 0, 0]
        q = _ssd_shared(xs, bm_ref[0], cm_ref[0], dtc_ref[0], dtr_ref[0, 0], plane_ref[0],
                        psub_ref[0], st)
        d_chv = d_ref[...]
        nwv = nw_ref[...]
        y = _ssd_ydiag(q) + q["yoff"] + xs * d_chv
        zv = z_ref[0]
        sz = _sigmoid(zv)
        silu_z = zv * sz
        gy = y * silu_z
        rstd = lax.rsqrt(jnp.mean(gy * gy, axis=-1, keepdims=True) + SSM_NORM_EPS)
        gyh = gy * rstd
        dgnv = dgn_ref[0]
        dgyh = dgnv * nwv
        dgy = rstd * (dgyh - gyh * jnp.mean(dgyh * gyh, axis=-1, keepdims=True))
        dy = dgy * silu_z
        dz_ref[0] = (dgy * y * (sz * (1.0 + zv * (1.0 - sz)))).astype(BF16)
        ach_ref[0, 0, 0:1, :] += jnp.sum(dgnv * gyh, axis=0, keepdims=True)
        ach_ref[0, 0, 1:2, :] += jnp.sum(dy * xs, axis=0, keepdims=True)
        st_b = st.astype(BF16)
        dyd = (dy * q["decay_ch"]).astype(BF16)
        dcm = _dot_nt(dyd, st_b)
        dstn = dst_ref[...]
        dstn_b = dstn.astype(BF16)
        bds = _dot(q["bm_b"], dstn_b)
        wx = q["w_ch"] * q["xdt"]
        dbm = _dot_nt(wx.astype(BF16), dstn_b)
        dst_ref[...] = q["tot_ch"] * dstn + _dot_tn(q["cm_b"], dyd)
        vterm = wx * bds
        cs_terms = dy * q["yoff"] - vterm
        last_ch = q["tot_ch"] * jnp.sum(dstn * st, axis=0, keepdims=True) + jnp.sum(vterm, axis=0, keepdims=True)
        lane = lax.broadcasted_iota(I32, (cl, LANES), 1)
        rowi = lax.broadcasted_iota(I32, (SSM_ROWS, cl), 0)
        dg_sum = jnp.zeros((cl, cl), F32)
        dcs_col = jnp.zeros((cl, LANES), F32)
        dcs_row = jnp.zeros((SSM_ROWS, cl), F32)
        dxdt_pairs = []
        for pr in range(SSM_HEADS_PER_GROUP // 2):
            xp_b = q["xdt"][:, pr * LANES:(pr + 1) * LANES].astype(BF16)
            dyp = dy[:, pr * LANES:(pr + 1) * LANES]
            acc = None
            for hh in range(2):
                r = 2 * pr + hh
                dm = _head_decay(q, r)
                mmat = q["gmat"] * dm
                dym = jnp.where(_half_mask(hh), dyp, 0.0).astype(BF16)
                dmat = jnp.where(q["tril"], _dot_nt(dym, xp_b), 0.0)
                part = _dot_tn(mmat.astype(BF16), dym)
                acc = part if acc is None else acc + part
                dg_sum = dg_sum + dmat * dm
                e = dmat * mmat
                dcs_col = dcs_col + jnp.where(lane == r, jnp.sum(e, axis=1, keepdims=True), 0.0)
                dcs_row = dcs_row + jnp.where(rowi == r, jnp.sum(e, axis=0, keepdims=True), 0.0)
            dxdt_pairs.append(acc)
        dg_b = dg_sum.astype(BF16)
        dcm_ref[0] = dcm + _dot(dg_b, q["bm_b"])
        dbm_ref[0] = dbm + _dot_tn(dg_b, q["cm_b"])
        dxdt = q["w_ch"] * bds + jnp.concatenate(dxdt_pairs, axis=1)
        dxs_ref[0] = dy * d_chv + dxdt * q["dt_ch"]
        seg = (lax.broadcasted_iota(I32, (SSM_GROUP_W, LANES), 0) // SSM_HEAD_DIM
               == lax.broadcasted_iota(I32, (SSM_GROUP_W, LANES), 1)).astype(BF16)
        row_as_col = jnp.transpose(jnp.concatenate(
            [dcs_row, jnp.zeros((cl - SSM_ROWS, cl), F32)], axis=0))
        dcs = dcs_col - row_as_col + _dot2(cs_terms, seg)
        last = _dot2(jnp.zeros((8, SSM_GROUP_W), F32) + last_ch, seg)[0:1, :]
        da = _dot2_left(q["up_incl"], dcs) + last
        ddt = _dot2(dxdt * xs, seg) + da * q["a_l"]
        ddtr = ddt * _sigmoid(dtc_ref[0] + q["bias_l"])
        ddt_ref[0] = ddtr.astype(BF16)
        aln_ref[0, 0, 0:1, :] += jnp.sum(ddtr, axis=0, keepdims=True)
        aln_ref[0, 0, 1:2, :] += jnp.sum(da * q["dt_c"], axis=0, keepdims=True)

    outs = pl.pallas_call(
        body, name=name, grid=(bsz, g_n, nc),
        in_specs=[sp["z"], sp["dtc"], sp["xs"], sp["bm"], sp["cm"], sp["dtr"], sp["plane"],
                  sp["psub"], sp["chan"], sp["chan"], sp["state"], sp["z"]],
        out_specs=[sp["z"], sp["bgrp"], sp["bgrp"], sp["z"], sp["bgrp"], acc_ch, acc_ln],
        out_shape=[jax.ShapeDtypeStruct((bsz, s, d_inner), F32),
                   jax.ShapeDtypeStruct((bsz, s, g_n * SSM_STATE), F32),
                   jax.ShapeDtypeStruct((bsz, s, g_n * SSM_STATE), F32),
                   jax.ShapeDtypeStruct((bsz, s, d_inner), BF16),
                   jax.ShapeDtypeStruct((bsz, s, g_n * LANES), BF16),
                   jax.ShapeDtypeStruct((bsz, g_n, 8, SSM_GROUP_W), F32),
                   jax.ShapeDtypeStruct((bsz, g_n, 8, LANES), F32)],
        scratch_shapes=[pltpu.VMEM((SSM_STATE, SSM_GROUP_W), F32)],
        compiler_params=_params("parallel", "parallel", "arbitrary"),
    )(zx, zx, xbc, xbc, xbc, dtr_row, plane, psub, d_ch, nw, states, dgn)
    return outs


def _sb_stack(x):
    out = []
    for i in range(x.shape[0] // SB_BLOCK):
        xb = x[i * SB_BLOCK:(i + 1) * SB_BLOCK]
        lane = lax.broadcasted_iota(I32, xb.shape, 1)
        zero = jnp.zeros_like(xb)
        out += [jnp.where(lane < SB_HEAD_DIM, xb, zero), jnp.where(lane >= SB_HEAD_DIM, xb, zero)]
    return jnp.concatenate(out, axis=0)


def _sb_unstack_t(acc_t):
    row = lax.broadcasted_iota(I32, (LANES, SB_BLOCK), 0)
    out = []
    for i in range(acc_t.shape[1] // (2 * SB_BLOCK)):
        a = acc_t[:, 2 * i * SB_BLOCK:(2 * i + 1) * SB_BLOCK]
        b = acc_t[:, (2 * i + 1) * SB_BLOCK:(2 * i + 2) * SB_BLOCK]
        out.append(jnp.transpose(jnp.where(row < SB_HEAD_DIM, a, b)))
    return jnp.concatenate(out, axis=0)


def _sb_tile_blocks(nq, q_blocks):
    nb = 4 if nq % 4 == 0 else (2 if nq % 2 == 0 else 1)
    return nb, min(nb, q_blocks)


def _sb_valid(u, qi0, nb, nqb):
    shape = (nb * SB_BLOCK, nqb * 2 * SB_BLOCK)
    key = u * (nb * SB_BLOCK) + lax.broadcasted_iota(I32, shape, 0)
    col = lax.broadcasted_iota(I32, shape, 1)
    qpos = (qi0 + col // (2 * SB_BLOCK)) * SB_BLOCK + col % SB_BLOCK
    return key < qpos


def _sb_logits(kb, qs, valid):
    z = _dot_nt(kb, qs)
    lb = jnp.minimum(z, 0.0) - jnp.log(1.0 + jnp.exp(-jnp.abs(z)))
    lk_all = lb - z
    lk = lk_all if valid is None else jnp.where(valid, lk_all, 0.0)
    return z, lb, lk_all, lk


def _sb_diag(x):
    w2 = 2 * SB_BLOCK
    ri = lax.broadcasted_iota(I32, (SB_BLOCK, w2), 0)
    ci = lax.broadcasted_iota(I32, (SB_BLOCK, w2), 1) % SB_BLOCK
    first = jnp.where(ri < ci, x[:, :w2], 0.0)
    return first if x.shape[1] == w2 else jnp.concatenate([first, x[:, w2:]], axis=1)


def _sb_add_from(full, part, lo):
    if lo == 0:
        return full + part
    return jnp.concatenate([full[:, :lo], full[:, lo:] + part], axis=1)


def _sb_scan(tri2, x, nb, reverse, exact=True):
    blk = SB_BLOCK
    edge = 0 if reverse else blk - 1
    carry = jnp.zeros((1, x.shape[1]), F32)
    res = [None] * nb
    for i in (reversed(range(nb)) if reverse else range(nb)):
        part = x[i * blk:(i + 1) * blk]
        if exact:
            hi, lo = _split2(part)
            raw = _dot(tri2, jnp.concatenate([hi, lo], axis=0))
        else:
            raw = _dot(tri2[:, :blk], part.astype(BF16))
        res[i] = raw + carry
        carry = carry + (raw[edge:edge + 1] + part[edge:edge + 1])
    return jnp.concatenate(res, axis=0), carry


def _sb_fwd(q, kv, kvt, *, name):
    bsz, s, w = q.shape
    blk = SB_BLOCK
    npair = w // LANES
    nq = s // blk
    nb, nqb = _sb_tile_blocks(nq, SB_Q_BLOCKS_FWD)
    width = nqb * 2 * blk

    def body(q_ref, k_ref, vt_ref, o_ref, tot_ref):
        qi0 = pl.program_id(2) * nqb
        qs = _sb_stack(q_ref[0] * SB_SCALE)
        ri = lax.broadcasted_iota(I32, (blk, blk), 0)
        ci = lax.broadcasted_iota(I32, (blk, blk), 1)
        upper = (ri < ci).astype(BF16)
        tri2 = jnp.concatenate([upper, upper], axis=1)

        def tile(u, r, acc, masked):
            rows = pl.ds(pl.multiple_of(u * (nb * blk), nb * blk), nb * blk)
            valid = _sb_valid(u, qi0, nb, nqb) if masked else None
            _, lb, _, lk = _sb_logits(k_ref[0, rows, :], qs, valid)
            sfx, total = _sb_scan(tri2, lk, nb, True)
            wgt = jnp.exp(lb + sfx + r)
            if masked:
                wgt = jnp.where(valid, wgt, 0.0)
            wb = wgt.astype(BF16)
            for i in range(nb):
                acc = acc + _dot(vt_ref[0, 0, u * nb + i], wb[i * blk:(i + 1) * blk])
            return r + total, acc

        def top_tile(u):
            r = jnp.zeros((1, width), F32)
            acc = jnp.zeros((LANES, width), F32)
            for kb in reversed(range(nb)):
                lo = kb * 2 * blk
                rows = pl.ds(pl.multiple_of((u * nb + kb) * blk, blk), blk)
                _, lb, lk_all, _ = _sb_logits(k_ref[0, rows, :], qs[lo:], None)
                lk = _sb_diag(lk_all)
                hi, lo_part = _split2(lk)
                raw = _dot(tri2, jnp.concatenate([hi, lo_part], axis=0))
                wgt = _sb_diag(jnp.exp(lb + raw + r[:, lo:]))
                add_acc = _dot(vt_ref[0, 0, u * nb + kb], wgt.astype(BF16))
                add_r = raw[0:1] + lk[0:1]
                acc = _sb_add_from(acc, add_acc, lo)
                r = _sb_add_from(r, add_r, lo)
            return r, acc

        top = qi0 // nb
        zero_r, zero_acc = jnp.zeros((1, width), F32), jnp.zeros((LANES, width), F32)
        r, acc = top_tile(top) if nb == nqb else tile(top, zero_r, zero_acc, True)
        r, acc = lax.fori_loop(0, top, lambda t, c: tile(top - 1 - t, c[0], c[1], False), (r, acc))
        o_ref[0] = _sb_unstack_t(acc).astype(BF16)
        tot_ref[0, 0, 0] = r

    qspec = pl.BlockSpec((1, nqb * blk, LANES), lambda b, p, i: (b, i, p))
    return pl.pallas_call(
        body, name=name, grid=(bsz, npair, nq // nqb),
        in_specs=[qspec,
                  pl.BlockSpec((1, s, LANES), lambda b, p, i: (b, 0, p)),
                  pl.BlockSpec((1, 1, nq, LANES, blk), lambda b, p, i: (b, npair + p, 0, 0, 0))],
        out_specs=[qspec, pl.BlockSpec((1, 1, 1, 1, width), lambda b, p, i: (b, p, i, 0, 0))],
        out_shape=[jax.ShapeDtypeStruct((bsz, s, w), BF16),
                   jax.ShapeDtypeStruct((bsz, npair, nq // nqb, 1, width), F32)],
        compiler_params=_params("parallel", "parallel", "arbitrary"),
    )(q, kv, kvt)


def _kv_blocks_t(kv3):
    bsz, s, w2 = kv3.shape
    x = kv3.reshape(bsz, s // SB_BLOCK, SB_BLOCK, w2 // LANES, LANES)
    return jnp.transpose(x, (0, 3, 1, 4, 2))


def _sb_bwd(q, kv, kvt, do, tot, dk_in, dv_in, *, name):
    bsz, s, w = q.shape
    blk = SB_BLOCK
    npair = w // LANES
    nq = s // blk
    nb, nqb = _sb_tile_blocks(nq, SB_Q_BLOCKS_BWD)
    width = nqb * 2 * blk
    tot = tot.reshape(bsz, npair, nq // nqb, 1, width)
    has_init = dk_in is not None

    def body(*refs):
        if has_init:
            q_ref, k_ref, v_ref, kt_ref, do_ref, tot_ref, dki_ref, dvi_ref, dq_ref, dk_ref, dv_ref = refs
        else:
            q_ref, k_ref, v_ref, kt_ref, do_ref, tot_ref, dq_ref, dk_ref, dv_ref = refs
        qi0 = pl.program_id(2) * nqb

        @pl.when(qi0 == 0)
        def _():
            if has_init:
                dk_ref[...] = dki_ref[...]
                dv_ref[...] = dvi_ref[...]
            else:
                dk_ref[...] = jnp.zeros_like(dk_ref)
                dv_ref[...] = jnp.zeros_like(dv_ref)

        qs = _sb_stack(q_ref[0] * SB_SCALE)
        dos = _sb_stack(do_ref[0])
        totv = tot_ref[0, 0, 0]
        ri = lax.broadcasted_iota(I32, (blk, blk), 0)
        ci = lax.broadcasted_iota(I32, (blk, blk), 1)
        lower = (ri > ci).astype(BF16)
        tri2 = jnp.concatenate([lower, lower], axis=1)

        def tile(u, pre_lk, pre_d, dqt, masked):
            rows = pl.ds(pl.multiple_of(u * (nb * blk), nb * blk), nb * blk)
            valid = _sb_valid(u, qi0, nb, nqb) if masked else None
            z, lb, lk_all, lk = _sb_logits(k_ref[0, rows, :], qs, valid)
            before, tot_lk = _sb_scan(tri2, lk, nb, False)
            wgt = jnp.exp(z + ((totv - pre_lk) - before))
            if masked:
                wgt = jnp.where(valid, wgt, 0.0)
            dlogit = _dot_nt(v_ref[0, rows, :], dos) * wgt
            dbefore, tot_d = _sb_scan(tri2, dlogit, nb, False, exact=False)
            sig = jnp.exp(lb)
            dz = dlogit * (1.0 - sig) - (pre_d + dbefore) * sig
            if masked:
                dz = jnp.where(valid, dz, 0.0)
            dz_b = dz.astype(BF16)
            for i in range(nb):
                dqt = dqt + _dot(kt_ref[0, 0, u * nb + i], dz_b[i * blk:(i + 1) * blk])
            dk_ref[0, rows, :] += _dot(dz_b, qs)
            dv_ref[0, rows, :] += _dot(wgt.astype(BF16), dos)
            return pre_lk + tot_lk, pre_d + tot_d, dqt

        def top_tile(u, pre_lk, pre_d, dqt):
            rest = totv - pre_lk
            for kb in range(nb):
                lo = kb * 2 * blk
                rows = pl.ds(pl.multiple_of((u * nb + kb) * blk, blk), blk)
                qs_k, dos_k = qs[lo:], dos[lo:]
                z, lb, lk_all, _ = _sb_logits(k_ref[0, rows, :], qs_k, None)
                lk = _sb_diag(lk_all)
                hi, lo_part = _split2(lk)
                raw = _dot(tri2, jnp.concatenate([hi, lo_part], axis=0))
                wgt = _sb_diag(jnp.exp(z + (rest[:, lo:] - raw)))
                dlogit = _dot_nt(v_ref[0, rows, :], dos_k) * wgt
                draw = _dot(tri2[:, :blk], dlogit.astype(BF16))
                sig = jnp.exp(lb)
                dz_b = _sb_diag(dlogit * (1.0 - sig) - (pre_d[:, lo:] + draw) * sig).astype(BF16)
                dqt = _sb_add_from(dqt, _dot(kt_ref[0, 0, u * nb + kb], dz_b), lo)
                dk_ref[0, rows, :] += _dot(dz_b, qs_k)
                dv_ref[0, rows, :] += _dot(wgt.astype(BF16), dos_k)
                rest = _sb_add_from(rest, -(raw[blk - 1:blk] + lk[blk - 1:blk]), lo)
                pre_d = _sb_add_from(pre_d, draw[blk - 1:blk] + dlogit[blk - 1:blk], lo)
            return dqt

        zero = jnp.zeros((1, width), F32)
        top = qi0 // nb
        c = lax.fori_loop(0, top, lambda u, c: tile(u, c[0], c[1], c[2], False),
                          (zero, zero, jnp.zeros((LANES, width), F32)))
        dqt = top_tile(top, *c) if nb == nqb else tile(top, c[0], c[1], c[2], True)[2]
        dq_ref[0] = (_sb_unstack_t(dqt) * SB_SCALE).astype(BF16)

    qspec = pl.BlockSpec((1, nqb * blk, LANES), lambda b, p, i: (b, i, p))
    kspec = pl.BlockSpec((1, s, LANES), lambda b, p, i: (b, 0, p))
    vspec = pl.BlockSpec((1, s, LANES), lambda b, p, i: (b, 0, npair + p))
    ktspec = pl.BlockSpec((1, 1, nq, LANES, blk), lambda b, p, i: (b, p, 0, 0, 0))
    tspec = pl.BlockSpec((1, 1, 1, 1, width), lambda b, p, i: (b, p, i, 0, 0))
    in_specs = [qspec, kspec, vspec, ktspec, qspec, tspec] + ([kspec, kspec] if has_init else [])
    args = (q, kv, kv, kvt, do, tot) + ((dk_in, dv_in) if has_init else ())
    return pl.pallas_call(
        body, name=name, grid=(bsz, npair, nq // nqb), in_specs=in_specs,
        out_specs=[qspec, kspec, kspec],
        out_shape=[jax.ShapeDtypeStruct((bsz, s, w), BF16), jax.ShapeDtypeStruct((bsz, s, w), F32),
                   jax.ShapeDtypeStruct((bsz, s, w), F32)],
        compiler_params=_params("parallel", "parallel", "arbitrary"),
    )(*args)


ADAM_BLOCK_BYTES = 1 << 20


def _adamw(w, g, m, v, *, name):
    shape = w.shape
    r, c = shape[-2], shape[-1]
    lead = _size(shape[:-2])
    tr = r
    for cand in range(8, r, 8):
        if r % cand == 0 and cand * c * 4 <= ADAM_BLOCK_BYTES:
            tr = cand
    if r * c * 4 <= ADAM_BLOCK_BYTES:
        tr = r

    def body(w_ref, g_ref, m_ref, v_ref, d_ref, mo_ref, vo_ref):
        gv = g_ref[...]
        mn = ADAM_B1 * m_ref[...] + (1.0 - ADAM_B1) * gv
        vn = ADAM_B2 * v_ref[...] + (1.0 - ADAM_B2) * (gv * gv)
        m_hat = mn / (1.0 - ADAM_B1 ** ADAM_STEP)
        v_hat = vn / (1.0 - ADAM_B2 ** ADAM_STEP)
        d_ref[...] = -ADAM_LR * (m_hat / (jnp.sqrt(v_hat) + ADAM_EPS) + ADAM_WD * w_ref[...])
        mo_ref[...] = mn
        vo_ref[...] = vn

    blk = pl.BlockSpec((1, tr, c), lambda l, i: (l, i, 0))
    outs = pl.pallas_call(
        body, name=name, grid=(lead, r // tr), in_specs=[blk] * 4, out_specs=[blk] * 3,
        out_shape=[jax.ShapeDtypeStruct((lead, r, c), F32)] * 3,
        compiler_params=_params("parallel", "parallel"),
    )(*[a.reshape(lead, r, c) for a in (w, g, m, v)])
    return [o.reshape(shape) for o in outs]


def _row_tile(r, c, itemsize):
    if r * c * 4 <= ADAM_BLOCK_BYTES:
        return r
    step = 32 // itemsize
    tr = r
    for cand in range(step, r, step):
        if r % cand == 0 and cand * c * 4 <= ADAM_BLOCK_BYTES:
            tr = cand
    return tr


def _add_own_half(idx, g, recv, *, name):
    _, lh, r, c = recv.shape
    tr = _row_tile(r, c, g.dtype.itemsize)

    def body(idx_ref, a_ref, b_ref, o_ref):
        o_ref[...] = (a_ref[...].astype(F32) + b_ref[...].astype(F32)).astype(o_ref.dtype)

    blk = pl.BlockSpec((1, 1, tr, c), lambda k, l, i, idx: (k, l, i, 0))
    return pl.pallas_call(
        body, name=name,
        grid_spec=pltpu.PrefetchScalarGridSpec(
            num_scalar_prefetch=1, grid=(N_CHIPS, lh, r // tr),
            in_specs=[pl.BlockSpec((1, 1, tr, c), lambda k, l, i, idx: (k, idx[0] * lh + l, i, 0)), blk],
            out_specs=blk),
        out_shape=jax.ShapeDtypeStruct(recv.shape, g.dtype),
        compiler_params=_params("parallel", "parallel", "parallel"),
    )(idx, g, recv)


def _add_chips(idx, own, recv, *, name):
    _, lh, r, c = own.shape
    tr = _row_tile(r, c, own.dtype.itemsize)

    def body(idx_ref, a_ref, b_ref, o_ref):
        f = lambda v: v.astype(F32)
        o_ref[0] = ((f(a_ref[0, 0]) + f(b_ref[0, 0])) + f(b_ref[1, 0])) + f(b_ref[2, 0])

    return pl.pallas_call(
        body, name=name,
        grid_spec=pltpu.PrefetchScalarGridSpec(
            num_scalar_prefetch=1, grid=(lh, r // tr),
            in_specs=[pl.BlockSpec((1, 1, tr, c), lambda l, i, idx: (idx[0], l, i, 0)),
                      pl.BlockSpec((3, 1, tr, c), lambda l, i, idx: (0, l, i, 0))],
            out_specs=pl.BlockSpec((1, tr, c), lambda l, i, idx: (l, i, 0))),
        out_shape=jax.ShapeDtypeStruct((lh, r, c), F32),
        compiler_params=_params("parallel", "parallel"),
    )(idx, own, recv)


def _sum_devices(parts):
    _, r, _ = parts.shape

    def body(p_ref, o_ref):
        acc = p_ref[0]
        for k in range(1, N_DEV):
            acc = acc + p_ref[k]
        o_ref[...] = acc

    return pl.pallas_call(
        body, name="small_grad_sum", grid=(1,),
        in_specs=[pl.BlockSpec((N_DEV, r, LANES), lambda i: (0, 0, 0))],
        out_specs=pl.BlockSpec((r, LANES), lambda i: (0, 0)),
        out_shape=jax.ShapeDtypeStruct((r, LANES), F32),
    )(parts)


def _place():
    return lax.axis_index("x"), lax.axis_index("y"), lax.axis_index("c")


def _rcopy(src, dst, send_sems, recv_sems, k, to):
    return pltpu.make_async_remote_copy(src_ref=src, dst_ref=dst, send_sem=send_sems.at[k],
                                        recv_sem=recv_sems.at[k], device_id=to, device_id_type=MESH)


def _exchange_call(body, name, ins, out_shapes, n_sems):
    return pl.pallas_call(
        body, name=name, in_specs=[ANY] * len(ins), out_specs=[ANY] * len(out_shapes),
        out_shape=out_shapes,
        scratch_shapes=[pltpu.SemaphoreType.DMA((n_sems,)), pltpu.SemaphoreType.DMA((n_sems,))],
    )(*ins)


def _gather_weights(shards):
    n = len(shards)

    def body(*refs):
        ins, outs, send_sems, recv_sems = refs[:n], refs[n:2 * n], refs[2 * n], refs[2 * n + 1]
        x, y, c = _place()
        sibling = (x, y, 1 - c)
        chips = [(1 - x, y), (x, 1 - y), (1 - x, 1 - y)]

        def piece(i, px, py, pc):
            lh = ins[i].shape[0] // 2
            return outs[i].at[2 * px + py, pl.ds(pc * lh, lh)]

        def mine(i):
            lh = ins[i].shape[0] // 2
            return ins[i].at[pl.ds(c * lh, lh)]

        first = [_rcopy(mine(i), piece(i, x, y, c), send_sems, recv_sems, 6 * i + j, (*chip, c))
                 for i in range(n) for j, chip in enumerate(chips)]
        for cp in first:
            cp.start()
        passed = []
        for i in range(n):
            for j, chip in enumerate(chips):
                landed = piece(i, *chip, c)
                _rcopy(landed, landed, send_sems, recv_sems, 6 * i + j, (*chip, c)).wait_recv()
                passed.append(_rcopy(landed, landed, send_sems, recv_sems, 6 * i + 3 + j, sibling))
                passed[-1].start()
        for i in range(n):
            for j, chip in enumerate(chips):
                theirs = piece(i, *chip, 1 - c)
                _rcopy(theirs, theirs, send_sems, recv_sems, 6 * i + 3 + j, sibling).wait_recv()
        for cp in first + passed:
            cp.wait_send()

    shapes = [jax.ShapeDtypeStruct((N_CHIPS,) + s_.shape, s_.dtype) for s_ in shards]
    return _exchange_call(body, "gather_weights", shards, shapes, 6 * n)


def _swap_halves(gs):
    n = len(gs)

    def body(*refs):
        ins, outs, send_sems, recv_sems = refs[:n], refs[n:2 * n], refs[2 * n], refs[2 * n + 1]
        x, y, c = _place()
        cps = []
        for i in range(n):
            lh = ins[i].shape[1] // 2
            src = ins[i].at[pl.ds(0, N_CHIPS), pl.ds((1 - c) * lh, lh)]
            cps.append(_rcopy(src, outs[i], send_sems, recv_sems, i, (x, y, 1 - c)))
        for cp in cps:
            cp.start()
        for cp in cps:
            cp.wait()

    shapes = [jax.ShapeDtypeStruct((N_CHIPS, g.shape[1] // 2) + g.shape[2:], g.dtype) for g in gs]
    return _exchange_call(body, "grad_swap_halves", gs, shapes, n)


def _scatter_chunks(sums):
    n = len(sums)

    def body(*refs):
        ins, outs, send_sems, recv_sems = refs[:n], refs[n:2 * n], refs[2 * n], refs[2 * n + 1]
        x, y, c = _place()
        chips = [(1 - x, y), (x, 1 - y), (1 - x, 1 - y)]
        cps = [_rcopy(ins[i].at[2 * chip[0] + chip[1]], outs[i].at[j], send_sems, recv_sems, 3 * i + j,
                      (*chip, c)) for i in range(n) for j, chip in enumerate(chips)]
        for cp in cps:
            cp.start()
        for cp in cps:
            cp.wait()

    shapes = [jax.ShapeDtypeStruct((3,) + s_.shape[1:], s_.dtype) for s_ in sums]
    return _exchange_call(body, "grad_scatter_chunks", sums, shapes, 3 * n)


def _share_half(tots):
    n = len(tots)

    def body(*refs):
        ins, outs, send_sems, recv_sems = refs[:n], refs[n:2 * n], refs[2 * n], refs[2 * n + 1]
        x, y, c = _place()
        cps = [_rcopy(ins[i], outs[i], send_sems, recv_sems, i, (x, y, 1 - c)) for i in range(n)]
        for cp in cps:
            cp.start()
        for cp in cps:
            cp.wait()

    shapes = [jax.ShapeDtypeStruct(t_.shape, t_.dtype) for t_ in tots]
    return _exchange_call(body, "grad_share_half", tots, shapes, n)


def _exchange_small(r):
    rr, _ = r.shape

    def body(r_ref, out_ref, send_sems, recv_sems, local_sem):
        x, y, c = _place()
        me = 4 * x + 2 * y + c
        mine = pltpu.make_async_copy(r_ref, out_ref.at[me], local_sem)
        mine.start()
        cps = []
        for k in range(N_DEV - 1):
            fx, fy, fc = ((k + 1) >> 2) & 1, ((k + 1) >> 1) & 1, (k + 1) & 1
            to = (x ^ fx, y ^ fy, c ^ fc)
            cps.append((_rcopy(r_ref, out_ref.at[me], send_sems, recv_sems, k, to), to))
        for cp, _ in cps:
            cp.start()
        for k, (cp, to) in enumerate(cps):
            src = 4 * to[0] + 2 * to[1] + to[2]
            _rcopy(r_ref, out_ref.at[src], send_sems, recv_sems, k, to).wait_recv()
        for cp, _ in cps:
            cp.wait_send()
        mine.wait()

    return pl.pallas_call(
        body, name="small_grad_exchange", in_specs=[ANY], out_specs=ANY,
        out_shape=jax.ShapeDtypeStruct((N_DEV, rr, LANES), r.dtype),
        scratch_shapes=[pltpu.SemaphoreType.DMA((N_DEV - 1,)), pltpu.SemaphoreType.DMA((N_DEV - 1,)),
                        pltpu.SemaphoreType.DMA],
    )(r)


def _size(shape):
    n = 1
    for d in shape:
        n *= d
    return n


def _slab_rows(shape):
    rows = -(-_size(shape) // LANES)
    return -(-rows // SLAB_ROW_ALIGN) * SLAB_ROW_ALIGN


def _pack_rows(arrs, dtype, lead=0, unit=PACK_ROWS):
    parts, total = [], 0
    for a in arrs:
        front, shp = a.shape[:lead], a.shape[lead:]
        n, rows = _size(shp), _slab_rows(shp)
        nopad = [(0, 0)] * lead
        if n % LANES == 0:
            p = a.reshape(front + (n // LANES, LANES)).astype(dtype)
        else:
            p = jnp.pad(a.reshape(front + (n,)).astype(dtype), nopad + [(0, rows * LANES - n)])
            p = p.reshape(front + (rows, LANES))
        if p.shape[lead] != rows:
            p = jnp.pad(p, nopad + [(0, rows - p.shape[lead]), (0, 0)])
        parts.append(p)
        total += rows
    pad = (-total) % unit
    if pad:
        parts.append(jnp.zeros(parts[0].shape[:lead] + (pad, LANES), dtype))
    return jnp.concatenate(parts, axis=lead)


def _unpack_rows(slab, shapes):
    lead = slab.shape[:-2]
    out, off = [], 0
    for shp in shapes:
        n, rows = _size(shp), _slab_rows(shp)
        piece = slab[..., off:off + rows, :]
        if n % LANES == 0:
            piece = piece[..., :n // LANES, :].reshape(lead + tuple(shp))
        else:
            piece = piece.reshape(lead + (rows * LANES,))[..., :n].reshape(lead + tuple(shp))
        out.append(piece)
        off += rows
    return out


def _ffn_fwd(h, bsz, s, gain, w_up, cw, cb, w_down, i):
    hf = _rmsnorm_fwd(h, gain, name=f"ffn_norm_{i}")
    up = _mm(hf, w_up, name=f"ffn_up_{i}")
    up3 = up.reshape(bsz, s, -1)
    f = _conv_glu_fwd(up3, cw, cb, name=f"ffn_glu_{i}").reshape(h.shape[0], -1)
    h2 = _mm(f, w_down, add=h, name=f"ffn_down_{i}")
    return h2, (h, hf, up3, f)


def _ffn_bwd(dh, saved, gain, w_up, cw, cb, w_down, i):
    h, hf, up3, f = saved
    t = h.shape[0]
    d_down = _mm(f, dh, ta=True, name=f"ffn_down_dw_{i}")
    df = _mm(dh, w_down, tb=True, name=f"ffn_down_dx_{i}")
    dug, duv, dcw, dcb = _conv_glu_bwd(up3, cw, cb, df.reshape(up3.shape[0], up3.shape[1], -1),
                                       name=f"ffn_glu_bwd_{i}")
    dug, duv = dug.reshape(t, -1), duv.reshape(t, -1)
    fdim = dug.shape[1]
    d_up = jnp.concatenate([_mm(hf, dug, ta=True, name=f"ffn_up_dwg_{i}"),
                            _mm(hf, duv, ta=True, name=f"ffn_up_dwv_{i}")], axis=1)
    dhf = _mm(dug, w_up[:, :fdim], tb=True, name=f"ffn_up_dxg_{i}")
    dh, dgain = _mm_norm_bwd(duv, w_up[:, fdim:], h, gain, dh, add=dhf, name=f"ffn_up_dxv_{i}")
    return dh, dgain, d_up, dcw, dcb, d_down


def _ple_layer_fwd(h, p_i, gain, w_gate, w_proj, i):
    hp = _rmsnorm_fwd(h, gain, name=f"ple_norm_{i}")
    a = _mm(hp, w_gate, name=f"ple_gate_{i}")
    pp = _mm(p_i, w_proj, name=f"ple_proj_{i}")
    return _ple_fwd(h, a, pp, name=f"ple_mix_{i}"), (h, hp, a, pp)


def _ple_layer_bwd(dh, saved, p_i, gain, w_gate, i):
    h, hp, a, pp = saved
    da, dpp = _ple_bwd(dh, a, pp, name=f"ple_mix_bwd_{i}")
    d_gate = _mm(hp, da, ta=True, name=f"ple_gate_dw_{i}")
    d_proj = _mm(p_i, dpp, ta=True, name=f"ple_proj_dw_{i}")
    dh, dgain = _mm_norm_bwd(da, w_gate, h, gain, dh, name=f"ple_gate_dx_{i}")
    return dh, dgain, d_gate, d_proj


def _ssm_consts(dt_bias, a_log, d_skip, g_n):
    hpg = SSM_HEADS_PER_GROUP
    a = -jnp.exp(a_log)
    rows = jnp.stack([dt_bias.reshape(g_n, hpg), a.reshape(g_n, hpg)], axis=1)
    plane = jnp.zeros((g_n, 8, LANES), F32).at[:, 0:2, 0:hpg].set(rows)
    psub = jnp.zeros((g_n, SSM_ROWS, LANES), F32).at[:, 0:hpg, 0:2].set(jnp.swapaxes(rows, 1, 2))
    d_ch = jnp.repeat(d_skip, SSM_HEAD_DIM).reshape(1, -1)
    return a, plane, psub, d_ch


def _ssm_in_big(w_in, d_inner, g_n):
    d = w_in.shape[0]
    cut = w_in.shape[1] - g_n * SSM_HEADS_PER_GROUP
    wdt = w_in[:, cut:].reshape(d, g_n, SSM_HEADS_PER_GROUP)
    wdt = jnp.pad(wdt, ((0, 0), (0, 0), (0, LANES - SSM_HEADS_PER_GROUP))).reshape(d, g_n * LANES)
    return jnp.concatenate([w_in[:, :cut], wdt], axis=1)


def _ssm_in_small(dw_big, g_n):
    d = dw_big.shape[0]
    cut = dw_big.shape[1] - g_n * LANES
    ddt = dw_big[:, cut:].reshape(d, g_n, LANES)[:, :, :SSM_HEADS_PER_GROUP].reshape(d, -1)
    return jnp.concatenate([dw_big[:, :cut], ddt], axis=1)


def _ssm_fwd(h, bsz, s, gain, w_in_big, cw, cb, plane, psub, d_ch, nw, w_out, i):
    d_inner = d_ch.shape[1]
    g_n = d_inner // SSM_GROUP_W
    conv_dim = cw.shape[1]
    hn = _rmsnorm_fwd(h, gain, name=f"attn_norm_{i}")
    zx = _mm(hn, w_in_big, name=f"ssm_in_{i}").reshape(bsz, s, -1)
    xbc = _conv_silu_fwd(zx, d_inner, cw, cb, name=f"ssm_conv_{i}")
    dtr = zx[:, :, d_inner + conv_dim:].reshape(bsz, s, g_n, LANES)[..., :SSM_HEADS_PER_GROUP]
    dtr_row = jnp.pad(jnp.transpose(dtr, (0, 2, 3, 1)),
                      ((0, 0), (0, 0), (0, SSM_ROWS - SSM_HEADS_PER_GROUP), (0, 0)))
    gn, states = _ssd_fwd(zx, xbc, dtr_row, plane, psub, d_ch, nw, name=f"ssd_{i}")
    gn2 = gn.reshape(h.shape[0], -1)
    h1 = _mm(gn2, w_out, add=h, name=f"ssm_out_{i}")
    return h1, (h, hn, zx, xbc, dtr_row, states, gn2)


def _ssm_bwd(dh, saved, gain, w_in_big, cw, cb, plane, psub, d_ch, nw, w_out, i):
    h, hn, zx, xbc, dtr_row, states, gn2 = saved
    t = h.shape[0]
    bsz, s, _ = zx.shape
    d_inner = d_ch.shape[1]
    d_out = _mm(gn2, dh, ta=True, name=f"ssm_out_dw_{i}")
    dgn = _mm(dh, w_out, tb=True, name=f"ssm_out_dx_{i}").reshape(bsz, s, -1)
    dxs, dbm, dcm, dz, ddtr, ach, aln = _ssd_bwd(zx, xbc, dtr_row, plane, psub, d_ch, nw, states, dgn,
                                                  name=f"ssd_bwd_{i}")
    dxbc, dcw, dcb = _conv_silu_bwd(zx, d_inner, cw, cb, [dxs, dbm, dcm], name=f"ssm_conv_bwd_{i}")
    d_in_parts, dhn, col = [], None, 0
    for tag, part in (("z", dz), ("dt", ddtr), ("xbc", dxbc)):
        part = part.reshape(t, -1)
        col = {"z": 0, "xbc": dz.shape[-1], "dt": dz.shape[-1] + dxbc.shape[-1]}[tag]
        w_part = w_in_big[:, col:col + part.shape[1]]
        d_in_parts.append(_mm(hn, part, ta=True, name=f"ssm_in_dw_{tag}_{i}"))
        if tag == "xbc":
            dh, dgain = _mm_norm_bwd(part, w_part, h, gain, dh, add=dhn, name=f"ssm_in_dx_{tag}_{i}")
        else:
            dhn = _mm(part, w_part, tb=True, add=dhn, name=f"ssm_in_dx_{tag}_{i}")
    d_in_big = jnp.concatenate([d_in_parts[0], d_in_parts[2], d_in_parts[1]], axis=1)
    hpg = SSM_HEADS_PER_GROUP
    ach = jnp.sum(ach, axis=0)
    aln = jnp.sum(aln, axis=0)
    d_nw = ach[:, 0, :].reshape(-1)
    d_dskip = jnp.sum(ach[:, 1, :].reshape(-1, SSM_HEAD_DIM), axis=1)
    d_bias = aln[:, 0, :hpg].reshape(-1)
    d_a = aln[:, 1, :hpg].reshape(-1)
    return dh, dgain, d_in_big, dcw, dcb, d_bias, d_a, d_dskip, d_nw, d_out


def _sb_layer_fwd(h, bsz, s, gain, w_q, w_o, kv3, kvt, i):
    hn = _rmsnorm_fwd(h, gain, name=f"attn_norm_{i}")
    q3 = _mm(hn, w_q, out_dtype=BF16, name=f"sb_q_{i}").reshape(bsz, s, -1)
    o3, tot = _sb_fwd(q3, kv3, kvt, name=f"sb_attn_{i}")
    o2 = o3.reshape(h.shape[0], -1)
    h1 = _mm(o2, w_o, add=h, name=f"sb_o_{i}")
    return h1, (h, hn, q3, o2, tot)


def _sb_layer_bwd(dh, saved, gain, w_q, w_o, kv3, kvt, dk, dv, i):
    h, hn, q3, o2, tot = saved
    t = h.shape[0]
    d_o = _mm(o2, dh, ta=True, name=f"sb_o_dw_{i}")
    do3 = _mm(dh, w_o, tb=True, out_dtype=BF16, name=f"sb_o_dx_{i}").reshape(q3.shape)
    dq3, dk, dv = _sb_bwd(q3, kv3, kvt, do3, tot, dk, dv, name=f"sb_attn_bwd_{i}")
    dq = dq3.reshape(t, -1)
    d_q = _mm(hn, dq, ta=True, name=f"sb_q_dw_{i}")
    dh, dgain = _mm_norm_bwd(dq, w_q, h, gain, dh, name=f"sb_q_dx_{i}")
    return dh, dgain, d_q, d_o, dk, dv


def kernel(x, p, attn_norm, ffn_norm, ple_norm, ssm_in_proj, ssm_conv_w, ssm_conv_b, ssm_dt_bias, ssm_a_log, ssm_d, ssm_norm, ssm_out_proj, kv_norm, w_kv, w_q, w_o, ffn_up, ffn_conv_w, ffn_conv_b, ffn_down, ple_gate, ple_proj, final_norm, loss_target, m_attn_norm, m_ffn_norm, m_ple_norm, m_ssm_in_proj, m_ssm_conv_w, m_ssm_conv_b, m_ssm_dt_bias, m_ssm_a_log, m_ssm_d, m_ssm_norm, m_ssm_out_proj, m_kv_norm, m_w_kv, m_w_q, m_w_o, m_ffn_up, m_ffn_conv_w, m_ffn_conv_b, m_ffn_down, m_ple_gate, m_ple_proj, m_final_norm, v_attn_norm, v_ffn_norm, v_ple_norm, v_ssm_in_proj, v_ssm_conv_w, v_ssm_conv_b, v_ssm_dt_bias, v_ssm_a_log, v_ssm_d, v_ssm_norm, v_ssm_out_proj, v_kv_norm, v_w_kv, v_w_q, v_w_o, v_ffn_up, v_ffn_conv_w, v_ffn_conv_b, v_ffn_down, v_ple_gate, v_ple_proj, v_final_norm):
    given = dict(locals())
    wl = {n: given[n] for n in WEIGHTS}
    bsz, s, d = x.shape
    t = bsz * s
    depth = attn_norm.shape[0]
    n_a = ssm_in_proj.shape[0]
    d_inner = ssm_norm.shape[1] * N_CHIPS
    g_n = d_inner // SSM_GROUP_W
    cidx = lax.axis_index("c").astype(I32).reshape(1)
    chip_idx = (2 * lax.axis_index("x") + lax.axis_index("y")).astype(I32).reshape(1)

    big = [n for n in SHARDED if _size(wl[n].shape) >= BIG_WEIGHT]
    small = [n for n in SHARDED if n not in big]
    small_shapes = [wl[n].shape for n in small]
    halves = lambda shp: shp if len(shp) == 3 else (2, shp[0] // 2, shp[1])
    small_slab = _pack_rows([wl[n] for n in small], BF16, unit=2 * SLAB_ROW_ALIGN)
    small_rows = small_slab.shape[0]
    mine = [wl[n].astype(BF16).reshape(halves(wl[n].shape)) for n in big]
    mine.append(small_slab.reshape(2, small_rows // 2, LANES))
    gathered = [lax.dynamic_update_index_in_dim(g, m_, chip_idx[0], 0)
                for g, m_ in zip(_gather_weights(mine), mine)]
    per_chip = {n: g.reshape((N_CHIPS,) + wl[n].shape) for n, g in zip(big, gathered)}
    per_chip.update(zip(small, _unpack_rows(gathered[-1].reshape(N_CHIPS, small_rows, LANES), small_shapes)))
    full = {}
    for n in SHARDED:
        ax, piece = SHARD_AXIS[n], per_chip[n]
        merged = piece.shape[1:ax + 1] + (N_CHIPS * piece.shape[ax + 1],) + piece.shape[ax + 2:]
        full[n] = jnp.moveaxis(piece, 0, ax).reshape(merged)

    h = x.reshape(t, d)
    tgt = loss_target.reshape(t, d)
    saved = []
    kv3 = kvt = hkv = h_kv_in = None
    consts = []
    for i in range(depth):
        if i < n_a:
            a_neg, plane, psub, d_ch = _ssm_consts(ssm_dt_bias[i], ssm_a_log[i], ssm_d[i], g_n)
            w_in_big = _ssm_in_big(full["ssm_in_proj"][i], d_inner, g_n)
            cw = full["ssm_conv_w"][i].astype(F32)
            cb = full["ssm_conv_b"][i].astype(F32)
            nw = full["ssm_norm"][i].astype(F32).reshape(1, -1)
            consts.append((a_neg, plane, psub, d_ch, w_in_big, cw, cb, nw))
            h, sv_mix = _ssm_fwd(h, bsz, s, attn_norm[i], w_in_big, cw, cb, plane, psub, d_ch, nw,
                                 full["ssm_out_proj"][i], i)
        else:
            j = i - n_a
            h, sv_mix = _sb_layer_fwd(h, bsz, s, attn_norm[i], full["w_q"][j], full["w_o"][j], kv3, kvt, i)
        fcw = full["ffn_conv_w"][i].astype(F32)
        h, sv_ffn = _ffn_fwd(h, bsz, s, ffn_norm[i], full["ffn_up"][i], fcw, ffn_conv_b[i],
                             full["ffn_down"][i], i)
        p_i = p[i].reshape(t, -1)
        h, sv_ple = _ple_layer_fwd(h, p_i, ple_norm[i], full["ple_gate"][i], full["ple_proj"][i], i)
        saved.append((sv_mix, sv_ffn, sv_ple))
        if i == n_a - 1:
            h_kv_in = h
            hkv = _rmsnorm_fwd(h, kv_norm, name="kv_norm")
            kv3 = _mm(hkv, full["w_kv"], out_dtype=BF16, name="kv_proj").reshape(bsz, s, -1)
            kvt = _kv_blocks_t(kv3)

    loss_local, dh, g_final = _final_loss(h, final_norm, tgt)
    gr = {n: [None] * wl[n].shape[0] for n in WEIGHTS if n not in ("kv_norm", "w_kv", "final_norm")}
    gr["final_norm"] = g_final
    dk = dv = None
    for i in reversed(range(depth)):
        sv_mix, sv_ffn, sv_ple = saved[i]
        if i == n_a - 1:
            dkv = jnp.concatenate([dk, dv], axis=-1).reshape(t, -1)
            gr["w_kv"] = _mm(hkv, dkv, ta=True, name="kv_proj_dw")
            dh, gr["kv_norm"] = _mm_norm_bwd(dkv, full["w_kv"], h_kv_in, kv_norm, dh, name="kv_proj_dx")
        p_i = p[i].reshape(t, -1)
        dh, gr["ple_norm"][i], gr["ple_gate"][i], gr["ple_proj"][i] = _ple_layer_bwd(
            dh, sv_ple, p_i, ple_norm[i], full["ple_gate"][i], i)
        fcw = full["ffn_conv_w"][i].astype(F32)
        (dh, gr["ffn_norm"][i], gr["ffn_up"][i], gr["ffn_conv_w"][i], gr["ffn_conv_b"][i],
         gr["ffn_down"][i]) = _ffn_bwd(dh, sv_ffn, ffn_norm[i], full["ffn_up"][i], fcw, ffn_conv_b[i],
                                       full["ffn_down"][i], i)
        if i < n_a:
            a_neg, plane, psub, d_ch, w_in_big, cw, cb, nw = consts[i]
            (dh, gr["attn_norm"][i], d_in_big, gr["ssm_conv_w"][i], gr["ssm_conv_b"][i],
             gr["ssm_dt_bias"][i], d_a, gr["ssm_d"][i], gr["ssm_norm"][i],
             gr["ssm_out_proj"][i]) = _ssm_bwd(dh, sv_mix, attn_norm[i], w_in_big, cw, cb, plane, psub,
                                               d_ch, nw, full["ssm_out_proj"][i], i)
            gr["ssm_in_proj"][i] = _ssm_in_small(d_in_big, g_n)
            gr["ssm_a_log"][i] = d_a * a_neg
        else:
            j = i - n_a
            dh, gr["attn_norm"][i], gr["w_q"][j], gr["w_o"][j], dk, dv = _sb_layer_bwd(
                dh, sv_mix, attn_norm[i], full["w_q"][j], full["w_o"][j], kv3, kvt, dk, dv, i)
    grad_x = dh.reshape(bsz, s, d)
    gfull = {n: (jnp.stack(v) if isinstance(v, list) else v) for n, v in gr.items()}

    by_chip = {}
    for n in SHARDED:
        ax, shp = SHARD_AXIS[n], gfull[n].shape
        split = gfull[n].reshape(shp[:ax] + (N_CHIPS, shp[ax] // N_CHIPS) + shp[ax + 1:])
        by_chip[n] = jnp.moveaxis(split, ax, 0)
    g4 = [by_chip[n].astype(BF16).reshape((N_CHIPS,) + halves(wl[n].shape)) for n in big]
    g4.append(_pack_rows([by_chip[n] for n in small], BF16, lead=1, unit=2 * SLAB_ROW_ALIGN)
              .reshape(N_CHIPS, 2, small_rows // 2, LANES))
    tags = big + ["small"]
    from_sibling = _swap_halves(g4)
    chip_sums = [_add_own_half(cidx, g, r_, name="grad_pair_sum_" + tg)
                 for g, r_, tg in zip(g4, from_sibling, tags)]
    from_chips = _scatter_chunks(chip_sums)
    my_half = [_add_chips(chip_idx, s_, r_, name="grad_chip_sum_" + tg)
               for s_, r_, tg in zip(chip_sums, from_chips, tags)]
    other_half = _share_half(my_half)
    low_core = cidx[0] == 0
    reduced = [jnp.concatenate([jnp.where(low_core, a, b_), jnp.where(low_core, b_, a)], axis=0)
               for a, b_ in zip(my_half, other_half)]
    grads = {n: g.reshape(wl[n].shape) for n, g in zip(big, reduced)}
    grads.update(zip(small, _unpack_rows(reduced[-1].reshape(small_rows, LANES), small_shapes)))

    rep_shapes = [wl[n].shape for n in REPLICATED]
    packed_r = _pack_rows([gfull[n] for n in REPLICATED], F32, unit=SLAB_ROW_ALIGN)
    g_rep = _sum_devices(_exchange_small(packed_r))

    delta, new_m, new_v = {}, {}, {}
    for n in SHARDED:
        delta[n], new_m[n], new_v[n] = _adamw(wl[n], grads[n], given["m_" + n], given["v_" + n],
                                              name="adamw_" + n)
    slabs = [_pack_rows([src[pre + n] for n in REPLICATED], F32, unit=SLAB_ROW_ALIGN)
             for src, pre in ((wl, ""), (given, "m_"), (given, "v_"))]
    rep_out = _adamw(slabs[0], g_rep, slabs[1], slabs[2], name="adamw_replicated")
    for dst, slab in zip((grads, delta, new_m, new_v), [g_rep] + list(rep_out)):
        dst.update(zip(REPLICATED, _unpack_rows(slab, rep_shapes)))
    loss = lax.psum(loss_local, ("x", "y", "c"))
    return (loss, grad_x, *[grads[n] for n in WEIGHTS], *[delta[n] for n in WEIGHTS],
            *[new_m[n] for n in WEIGHTS], *[new_v[n] for n in WEIGHTS])
```

```python
import functools

import jax
import jax.numpy as jnp
from jax import lax
from jax.experimental import pallas as pl
from jax.experimental.pallas import tpu as pltpu

F32 = jnp.float32
BF16 = jnp.bfloat16
I32 = jnp.int32

NORM_EPS = 1e-6
SSM_NORM_EPS = 1e-5
SSM_HEAD_DIM = 64
SSM_STATE = 128
SSM_CHUNK = 128
SSM_HEADS_PER_GROUP = 8
SSM_GROUP_W = SSM_HEADS_PER_GROUP * SSM_HEAD_DIM
SSM_CONV = 4
SSM_ROWS = 16
SB_HEAD_DIM = 64
SB_BLOCK = 128
SB_SCALE = SB_HEAD_DIM ** -0.5
SB_Q_BLOCKS_FWD = 4
SB_Q_BLOCKS_BWD = 4
FFN_CONV = 3
LANES = 128
N_CHIPS = 4
N_DEV = 8

ADAM_LR = 0.001
ADAM_B1 = 0.9
ADAM_B2 = 0.999
ADAM_EPS = 1e-08
ADAM_WD = 0.01
ADAM_STEP = 10

MESH = pl.DeviceIdType.MESH
ANY = pl.BlockSpec(memory_space=pl.ANY)

SHARD_AXIS = {
    "ssm_in_proj": 2, "ssm_conv_w": 2, "ssm_conv_b": 1, "ssm_norm": 1, "ssm_out_proj": 1,
    "w_kv": 1, "w_q": 1, "w_o": 1, "ffn_up": 2, "ffn_conv_w": 2, "ffn_down": 1,
    "ple_gate": 1, "ple_proj": 2,
}
REPLICATED = ["attn_norm", "ffn_norm", "ple_norm", "ssm_dt_bias", "ssm_a_log", "ssm_d",
              "kv_norm", "ffn_conv_b", "final_norm"]
WEIGHTS = ["attn_norm", "ffn_norm", "ple_norm", "ssm_in_proj", "ssm_conv_w", "ssm_conv_b",
           "ssm_dt_bias", "ssm_a_log", "ssm_d", "ssm_norm", "ssm_out_proj", "kv_norm", "w_kv",
           "w_q", "w_o", "ffn_up", "ffn_conv_w", "ffn_conv_b", "ffn_down", "ple_gate",
           "ple_proj", "final_norm"]
SHARDED = [n for n in WEIGHTS if n in SHARD_AXIS]
PACK_ROWS = 2048
SLAB_ROW_ALIGN = 16
BIG_WEIGHT = 1 << 17


def _tile(n, pref):
    t = (min(pref, n) // 128) * 128
    while t >= 128:
        if n % t == 0:
            return t
        t -= 128
    return n


def _dot(a, b):
    return jnp.dot(a, b, preferred_element_type=F32)


def _dot_nt(a, b):
    return lax.dot_general(a, b, (((1,), (1,)), ((), ())), preferred_element_type=F32)


def _dot_tn(a, b):
    return lax.dot_general(a, b, (((0,), (0,)), ((), ())), preferred_element_type=F32)


def _split2(x):
    hi = x.astype(BF16)
    lo = (x - hi.astype(F32)).astype(BF16)
    return hi, lo


def _dot2(x, m):
    hi, lo = _split2(x)
    return _dot(hi, m) + _dot(lo, m)


def _dot2_left(m, x):
    hi, lo = _split2(x)
    return _dot(m, hi) + _dot(m, lo)


def _softplus(x):
    return jnp.maximum(x, 0.0) + jnp.log(1.0 + jnp.exp(-jnp.abs(x)))


def _sigmoid(x):
    return 0.5 * jnp.tanh(0.5 * x) + 0.5


def _params(*sem):
    return pltpu.CompilerParams(dimension_semantics=sem)


MM_VMEM_BUDGET = 36 * 1024 * 1024
MM_FULL_K = 2816


def _mm_tiles(m, n, k, sa, sb, so, has_add, extra=0):
    tk = k if k <= MM_FULL_K else _tile(k, 1024)
    tn = _tile(n, 1408)
    tm = _tile(m, 1408)

    def need(tm_):
        return (2 * tm_ * tk * sa + 2 * tk * tn * sb + tm_ * tn * 4 + 2 * tm_ * tn * so
                + 2 * tm_ * tn * 4 * (extra + (1 if has_add else 0)))

    while need(tm) > MM_VMEM_BUDGET and tm % 256 == 0:
        tm //= 2
    return tm, tn, tk


def _mm(a, b, *, name, ta=False, tb=False, add=None, out_dtype=F32):
    m = a.shape[1] if ta else a.shape[0]
    k = a.shape[0] if ta else a.shape[1]
    n = b.shape[0] if tb else b.shape[1]
    assert (b.shape[1] if tb else b.shape[0]) == k, (a.shape, b.shape, ta, tb)
    tm, tn, tk = _mm_tiles(m, n, k, a.dtype.itemsize, b.dtype.itemsize,
                           jnp.dtype(out_dtype).itemsize, add is not None)
    nk = k // tk
    dims = (((0 if ta else 1,), (1 if tb else 0,)), ((), ()))
    has_add = add is not None

    def body(*refs):
        if has_add:
            a_ref, b_ref, add_ref, o_ref, acc_ref = refs
        else:
            a_ref, b_ref, o_ref, acc_ref = refs
        kk = pl.program_id(2)

        @pl.when(kk == 0)
        def _():
            acc_ref[...] = jnp.zeros_like(acc_ref)

        acc_ref[...] += lax.dot_general(a_ref[...].astype(BF16), b_ref[...].astype(BF16), dims,
                                        preferred_element_type=F32)

        @pl.when(kk == nk - 1)
        def _():
            r = acc_ref[...]
            if has_add:
                r = r + add_ref[...].astype(F32)
            o_ref[...] = r.astype(out_dtype)

    a_spec = (pl.BlockSpec((tk, tm), lambda i, j, kk: (kk, i)) if ta
              else pl.BlockSpec((tm, tk), lambda i, j, kk: (i, kk)))
    b_spec = (pl.BlockSpec((tn, tk), lambda i, j, kk: (j, kk)) if tb
              else pl.BlockSpec((tk, tn), lambda i, j, kk: (kk, j)))
    o_spec = pl.BlockSpec((tm, tn), lambda i, j, kk: (i, j))
    in_specs = [a_spec, b_spec] + ([o_spec] if has_add else [])
    args = (a, b) + ((add,) if has_add else ())
    return pl.pallas_call(
        body, name=name, grid=(m // tm, n // tn, nk), in_specs=in_specs, out_specs=o_spec,
        out_shape=jax.ShapeDtypeStruct((m, n), out_dtype),
        scratch_shapes=[pltpu.VMEM((tm, tn), F32)],
        compiler_params=_params("parallel", "parallel", "arbitrary"),
    )(*args)


def _mm_norm_bwd(a, b, x, gain, dres, *, name, add=None):
    m, k = a.shape
    n = b.shape[0]
    has_add = add is not None
    tm, tn, tk = _mm_tiles(m, n, k, a.dtype.itemsize, b.dtype.itemsize, 4, has_add, extra=2)
    assert tn == n, (tn, n)
    nk = k // tk

    def body(*refs):
        if has_add:
            a_ref, b_ref, add_ref, x_ref, g_ref, dres_ref, dx_ref, dg_ref, acc_ref = refs
        else:
            a_ref, b_ref, x_ref, g_ref, dres_ref, dx_ref, dg_ref, acc_ref = refs
        i, kk = pl.program_id(0), pl.program_id(1)

        @pl.when(kk == 0)
        def _():
            acc_ref[...] = jnp.zeros_like(acc_ref)

        acc_ref[...] += _dot_nt(a_ref[...].astype(BF16), b_ref[...].astype(BF16))

        @pl.when(kk == nk - 1)
        def _():
            dyv = acc_ref[...]
            if has_add:
                dyv = dyv + add_ref[...]
            xv = x_ref[...]
            r = lax.rsqrt(jnp.mean(xv * xv, axis=-1, keepdims=True) + NORM_EPS)
            xh = xv * r
            dxh = dyv * g_ref[...]
            dx_ref[...] = dres_ref[...] + r * (dxh - xh * jnp.mean(dxh * xh, axis=-1, keepdims=True))
            part = jnp.sum(dyv * xh, axis=0, keepdims=True)

            @pl.when(i == 0)
            def _():
                dg_ref[...] = part

            @pl.when(i > 0)
            def _():
                dg_ref[...] += part

    row = pl.BlockSpec((tm, n), lambda i, kk: (i, 0))
    vec = pl.BlockSpec((1, n), lambda i, kk: (0, 0))
    in_specs = ([pl.BlockSpec((tm, tk), lambda i, kk: (i, kk)), pl.BlockSpec((n, tk), lambda i, kk: (0, kk))]
                + ([row] if has_add else []) + [row, vec, row])
    args = (a, b) + ((add,) if has_add else ()) + (x, gain.reshape(1, n), dres)
    dx, dg = pl.pallas_call(
        body, name=name, grid=(m // tm, nk), in_specs=in_specs, out_specs=[row, vec],
        out_shape=[jax.ShapeDtypeStruct((m, n), F32), jax.ShapeDtypeStruct((1, n), F32)],
        scratch_shapes=[pltpu.VMEM((tm, n), F32)],
        compiler_params=_params("arbitrary", "arbitrary"),
    )(*args)
    return dx, dg.reshape(n)


def _rmsnorm_fwd(x, gain, *, name, rows=512):
    t, d = x.shape
    tr = _tile(t, rows)

    def body(x_ref, g_ref, o_ref):
        xv = x_ref[...]
        r = lax.rsqrt(jnp.mean(xv * xv, axis=-1, keepdims=True) + NORM_EPS)
        o_ref[...] = ((xv * r) * g_ref[...]).astype(BF16)

    return pl.pallas_call(
        body, name=name, grid=(t // tr,),
        in_specs=[pl.BlockSpec((tr, d), lambda i: (i, 0)), pl.BlockSpec((1, d), lambda i: (0, 0))],
        out_specs=pl.BlockSpec((tr, d), lambda i: (i, 0)),
        out_shape=jax.ShapeDtypeStruct((t, d), BF16),
        compiler_params=_params("parallel"),
    )(x, gain.reshape(1, d))


def _rmsnorm_bwd(x, gain, dy, dres, *, name, rows=512):
    t, d = x.shape
    tr = _tile(t, rows)

    def body(x_ref, g_ref, dy_ref, dres_ref, dx_ref, dg_ref):
        xv = x_ref[...]
        r = lax.rsqrt(jnp.mean(xv * xv, axis=-1, keepdims=True) + NORM_EPS)
        xh = xv * r
        dyv = dy_ref[...].astype(F32)
        dxh = dyv * g_ref[...]
        dx = r * (dxh - xh * jnp.mean(dxh * xh, axis=-1, keepdims=True))
        dx_ref[...] = dres_ref[...] + dx
        part = jnp.sum(dyv * xh, axis=0, keepdims=True)

        @pl.when(pl.program_id(0) == 0)
        def _():
            dg_ref[...] = part

        @pl.when(pl.program_id(0) > 0)
        def _():
            dg_ref[...] += part

    row = pl.BlockSpec((tr, d), lambda i: (i, 0))
    vec = pl.BlockSpec((1, d), lambda i: (0, 0))
    dx, dg = pl.pallas_call(
        body, name=name, grid=(t // tr,), in_specs=[row, vec, row, row], out_specs=[row, vec],
        out_shape=[jax.ShapeDtypeStruct((t, d), F32), jax.ShapeDtypeStruct((1, d), F32)],
        compiler_params=_params("arbitrary"),
    )(x, gain.reshape(1, d), dy, dres)
    return dx, dg.reshape(d)


def _final_loss(h, gain, target, *, rows=512):
    t, d = h.shape
    tr = _tile(t, rows)

    def body(x_ref, g_ref, tg_ref, dx_ref, dg_ref, loss_ref):
        xv = x_ref[...]
        g = g_ref[...]
        r = lax.rsqrt(jnp.mean(xv * xv, axis=-1, keepdims=True) + NORM_EPS)
        xh = xv * r
        err = xh * g - tg_ref[...]
        dyv = err * (1.0 / d)
        dxh = dyv * g
        dx_ref[...] = r * (dxh - xh * jnp.mean(dxh * xh, axis=-1, keepdims=True))
        part = jnp.sum(dyv * xh, axis=0, keepdims=True)
        lpart = jnp.zeros((1, LANES), F32) + (0.5 / d) * jnp.sum(err * err)

        @pl.when(pl.program_id(0) == 0)
        def _():
            dg_ref[...] = part
            loss_ref[...] = lpart

        @pl.when(pl.program_id(0) > 0)
        def _():
            dg_ref[...] += part
            loss_ref[...] += lpart

    row = pl.BlockSpec((tr, d), lambda i: (i, 0))
    vec = pl.BlockSpec((1, d), lambda i: (0, 0))
    dx, dg, loss = pl.pallas_call(
        body, name="final_loss", grid=(t // tr,), in_specs=[row, vec, row],
        out_specs=[row, vec, pl.BlockSpec((1, LANES), lambda i: (0, 0))],
        out_shape=[jax.ShapeDtypeStruct((t, d), F32), jax.ShapeDtypeStruct((1, d), F32),
                   jax.ShapeDtypeStruct((1, LANES), F32)],
        compiler_params=_params("arbitrary"),
    )(h, gain.reshape(1, d), target)
    return loss[0, 0], dx, dg.reshape(d)


def _ple_fwd(h, a, pp, *, name, rows=512):
    t, d = h.shape
    tr = _tile(t, rows)

    def body(h_ref, a_ref, p_ref, o_ref):
        o_ref[...] = h_ref[...] + _sigmoid(a_ref[...]) * p_ref[...]

    row = pl.BlockSpec((tr, d), lambda i: (i, 0))
    return pl.pallas_call(
        body, name=name, grid=(t // tr,), in_specs=[row, row, row], out_specs=row,
        out_shape=jax.ShapeDtypeStruct((t, d), F32), compiler_params=_params("parallel"),
    )(h, a, pp)


def _ple_bwd(dh, a, pp, *, name, rows=512):
    t, d = dh.shape
    tr = _tile(t, rows)

    def body(dh_ref, a_ref, p_ref, da_ref, dp_ref):
        s = _sigmoid(a_ref[...])
        dhv = dh_ref[...]
        da_ref[...] = (dhv * p_ref[...] * (s * (1.0 - s))).astype(BF16)
        dp_ref[...] = (dhv * s).astype(BF16)

    row = pl.BlockSpec((tr, d), lambda i: (i, 0))
    return pl.pallas_call(
        body, name=name, grid=(t // tr,), in_specs=[row, row, row], out_specs=[row, row],
        out_shape=[jax.ShapeDtypeStruct((t, d), BF16)] * 2, compiler_params=_params("parallel"),
    )(dh, a, pp)


CONV_ROWS = 64
CONV_HALO = 8


def _conv_window(ref, r0, with_prev, with_next):
    s = ref.shape[1]
    parts = []
    if with_prev:
        prev = ref[0, pl.ds(pl.multiple_of(jnp.maximum(r0 - CONV_HALO, 0), CONV_HALO), CONV_HALO), :]
        parts.append(jnp.where(r0 > 0, prev, 0.0))
    parts.append(ref[0, pl.ds(r0, CONV_ROWS), :])
    if with_next:
        nxt = pl.multiple_of(jnp.minimum(r0 + CONV_ROWS, s - CONV_HALO), CONV_HALO)
        parts.append(ref[0, pl.ds(nxt, CONV_HALO), :])
    return jnp.concatenate(parts, axis=0)


def _conv_taps(win, kw, n):
    return [win[CONV_HALO - (kw - 1 - k):CONV_HALO - (kw - 1 - k) + n] for k in range(kw)]


def _conv_apply(taps, wv, bv):
    pre = bv + wv[0:1, :] * taps[0]
    for k in range(1, len(taps)):
        pre = pre + wv[k:k + 1, :] * taps[k]
    return pre


def _rows8(x):
    acc = x[0:8]
    for i in range(1, x.shape[0] // 8):
        acc = acc + x[8 * i:8 * i + 8]
    return acc


def _conv_grad_step(dpre_ext, taps, wv, is_last):
    kw = wv.shape[0]
    halo = jnp.where(is_last, 0.0, dpre_ext[CONV_ROWS:])
    dpre_ext = jnp.concatenate([dpre_ext[:CONV_ROWS], halo], axis=0)
    dpre = dpre_ext[:CONV_ROWS]
    du = wv[kw - 1:kw, :] * dpre
    for k in range(kw - 1):
        du = du + wv[k:k + 1, :] * dpre_ext[kw - 1 - k:kw - 1 - k + CONV_ROWS]
    sums = [_rows8(dpre * taps[k][:CONV_ROWS]) for k in range(kw)] + [_rows8(dpre)]
    return du, sums


def _conv_store_sums(sums, dw_ref, db_ref, first):
    kw = len(sums) - 1
    vals = [jnp.sum(s_, axis=0, keepdims=True) for s_ in sums]

    @pl.when(first)
    def _():
        for k in range(kw):
            dw_ref[k:k + 1, :] = vals[k]
        db_ref[...] = vals[kw]

    @pl.when(jnp.logical_not(first))
    def _():
        for k in range(kw):
            dw_ref[k:k + 1, :] += vals[k]
        db_ref[...] += vals[kw]


def _dsilu(pre):
    s = _sigmoid(pre)
    return s, s * (1.0 + pre * (1.0 - s))


def _conv_silu_fwd(zx, off, w, b, *, name, tc=128):
    bsz, s, _ = zx.shape
    kw, c = w.shape
    o0 = off // tc

    def body(u_ref, w_ref, b_ref, o_ref):
        wv, bv = w_ref[...], b_ref[...]

        def step(i, carry):
            r0 = pl.multiple_of(i * CONV_ROWS, CONV_ROWS)
            taps = _conv_taps(_conv_window(u_ref, r0, True, False), kw, CONV_ROWS)
            pre = _conv_apply(taps, wv, bv)
            o_ref[0, pl.ds(r0, CONV_ROWS), :] = pre * _sigmoid(pre)
            return carry

        lax.fori_loop(0, s // CONV_ROWS, step, 0)

    return pl.pallas_call(
        body, name=name, grid=(bsz, c // tc),
        in_specs=[pl.BlockSpec((1, s, tc), lambda i, j: (i, 0, o0 + j)),
                  pl.BlockSpec((kw, tc), lambda i, j: (0, j)),
                  pl.BlockSpec((1, tc), lambda i, j: (0, j))],
        out_specs=pl.BlockSpec((1, s, tc), lambda i, j: (i, 0, j)),
        out_shape=jax.ShapeDtypeStruct((bsz, s, c), F32),
        compiler_params=_params("parallel", "parallel"),
    )(zx, w, b.reshape(1, c))


def _conv_silu_bwd(zx, off, w, b, douts, *, name, tc=128):
    bsz, s, _ = zx.shape
    kw, c = w.shape
    o0 = off // tc
    counts = [d.shape[2] // tc for d in douts]
    starts = [sum(counts[:k]) for k in range(len(douts))]
    assert sum(counts) == c // tc

    def body(u_ref, w_ref, b_ref, *rest):
        dy_refs = rest[:len(douts)]
        du_ref, dw_ref, db_ref = rest[len(douts):]
        j = pl.program_id(0)
        wv, bv = w_ref[...], b_ref[...]
        n = CONV_ROWS + CONV_HALO

        def step(i, sums):
            r0 = pl.multiple_of(i * CONV_ROWS, CONV_ROWS)
            taps = _conv_taps(_conv_window(u_ref, r0, True, True), kw, n)
            _, ds = _dsilu(_conv_apply(taps, wv, bv))
            dy = _conv_window(dy_refs[0], r0, False, True)
            for k in range(1, len(douts)):
                dy = jnp.where(j >= starts[k], _conv_window(dy_refs[k], r0, False, True), dy)
            du, new = _conv_grad_step(dy * ds, taps, wv, r0 + CONV_ROWS >= s)
            du_ref[0, pl.ds(r0, CONV_ROWS), :] = du.astype(BF16)
            return tuple(a + b_ for a, b_ in zip(sums, new))

        zero = tuple(jnp.zeros((8, tc), F32) for _ in range(kw + 1))
        sums = lax.fori_loop(0, s // CONV_ROWS, step, zero)
        _conv_store_sums(sums, dw_ref, db_ref, pl.program_id(1) == 0)

    def part_spec(k):
        return pl.BlockSpec((1, s, tc), lambda j, i: (i, 0, jnp.clip(j - starts[k], 0, counts[k] - 1)))

    du, dw, db = pl.pallas_call(
        body, name=name, grid=(c // tc, bsz),
        in_specs=[pl.BlockSpec((1, s, tc), lambda j, i: (i, 0, o0 + j)),
                  pl.BlockSpec((kw, tc), lambda j, i: (0, j)),
                  pl.BlockSpec((1, tc), lambda j, i: (0, j))] + [part_spec(k) for k in range(len(douts))],
        out_specs=[pl.BlockSpec((1, s, tc), lambda j, i: (i, 0, j)),
                   pl.BlockSpec((kw, tc), lambda j, i: (0, j)),
                   pl.BlockSpec((1, tc), lambda j, i: (0, j))],
        out_shape=[jax.ShapeDtypeStruct((bsz, s, c), BF16), jax.ShapeDtypeStruct((kw, c), F32),
                   jax.ShapeDtypeStruct((1, c), F32)],
        compiler_params=_params("parallel", "arbitrary"),
    )(zx, w, b.reshape(1, c), *douts)
    return du, dw, db.reshape(c)


def _conv_glu_fwd(up, w, b, *, name, tc=128):
    bsz, s, c2 = up.shape
    kw = w.shape[0]
    f = c2 // 2
    nt = f // tc

    def body(ug_ref, uv_ref, wg_ref, wv_ref, bg_ref, bv_ref, o_ref):
        wg, wv, bg, bv = wg_ref[...], wv_ref[...], bg_ref[...], bv_ref[...]

        def step(i, carry):
            r0 = pl.multiple_of(i * CONV_ROWS, CONV_ROWS)
            pg = _conv_apply(_conv_taps(_conv_window(ug_ref, r0, True, False), kw, CONV_ROWS), wg, bg)
            pv = _conv_apply(_conv_taps(_conv_window(uv_ref, r0, True, False), kw, CONV_ROWS), wv, bv)
            o_ref[0, pl.ds(r0, CONV_ROWS), :] = (pg * _sigmoid(pg) * pv).astype(BF16)
            return carry

        lax.fori_loop(0, s // CONV_ROWS, step, 0)

    b2 = b.reshape(1, c2)
    return pl.pallas_call(
        body, name=name, grid=(bsz, nt),
        in_specs=[pl.BlockSpec((1, s, tc), lambda i, j: (i, 0, j)),
                  pl.BlockSpec((1, s, tc), lambda i, j: (i, 0, nt + j)),
                  pl.BlockSpec((kw, tc), lambda i, j: (0, j)),
                  pl.BlockSpec((kw, tc), lambda i, j: (0, nt + j)),
                  pl.BlockSpec((1, tc), lambda i, j: (0, j)),
                  pl.BlockSpec((1, tc), lambda i, j: (0, nt + j))],
        out_specs=pl.BlockSpec((1, s, tc), lambda i, j: (i, 0, j)),
        out_shape=jax.ShapeDtypeStruct((bsz, s, f), BF16),
        compiler_params=_params("parallel", "parallel"),
    )(up, up, w, w, b2, b2)


def _conv_glu_bwd(up, w, b, df, *, name, tc=128):
    bsz, s, c2 = up.shape
    kw = w.shape[0]
    f = c2 // 2
    nt = f // tc

    def body(ug_ref, uv_ref, wg_ref, wv_ref, bg_ref, bv_ref, df_ref,
             dug_ref, duv_ref, dwg_ref, dwv_ref, dbg_ref, dbv_ref):
        first = pl.program_id(1) == 0
        wg, wv, bg, bv = wg_ref[...], wv_ref[...], bg_ref[...], bv_ref[...]
        n = CONV_ROWS + CONV_HALO

        def step(i, sums):
            r0 = pl.multiple_of(i * CONV_ROWS, CONV_ROWS)
            is_last = r0 + CONV_ROWS >= s
            tg = _conv_taps(_conv_window(ug_ref, r0, True, True), kw, n)
            tv = _conv_taps(_conv_window(uv_ref, r0, True, True), kw, n)
            pg = _conv_apply(tg, wg, bg)
            pv = _conv_apply(tv, wv, bv)
            sig, dsl = _dsilu(pg)
            dfv = _conv_window(df_ref, r0, False, True)
            dug, new_g = _conv_grad_step(dfv * pv * dsl, tg, wg, is_last)
            duv, new_v = _conv_grad_step(dfv * (pg * sig), tv, wv, is_last)
            dug_ref[0, pl.ds(r0, CONV_ROWS), :] = dug.astype(BF16)
            duv_ref[0, pl.ds(r0, CONV_ROWS), :] = duv.astype(BF16)
            return tuple(a + b_ for a, b_ in zip(sums, new_g + new_v))

        zero = tuple(jnp.zeros((8, tc), F32) for _ in range(2 * (kw + 1)))
        sums = lax.fori_loop(0, s // CONV_ROWS, step, zero)
        _conv_store_sums(sums[:kw + 1], dwg_ref, dbg_ref, first)
        _conv_store_sums(sums[kw + 1:], dwv_ref, dbv_ref, first)

    b2 = b.reshape(1, c2)
    act = lambda j, i: (i, 0, j)
    wsp = pl.BlockSpec((kw, tc), lambda j, i: (0, j))
    bsp = pl.BlockSpec((1, tc), lambda j, i: (0, j))
    dug, duv, dwg, dwv, dbg, dbv = pl.pallas_call(
        body, name=name, grid=(nt, bsz),
        in_specs=[pl.BlockSpec((1, s, tc), act),
                  pl.BlockSpec((1, s, tc), lambda j, i: (i, 0, nt + j)),
                  wsp, pl.BlockSpec((kw, tc), lambda j, i: (0, nt + j)),
                  bsp, pl.BlockSpec((1, tc), lambda j, i: (0, nt + j)),
                  pl.BlockSpec((1, s, tc), act)],
        out_specs=[pl.BlockSpec((1, s, tc), act), pl.BlockSpec((1, s, tc), act), wsp, wsp, bsp, bsp],
        out_shape=[jax.ShapeDtypeStruct((bsz, s, f), BF16)] * 2
        + [jax.ShapeDtypeStruct((kw, f), F32)] * 2 + [jax.ShapeDtypeStruct((1, f), F32)] * 2,
        compiler_params=_params("parallel", "arbitrary"),
    )(up, up, w, w, b2, b2, df)
    return (dug, duv, jnp.concatenate([dwg, dwv], axis=1),
            jnp.concatenate([dbg.reshape(f), dbv.reshape(f)]))


def _ssd_shared(xs, bm, cm, dtc_raw, dtr_raw, plane, psub, st):
    cl = SSM_CHUNK
    bias_l, a_l = plane[0:1, :], plane[1:2, :]
    bias_s, a_s = psub[:, 0:1], psub[:, 1:2]
    ri = lax.broadcasted_iota(I32, (cl, cl), 0)
    ci = lax.broadcasted_iota(I32, (cl, cl), 1)
    tril = ri >= ci
    low_incl = tril.astype(BF16)
    up_incl = (ri <= ci).astype(BF16)
    seg_t = (lax.broadcasted_iota(I32, (LANES, SSM_GROUP_W), 0)
             == lax.broadcasted_iota(I32, (LANES, SSM_GROUP_W), 1) // SSM_HEAD_DIM).astype(BF16)
    dt_c = _softplus(dtc_raw + bias_l)
    cs_c = _dot2_left(low_incl, dt_c * a_l)
    dt_r = _softplus(dtr_raw + bias_s)
    cs_r = _dot2(dt_r * a_s, up_incl)
    yield
    dt_ch = _dot2(dt_c, seg_t)
    cs_ch = _dot2(cs_c, seg_t)
    yield
    cs_last = cs_ch[cl - 1:cl, :]
    decay_ch = jnp.exp(cs_ch)
    w_ch = jnp.exp(cs_last - cs_ch)
    tot_ch = jnp.exp(cs_last)
    xdt = xs * dt_ch
    bm_b, cm_b = bm.astype(BF16), cm.astype(BF16)
    gmat = _dot_nt(cm_b, bm_b)
    cst = _dot(cm_b, st.astype(BF16))
    yield
    yoff = decay_ch * cst
    return dict(tril=tril, low_incl=low_incl, up_incl=up_incl, seg_t=seg_t, a_l=a_l, bias_l=bias_l,
                dt_c=dt_c, cs_c=cs_c, cs_r=cs_r, dt_ch=dt_ch, decay_ch=decay_ch, w_ch=w_ch,
                tot_ch=tot_ch, xdt=xdt, bm_b=bm_b, cm_b=cm_b, gmat=gmat, yoff=yoff)


def _head_decay(q, r):
    diff = q["cs_c"][:, r:r + 1] - q["cs_r"][r:r + 1, :]
    return jnp.where(q["tril"], jnp.exp(jnp.minimum(diff, 0.0)), 0.0)


def _half_mask(hh):
    lane = lax.broadcasted_iota(I32, (SSM_CHUNK, LANES), 1)
    return (lane < SSM_HEAD_DIM) if hh == 0 else (lane >= SSM_HEAD_DIM)


def _ssd_ydiag(q):
    pairs = []
    for pr in range(SSM_HEADS_PER_GROUP // 2):
        xp = q["xdt"][:, pr * LANES:(pr + 1) * LANES]
        acc = None
        for hh in range(2):
            mm_ = (q["gmat"] * _head_decay(q, 2 * pr + hh)).astype(BF16)
            part = _dot(mm_, jnp.where(_half_mask(hh), xp, 0.0).astype(BF16))
            acc = part if acc is None else acc + part
        pairs.append(acc)
        yield
    return jnp.concatenate(pairs, axis=1)


def _ssd_specs(bsz, s, g_n, d_inner, rev):
    cl = SSM_CHUNK
    nc = s // cl
    cc = (lambda c: nc - 1 - c) if rev else (lambda c: c)
    gb = d_inner // LANES
    dt0 = (d_inner + d_inner + 2 * g_n * SSM_STATE) // LANES
    gpb = 2 if all(v % 2 == 0 for v in (g_n, gb, dt0)) else 1
    gw = SSM_GROUP_W
    specs = dict(
        z=pl.BlockSpec((1, cl, gw * gpb), lambda b, g, c: (b, cc(c), g)),
        dtc=pl.BlockSpec((1, cl, LANES * gpb), lambda b, g, c: (b, cc(c), dt0 // gpb + g)),
        xs=pl.BlockSpec((1, cl, gw * gpb), lambda b, g, c: (b, cc(c), g)),
        bm=pl.BlockSpec((1, cl, LANES * gpb), lambda b, g, c: (b, cc(c), gb // gpb + g)),
        cm=pl.BlockSpec((1, cl, LANES * gpb), lambda b, g, c: (b, cc(c), (gb + g_n) // gpb + g)),
        dtr=pl.BlockSpec((1, gpb, SSM_ROWS, cl), lambda b, g, c: (b, g, 0, cc(c))),
        plane=pl.BlockSpec((gpb, 8, LANES), lambda b, g, c: (g, 0, 0)),
        psub=pl.BlockSpec((gpb, SSM_ROWS, LANES), lambda b, g, c: (g, 0, 0)),
        chan=pl.BlockSpec((1, gw * gpb), lambda b, g, c: (0, g)),
        state=pl.BlockSpec((1, gpb, 1, SSM_STATE, gw), lambda b, g, c: (b, g, cc(c), 0, 0)),
        bgrp=pl.BlockSpec((1, cl, LANES * gpb), lambda b, g, c: (b, cc(c), g)),
        acc_ch=pl.BlockSpec((1, gpb, 8, gw), lambda b, g, c: (b, g, 0, 0)),
        acc_ln=pl.BlockSpec((1, gpb, 8, LANES), lambda b, g, c: (b, g, 0, 0)),
    )
    lanes = lambda w: (lambda ref, gg: ref.at[:, :, pl.ds(gg * w, w)])
    second = lambda ref, gg: ref.at[:, pl.ds(gg, 1)]
    first = lambda ref, gg: ref.at[pl.ds(gg, 1)]
    views = dict(z=lanes(gw), xs=lanes(gw), dtc=lanes(LANES), bm=lanes(LANES), cm=lanes(LANES),
                 bgrp=lanes(LANES), dtr=second, state=second, acc_ch=second, acc_ln=second,
                 plane=first, psub=first, chan=lambda ref, gg: ref.at[:, pl.ds(gg * gw, gw)],
                 scratch=lambda ref, gg: ref.at[gg])
    return specs, views, gpb


def _per_group(body, names, views, gpb):
    def run(*refs):
        live = [body(*[views[nm](ref, gg) for nm, ref in zip(names, refs)]) for gg in range(gpb)]
        while live:
            still = []
            for gen in live:
                try:
                    next(gen)
                    still.append(gen)
                except StopIteration:
                    pass
            live = still
    return run


def _ssd_fwd(zx, xbc, dtr_row, plane, psub, d_ch, nw, *, name):
    bsz, s, _ = zx.shape
    d_inner = d_ch.shape[1]
    g_n = d_inner // SSM_GROUP_W
    nc = s // SSM_CHUNK
    sp, views, gpb = _ssd_specs(bsz, s, g_n, d_inner, False)
    names = ["z", "dtc", "xs", "bm", "cm", "dtr", "plane", "psub", "chan", "chan", "z", "state", "scratch"]

    def body(z_ref, dtc_ref, xs_ref, bm_ref, cm_ref, dtr_ref, plane_ref, psub_ref, d_ref, nw_ref,
             gn_ref, st_out_ref, st_ref):
        @pl.when(pl.program_id(2) == 0)
        def _():
            st_ref[...] = jnp.zeros_like(st_ref)

        xs = xs_ref[0]
        st = st_ref[...]
        st_out_ref[0, 0, 0] = st
        q = yield from _ssd_shared(xs, bm_ref[0], cm_ref[0], dtc_ref[0], dtr_ref[0, 0], plane_ref[0],
                                   psub_ref[0], st)
        y = (yield from _ssd_ydiag(q)) + q["yoff"] + xs * d_ref[...]
        st_ref[...] = q["tot_ch"] * st + _dot_tn(q["bm_b"], (q["w_ch"] * q["xdt"]).astype(BF16))
        zv = z_ref[0]
        gy = y * (zv * _sigmoid(zv))
        rstd = lax.rsqrt(jnp.mean(gy * gy, axis=-1, keepdims=True) + SSM_NORM_EPS)
        gn_ref[0] = ((gy * rstd) * nw_ref[...]).astype(BF16)

    return pl.pallas_call(
        _per_group(body, names, views, gpb), name=name, grid=(bsz, g_n // gpb, nc),
        in_specs=[sp["z"], sp["dtc"], sp["xs"], sp["bm"], sp["cm"], sp["dtr"], sp["plane"],
                  sp["psub"], sp["chan"], sp["chan"]],
        out_specs=[sp["z"], sp["state"]],
        out_shape=[jax.ShapeDtypeStruct((bsz, s, d_inner), BF16),
                   jax.ShapeDtypeStruct((bsz, g_n, nc, SSM_STATE, SSM_GROUP_W), F32)],
        scratch_shapes=[pltpu.VMEM((gpb, SSM_STATE, SSM_GROUP_W), F32)],
        compiler_params=_params("parallel", "parallel", "arbitrary"),
    )(zx, zx, xbc, xbc, xbc, dtr_row, plane, psub, d_ch, nw)


def _ssd_bwd(zx, xbc, dtr_row, plane, psub, d_ch, nw, states, dgn, *, name):
    bsz, s, _ = zx.shape
    d_inner = d_ch.shape[1]
    g_n = d_inner // SSM_GROUP_W
    cl = SSM_CHUNK
    nc = s // cl
    sp, views, gpb = _ssd_specs(bsz, s, g_n, d_inner, True)
    acc_ch, acc_ln = sp["acc_ch"], sp["acc_ln"]
    names = ["z", "dtc", "xs", "bm", "cm", "dtr", "plane", "psub", "chan", "chan", "state", "z",
             "z", "bgrp", "bgrp", "z", "bgrp", "acc_ch", "acc_ln", "scratch"]

    def body(z_ref, dtc_ref, xs_ref, bm_ref, cm_ref, dtr_ref, plane_ref, psub_ref, d_ref, nw_ref,
             st_in_ref, dgn_ref,
             dxs_ref, dbm_ref, dcm_ref, dz_ref, ddt_ref, ach_ref, aln_ref, dst_ref):
        first = pl.program_id(2) == 0

        @pl.when(first)
        def _():
            dst_ref[...] = jnp.zeros_like(dst_ref)
            ach_ref[...] = jnp.zeros_like(ach_ref)
            aln_ref[...] = jnp.zeros_like(aln_ref)

        xs = xs_ref[0]
        st = st_in_ref[0, 0, 0]
        q = yield from _ssd_shared(xs, bm_ref[0], cm_ref[0], dtc_ref[0], dtr_ref[0, 0], plane_ref[0],
                                   psub_ref[0], st)
        d_chv = d_ref[...]
        nwv = nw_ref[...]
        y = (yield from _ssd_ydiag(q)) + q["yoff"] + xs * d_chv
        zv = z_ref[0]
        sz = _sigmoid(zv)
        silu_z = zv * sz
        gy = y * silu_z
        rstd = lax.rsqrt(jnp.mean(gy * gy, axis=-1, keepdims=True) + SSM_NORM_EPS)
        gyh = gy * rstd
        dgnv = dgn_ref[0]
        dgyh = dgnv * nwv
        dgy = rstd * (dgyh - gyh * jnp.mean(dgyh * gyh, axis=-1, keepdims=True))
        dy = dgy * silu_z
        dz_ref[0] = (dgy * y * (sz * (1.0 + zv * (1.0 - sz)))).astype(BF16)
        ach_ref[0, 0, 0:1, :] += jnp.sum(dgnv * gyh, axis=0, keepdims=True)
        ach_ref[0, 0, 1:2, :] += jnp.sum(dy * xs, axis=0, keepdims=True)
        yield
        st_b = st.astype(BF16)
        dyd = (dy * q["decay_ch"]).astype(BF16)
        dcm = _dot_nt(dyd, st_b)
        dstn = dst_ref[...]
        dstn_b = dstn.astype(BF16)
        bds = _dot(q["bm_b"], dstn_b)
        wx = q["w_ch"] * q["xdt"]
        dbm = _dot_nt(wx.astype(BF16), dstn_b)
        dst_ref[...] = q["tot_ch"] * dstn + _dot_tn(q["cm_b"], dyd)
        vterm = wx * bds
        cs_terms = dy * q["yoff"] - vterm
        last_ch = q["tot_ch"] * jnp.sum(dstn * st, axis=0, keepdims=True) + jnp.sum(vterm, axis=0, keepdims=True)
        yield
        lane = lax.broadcasted_iota(I32, (cl, LANES), 1)
        rowi = lax.broadcasted_iota(I32, (SSM_ROWS, cl), 0)
        dg_sum = jnp.zeros((cl, cl), F32)
        dcs_col = jnp.zeros((cl, LANES), F32)
        dcs_row = jnp.zeros((SSM_ROWS, cl), F32)
        dxdt_pairs = []
        for pr in range(SSM_HEADS_PER_GROUP // 2):
            xp_b = q["xdt"][:, pr * LANES:(pr + 1) * LANES].astype(BF16)
            dyp = dy[:, pr * LANES:(pr + 1) * LANES]
            acc = None
            for hh in range(2):
                r = 2 * pr + hh
                dm = _head_decay(q, r)
                mmat = q["gmat"] * dm
                dym = jnp.where(_half_mask(hh), dyp, 0.0).astype(BF16)
                dmat = jnp.where(q["tril"], _dot_nt(dym, xp_b), 0.0)
                part = _dot_tn(mmat.astype(BF16), dym)
                acc = part if acc is None else acc + part
                dg_sum = dg_sum + dmat * dm
                e = dmat * mmat
                dcs_col = dcs_col + jnp.where(lane == r, jnp.sum(e, axis=1, keepdims=True), 0.0)
                dcs_row = dcs_row + jnp.where(rowi == r, jnp.sum(e, axis=0, keepdims=True), 0.0)
            dxdt_pairs.append(acc)
            yield
        dg_b = dg_sum.astype(BF16)
        dcm_ref[0] = dcm + _dot(dg_b, q["bm_b"])
        dbm_ref[0] = dbm + _dot_tn(dg_b, q["cm_b"])
        dxdt = q["w_ch"] * bds + jnp.concatenate(dxdt_pairs, axis=1)
        dxs_ref[0] = dy * d_chv + dxdt * q["dt_ch"]
        yield
        seg = (lax.broadcasted_iota(I32, (SSM_GROUP_W, LANES), 0) // SSM_HEAD_DIM
               == lax.broadcasted_iota(I32, (SSM_GROUP_W, LANES), 1)).astype(BF16)
        row_as_col = jnp.transpose(jnp.concatenate(
            [dcs_row, jnp.zeros((cl - SSM_ROWS, cl), F32)], axis=0))
        dcs = dcs_col - row_as_col + _dot2(cs_terms, seg)
        last = _dot2(jnp.zeros((8, SSM_GROUP_W), F32) + last_ch, seg)[0:1, :]
        da = _dot2_left(q["up_incl"], dcs) + last
        ddt = _dot2(dxdt * xs, seg) + da * q["a_l"]
        ddtr = ddt * _sigmoid(dtc_ref[0] + q["bias_l"])
        ddt_ref[0] = ddtr.astype(BF16)
        aln_ref[0, 0, 0:1, :] += jnp.sum(ddtr, axis=0, keepdims=True)
        aln_ref[0, 0, 1:2, :] += jnp.sum(da * q["dt_c"], axis=0, keepdims=True)

    outs = pl.pallas_call(
        _per_group(body, names, views, gpb), name=name, grid=(bsz, g_n // gpb, nc),
        in_specs=[sp["z"], sp["dtc"], sp["xs"], sp["bm"], sp["cm"], sp["dtr"], sp["plane"],
                  sp["psub"], sp["chan"], sp["chan"], sp["state"], sp["z"]],
        out_specs=[sp["z"], sp["bgrp"], sp["bgrp"], sp["z"], sp["bgrp"], acc_ch, acc_ln],
        out_shape=[jax.ShapeDtypeStruct((bsz, s, d_inner), F32),
                   jax.ShapeDtypeStruct((bsz, s, g_n * SSM_STATE), F32),
                   jax.ShapeDtypeStruct((bsz, s, g_n * SSM_STATE), F32),
                   jax.ShapeDtypeStruct((bsz, s, d_inner), BF16),
                   jax.ShapeDtypeStruct((bsz, s, g_n * LANES), BF16),
                   jax.ShapeDtypeStruct((bsz, g_n, 8, SSM_GROUP_W), F32),
                   jax.ShapeDtypeStruct((bsz, g_n, 8, LANES), F32)],
        scratch_shapes=[pltpu.VMEM((gpb, SSM_STATE, SSM_GROUP_W), F32)],
        compiler_params=_params("parallel", "parallel", "arbitrary"),
    )(zx, zx, xbc, xbc, xbc, dtr_row, plane, psub, d_ch, nw, states, dgn)
    return outs


def _sb_stack(x):
    out = []
    for i in range(x.shape[0] // SB_BLOCK):
        xb = x[i * SB_BLOCK:(i + 1) * SB_BLOCK]
        lane = lax.broadcasted_iota(I32, xb.shape, 1)
        zero = jnp.zeros_like(xb)
        out += [jnp.where(lane < SB_HEAD_DIM, xb, zero), jnp.where(lane >= SB_HEAD_DIM, xb, zero)]
    return jnp.concatenate(out, axis=0)


def _sb_unstack_t(acc_t):
    row = lax.broadcasted_iota(I32, (LANES, SB_BLOCK), 0)
    out = []
    for i in range(acc_t.shape[1] // (2 * SB_BLOCK)):
        a = acc_t[:, 2 * i * SB_BLOCK:(2 * i + 1) * SB_BLOCK]
        b = acc_t[:, (2 * i + 1) * SB_BLOCK:(2 * i + 2) * SB_BLOCK]
        out.append(jnp.transpose(jnp.where(row < SB_HEAD_DIM, a, b)))
    return jnp.concatenate(out, axis=0)


def _sb_tile_blocks(nq, q_blocks):
    nb = 4 if nq % 4 == 0 else (2 if nq % 2 == 0 else 1)
    return nb, min(nb, q_blocks)


def _sb_valid(u, qi0, nb, nqb):
    shape = (nb * SB_BLOCK, nqb * 2 * SB_BLOCK)
    key = u * (nb * SB_BLOCK) + lax.broadcasted_iota(I32, shape, 0)
    col = lax.broadcasted_iota(I32, shape, 1)
    qpos = (qi0 + col // (2 * SB_BLOCK)) * SB_BLOCK + col % SB_BLOCK
    return key < qpos


def _sb_logits(kb, qs, valid):
    z = _dot_nt(kb, qs)
    lb = jnp.minimum(z, 0.0) - jnp.log(1.0 + jnp.exp(-jnp.abs(z)))
    lk_all = lb - z
    lk = lk_all if valid is None else jnp.where(valid, lk_all, 0.0)
    return z, lb, lk_all, lk


def _sb_diag(x):
    w2 = 2 * SB_BLOCK
    ri = lax.broadcasted_iota(I32, (SB_BLOCK, w2), 0)
    ci = lax.broadcasted_iota(I32, (SB_BLOCK, w2), 1) % SB_BLOCK
    first = jnp.where(ri < ci, x[:, :w2], 0.0)
    return first if x.shape[1] == w2 else jnp.concatenate([first, x[:, w2:]], axis=1)


def _sb_add_from(full, part, lo):
    if lo == 0:
        return full + part
    return jnp.concatenate([full[:, :lo], full[:, lo:] + part], axis=1)


def _sb_scan(tri2, x, nb, reverse, exact=True):
    blk = SB_BLOCK
    edge = 0 if reverse else blk - 1
    carry = jnp.zeros((1, x.shape[1]), F32)
    res = [None] * nb
    for i in (reversed(range(nb)) if reverse else range(nb)):
        part = x[i * blk:(i + 1) * blk]
        if exact:
            hi, lo = _split2(part)
            raw = _dot(tri2, jnp.concatenate([hi, lo], axis=0))
        else:
            raw = _dot(tri2[:, :blk], part.astype(BF16))
        res[i] = raw + carry
        carry = carry + (raw[edge:edge + 1] + part[edge:edge + 1])
    return jnp.concatenate(res, axis=0), carry


def _sb_fwd(q, kv, kvt, *, name):
    bsz, s, w = q.shape
    blk = SB_BLOCK
    npair = w // LANES
    nq = s // blk
    nb, nqb = _sb_tile_blocks(nq, SB_Q_BLOCKS_FWD)
    width = nqb * 2 * blk

    def body(q_ref, k_ref, vt_ref, o_ref, tot_ref):
        qi0 = pl.program_id(2) * nqb
        qs = _sb_stack(q_ref[0] * SB_SCALE)
        ri = lax.broadcasted_iota(I32, (blk, blk), 0)
        ci = lax.broadcasted_iota(I32, (blk, blk), 1)
        upper = (ri < ci).astype(BF16)
        tri2 = jnp.concatenate([upper, upper], axis=1)

        def tile(u, r, acc, masked):
            rows = pl.ds(pl.multiple_of(u * (nb * blk), nb * blk), nb * blk)
            valid = _sb_valid(u, qi0, nb, nqb) if masked else None
            _, lb, _, lk = _sb_logits(k_ref[0, rows, :], qs, valid)
            sfx, total = _sb_scan(tri2, lk, nb, True)
            wgt = jnp.exp(lb + sfx + r)
            if masked:
                wgt = jnp.where(valid, wgt, 0.0)
            wb = wgt.astype(BF16)
            for i in range(nb):
                acc = acc + _dot(vt_ref[0, 0, u * nb + i], wb[i * blk:(i + 1) * blk])
            return r + total, acc

        def top_tile(u):
            r = jnp.zeros((1, width), F32)
            acc = jnp.zeros((LANES, width), F32)
            for kb in reversed(range(nb)):
                lo = kb * 2 * blk
                rows = pl.ds(pl.multiple_of((u * nb + kb) * blk, blk), blk)
                _, lb, lk_all, _ = _sb_logits(k_ref[0, rows, :], qs[lo:], None)
                lk = _sb_diag(lk_all)
                hi, lo_part = _split2(lk)
                raw = _dot(tri2, jnp.concatenate([hi, lo_part], axis=0))
                wgt = _sb_diag(jnp.exp(lb + raw + r[:, lo:]))
                add_acc = _dot(vt_ref[0, 0, u * nb + kb], wgt.astype(BF16))
                add_r = raw[0:1] + lk[0:1]
                acc = _sb_add_from(acc, add_acc, lo)
                r = _sb_add_from(r, add_r, lo)
            return r, acc

        top = qi0 // nb
        zero_r, zero_acc = jnp.zeros((1, width), F32), jnp.zeros((LANES, width), F32)
        r, acc = top_tile(top) if nb == nqb else tile(top, zero_r, zero_acc, True)
        r, acc = lax.fori_loop(0, top, lambda t, c: tile(top - 1 - t, c[0], c[1], False), (r, acc))
        o_ref[0] = _sb_unstack_t(acc).astype(BF16)
        tot_ref[0, 0, 0] = r

    qspec = pl.BlockSpec((1, nqb * blk, LANES), lambda b, p, i: (b, i, p))
    return pl.pallas_call(
        body, name=name, grid=(bsz, npair, nq // nqb),
        in_specs=[qspec,
                  pl.BlockSpec((1, s, LANES), lambda b, p, i: (b, 0, p)),
                  pl.BlockSpec((1, 1, nq, LANES, blk), lambda b, p, i: (b, npair + p, 0, 0, 0))],
        out_specs=[qspec, pl.BlockSpec((1, 1, 1, 1, width), lambda b, p, i: (b, p, i, 0, 0))],
        out_shape=[jax.ShapeDtypeStruct((bsz, s, w), BF16),
                   jax.ShapeDtypeStruct((bsz, npair, nq // nqb, 1, width), F32)],
        compiler_params=_params("parallel", "parallel", "arbitrary"),
    )(q, kv, kvt)


def _kv_blocks_t(kv3):
    bsz, s, w2 = kv3.shape
    x = kv3.reshape(bsz, s // SB_BLOCK, SB_BLOCK, w2 // LANES, LANES)
    return jnp.transpose(x, (0, 3, 1, 4, 2))


def _sb_bwd(q, kv, kvt, do, tot, dk_in, dv_in, *, name):
    bsz, s, w = q.shape
    blk = SB_BLOCK
    npair = w // LANES
    nq = s // blk
    nb, nqb = _sb_tile_blocks(nq, SB_Q_BLOCKS_BWD)
    width = nqb * 2 * blk
    tot = tot.reshape(bsz, npair, nq // nqb, 1, width)
    has_init = dk_in is not None

    def body(*refs):
        if has_init:
            q_ref, k_ref, v_ref, kt_ref, do_ref, tot_ref, dki_ref, dvi_ref, dq_ref, dk_ref, dv_ref = refs
        else:
            q_ref, k_ref, v_ref, kt_ref, do_ref, tot_ref, dq_ref, dk_ref, dv_ref = refs
        qi0 = pl.program_id(2) * nqb

        @pl.when(qi0 == 0)
        def _():
            if has_init:
                dk_ref[...] = dki_ref[...]
                dv_ref[...] = dvi_ref[...]
            else:
                dk_ref[...] = jnp.zeros_like(dk_ref)
                dv_ref[...] = jnp.zeros_like(dv_ref)

        qs = _sb_stack(q_ref[0] * SB_SCALE)
        dos = _sb_stack(do_ref[0])
        totv = tot_ref[0, 0, 0]
        ri = lax.broadcasted_iota(I32, (blk, blk), 0)
        ci = lax.broadcasted_iota(I32, (blk, blk), 1)
        lower = (ri > ci).astype(BF16)
        tri2 = jnp.concatenate([lower, lower], axis=1)

        def tile(u, pre_lk, pre_d, dqt, masked):
            rows = pl.ds(pl.multiple_of(u * (nb * blk), nb * blk), nb * blk)
            valid = _sb_valid(u, qi0, nb, nqb) if masked else None
            z, lb, lk_all, lk = _sb_logits(k_ref[0, rows, :], qs, valid)
            before, tot_lk = _sb_scan(tri2, lk, nb, False)
            wgt = jnp.exp(z + ((totv - pre_lk) - before))
            if masked:
                wgt = jnp.where(valid, wgt, 0.0)
            dlogit = _dot_nt(v_ref[0, rows, :], dos) * wgt
            dbefore, tot_d = _sb_scan(tri2, dlogit, nb, False, exact=False)
            sig = jnp.exp(lb)
            dz = dlogit * (1.0 - sig) - (pre_d + dbefore) * sig
            if masked:
                dz = jnp.where(valid, dz, 0.0)
            dz_b = dz.astype(BF16)
            for i in range(nb):
                dqt = dqt + _dot(kt_ref[0, 0, u * nb + i], dz_b[i * blk:(i + 1) * blk])
            dk_ref[0, rows, :] += _dot(dz_b, qs)
            dv_ref[0, rows, :] += _dot(wgt.astype(BF16), dos)
            return pre_lk + tot_lk, pre_d + tot_d, dqt

        def top_tile(u, pre_lk, pre_d, dqt):
            rest = totv - pre_lk
            for kb in range(nb):
                lo = kb * 2 * blk
                rows = pl.ds(pl.multiple_of((u * nb + kb) * blk, blk), blk)
                qs_k, dos_k = qs[lo:], dos[lo:]
                z, lb, lk_all, _ = _sb_logits(k_ref[0, rows, :], qs_k, None)
                lk = _sb_diag(lk_all)
                hi, lo_part = _split2(lk)
                raw = _dot(tri2, jnp.concatenate([hi, lo_part], axis=0))
                wgt = _sb_diag(jnp.exp(z + (rest[:, lo:] - raw)))
                dlogit = _dot_nt(v_ref[0, rows, :], dos_k) * wgt
                draw = _dot(tri2[:, :blk], dlogit.astype(BF16))
                sig = jnp.exp(lb)
                dz_b = _sb_diag(dlogit * (1.0 - sig) - (pre_d[:, lo:] + draw) * sig).astype(BF16)
                dqt = _sb_add_from(dqt, _dot(kt_ref[0, 0, u * nb + kb], dz_b), lo)
                dk_ref[0, rows, :] += _dot(dz_b, qs_k)
                dv_ref[0, rows, :] += _dot(wgt.astype(BF16), dos_k)
                rest = _sb_add_from(rest, -(raw[blk - 1:blk] + lk[blk - 1:blk]), lo)
                pre_d = _sb_add_from(pre_d, draw[blk - 1:blk] + dlogit[blk - 1:blk], lo)
            return dqt

        zero = jnp.zeros((1, width), F32)
        top = qi0 // nb
        c = lax.fori_loop(0, top, lambda u, c: tile(u, c[0], c[1], c[2], False),
                          (zero, zero, jnp.zeros((LANES, width), F32)))
        dqt = top_tile(top, *c) if nb == nqb else tile(top, c[0], c[1], c[2], True)[2]
        dq_ref[0] = (_sb_unstack_t(dqt) * SB_SCALE).astype(BF16)

    qspec = pl.BlockSpec((1, nqb * blk, LANES), lambda b, p, i: (b, i, p))
    kspec = pl.BlockSpec((1, s, LANES), lambda b, p, i: (b, 0, p))
    vspec = pl.BlockSpec((1, s, LANES), lambda b, p, i: (b, 0, npair + p))
    ktspec = pl.BlockSpec((1, 1, nq, LANES, blk), lambda b, p, i: (b, p, 0, 0, 0))
    tspec = pl.BlockSpec((1, 1, 1, 1, width), lambda b, p, i: (b, p, i, 0, 0))
    in_specs = [qspec, kspec, vspec, ktspec, qspec, tspec] + ([kspec, kspec] if has_init else [])
    args = (q, kv, kv, kvt, do, tot) + ((dk_in, dv_in) if has_init else ())
    return pl.pallas_call(
        body, name=name, grid=(bsz, npair, nq // nqb), in_specs=in_specs,
        out_specs=[qspec, kspec, kspec],
        out_shape=[jax.ShapeDtypeStruct((bsz, s, w), BF16), jax.ShapeDtypeStruct((bsz, s, w), F32),
                   jax.ShapeDtypeStruct((bsz, s, w), F32)],
        compiler_params=_params("parallel", "parallel", "arbitrary"),
    )(*args)


ADAM_BLOCK_BYTES = 1 << 20


def _adamw(w, g, m, v, *, name):
    shape = w.shape
    r, c = shape[-2], shape[-1]
    lead = _size(shape[:-2])
    tr = r
    for cand in range(8, r, 8):
        if r % cand == 0 and cand * c * 4 <= ADAM_BLOCK_BYTES:
            tr = cand
    if r * c * 4 <= ADAM_BLOCK_BYTES:
        tr = r

    def body(w_ref, g_ref, m_ref, v_ref, d_ref, mo_ref, vo_ref):
        gv = g_ref[...]
        mn = ADAM_B1 * m_ref[...] + (1.0 - ADAM_B1) * gv
        vn = ADAM_B2 * v_ref[...] + (1.0 - ADAM_B2) * (gv * gv)
        m_hat = mn / (1.0 - ADAM_B1 ** ADAM_STEP)
        v_hat = vn / (1.0 - ADAM_B2 ** ADAM_STEP)
        d_ref[...] = -ADAM_LR * (m_hat / (jnp.sqrt(v_hat) + ADAM_EPS) + ADAM_WD * w_ref[...])
        mo_ref[...] = mn
        vo_ref[...] = vn

    blk = pl.BlockSpec((1, tr, c), lambda l, i: (l, i, 0))
    outs = pl.pallas_call(
        body, name=name, grid=(lead, r // tr), in_specs=[blk] * 4, out_specs=[blk] * 3,
        out_shape=[jax.ShapeDtypeStruct((lead, r, c), F32)] * 3,
        compiler_params=_params("parallel", "parallel"),
    )(*[a.reshape(lead, r, c) for a in (w, g, m, v)])
    return [o.reshape(shape) for o in outs]


def _row_tile(r, c, itemsize):
    if r * c * 4 <= ADAM_BLOCK_BYTES:
        return r
    step = 32 // itemsize
    tr = r
    for cand in range(step, r, step):
        if r % cand == 0 and cand * c * 4 <= ADAM_BLOCK_BYTES:
            tr = cand
    return tr


def _add_own_half(idx, g, recv, *, name):
    _, lh, r, c = recv.shape
    tr = _row_tile(r, c, g.dtype.itemsize)

    def body(idx_ref, a_ref, b_ref, o_ref):
        o_ref[...] = (a_ref[...].astype(F32) + b_ref[...].astype(F32)).astype(o_ref.dtype)

    blk = pl.BlockSpec((1, 1, tr, c), lambda k, l, i, idx: (k, l, i, 0))
    return pl.pallas_call(
        body, name=name,
        grid_spec=pltpu.PrefetchScalarGridSpec(
            num_scalar_prefetch=1, grid=(N_CHIPS, lh, r // tr),
            in_specs=[pl.BlockSpec((1, 1, tr, c), lambda k, l, i, idx: (k, idx[0] * lh + l, i, 0)), blk],
            out_specs=blk),
        out_shape=jax.ShapeDtypeStruct(recv.shape, g.dtype),
        compiler_params=_params("parallel", "parallel", "parallel"),
    )(idx, g, recv)


def _add_chips(idx, own, recv, *, name):
    _, lh, r, c = own.shape
    tr = _row_tile(r, c, own.dtype.itemsize)

    def body(idx_ref, a_ref, b_ref, o_ref):
        f = lambda v: v.astype(F32)
        o_ref[0] = ((f(a_ref[0, 0]) + f(b_ref[0, 0])) + f(b_ref[1, 0])) + f(b_ref[2, 0])

    return pl.pallas_call(
        body, name=name,
        grid_spec=pltpu.PrefetchScalarGridSpec(
            num_scalar_prefetch=1, grid=(lh, r // tr),
            in_specs=[pl.BlockSpec((1, 1, tr, c), lambda l, i, idx: (idx[0], l, i, 0)),
                      pl.BlockSpec((3, 1, tr, c), lambda l, i, idx: (0, l, i, 0))],
            out_specs=pl.BlockSpec((1, tr, c), lambda l, i, idx: (l, i, 0))),
        out_shape=jax.ShapeDtypeStruct((lh, r, c), F32),
        compiler_params=_params("parallel", "parallel"),
    )(idx, own, recv)


def _sum_devices(parts):
    _, r, _ = parts.shape

    def body(p_ref, o_ref):
        acc = p_ref[0]
        for k in range(1, N_DEV):
            acc = acc + p_ref[k]
        o_ref[...] = acc

    return pl.pallas_call(
        body, name="small_grad_sum", grid=(1,),
        in_specs=[pl.BlockSpec((N_DEV, r, LANES), lambda i: (0, 0, 0))],
        out_specs=pl.BlockSpec((r, LANES), lambda i: (0, 0)),
        out_shape=jax.ShapeDtypeStruct((r, LANES), F32),
    )(parts)


def _place():
    return lax.axis_index("x"), lax.axis_index("y"), lax.axis_index("c")


def _rcopy(src, dst, send_sems, recv_sems, k, to):
    return pltpu.make_async_remote_copy(src_ref=src, dst_ref=dst, send_sem=send_sems.at[k],
                                        recv_sem=recv_sems.at[k], device_id=to, device_id_type=MESH)


def _exchange_call(body, name, ins, out_shapes, n_sems):
    return pl.pallas_call(
        body, name=name, in_specs=[ANY] * len(ins), out_specs=[ANY] * len(out_shapes),
        out_shape=out_shapes,
        scratch_shapes=[pltpu.SemaphoreType.DMA((n_sems,)), pltpu.SemaphoreType.DMA((n_sems,))],
    )(*ins)


def _gather_weights(shards):
    n = len(shards)

    def body(*refs):
        ins, outs, send_sems, recv_sems = refs[:n], refs[n:2 * n], refs[2 * n], refs[2 * n + 1]
        x, y, c = _place()
        sibling = (x, y, 1 - c)
        chips = [(1 - x, y), (x, 1 - y), (1 - x, 1 - y)]

        def piece(i, px, py, pc):
            lh = ins[i].shape[0] // 2
            return outs[i].at[2 * px + py, pl.ds(pc * lh, lh)]

        def mine(i):
            lh = ins[i].shape[0] // 2
            return ins[i].at[pl.ds(c * lh, lh)]

        first = [_rcopy(mine(i), piece(i, x, y, c), send_sems, recv_sems, 6 * i + j, (*chip, c))
                 for i in range(n) for j, chip in enumerate(chips)]
        for cp in first:
            cp.start()
        passed = []
        for i in range(n):
            for j, chip in enumerate(chips):
                landed = piece(i, *chip, c)
                _rcopy(landed, landed, send_sems, recv_sems, 6 * i + j, (*chip, c)).wait_recv()
                passed.append(_rcopy(landed, landed, send_sems, recv_sems, 6 * i + 3 + j, sibling))
                passed[-1].start()
        for i in range(n):
            for j, chip in enumerate(chips):
                theirs = piece(i, *chip, 1 - c)
                _rcopy(theirs, theirs, send_sems, recv_sems, 6 * i + 3 + j, sibling).wait_recv()
        for cp in first + passed:
            cp.wait_send()

    shapes = [jax.ShapeDtypeStruct((N_CHIPS,) + s_.shape, s_.dtype) for s_ in shards]
    return _exchange_call(body, "gather_weights", shards, shapes, 6 * n)


def _swap_halves(gs):
    n = len(gs)

    def body(*refs):
        ins, outs, send_sems, recv_sems = refs[:n], refs[n:2 * n], refs[2 * n], refs[2 * n + 1]
        x, y, c = _place()
        cps = []
        for i in range(n):
            lh = ins[i].shape[1] // 2
            src = ins[i].at[pl.ds(0, N_CHIPS), pl.ds((1 - c) * lh, lh)]
            cps.append(_rcopy(src, outs[i], send_sems, recv_sems, i, (x, y, 1 - c)))
        for cp in cps:
            cp.start()
        for cp in cps:
            cp.wait()

    shapes = [jax.ShapeDtypeStruct((N_CHIPS, g.shape[1] // 2) + g.shape[2:], g.dtype) for g in gs]
    return _exchange_call(body, "grad_swap_halves", gs, shapes, n)


def _scatter_chunks(sums):
    n = len(sums)

    def body(*refs):
        ins, outs, send_sems, recv_sems = refs[:n], refs[n:2 * n], refs[2 * n], refs[2 * n + 1]
        x, y, c = _place()
        chips = [(1 - x, y), (x, 1 - y), (1 - x, 1 - y)]
        cps = [_rcopy(ins[i].at[2 * chip[0] + chip[1]], outs[i].at[j], send_sems, recv_sems, 3 * i + j,
                      (*chip, c)) for i in range(n) for j, chip in enumerate(chips)]
        for cp in cps:
            cp.start()
        for cp in cps:
            cp.wait()

    shapes = [jax.ShapeDtypeStruct((3,) + s_.shape[1:], s_.dtype) for s_ in sums]
    return _exchange_call(body, "grad_scatter_chunks", sums, shapes, 3 * n)


def _share_half(tots):
    n = len(tots)

    def body(*refs):
        ins, outs, send_sems, recv_sems = refs[:n], refs[n:2 * n], refs[2 * n], refs[2 * n + 1]
        x, y, c = _place()
        cps = [_rcopy(ins[i], outs[i], send_sems, recv_sems, i, (x, y, 1 - c)) for i in range(n)]
        for cp in cps:
            cp.start()
        for cp in cps:
            cp.wait()

    shapes = [jax.ShapeDtypeStruct(t_.shape, t_.dtype) for t_ in tots]
    return _exchange_call(body, "grad_share_half", tots, shapes, n)


def _exchange_small(r):
    rr, _ = r.shape

    def body(r_ref, out_ref, send_sems, recv_sems, local_sem):
        x, y, c = _place()
        me = 4 * x + 2 * y + c
        mine = pltpu.make_async_copy(r_ref, out_ref.at[me], local_sem)
        mine.start()
        cps = []
        for k in range(N_DEV - 1):
            fx, fy, fc = ((k + 1) >> 2) & 1, ((k + 1) >> 1) & 1, (k + 1) & 1
            to = (x ^ fx, y ^ fy, c ^ fc)
            cps.append((_rcopy(r_ref, out_ref.at[me], send_sems, recv_sems, k, to), to))
        for cp, _ in cps:
            cp.start()
        for k, (cp, to) in enumerate(cps):
            src = 4 * to[0] + 2 * to[1] + to[2]
            _rcopy(r_ref, out_ref.at[src], send_sems, recv_sems, k, to).wait_recv()
        for cp, _ in cps:
            cp.wait_send()
        mine.wait()

    return pl.pallas_call(
        body, name="small_grad_exchange", in_specs=[ANY], out_specs=ANY,
        out_shape=jax.ShapeDtypeStruct((N_DEV, rr, LANES), r.dtype),
        scratch_shapes=[pltpu.SemaphoreType.DMA((N_DEV - 1,)), pltpu.SemaphoreType.DMA((N_DEV - 1,)),
                        pltpu.SemaphoreType.DMA],
    )(r)


def _size(shape):
    n = 1
    for d in shape:
        n *= d
    return n


def _slab_rows(shape):
    rows = -(-_size(shape) // LANES)
    return -(-rows // SLAB_ROW_ALIGN) * SLAB_ROW_ALIGN


def _pack_rows(arrs, dtype, lead=0, unit=PACK_ROWS):
    parts, total = [], 0
    for a in arrs:
        front, shp = a.shape[:lead], a.shape[lead:]
        n, rows = _size(shp), _slab_rows(shp)
        nopad = [(0, 0)] * lead
        if n % LANES == 0:
            p = a.reshape(front + (n // LANES, LANES)).astype(dtype)
        else:
            p = jnp.pad(a.reshape(front + (n,)).astype(dtype), nopad + [(0, rows * LANES - n)])
            p = p.reshape(front + (rows, LANES))
        if p.shape[lead] != rows:
            p = jnp.pad(p, nopad + [(0, rows - p.shape[lead]), (0, 0)])
        parts.append(p)
        total += rows
    pad = (-total) % unit
    if pad:
        parts.append(jnp.zeros(parts[0].shape[:lead] + (pad, LANES), dtype))
    return jnp.concatenate(parts, axis=lead)


def _unpack_rows(slab, shapes):
    lead = slab.shape[:-2]
    out, off = [], 0
    for shp in shapes:
        n, rows = _size(shp), _slab_rows(shp)
        piece = slab[..., off:off + rows, :]
        if n % LANES == 0:
            piece = piece[..., :n // LANES, :].reshape(lead + tuple(shp))
        else:
            piece = piece.reshape(lead + (rows * LANES,))[..., :n].reshape(lead + tuple(shp))
        out.append(piece)
        off += rows
    return out


def _ffn_fwd(h, bsz, s, gain, w_up, cw, cb, w_down, i):
    hf = _rmsnorm_fwd(h, gain, name=f"ffn_norm_{i}")
    up = _mm(hf, w_up, name=f"ffn_up_{i}")
    up3 = up.reshape(bsz, s, -1)
    f = _conv_glu_fwd(up3, cw, cb, name=f"ffn_glu_{i}").reshape(h.shape[0], -1)
    h2 = _mm(f, w_down, add=h, name=f"ffn_down_{i}")
    return h2, (h, hf, up3, f)


def _ffn_bwd(dh, saved, gain, w_up, cw, cb, w_down, i):
    h, hf, up3, f = saved
    t = h.shape[0]
    d_down = _mm(f, dh, ta=True, name=f"ffn_down_dw_{i}")
    df = _mm(dh, w_down, tb=True, name=f"ffn_down_dx_{i}")
    dug, duv, dcw, dcb = _conv_glu_bwd(up3, cw, cb, df.reshape(up3.shape[0], up3.shape[1], -1),
                                       name=f"ffn_glu_bwd_{i}")
    dug, duv = dug.reshape(t, -1), duv.reshape(t, -1)
    fdim = dug.shape[1]
    d_up = jnp.concatenate([_mm(hf, dug, ta=True, name=f"ffn_up_dwg_{i}"),
                            _mm(hf, duv, ta=True, name=f"ffn_up_dwv_{i}")], axis=1)
    dhf = _mm(dug, w_up[:, :fdim], tb=True, name=f"ffn_up_dxg_{i}")
    dh, dgain = _mm_norm_bwd(duv, w_up[:, fdim:], h, gain, dh, add=dhf, name=f"ffn_up_dxv_{i}")
    return dh, dgain, d_up, dcw, dcb, d_down


def _ple_layer_fwd(h, p_i, gain, w_gate, w_proj, i):
    hp = _rmsnorm_fwd(h, gain, name=f"ple_norm_{i}")
    a = _mm(hp, w_gate, name=f"ple_gate_{i}")
    pp = _mm(p_i, w_proj, name=f"ple_proj_{i}")
    return _ple_fwd(h, a, pp, name=f"ple_mix_{i}"), (h, hp, a, pp)


def _ple_layer_bwd(dh, saved, p_i, gain, w_gate, i):
    h, hp, a, pp = saved
    da, dpp = _ple_bwd(dh, a, pp, name=f"ple_mix_bwd_{i}")
    d_gate = _mm(hp, da, ta=True, name=f"ple_gate_dw_{i}")
    d_proj = _mm(p_i, dpp, ta=True, name=f"ple_proj_dw_{i}")
    dh, dgain = _mm_norm_bwd(da, w_gate, h, gain, dh, name=f"ple_gate_dx_{i}")
    return dh, dgain, d_gate, d_proj


def _ssm_consts(dt_bias, a_log, d_skip, g_n):
    hpg = SSM_HEADS_PER_GROUP
    a = -jnp.exp(a_log)
    rows = jnp.stack([dt_bias.reshape(g_n, hpg), a.reshape(g_n, hpg)], axis=1)
    plane = jnp.zeros((g_n, 8, LANES), F32).at[:, 0:2, 0:hpg].set(rows)
    psub = jnp.zeros((g_n, SSM_ROWS, LANES), F32).at[:, 0:hpg, 0:2].set(jnp.swapaxes(rows, 1, 2))
    d_ch = jnp.repeat(d_skip, SSM_HEAD_DIM).reshape(1, -1)
    return a, plane, psub, d_ch


def _ssm_in_big(w_in, d_inner, g_n):
    d = w_in.shape[0]
    cut = w_in.shape[1] - g_n * SSM_HEADS_PER_GROUP
    wdt = w_in[:, cut:].reshape(d, g_n, SSM_HEADS_PER_GROUP)
    wdt = jnp.pad(wdt, ((0, 0), (0, 0), (0, LANES - SSM_HEADS_PER_GROUP))).reshape(d, g_n * LANES)
    return jnp.concatenate([w_in[:, :cut], wdt], axis=1)


def _ssm_in_small(dw_big, g_n):
    d = dw_big.shape[0]
    cut = dw_big.shape[1] - g_n * LANES
    ddt = dw_big[:, cut:].reshape(d, g_n, LANES)[:, :, :SSM_HEADS_PER_GROUP].reshape(d, -1)
    return jnp.concatenate([dw_big[:, :cut], ddt], axis=1)


def _ssm_fwd(h, bsz, s, gain, w_in_big, cw, cb, plane, psub, d_ch, nw, w_out, i):
    d_inner = d_ch.shape[1]
    g_n = d_inner // SSM_GROUP_W
    conv_dim = cw.shape[1]
    hn = _rmsnorm_fwd(h, gain, name=f"attn_norm_{i}")
    zx = _mm(hn, w_in_big, name=f"ssm_in_{i}").reshape(bsz, s, -1)
    xbc = _conv_silu_fwd(zx, d_inner, cw, cb, name=f"ssm_conv_{i}")
    dtr = zx[:, :, d_inner + conv_dim:].reshape(bsz, s, g_n, LANES)[..., :SSM_HEADS_PER_GROUP]
    dtr_row = jnp.pad(jnp.transpose(dtr, (0, 2, 3, 1)),
                      ((0, 0), (0, 0), (0, SSM_ROWS - SSM_HEADS_PER_GROUP), (0, 0)))
    gn, states = _ssd_fwd(zx, xbc, dtr_row, plane, psub, d_ch, nw, name=f"ssd_{i}")
    gn2 = gn.reshape(h.shape[0], -1)
    h1 = _mm(gn2, w_out, add=h, name=f"ssm_out_{i}")
    return h1, (h, hn, zx, xbc, dtr_row, states, gn2)


def _ssm_bwd(dh, saved, gain, w_in_big, cw, cb, plane, psub, d_ch, nw, w_out, i):
    h, hn, zx, xbc, dtr_row, states, gn2 = saved
    t = h.shape[0]
    bsz, s, _ = zx.shape
    d_inner = d_ch.shape[1]
    d_out = _mm(gn2, dh, ta=True, name=f"ssm_out_dw_{i}")
    dgn = _mm(dh, w_out, tb=True, name=f"ssm_out_dx_{i}").reshape(bsz, s, -1)
    dxs, dbm, dcm, dz, ddtr, ach, aln = _ssd_bwd(zx, xbc, dtr_row, plane, psub, d_ch, nw, states, dgn,
                                                  name=f"ssd_bwd_{i}")
    dxbc, dcw, dcb = _conv_silu_bwd(zx, d_inner, cw, cb, [dxs, dbm, dcm], name=f"ssm_conv_bwd_{i}")
    d_in_parts, dhn, col = [], None, 0
    for tag, part in (("z", dz), ("dt", ddtr), ("xbc", dxbc)):
        part = part.reshape(t, -1)
        col = {"z": 0, "xbc": dz.shape[-1], "dt": dz.shape[-1] + dxbc.shape[-1]}[tag]
        w_part = w_in_big[:, col:col + part.shape[1]]
        d_in_parts.append(_mm(hn, part, ta=True, name=f"ssm_in_dw_{tag}_{i}"))
        if tag == "xbc":
            dh, dgain = _mm_norm_bwd(part, w_part, h, gain, dh, add=dhn, name=f"ssm_in_dx_{tag}_{i}")
        else:
            dhn = _mm(part, w_part, tb=True, add=dhn, name=f"ssm_in_dx_{tag}_{i}")
    d_in_big = jnp.concatenate([d_in_parts[0], d_in_parts[2], d_in_parts[1]], axis=1)
    hpg = SSM_HEADS_PER_GROUP
    ach = jnp.sum(ach, axis=0)
    aln = jnp.sum(aln, axis=0)
    d_nw = ach[:, 0, :].reshape(-1)
    d_dskip = jnp.sum(ach[:, 1, :].reshape(-1, SSM_HEAD_DIM), axis=1)
    d_bias = aln[:, 0, :hpg].reshape(-1)
    d_a = aln[:, 1, :hpg].reshape(-1)
    return dh, dgain, d_in_big, dcw, dcb, d_bias, d_a, d_dskip, d_nw, d_out


def _sb_layer_fwd(h, bsz, s, gain, w_q, w_o, kv3, kvt, i):
    hn = _rmsnorm_fwd(h, gain, name=f"attn_norm_{i}")
    q3 = _mm(hn, w_q, out_dtype=BF16, name=f"sb_q_{i}").reshape(bsz, s, -1)
    o3, tot = _sb_fwd(q3, kv3, kvt, name=f"sb_attn_{i}")
    o2 = o3.reshape(h.shape[0], -1)
    h1 = _mm(o2, w_o, add=h, name=f"sb_o_{i}")
    return h1, (h, hn, q3, o2, tot)


def _sb_layer_bwd(dh, saved, gain, w_q, w_o, kv3, kvt, dk, dv, i):
    h, hn, q3, o2, tot = saved
    t = h.shape[0]
    d_o = _mm(o2, dh, ta=True, name=f"sb_o_dw_{i}")
    do3 = _mm(dh, w_o, tb=True, out_dtype=BF16, name=f"sb_o_dx_{i}").reshape(q3.shape)
    dq3, dk, dv = _sb_bwd(q3, kv3, kvt, do3, tot, dk, dv, name=f"sb_attn_bwd_{i}")
    dq = dq3.reshape(t, -1)
    d_q = _mm(hn, dq, ta=True, name=f"sb_q_dw_{i}")
    dh, dgain = _mm_norm_bwd(dq, w_q, h, gain, dh, name=f"sb_q_dx_{i}")
    return dh, dgain, d_q, d_o, dk, dv


def kernel(x, p, attn_norm, ffn_norm, ple_norm, ssm_in_proj, ssm_conv_w, ssm_conv_b, ssm_dt_bias, ssm_a_log, ssm_d, ssm_norm, ssm_out_proj, kv_norm, w_kv, w_q, w_o, ffn_up, ffn_conv_w, ffn_conv_b, ffn_down, ple_gate, ple_proj, final_norm, loss_target, m_attn_norm, m_ffn_norm, m_ple_norm, m_ssm_in_proj, m_ssm_conv_w, m_ssm_conv_b, m_ssm_dt_bias, m_ssm_a_log, m_ssm_d, m_ssm_norm, m_ssm_out_proj, m_kv_norm, m_w_kv, m_w_q, m_w_o, m_ffn_up, m_ffn_conv_w, m_ffn_conv_b, m_ffn_down, m_ple_gate, m_ple_proj, m_final_norm, v_attn_norm, v_ffn_norm, v_ple_norm, v_ssm_in_proj, v_ssm_conv_w, v_ssm_conv_b, v_ssm_dt_bias, v_ssm_a_log, v_ssm_d, v_ssm_norm, v_ssm_out_proj, v_kv_norm, v_w_kv, v_w_q, v_w_o, v_ffn_up, v_ffn_conv_w, v_ffn_conv_b, v_ffn_down, v_ple_gate, v_ple_proj, v_final_norm):
    given = dict(locals())
    wl = {n: given[n] for n in WEIGHTS}
    bsz, s, d = x.shape
    t = bsz * s
    depth = attn_norm.shape[0]
    n_a = ssm_in_proj.shape[0]
    d_inner = ssm_norm.shape[1] * N_CHIPS
    g_n = d_inner // SSM_GROUP_W
    cidx = lax.axis_index("c").astype(I32).reshape(1)
    chip_idx = (2 * lax.axis_index("x") + lax.axis_index("y")).astype(I32).reshape(1)

    big = [n for n in SHARDED if _size(wl[n].shape) >= BIG_WEIGHT]
    small = [n for n in SHARDED if n not in big]
    small_shapes = [wl[n].shape for n in small]
    halves = lambda shp: shp if len(shp) == 3 else (2, shp[0] // 2, shp[1])
    small_slab = _pack_rows([wl[n] for n in small], BF16, unit=2 * SLAB_ROW_ALIGN)
    small_rows = small_slab.shape[0]
    mine = [wl[n].astype(BF16).reshape(halves(wl[n].shape)) for n in big]
    mine.append(small_slab.reshape(2, small_rows // 2, LANES))
    gathered = [lax.dynamic_update_index_in_dim(g, m_, chip_idx[0], 0)
                for g, m_ in zip(_gather_weights(mine), mine)]
    per_chip = {n: g.reshape((N_CHIPS,) + wl[n].shape) for n, g in zip(big, gathered)}
    per_chip.update(zip(small, _unpack_rows(gathered[-1].reshape(N_CHIPS, small_rows, LANES), small_shapes)))
    full = {}
    for n in SHARDED:
        ax, piece = SHARD_AXIS[n], per_chip[n]
        merged = piece.shape[1:ax + 1] + (N_CHIPS * piece.shape[ax + 1],) + piece.shape[ax + 2:]
        full[n] = jnp.moveaxis(piece, 0, ax).reshape(merged)

    h = x.reshape(t, d)
    tgt = loss_target.reshape(t, d)
    saved = []
    kv3 = kvt = hkv = h_kv_in = None
    consts = []
    for i in range(depth):
        if i < n_a:
            a_neg, plane, psub, d_ch = _ssm_consts(ssm_dt_bias[i], ssm_a_log[i], ssm_d[i], g_n)
            w_in_big = _ssm_in_big(full["ssm_in_proj"][i], d_inner, g_n)
            cw = full["ssm_conv_w"][i].astype(F32)
            cb = full["ssm_conv_b"][i].astype(F32)
            nw = full["ssm_norm"][i].astype(F32).reshape(1, -1)
            consts.append((a_neg, plane, psub, d_ch, w_in_big, cw, cb, nw))
            h, sv_mix = _ssm_fwd(h, bsz, s, attn_norm[i], w_in_big, cw, cb, plane, psub, d_ch, nw,
                                 full["ssm_out_proj"][i], i)
        else:
            j = i - n_a
            h, sv_mix = _sb_layer_fwd(h, bsz, s, attn_norm[i], full["w_q"][j], full["w_o"][j], kv3, kvt, i)
        fcw = full["ffn_conv_w"][i].astype(F32)
        h, sv_ffn = _ffn_fwd(h, bsz, s, ffn_norm[i], full["ffn_up"][i], fcw, ffn_conv_b[i],
                             full["ffn_down"][i], i)
        p_i = p[i].reshape(t, -1)
        h, sv_ple = _ple_layer_fwd(h, p_i, ple_norm[i], full["ple_gate"][i], full["ple_proj"][i], i)
        saved.append((sv_mix, sv_ffn, sv_ple))
        if i == n_a - 1:
            h_kv_in = h
            hkv = _rmsnorm_fwd(h, kv_norm, name="kv_norm")
            kv3 = _mm(hkv, full["w_kv"], out_dtype=BF16, name="kv_proj").reshape(bsz, s, -1)
            kvt = _kv_blocks_t(kv3)

    loss_local, dh, g_final = _final_loss(h, final_norm, tgt)
    gr = {n: [None] * wl[n].shape[0] for n in WEIGHTS if n not in ("kv_norm", "w_kv", "final_norm")}
    gr["final_norm"] = g_final
    dk = dv = None
    for i in reversed(range(depth)):
        sv_mix, sv_ffn, sv_ple = saved[i]
        if i == n_a - 1:
            dkv = jnp.concatenate([dk, dv], axis=-1).reshape(t, -1)
            gr["w_kv"] = _mm(hkv, dkv, ta=True, name="kv_proj_dw")
            dh, gr["kv_norm"] = _mm_norm_bwd(dkv, full["w_kv"], h_kv_in, kv_norm, dh, name="kv_proj_dx")
        p_i = p[i].reshape(t, -1)
        dh, gr["ple_norm"][i], gr["ple_gate"][i], gr["ple_proj"][i] = _ple_layer_bwd(
            dh, sv_ple, p_i, ple_norm[i], full["ple_gate"][i], i)
        fcw = full["ffn_conv_w"][i].astype(F32)
        (dh, gr["ffn_norm"][i], gr["ffn_up"][i], gr["ffn_conv_w"][i], gr["ffn_conv_b"][i],
         gr["ffn_down"][i]) = _ffn_bwd(dh, sv_ffn, ffn_norm[i], full["ffn_up"][i], fcw, ffn_conv_b[i],
                                       full["ffn_down"][i], i)
        if i < n_a:
            a_neg, plane, psub, d_ch, w_in_big, cw, cb, nw = consts[i]
            (dh, gr["attn_norm"][i], d_in_big, gr["ssm_conv_w"][i], gr["ssm_conv_b"][i],
             gr["ssm_dt_bias"][i], d_a, gr["ssm_d"][i], gr["ssm_norm"][i],
             gr["ssm_out_proj"][i]) = _ssm_bwd(dh, sv_mix, attn_norm[i], w_in_big, cw, cb, plane, psub,
                                               d_ch, nw, full["ssm_out_proj"][i], i)
            gr["ssm_in_proj"][i] = _ssm_in_small(d_in_big, g_n)
            gr["ssm_a_log"][i] = d_a * a_neg
        else:
            j = i - n_a
            dh, gr["attn_norm"][i], gr["w_q"][j], gr["w_o"][j], dk, dv = _sb_layer_bwd(
                dh, sv_mix, attn_norm[i], full["w_q"][j], full["w_o"][j], kv3, kvt, dk, dv, i)
    grad_x = dh.reshape(bsz, s, d)
    gfull = {n: (jnp.stack(v) if isinstance(v, list) else v) for n, v in gr.items()}

    by_chip = {}
    for n in SHARDED:
        ax, shp = SHARD_AXIS[n], gfull[n].shape
        split = gfull[n].reshape(shp[:ax] + (N_CHIPS, shp[ax] // N_CHIPS) + shp[ax + 1:])
        by_chip[n] = jnp.moveaxis(split, ax, 0)
    g4 = [by_chip[n].astype(BF16).reshape((N_CHIPS,) + halves(wl[n].shape)) for n in big]
    g4.append(_pack_rows([by_chip[n] for n in small], BF16, lead=1, unit=2 * SLAB_ROW_ALIGN)
              .reshape(N_CHIPS, 2, small_rows // 2, LANES))
    tags = big + ["small"]
    from_sibling = _swap_halves(g4)
    chip_sums = [_add_own_half(cidx, g, r_, name="grad_pair_sum_" + tg)
                 for g, r_, tg in zip(g4, from_sibling, tags)]
    from_chips = _scatter_chunks(chip_sums)
    my_half = [_add_chips(chip_idx, s_, r_, name="grad_chip_sum_" + tg)
               for s_, r_, tg in zip(chip_sums, from_chips, tags)]
    other_half = _share_half(my_half)
    low_core = cidx[0] == 0
    reduced = [jnp.concatenate([jnp.where(low_core, a, b_), jnp.where(low_core, b_, a)], axis=0)
               for a, b_ in zip(my_half, other_half)]
    grads = {n: g.reshape(wl[n].shape) for n, g in zip(big, reduced)}
    grads.update(zip(small, _unpack_rows(reduced[-1].reshape(small_rows, LANES), small_shapes)))

    rep_shapes = [wl[n].shape for n in REPLICATED]
    packed_r = _pack_rows([gfull[n] for n in REPLICATED], F32, unit=SLAB_ROW_ALIGN)
    g_rep = _sum_devices(_exchange_small(packed_r))

    delta, new_m, new_v = {}, {}, {}
    for n in SHARDED:
        delta[n], new_m[n], new_v[n] = _adamw(wl[n], grads[n], given["m_" + n], given["v_" + n],
                                              name="adamw_" + n)
    slabs = [_pack_rows([src[pre + n] for n in REPLICATED], F32, unit=SLAB_ROW_ALIGN)
             for src, pre in ((wl, ""), (given, "m_"), (given, "v_"))]
    rep_out = _adamw(slabs[0], g_rep, slabs[1], slabs[2], name="adamw_replicated")
    for dst, slab in zip((grads, delta, new_m, new_v), [g_rep] + list(rep_out)):
        dst.update(zip(REPLICATED, _unpack_rows(slab, rep_shapes)))
    loss = lax.psum(loss_local, ("x", "y", "c"))
    return (loss, grad_x, *[grads[n] for n in WEIGHTS], *[delta[n] for n in WEIGHTS],
            *[new_m[n] for n in WEIGHTS], *[new_v[n] for n in WEIGHTS])
```

```python
import functools

import jax
import jax.numpy as jnp
from jax import lax
from jax.experimental import pallas as pl
from jax.experimental.pallas import tpu as pltpu

F32 = jnp.float32
BF16 = jnp.bfloat16
I32 = jnp.int32

NORM_EPS = 1e-6
SSM_NORM_EPS = 1e-5
SSM_HEAD_DIM = 64
SSM_STATE = 128
SSM_CHUNK = 128
SSM_HEADS_PER_GROUP = 8
SSM_GROUP_W = SSM_HEADS_PER_GROUP * SSM_HEAD_DIM
SSM_CONV = 4
SSM_ROWS = 16
SB_HEAD_DIM = 64
SB_BLOCK = 128
SB_SCALE = SB_HEAD_DIM ** -0.5
SB_Q_BLOCKS_FWD = 4
SB_Q_BLOCKS_BWD = 4
FFN_CONV = 3
LANES = 128
N_CHIPS = 4
N_DEV = 8

ADAM_LR = 0.001
ADAM_B1 = 0.9
ADAM_B2 = 0.999
ADAM_EPS = 1e-08
ADAM_WD = 0.01
ADAM_STEP = 10

MESH = pl.DeviceIdType.MESH
ANY = pl.BlockSpec(memory_space=pl.ANY)

SHARD_AXIS = {
    "ssm_in_proj": 2, "ssm_conv_w": 2, "ssm_conv_b": 1, "ssm_norm": 1, "ssm_out_proj": 1,
    "w_kv": 1, "w_q": 1, "w_o": 1, "ffn_up": 2, "ffn_conv_w": 2, "ffn_down": 1,
    "ple_gate": 1, "ple_proj": 2,
}
REPLICATED = ["attn_norm", "ffn_norm", "ple_norm", "ssm_dt_bias", "ssm_a_log", "ssm_d",
              "kv_norm", "ffn_conv_b", "final_norm"]
WEIGHTS = ["attn_norm", "ffn_norm", "ple_norm", "ssm_in_proj", "ssm_conv_w", "ssm_conv_b",
           "ssm_dt_bias", "ssm_a_log", "ssm_d", "ssm_norm", "ssm_out_proj", "kv_norm", "w_kv",
           "w_q", "w_o", "ffn_up", "ffn_conv_w", "ffn_conv_b", "ffn_down", "ple_gate",
           "ple_proj", "final_norm"]
SHARDED = [n for n in WEIGHTS if n in SHARD_AXIS]
PACK_ROWS = 2048
SLAB_ROW_ALIGN = 16
BIG_WEIGHT = 1 << 17


def _tile(n, pref):
    t = (min(pref, n) // 128) * 128
    while t >= 128:
        if n % t == 0:
            return t
        t -= 128
    return n


def _dot(a, b):
    return jnp.dot(a, b, preferred_element_type=F32)


def _dot_nt(a, b):
    return lax.dot_general(a, b, (((1,), (1,)), ((), ())), preferred_element_type=F32)


def _dot_tn(a, b):
    return lax.dot_general(a, b, (((0,), (0,)), ((), ())), preferred_element_type=F32)


def _split2(x):
    hi = x.astype(BF16)
    lo = (x - hi.astype(F32)).astype(BF16)
    return hi, lo


def _dot2(x, m):
    hi, lo = _split2(x)
    return _dot(hi, m) + _dot(lo, m)


def _dot2_left(m, x):
    hi, lo = _split2(x)
    return _dot(m, hi) + _dot(m, lo)


def _softplus(x):
    return jnp.maximum(x, 0.0) + jnp.log(1.0 + jnp.exp(-jnp.abs(x)))


def _sigmoid(x):
    return 0.5 * jnp.tanh(0.5 * x) + 0.5


def _params(*sem):
    return pltpu.CompilerParams(dimension_semantics=sem)


def _round_robin(gens):
    live = list(gens)
    while live:
        still = []
        for gen in live:
            try:
                next(gen)
                still.append(gen)
            except StopIteration:
                pass
        live = still


MM_VMEM_BUDGET = 36 * 1024 * 1024
MM_FULL_K = 2816


def _mm_tiles(m, n, k, sa, sb, so, has_add, extra=0):
    tk = k if k <= MM_FULL_K else _tile(k, 1024)
    tn = _tile(n, 1408)
    tm = _tile(m, 1408)

    def need(tm_):
        return (2 * tm_ * tk * sa + 2 * tk * tn * sb + tm_ * tn * 4 + 2 * tm_ * tn * so
                + 2 * tm_ * tn * 4 * (extra + (1 if has_add else 0)))

    while need(tm) > MM_VMEM_BUDGET and tm % 256 == 0:
        tm //= 2
    return tm, tn, tk


def _mm(a, b, *, name, ta=False, tb=False, add=None, out_dtype=F32):
    m = a.shape[1] if ta else a.shape[0]
    k = a.shape[0] if ta else a.shape[1]
    n = b.shape[0] if tb else b.shape[1]
    assert (b.shape[1] if tb else b.shape[0]) == k, (a.shape, b.shape, ta, tb)
    tm, tn, tk = _mm_tiles(m, n, k, a.dtype.itemsize, b.dtype.itemsize,
                           jnp.dtype(out_dtype).itemsize, add is not None)
    nk = k // tk
    dims = (((0 if ta else 1,), (1 if tb else 0,)), ((), ()))
    has_add = add is not None

    def body(*refs):
        if has_add:
            a_ref, b_ref, add_ref, o_ref, acc_ref = refs
        else:
            a_ref, b_ref, o_ref, acc_ref = refs
        kk = pl.program_id(2)

        @pl.when(kk == 0)
        def _():
            acc_ref[...] = jnp.zeros_like(acc_ref)

        acc_ref[...] += lax.dot_general(a_ref[...].astype(BF16), b_ref[...].astype(BF16), dims,
                                        preferred_element_type=F32)

        @pl.when(kk == nk - 1)
        def _():
            r = acc_ref[...]
            if has_add:
                r = r + add_ref[...].astype(F32)
            o_ref[...] = r.astype(out_dtype)

    a_spec = (pl.BlockSpec((tk, tm), lambda i, j, kk: (kk, i)) if ta
              else pl.BlockSpec((tm, tk), lambda i, j, kk: (i, kk)))
    b_spec = (pl.BlockSpec((tn, tk), lambda i, j, kk: (j, kk)) if tb
              else pl.BlockSpec((tk, tn), lambda i, j, kk: (kk, j)))
    o_spec = pl.BlockSpec((tm, tn), lambda i, j, kk: (i, j))
    in_specs = [a_spec, b_spec] + ([o_spec] if has_add else [])
    args = (a, b) + ((add,) if has_add else ())
    return pl.pallas_call(
        body, name=name, grid=(m // tm, n // tn, nk), in_specs=in_specs, out_specs=o_spec,
        out_shape=jax.ShapeDtypeStruct((m, n), out_dtype),
        scratch_shapes=[pltpu.VMEM((tm, tn), F32)],
        compiler_params=_params("parallel", "parallel", "arbitrary"),
    )(*args)


def _mm_norm_bwd(a, b, x, gain, dres, *, name, add=None):
    m, k = a.shape
    n = b.shape[0]
    has_add = add is not None
    tm, tn, tk = _mm_tiles(m, n, k, a.dtype.itemsize, b.dtype.itemsize, 4, has_add, extra=2)
    assert tn == n, (tn, n)
    nk = k // tk

    def body(*refs):
        if has_add:
            a_ref, b_ref, add_ref, x_ref, g_ref, dres_ref, dx_ref, dg_ref, acc_ref = refs
        else:
            a_ref, b_ref, x_ref, g_ref, dres_ref, dx_ref, dg_ref, acc_ref = refs
        i, kk = pl.program_id(0), pl.program_id(1)

        @pl.when(kk == 0)
        def _():
            acc_ref[...] = jnp.zeros_like(acc_ref)

        acc_ref[...] += _dot_nt(a_ref[...].astype(BF16), b_ref[...].astype(BF16))

        @pl.when(kk == nk - 1)
        def _():
            dyv = acc_ref[...]
            if has_add:
                dyv = dyv + add_ref[...]
            xv = x_ref[...]
            r = lax.rsqrt(jnp.mean(xv * xv, axis=-1, keepdims=True) + NORM_EPS)
            xh = xv * r
            dxh = dyv * g_ref[...]
            dx_ref[...] = dres_ref[...] + r * (dxh - xh * jnp.mean(dxh * xh, axis=-1, keepdims=True))
            part = jnp.sum(dyv * xh, axis=0, keepdims=True)

            @pl.when(i == 0)
            def _():
                dg_ref[...] = part

            @pl.when(i > 0)
            def _():
                dg_ref[...] += part

    row = pl.BlockSpec((tm, n), lambda i, kk: (i, 0))
    vec = pl.BlockSpec((1, n), lambda i, kk: (0, 0))
    in_specs = ([pl.BlockSpec((tm, tk), lambda i, kk: (i, kk)), pl.BlockSpec((n, tk), lambda i, kk: (0, kk))]
                + ([row] if has_add else []) + [row, vec, row])
    args = (a, b) + ((add,) if has_add else ()) + (x, gain.reshape(1, n), dres)
    dx, dg = pl.pallas_call(
        body, name=name, grid=(m // tm, nk), in_specs=in_specs, out_specs=[row, vec],
        out_shape=[jax.ShapeDtypeStruct((m, n), F32), jax.ShapeDtypeStruct((1, n), F32)],
        scratch_shapes=[pltpu.VMEM((tm, n), F32)],
        compiler_params=_params("arbitrary", "arbitrary"),
    )(*args)
    return dx, dg.reshape(n)


def _rmsnorm_fwd(x, gain, *, name, rows=512):
    t, d = x.shape
    tr = _tile(t, rows)

    def body(x_ref, g_ref, o_ref):
        xv = x_ref[...]
        r = lax.rsqrt(jnp.mean(xv * xv, axis=-1, keepdims=True) + NORM_EPS)
        o_ref[...] = ((xv * r) * g_ref[...]).astype(BF16)

    return pl.pallas_call(
        body, name=name, grid=(t // tr,),
        in_specs=[pl.BlockSpec((tr, d), lambda i: (i, 0)), pl.BlockSpec((1, d), lambda i: (0, 0))],
        out_specs=pl.BlockSpec((tr, d), lambda i: (i, 0)),
        out_shape=jax.ShapeDtypeStruct((t, d), BF16),
        compiler_params=_params("parallel"),
    )(x, gain.reshape(1, d))


def _rmsnorm_bwd(x, gain, dy, dres, *, name, rows=512):
    t, d = x.shape
    tr = _tile(t, rows)

    def body(x_ref, g_ref, dy_ref, dres_ref, dx_ref, dg_ref):
        xv = x_ref[...]
        r = lax.rsqrt(jnp.mean(xv * xv, axis=-1, keepdims=True) + NORM_EPS)
        xh = xv * r
        dyv = dy_ref[...].astype(F32)
        dxh = dyv * g_ref[...]
        dx = r * (dxh - xh * jnp.mean(dxh * xh, axis=-1, keepdims=True))
        dx_ref[...] = dres_ref[...] + dx
        part = jnp.sum(dyv * xh, axis=0, keepdims=True)

        @pl.when(pl.program_id(0) == 0)
        def _():
            dg_ref[...] = part

        @pl.when(pl.program_id(0) > 0)
        def _():
            dg_ref[...] += part

    row = pl.BlockSpec((tr, d), lambda i: (i, 0))
    vec = pl.BlockSpec((1, d), lambda i: (0, 0))
    dx, dg = pl.pallas_call(
        body, name=name, grid=(t // tr,), in_specs=[row, vec, row, row], out_specs=[row, vec],
        out_shape=[jax.ShapeDtypeStruct((t, d), F32), jax.ShapeDtypeStruct((1, d), F32)],
        compiler_params=_params("arbitrary"),
    )(x, gain.reshape(1, d), dy, dres)
    return dx, dg.reshape(d)


def _final_loss(h, gain, target, *, rows=512):
    t, d = h.shape
    tr = _tile(t, rows)

    def body(x_ref, g_ref, tg_ref, dx_ref, dg_ref, loss_ref):
        xv = x_ref[...]
        g = g_ref[...]
        r = lax.rsqrt(jnp.mean(xv * xv, axis=-1, keepdims=True) + NORM_EPS)
        xh = xv * r
        err = xh * g - tg_ref[...]
        dyv = err * (1.0 / d)
        dxh = dyv * g
        dx_ref[...] = r * (dxh - xh * jnp.mean(dxh * xh, axis=-1, keepdims=True))
        part = jnp.sum(dyv * xh, axis=0, keepdims=True)
        lpart = jnp.zeros((1, LANES), F32) + (0.5 / d) * jnp.sum(err * err)

        @pl.when(pl.program_id(0) == 0)
        def _():
            dg_ref[...] = part
            loss_ref[...] = lpart

        @pl.when(pl.program_id(0) > 0)
        def _():
            dg_ref[...] += part
            loss_ref[...] += lpart

    row = pl.BlockSpec((tr, d), lambda i: (i, 0))
    vec = pl.BlockSpec((1, d), lambda i: (0, 0))
    dx, dg, loss = pl.pallas_call(
        body, name="final_loss", grid=(t // tr,), in_specs=[row, vec, row],
        out_specs=[row, vec, pl.BlockSpec((1, LANES), lambda i: (0, 0))],
        out_shape=[jax.ShapeDtypeStruct((t, d), F32), jax.ShapeDtypeStruct((1, d), F32),
                   jax.ShapeDtypeStruct((1, LANES), F32)],
        compiler_params=_params("arbitrary"),
    )(h, gain.reshape(1, d), target)
    return loss[0, 0], dx, dg.reshape(d)


def _ple_fwd(h, a, pp, *, name, rows=512):
    t, d = h.shape
    tr = _tile(t, rows)

    def body(h_ref, a_ref, p_ref, o_ref):
        o_ref[...] = h_ref[...] + _sigmoid(a_ref[...]) * p_ref[...]

    row = pl.BlockSpec((tr, d), lambda i: (i, 0))
    return pl.pallas_call(
        body, name=name, grid=(t // tr,), in_specs=[row, row, row], out_specs=row,
        out_shape=jax.ShapeDtypeStruct((t, d), F32), compiler_params=_params("parallel"),
    )(h, a, pp)


def _ple_bwd(dh, a, pp, *, name, rows=512):
    t, d = dh.shape
    tr = _tile(t, rows)

    def body(dh_ref, a_ref, p_ref, da_ref, dp_ref):
        s = _sigmoid(a_ref[...])
        dhv = dh_ref[...]
        da_ref[...] = (dhv * p_ref[...] * (s * (1.0 - s))).astype(BF16)
        dp_ref[...] = (dhv * s).astype(BF16)

    row = pl.BlockSpec((tr, d), lambda i: (i, 0))
    return pl.pallas_call(
        body, name=name, grid=(t // tr,), in_specs=[row, row, row], out_specs=[row, row],
        out_shape=[jax.ShapeDtypeStruct((t, d), BF16)] * 2, compiler_params=_params("parallel"),
    )(dh, a, pp)


CONV_ROWS = 64
CONV_HALO = 8


def _conv_window(ref, r0, with_prev, with_next):
    s = ref.shape[1]
    parts = []
    if with_prev:
        prev = ref[0, pl.ds(pl.multiple_of(jnp.maximum(r0 - CONV_HALO, 0), CONV_HALO), CONV_HALO), :]
        parts.append(jnp.where(r0 > 0, prev, 0.0))
    parts.append(ref[0, pl.ds(r0, CONV_ROWS), :])
    if with_next:
        nxt = pl.multiple_of(jnp.minimum(r0 + CONV_ROWS, s - CONV_HALO), CONV_HALO)
        parts.append(ref[0, pl.ds(nxt, CONV_HALO), :])
    return jnp.concatenate(parts, axis=0)


def _conv_taps(win, kw, n):
    return [win[CONV_HALO - (kw - 1 - k):CONV_HALO - (kw - 1 - k) + n] for k in range(kw)]


def _conv_apply(taps, wv, bv):
    pre = bv + wv[0:1, :] * taps[0]
    for k in range(1, len(taps)):
        pre = pre + wv[k:k + 1, :] * taps[k]
    return pre


def _rows8(x):
    acc = x[0:8]
    for i in range(1, x.shape[0] // 8):
        acc = acc + x[8 * i:8 * i + 8]
    return acc


def _conv_grad_step(dpre_ext, taps, wv, is_last):
    kw = wv.shape[0]
    halo = jnp.where(is_last, 0.0, dpre_ext[CONV_ROWS:])
    dpre_ext = jnp.concatenate([dpre_ext[:CONV_ROWS], halo], axis=0)
    dpre = dpre_ext[:CONV_ROWS]
    du = wv[kw - 1:kw, :] * dpre
    for k in range(kw - 1):
        du = du + wv[k:k + 1, :] * dpre_ext[kw - 1 - k:kw - 1 - k + CONV_ROWS]
    sums = [_rows8(dpre * taps[k][:CONV_ROWS]) for k in range(kw)] + [_rows8(dpre)]
    return du, sums


def _conv_store_sums(sums, dw_ref, db_ref, first):
    kw = len(sums) - 1
    vals = [jnp.sum(s_, axis=0, keepdims=True) for s_ in sums]

    @pl.when(first)
    def _():
        for k in range(kw):
            dw_ref[k:k + 1, :] = vals[k]
        db_ref[...] = vals[kw]

    @pl.when(jnp.logical_not(first))
    def _():
        for k in range(kw):
            dw_ref[k:k + 1, :] += vals[k]
        db_ref[...] += vals[kw]


def _dsilu(pre):
    s = _sigmoid(pre)
    return s, s * (1.0 + pre * (1.0 - s))


def _conv_silu_fwd(zx, off, w, b, *, name, tc=128):
    bsz, s, _ = zx.shape
    kw, c = w.shape
    o0 = off // tc

    def body(u_ref, w_ref, b_ref, o_ref):
        wv, bv = w_ref[...], b_ref[...]

        def step(i, carry):
            r0 = pl.multiple_of(i * CONV_ROWS, CONV_ROWS)
            taps = _conv_taps(_conv_window(u_ref, r0, True, False), kw, CONV_ROWS)
            pre = _conv_apply(taps, wv, bv)
            o_ref[0, pl.ds(r0, CONV_ROWS), :] = pre * _sigmoid(pre)
            return carry

        lax.fori_loop(0, s // CONV_ROWS, step, 0)

    return pl.pallas_call(
        body, name=name, grid=(bsz, c // tc),
        in_specs=[pl.BlockSpec((1, s, tc), lambda i, j: (i, 0, o0 + j)),
                  pl.BlockSpec((kw, tc), lambda i, j: (0, j)),
                  pl.BlockSpec((1, tc), lambda i, j: (0, j))],
        out_specs=pl.BlockSpec((1, s, tc), lambda i, j: (i, 0, j)),
        out_shape=jax.ShapeDtypeStruct((bsz, s, c), F32),
        compiler_params=_params("parallel", "parallel"),
    )(zx, w, b.reshape(1, c))


def _conv_silu_bwd(zx, off, w, b, douts, *, name, tc=128):
    bsz, s, _ = zx.shape
    kw, c = w.shape
    o0 = off // tc
    counts = [d.shape[2] // tc for d in douts]
    starts = [sum(counts[:k]) for k in range(len(douts))]
    assert sum(counts) == c // tc

    def body(u_ref, w_ref, b_ref, *rest):
        dy_refs = rest[:len(douts)]
        du_ref, dw_ref, db_ref = rest[len(douts):]
        j = pl.program_id(0)
        wv, bv = w_ref[...], b_ref[...]
        n = CONV_ROWS + CONV_HALO

        def step(i, sums):
            r0 = pl.multiple_of(i * CONV_ROWS, CONV_ROWS)
            taps = _conv_taps(_conv_window(u_ref, r0, True, True), kw, n)
            _, ds = _dsilu(_conv_apply(taps, wv, bv))
            dy = _conv_window(dy_refs[0], r0, False, True)
            for k in range(1, len(douts)):
                dy = jnp.where(j >= starts[k], _conv_window(dy_refs[k], r0, False, True), dy)
            du, new = _conv_grad_step(dy * ds, taps, wv, r0 + CONV_ROWS >= s)
            du_ref[0, pl.ds(r0, CONV_ROWS), :] = du.astype(BF16)
            return tuple(a + b_ for a, b_ in zip(sums, new))

        zero = tuple(jnp.zeros((8, tc), F32) for _ in range(kw + 1))
        sums = lax.fori_loop(0, s // CONV_ROWS, step, zero)
        _conv_store_sums(sums, dw_ref, db_ref, pl.program_id(1) == 0)

    def part_spec(k):
        return pl.BlockSpec((1, s, tc), lambda j, i: (i, 0, jnp.clip(j - starts[k], 0, counts[k] - 1)))

    du, dw, db = pl.pallas_call(
        body, name=name, grid=(c // tc, bsz),
        in_specs=[pl.BlockSpec((1, s, tc), lambda j, i: (i, 0, o0 + j)),
                  pl.BlockSpec((kw, tc), lambda j, i: (0, j)),
                  pl.BlockSpec((1, tc), lambda j, i: (0, j))] + [part_spec(k) for k in range(len(douts))],
        out_specs=[pl.BlockSpec((1, s, tc), lambda j, i: (i, 0, j)),
                   pl.BlockSpec((kw, tc), lambda j, i: (0, j)),
                   pl.BlockSpec((1, tc), lambda j, i: (0, j))],
        out_shape=[jax.ShapeDtypeStruct((bsz, s, c), BF16), jax.ShapeDtypeStruct((kw, c), F32),
                   jax.ShapeDtypeStruct((1, c), F32)],
        compiler_params=_params("parallel", "arbitrary"),
    )(zx, w, b.reshape(1, c), *douts)
    return du, dw, db.reshape(c)


def _conv_glu_fwd(up, w, b, *, name, tc=128):
    bsz, s, c2 = up.shape
    kw = w.shape[0]
    f = c2 // 2
    nt = f // tc

    def body(ug_ref, uv_ref, wg_ref, wv_ref, bg_ref, bv_ref, o_ref):
        wg, wv, bg, bv = wg_ref[...], wv_ref[...], bg_ref[...], bv_ref[...]

        def step(i, carry):
            r0 = pl.multiple_of(i * CONV_ROWS, CONV_ROWS)
            pg = _conv_apply(_conv_taps(_conv_window(ug_ref, r0, True, False), kw, CONV_ROWS), wg, bg)
            pv = _conv_apply(_conv_taps(_conv_window(uv_ref, r0, True, False), kw, CONV_ROWS), wv, bv)
            o_ref[0, pl.ds(r0, CONV_ROWS), :] = (pg * _sigmoid(pg) * pv).astype(BF16)
            return carry

        lax.fori_loop(0, s // CONV_ROWS, step, 0)

    b2 = b.reshape(1, c2)
    return pl.pallas_call(
        body, name=name, grid=(bsz, nt),
        in_specs=[pl.BlockSpec((1, s, tc), lambda i, j: (i, 0, j)),
                  pl.BlockSpec((1, s, tc), lambda i, j: (i, 0, nt + j)),
                  pl.BlockSpec((kw, tc), lambda i, j: (0, j)),
                  pl.BlockSpec((kw, tc), lambda i, j: (0, nt + j)),
                  pl.BlockSpec((1, tc), lambda i, j: (0, j)),
                  pl.BlockSpec((1, tc), lambda i, j: (0, nt + j))],
        out_specs=pl.BlockSpec((1, s, tc), lambda i, j: (i, 0, j)),
        out_shape=jax.ShapeDtypeStruct((bsz, s, f), BF16),
        compiler_params=_params("parallel", "parallel"),
    )(up, up, w, w, b2, b2)


def _conv_glu_bwd(up, w, b, df, *, name, tc=128):
    bsz, s, c2 = up.shape
    kw = w.shape[0]
    f = c2 // 2
    nt = f // tc

    def body(ug_ref, uv_ref, wg_ref, wv_ref, bg_ref, bv_ref, df_ref,
             dug_ref, duv_ref, dwg_ref, dwv_ref, dbg_ref, dbv_ref):
        first = pl.program_id(1) == 0
        wg, wv, bg, bv = wg_ref[...], wv_ref[...], bg_ref[...], bv_ref[...]
        n = CONV_ROWS + CONV_HALO

        def step(i, sums):
            r0 = pl.multiple_of(i * CONV_ROWS, CONV_ROWS)
            is_last = r0 + CONV_ROWS >= s
            tg = _conv_taps(_conv_window(ug_ref, r0, True, True), kw, n)
            tv = _conv_taps(_conv_window(uv_ref, r0, True, True), kw, n)
            pg = _conv_apply(tg, wg, bg)
            pv = _conv_apply(tv, wv, bv)
            sig, dsl = _dsilu(pg)
            dfv = _conv_window(df_ref, r0, False, True)
            dug, new_g = _conv_grad_step(dfv * pv * dsl, tg, wg, is_last)
            duv, new_v = _conv_grad_step(dfv * (pg * sig), tv, wv, is_last)
            dug_ref[0, pl.ds(r0, CONV_ROWS), :] = dug.astype(BF16)
            duv_ref[0, pl.ds(r0, CONV_ROWS), :] = duv.astype(BF16)
            return tuple(a + b_ for a, b_ in zip(sums, new_g + new_v))

        zero = tuple(jnp.zeros((8, tc), F32) for _ in range(2 * (kw + 1)))
        sums = lax.fori_loop(0, s // CONV_ROWS, step, zero)
        _conv_store_sums(sums[:kw + 1], dwg_ref, dbg_ref, first)
        _conv_store_sums(sums[kw + 1:], dwv_ref, dbv_ref, first)

    b2 = b.reshape(1, c2)
    act = lambda j, i: (i, 0, j)
    wsp = pl.BlockSpec((kw, tc), lambda j, i: (0, j))
    bsp = pl.BlockSpec((1, tc), lambda j, i: (0, j))
    dug, duv, dwg, dwv, dbg, dbv = pl.pallas_call(
        body, name=name, grid=(nt, bsz),
        in_specs=[pl.BlockSpec((1, s, tc), act),
                  pl.BlockSpec((1, s, tc), lambda j, i: (i, 0, nt + j)),
                  wsp, pl.BlockSpec((kw, tc), lambda j, i: (0, nt + j)),
                  bsp, pl.BlockSpec((1, tc), lambda j, i: (0, nt + j)),
                  pl.BlockSpec((1, s, tc), act)],
        out_specs=[pl.BlockSpec((1, s, tc), act), pl.BlockSpec((1, s, tc), act), wsp, wsp, bsp, bsp],
        out_shape=[jax.ShapeDtypeStruct((bsz, s, f), BF16)] * 2
        + [jax.ShapeDtypeStruct((kw, f), F32)] * 2 + [jax.ShapeDtypeStruct((1, f), F32)] * 2,
        compiler_params=_params("parallel", "arbitrary"),
    )(up, up, w, w, b2, b2, df)
    return (dug, duv, jnp.concatenate([dwg, dwv], axis=1),
            jnp.concatenate([dbg.reshape(f), dbv.reshape(f)]))


def _ssd_shared(xs, bm, cm, dtc_raw, dtr_raw, plane, psub, st):
    cl = SSM_CHUNK
    bias_l, a_l = plane[0:1, :], plane[1:2, :]
    bias_s, a_s = psub[:, 0:1], psub[:, 1:2]
    ri = lax.broadcasted_iota(I32, (cl, cl), 0)
    ci = lax.broadcasted_iota(I32, (cl, cl), 1)
    tril = ri >= ci
    low_incl = tril.astype(BF16)
    up_incl = (ri <= ci).astype(BF16)
    seg_t = (lax.broadcasted_iota(I32, (LANES, SSM_GROUP_W), 0)
             == lax.broadcasted_iota(I32, (LANES, SSM_GROUP_W), 1) // SSM_HEAD_DIM).astype(BF16)
    dt_c = _softplus(dtc_raw + bias_l)
    cs_c = _dot2_left(low_incl, dt_c * a_l)
    dt_r = _softplus(dtr_raw + bias_s)
    cs_r = _dot2(dt_r * a_s, up_incl)
    yield
    dt_ch = _dot2(dt_c, seg_t)
    cs_ch = _dot2(cs_c, seg_t)
    yield
    cs_last = cs_ch[cl - 1:cl, :]
    decay_ch = jnp.exp(cs_ch)
    w_ch = jnp.exp(cs_last - cs_ch)
    tot_ch = jnp.exp(cs_last)
    xdt = xs * dt_ch
    bm_b, cm_b = bm.astype(BF16), cm.astype(BF16)
    gmat = _dot_nt(cm_b, bm_b)
    cst = _dot(cm_b, st.astype(BF16))
    yield
    yoff = decay_ch * cst
    return dict(tril=tril, low_incl=low_incl, up_incl=up_incl, seg_t=seg_t, a_l=a_l, bias_l=bias_l,
                dt_c=dt_c, cs_c=cs_c, cs_r=cs_r, dt_ch=dt_ch, decay_ch=decay_ch, w_ch=w_ch,
                tot_ch=tot_ch, xdt=xdt, bm_b=bm_b, cm_b=cm_b, gmat=gmat, yoff=yoff)


def _head_decay(q, r):
    diff = q["cs_c"][:, r:r + 1] - q["cs_r"][r:r + 1, :]
    return jnp.where(q["tril"], jnp.exp(jnp.minimum(diff, 0.0)), 0.0)


def _half_mask(hh):
    lane = lax.broadcasted_iota(I32, (SSM_CHUNK, LANES), 1)
    return (lane < SSM_HEAD_DIM) if hh == 0 else (lane >= SSM_HEAD_DIM)


def _ssd_ydiag(q):
    pairs = []
    for pr in range(SSM_HEADS_PER_GROUP // 2):
        xp = q["xdt"][:, pr * LANES:(pr + 1) * LANES]
        acc = None
        for hh in range(2):
            mm_ = (q["gmat"] * _head_decay(q, 2 * pr + hh)).astype(BF16)
            part = _dot(mm_, jnp.where(_half_mask(hh), xp, 0.0).astype(BF16))
            acc = part if acc is None else acc + part
        pairs.append(acc)
        yield
    return jnp.concatenate(pairs, axis=1)


def _ssd_specs(bsz, s, g_n, d_inner, rev):
    cl = SSM_CHUNK
    nc = s // cl
    cc = (lambda c: nc - 1 - c) if rev else (lambda c: c)
    gb = d_inner // LANES
    dt0 = (d_inner + d_inner + 2 * g_n * SSM_STATE) // LANES
    gpb = 2 if all(v % 2 == 0 for v in (g_n, gb, dt0)) else 1
    gw = SSM_GROUP_W
    specs = dict(
        z=pl.BlockSpec((1, cl, gw * gpb), lambda b, g, c: (b, cc(c), g)),
        dtc=pl.BlockSpec((1, cl, LANES * gpb), lambda b, g, c: (b, cc(c), dt0 // gpb + g)),
        xs=pl.BlockSpec((1, cl, gw * gpb), lambda b, g, c: (b, cc(c), g)),
        bm=pl.BlockSpec((1, cl, LANES * gpb), lambda b, g, c: (b, cc(c), gb // gpb + g)),
        cm=pl.BlockSpec((1, cl, LANES * gpb), lambda b, g, c: (b, cc(c), (gb + g_n) // gpb + g)),
        dtr=pl.BlockSpec((1, gpb, SSM_ROWS, cl), lambda b, g, c: (b, g, 0, cc(c))),
        plane=pl.BlockSpec((gpb, 8, LANES), lambda b, g, c: (g, 0, 0)),
        psub=pl.BlockSpec((gpb, SSM_ROWS, LANES), lambda b, g, c: (g, 0, 0)),
        chan=pl.BlockSpec((1, gw * gpb), lambda b, g, c: (0, g)),
        state=pl.BlockSpec((1, gpb, 1, SSM_STATE, gw), lambda b, g, c: (b, g, cc(c), 0, 0)),
        bgrp=pl.BlockSpec((1, cl, LANES * gpb), lambda b, g, c: (b, cc(c), g)),
        acc_ch=pl.BlockSpec((1, gpb, 8, gw), lambda b, g, c: (b, g, 0, 0)),
        acc_ln=pl.BlockSpec((1, gpb, 8, LANES), lambda b, g, c: (b, g, 0, 0)),
    )
    lanes = lambda w: (lambda ref, gg: ref.at[:, :, pl.ds(gg * w, w)])
    second = lambda ref, gg: ref.at[:, pl.ds(gg, 1)]
    first = lambda ref, gg: ref.at[pl.ds(gg, 1)]
    views = dict(z=lanes(gw), xs=lanes(gw), dtc=lanes(LANES), bm=lanes(LANES), cm=lanes(LANES),
                 bgrp=lanes(LANES), dtr=second, state=second, acc_ch=second, acc_ln=second,
                 plane=first, psub=first, chan=lambda ref, gg: ref.at[:, pl.ds(gg * gw, gw)],
                 scratch=lambda ref, gg: ref.at[gg])
    return specs, views, gpb


def _per_group(body, names, views, gpb):
    def run(*refs):
        _round_robin([body(*[views[nm](ref, gg) for nm, ref in zip(names, refs)]) for gg in range(gpb)])
    return run


def _ssd_fwd(zx, xbc, dtr_row, plane, psub, d_ch, nw, *, name):
    bsz, s, _ = zx.shape
    d_inner = d_ch.shape[1]
    g_n = d_inner // SSM_GROUP_W
    nc = s // SSM_CHUNK
    sp, views, gpb = _ssd_specs(bsz, s, g_n, d_inner, False)
    names = ["z", "dtc", "xs", "bm", "cm", "dtr", "plane", "psub", "chan", "chan", "z", "state", "scratch"]

    def body(z_ref, dtc_ref, xs_ref, bm_ref, cm_ref, dtr_ref, plane_ref, psub_ref, d_ref, nw_ref,
             gn_ref, st_out_ref, st_ref):
        @pl.when(pl.program_id(2) == 0)
        def _():
            st_ref[...] = jnp.zeros_like(st_ref)

        xs = xs_ref[0]
        st = st_ref[...]
        st_out_ref[0, 0, 0] = st
        q = yield from _ssd_shared(xs, bm_ref[0], cm_ref[0], dtc_ref[0], dtr_ref[0, 0], plane_ref[0],
                                   psub_ref[0], st)
        y = (yield from _ssd_ydiag(q)) + q["yoff"] + xs * d_ref[...]
        st_ref[...] = q["tot_ch"] * st + _dot_tn(q["bm_b"], (q["w_ch"] * q["xdt"]).astype(BF16))
        zv = z_ref[0]
        gy = y * (zv * _sigmoid(zv))
        rstd = lax.rsqrt(jnp.mean(gy * gy, axis=-1, keepdims=True) + SSM_NORM_EPS)
        gn_ref[0] = ((gy * rstd) * nw_ref[...]).astype(BF16)

    return pl.pallas_call(
        _per_group(body, names, views, gpb), name=name, grid=(bsz, g_n // gpb, nc),
        in_specs=[sp["z"], sp["dtc"], sp["xs"], sp["bm"], sp["cm"], sp["dtr"], sp["plane"],
                  sp["psub"], sp["chan"], sp["chan"]],
        out_specs=[sp["z"], sp["state"]],
        out_shape=[jax.ShapeDtypeStruct((bsz, s, d_inner), BF16),
                   jax.ShapeDtypeStruct((bsz, g_n, nc, SSM_STATE, SSM_GROUP_W), F32)],
        scratch_shapes=[pltpu.VMEM((gpb, SSM_STATE, SSM_GROUP_W), F32)],
        compiler_params=_params("parallel", "parallel", "arbitrary"),
    )(zx, zx, xbc, xbc, xbc, dtr_row, plane, psub, d_ch, nw)


def _ssd_bwd(zx, xbc, dtr_row, plane, psub, d_ch, nw, states, dgn, *, name):
    bsz, s, _ = zx.shape
    d_inner = d_ch.shape[1]
    g_n = d_inner // SSM_GROUP_W
    cl = SSM_CHUNK
    nc = s // cl
    sp, views, gpb = _ssd_specs(bsz, s, g_n, d_inner, True)
    acc_ch, acc_ln = sp["acc_ch"], sp["acc_ln"]
    names = ["z", "dtc", "xs", "bm", "cm", "dtr", "plane", "psub", "chan", "chan", "state", "z",
             "z", "bgrp", "bgrp", "z", "bgrp", "acc_ch", "acc_ln", "scratch"]

    def body(z_ref, dtc_ref, xs_ref, bm_ref, cm_ref, dtr_ref, plane_ref, psub_ref, d_ref, nw_ref,
             st_in_ref, dgn_ref,
             dxs_ref, dbm_ref, dcm_ref, dz_ref, ddt_ref, ach_ref, aln_ref, dst_ref):
        first = pl.program_id(2) == 0

        @pl.when(first)
        def _():
            dst_ref[...] = jnp.zeros_like(dst_ref)
            ach_ref[...] = jnp.zeros_like(ach_ref)
            aln_ref[...] = jnp.zeros_like(aln_ref)

        xs = xs_ref[0]
        st = st_in_ref[0, 0, 0]
        q = yield from _ssd_shared(xs, bm_ref[0], cm_ref[0], dtc_ref[0], dtr_ref[0, 0], plane_ref[0],
                                   psub_ref[0], st)
        d_chv = d_ref[...]
        nwv = nw_ref[...]
        y = (yield from _ssd_ydiag(q)) + q["yoff"] + xs * d_chv
        zv = z_ref[0]
        sz = _sigmoid(zv)
        silu_z = zv * sz
        gy = y * silu_z
        rstd = lax.rsqrt(jnp.mean(gy * gy, axis=-1, keepdims=True) + SSM_NORM_EPS)
        gyh = gy * rstd
        dgnv = dgn_ref[0]
        dgyh = dgnv * nwv
        dgy = rstd * (dgyh - gyh * jnp.mean(dgyh * gyh, axis=-1, keepdims=True))
        dy = dgy * silu_z
        dz_ref[0] = (dgy * y * (sz * (1.0 + zv * (1.0 - sz)))).astype(BF16)
        ach_ref[0, 0, 0:1, :] += jnp.sum(dgnv * gyh, axis=0, keepdims=True)
        ach_ref[0, 0, 1:2, :] += jnp.sum(dy * xs, axis=0, keepdims=True)
        yield
        st_b = st.astype(BF16)
        dyd = (dy * q["decay_ch"]).astype(BF16)
        dcm = _dot_nt(dyd, st_b)
        dstn = dst_ref[...]
        dstn_b = dstn.astype(BF16)
        bds = _dot(q["bm_b"], dstn_b)
        wx = q["w_ch"] * q["xdt"]
        dbm = _dot_nt(wx.astype(BF16), dstn_b)
        dst_ref[...] = q["tot_ch"] * dstn + _dot_tn(q["cm_b"], dyd)
        vterm = wx * bds
        cs_terms = dy * q["yoff"] - vterm
        last_ch = q["tot_ch"] * jnp.sum(dstn * st, axis=0, keepdims=True) + jnp.sum(vterm, axis=0, keepdims=True)
        yield
        lane = lax.broadcasted_iota(I32, (cl, LANES), 1)
        rowi = lax.broadcasted_iota(I32, (SSM_ROWS, cl), 0)
        dg_sum = jnp.zeros((cl, cl), F32)
        dcs_col = jnp.zeros((cl, LANES), F32)
        dcs_row = jnp.zeros((SSM_ROWS, cl), F32)
        dxdt_pairs = []
        for pr in range(SSM_HEADS_PER_GROUP // 2):
            xp_b = q["xdt"][:, pr * LANES:(pr + 1) * LANES].astype(BF16)
            dyp = dy[:, pr * LANES:(pr + 1) * LANES]
            acc = None
            for hh in range(2):
                r = 2 * pr + hh
                dm = _head_decay(q, r)
                mmat = q["gmat"] * dm
                dym = jnp.where(_half_mask(hh), dyp, 0.0).astype(BF16)
                dmat = jnp.where(q["tril"], _dot_nt(dym, xp_b), 0.0)
                part = _dot_tn(mmat.astype(BF16), dym)
                acc = part if acc is None else acc + part
                dg_sum = dg_sum + dmat * dm
                e = dmat * mmat
                dcs_col = dcs_col + jnp.where(lane == r, jnp.sum(e, axis=1, keepdims=True), 0.0)
                dcs_row = dcs_row + jnp.where(rowi == r, jnp.sum(e, axis=0, keepdims=True), 0.0)
            dxdt_pairs.append(acc)
            yield
        dg_b = dg_sum.astype(BF16)
        dcm_ref[0] = dcm + _dot(dg_b, q["bm_b"])
        dbm_ref[0] = dbm + _dot_tn(dg_b, q["cm_b"])
        dxdt = q["w_ch"] * bds + jnp.concatenate(dxdt_pairs, axis=1)
        dxs_ref[0] = dy * d_chv + dxdt * q["dt_ch"]
        yield
        seg = (lax.broadcasted_iota(I32, (SSM_GROUP_W, LANES), 0) // SSM_HEAD_DIM
               == lax.broadcasted_iota(I32, (SSM_GROUP_W, LANES), 1)).astype(BF16)
        row_as_col = jnp.transpose(jnp.concatenate(
            [dcs_row, jnp.zeros((cl - SSM_ROWS, cl), F32)], axis=0))
        dcs = dcs_col - row_as_col + _dot2(cs_terms, seg)
        last = _dot2(jnp.zeros((8, SSM_GROUP_W), F32) + last_ch, seg)[0:1, :]
        da = _dot2_left(q["up_incl"], dcs) + last
        ddt = _dot2(dxdt * xs, seg) + da * q["a_l"]
        ddtr = ddt * _sigmoid(dtc_ref[0] + q["bias_l"])
        ddt_ref[0] = ddtr.astype(BF16)
        aln_ref[0, 0, 0:1, :] += jnp.sum(ddtr, axis=0, keepdims=True)
        aln_ref[0, 0, 1:2, :] += jnp.sum(da * q["dt_c"], axis=0, keepdims=True)

    outs = pl.pallas_call(
        _per_group(body, names, views, gpb), name=name, grid=(bsz, g_n // gpb, nc),
        in_specs=[sp["z"], sp["dtc"], sp["xs"], sp["bm"], sp["cm"], sp["dtr"], sp["plane"],
                  sp["psub"], sp["chan"], sp["chan"], sp["state"], sp["z"]],
        out_specs=[sp["z"], sp["bgrp"], sp["bgrp"], sp["z"], sp["bgrp"], acc_ch, acc_ln],
        out_shape=[jax.ShapeDtypeStruct((bsz, s, d_inner), F32),
                   jax.ShapeDtypeStruct((bsz, s, g_n * SSM_STATE), F32),
                   jax.ShapeDtypeStruct((bsz, s, g_n * SSM_STATE), F32),
                   jax.ShapeDtypeStruct((bsz, s, d_inner), BF16),
                   jax.ShapeDtypeStruct((bsz, s, g_n * LANES), BF16),
                   jax.ShapeDtypeStruct((bsz, g_n, 8, SSM_GROUP_W), F32),
                   jax.ShapeDtypeStruct((bsz, g_n, 8, LANES), F32)],
        scratch_shapes=[pltpu.VMEM((gpb, SSM_STATE, SSM_GROUP_W), F32)],
        compiler_params=_params("parallel", "parallel", "arbitrary"),
    )(zx, zx, xbc, xbc, xbc, dtr_row, plane, psub, d_ch, nw, states, dgn)
    return outs


def _sb_stack(x):
    out = []
    for i in range(x.shape[0] // SB_BLOCK):
        xb = x[i * SB_BLOCK:(i + 1) * SB_BLOCK]
        lane = lax.broadcasted_iota(I32, xb.shape, 1)
        zero = jnp.zeros_like(xb)
        out += [jnp.where(lane < SB_HEAD_DIM, xb, zero), jnp.where(lane >= SB_HEAD_DIM, xb, zero)]
    return jnp.concatenate(out, axis=0)


def _sb_unstack_t(acc_t):
    row = lax.broadcasted_iota(I32, (LANES, SB_BLOCK), 0)
    out = []
    for i in range(acc_t.shape[1] // (2 * SB_BLOCK)):
        a = acc_t[:, 2 * i * SB_BLOCK:(2 * i + 1) * SB_BLOCK]
        b = acc_t[:, (2 * i + 1) * SB_BLOCK:(2 * i + 2) * SB_BLOCK]
        out.append(jnp.transpose(jnp.where(row < SB_HEAD_DIM, a, b)))
    return jnp.concatenate(out, axis=0)


def _sb_tile_blocks(nq, q_blocks):
    nb = 4 if nq % 4 == 0 else (2 if nq % 2 == 0 else 1)
    return nb, min(nb, q_blocks)


def _sb_valid(u, qi0, nb, nqb):
    shape = (nb * SB_BLOCK, nqb * 2 * SB_BLOCK)
    key = u * (nb * SB_BLOCK) + lax.broadcasted_iota(I32, shape, 0)
    col = lax.broadcasted_iota(I32, shape, 1)
    qpos = (qi0 + col // (2 * SB_BLOCK)) * SB_BLOCK + col % SB_BLOCK
    return key < qpos


def _sb_logits(kb, qs, valid):
    z = _dot_nt(kb, qs)
    lb = jnp.minimum(z, 0.0) - jnp.log(1.0 + jnp.exp(-jnp.abs(z)))
    lk_all = lb - z
    lk = lk_all if valid is None else jnp.where(valid, lk_all, 0.0)
    return z, lb, lk_all, lk


def _sb_diag(x):
    w2 = 2 * SB_BLOCK
    ri = lax.broadcasted_iota(I32, (SB_BLOCK, w2), 0)
    ci = lax.broadcasted_iota(I32, (SB_BLOCK, w2), 1) % SB_BLOCK
    first = jnp.where(ri < ci, x[:, :w2], 0.0)
    return first if x.shape[1] == w2 else jnp.concatenate([first, x[:, w2:]], axis=1)


def _sb_add_from(full, part, lo):
    if lo == 0:
        return full + part
    return jnp.concatenate([full[:, :lo], full[:, lo:] + part], axis=1)


def _sb_scan(tri2, x, nb, reverse, exact=True):
    blk = SB_BLOCK
    edge = 0 if reverse else blk - 1
    carry = jnp.zeros((1, x.shape[1]), F32)
    res = [None] * nb
    for i in (reversed(range(nb)) if reverse else range(nb)):
        part = x[i * blk:(i + 1) * blk]
        if exact:
            hi, lo = _split2(part)
            raw = _dot(tri2, jnp.concatenate([hi, lo], axis=0))
        else:
            raw = _dot(tri2[:, :blk], part.astype(BF16))
        res[i] = raw + carry
        carry = carry + (raw[edge:edge + 1] + part[edge:edge + 1])
    return jnp.concatenate(res, axis=0), carry


def _sb_fwd(q, kv, kvt, *, name):
    bsz, s, w = q.shape
    blk = SB_BLOCK
    npair = w // LANES
    nq = s // blk
    nb, nqb = _sb_tile_blocks(nq, SB_Q_BLOCKS_FWD)
    width = nqb * 2 * blk

    def body(q_ref, k_ref, vt_ref, o_ref, tot_ref):
        qi0 = pl.program_id(2) * nqb
        qs = _sb_stack(q_ref[0] * SB_SCALE)
        ri = lax.broadcasted_iota(I32, (blk, blk), 0)
        ci = lax.broadcasted_iota(I32, (blk, blk), 1)
        upper = (ri < ci).astype(BF16)
        tri2 = jnp.concatenate([upper, upper], axis=1)

        def tile_phases(u, carry, masked):
            rows = pl.ds(pl.multiple_of(u * (nb * blk), nb * blk), nb * blk)
            valid = _sb_valid(u, qi0, nb, nqb) if masked else None
            _, lb, _, lk = _sb_logits(k_ref[0, rows, :], qs, valid)
            yield
            sfx, total = _sb_scan(tri2, lk, nb, True)
            yield
            wgt = jnp.exp(lb + sfx + carry["r"])
            if masked:
                wgt = jnp.where(valid, wgt, 0.0)
            carry["r"] = carry["r"] + total
            wb = wgt.astype(BF16)
            yield
            for i in range(nb):
                carry["acc"] = carry["acc"] + _dot(vt_ref[0, 0, u * nb + i], wb[i * blk:(i + 1) * blk])

        def tile(us, r, acc, masked):
            carry = {"r": r, "acc": acc}
            _round_robin([tile_phases(u, carry, masked) for u in us])
            return carry["r"], carry["acc"]

        def top_tile(u):
            carry = {"r": jnp.zeros((1, width), F32), "acc": jnp.zeros((LANES, width), F32)}

            def block(kb):
                lo = kb * 2 * blk
                rows = pl.ds(pl.multiple_of((u * nb + kb) * blk, blk), blk)
                _, lb, lk_all, _ = _sb_logits(k_ref[0, rows, :], qs[lo:], None)
                lk = _sb_diag(lk_all)
                yield
                hi, lo_part = _split2(lk)
                raw = _dot(tri2, jnp.concatenate([hi, lo_part], axis=0))
                yield
                wgt = _sb_diag(jnp.exp(lb + raw + carry["r"][:, lo:]))
                carry["r"] = _sb_add_from(carry["r"], raw[0:1] + lk[0:1], lo)
                yield
                carry["acc"] = _sb_add_from(carry["acc"], _dot(vt_ref[0, 0, u * nb + kb], wgt.astype(BF16)), lo)

            _round_robin([block(kb) for kb in reversed(range(nb))])
            return carry["r"], carry["acc"]

        top = qi0 // nb
        zero_r, zero_acc = jnp.zeros((1, width), F32), jnp.zeros((LANES, width), F32)
        r, acc = top_tile(top) if nb == nqb else tile([top], zero_r, zero_acc, True)
        r, acc = lax.fori_loop(
            0, top // 2, lambda t, c: tile([top - 1 - 2 * t, top - 2 - 2 * t], c[0], c[1], False), (r, acc))
        r, acc = lax.fori_loop(0, top % 2, lambda t, c: tile([0], c[0], c[1], False), (r, acc))
        o_ref[0] = _sb_unstack_t(acc).astype(BF16)
        tot_ref[0, 0, 0] = r

    qspec = pl.BlockSpec((1, nqb * blk, LANES), lambda b, p, i: (b, i, p))
    return pl.pallas_call(
        body, name=name, grid=(bsz, npair, nq // nqb),
        in_specs=[qspec,
                  pl.BlockSpec((1, s, LANES), lambda b, p, i: (b, 0, p)),
                  pl.BlockSpec((1, 1, nq, LANES, blk), lambda b, p, i: (b, npair + p, 0, 0, 0))],
        out_specs=[qspec, pl.BlockSpec((1, 1, 1, 1, width), lambda b, p, i: (b, p, i, 0, 0))],
        out_shape=[jax.ShapeDtypeStruct((bsz, s, w), BF16),
                   jax.ShapeDtypeStruct((bsz, npair, nq // nqb, 1, width), F32)],
        compiler_params=_params("parallel", "parallel", "arbitrary"),
    )(q, kv, kvt)


def _kv_blocks_t(kv3):
    bsz, s, w2 = kv3.shape
    x = kv3.reshape(bsz, s // SB_BLOCK, SB_BLOCK, w2 // LANES, LANES)
    return jnp.transpose(x, (0, 3, 1, 4, 2))


def _sb_bwd(q, kv, kvt, do, tot, dk_in, dv_in, *, name):
    bsz, s, w = q.shape
    blk = SB_BLOCK
    npair = w // LANES
    nq = s // blk
    nb, nqb = _sb_tile_blocks(nq, SB_Q_BLOCKS_BWD)
    width = nqb * 2 * blk
    tot = tot.reshape(bsz, npair, nq // nqb, 1, width)
    has_init = dk_in is not None

    def body(*refs):
        if has_init:
            q_ref, k_ref, v_ref, kt_ref, do_ref, tot_ref, dki_ref, dvi_ref, dq_ref, dk_ref, dv_ref = refs
        else:
            q_ref, k_ref, v_ref, kt_ref, do_ref, tot_ref, dq_ref, dk_ref, dv_ref = refs
        qi0 = pl.program_id(2) * nqb

        @pl.when(qi0 == 0)
        def _():
            if has_init:
                dk_ref[...] = dki_ref[...]
                dv_ref[...] = dvi_ref[...]
            else:
                dk_ref[...] = jnp.zeros_like(dk_ref)
                dv_ref[...] = jnp.zeros_like(dv_ref)

        qs = _sb_stack(q_ref[0] * SB_SCALE)
        dos = _sb_stack(do_ref[0])
        totv = tot_ref[0, 0, 0]
        ri = lax.broadcasted_iota(I32, (blk, blk), 0)
        ci = lax.broadcasted_iota(I32, (blk, blk), 1)
        lower = (ri > ci).astype(BF16)
        tri2 = jnp.concatenate([lower, lower], axis=1)

        def tile_phases(u, carry, masked):
            rows = pl.ds(pl.multiple_of(u * (nb * blk), nb * blk), nb * blk)
            valid = _sb_valid(u, qi0, nb, nqb) if masked else None
            z, lb, lk_all, lk = _sb_logits(k_ref[0, rows, :], qs, valid)
            yield
            before, tot_lk = _sb_scan(tri2, lk, nb, False)
            yield
            wgt = jnp.exp(z + ((totv - carry["pre_lk"]) - before))
            if masked:
                wgt = jnp.where(valid, wgt, 0.0)
            carry["pre_lk"] = carry["pre_lk"] + tot_lk
            dlogit = _dot_nt(v_ref[0, rows, :], dos) * wgt
            yield
            dbefore, tot_d = _sb_scan(tri2, dlogit, nb, False, exact=False)
            yield
            sig = jnp.exp(lb)
            dz = dlogit * (1.0 - sig) - (carry["pre_d"] + dbefore) * sig
            if masked:
                dz = jnp.where(valid, dz, 0.0)
            carry["pre_d"] = carry["pre_d"] + tot_d
            dz_b = dz.astype(BF16)
            yield
            for i in range(nb):
                carry["dqt"] = carry["dqt"] + _dot(kt_ref[0, 0, u * nb + i], dz_b[i * blk:(i + 1) * blk])
            dk_ref[0, rows, :] += _dot(dz_b, qs)
            dv_ref[0, rows, :] += _dot(wgt.astype(BF16), dos)

        def tile(us, pre_lk, pre_d, dqt, masked):
            carry = {"pre_lk": pre_lk, "pre_d": pre_d, "dqt": dqt}
            _round_robin([tile_phases(u, carry, masked) for u in us])
            return carry["pre_lk"], carry["pre_d"], carry["dqt"]

        def top_tile(u, pre_lk, pre_d, dqt):
            carry = {"rest": totv - pre_lk, "pre_d": pre_d, "dqt": dqt}

            def block(kb):
                lo = kb * 2 * blk
                rows = pl.ds(pl.multiple_of((u * nb + kb) * blk, blk), blk)
                qs_k, dos_k = qs[lo:], dos[lo:]
                z, lb, lk_all, _ = _sb_logits(k_ref[0, rows, :], qs_k, None)
                lk = _sb_diag(lk_all)
                yield
                hi, lo_part = _split2(lk)
                raw = _dot(tri2, jnp.concatenate([hi, lo_part], axis=0))
                yield
                wgt = _sb_diag(jnp.exp(z + (carry["rest"][:, lo:] - raw)))
                carry["rest"] = _sb_add_from(carry["rest"], -(raw[blk - 1:blk] + lk[blk - 1:blk]), lo)
                dlogit = _dot_nt(v_ref[0, rows, :], dos_k) * wgt
                yield
                draw = _dot(tri2[:, :blk], dlogit.astype(BF16))
                yield
                sig = jnp.exp(lb)
                dz_b = _sb_diag(dlogit * (1.0 - sig) - (carry["pre_d"][:, lo:] + draw) * sig).astype(BF16)
                carry["pre_d"] = _sb_add_from(carry["pre_d"], draw[blk - 1:blk] + dlogit[blk - 1:blk], lo)
                yield
                carry["dqt"] = _sb_add_from(carry["dqt"], _dot(kt_ref[0, 0, u * nb + kb], dz_b), lo)
                dk_ref[0, rows, :] += _dot(dz_b, qs_k)
                dv_ref[0, rows, :] += _dot(wgt.astype(BF16), dos_k)

            _round_robin([block(kb) for kb in range(nb)])
            return carry["dqt"]

        zero = jnp.zeros((1, width), F32)
        top = qi0 // nb
        c = lax.fori_loop(0, top // 2, lambda t, c: tile([2 * t, 2 * t + 1], c[0], c[1], c[2], False),
                          (zero, zero, jnp.zeros((LANES, width), F32)))
        c = lax.fori_loop(0, top % 2, lambda t, c: tile([top - 1], c[0], c[1], c[2], False), c)
        dqt = top_tile(top, *c) if nb == nqb else tile([top], c[0], c[1], c[2], True)[2]
        dq_ref[0] = (_sb_unstack_t(dqt) * SB_SCALE).astype(BF16)

    qspec = pl.BlockSpec((1, nqb * blk, LANES), lambda b, p, i: (b, i, p))
    kspec = pl.BlockSpec((1, s, LANES), lambda b, p, i: (b, 0, p))
    vspec = pl.BlockSpec((1, s, LANES), lambda b, p, i: (b, 0, npair + p))
    ktspec = pl.BlockSpec((1, 1, nq, LANES, blk), lambda b, p, i: (b, p, 0, 0, 0))
    tspec = pl.BlockSpec((1, 1, 1, 1, width), lambda b, p, i: (b, p, i, 0, 0))
    in_specs = [qspec, kspec, vspec, ktspec, qspec, tspec] + ([kspec, kspec] if has_init else [])
    args = (q, kv, kv, kvt, do, tot) + ((dk_in, dv_in) if has_init else ())
    return pl.pallas_call(
        body, name=name, grid=(bsz, npair, nq // nqb), in_specs=in_specs,
        out_specs=[qspec, kspec, kspec],
        out_shape=[jax.ShapeDtypeStruct((bsz, s, w), BF16), jax.ShapeDtypeStruct((bsz, s, w), F32),
                   jax.ShapeDtypeStruct((bsz, s, w), F32)],
        compiler_params=_params("parallel", "parallel", "arbitrary"),
    )(*args)


ADAM_BLOCK_BYTES = 1 << 20


def _adamw(w, g, m, v, *, name):
    shape = w.shape
    r, c = shape[-2], shape[-1]
    lead = _size(shape[:-2])
    tr = r
    for cand in range(8, r, 8):
        if r % cand == 0 and cand * c * 4 <= ADAM_BLOCK_BYTES:
            tr = cand
    if r * c * 4 <= ADAM_BLOCK_BYTES:
        tr = r

    def body(w_ref, g_ref, m_ref, v_ref, d_ref, mo_ref, vo_ref):
        gv = g_ref[...]
        mn = ADAM_B1 * m_ref[...] + (1.0 - ADAM_B1) * gv
        vn = ADAM_B2 * v_ref[...] + (1.0 - ADAM_B2) * (gv * gv)
        m_hat = mn / (1.0 - ADAM_B1 ** ADAM_STEP)
        v_hat = vn / (1.0 - ADAM_B2 ** ADAM_STEP)
        d_ref[...] = -ADAM_LR * (m_hat / (jnp.sqrt(v_hat) + ADAM_EPS) + ADAM_WD * w_ref[...])
        mo_ref[...] = mn
        vo_ref[...] = vn

    blk = pl.BlockSpec((1, tr, c), lambda l, i: (l, i, 0))
    outs = pl.pallas_call(
        body, name=name, grid=(lead, r // tr), in_specs=[blk] * 4, out_specs=[blk] * 3,
        out_shape=[jax.ShapeDtypeStruct((lead, r, c), F32)] * 3,
        compiler_params=_params("parallel", "parallel"),
    )(*[a.reshape(lead, r, c) for a in (w, g, m, v)])
    return [o.reshape(shape) for o in outs]


def _row_tile(r, c, itemsize):
    if r * c * 4 <= ADAM_BLOCK_BYTES:
        return r
    step = 32 // itemsize
    tr = r
    for cand in range(step, r, step):
        if r % cand == 0 and cand * c * 4 <= ADAM_BLOCK_BYTES:
            tr = cand
    return tr


def _add_own_half(idx, g, recv, *, name):
    _, lh, r, c = recv.shape
    tr = _row_tile(r, c, g.dtype.itemsize)

    def body(idx_ref, a_ref, b_ref, o_ref):
        o_ref[...] = (a_ref[...].astype(F32) + b_ref[...].astype(F32)).astype(o_ref.dtype)

    blk = pl.BlockSpec((1, 1, tr, c), lambda k, l, i, idx: (k, l, i, 0))
    return pl.pallas_call(
        body, name=name,
        grid_spec=pltpu.PrefetchScalarGridSpec(
            num_scalar_prefetch=1, grid=(N_CHIPS, lh, r // tr),
            in_specs=[pl.BlockSpec((1, 1, tr, c), lambda k, l, i, idx: (k, idx[0] * lh + l, i, 0)), blk],
            out_specs=blk),
        out_shape=jax.ShapeDtypeStruct(recv.shape, g.dtype),
        compiler_params=_params("parallel", "parallel", "parallel"),
    )(idx, g, recv)


def _add_chips(idx, own, recv, *, name):
    _, lh, r, c = own.shape
    tr = _row_tile(r, c, own.dtype.itemsize)

    def body(idx_ref, a_ref, b_ref, o_ref):
        f = lambda v: v.astype(F32)
        o_ref[0] = ((f(a_ref[0, 0]) + f(b_ref[0, 0])) + f(b_ref[1, 0])) + f(b_ref[2, 0])

    return pl.pallas_call(
        body, name=name,
        grid_spec=pltpu.PrefetchScalarGridSpec(
            num_scalar_prefetch=1, grid=(lh, r // tr),
            in_specs=[pl.BlockSpec((1, 1, tr, c), lambda l, i, idx: (idx[0], l, i, 0)),
                      pl.BlockSpec((3, 1, tr, c), lambda l, i, idx: (0, l, i, 0))],
            out_specs=pl.BlockSpec((1, tr, c), lambda l, i, idx: (l, i, 0))),
        out_shape=jax.ShapeDtypeStruct((lh, r, c), F32),
        compiler_params=_params("parallel", "parallel"),
    )(idx, own, recv)


def _sum_devices(parts):
    _, r, _ = parts.shape

    def body(p_ref, o_ref):
        acc = p_ref[0]
        for k in range(1, N_DEV):
            acc = acc + p_ref[k]
        o_ref[...] = acc

    return pl.pallas_call(
        body, name="small_grad_sum", grid=(1,),
        in_specs=[pl.BlockSpec((N_DEV, r, LANES), lambda i: (0, 0, 0))],
        out_specs=pl.BlockSpec((r, LANES), lambda i: (0, 0)),
        out_shape=jax.ShapeDtypeStruct((r, LANES), F32),
    )(parts)


def _place():
    return lax.axis_index("x"), lax.axis_index("y"), lax.axis_index("c")


def _rcopy(src, dst, send_sems, recv_sems, k, to):
    return pltpu.make_async_remote_copy(src_ref=src, dst_ref=dst, send_sem=send_sems.at[k],
                                        recv_sem=recv_sems.at[k], device_id=to, device_id_type=MESH)


def _exchange_call(body, name, ins, out_shapes, n_sems):
    return pl.pallas_call(
        body, name=name, in_specs=[ANY] * len(ins), out_specs=[ANY] * len(out_shapes),
        out_shape=out_shapes,
        scratch_shapes=[pltpu.SemaphoreType.DMA((n_sems,)), pltpu.SemaphoreType.DMA((n_sems,))],
    )(*ins)


def _gather_weights(shards):
    n = len(shards)

    def body(*refs):
        ins, outs, send_sems, recv_sems = refs[:n], refs[n:2 * n], refs[2 * n], refs[2 * n + 1]
        x, y, c = _place()
        sibling = (x, y, 1 - c)
        chips = [(1 - x, y), (x, 1 - y), (1 - x, 1 - y)]

        def piece(i, px, py, pc):
            lh = ins[i].shape[0] // 2
            return outs[i].at[2 * px + py, pl.ds(pc * lh, lh)]

        def mine(i):
            lh = ins[i].shape[0] // 2
            return ins[i].at[pl.ds(c * lh, lh)]

        first = [_rcopy(mine(i), piece(i, x, y, c), send_sems, recv_sems, 6 * i + j, (*chip, c))
                 for i in range(n) for j, chip in enumerate(chips)]
        for cp in first:
            cp.start()
        passed = []
        for i in range(n):
            for j, chip in enumerate(chips):
                landed = piece(i, *chip, c)
                _rcopy(landed, landed, send_sems, recv_sems, 6 * i + j, (*chip, c)).wait_recv()
                passed.append(_rcopy(landed, landed, send_sems, recv_sems, 6 * i + 3 + j, sibling))
                passed[-1].start()
        for i in range(n):
            for j, chip in enumerate(chips):
                theirs = piece(i, *chip, 1 - c)
                _rcopy(theirs, theirs, send_sems, recv_sems, 6 * i + 3 + j, sibling).wait_recv()
        for cp in first + passed:
            cp.wait_send()

    shapes = [jax.ShapeDtypeStruct((N_CHIPS,) + s_.shape, s_.dtype) for s_ in shards]
    return _exchange_call(body, "gather_weights", shards, shapes, 6 * n)


def _swap_halves(gs):
    n = len(gs)

    def body(*refs):
        ins, outs, send_sems, recv_sems = refs[:n], refs[n:2 * n], refs[2 * n], refs[2 * n + 1]
        x, y, c = _place()
        cps = []
        for i in range(n):
            lh = ins[i].shape[1] // 2
            src = ins[i].at[pl.ds(0, N_CHIPS), pl.ds((1 - c) * lh, lh)]
            cps.append(_rcopy(src, outs[i], send_sems, recv_sems, i, (x, y, 1 - c)))
        for cp in cps:
            cp.start()
        for cp in cps:
            cp.wait()

    shapes = [jax.ShapeDtypeStruct((N_CHIPS, g.shape[1] // 2) + g.shape[2:], g.dtype) for g in gs]
    return _exchange_call(body, "grad_swap_halves", gs, shapes, n)


def _scatter_chunks(sums):
    n = len(sums)

    def body(*refs):
        ins, outs, send_sems, recv_sems = refs[:n], refs[n:2 * n], refs[2 * n], refs[2 * n + 1]
        x, y, c = _place()
        chips = [(1 - x, y), (x, 1 - y), (1 - x, 1 - y)]
        cps = [_rcopy(ins[i].at[2 * chip[0] + chip[1]], outs[i].at[j], send_sems, recv_sems, 3 * i + j,
                      (*chip, c)) for i in range(n) for j, chip in enumerate(chips)]
        for cp in cps:
            cp.start()
        for cp in cps:
            cp.wait()

    shapes = [jax.ShapeDtypeStruct((3,) + s_.shape[1:], s_.dtype) for s_ in sums]
    return _exchange_call(body, "grad_scatter_chunks", sums, shapes, 3 * n)


def _share_half(tots):
    n = len(tots)

    def body(*refs):
        ins, outs, send_sems, recv_sems = refs[:n], refs[n:2 * n], refs[2 * n], refs[2 * n + 1]
        x, y, c = _place()
        cps = [_rcopy(ins[i], outs[i], send_sems, recv_sems, i, (x, y, 1 - c)) for i in range(n)]
        for cp in cps:
            cp.start()
        for cp in cps:
            cp.wait()

    shapes = [jax.ShapeDtypeStruct(t_.shape, t_.dtype) for t_ in tots]
    return _exchange_call(body, "grad_share_half", tots, shapes, n)


def _exchange_small(r):
    rr, _ = r.shape

    def body(r_ref, out_ref, send_sems, recv_sems, local_sem):
        x, y, c = _place()
        me = 4 * x + 2 * y + c
        mine = pltpu.make_async_copy(r_ref, out_ref.at[me], local_sem)
        mine.start()
        cps = []
        for k in range(N_DEV - 1):
            fx, fy, fc = ((k + 1) >> 2) & 1, ((k + 1) >> 1) & 1, (k + 1) & 1
            to = (x ^ fx, y ^ fy, c ^ fc)
            cps.append((_rcopy(r_ref, out_ref.at[me], send_sems, recv_sems, k, to), to))
        for cp, _ in cps:
            cp.start()
        for k, (cp, to) in enumerate(cps):
            src = 4 * to[0] + 2 * to[1] + to[2]
            _rcopy(r_ref, out_ref.at[src], send_sems, recv_sems, k, to).wait_recv()
        for cp, _ in cps:
            cp.wait_send()
        mine.wait()

    return pl.pallas_call(
        body, name="small_grad_exchange", in_specs=[ANY], out_specs=ANY,
        out_shape=jax.ShapeDtypeStruct((N_DEV, rr, LANES), r.dtype),
        scratch_shapes=[pltpu.SemaphoreType.DMA((N_DEV - 1,)), pltpu.SemaphoreType.DMA((N_DEV - 1,)),
                        pltpu.SemaphoreType.DMA],
    )(r)


def _size(shape):
    n = 1
    for d in shape:
        n *= d
    return n


def _slab_rows(shape):
    rows = -(-_size(shape) // LANES)
    return -(-rows // SLAB_ROW_ALIGN) * SLAB_ROW_ALIGN


def _pack_rows(arrs, dtype, lead=0, unit=PACK_ROWS):
    parts, total = [], 0
    for a in arrs:
        front, shp = a.shape[:lead], a.shape[lead:]
        n, rows = _size(shp), _slab_rows(shp)
        nopad = [(0, 0)] * lead
        if n % LANES == 0:
            p = a.reshape(front + (n // LANES, LANES)).astype(dtype)
        else:
            p = jnp.pad(a.reshape(front + (n,)).astype(dtype), nopad + [(0, rows * LANES - n)])
            p = p.reshape(front + (rows, LANES))
        if p.shape[lead] != rows:
            p = jnp.pad(p, nopad + [(0, rows - p.shape[lead]), (0, 0)])
        parts.append(p)
        total += rows
    pad = (-total) % unit
    if pad:
        parts.append(jnp.zeros(parts[0].shape[:lead] + (pad, LANES), dtype))
    return jnp.concatenate(parts, axis=lead)


def _unpack_rows(slab, shapes):
    lead = slab.shape[:-2]
    out, off = [], 0
    for shp in shapes:
        n, rows = _size(shp), _slab_rows(shp)
        piece = slab[..., off:off + rows, :]
        if n % LANES == 0:
            piece = piece[..., :n // LANES, :].reshape(lead + tuple(shp))
        else:
            piece = piece.reshape(lead + (rows * LANES,))[..., :n].reshape(lead + tuple(shp))
        out.append(piece)
        off += rows
    return out


def _ffn_fwd(h, bsz, s, gain, w_up, cw, cb, w_down, i):
    hf = _rmsnorm_fwd(h, gain, name=f"ffn_norm_{i}")
    up = _mm(hf, w_up, name=f"ffn_up_{i}")
    up3 = up.reshape(bsz, s, -1)
    f = _conv_glu_fwd(up3, cw, cb, name=f"ffn_glu_{i}").reshape(h.shape[0], -1)
    h2 = _mm(f, w_down, add=h, name=f"ffn_down_{i}")
    return h2, (h, hf, up3, f)


def _ffn_bwd(dh, saved, gain, w_up, cw, cb, w_down, i):
    h, hf, up3, f = saved
    t = h.shape[0]
    d_down = _mm(f, dh, ta=True, name=f"ffn_down_dw_{i}")
    df = _mm(dh, w_down, tb=True, name=f"ffn_down_dx_{i}")
    dug, duv, dcw, dcb = _conv_glu_bwd(up3, cw, cb, df.reshape(up3.shape[0], up3.shape[1], -1),
                                       name=f"ffn_glu_bwd_{i}")
    dug, duv = dug.reshape(t, -1), duv.reshape(t, -1)
    fdim = dug.shape[1]
    d_up = jnp.concatenate([_mm(hf, dug, ta=True, name=f"ffn_up_dwg_{i}"),
                            _mm(hf, duv, ta=True, name=f"ffn_up_dwv_{i}")], axis=1)
    dhf = _mm(dug, w_up[:, :fdim], tb=True, name=f"ffn_up_dxg_{i}")
    dh, dgain = _mm_norm_bwd(duv, w_up[:, fdim:], h, gain, dh, add=dhf, name=f"ffn_up_dxv_{i}")
    return dh, dgain, d_up, dcw, dcb, d_down


def _ple_layer_fwd(h, p_i, gain, w_gate, w_proj, i):
    hp = _rmsnorm_fwd(h, gain, name=f"ple_norm_{i}")
    a = _mm(hp, w_gate, name=f"ple_gate_{i}")
    pp = _mm(p_i, w_proj, name=f"ple_proj_{i}")
    return _ple_fwd(h, a, pp, name=f"ple_mix_{i}"), (h, hp, a, pp)


def _ple_layer_bwd(dh, saved, p_i, gain, w_gate, i):
    h, hp, a, pp = saved
    da, dpp = _ple_bwd(dh, a, pp, name=f"ple_mix_bwd_{i}")
    d_gate = _mm(hp, da, ta=True, name=f"ple_gate_dw_{i}")
    d_proj = _mm(p_i, dpp, ta=True, name=f"ple_proj_dw_{i}")
    dh, dgain = _mm_norm_bwd(da, w_gate, h, gain, dh, name=f"ple_gate_dx_{i}")
    return dh, dgain, d_gate, d_proj


def _ssm_consts(dt_bias, a_log, d_skip, g_n):
    hpg = SSM_HEADS_PER_GROUP
    a = -jnp.exp(a_log)
    rows = jnp.stack([dt_bias.reshape(g_n, hpg), a.reshape(g_n, hpg)], axis=1)
    plane = jnp.zeros((g_n, 8, LANES), F32).at[:, 0:2, 0:hpg].set(rows)
    psub = jnp.zeros((g_n, SSM_ROWS, LANES), F32).at[:, 0:hpg, 0:2].set(jnp.swapaxes(rows, 1, 2))
    d_ch = jnp.repeat(d_skip, SSM_HEAD_DIM).reshape(1, -1)
    return a, plane, psub, d_ch


def _ssm_in_big(w_in, d_inner, g_n):
    d = w_in.shape[0]
    cut = w_in.shape[1] - g_n * SSM_HEADS_PER_GROUP
    wdt = w_in[:, cut:].reshape(d, g_n, SSM_HEADS_PER_GROUP)
    wdt = jnp.pad(wdt, ((0, 0), (0, 0), (0, LANES - SSM_HEADS_PER_GROUP))).reshape(d, g_n * LANES)
    return jnp.concatenate([w_in[:, :cut], wdt], axis=1)


def _ssm_in_small(dw_big, g_n):
    d = dw_big.shape[0]
    cut = dw_big.shape[1] - g_n * LANES
    ddt = dw_big[:, cut:].reshape(d, g_n, LANES)[:, :, :SSM_HEADS_PER_GROUP].reshape(d, -1)
    return jnp.concatenate([dw_big[:, :cut], ddt], axis=1)


def _ssm_fwd(h, bsz, s, gain, w_in_big, cw, cb, plane, psub, d_ch, nw, w_out, i):
    d_inner = d_ch.shape[1]
    g_n = d_inner // SSM_GROUP_W
    conv_dim = cw.shape[1]
    hn = _rmsnorm_fwd(h, gain, name=f"attn_norm_{i}")
    zx = _mm(hn, w_in_big, name=f"ssm_in_{i}").reshape(bsz, s, -1)
    xbc = _conv_silu_fwd(zx, d_inner, cw, cb, name=f"ssm_conv_{i}")
    dtr = zx[:, :, d_inner + conv_dim:].reshape(bsz, s, g_n, LANES)[..., :SSM_HEADS_PER_GROUP]
    dtr_row = jnp.pad(jnp.transpose(dtr, (0, 2, 3, 1)),
                      ((0, 0), (0, 0), (0, SSM_ROWS - SSM_HEADS_PER_GROUP), (0, 0)))
    gn, states = _ssd_fwd(zx, xbc, dtr_row, plane, psub, d_ch, nw, name=f"ssd_{i}")
    gn2 = gn.reshape(h.shape[0], -1)
    h1 = _mm(gn2, w_out, add=h, name=f"ssm_out_{i}")
    return h1, (h, hn, zx, xbc, dtr_row, states, gn2)


def _ssm_bwd(dh, saved, gain, w_in_big, cw, cb, plane, psub, d_ch, nw, w_out, i):
    h, hn, zx, xbc, dtr_row, states, gn2 = saved
    t = h.shape[0]
    bsz, s, _ = zx.shape
    d_inner = d_ch.shape[1]
    d_out = _mm(gn2, dh, ta=True, name=f"ssm_out_dw_{i}")
    dgn = _mm(dh, w_out, tb=True, name=f"ssm_out_dx_{i}").reshape(bsz, s, -1)
    dxs, dbm, dcm, dz, ddtr, ach, aln = _ssd_bwd(zx, xbc, dtr_row, plane, psub, d_ch, nw, states, dgn,
                                                  name=f"ssd_bwd_{i}")
    dxbc, dcw, dcb = _conv_silu_bwd(zx, d_inner, cw, cb, [dxs, dbm, dcm], name=f"ssm_conv_bwd_{i}")
    d_in_parts, dhn, col = [], None, 0
    for tag, part in (("z", dz), ("dt", ddtr), ("xbc", dxbc)):
        part = part.reshape(t, -1)
        col = {"z": 0, "xbc": dz.shape[-1], "dt": dz.shape[-1] + dxbc.shape[-1]}[tag]
        w_part = w_in_big[:, col:col + part.shape[1]]
        d_in_parts.append(_mm(hn, part, ta=True, name=f"ssm_in_dw_{tag}_{i}"))
        if tag == "xbc":
            dh, dgain = _mm_norm_bwd(part, w_part, h, gain, dh, add=dhn, name=f"ssm_in_dx_{tag}_{i}")
        else:
            dhn = _mm(part, w_part, tb=True, add=dhn, name=f"ssm_in_dx_{tag}_{i}")
    d_in_big = jnp.concatenate([d_in_parts[0], d_in_parts[2], d_in_parts[1]], axis=1)
    hpg = SSM_HEADS_PER_GROUP
    ach = jnp.sum(ach, axis=0)
    aln = jnp.sum(aln, axis=0)
    d_nw = ach[:, 0, :].reshape(-1)
    d_dskip = jnp.sum(ach[:, 1, :].reshape(-1, SSM_HEAD_DIM), axis=1)
    d_bias = aln[:, 0, :hpg].reshape(-1)
    d_a = aln[:, 1, :hpg].reshape(-1)
    return dh, dgain, d_in_big, dcw, dcb, d_bias, d_a, d_dskip, d_nw, d_out


def _sb_layer_fwd(h, bsz, s, gain, w_q, w_o, kv3, kvt, i):
    hn = _rmsnorm_fwd(h, gain, name=f"attn_norm_{i}")
    q3 = _mm(hn, w_q, out_dtype=BF16, name=f"sb_q_{i}").reshape(bsz, s, -1)
    o3, tot = _sb_fwd(q3, kv3, kvt, name=f"sb_attn_{i}")
    o2 = o3.reshape(h.shape[0], -1)
    h1 = _mm(o2, w_o, add=h, name=f"sb_o_{i}")
    return h1, (h, hn, q3, o2, tot)


def _sb_layer_bwd(dh, saved, gain, w_q, w_o, kv3, kvt, dk, dv, i):
    h, hn, q3, o2, tot = saved
    t = h.shape[0]
    d_o = _mm(o2, dh, ta=True, name=f"sb_o_dw_{i}")
    do3 = _mm(dh, w_o, tb=True, out_dtype=BF16, name=f"sb_o_dx_{i}").reshape(q3.shape)
    dq3, dk, dv = _sb_bwd(q3, kv3, kvt, do3, tot, dk, dv, name=f"sb_attn_bwd_{i}")
    dq = dq3.reshape(t, -1)
    d_q = _mm(hn, dq, ta=True, name=f"sb_q_dw_{i}")
    dh, dgain = _mm_norm_bwd(dq, w_q, h, gain, dh, name=f"sb_q_dx_{i}")
    return dh, dgain, d_q, d_o, dk, dv


def kernel(x, p, attn_norm, ffn_norm, ple_norm, ssm_in_proj, ssm_conv_w, ssm_conv_b, ssm_dt_bias, ssm_a_log, ssm_d, ssm_norm, ssm_out_proj, kv_norm, w_kv, w_q, w_o, ffn_up, ffn_conv_w, ffn_conv_b, ffn_down, ple_gate, ple_proj, final_norm, loss_target, m_attn_norm, m_ffn_norm, m_ple_norm, m_ssm_in_proj, m_ssm_conv_w, m_ssm_conv_b, m_ssm_dt_bias, m_ssm_a_log, m_ssm_d, m_ssm_norm, m_ssm_out_proj, m_kv_norm, m_w_kv, m_w_q, m_w_o, m_ffn_up, m_ffn_conv_w, m_ffn_conv_b, m_ffn_down, m_ple_gate, m_ple_proj, m_final_norm, v_attn_norm, v_ffn_norm, v_ple_norm, v_ssm_in_proj, v_ssm_conv_w, v_ssm_conv_b, v_ssm_dt_bias, v_ssm_a_log, v_ssm_d, v_ssm_norm, v_ssm_out_proj, v_kv_norm, v_w_kv, v_w_q, v_w_o, v_ffn_up, v_ffn_conv_w, v_ffn_conv_b, v_ffn_down, v_ple_gate, v_ple_proj, v_final_norm):
    given = dict(locals())
    wl = {n: given[n] for n in WEIGHTS}
    bsz, s, d = x.shape
    t = bsz * s
    depth = attn_norm.shape[0]
    n_a = ssm_in_proj.shape[0]
    d_inner = ssm_norm.shape[1] * N_CHIPS
    g_n = d_inner // SSM_GROUP_W
    cidx = lax.axis_index("c").astype(I32).reshape(1)
    chip_idx = (2 * lax.axis_index("x") + lax.axis_index("y")).astype(I32).reshape(1)

    big = [n for n in SHARDED if _size(wl[n].shape) >= BIG_WEIGHT]
    small = [n for n in SHARDED if n not in big]
    small_shapes = [wl[n].shape for n in small]
    halves = lambda shp: shp if len(shp) == 3 else (2, shp[0] // 2, shp[1])
    small_slab = _pack_rows([wl[n] for n in small], BF16, unit=2 * SLAB_ROW_ALIGN)
    small_rows = small_slab.shape[0]
    mine = [wl[n].astype(BF16).reshape(halves(wl[n].shape)) for n in big]
    mine.append(small_slab.reshape(2, small_rows // 2, LANES))
    gathered = [lax.dynamic_update_index_in_dim(g, m_, chip_idx[0], 0)
                for g, m_ in zip(_gather_weights(mine), mine)]
    per_chip = {n: g.reshape((N_CHIPS,) + wl[n].shape) for n, g in zip(big, gathered)}
    per_chip.update(zip(small, _unpack_rows(gathered[-1].reshape(N_CHIPS, small_rows, LANES), small_shapes)))
    full = {}
    for n in SHARDED:
        ax, piece = SHARD_AXIS[n], per_chip[n]
        merged = piece.shape[1:ax + 1] + (N_CHIPS * piece.shape[ax + 1],) + piece.shape[ax + 2:]
        full[n] = jnp.moveaxis(piece, 0, ax).reshape(merged)

    h = x.reshape(t, d)
    tgt = loss_target.reshape(t, d)
    saved = []
    kv3 = kvt = hkv = h_kv_in = None
    consts = []
    for i in range(depth):
        if i < n_a:
            a_neg, plane, psub, d_ch = _ssm_consts(ssm_dt_bias[i], ssm_a_log[i], ssm_d[i], g_n)
            w_in_big = _ssm_in_big(full["ssm_in_proj"][i], d_inner, g_n)
            cw = full["ssm_conv_w"][i].astype(F32)
            cb = full["ssm_conv_b"][i].astype(F32)
            nw = full["ssm_norm"][i].astype(F32).reshape(1, -1)
            consts.append((a_neg, plane, psub, d_ch, w_in_big, cw, cb, nw))
            h, sv_mix = _ssm_fwd(h, bsz, s, attn_norm[i], w_in_big, cw, cb, plane, psub, d_ch, nw,
                                 full["ssm_out_proj"][i], i)
        else:
            j = i - n_a
            h, sv_mix = _sb_layer_fwd(h, bsz, s, attn_norm[i], full["w_q"][j], full["w_o"][j], kv3, kvt, i)
        fcw = full["ffn_conv_w"][i].astype(F32)
        h, sv_ffn = _ffn_fwd(h, bsz, s, ffn_norm[i], full["ffn_up"][i], fcw, ffn_conv_b[i],
                             full["ffn_down"][i], i)
        p_i = p[i].reshape(t, -1)
        h, sv_ple = _ple_layer_fwd(h, p_i, ple_norm[i], full["ple_gate"][i], full["ple_proj"][i], i)
        saved.append((sv_mix, sv_ffn, sv_ple))
        if i == n_a - 1:
            h_kv_in = h
            hkv = _rmsnorm_fwd(h, kv_norm, name="kv_norm")
            kv3 = _mm(hkv, full["w_kv"], out_dtype=BF16, name="kv_proj").reshape(bsz, s, -1)
            kvt = _kv_blocks_t(kv3)

    loss_local, dh, g_final = _final_loss(h, final_norm, tgt)
    gr = {n: [None] * wl[n].shape[0] for n in WEIGHTS if n not in ("kv_norm", "w_kv", "final_norm")}
    gr["final_norm"] = g_final
    dk = dv = None
    for i in reversed(range(depth)):
        sv_mix, sv_ffn, sv_ple = saved[i]
        if i == n_a - 1:
            dkv = jnp.concatenate([dk, dv], axis=-1).reshape(t, -1)
            gr["w_kv"] = _mm(hkv, dkv, ta=True, name="kv_proj_dw")
            dh, gr["kv_norm"] = _mm_norm_bwd(dkv, full["w_kv"], h_kv_in, kv_norm, dh, name="kv_proj_dx")
        p_i = p[i].reshape(t, -1)
        dh, gr["ple_norm"][i], gr["ple_gate"][i], gr["ple_proj"][i] = _ple_layer_bwd(
            dh, sv_ple, p_i, ple_norm[i], full["ple_gate"][i], i)
        fcw = full["ffn_conv_w"][i].astype(F32)
        (dh, gr["ffn_norm"][i], gr["ffn_up"][i], gr["ffn_conv_w"][i], gr["ffn_conv_b"][i],
         gr["ffn_down"][i]) = _ffn_bwd(dh, sv_ffn, ffn_norm[i], full["ffn_up"][i], fcw, ffn_conv_b[i],
                                       full["ffn_down"][i], i)
        if i < n_a:
            a_neg, plane, psub, d_ch, w_in_big, cw, cb, nw = consts[i]
            (dh, gr["attn_norm"][i], d_in_big, gr["ssm_conv_w"][i], gr["ssm_conv_b"][i],
             gr["ssm_dt_bias"][i], d_a, gr["ssm_d"][i], gr["ssm_norm"][i],
             gr["ssm_out_proj"][i]) = _ssm_bwd(dh, sv_mix, attn_norm[i], w_in_big, cw, cb, plane, psub,
                                               d_ch, nw, full["ssm_out_proj"][i], i)
            gr["ssm_in_proj"][i] = _ssm_in_small(d_in_big, g_n)
            gr["ssm_a_log"][i] = d_a * a_neg
        else:
            j = i - n_a
            dh, gr["attn_norm"][i], gr["w_q"][j], gr["w_o"][j], dk, dv = _sb_layer_bwd(
                dh, sv_mix, attn_norm[i], full["w_q"][j], full["w_o"][j], kv3, kvt, dk, dv, i)
    grad_x = dh.reshape(bsz, s, d)
    gfull = {n: (jnp.stack(v) if isinstance(v, list) else v) for n, v in gr.items()}

    by_chip = {}
    for n in SHARDED:
        ax, shp = SHARD_AXIS[n], gfull[n].shape
        split = gfull[n].reshape(shp[:ax] + (N_CHIPS, shp[ax] // N_CHIPS) + shp[ax + 1:])
        by_chip[n] = jnp.moveaxis(split, ax, 0)
    g4 = [by_chip[n].astype(BF16).reshape((N_CHIPS,) + halves(wl[n].shape)) for n in big]
    g4.append(_pack_rows([by_chip[n] for n in small], BF16, lead=1, unit=2 * SLAB_ROW_ALIGN)
              .reshape(N_CHIPS, 2, small_rows // 2, LANES))
    tags = big + ["small"]
    from_sibling = _swap_halves(g4)
    chip_sums = [_add_own_half(cidx, g, r_, name="grad_pair_sum_" + tg)
                 for g, r_, tg in zip(g4, from_sibling, tags)]
    from_chips = _scatter_chunks(chip_sums)
    my_half = [_add_chips(chip_idx, s_, r_, name="grad_chip_sum_" + tg)
               for s_, r_, tg in zip(chip_sums, from_chips, tags)]
    other_half = _share_half(my_half)
    low_core = cidx[0] == 0
    reduced = [jnp.concatenate([jnp.where(low_core, a, b_), jnp.where(low_core, b_, a)], axis=0)
               for a, b_ in zip(my_half, other_half)]
    grads = {n: g.reshape(wl[n].shape) for n, g in zip(big, reduced)}
    grads.update(zip(small, _unpack_rows(reduced[-1].reshape(small_rows, LANES), small_shapes)))

    rep_shapes = [wl[n].shape for n in REPLICATED]
    packed_r = _pack_rows([gfull[n] for n in REPLICATED], F32, unit=SLAB_ROW_ALIGN)
    g_rep = _sum_devices(_exchange_small(packed_r))

    delta, new_m, new_v = {}, {}, {}
    for n in SHARDED:
        delta[n], new_m[n], new_v[n] = _adamw(wl[n], grads[n], given["m_" + n], given["v_" + n],
                                              name="adamw_" + n)
    slabs = [_pack_rows([src[pre + n] for n in REPLICATED], F32, unit=SLAB_ROW_ALIGN)
             for src, pre in ((wl, ""), (given, "m_"), (given, "v_"))]
    rep_out = _adamw(slabs[0], g_rep, slabs[1], slabs[2], name="adamw_replicated")
    for dst, slab in zip((grads, delta, new_m, new_v), [g_rep] + list(rep_out)):
        dst.update(zip(REPLICATED, _unpack_rows(slab, rep_shapes)))
    loss = lax.psum(loss_local, ("x", "y", "c"))
    return (loss, grad_x, *[grads[n] for n in WEIGHTS], *[delta[n] for n in WEIGHTS],
            *[new_m[n] for n in WEIGHTS], *[new_v[n] for n in WEIGHTS])
```

```python
import functools

import jax
import jax.numpy as jnp
from jax import lax
from jax.experimental import pallas as pl
from jax.experimental.pallas import tpu as pltpu

F32 = jnp.float32
BF16 = jnp.bfloat16
I32 = jnp.int32

NORM_EPS = 1e-6
SSM_NORM_EPS = 1e-5
SSM_HEAD_DIM = 64
SSM_STATE = 128
SSM_CHUNK = 128
SSM_HEADS_PER_GROUP = 8
SSM_GROUP_W = SSM_HEADS_PER_GROUP * SSM_HEAD_DIM
SSM_CONV = 4
SSM_ROWS = 16
SB_HEAD_DIM = 64
SB_BLOCK = 128
SB_SCALE = SB_HEAD_DIM ** -0.5
SB_Q_BLOCKS_FWD = 4
SB_Q_BLOCKS_BWD = 4
FFN_CONV = 3
LANES = 128
N_CHIPS = 4
N_DEV = 8

ADAM_LR = 0.001
ADAM_B1 = 0.9
ADAM_B2 = 0.999
ADAM_EPS = 1e-08
ADAM_WD = 0.01
ADAM_STEP = 10

MESH = pl.DeviceIdType.MESH
ANY = pl.BlockSpec(memory_space=pl.ANY)

SHARD_AXIS = {
    "ssm_in_proj": 2, "ssm_conv_w": 2, "ssm_conv_b": 1, "ssm_norm": 1, "ssm_out_proj": 1,
    "w_kv": 1, "w_q": 1, "w_o": 1, "ffn_up": 2, "ffn_conv_w": 2, "ffn_down": 1,
    "ple_gate": 1, "ple_proj": 2,
}
REPLICATED = ["attn_norm", "ffn_norm", "ple_norm", "ssm_dt_bias", "ssm_a_log", "ssm_d",
              "kv_norm", "ffn_conv_b", "final_norm"]
WEIGHTS = ["attn_norm", "ffn_norm", "ple_norm", "ssm_in_proj", "ssm_conv_w", "ssm_conv_b",
           "ssm_dt_bias", "ssm_a_log", "ssm_d", "ssm_norm", "ssm_out_proj", "kv_norm", "w_kv",
           "w_q", "w_o", "ffn_up", "ffn_conv_w", "ffn_conv_b", "ffn_down", "ple_gate",
           "ple_proj", "final_norm"]
SHARDED = [n for n in WEIGHTS if n in SHARD_AXIS]
PACK_ROWS = 2048
SLAB_ROW_ALIGN = 16
BIG_WEIGHT = 1 << 17


def _tile(n, pref):
    t = (min(pref, n) // 128) * 128
    while t >= 128:
        if n % t == 0:
            return t
        t -= 128
    return n


def _dot(a, b):
    return jnp.dot(a, b, preferred_element_type=F32)


def _dot_nt(a, b):
    return lax.dot_general(a, b, (((1,), (1,)), ((), ())), preferred_element_type=F32)


def _dot_tn(a, b):
    return lax.dot_general(a, b, (((0,), (0,)), ((), ())), preferred_element_type=F32)


def _split2(x):
    hi = x.astype(BF16)
    lo = (x - hi.astype(F32)).astype(BF16)
    return hi, lo


def _dot2(x, m):
    hi, lo = _split2(x)
    return _dot(hi, m) + _dot(lo, m)


def _dot2_left(m, x):
    hi, lo = _split2(x)
    return _dot(m, hi) + _dot(m, lo)


def _softplus(x):
    return jnp.maximum(x, 0.0) + jnp.log(1.0 + jnp.exp(-jnp.abs(x)))


def _sigmoid(x):
    return 0.5 * jnp.tanh(0.5 * x) + 0.5


def _params(*sem):
    return pltpu.CompilerParams(dimension_semantics=sem)


def _round_robin(gens):
    live = list(gens)
    while live:
        still = []
        for gen in live:
            try:
                next(gen)
                still.append(gen)
            except StopIteration:
                pass
        live = still


MM_VMEM_BUDGET = 36 * 1024 * 1024
MM_FULL_K = 2816


def _mm_tiles(m, n, k, sa, sb, so, has_add, extra=0):
    tk = k if k <= MM_FULL_K else _tile(k, 1024)
    tn = _tile(n, 1408)
    tm = _tile(m, 1408)

    def need(tm_):
        return (2 * tm_ * tk * sa + 2 * tk * tn * sb + tm_ * tn * 4 + 2 * tm_ * tn * so
                + 2 * tm_ * tn * 4 * (extra + (1 if has_add else 0)))

    while need(tm) > MM_VMEM_BUDGET and tm % 256 == 0:
        tm //= 2
    return tm, tn, tk


def _mm(a, b, *, name, ta=False, tb=False, add=None, out_dtype=F32):
    m = a.shape[1] if ta else a.shape[0]
    k = a.shape[0] if ta else a.shape[1]
    n = b.shape[0] if tb else b.shape[1]
    assert (b.shape[1] if tb else b.shape[0]) == k, (a.shape, b.shape, ta, tb)
    tm, tn, tk = _mm_tiles(m, n, k, a.dtype.itemsize, b.dtype.itemsize,
                           jnp.dtype(out_dtype).itemsize, add is not None)
    nk = k // tk
    dims = (((0 if ta else 1,), (1 if tb else 0,)), ((), ()))
    has_add = add is not None

    def body(*refs):
        if has_add:
            a_ref, b_ref, add_ref, o_ref, acc_ref = refs
        else:
            a_ref, b_ref, o_ref, acc_ref = refs
        kk = pl.program_id(2)

        @pl.when(kk == 0)
        def _():
            acc_ref[...] = jnp.zeros_like(acc_ref)

        acc_ref[...] += lax.dot_general(a_ref[...].astype(BF16), b_ref[...].astype(BF16), dims,
                                        preferred_element_type=F32)

        @pl.when(kk == nk - 1)
        def _():
            r = acc_ref[...]
            if has_add:
                r = r + add_ref[...].astype(F32)
            o_ref[...] = r.astype(out_dtype)

    a_spec = (pl.BlockSpec((tk, tm), lambda i, j, kk: (kk, i)) if ta
              else pl.BlockSpec((tm, tk), lambda i, j, kk: (i, kk)))
    b_spec = (pl.BlockSpec((tn, tk), lambda i, j, kk: (j, kk)) if tb
              else pl.BlockSpec((tk, tn), lambda i, j, kk: (kk, j)))
    o_spec = pl.BlockSpec((tm, tn), lambda i, j, kk: (i, j))
    in_specs = [a_spec, b_spec] + ([o_spec] if has_add else [])
    args = (a, b) + ((add,) if has_add else ())
    return pl.pallas_call(
        body, name=name, grid=(m // tm, n // tn, nk), in_specs=in_specs, out_specs=o_spec,
        out_shape=jax.ShapeDtypeStruct((m, n), out_dtype),
        scratch_shapes=[pltpu.VMEM((tm, tn), F32)],
        compiler_params=_params("parallel", "parallel", "arbitrary"),
    )(*args)


def _mm_norm_bwd(a, b, x, gain, dres, *, name, add=None):
    m, k = a.shape
    n = b.shape[0]
    has_add = add is not None
    tm, tn, tk = _mm_tiles(m, n, k, a.dtype.itemsize, b.dtype.itemsize, 4, has_add, extra=2)
    assert tn == n, (tn, n)
    nk = k // tk

    def body(*refs):
        if has_add:
            a_ref, b_ref, add_ref, x_ref, g_ref, dres_ref, dx_ref, dg_ref, acc_ref = refs
        else:
            a_ref, b_ref, x_ref, g_ref, dres_ref, dx_ref, dg_ref, acc_ref = refs
        i, kk = pl.program_id(0), pl.program_id(1)

        @pl.when(kk == 0)
        def _():
            acc_ref[...] = jnp.zeros_like(acc_ref)

        acc_ref[...] += _dot_nt(a_ref[...].astype(BF16), b_ref[...].astype(BF16))

        @pl.when(kk == nk - 1)
        def _():
            dyv = acc_ref[...]
            if has_add:
                dyv = dyv + add_ref[...]
            xv = x_ref[...]
            r = lax.rsqrt(jnp.mean(xv * xv, axis=-1, keepdims=True) + NORM_EPS)
            xh = xv * r
            dxh = dyv * g_ref[...]
            dx_ref[...] = dres_ref[...] + r * (dxh - xh * jnp.mean(dxh * xh, axis=-1, keepdims=True))
            part = jnp.sum(dyv * xh, axis=0, keepdims=True)

            @pl.when(i == 0)
            def _():
                dg_ref[...] = part

            @pl.when(i > 0)
            def _():
                dg_ref[...] += part

    row = pl.BlockSpec((tm, n), lambda i, kk: (i, 0))
    vec = pl.BlockSpec((1, n), lambda i, kk: (0, 0))
    in_specs = ([pl.BlockSpec((tm, tk), lambda i, kk: (i, kk)), pl.BlockSpec((n, tk), lambda i, kk: (0, kk))]
                + ([row] if has_add else []) + [row, vec, row])
    args = (a, b) + ((add,) if has_add else ()) + (x, gain.reshape(1, n), dres)
    dx, dg = pl.pallas_call(
        body, name=name, grid=(m // tm, nk), in_specs=in_specs, out_specs=[row, vec],
        out_shape=[jax.ShapeDtypeStruct((m, n), F32), jax.ShapeDtypeStruct((1, n), F32)],
        scratch_shapes=[pltpu.VMEM((tm, n), F32)],
        compiler_params=_params("arbitrary", "arbitrary"),
    )(*args)
    return dx, dg.reshape(n)


def _rmsnorm_fwd(x, gain, *, name, rows=512):
    t, d = x.shape
    tr = _tile(t, rows)

    def body(x_ref, g_ref, o_ref):
        xv = x_ref[...]
        r = lax.rsqrt(jnp.mean(xv * xv, axis=-1, keepdims=True) + NORM_EPS)
        o_ref[...] = ((xv * r) * g_ref[...]).astype(BF16)

    return pl.pallas_call(
        body, name=name, grid=(t // tr,),
        in_specs=[pl.BlockSpec((tr, d), lambda i: (i, 0)), pl.BlockSpec((1, d), lambda i: (0, 0))],
        out_specs=pl.BlockSpec((tr, d), lambda i: (i, 0)),
        out_shape=jax.ShapeDtypeStruct((t, d), BF16),
        compiler_params=_params("parallel"),
    )(x, gain.reshape(1, d))


def _rmsnorm_bwd(x, gain, dy, dres, *, name, rows=512):
    t, d = x.shape
    tr = _tile(t, rows)

    def body(x_ref, g_ref, dy_ref, dres_ref, dx_ref, dg_ref):
        xv = x_ref[...]
        r = lax.rsqrt(jnp.mean(xv * xv, axis=-1, keepdims=True) + NORM_EPS)
        xh = xv * r
        dyv = dy_ref[...].astype(F32)
        dxh = dyv * g_ref[...]
        dx = r * (dxh - xh * jnp.mean(dxh * xh, axis=-1, keepdims=True))
        dx_ref[...] = dres_ref[...] + dx
        part = jnp.sum(dyv * xh, axis=0, keepdims=True)

        @pl.when(pl.program_id(0) == 0)
        def _():
            dg_ref[...] = part

        @pl.when(pl.program_id(0) > 0)
        def _():
            dg_ref[...] += part

    row = pl.BlockSpec((tr, d), lambda i: (i, 0))
    vec = pl.BlockSpec((1, d), lambda i: (0, 0))
    dx, dg = pl.pallas_call(
        body, name=name, grid=(t // tr,), in_specs=[row, vec, row, row], out_specs=[row, vec],
        out_shape=[jax.ShapeDtypeStruct((t, d), F32), jax.ShapeDtypeStruct((1, d), F32)],
        compiler_params=_params("arbitrary"),
    )(x, gain.reshape(1, d), dy, dres)
    return dx, dg.reshape(d)


def _final_loss(h, gain, target, *, rows=512):
    t, d = h.shape
    tr = _tile(t, rows)

    def body(x_ref, g_ref, tg_ref, dx_ref, dg_ref, loss_ref):
        xv = x_ref[...]
        g = g_ref[...]
        r = lax.rsqrt(jnp.mean(xv * xv, axis=-1, keepdims=True) + NORM_EPS)
        xh = xv * r
        err = xh * g - tg_ref[...]
        dyv = err * (1.0 / d)
        dxh = dyv * g
        dx_ref[...] = r * (dxh - xh * jnp.mean(dxh * xh, axis=-1, keepdims=True))
        part = jnp.sum(dyv * xh, axis=0, keepdims=True)
        lpart = jnp.zeros((1, LANES), F32) + (0.5 / d) * jnp.sum(err * err)

        @pl.when(pl.program_id(0) == 0)
        def _():
            dg_ref[...] = part
            loss_ref[...] = lpart

        @pl.when(pl.program_id(0) > 0)
        def _():
            dg_ref[...] += part
            loss_ref[...] += lpart

    row = pl.BlockSpec((tr, d), lambda i: (i, 0))
    vec = pl.BlockSpec((1, d), lambda i: (0, 0))
    dx, dg, loss = pl.pallas_call(
        body, name="final_loss", grid=(t // tr,), in_specs=[row, vec, row],
        out_specs=[row, vec, pl.BlockSpec((1, LANES), lambda i: (0, 0))],
        out_shape=[jax.ShapeDtypeStruct((t, d), F32), jax.ShapeDtypeStruct((1, d), F32),
                   jax.ShapeDtypeStruct((1, LANES), F32)],
        compiler_params=_params("arbitrary"),
    )(h, gain.reshape(1, d), target)
    return loss[0, 0], dx, dg.reshape(d)


def _ple_fwd(h, a, pp, *, name, rows=512):
    t, d = h.shape
    tr = _tile(t, rows)

    def body(h_ref, a_ref, p_ref, o_ref):
        o_ref[...] = h_ref[...] + _sigmoid(a_ref[...]) * p_ref[...]

    row = pl.BlockSpec((tr, d), lambda i: (i, 0))
    return pl.pallas_call(
        body, name=name, grid=(t // tr,), in_specs=[row, row, row], out_specs=row,
        out_shape=jax.ShapeDtypeStruct((t, d), F32), compiler_params=_params("parallel"),
    )(h, a, pp)


def _ple_bwd(dh, a, pp, *, name, rows=512):
    t, d = dh.shape
    tr = _tile(t, rows)

    def body(dh_ref, a_ref, p_ref, da_ref, dp_ref):
        s = _sigmoid(a_ref[...])
        dhv = dh_ref[...]
        da_ref[...] = (dhv * p_ref[...] * (s * (1.0 - s))).astype(BF16)
        dp_ref[...] = (dhv * s).astype(BF16)

    row = pl.BlockSpec((tr, d), lambda i: (i, 0))
    return pl.pallas_call(
        body, name=name, grid=(t // tr,), in_specs=[row, row, row], out_specs=[row, row],
        out_shape=[jax.ShapeDtypeStruct((t, d), BF16)] * 2, compiler_params=_params("parallel"),
    )(dh, a, pp)


CONV_ROWS = 64
CONV_HALO = 8


def _conv_window(ref, r0, with_prev, with_next):
    s = ref.shape[1]
    parts = []
    if with_prev:
        prev = ref[0, pl.ds(pl.multiple_of(jnp.maximum(r0 - CONV_HALO, 0), CONV_HALO), CONV_HALO), :]
        parts.append(jnp.where(r0 > 0, prev, 0.0))
    parts.append(ref[0, pl.ds(r0, CONV_ROWS), :])
    if with_next:
        nxt = pl.multiple_of(jnp.minimum(r0 + CONV_ROWS, s - CONV_HALO), CONV_HALO)
        parts.append(ref[0, pl.ds(nxt, CONV_HALO), :])
    return jnp.concatenate(parts, axis=0)


def _conv_taps(win, kw, n):
    return [win[CONV_HALO - (kw - 1 - k):CONV_HALO - (kw - 1 - k) + n] for k in range(kw)]


def _conv_apply(taps, wv, bv):
    pre = bv + wv[0:1, :] * taps[0]
    for k in range(1, len(taps)):
        pre = pre + wv[k:k + 1, :] * taps[k]
    return pre


def _rows8(x):
    acc = x[0:8]
    for i in range(1, x.shape[0] // 8):
        acc = acc + x[8 * i:8 * i + 8]
    return acc


def _conv_grad_step(dpre_ext, taps, wv, is_last):
    kw = wv.shape[0]
    halo = jnp.where(is_last, 0.0, dpre_ext[CONV_ROWS:])
    dpre_ext = jnp.concatenate([dpre_ext[:CONV_ROWS], halo], axis=0)
    dpre = dpre_ext[:CONV_ROWS]
    du = wv[kw - 1:kw, :] * dpre
    for k in range(kw - 1):
        du = du + wv[k:k + 1, :] * dpre_ext[kw - 1 - k:kw - 1 - k + CONV_ROWS]
    sums = [_rows8(dpre * taps[k][:CONV_ROWS]) for k in range(kw)] + [_rows8(dpre)]
    return du, sums


def _conv_store_sums(sums, dw_ref, db_ref, first):
    kw = len(sums) - 1
    vals = [jnp.sum(s_, axis=0, keepdims=True) for s_ in sums]

    @pl.when(first)
    def _():
        for k in range(kw):
            dw_ref[k:k + 1, :] = vals[k]
        db_ref[...] = vals[kw]

    @pl.when(jnp.logical_not(first))
    def _():
        for k in range(kw):
            dw_ref[k:k + 1, :] += vals[k]
        db_ref[...] += vals[kw]


def _dsilu(pre):
    s = _sigmoid(pre)
    return s, s * (1.0 + pre * (1.0 - s))


def _conv_silu_fwd(zx, off, w, b, *, name, tc=128):
    bsz, s, _ = zx.shape
    kw, c = w.shape
    o0 = off // tc

    def body(u_ref, w_ref, b_ref, o_ref):
        wv, bv = w_ref[...], b_ref[...]

        def step(i, carry):
            r0 = pl.multiple_of(i * CONV_ROWS, CONV_ROWS)
            taps = _conv_taps(_conv_window(u_ref, r0, True, False), kw, CONV_ROWS)
            pre = _conv_apply(taps, wv, bv)
            o_ref[0, pl.ds(r0, CONV_ROWS), :] = pre * _sigmoid(pre)
            return carry

        lax.fori_loop(0, s // CONV_ROWS, step, 0)

    return pl.pallas_call(
        body, name=name, grid=(bsz, c // tc),
        in_specs=[pl.BlockSpec((1, s, tc), lambda i, j: (i, 0, o0 + j)),
                  pl.BlockSpec((kw, tc), lambda i, j: (0, j)),
                  pl.BlockSpec((1, tc), lambda i, j: (0, j))],
        out_specs=pl.BlockSpec((1, s, tc), lambda i, j: (i, 0, j)),
        out_shape=jax.ShapeDtypeStruct((bsz, s, c), F32),
        compiler_params=_params("parallel", "parallel"),
    )(zx, w, b.reshape(1, c))


def _conv_silu_bwd(zx, off, w, b, douts, *, name, tc=128):
    bsz, s, _ = zx.shape
    kw, c = w.shape
    o0 = off // tc
    counts = [d.shape[2] // tc for d in douts]
    starts = [sum(counts[:k]) for k in range(len(douts))]
    assert sum(counts) == c // tc

    def body(u_ref, w_ref, b_ref, *rest):
        dy_refs = rest[:len(douts)]
        du_ref, dw_ref, db_ref = rest[len(douts):]
        j = pl.program_id(0)
        wv, bv = w_ref[...], b_ref[...]
        n = CONV_ROWS + CONV_HALO

        def step(i, sums):
            r0 = pl.multiple_of(i * CONV_ROWS, CONV_ROWS)
            taps = _conv_taps(_conv_window(u_ref, r0, True, True), kw, n)
            _, ds = _dsilu(_conv_apply(taps, wv, bv))
            dy = _conv_window(dy_refs[0], r0, False, True)
            for k in range(1, len(douts)):
                dy = jnp.where(j >= starts[k], _conv_window(dy_refs[k], r0, False, True), dy)
            du, new = _conv_grad_step(dy * ds, taps, wv, r0 + CONV_ROWS >= s)
            du_ref[0, pl.ds(r0, CONV_ROWS), :] = du.astype(BF16)
            return tuple(a + b_ for a, b_ in zip(sums, new))

        zero = tuple(jnp.zeros((8, tc), F32) for _ in range(kw + 1))
        sums = lax.fori_loop(0, s // CONV_ROWS, step, zero)
        _conv_store_sums(sums, dw_ref, db_ref, pl.program_id(1) == 0)

    def part_spec(k):
        return pl.BlockSpec((1, s, tc), lambda j, i: (i, 0, jnp.clip(j - starts[k], 0, counts[k] - 1)))

    du, dw, db = pl.pallas_call(
        body, name=name, grid=(c // tc, bsz),
        in_specs=[pl.BlockSpec((1, s, tc), lambda j, i: (i, 0, o0 + j)),
                  pl.BlockSpec((kw, tc), lambda j, i: (0, j)),
                  pl.BlockSpec((1, tc), lambda j, i: (0, j))] + [part_spec(k) for k in range(len(douts))],
        out_specs=[pl.BlockSpec((1, s, tc), lambda j, i: (i, 0, j)),
                   pl.BlockSpec((kw, tc), lambda j, i: (0, j)),
                   pl.BlockSpec((1, tc), lambda j, i: (0, j))],
        out_shape=[jax.ShapeDtypeStruct((bsz, s, c), BF16), jax.ShapeDtypeStruct((kw, c), F32),
                   jax.ShapeDtypeStruct((1, c), F32)],
        compiler_params=_params("parallel", "arbitrary"),
    )(zx, w, b.reshape(1, c), *douts)
    return du, dw, db.reshape(c)


def _conv_glu_fwd(up, w, b, *, name, tc=128):
    bsz, s, c2 = up.shape
    kw = w.shape[0]
    f = c2 // 2
    nt = f // tc

    def body(ug_ref, uv_ref, wg_ref, wv_ref, bg_ref, bv_ref, o_ref):
        wg, wv, bg, bv = wg_ref[...], wv_ref[...], bg_ref[...], bv_ref[...]

        def step(i, carry):
            r0 = pl.multiple_of(i * CONV_ROWS, CONV_ROWS)
            pg = _conv_apply(_conv_taps(_conv_window(ug_ref, r0, True, False), kw, CONV_ROWS), wg, bg)
            pv = _conv_apply(_conv_taps(_conv_window(uv_ref, r0, True, False), kw, CONV_ROWS), wv, bv)
            o_ref[0, pl.ds(r0, CONV_ROWS), :] = (pg * _sigmoid(pg) * pv).astype(BF16)
            return carry

        lax.fori_loop(0, s // CONV_ROWS, step, 0)

    b2 = b.reshape(1, c2)
    return pl.pallas_call(
        body, name=name, grid=(bsz, nt),
        in_specs=[pl.BlockSpec((1, s, tc), lambda i, j: (i, 0, j)),
                  pl.BlockSpec((1, s, tc), lambda i, j: (i, 0, nt + j)),
                  pl.BlockSpec((kw, tc), lambda i, j: (0, j)),
                  pl.BlockSpec((kw, tc), lambda i, j: (0, nt + j)),
                  pl.BlockSpec((1, tc), lambda i, j: (0, j)),
                  pl.BlockSpec((1, tc), lambda i, j: (0, nt + j))],
        out_specs=pl.BlockSpec((1, s, tc), lambda i, j: (i, 0, j)),
        out_shape=jax.ShapeDtypeStruct((bsz, s, f), BF16),
        compiler_params=_params("parallel", "parallel"),
    )(up, up, w, w, b2, b2)


def _conv_glu_bwd(up, w, b, df, *, name, tc=128):
    bsz, s, c2 = up.shape
    kw = w.shape[0]
    f = c2 // 2
    nt = f // tc

    def body(ug_ref, uv_ref, wg_ref, wv_ref, bg_ref, bv_ref, df_ref,
             dug_ref, duv_ref, dwg_ref, dwv_ref, dbg_ref, dbv_ref):
        first = pl.program_id(1) == 0
        wg, wv, bg, bv = wg_ref[...], wv_ref[...], bg_ref[...], bv_ref[...]
        n = CONV_ROWS + CONV_HALO

        def step(i, sums):
            r0 = pl.multiple_of(i * CONV_ROWS, CONV_ROWS)
            is_last = r0 + CONV_ROWS >= s
            tg = _conv_taps(_conv_window(ug_ref, r0, True, True), kw, n)
            tv = _conv_taps(_conv_window(uv_ref, r0, True, True), kw, n)
            pg = _conv_apply(tg, wg, bg)
            pv = _conv_apply(tv, wv, bv)
            sig, dsl = _dsilu(pg)
            dfv = _conv_window(df_ref, r0, False, True)
            dug, new_g = _conv_grad_step(dfv * pv * dsl, tg, wg, is_last)
            duv, new_v = _conv_grad_step(dfv * (pg * sig), tv, wv, is_last)
            dug_ref[0, pl.ds(r0, CONV_ROWS), :] = dug.astype(BF16)
            duv_ref[0, pl.ds(r0, CONV_ROWS), :] = duv.astype(BF16)
            return tuple(a + b_ for a, b_ in zip(sums, new_g + new_v))

        zero = tuple(jnp.zeros((8, tc), F32) for _ in range(2 * (kw + 1)))
        sums = lax.fori_loop(0, s // CONV_ROWS, step, zero)
        _conv_store_sums(sums[:kw + 1], dwg_ref, dbg_ref, first)
        _conv_store_sums(sums[kw + 1:], dwv_ref, dbv_ref, first)

    b2 = b.reshape(1, c2)
    act = lambda j, i: (i, 0, j)
    wsp = pl.BlockSpec((kw, tc), lambda j, i: (0, j))
    bsp = pl.BlockSpec((1, tc), lambda j, i: (0, j))
    dug, duv, dwg, dwv, dbg, dbv = pl.pallas_call(
        body, name=name, grid=(nt, bsz),
        in_specs=[pl.BlockSpec((1, s, tc), act),
                  pl.BlockSpec((1, s, tc), lambda j, i: (i, 0, nt + j)),
                  wsp, pl.BlockSpec((kw, tc), lambda j, i: (0, nt + j)),
                  bsp, pl.BlockSpec((1, tc), lambda j, i: (0, nt + j)),
                  pl.BlockSpec((1, s, tc), act)],
        out_specs=[pl.BlockSpec((1, s, tc), act), pl.BlockSpec((1, s, tc), act), wsp, wsp, bsp, bsp],
        out_shape=[jax.ShapeDtypeStruct((bsz, s, f), BF16)] * 2
        + [jax.ShapeDtypeStruct((kw, f), F32)] * 2 + [jax.ShapeDtypeStruct((1, f), F32)] * 2,
        compiler_params=_params("parallel", "arbitrary"),
    )(up, up, w, w, b2, b2, df)
    return (dug, duv, jnp.concatenate([dwg, dwv], axis=1),
            jnp.concatenate([dbg.reshape(f), dbv.reshape(f)]))


def _ssd_shared(xs, bm, cm, dtc_raw, dtr_raw, plane, psub, st):
    cl = SSM_CHUNK
    bias_l, a_l = plane[0:1, :], plane[1:2, :]
    bias_s, a_s = psub[:, 0:1], psub[:, 1:2]
    ri = lax.broadcasted_iota(I32, (cl, cl), 0)
    ci = lax.broadcasted_iota(I32, (cl, cl), 1)
    tril = ri >= ci
    low_incl = tril.astype(BF16)
    up_incl = (ri <= ci).astype(BF16)
    seg_t = (lax.broadcasted_iota(I32, (LANES, SSM_GROUP_W), 0)
             == lax.broadcasted_iota(I32, (LANES, SSM_GROUP_W), 1) // SSM_HEAD_DIM).astype(BF16)
    dt_c = _softplus(dtc_raw + bias_l)
    cs_c = _dot2_left(low_incl, dt_c * a_l)
    dt_r = _softplus(dtr_raw + bias_s)
    cs_r = _dot2(dt_r * a_s, up_incl)
    yield
    dt_ch = _dot2(dt_c, seg_t)
    cs_ch = _dot2(cs_c, seg_t)
    yield
    cs_last = cs_ch[cl - 1:cl, :]
    decay_ch = jnp.exp(cs_ch)
    w_ch = jnp.exp(cs_last - cs_ch)
    tot_ch = jnp.exp(cs_last)
    xdt = xs * dt_ch
    bm_b, cm_b = bm.astype(BF16), cm.astype(BF16)
    gmat = _dot_nt(cm_b, bm_b)
    cst = _dot(cm_b, st.astype(BF16))
    yield
    yoff = decay_ch * cst
    return dict(tril=tril, low_incl=low_incl, up_incl=up_incl, seg_t=seg_t, a_l=a_l, bias_l=bias_l,
                dt_c=dt_c, cs_c=cs_c, cs_r=cs_r, dt_ch=dt_ch, decay_ch=decay_ch, w_ch=w_ch,
                tot_ch=tot_ch, xdt=xdt, bm_b=bm_b, cm_b=cm_b, gmat=gmat, yoff=yoff)


def _head_decay(q, r):
    diff = q["cs_c"][:, r:r + 1] - q["cs_r"][r:r + 1, :]
    return jnp.where(q["tril"], jnp.exp(jnp.minimum(diff, 0.0)), 0.0)


def _half_mask(hh):
    lane = lax.broadcasted_iota(I32, (SSM_CHUNK, LANES), 1)
    return (lane < SSM_HEAD_DIM) if hh == 0 else (lane >= SSM_HEAD_DIM)


def _ssd_ydiag(q):
    pairs = []
    for pr in range(SSM_HEADS_PER_GROUP // 2):
        xp = q["xdt"][:, pr * LANES:(pr + 1) * LANES]
        acc = None
        for hh in range(2):
            mm_ = (q["gmat"] * _head_decay(q, 2 * pr + hh)).astype(BF16)
            part = _dot(mm_, jnp.where(_half_mask(hh), xp, 0.0).astype(BF16))
            acc = part if acc is None else acc + part
        pairs.append(acc)
        yield
    return jnp.concatenate(pairs, axis=1)


def _ssd_specs(bsz, s, g_n, d_inner, rev):
    cl = SSM_CHUNK
    nc = s // cl
    cc = (lambda c: nc - 1 - c) if rev else (lambda c: c)
    gb = d_inner // LANES
    dt0 = (d_inner + d_inner + 2 * g_n * SSM_STATE) // LANES
    gpb = next(n_ for n_ in (4, 2, 1) if all(v % n_ == 0 for v in (g_n, gb, dt0)))
    gw = SSM_GROUP_W
    specs = dict(
        z=pl.BlockSpec((1, cl, gw * gpb), lambda b, g, c: (b, cc(c), g)),
        dtc=pl.BlockSpec((1, cl, LANES * gpb), lambda b, g, c: (b, cc(c), dt0 // gpb + g)),
        xs=pl.BlockSpec((1, cl, gw * gpb), lambda b, g, c: (b, cc(c), g)),
        bm=pl.BlockSpec((1, cl, LANES * gpb), lambda b, g, c: (b, cc(c), gb // gpb + g)),
        cm=pl.BlockSpec((1, cl, LANES * gpb), lambda b, g, c: (b, cc(c), (gb + g_n) // gpb + g)),
        dtr=pl.BlockSpec((1, gpb, SSM_ROWS, cl), lambda b, g, c: (b, g, 0, cc(c))),
        plane=pl.BlockSpec((gpb, 8, LANES), lambda b, g, c: (g, 0, 0)),
        psub=pl.BlockSpec((gpb, SSM_ROWS, LANES), lambda b, g, c: (g, 0, 0)),
        chan=pl.BlockSpec((1, gw * gpb), lambda b, g, c: (0, g)),
        state=pl.BlockSpec((1, gpb, 1, SSM_STATE, gw), lambda b, g, c: (b, g, cc(c), 0, 0)),
        bgrp=pl.BlockSpec((1, cl, LANES * gpb), lambda b, g, c: (b, cc(c), g)),
        acc_ch=pl.BlockSpec((1, gpb, 8, gw), lambda b, g, c: (b, g, 0, 0)),
        acc_ln=pl.BlockSpec((1, gpb, 8, LANES), lambda b, g, c: (b, g, 0, 0)),
    )
    lanes = lambda w: (lambda ref, gg: ref.at[:, :, pl.ds(gg * w, w)])
    second = lambda ref, gg: ref.at[:, pl.ds(gg, 1)]
    first = lambda ref, gg: ref.at[pl.ds(gg, 1)]
    views = dict(z=lanes(gw), xs=lanes(gw), dtc=lanes(LANES), bm=lanes(LANES), cm=lanes(LANES),
                 bgrp=lanes(LANES), dtr=second, state=second, acc_ch=second, acc_ln=second,
                 plane=first, psub=first, chan=lambda ref, gg: ref.at[:, pl.ds(gg * gw, gw)],
                 scratch=lambda ref, gg: ref.at[gg])
    return specs, views, gpb


def _per_group(body, names, views, gpb):
    def run(*refs):
        _round_robin([body(*[views[nm](ref, gg) for nm, ref in zip(names, refs)]) for gg in range(gpb)])
    return run


def _ssd_fwd(zx, xbc, dtr_row, plane, psub, d_ch, nw, *, name):
    bsz, s, _ = zx.shape
    d_inner = d_ch.shape[1]
    g_n = d_inner // SSM_GROUP_W
    nc = s // SSM_CHUNK
    sp, views, gpb = _ssd_specs(bsz, s, g_n, d_inner, False)
    names = ["z", "dtc", "xs", "bm", "cm", "dtr", "plane", "psub", "chan", "chan", "z", "state", "scratch"]

    def body(z_ref, dtc_ref, xs_ref, bm_ref, cm_ref, dtr_ref, plane_ref, psub_ref, d_ref, nw_ref,
             gn_ref, st_out_ref, st_ref):
        @pl.when(pl.program_id(2) == 0)
        def _():
            st_ref[...] = jnp.zeros_like(st_ref)

        xs = xs_ref[0]
        st = st_ref[...]
        st_out_ref[0, 0, 0] = st
        q = yield from _ssd_shared(xs, bm_ref[0], cm_ref[0], dtc_ref[0], dtr_ref[0, 0], plane_ref[0],
                                   psub_ref[0], st)
        y = (yield from _ssd_ydiag(q)) + q["yoff"] + xs * d_ref[...]
        st_ref[...] = q["tot_ch"] * st + _dot_tn(q["bm_b"], (q["w_ch"] * q["xdt"]).astype(BF16))
        zv = z_ref[0]
        gy = y * (zv * _sigmoid(zv))
        rstd = lax.rsqrt(jnp.mean(gy * gy, axis=-1, keepdims=True) + SSM_NORM_EPS)
        gn_ref[0] = ((gy * rstd) * nw_ref[...]).astype(BF16)

    return pl.pallas_call(
        _per_group(body, names, views, gpb), name=name, grid=(bsz, g_n // gpb, nc),
        in_specs=[sp["z"], sp["dtc"], sp["xs"], sp["bm"], sp["cm"], sp["dtr"], sp["plane"],
                  sp["psub"], sp["chan"], sp["chan"]],
        out_specs=[sp["z"], sp["state"]],
        out_shape=[jax.ShapeDtypeStruct((bsz, s, d_inner), BF16),
                   jax.ShapeDtypeStruct((bsz, g_n, nc, SSM_STATE, SSM_GROUP_W), F32)],
        scratch_shapes=[pltpu.VMEM((gpb, SSM_STATE, SSM_GROUP_W), F32)],
        compiler_params=_params("parallel", "parallel", "arbitrary"),
    )(zx, zx, xbc, xbc, xbc, dtr_row, plane, psub, d_ch, nw)


def _ssd_bwd(zx, xbc, dtr_row, plane, psub, d_ch, nw, states, dgn, *, name):
    bsz, s, _ = zx.shape
    d_inner = d_ch.shape[1]
    g_n = d_inner // SSM_GROUP_W
    cl = SSM_CHUNK
    nc = s // cl
    sp, views, gpb = _ssd_specs(bsz, s, g_n, d_inner, True)
    acc_ch, acc_ln = sp["acc_ch"], sp["acc_ln"]
    names = ["z", "dtc", "xs", "bm", "cm", "dtr", "plane", "psub", "chan", "chan", "state", "z",
             "z", "bgrp", "bgrp", "z", "bgrp", "acc_ch", "acc_ln", "scratch"]

    def body(z_ref, dtc_ref, xs_ref, bm_ref, cm_ref, dtr_ref, plane_ref, psub_ref, d_ref, nw_ref,
             st_in_ref, dgn_ref,
             dxs_ref, dbm_ref, dcm_ref, dz_ref, ddt_ref, ach_ref, aln_ref, dst_ref):
        first = pl.program_id(2) == 0

        @pl.when(first)
        def _():
            dst_ref[...] = jnp.zeros_like(dst_ref)
            ach_ref[...] = jnp.zeros_like(ach_ref)
            aln_ref[...] = jnp.zeros_like(aln_ref)

        xs = xs_ref[0]
        st = st_in_ref[0, 0, 0]
        q = yield from _ssd_shared(xs, bm_ref[0], cm_ref[0], dtc_ref[0], dtr_ref[0, 0], plane_ref[0],
                                   psub_ref[0], st)
        d_chv = d_ref[...]
        nwv = nw_ref[...]
        y = (yield from _ssd_ydiag(q)) + q["yoff"] + xs * d_chv
        zv = z_ref[0]
        sz = _sigmoid(zv)
        silu_z = zv * sz
        gy = y * silu_z
        rstd = lax.rsqrt(jnp.mean(gy * gy, axis=-1, keepdims=True) + SSM_NORM_EPS)
        gyh = gy * rstd
        dgnv = dgn_ref[0]
        dgyh = dgnv * nwv
        dgy = rstd * (dgyh - gyh * jnp.mean(dgyh * gyh, axis=-1, keepdims=True))
        dy = dgy * silu_z
        dz_ref[0] = (dgy * y * (sz * (1.0 + zv * (1.0 - sz)))).astype(BF16)
        ach_ref[0, 0, 0:1, :] += jnp.sum(dgnv * gyh, axis=0, keepdims=True)
        ach_ref[0, 0, 1:2, :] += jnp.sum(dy * xs, axis=0, keepdims=True)
        yield
        st_b = st.astype(BF16)
        dyd = (dy * q["decay_ch"]).astype(BF16)
        dcm = _dot_nt(dyd, st_b)
        dstn = dst_ref[...]
        dstn_b = dstn.astype(BF16)
        bds = _dot(q["bm_b"], dstn_b)
        wx = q["w_ch"] * q["xdt"]
        dbm = _dot_nt(wx.astype(BF16), dstn_b)
        dst_ref[...] = q["tot_ch"] * dstn + _dot_tn(q["cm_b"], dyd)
        vterm = wx * bds
        cs_terms = dy * q["yoff"] - vterm
        last_ch = q["tot_ch"] * jnp.sum(dstn * st, axis=0, keepdims=True) + jnp.sum(vterm, axis=0, keepdims=True)
        yield
        lane = lax.broadcasted_iota(I32, (cl, LANES), 1)
        rowi = lax.broadcasted_iota(I32, (SSM_ROWS, cl), 0)
        dg_sum = jnp.zeros((cl, cl), F32)
        dcs_col = jnp.zeros((cl, LANES), F32)
        dcs_row = jnp.zeros((SSM_ROWS, cl), F32)
        dxdt_pairs = []
        for pr in range(SSM_HEADS_PER_GROUP // 2):
            xp_b = q["xdt"][:, pr * LANES:(pr + 1) * LANES].astype(BF16)
            dyp = dy[:, pr * LANES:(pr + 1) * LANES]
            acc = None
            for hh in range(2):
                r = 2 * pr + hh
                dm = _head_decay(q, r)
                mmat = q["gmat"] * dm
                dym = jnp.where(_half_mask(hh), dyp, 0.0).astype(BF16)
                dmat = jnp.where(q["tril"], _dot_nt(dym, xp_b), 0.0)
                part = _dot_tn(mmat.astype(BF16), dym)
                acc = part if acc is None else acc + part
                dg_sum = dg_sum + dmat * dm
                e = dmat * mmat
                dcs_col = dcs_col + jnp.where(lane == r, jnp.sum(e, axis=1, keepdims=True), 0.0)
                dcs_row = dcs_row + jnp.where(rowi == r, jnp.sum(e, axis=0, keepdims=True), 0.0)
            dxdt_pairs.append(acc)
            yield
        dg_b = dg_sum.astype(BF16)
        dcm_ref[0] = dcm + _dot(dg_b, q["bm_b"])
        dbm_ref[0] = dbm + _dot_tn(dg_b, q["cm_b"])
        dxdt = q["w_ch"] * bds + jnp.concatenate(dxdt_pairs, axis=1)
        dxs_ref[0] = dy * d_chv + dxdt * q["dt_ch"]
        yield
        seg = (lax.broadcasted_iota(I32, (SSM_GROUP_W, LANES), 0) // SSM_HEAD_DIM
               == lax.broadcasted_iota(I32, (SSM_GROUP_W, LANES), 1)).astype(BF16)
        row_as_col = jnp.transpose(jnp.concatenate(
            [dcs_row, jnp.zeros((cl - SSM_ROWS, cl), F32)], axis=0))
        dcs = dcs_col - row_as_col + _dot2(cs_terms, seg)
        last = _dot2(jnp.zeros((8, SSM_GROUP_W), F32) + last_ch, seg)[0:1, :]
        da = _dot2_left(q["up_incl"], dcs) + last
        ddt = _dot2(dxdt * xs, seg) + da * q["a_l"]
        ddtr = ddt * _sigmoid(dtc_ref[0] + q["bias_l"])
        ddt_ref[0] = ddtr.astype(BF16)
        aln_ref[0, 0, 0:1, :] += jnp.sum(ddtr, axis=0, keepdims=True)
        aln_ref[0, 0, 1:2, :] += jnp.sum(da * q["dt_c"], axis=0, keepdims=True)

    outs = pl.pallas_call(
        _per_group(body, names, views, gpb), name=name, grid=(bsz, g_n // gpb, nc),
        in_specs=[sp["z"], sp["dtc"], sp["xs"], sp["bm"], sp["cm"], sp["dtr"], sp["plane"],
                  sp["psub"], sp["chan"], sp["chan"], sp["state"], sp["z"]],
        out_specs=[sp["z"], sp["bgrp"], sp["bgrp"], sp["z"], sp["bgrp"], acc_ch, acc_ln],
        out_shape=[jax.ShapeDtypeStruct((bsz, s, d_inner), F32),
                   jax.ShapeDtypeStruct((bsz, s, g_n * SSM_STATE), F32),
                   jax.ShapeDtypeStruct((bsz, s, g_n * SSM_STATE), F32),
                   jax.ShapeDtypeStruct((bsz, s, d_inner), BF16),
                   jax.ShapeDtypeStruct((bsz, s, g_n * LANES), BF16),
                   jax.ShapeDtypeStruct((bsz, g_n, 8, SSM_GROUP_W), F32),
                   jax.ShapeDtypeStruct((bsz, g_n, 8, LANES), F32)],
        scratch_shapes=[pltpu.VMEM((gpb, SSM_STATE, SSM_GROUP_W), F32)],
        compiler_params=_params("parallel", "parallel", "arbitrary"),
    )(zx, zx, xbc, xbc, xbc, dtr_row, plane, psub, d_ch, nw, states, dgn)
    return outs


def _sb_stack(x):
    out = []
    for i in range(x.shape[0] // SB_BLOCK):
        xb = x[i * SB_BLOCK:(i + 1) * SB_BLOCK]
        lane = lax.broadcasted_iota(I32, xb.shape, 1)
        zero = jnp.zeros_like(xb)
        out += [jnp.where(lane < SB_HEAD_DIM, xb, zero), jnp.where(lane >= SB_HEAD_DIM, xb, zero)]
    return jnp.concatenate(out, axis=0)


def _sb_unstack_t(acc_t):
    row = lax.broadcasted_iota(I32, (LANES, SB_BLOCK), 0)
    out = []
    for i in range(acc_t.shape[1] // (2 * SB_BLOCK)):
        a = acc_t[:, 2 * i * SB_BLOCK:(2 * i + 1) * SB_BLOCK]
        b = acc_t[:, (2 * i + 1) * SB_BLOCK:(2 * i + 2) * SB_BLOCK]
        out.append(jnp.transpose(jnp.where(row < SB_HEAD_DIM, a, b)))
    return jnp.concatenate(out, axis=0)


def _sb_tile_blocks(nq, q_blocks):
    nb = 4 if nq % 4 == 0 else (2 if nq % 2 == 0 else 1)
    return nb, min(nb, q_blocks)


def _sb_valid(u, qi0, nb, nqb):
    shape = (nb * SB_BLOCK, nqb * 2 * SB_BLOCK)
    key = u * (nb * SB_BLOCK) + lax.broadcasted_iota(I32, shape, 0)
    col = lax.broadcasted_iota(I32, shape, 1)
    qpos = (qi0 + col // (2 * SB_BLOCK)) * SB_BLOCK + col % SB_BLOCK
    return key < qpos


def _sb_logits(kb, qs, valid):
    z = _dot_nt(kb, qs)
    lb = jnp.minimum(z, 0.0) - jnp.log(1.0 + jnp.exp(-jnp.abs(z)))
    lk_all = lb - z
    lk = lk_all if valid is None else jnp.where(valid, lk_all, 0.0)
    return z, lb, lk_all, lk


def _sb_diag(x):
    w2 = 2 * SB_BLOCK
    ri = lax.broadcasted_iota(I32, (SB_BLOCK, w2), 0)
    ci = lax.broadcasted_iota(I32, (SB_BLOCK, w2), 1) % SB_BLOCK
    first = jnp.where(ri < ci, x[:, :w2], 0.0)
    return first if x.shape[1] == w2 else jnp.concatenate([first, x[:, w2:]], axis=1)


def _sb_add_from(full, part, lo):
    if lo == 0:
        return full + part
    return jnp.concatenate([full[:, :lo], full[:, lo:] + part], axis=1)


def _sb_scan(tri2, x, nb, reverse, exact=True):
    blk = SB_BLOCK
    edge = 0 if reverse else blk - 1
    carry = jnp.zeros((1, x.shape[1]), F32)
    res = [None] * nb
    for i in (reversed(range(nb)) if reverse else range(nb)):
        part = x[i * blk:(i + 1) * blk]
        if exact:
            hi, lo = _split2(part)
            raw = _dot(tri2, jnp.concatenate([hi, lo], axis=0))
        else:
            raw = _dot(tri2[:, :blk], part.astype(BF16))
        res[i] = raw + carry
        carry = carry + (raw[edge:edge + 1] + part[edge:edge + 1])
    return jnp.concatenate(res, axis=0), carry


def _sb_fwd(q, kv, kvt, *, name):
    bsz, s, w = q.shape
    blk = SB_BLOCK
    npair = w // LANES
    nq = s // blk
    nb, nqb = _sb_tile_blocks(nq, SB_Q_BLOCKS_FWD)
    width = nqb * 2 * blk

    def body(q_ref, k_ref, vt_ref, o_ref, tot_ref):
        qi0 = pl.program_id(2) * nqb
        qs = _sb_stack(q_ref[0] * SB_SCALE)
        ri = lax.broadcasted_iota(I32, (blk, blk), 0)
        ci = lax.broadcasted_iota(I32, (blk, blk), 1)
        upper = (ri < ci).astype(BF16)
        tri2 = jnp.concatenate([upper, upper], axis=1)

        def tile_phases(u, carry, masked):
            rows = pl.ds(pl.multiple_of(u * (nb * blk), nb * blk), nb * blk)
            valid = _sb_valid(u, qi0, nb, nqb) if masked else None
            _, lb, _, lk = _sb_logits(k_ref[0, rows, :], qs, valid)
            yield
            sfx, total = _sb_scan(tri2, lk, nb, True)
            yield
            wgt = jnp.exp(lb + sfx + carry["r"])
            if masked:
                wgt = jnp.where(valid, wgt, 0.0)
            carry["r"] = carry["r"] + total
            wb = wgt.astype(BF16)
            yield
            for i in range(nb):
                carry["acc"] = carry["acc"] + _dot(vt_ref[0, 0, u * nb + i], wb[i * blk:(i + 1) * blk])

        def tile(us, r, acc, masked):
            carry = {"r": r, "acc": acc}
            _round_robin([tile_phases(u, carry, masked) for u in us])
            return carry["r"], carry["acc"]

        def top_tile(u):
            carry = {"r": jnp.zeros((1, width), F32), "acc": jnp.zeros((LANES, width), F32)}

            def block(kb):
                lo = kb * 2 * blk
                rows = pl.ds(pl.multiple_of((u * nb + kb) * blk, blk), blk)
                _, lb, lk_all, _ = _sb_logits(k_ref[0, rows, :], qs[lo:], None)
                lk = _sb_diag(lk_all)
                yield
                hi, lo_part = _split2(lk)
                raw = _dot(tri2, jnp.concatenate([hi, lo_part], axis=0))
                yield
                wgt = _sb_diag(jnp.exp(lb + raw + carry["r"][:, lo:]))
                carry["r"] = _sb_add_from(carry["r"], raw[0:1] + lk[0:1], lo)
                yield
                carry["acc"] = _sb_add_from(carry["acc"], _dot(vt_ref[0, 0, u * nb + kb], wgt.astype(BF16)), lo)

            _round_robin([block(kb) for kb in reversed(range(nb))])
            return carry["r"], carry["acc"]

        top = qi0 // nb
        zero_r, zero_acc = jnp.zeros((1, width), F32), jnp.zeros((LANES, width), F32)
        r, acc = top_tile(top) if nb == nqb else tile([top], zero_r, zero_acc, True)
        r, acc = lax.fori_loop(
            0, top // 2, lambda t, c: tile([top - 1 - 2 * t, top - 2 - 2 * t], c[0], c[1], False), (r, acc))
        r, acc = lax.fori_loop(0, top % 2, lambda t, c: tile([0], c[0], c[1], False), (r, acc))
        o_ref[0] = _sb_unstack_t(acc).astype(BF16)
        tot_ref[0, 0, 0] = r

    qspec = pl.BlockSpec((1, nqb * blk, LANES), lambda b, p, i: (b, i, p))
    return pl.pallas_call(
        body, name=name, grid=(bsz, npair, nq // nqb),
        in_specs=[qspec,
                  pl.BlockSpec((1, s, LANES), lambda b, p, i: (b, 0, p)),
                  pl.BlockSpec((1, 1, nq, LANES, blk), lambda b, p, i: (b, npair + p, 0, 0, 0))],
        out_specs=[qspec, pl.BlockSpec((1, 1, 1, 1, width), lambda b, p, i: (b, p, i, 0, 0))],
        out_shape=[jax.ShapeDtypeStruct((bsz, s, w), BF16),
                   jax.ShapeDtypeStruct((bsz, npair, nq // nqb, 1, width), F32)],
        compiler_params=_params("parallel", "parallel", "arbitrary"),
    )(q, kv, kvt)


def _kv_blocks_t(kv3):
    bsz, s, w2 = kv3.shape
    x = kv3.reshape(bsz, s // SB_BLOCK, SB_BLOCK, w2 // LANES, LANES)
    return jnp.transpose(x, (0, 3, 1, 4, 2))


def _sb_bwd(q, kv, kvt, do, tot, dk_in, dv_in, *, name):
    bsz, s, w = q.shape
    blk = SB_BLOCK
    npair = w // LANES
    nq = s // blk
    nb, nqb = _sb_tile_blocks(nq, SB_Q_BLOCKS_BWD)
    width = nqb * 2 * blk
    tot = tot.reshape(bsz, npair, nq // nqb, 1, width)
    has_init = dk_in is not None

    def body(*refs):
        if has_init:
            q_ref, k_ref, v_ref, kt_ref, do_ref, tot_ref, dki_ref, dvi_ref, dq_ref, dk_ref, dv_ref = refs
        else:
            q_ref, k_ref, v_ref, kt_ref, do_ref, tot_ref, dq_ref, dk_ref, dv_ref = refs
        qi0 = pl.program_id(2) * nqb

        @pl.when(qi0 == 0)
        def _():
            if has_init:
                dk_ref[...] = dki_ref[...]
                dv_ref[...] = dvi_ref[...]
            else:
                dk_ref[...] = jnp.zeros_like(dk_ref)
                dv_ref[...] = jnp.zeros_like(dv_ref)

        qs = _sb_stack(q_ref[0] * SB_SCALE)
        dos = _sb_stack(do_ref[0])
        totv = tot_ref[0, 0, 0]
        ri = lax.broadcasted_iota(I32, (blk, blk), 0)
        ci = lax.broadcasted_iota(I32, (blk, blk), 1)
        lower = (ri > ci).astype(BF16)
        tri2 = jnp.concatenate([lower, lower], axis=1)

        def tile_phases(u, carry, masked):
            rows = pl.ds(pl.multiple_of(u * (nb * blk), nb * blk), nb * blk)
            valid = _sb_valid(u, qi0, nb, nqb) if masked else None
            z, lb, lk_all, lk = _sb_logits(k_ref[0, rows, :], qs, valid)
            yield
            before, tot_lk = _sb_scan(tri2, lk, nb, False)
            yield
            wgt = jnp.exp(z + ((totv - carry["pre_lk"]) - before))
            if masked:
                wgt = jnp.where(valid, wgt, 0.0)
            carry["pre_lk"] = carry["pre_lk"] + tot_lk
            dlogit = _dot_nt(v_ref[0, rows, :], dos) * wgt
            yield
            dbefore, tot_d = _sb_scan(tri2, dlogit, nb, False, exact=False)
            yield
            sig = jnp.exp(lb)
            dz = dlogit * (1.0 - sig) - (carry["pre_d"] + dbefore) * sig
            if masked:
                dz = jnp.where(valid, dz, 0.0)
            carry["pre_d"] = carry["pre_d"] + tot_d
            dz_b = dz.astype(BF16)
            yield
            for i in range(nb):
                carry["dqt"] = carry["dqt"] + _dot(kt_ref[0, 0, u * nb + i], dz_b[i * blk:(i + 1) * blk])
            dk_ref[0, rows, :] += _dot(dz_b, qs)
            dv_ref[0, rows, :] += _dot(wgt.astype(BF16), dos)

        def tile(us, pre_lk, pre_d, dqt, masked):
            carry = {"pre_lk": pre_lk, "pre_d": pre_d, "dqt": dqt}
            _round_robin([tile_phases(u, carry, masked) for u in us])
            return carry["pre_lk"], carry["pre_d"], carry["dqt"]

        def top_tile(u, pre_lk, pre_d, dqt):
            carry = {"rest": totv - pre_lk, "pre_d": pre_d, "dqt": dqt}

            def block(kb):
                lo = kb * 2 * blk
                rows = pl.ds(pl.multiple_of((u * nb + kb) * blk, blk), blk)
                qs_k, dos_k = qs[lo:], dos[lo:]
                z, lb, lk_all, _ = _sb_logits(k_ref[0, rows, :], qs_k, None)
                lk = _sb_diag(lk_all)
                yield
                hi, lo_part = _split2(lk)
                raw = _dot(tri2, jnp.concatenate([hi, lo_part], axis=0))
                yield
                wgt = _sb_diag(jnp.exp(z + (carry["rest"][:, lo:] - raw)))
                carry["rest"] = _sb_add_from(carry["rest"], -(raw[blk - 1:blk] + lk[blk - 1:blk]), lo)
                dlogit = _dot_nt(v_ref[0, rows, :], dos_k) * wgt
                yield
                draw = _dot(tri2[:, :blk], dlogit.astype(BF16))
                yield
                sig = jnp.exp(lb)
                dz_b = _sb_diag(dlogit * (1.0 - sig) - (carry["pre_d"][:, lo:] + draw) * sig).astype(BF16)
                carry["pre_d"] = _sb_add_from(carry["pre_d"], draw[blk - 1:blk] + dlogit[blk - 1:blk], lo)
                yield
                carry["dqt"] = _sb_add_from(carry["dqt"], _dot(kt_ref[0, 0, u * nb + kb], dz_b), lo)
                dk_ref[0, rows, :] += _dot(dz_b, qs_k)
                dv_ref[0, rows, :] += _dot(wgt.astype(BF16), dos_k)

            _round_robin([block(kb) for kb in range(nb)])
            return carry["dqt"]

        zero = jnp.zeros((1, width), F32)
        top = qi0 // nb
        c = lax.fori_loop(0, top // 2, lambda t, c: tile([2 * t, 2 * t + 1], c[0], c[1], c[2], False),
                          (zero, zero, jnp.zeros((LANES, width), F32)))
        c = lax.fori_loop(0, top % 2, lambda t, c: tile([top - 1], c[0], c[1], c[2], False), c)
        dqt = top_tile(top, *c) if nb == nqb else tile([top], c[0], c[1], c[2], True)[2]
        dq_ref[0] = (_sb_unstack_t(dqt) * SB_SCALE).astype(BF16)

    qspec = pl.BlockSpec((1, nqb * blk, LANES), lambda b, p, i: (b, i, p))
    kspec = pl.BlockSpec((1, s, LANES), lambda b, p, i: (b, 0, p))
    vspec = pl.BlockSpec((1, s, LANES), lambda b, p, i: (b, 0, npair + p))
    ktspec = pl.BlockSpec((1, 1, nq, LANES, blk), lambda b, p, i: (b, p, 0, 0, 0))
    tspec = pl.BlockSpec((1, 1, 1, 1, width), lambda b, p, i: (b, p, i, 0, 0))
    in_specs = [qspec, kspec, vspec, ktspec, qspec, tspec] + ([kspec, kspec] if has_init else [])
    args = (q, kv, kv, kvt, do, tot) + ((dk_in, dv_in) if has_init else ())
    return pl.pallas_call(
        body, name=name, grid=(bsz, npair, nq // nqb), in_specs=in_specs,
        out_specs=[qspec, kspec, kspec],
        out_shape=[jax.ShapeDtypeStruct((bsz, s, w), BF16), jax.ShapeDtypeStruct((bsz, s, w), F32),
                   jax.ShapeDtypeStruct((bsz, s, w), F32)],
        compiler_params=_params("parallel", "parallel", "arbitrary"),
    )(*args)


ADAM_BLOCK_BYTES = 1 << 20


def _adamw(w, g, m, v, *, name):
    shape = w.shape
    r, c = shape[-2], shape[-1]
    lead = _size(shape[:-2])
    tr = r
    for cand in range(8, r, 8):
        if r % cand == 0 and cand * c * 4 <= ADAM_BLOCK_BYTES:
            tr = cand
    if r * c * 4 <= ADAM_BLOCK_BYTES:
        tr = r

    def body(w_ref, g_ref, m_ref, v_ref, d_ref, mo_ref, vo_ref):
        gv = g_ref[...]
        mn = ADAM_B1 * m_ref[...] + (1.0 - ADAM_B1) * gv
        vn = ADAM_B2 * v_ref[...] + (1.0 - ADAM_B2) * (gv * gv)
        m_hat = mn / (1.0 - ADAM_B1 ** ADAM_STEP)
        v_hat = vn / (1.0 - ADAM_B2 ** ADAM_STEP)
        d_ref[...] = -ADAM_LR * (m_hat / (jnp.sqrt(v_hat) + ADAM_EPS) + ADAM_WD * w_ref[...])
        mo_ref[...] = mn
        vo_ref[...] = vn

    blk = pl.BlockSpec((1, tr, c), lambda l, i: (l, i, 0))
    outs = pl.pallas_call(
        body, name=name, grid=(lead, r // tr), in_specs=[blk] * 4, out_specs=[blk] * 3,
        out_shape=[jax.ShapeDtypeStruct((lead, r, c), F32)] * 3,
        compiler_params=_params("parallel", "parallel"),
    )(*[a.reshape(lead, r, c) for a in (w, g, m, v)])
    return [o.reshape(shape) for o in outs]


def _row_tile(r, c, itemsize):
    if r * c * 4 <= ADAM_BLOCK_BYTES:
        return r
    step = 32 // itemsize
    tr = r
    for cand in range(step, r, step):
        if r % cand == 0 and cand * c * 4 <= ADAM_BLOCK_BYTES:
            tr = cand
    return tr


def _add_own_half(idx, g, recv, *, name):
    _, lh, r, c = recv.shape
    tr = _row_tile(r, c, g.dtype.itemsize)

    def body(idx_ref, a_ref, b_ref, o_ref):
        o_ref[...] = (a_ref[...].astype(F32) + b_ref[...].astype(F32)).astype(o_ref.dtype)

    blk = pl.BlockSpec((1, 1, tr, c), lambda k, l, i, idx: (k, l, i, 0))
    return pl.pallas_call(
        body, name=name,
        grid_spec=pltpu.PrefetchScalarGridSpec(
            num_scalar_prefetch=1, grid=(N_CHIPS, lh, r // tr),
            in_specs=[pl.BlockSpec((1, 1, tr, c), lambda k, l, i, idx: (k, idx[0] * lh + l, i, 0)), blk],
            out_specs=blk),
        out_shape=jax.ShapeDtypeStruct(recv.shape, g.dtype),
        compiler_params=_params("parallel", "parallel", "parallel"),
    )(idx, g, recv)


def _add_chips(idx, own, recv, *, name):
    _, lh, r, c = own.shape
    tr = _row_tile(r, c, own.dtype.itemsize)

    def body(idx_ref, a_ref, b_ref, o_ref):
        f = lambda v: v.astype(F32)
        o_ref[0] = ((f(a_ref[0, 0]) + f(b_ref[0, 0])) + f(b_ref[1, 0])) + f(b_ref[2, 0])

    return pl.pallas_call(
        body, name=name,
        grid_spec=pltpu.PrefetchScalarGridSpec(
            num_scalar_prefetch=1, grid=(lh, r // tr),
            in_specs=[pl.BlockSpec((1, 1, tr, c), lambda l, i, idx: (idx[0], l, i, 0)),
                      pl.BlockSpec((3, 1, tr, c), lambda l, i, idx: (0, l, i, 0))],
            out_specs=pl.BlockSpec((1, tr, c), lambda l, i, idx: (l, i, 0))),
        out_shape=jax.ShapeDtypeStruct((lh, r, c), F32),
        compiler_params=_params("parallel", "parallel"),
    )(idx, own, recv)


def _sum_devices(parts):
    _, r, _ = parts.shape

    def body(p_ref, o_ref):
        acc = p_ref[0]
        for k in range(1, N_DEV):
            acc = acc + p_ref[k]
        o_ref[...] = acc

    return pl.pallas_call(
        body, name="small_grad_sum", grid=(1,),
        in_specs=[pl.BlockSpec((N_DEV, r, LANES), lambda i: (0, 0, 0))],
        out_specs=pl.BlockSpec((r, LANES), lambda i: (0, 0)),
        out_shape=jax.ShapeDtypeStruct((r, LANES), F32),
    )(parts)


def _place():
    return lax.axis_index("x"), lax.axis_index("y"), lax.axis_index("c")


def _rcopy(src, dst, send_sems, recv_sems, k, to):
    return pltpu.make_async_remote_copy(src_ref=src, dst_ref=dst, send_sem=send_sems.at[k],
                                        recv_sem=recv_sems.at[k], device_id=to, device_id_type=MESH)


def _exchange_call(body, name, ins, out_shapes, n_sems):
    return pl.pallas_call(
        body, name=name, in_specs=[ANY] * len(ins), out_specs=[ANY] * len(out_shapes),
        out_shape=out_shapes,
        scratch_shapes=[pltpu.SemaphoreType.DMA((n_sems,)), pltpu.SemaphoreType.DMA((n_sems,))],
    )(*ins)


def _gather_weights(shards):
    n = len(shards)

    def body(*refs):
        ins, outs, send_sems, recv_sems = refs[:n], refs[n:2 * n], refs[2 * n], refs[2 * n + 1]
        x, y, c = _place()
        sibling = (x, y, 1 - c)
        chips = [(1 - x, y), (x, 1 - y), (1 - x, 1 - y)]

        def piece(i, px, py, pc):
            lh = ins[i].shape[0] // 2
            return outs[i].at[2 * px + py, pl.ds(pc * lh, lh)]

        def mine(i):
            lh = ins[i].shape[0] // 2
            return ins[i].at[pl.ds(c * lh, lh)]

        first = [_rcopy(mine(i), piece(i, x, y, c), send_sems, recv_sems, 6 * i + j, (*chip, c))
                 for i in range(n) for j, chip in enumerate(chips)]
        for cp in first:
            cp.start()
        passed = []
        for i in range(n):
            for j, chip in enumerate(chips):
                landed = piece(i, *chip, c)
                _rcopy(landed, landed, send_sems, recv_sems, 6 * i + j, (*chip, c)).wait_recv()
                passed.append(_rcopy(landed, landed, send_sems, recv_sems, 6 * i + 3 + j, sibling))
                passed[-1].start()
        for i in range(n):
            for j, chip in enumerate(chips):
                theirs = piece(i, *chip, 1 - c)
                _rcopy(theirs, theirs, send_sems, recv_sems, 6 * i + 3 + j, sibling).wait_recv()
        for cp in first + passed:
            cp.wait_send()

    shapes = [jax.ShapeDtypeStruct((N_CHIPS,) + s_.shape, s_.dtype) for s_ in shards]
    return _exchange_call(body, "gather_weights", shards, shapes, 6 * n)


def _swap_halves(gs):
    n = len(gs)

    def body(*refs):
        ins, outs, send_sems, recv_sems = refs[:n], refs[n:2 * n], refs[2 * n], refs[2 * n + 1]
        x, y, c = _place()
        cps = []
        for i in range(n):
            lh = ins[i].shape[1] // 2
            src = ins[i].at[pl.ds(0, N_CHIPS), pl.ds((1 - c) * lh, lh)]
            cps.append(_rcopy(src, outs[i], send_sems, recv_sems, i, (x, y, 1 - c)))
        for cp in cps:
            cp.start()
        for cp in cps:
            cp.wait()

    shapes = [jax.ShapeDtypeStruct((N_CHIPS, g.shape[1] // 2) + g.shape[2:], g.dtype) for g in gs]
    return _exchange_call(body, "grad_swap_halves", gs, shapes, n)


def _scatter_chunks(sums):
    n = len(sums)

    def body(*refs):
        ins, outs, send_sems, recv_sems = refs[:n], refs[n:2 * n], refs[2 * n], refs[2 * n + 1]
        x, y, c = _place()
        chips = [(1 - x, y), (x, 1 - y), (1 - x, 1 - y)]
        cps = [_rcopy(ins[i].at[2 * chip[0] + chip[1]], outs[i].at[j], send_sems, recv_sems, 3 * i + j,
                      (*chip, c)) for i in range(n) for j, chip in enumerate(chips)]
        for cp in cps:
            cp.start()
        for cp in cps:
            cp.wait()

    shapes = [jax.ShapeDtypeStruct((3,) + s_.shape[1:], s_.dtype) for s_ in sums]
    return _exchange_call(body, "grad_scatter_chunks", sums, shapes, 3 * n)


def _share_half(tots):
    n = len(tots)

    def body(*refs):
        ins, outs, send_sems, recv_sems = refs[:n], refs[n:2 * n], refs[2 * n], refs[2 * n + 1]
        x, y, c = _place()
        cps = [_rcopy(ins[i], outs[i], send_sems, recv_sems, i, (x, y, 1 - c)) for i in range(n)]
        for cp in cps:
            cp.start()
        for cp in cps:
            cp.wait()

    shapes = [jax.ShapeDtypeStruct(t_.shape, t_.dtype) for t_ in tots]
    return _exchange_call(body, "grad_share_half", tots, shapes, n)


def _exchange_small(r):
    rr, _ = r.shape

    def body(r_ref, out_ref, send_sems, recv_sems, local_sem):
        x, y, c = _place()
        me = 4 * x + 2 * y + c
        mine = pltpu.make_async_copy(r_ref, out_ref.at[me], local_sem)
        mine.start()
        cps = []
        for k in range(N_DEV - 1):
            fx, fy, fc = ((k + 1) >> 2) & 1, ((k + 1) >> 1) & 1, (k + 1) & 1
            to = (x ^ fx, y ^ fy, c ^ fc)
            cps.append((_rcopy(r_ref, out_ref.at[me], send_sems, recv_sems, k, to), to))
        for cp, _ in cps:
            cp.start()
        for k, (cp, to) in enumerate(cps):
            src = 4 * to[0] + 2 * to[1] + to[2]
            _rcopy(r_ref, out_ref.at[src], send_sems, recv_sems, k, to).wait_recv()
        for cp, _ in cps:
            cp.wait_send()
        mine.wait()

    return pl.pallas_call(
        body, name="small_grad_exchange", in_specs=[ANY], out_specs=ANY,
        out_shape=jax.ShapeDtypeStruct((N_DEV, rr, LANES), r.dtype),
        scratch_shapes=[pltpu.SemaphoreType.DMA((N_DEV - 1,)), pltpu.SemaphoreType.DMA((N_DEV - 1,)),
                        pltpu.SemaphoreType.DMA],
    )(r)


def _size(shape):
    n = 1
    for d in shape:
        n *= d
    return n


def _slab_rows(shape):
    rows = -(-_size(shape) // LANES)
    return -(-rows // SLAB_ROW_ALIGN) * SLAB_ROW_ALIGN


def _pack_rows(arrs, dtype, lead=0, unit=PACK_ROWS):
    parts, total = [], 0
    for a in arrs:
        front, shp = a.shape[:lead], a.shape[lead:]
        n, rows = _size(shp), _slab_rows(shp)
        nopad = [(0, 0)] * lead
        if n % LANES == 0:
            p = a.reshape(front + (n // LANES, LANES)).astype(dtype)
        else:
            p = jnp.pad(a.reshape(front + (n,)).astype(dtype), nopad + [(0, rows * LANES - n)])
            p = p.reshape(front + (rows, LANES))
        if p.shape[lead] != rows:
            p = jnp.pad(p, nopad + [(0, rows - p.shape[lead]), (0, 0)])
        parts.append(p)
        total += rows
    pad = (-total) % unit
    if pad:
        parts.append(jnp.zeros(parts[0].shape[:lead] + (pad, LANES), dtype))
    return jnp.concatenate(parts, axis=lead)


def _unpack_rows(slab, shapes):
    lead = slab.shape[:-2]
    out, off = [], 0
    for shp in shapes:
        n, rows = _size(shp), _slab_rows(shp)
        piece = slab[..., off:off + rows, :]
        if n % LANES == 0:
            piece = piece[..., :n // LANES, :].reshape(lead + tuple(shp))
        else:
            piece = piece.reshape(lead + (rows * LANES,))[..., :n].reshape(lead + tuple(shp))
        out.append(piece)
        off += rows
    return out


def _ffn_fwd(h, bsz, s, gain, w_up, cw, cb, w_down, i):
    hf = _rmsnorm_fwd(h, gain, name=f"ffn_norm_{i}")
    up = _mm(hf, w_up, name=f"ffn_up_{i}")
    up3 = up.reshape(bsz, s, -1)
    f = _conv_glu_fwd(up3, cw, cb, name=f"ffn_glu_{i}").reshape(h.shape[0], -1)
    h2 = _mm(f, w_down, add=h, name=f"ffn_down_{i}")
    return h2, (h, hf, up3, f)


def _ffn_bwd(dh, saved, gain, w_up, cw, cb, w_down, i):
    h, hf, up3, f = saved
    t = h.shape[0]
    d_down = _mm(f, dh, ta=True, name=f"ffn_down_dw_{i}")
    df = _mm(dh, w_down, tb=True, name=f"ffn_down_dx_{i}")
    dug, duv, dcw, dcb = _conv_glu_bwd(up3, cw, cb, df.reshape(up3.shape[0], up3.shape[1], -1),
                                       name=f"ffn_glu_bwd_{i}")
    dug, duv = dug.reshape(t, -1), duv.reshape(t, -1)
    fdim = dug.shape[1]
    d_up = jnp.concatenate([_mm(hf, dug, ta=True, name=f"ffn_up_dwg_{i}"),
                            _mm(hf, duv, ta=True, name=f"ffn_up_dwv_{i}")], axis=1)
    dhf = _mm(dug, w_up[:, :fdim], tb=True, name=f"ffn_up_dxg_{i}")
    dh, dgain = _mm_norm_bwd(duv, w_up[:, fdim:], h, gain, dh, add=dhf, name=f"ffn_up_dxv_{i}")
    return dh, dgain, d_up, dcw, dcb, d_down


def _ple_layer_fwd(h, p_i, gain, w_gate, w_proj, i):
    hp = _rmsnorm_fwd(h, gain, name=f"ple_norm_{i}")
    a = _mm(hp, w_gate, name=f"ple_gate_{i}")
    pp = _mm(p_i, w_proj, name=f"ple_proj_{i}")
    return _ple_fwd(h, a, pp, name=f"ple_mix_{i}"), (h, hp, a, pp)


def _ple_layer_bwd(dh, saved, p_i, gain, w_gate, i):
    h, hp, a, pp = saved
    da, dpp = _ple_bwd(dh, a, pp, name=f"ple_mix_bwd_{i}")
    d_gate = _mm(hp, da, ta=True, name=f"ple_gate_dw_{i}")
    d_proj = _mm(p_i, dpp, ta=True, name=f"ple_proj_dw_{i}")
    dh, dgain = _mm_norm_bwd(da, w_gate, h, gain, dh, name=f"ple_gate_dx_{i}")
    return dh, dgain, d_gate, d_proj


def _ssm_consts(dt_bias, a_log, d_skip, g_n):
    hpg = SSM_HEADS_PER_GROUP
    a = -jnp.exp(a_log)
    rows = jnp.stack([dt_bias.reshape(g_n, hpg), a.reshape(g_n, hpg)], axis=1)
    plane = jnp.zeros((g_n, 8, LANES), F32).at[:, 0:2, 0:hpg].set(rows)
    psub = jnp.zeros((g_n, SSM_ROWS, LANES), F32).at[:, 0:hpg, 0:2].set(jnp.swapaxes(rows, 1, 2))
    d_ch = jnp.repeat(d_skip, SSM_HEAD_DIM).reshape(1, -1)
    return a, plane, psub, d_ch


def _ssm_in_big(w_in, d_inner, g_n):
    d = w_in.shape[0]
    cut = w_in.shape[1] - g_n * SSM_HEADS_PER_GROUP
    wdt = w_in[:, cut:].reshape(d, g_n, SSM_HEADS_PER_GROUP)
    wdt = jnp.pad(wdt, ((0, 0), (0, 0), (0, LANES - SSM_HEADS_PER_GROUP))).reshape(d, g_n * LANES)
    return jnp.concatenate([w_in[:, :cut], wdt], axis=1)


def _ssm_in_small(dw_big, g_n):
    d = dw_big.shape[0]
    cut = dw_big.shape[1] - g_n * LANES
    ddt = dw_big[:, cut:].reshape(d, g_n, LANES)[:, :, :SSM_HEADS_PER_GROUP].reshape(d, -1)
    return jnp.concatenate([dw_big[:, :cut], ddt], axis=1)


def _ssm_fwd(h, bsz, s, gain, w_in_big, cw, cb, plane, psub, d_ch, nw, w_out, i):
    d_inner = d_ch.shape[1]
    g_n = d_inner // SSM_GROUP_W
    conv_dim = cw.shape[1]
    hn = _rmsnorm_fwd(h, gain, name=f"attn_norm_{i}")
    zx = _mm(hn, w_in_big, name=f"ssm_in_{i}").reshape(bsz, s, -1)
    xbc = _conv_silu_fwd(zx, d_inner, cw, cb, name=f"ssm_conv_{i}")
    dtr = zx[:, :, d_inner + conv_dim:].reshape(bsz, s, g_n, LANES)[..., :SSM_HEADS_PER_GROUP]
    dtr_row = jnp.pad(jnp.transpose(dtr, (0, 2, 3, 1)),
                      ((0, 0), (0, 0), (0, SSM_ROWS - SSM_HEADS_PER_GROUP), (0, 0)))
    gn, states = _ssd_fwd(zx, xbc, dtr_row, plane, psub, d_ch, nw, name=f"ssd_{i}")
    gn2 = gn.reshape(h.shape[0], -1)
    h1 = _mm(gn2, w_out, add=h, name=f"ssm_out_{i}")
    return h1, (h, hn, zx, xbc, dtr_row, states, gn2)


def _ssm_bwd(dh, saved, gain, w_in_big, cw, cb, plane, psub, d_ch, nw, w_out, i):
    h, hn, zx, xbc, dtr_row, states, gn2 = saved
    t = h.shape[0]
    bsz, s, _ = zx.shape
    d_inner = d_ch.shape[1]
    d_out = _mm(gn2, dh, ta=True, name=f"ssm_out_dw_{i}")
    dgn = _mm(dh, w_out, tb=True, name=f"ssm_out_dx_{i}").reshape(bsz, s, -1)
    dxs, dbm, dcm, dz, ddtr, ach, aln = _ssd_bwd(zx, xbc, dtr_row, plane, psub, d_ch, nw, states, dgn,
                                                  name=f"ssd_bwd_{i}")
    dxbc, dcw, dcb = _conv_silu_bwd(zx, d_inner, cw, cb, [dxs, dbm, dcm], name=f"ssm_conv_bwd_{i}")
    d_in_parts, dhn, col = [], None, 0
    for tag, part in (("z", dz), ("dt", ddtr), ("xbc", dxbc)):
        part = part.reshape(t, -1)
        col = {"z": 0, "xbc": dz.shape[-1], "dt": dz.shape[-1] + dxbc.shape[-1]}[tag]
        w_part = w_in_big[:, col:col + part.shape[1]]
        d_in_parts.append(_mm(hn, part, ta=True, name=f"ssm_in_dw_{tag}_{i}"))
        if tag == "xbc":
            dh, dgain = _mm_norm_bwd(part, w_part, h, gain, dh, add=dhn, name=f"ssm_in_dx_{tag}_{i}")
        else:
            dhn = _mm(part, w_part, tb=True, add=dhn, name=f"ssm_in_dx_{tag}_{i}")
    d_in_big = jnp.concatenate([d_in_parts[0], d_in_parts[2], d_in_parts[1]], axis=1)
    hpg = SSM_HEADS_PER_GROUP
    ach = jnp.sum(ach, axis=0)
    aln = jnp.sum(aln, axis=0)
    d_nw = ach[:, 0, :].reshape(-1)
    d_dskip = jnp.sum(ach[:, 1, :].reshape(-1, SSM_HEAD_DIM), axis=1)
    d_bias = aln[:, 0, :hpg].reshape(-1)
    d_a = aln[:, 1, :hpg].reshape(-1)
    return dh, dgain, d_in_big, dcw, dcb, d_bias, d_a, d_dskip, d_nw, d_out


def _sb_layer_fwd(h, bsz, s, gain, w_q, w_o, kv3, kvt, i):
    hn = _rmsnorm_fwd(h, gain, name=f"attn_norm_{i}")
    q3 = _mm(hn, w_q, out_dtype=BF16, name=f"sb_q_{i}").reshape(bsz, s, -1)
    o3, tot = _sb_fwd(q3, kv3, kvt, name=f"sb_attn_{i}")
    o2 = o3.reshape(h.shape[0], -1)
    h1 = _mm(o2, w_o, add=h, name=f"sb_o_{i}")
    return h1, (h, hn, q3, o2, tot)


def _sb_layer_bwd(dh, saved, gain, w_q, w_o, kv3, kvt, dk, dv, i):
    h, hn, q3, o2, tot = saved
    t = h.shape[0]
    d_o = _mm(o2, dh, ta=True, name=f"sb_o_dw_{i}")
    do3 = _mm(dh, w_o, tb=True, out_dtype=BF16, name=f"sb_o_dx_{i}").reshape(q3.shape)
    dq3, dk, dv = _sb_bwd(q3, kv3, kvt, do3, tot, dk, dv, name=f"sb_attn_bwd_{i}")
    dq = dq3.reshape(t, -1)
    d_q = _mm(hn, dq, ta=True, name=f"sb_q_dw_{i}")
    dh, dgain = _mm_norm_bwd(dq, w_q, h, gain, dh, name=f"sb_q_dx_{i}")
    return dh, dgain, d_q, d_o, dk, dv


def kernel(x, p, attn_norm, ffn_norm, ple_norm, ssm_in_proj, ssm_conv_w, ssm_conv_b, ssm_dt_bias, ssm_a_log, ssm_d, ssm_norm, ssm_out_proj, kv_norm, w_kv, w_q, w_o, ffn_up, ffn_conv_w, ffn_conv_b, ffn_down, ple_gate, ple_proj, final_norm, loss_target, m_attn_norm, m_ffn_norm, m_ple_norm, m_ssm_in_proj, m_ssm_conv_w, m_ssm_conv_b, m_ssm_dt_bias, m_ssm_a_log, m_ssm_d, m_ssm_norm, m_ssm_out_proj, m_kv_norm, m_w_kv, m_w_q, m_w_o, m_ffn_up, m_ffn_conv_w, m_ffn_conv_b, m_ffn_down, m_ple_gate, m_ple_proj, m_final_norm, v_attn_norm, v_ffn_norm, v_ple_norm, v_ssm_in_proj, v_ssm_conv_w, v_ssm_conv_b, v_ssm_dt_bias, v_ssm_a_log, v_ssm_d, v_ssm_norm, v_ssm_out_proj, v_kv_norm, v_w_kv, v_w_q, v_w_o, v_ffn_up, v_ffn_conv_w, v_ffn_conv_b, v_ffn_down, v_ple_gate, v_ple_proj, v_final_norm):
    given = dict(locals())
    wl = {n: given[n] for n in WEIGHTS}
    bsz, s, d = x.shape
    t = bsz * s
    depth = attn_norm.shape[0]
    n_a = ssm_in_proj.shape[0]
    d_inner = ssm_norm.shape[1] * N_CHIPS
    g_n = d_inner // SSM_GROUP_W
    cidx = lax.axis_index("c").astype(I32).reshape(1)
    chip_idx = (2 * lax.axis_index("x") + lax.axis_index("y")).astype(I32).reshape(1)

    big = [n for n in SHARDED if _size(wl[n].shape) >= BIG_WEIGHT]
    small = [n for n in SHARDED if n not in big]
    small_shapes = [wl[n].shape for n in small]
    halves = lambda shp: shp if len(shp) == 3 else (2, shp[0] // 2, shp[1])
    small_slab = _pack_rows([wl[n] for n in small], BF16, unit=2 * SLAB_ROW_ALIGN)
    small_rows = small_slab.shape[0]
    mine = [wl[n].astype(BF16).reshape(halves(wl[n].shape)) for n in big]
    mine.append(small_slab.reshape(2, small_rows // 2, LANES))
    gathered = [lax.dynamic_update_index_in_dim(g, m_, chip_idx[0], 0)
                for g, m_ in zip(_gather_weights(mine), mine)]
    per_chip = {n: g.reshape((N_CHIPS,) + wl[n].shape) for n, g in zip(big, gathered)}
    per_chip.update(zip(small, _unpack_rows(gathered[-1].reshape(N_CHIPS, small_rows, LANES), small_shapes)))
    full = {}
    for n in SHARDED:
        ax, piece = SHARD_AXIS[n], per_chip[n]
        merged = piece.shape[1:ax + 1] + (N_CHIPS * piece.shape[ax + 1],) + piece.shape[ax + 2:]
        full[n] = jnp.moveaxis(piece, 0, ax).reshape(merged)

    h = x.reshape(t, d)
    tgt = loss_target.reshape(t, d)
    saved = []
    kv3 = kvt = hkv = h_kv_in = None
    consts = []
    for i in range(depth):
        if i < n_a:
            a_neg, plane, psub, d_ch = _ssm_consts(ssm_dt_bias[i], ssm_a_log[i], ssm_d[i], g_n)
            w_in_big = _ssm_in_big(full["ssm_in_proj"][i], d_inner, g_n)
            cw = full["ssm_conv_w"][i].astype(F32)
            cb = full["ssm_conv_b"][i].astype(F32)
            nw = full["ssm_norm"][i].astype(F32).reshape(1, -1)
            consts.append((a_neg, plane, psub, d_ch, w_in_big, cw, cb, nw))
            h, sv_mix = _ssm_fwd(h, bsz, s, attn_norm[i], w_in_big, cw, cb, plane, psub, d_ch, nw,
                                 full["ssm_out_proj"][i], i)
        else:
            j = i - n_a
            h, sv_mix = _sb_layer_fwd(h, bsz, s, attn_norm[i], full["w_q"][j], full["w_o"][j], kv3, kvt, i)
        fcw = full["ffn_conv_w"][i].astype(F32)
        h, sv_ffn = _ffn_fwd(h, bsz, s, ffn_norm[i], full["ffn_up"][i], fcw, ffn_conv_b[i],
                             full["ffn_down"][i], i)
        p_i = p[i].reshape(t, -1)
        h, sv_ple = _ple_layer_fwd(h, p_i, ple_norm[i], full["ple_gate"][i], full["ple_proj"][i], i)
        saved.append((sv_mix, sv_ffn, sv_ple))
        if i == n_a - 1:
            h_kv_in = h
            hkv = _rmsnorm_fwd(h, kv_norm, name="kv_norm")
            kv3 = _mm(hkv, full["w_kv"], out_dtype=BF16, name="kv_proj").reshape(bsz, s, -1)
            kvt = _kv_blocks_t(kv3)

    loss_local, dh, g_final = _final_loss(h, final_norm, tgt)
    gr = {n: [None] * wl[n].shape[0] for n in WEIGHTS if n not in ("kv_norm", "w_kv", "final_norm")}
    gr["final_norm"] = g_final
    dk = dv = None
    for i in reversed(range(depth)):
        sv_mix, sv_ffn, sv_ple = saved[i]
        if i == n_a - 1:
            dkv = jnp.concatenate([dk, dv], axis=-1).reshape(t, -1)
            gr["w_kv"] = _mm(hkv, dkv, ta=True, name="kv_proj_dw")
            dh, gr["kv_norm"] = _mm_norm_bwd(dkv, full["w_kv"], h_kv_in, kv_norm, dh, name="kv_proj_dx")
        p_i = p[i].reshape(t, -1)
        dh, gr["ple_norm"][i], gr["ple_gate"][i], gr["ple_proj"][i] = _ple_layer_bwd(
            dh, sv_ple, p_i, ple_norm[i], full["ple_gate"][i], i)
        fcw = full["ffn_conv_w"][i].astype(F32)
        (dh, gr["ffn_norm"][i], gr["ffn_up"][i], gr["ffn_conv_w"][i], gr["ffn_conv_b"][i],
         gr["ffn_down"][i]) = _ffn_bwd(dh, sv_ffn, ffn_norm[i], full["ffn_up"][i], fcw, ffn_conv_b[i],
                                       full["ffn_down"][i], i)
        if i < n_a:
            a_neg, plane, psub, d_ch, w_in_big, cw, cb, nw = consts[i]
            (dh, gr["attn_norm"][i], d_in_big, gr["ssm_conv_w"][i], gr["ssm_conv_b"][i],
             gr["ssm_dt_bias"][i], d_a, gr["ssm_d"][i], gr["ssm_norm"][i],
             gr["ssm_out_proj"][i]) = _ssm_bwd(dh, sv_mix, attn_norm[i], w_in_big, cw, cb, plane, psub,
                                               d_ch, nw, full["ssm_out_proj"][i], i)
            gr["ssm_in_proj"][i] = _ssm_in_small(d_in_big, g_n)
            gr["ssm_a_log"][i] = d_a * a_neg
        else:
            j = i - n_a
            dh, gr["attn_norm"][i], gr["w_q"][j], gr["w_o"][j], dk, dv = _sb_layer_bwd(
                dh, sv_mix, attn_norm[i], full["w_q"][j], full["w_o"][j], kv3, kvt, dk, dv, i)
    grad_x = dh.reshape(bsz, s, d)
    gfull = {n: (jnp.stack(v) if isinstance(v, list) else v) for n, v in gr.items()}

    by_chip = {}
    for n in SHARDED:
        ax, shp = SHARD_AXIS[n], gfull[n].shape
        split = gfull[n].reshape(shp[:ax] + (N_CHIPS, shp[ax] // N_CHIPS) + shp[ax + 1:])
        by_chip[n] = jnp.moveaxis(split, ax, 0)
    g4 = [by_chip[n].astype(BF16).reshape((N_CHIPS,) + halves(wl[n].shape)) for n in big]
    g4.append(_pack_rows([by_chip[n] for n in small], BF16, lead=1, unit=2 * SLAB_ROW_ALIGN)
              .reshape(N_CHIPS, 2, small_rows // 2, LANES))
    tags = big + ["small"]
    from_sibling = _swap_halves(g4)
    chip_sums = [_add_own_half(cidx, g, r_, name="grad_pair_sum_" + tg)
                 for g, r_, tg in zip(g4, from_sibling, tags)]
    from_chips = _scatter_chunks(chip_sums)
    my_half = [_add_chips(chip_idx, s_, r_, name="grad_chip_sum_" + tg)
               for s_, r_, tg in zip(chip_sums, from_chips, tags)]
    other_half = _share_half(my_half)
    low_core = cidx[0] == 0
    reduced = [jnp.concatenate([jnp.where(low_core, a, b_), jnp.where(low_core, b_, a)], axis=0)
               for a, b_ in zip(my_half, other_half)]
    grads = {n: g.reshape(wl[n].shape) for n, g in zip(big, reduced)}
    grads.update(zip(small, _unpack_rows(reduced[-1].reshape(small_rows, LANES), small_shapes)))

    rep_shapes = [wl[n].shape for n in REPLICATED]
    packed_r = _pack_rows([gfull[n] for n in REPLICATED], F32, unit=SLAB_ROW_ALIGN)
    g_rep = _sum_devices(_exchange_small(packed_r))

    delta, new_m, new_v = {}, {}, {}
    for n in SHARDED:
        delta[n], new_m[n], new_v[n] = _adamw(wl[n], grads[n], given["m_" + n], given["v_" + n],
                                              name="adamw_" + n)
    slabs = [_pack_rows([src[pre + n] for n in REPLICATED], F32, unit=SLAB_ROW_ALIGN)
             for src, pre in ((wl, ""), (given, "m_"), (given, "v_"))]
    rep_out = _adamw(slabs[0], g_rep, slabs[1], slabs[2], name="adamw_replicated")
    for dst, slab in zip((grads, delta, new_m, new_v), [g_rep] + list(rep_out)):
        dst.update(zip(REPLICATED, _unpack_rows(slab, rep_shapes)))
    loss = lax.psum(loss_local, ("x", "y", "c"))
    return (loss, grad_x, *[grads[n] for n in WEIGHTS], *[delta[n] for n in WEIGHTS],
            *[new_m[n] for n in WEIGHTS], *[new_v[n] for n in WEIGHTS])
```

```python
import functools

import jax
import jax.numpy as jnp
from jax import lax
from jax.experimental import pallas as pl
from jax.experimental.pallas import tpu as pltpu

F32 = jnp.float32
BF16 = jnp.bfloat16
I32 = jnp.int32

NORM_EPS = 1e-6
SSM_NORM_EPS = 1e-5
SSM_HEAD_DIM = 64
SSM_STATE = 128
SSM_CHUNK = 128
SSM_HEADS_PER_GROUP = 8
SSM_GROUP_W = SSM_HEADS_PER_GROUP * SSM_HEAD_DIM
SSM_CONV = 4
SSM_ROWS = 16
SB_HEAD_DIM = 64
SB_BLOCK = 128
SB_SCALE = SB_HEAD_DIM ** -0.5
SB_Q_BLOCKS_FWD = 4
SB_Q_BLOCKS_BWD = 4
FFN_CONV = 3
LANES = 128
N_CHIPS = 4
N_DEV = 8

ADAM_LR = 0.001
ADAM_B1 = 0.9
ADAM_B2 = 0.999
ADAM_EPS = 1e-08
ADAM_WD = 0.01
ADAM_STEP = 10

MESH = pl.DeviceIdType.MESH
ANY = pl.BlockSpec(memory_space=pl.ANY)

SHARD_AXIS = {
    "ssm_in_proj": 2, "ssm_conv_w": 2, "ssm_conv_b": 1, "ssm_norm": 1, "ssm_out_proj": 1,
    "w_kv": 1, "w_q": 1, "w_o": 1, "ffn_up": 2, "ffn_conv_w": 2, "ffn_down": 1,
    "ple_gate": 1, "ple_proj": 2,
}
REPLICATED = ["attn_norm", "ffn_norm", "ple_norm", "ssm_dt_bias", "ssm_a_log", "ssm_d",
              "kv_norm", "ffn_conv_b", "final_norm"]
WEIGHTS = ["attn_norm", "ffn_norm", "ple_norm", "ssm_in_proj", "ssm_conv_w", "ssm_conv_b",
           "ssm_dt_bias", "ssm_a_log", "ssm_d", "ssm_norm", "ssm_out_proj", "kv_norm", "w_kv",
           "w_q", "w_o", "ffn_up", "ffn_conv_w", "ffn_conv_b", "ffn_down", "ple_gate",
           "ple_proj", "final_norm"]
SHARDED = [n for n in WEIGHTS if n in SHARD_AXIS]
PACK_ROWS = 2048
SLAB_ROW_ALIGN = 16
BIG_WEIGHT = 1 << 17


def _tile(n, pref):
    t = (min(pref, n) // 128) * 128
    while t >= 128:
        if n % t == 0:
            return t
        t -= 128
    return n


def _dot(a, b):
    return jnp.dot(a, b, preferred_element_type=F32)


def _dot_nt(a, b):
    return lax.dot_general(a, b, (((1,), (1,)), ((), ())), preferred_element_type=F32)


def _dot_tn(a, b):
    return lax.dot_general(a, b, (((0,), (0,)), ((), ())), preferred_element_type=F32)


def _split2(x):
    hi = x.astype(BF16)
    lo = (x - hi.astype(F32)).astype(BF16)
    return hi, lo


def _dot2(x, m):
    hi, lo = _split2(x)
    return _dot(hi, m) + _dot(lo, m)


def _dot2_left(m, x):
    hi, lo = _split2(x)
    return _dot(m, hi) + _dot(m, lo)


def _softplus(x):
    return jnp.maximum(x, 0.0) + jnp.log(1.0 + jnp.exp(-jnp.abs(x)))


def _sigmoid(x):
    return 0.5 * jnp.tanh(0.5 * x) + 0.5


def _params(*sem):
    return pltpu.CompilerParams(dimension_semantics=sem)


def _round_robin(gens):
    live = list(gens)
    while live:
        still = []
        for gen in live:
            try:
                next(gen)
                still.append(gen)
            except StopIteration:
                pass
        live = still


MM_VMEM_BUDGET = 36 * 1024 * 1024
MM_FULL_K = 2816


def _mm_tiles(m, n, k, sa, sb, so, has_add, extra=0):
    tk = k if k <= MM_FULL_K else _tile(k, 1024)
    tn = _tile(n, 1408)
    tm = _tile(m, 1408)

    def need(tm_):
        return (2 * tm_ * tk * sa + 2 * tk * tn * sb + tm_ * tn * 4 + 2 * tm_ * tn * so
                + 2 * tm_ * tn * 4 * (extra + (1 if has_add else 0)))

    while need(tm) > MM_VMEM_BUDGET and tm % 256 == 0:
        tm //= 2
    return tm, tn, tk


def _mm(a, b, *, name, ta=False, tb=False, add=None, out_dtype=F32):
    m = a.shape[1] if ta else a.shape[0]
    k = a.shape[0] if ta else a.shape[1]
    n = b.shape[0] if tb else b.shape[1]
    assert (b.shape[1] if tb else b.shape[0]) == k, (a.shape, b.shape, ta, tb)
    tm, tn, tk = _mm_tiles(m, n, k, a.dtype.itemsize, b.dtype.itemsize,
                           jnp.dtype(out_dtype).itemsize, add is not None)
    nk = k // tk
    dims = (((0 if ta else 1,), (1 if tb else 0,)), ((), ()))
    has_add = add is not None

    def body(*refs):
        if has_add:
            a_ref, b_ref, add_ref, o_ref, acc_ref = refs
        else:
            a_ref, b_ref, o_ref, acc_ref = refs
        kk = pl.program_id(2)

        @pl.when(kk == 0)
        def _():
            acc_ref[...] = jnp.zeros_like(acc_ref)

        acc_ref[...] += lax.dot_general(a_ref[...].astype(BF16), b_ref[...].astype(BF16), dims,
                                        preferred_element_type=F32)

        @pl.when(kk == nk - 1)
        def _():
            r = acc_ref[...]
            if has_add:
                r = r + add_ref[...].astype(F32)
            o_ref[...] = r.astype(out_dtype)

    a_spec = (pl.BlockSpec((tk, tm), lambda i, j, kk: (kk, i)) if ta
              else pl.BlockSpec((tm, tk), lambda i, j, kk: (i, kk)))
    b_spec = (pl.BlockSpec((tn, tk), lambda i, j, kk: (j, kk)) if tb
              else pl.BlockSpec((tk, tn), lambda i, j, kk: (kk, j)))
    o_spec = pl.BlockSpec((tm, tn), lambda i, j, kk: (i, j))
    in_specs = [a_spec, b_spec] + ([o_spec] if has_add else [])
    args = (a, b) + ((add,) if has_add else ())
    return pl.pallas_call(
        body, name=name, grid=(m // tm, n // tn, nk), in_specs=in_specs, out_specs=o_spec,
        out_shape=jax.ShapeDtypeStruct((m, n), out_dtype),
        scratch_shapes=[pltpu.VMEM((tm, tn), F32)],
        compiler_params=_params("parallel", "parallel", "arbitrary"),
    )(*args)


def _mm_norm_bwd(a, b, x, gain, dres, *, name, add=None):
    m, k = a.shape
    n = b.shape[0]
    has_add = add is not None
    tm, tn, tk = _mm_tiles(m, n, k, a.dtype.itemsize, b.dtype.itemsize, 4, has_add, extra=2)
    assert tn == n, (tn, n)
    nk = k // tk

    def body(*refs):
        if has_add:
            a_ref, b_ref, add_ref, x_ref, g_ref, dres_ref, dx_ref, dg_ref, acc_ref = refs
        else:
            a_ref, b_ref, x_ref, g_ref, dres_ref, dx_ref, dg_ref, acc_ref = refs
        i, kk = pl.program_id(0), pl.program_id(1)

        @pl.when(kk == 0)
        def _():
            acc_ref[...] = jnp.zeros_like(acc_ref)

        acc_ref[...] += _dot_nt(a_ref[...].astype(BF16), b_ref[...].astype(BF16))

        @pl.when(kk == nk - 1)
        def _():
            dyv = acc_ref[...]
            if has_add:
                dyv = dyv + add_ref[...]
            xv = x_ref[...]
            r = lax.rsqrt(jnp.mean(xv * xv, axis=-1, keepdims=True) + NORM_EPS)
            xh = xv * r
            dxh = dyv * g_ref[...]
            dx_ref[...] = dres_ref[...] + r * (dxh - xh * jnp.mean(dxh * xh, axis=-1, keepdims=True))
            part = jnp.sum(dyv * xh, axis=0, keepdims=True)

            @pl.when(i == 0)
            def _():
                dg_ref[...] = part

            @pl.when(i > 0)
            def _():
                dg_ref[...] += part

    row = pl.BlockSpec((tm, n), lambda i, kk: (i, 0))
    vec = pl.BlockSpec((1, n), lambda i, kk: (0, 0))
    in_specs = ([pl.BlockSpec((tm, tk), lambda i, kk: (i, kk)), pl.BlockSpec((n, tk), lambda i, kk: (0, kk))]
                + ([row] if has_add else []) + [row, vec, row])
    args = (a, b) + ((add,) if has_add else ()) + (x, gain.reshape(1, n), dres)
    dx, dg = pl.pallas_call(
        body, name=name, grid=(m // tm, nk), in_specs=in_specs, out_specs=[row, vec],
        out_shape=[jax.ShapeDtypeStruct((m, n), F32), jax.ShapeDtypeStruct((1, n), F32)],
        scratch_shapes=[pltpu.VMEM((tm, n), F32)],
        compiler_params=_params("arbitrary", "arbitrary"),
    )(*args)
    return dx, dg.reshape(n)


def _rmsnorm_fwd(x, gain, *, name, rows=512):
    t, d = x.shape
    tr = _tile(t, rows)

    def body(x_ref, g_ref, o_ref):
        xv = x_ref[...]
        r = lax.rsqrt(jnp.mean(xv * xv, axis=-1, keepdims=True) + NORM_EPS)
        o_ref[...] = ((xv * r) * g_ref[...]).astype(BF16)

    return pl.pallas_call(
        body, name=name, grid=(t // tr,),
        in_specs=[pl.BlockSpec((tr, d), lambda i: (i, 0)), pl.BlockSpec((1, d), lambda i: (0, 0))],
        out_specs=pl.BlockSpec((tr, d), lambda i: (i, 0)),
        out_shape=jax.ShapeDtypeStruct((t, d), BF16),
        compiler_params=_params("parallel"),
    )(x, gain.reshape(1, d))


def _rmsnorm_bwd(x, gain, dy, dres, *, name, rows=512):
    t, d = x.shape
    tr = _tile(t, rows)

    def body(x_ref, g_ref, dy_ref, dres_ref, dx_ref, dg_ref):
        xv = x_ref[...]
        r = lax.rsqrt(jnp.mean(xv * xv, axis=-1, keepdims=True) + NORM_EPS)
        xh = xv * r
        dyv = dy_ref[...].astype(F32)
        dxh = dyv * g_ref[...]
        dx = r * (dxh - xh * jnp.mean(dxh * xh, axis=-1, keepdims=True))
        dx_ref[...] = dres_ref[...] + dx
        part = jnp.sum(dyv * xh, axis=0, keepdims=True)

        @pl.when(pl.program_id(0) == 0)
        def _():
            dg_ref[...] = part

        @pl.when(pl.program_id(0) > 0)
        def _():
            dg_ref[...] += part

    row = pl.BlockSpec((tr, d), lambda i: (i, 0))
    vec = pl.BlockSpec((1, d), lambda i: (0, 0))
    dx, dg = pl.pallas_call(
        body, name=name, grid=(t // tr,), in_specs=[row, vec, row, row], out_specs=[row, vec],
        out_shape=[jax.ShapeDtypeStruct((t, d), F32), jax.ShapeDtypeStruct((1, d), F32)],
        compiler_params=_params("arbitrary"),
    )(x, gain.reshape(1, d), dy, dres)
    return dx, dg.reshape(d)


def _final_loss(h, gain, target, *, rows=512):
    t, d = h.shape
    tr = _tile(t, rows)

    def body(x_ref, g_ref, tg_ref, dx_ref, dg_ref, loss_ref):
        xv = x_ref[...]
        g = g_ref[...]
        r = lax.rsqrt(jnp.mean(xv * xv, axis=-1, keepdims=True) + NORM_EPS)
        xh = xv * r
        err = xh * g - tg_ref[...]
        dyv = err * (1.0 / d)
        dxh = dyv * g
        dx_ref[...] = r * (dxh - xh * jnp.mean(dxh * xh, axis=-1, keepdims=True))
        part = jnp.sum(dyv * xh, axis=0, keepdims=True)
        lpart = jnp.zeros((1, LANES), F32) + (0.5 / d) * jnp.sum(err * err)

        @pl.when(pl.program_id(0) == 0)
        def _():
            dg_ref[...] = part
            loss_ref[...] = lpart

        @pl.when(pl.program_id(0) > 0)
        def _():
            dg_ref[...] += part
            loss_ref[...] += lpart

    row = pl.BlockSpec((tr, d), lambda i: (i, 0))
    vec = pl.BlockSpec((1, d), lambda i: (0, 0))
    dx, dg, loss = pl.pallas_call(
        body, name="final_loss", grid=(t // tr,), in_specs=[row, vec, row],
        out_specs=[row, vec, pl.BlockSpec((1, LANES), lambda i: (0, 0))],
        out_shape=[jax.ShapeDtypeStruct((t, d), F32), jax.ShapeDtypeStruct((1, d), F32),
                   jax.ShapeDtypeStruct((1, LANES), F32)],
        compiler_params=_params("arbitrary"),
    )(h, gain.reshape(1, d), target)
    return loss[0, 0], dx, dg.reshape(d)


def _ple_fwd(h, a, pp, *, name, rows=512):
    t, d = h.shape
    tr = _tile(t, rows)

    def body(h_ref, a_ref, p_ref, o_ref):
        o_ref[...] = h_ref[...] + _sigmoid(a_ref[...]) * p_ref[...]

    row = pl.BlockSpec((tr, d), lambda i: (i, 0))
    return pl.pallas_call(
        body, name=name, grid=(t // tr,), in_specs=[row, row, row], out_specs=row,
        out_shape=jax.ShapeDtypeStruct((t, d), F32), compiler_params=_params("parallel"),
    )(h, a, pp)


def _ple_bwd(dh, a, pp, *, name, rows=512):
    t, d = dh.shape
    tr = _tile(t, rows)

    def body(dh_ref, a_ref, p_ref, da_ref, dp_ref):
        s = _sigmoid(a_ref[...])
        dhv = dh_ref[...]
        da_ref[...] = (dhv * p_ref[...] * (s * (1.0 - s))).astype(BF16)
        dp_ref[...] = (dhv * s).astype(BF16)

    row = pl.BlockSpec((tr, d), lambda i: (i, 0))
    return pl.pallas_call(
        body, name=name, grid=(t // tr,), in_specs=[row, row, row], out_specs=[row, row],
        out_shape=[jax.ShapeDtypeStruct((t, d), BF16)] * 2, compiler_params=_params("parallel"),
    )(dh, a, pp)


CONV_ROWS = 256
CONV_HALO = 8


def _conv_window(ref, r0, with_prev, with_next):
    s = ref.shape[1]
    parts = []
    if with_prev:
        prev = ref[0, pl.ds(pl.multiple_of(jnp.maximum(r0 - CONV_HALO, 0), CONV_HALO), CONV_HALO), :]
        parts.append(jnp.where(r0 > 0, prev, 0.0))
    parts.append(ref[0, pl.ds(r0, CONV_ROWS), :])
    if with_next:
        nxt = pl.multiple_of(jnp.minimum(r0 + CONV_ROWS, s - CONV_HALO), CONV_HALO)
        parts.append(ref[0, pl.ds(nxt, CONV_HALO), :])
    return jnp.concatenate(parts, axis=0)


def _conv_taps(win, kw, n):
    return [win[CONV_HALO - (kw - 1 - k):CONV_HALO - (kw - 1 - k) + n] for k in range(kw)]


def _conv_apply(taps, wv, bv):
    pre = bv + wv[0:1, :] * taps[0]
    for k in range(1, len(taps)):
        pre = pre + wv[k:k + 1, :] * taps[k]
    return pre


def _rows8(x):
    acc = x[0:8]
    for i in range(1, x.shape[0] // 8):
        acc = acc + x[8 * i:8 * i + 8]
    return acc


def _conv_grad_step(dpre_ext, taps, wv, is_last):
    kw = wv.shape[0]
    halo = jnp.where(is_last, 0.0, dpre_ext[CONV_ROWS:])
    dpre_ext = jnp.concatenate([dpre_ext[:CONV_ROWS], halo], axis=0)
    dpre = dpre_ext[:CONV_ROWS]
    du = wv[kw - 1:kw, :] * dpre
    for k in range(kw - 1):
        du = du + wv[k:k + 1, :] * dpre_ext[kw - 1 - k:kw - 1 - k + CONV_ROWS]
    sums = [_rows8(dpre * taps[k][:CONV_ROWS]) for k in range(kw)] + [_rows8(dpre)]
    return du, sums


def _conv_store_sums(sums, dw_ref, db_ref, first):
    kw = len(sums) - 1
    vals = [jnp.sum(s_, axis=0, keepdims=True) for s_ in sums]

    @pl.when(first)
    def _():
        for k in range(kw):
            dw_ref[k:k + 1, :] = vals[k]
        db_ref[...] = vals[kw]

    @pl.when(jnp.logical_not(first))
    def _():
        for k in range(kw):
            dw_ref[k:k + 1, :] += vals[k]
        db_ref[...] += vals[kw]


def _dsilu(pre):
    s = _sigmoid(pre)
    return s, s * (1.0 + pre * (1.0 - s))


def _conv_silu_fwd(zx, off, w, b, *, name, tc=128):
    bsz, s, _ = zx.shape
    kw, c = w.shape
    o0 = off // tc

    def body(u_ref, w_ref, b_ref, o_ref):
        wv, bv = w_ref[...], b_ref[...]

        def step(i, carry):
            r0 = pl.multiple_of(i * CONV_ROWS, CONV_ROWS)
            taps = _conv_taps(_conv_window(u_ref, r0, True, False), kw, CONV_ROWS)
            pre = _conv_apply(taps, wv, bv)
            o_ref[0, pl.ds(r0, CONV_ROWS), :] = pre * _sigmoid(pre)
            return carry

        lax.fori_loop(0, s // CONV_ROWS, step, 0)

    return pl.pallas_call(
        body, name=name, grid=(bsz, c // tc),
        in_specs=[pl.BlockSpec((1, s, tc), lambda i, j: (i, 0, o0 + j)),
                  pl.BlockSpec((kw, tc), lambda i, j: (0, j)),
                  pl.BlockSpec((1, tc), lambda i, j: (0, j))],
        out_specs=pl.BlockSpec((1, s, tc), lambda i, j: (i, 0, j)),
        out_shape=jax.ShapeDtypeStruct((bsz, s, c), F32),
        compiler_params=_params("parallel", "parallel"),
    )(zx, w, b.reshape(1, c))


def _conv_silu_bwd(zx, off, w, b, douts, *, name, tc=128):
    bsz, s, _ = zx.shape
    kw, c = w.shape
    o0 = off // tc
    counts = [d.shape[2] // tc for d in douts]
    starts = [sum(counts[:k]) for k in range(len(douts))]
    assert sum(counts) == c // tc

    def body(u_ref, w_ref, b_ref, *rest):
        dy_refs = rest[:len(douts)]
        du_ref, dw_ref, db_ref = rest[len(douts):]
        j = pl.program_id(0)
        wv, bv = w_ref[...], b_ref[...]
        n = CONV_ROWS + CONV_HALO

        def step(i, sums):
            r0 = pl.multiple_of(i * CONV_ROWS, CONV_ROWS)
            taps = _conv_taps(_conv_window(u_ref, r0, True, True), kw, n)
            _, ds = _dsilu(_conv_apply(taps, wv, bv))
            dy = _conv_window(dy_refs[0], r0, False, True)
            for k in range(1, len(douts)):
                dy = jnp.where(j >= starts[k], _conv_window(dy_refs[k], r0, False, True), dy)
            du, new = _conv_grad_step(dy * ds, taps, wv, r0 + CONV_ROWS >= s)
            du_ref[0, pl.ds(r0, CONV_ROWS), :] = du.astype(BF16)
            return tuple(a + b_ for a, b_ in zip(sums, new))

        zero = tuple(jnp.zeros((8, tc), F32) for _ in range(kw + 1))
        sums = lax.fori_loop(0, s // CONV_ROWS, step, zero)
        _conv_store_sums(sums, dw_ref, db_ref, pl.program_id(1) == 0)

    def part_spec(k):
        return pl.BlockSpec((1, s, tc), lambda j, i: (i, 0, jnp.clip(j - starts[k], 0, counts[k] - 1)))

    du, dw, db = pl.pallas_call(
        body, name=name, grid=(c // tc, bsz),
        in_specs=[pl.BlockSpec((1, s, tc), lambda j, i: (i, 0, o0 + j)),
                  pl.BlockSpec((kw, tc), lambda j, i: (0, j)),
                  pl.BlockSpec((1, tc), lambda j, i: (0, j))] + [part_spec(k) for k in range(len(douts))],
        out_specs=[pl.BlockSpec((1, s, tc), lambda j, i: (i, 0, j)),
                   pl.BlockSpec((kw, tc), lambda j, i: (0, j)),
                   pl.BlockSpec((1, tc), lambda j, i: (0, j))],
        out_shape=[jax.ShapeDtypeStruct((bsz, s, c), BF16), jax.ShapeDtypeStruct((kw, c), F32),
                   jax.ShapeDtypeStruct((1, c), F32)],
        compiler_params=_params("parallel", "arbitrary"),
    )(zx, w, b.reshape(1, c), *douts)
    return du, dw, db.reshape(c)


def _conv_glu_fwd(up, w, b, *, name, tc=128):
    bsz, s, c2 = up.shape
    kw = w.shape[0]
    f = c2 // 2
    nt = f // tc

    def body(ug_ref, uv_ref, wg_ref, wv_ref, bg_ref, bv_ref, o_ref):
        wg, wv, bg, bv = wg_ref[...], wv_ref[...], bg_ref[...], bv_ref[...]

        def step(i, carry):
            r0 = pl.multiple_of(i * CONV_ROWS, CONV_ROWS)
            pg = _conv_apply(_conv_taps(_conv_window(ug_ref, r0, True, False), kw, CONV_ROWS), wg, bg)
            pv = _conv_apply(_conv_taps(_conv_window(uv_ref, r0, True, False), kw, CONV_ROWS), wv, bv)
            o_ref[0, pl.ds(r0, CONV_ROWS), :] = (pg * _sigmoid(pg) * pv).astype(BF16)
            return carry

        lax.fori_loop(0, s // CONV_ROWS, step, 0)

    b2 = b.reshape(1, c2)
    return pl.pallas_call(
        body, name=name, grid=(bsz, nt),
        in_specs=[pl.BlockSpec((1, s, tc), lambda i, j: (i, 0, j)),
                  pl.BlockSpec((1, s, tc), lambda i, j: (i, 0, nt + j)),
                  pl.BlockSpec((kw, tc), lambda i, j: (0, j)),
                  pl.BlockSpec((kw, tc), lambda i, j: (0, nt + j)),
                  pl.BlockSpec((1, tc), lambda i, j: (0, j)),
                  pl.BlockSpec((1, tc), lambda i, j: (0, nt + j))],
        out_specs=pl.BlockSpec((1, s, tc), lambda i, j: (i, 0, j)),
        out_shape=jax.ShapeDtypeStruct((bsz, s, f), BF16),
        compiler_params=_params("parallel", "parallel"),
    )(up, up, w, w, b2, b2)


def _conv_glu_bwd(up, w, b, df, *, name, tc=128):
    bsz, s, c2 = up.shape
    kw = w.shape[0]
    f = c2 // 2
    nt = f // tc

    def body(ug_ref, uv_ref, wg_ref, wv_ref, bg_ref, bv_ref, df_ref,
             dug_ref, duv_ref, dwg_ref, dwv_ref, dbg_ref, dbv_ref):
        first = pl.program_id(1) == 0
        wg, wv, bg, bv = wg_ref[...], wv_ref[...], bg_ref[...], bv_ref[...]
        n = CONV_ROWS + CONV_HALO

        def step(i, sums):
            r0 = pl.multiple_of(i * CONV_ROWS, CONV_ROWS)
            is_last = r0 + CONV_ROWS >= s
            tg = _conv_taps(_conv_window(ug_ref, r0, True, True), kw, n)
            tv = _conv_taps(_conv_window(uv_ref, r0, True, True), kw, n)
            pg = _conv_apply(tg, wg, bg)
            pv = _conv_apply(tv, wv, bv)
            sig, dsl = _dsilu(pg)
            dfv = _conv_window(df_ref, r0, False, True)
            dug, new_g = _conv_grad_step(dfv * pv * dsl, tg, wg, is_last)
            duv, new_v = _conv_grad_step(dfv * (pg * sig), tv, wv, is_last)
            dug_ref[0, pl.ds(r0, CONV_ROWS), :] = dug.astype(BF16)
            duv_ref[0, pl.ds(r0, CONV_ROWS), :] = duv.astype(BF16)
            return tuple(a + b_ for a, b_ in zip(sums, new_g + new_v))

        zero = tuple(jnp.zeros((8, tc), F32) for _ in range(2 * (kw + 1)))
        sums = lax.fori_loop(0, s // CONV_ROWS, step, zero)
        _conv_store_sums(sums[:kw + 1], dwg_ref, dbg_ref, first)
        _conv_store_sums(sums[kw + 1:], dwv_ref, dbv_ref, first)

    b2 = b.reshape(1, c2)
    act = lambda j, i: (i, 0, j)
    wsp = pl.BlockSpec((kw, tc), lambda j, i: (0, j))
    bsp = pl.BlockSpec((1, tc), lambda j, i: (0, j))
    dug, duv, dwg, dwv, dbg, dbv = pl.pallas_call(
        body, name=name, grid=(nt, bsz),
        in_specs=[pl.BlockSpec((1, s, tc), act),
                  pl.BlockSpec((1, s, tc), lambda j, i: (i, 0, nt + j)),
                  wsp, pl.BlockSpec((kw, tc), lambda j, i: (0, nt + j)),
                  bsp, pl.BlockSpec((1, tc), lambda j, i: (0, nt + j)),
                  pl.BlockSpec((1, s, tc), act)],
        out_specs=[pl.BlockSpec((1, s, tc), act), pl.BlockSpec((1, s, tc), act), wsp, wsp, bsp, bsp],
        out_shape=[jax.ShapeDtypeStruct((bsz, s, f), BF16)] * 2
        + [jax.ShapeDtypeStruct((kw, f), F32)] * 2 + [jax.ShapeDtypeStruct((1, f), F32)] * 2,
        compiler_params=_params("parallel", "arbitrary"),
    )(up, up, w, w, b2, b2, df)
    return (dug, duv, jnp.concatenate([dwg, dwv], axis=1),
            jnp.concatenate([dbg.reshape(f), dbv.reshape(f)]))


def _ssd_shared(xs, bm, cm, dtc_raw, dtr_raw, plane, psub, st):
    cl = SSM_CHUNK
    bias_l, a_l = plane[0:1, :], plane[1:2, :]
    bias_s, a_s = psub[:, 0:1], psub[:, 1:2]
    ri = lax.broadcasted_iota(I32, (cl, cl), 0)
    ci = lax.broadcasted_iota(I32, (cl, cl), 1)
    tril = ri >= ci
    low_incl = tril.astype(BF16)
    up_incl = (ri <= ci).astype(BF16)
    seg_t = (lax.broadcasted_iota(I32, (LANES, SSM_GROUP_W), 0)
             == lax.broadcasted_iota(I32, (LANES, SSM_GROUP_W), 1) // SSM_HEAD_DIM).astype(BF16)
    dt_c = _softplus(dtc_raw + bias_l)
    cs_c = _dot2_left(low_incl, dt_c * a_l)
    dt_r = _softplus(dtr_raw + bias_s)
    cs_r = _dot2(dt_r * a_s, up_incl)
    yield
    dt_ch = _dot2(dt_c, seg_t)
    cs_ch = _dot2(cs_c, seg_t)
    yield
    cs_last = cs_ch[cl - 1:cl, :]
    decay_ch = jnp.exp(cs_ch)
    w_ch = jnp.exp(cs_last - cs_ch)
    tot_ch = jnp.exp(cs_last)
    xdt = xs * dt_ch
    bm_b, cm_b = bm.astype(BF16), cm.astype(BF16)
    gmat = _dot_nt(cm_b, bm_b)
    cst = _dot(cm_b, st.astype(BF16))
    yield
    yoff = decay_ch * cst
    return dict(tril=tril, low_incl=low_incl, up_incl=up_incl, seg_t=seg_t, a_l=a_l, bias_l=bias_l,
                dt_c=dt_c, cs_c=cs_c, cs_r=cs_r, dt_ch=dt_ch, decay_ch=decay_ch, w_ch=w_ch,
                tot_ch=tot_ch, xdt=xdt, bm_b=bm_b, cm_b=cm_b, gmat=gmat, yoff=yoff)


def _head_decay(q, r):
    diff = q["cs_c"][:, r:r + 1] - q["cs_r"][r:r + 1, :]
    return jnp.where(q["tril"], jnp.exp(jnp.minimum(diff, 0.0)), 0.0)


def _half_mask(hh):
    lane = lax.broadcasted_iota(I32, (SSM_CHUNK, LANES), 1)
    return (lane < SSM_HEAD_DIM) if hh == 0 else (lane >= SSM_HEAD_DIM)


def _ssd_ydiag(q):
    pairs = []
    for pr in range(SSM_HEADS_PER_GROUP // 2):
        xp = q["xdt"][:, pr * LANES:(pr + 1) * LANES]
        acc = None
        for hh in range(2):
            mm_ = (q["gmat"] * _head_decay(q, 2 * pr + hh)).astype(BF16)
            part = _dot(mm_, jnp.where(_half_mask(hh), xp, 0.0).astype(BF16))
            acc = part if acc is None else acc + part
        pairs.append(acc)
        yield
    return jnp.concatenate(pairs, axis=1)


def _ssd_specs(bsz, s, g_n, d_inner, rev):
    cl = SSM_CHUNK
    nc = s // cl
    cc = (lambda c: nc - 1 - c) if rev else (lambda c: c)
    gb = d_inner // LANES
    dt0 = (d_inner + d_inner + 2 * g_n * SSM_STATE) // LANES
    gpb = next(n_ for n_ in (4, 2, 1) if all(v % n_ == 0 for v in (g_n, gb, dt0)))
    gw = SSM_GROUP_W
    specs = dict(
        z=pl.BlockSpec((1, cl, gw * gpb), lambda b, g, c: (b, cc(c), g)),
        dtc=pl.BlockSpec((1, cl, LANES * gpb), lambda b, g, c: (b, cc(c), dt0 // gpb + g)),
        xs=pl.BlockSpec((1, cl, gw * gpb), lambda b, g, c: (b, cc(c), g)),
        bm=pl.BlockSpec((1, cl, LANES * gpb), lambda b, g, c: (b, cc(c), gb // gpb + g)),
        cm=pl.BlockSpec((1, cl, LANES * gpb), lambda b, g, c: (b, cc(c), (gb + g_n) // gpb + g)),
        dtr=pl.BlockSpec((1, gpb, SSM_ROWS, cl), lambda b, g, c: (b, g, 0, cc(c))),
        plane=pl.BlockSpec((gpb, 8, LANES), lambda b, g, c: (g, 0, 0)),
        psub=pl.BlockSpec((gpb, SSM_ROWS, LANES), lambda b, g, c: (g, 0, 0)),
        chan=pl.BlockSpec((1, gw * gpb), lambda b, g, c: (0, g)),
        state=pl.BlockSpec((1, gpb, 1, SSM_STATE, gw), lambda b, g, c: (b, g, cc(c), 0, 0)),
        bgrp=pl.BlockSpec((1, cl, LANES * gpb), lambda b, g, c: (b, cc(c), g)),
        acc_ch=pl.BlockSpec((1, gpb, 8, gw), lambda b, g, c: (b, g, 0, 0)),
        acc_ln=pl.BlockSpec((1, gpb, 8, LANES), lambda b, g, c: (b, g, 0, 0)),
    )
    lanes = lambda w: (lambda ref, gg: ref.at[:, :, pl.ds(gg * w, w)])
    second = lambda ref, gg: ref.at[:, pl.ds(gg, 1)]
    first = lambda ref, gg: ref.at[pl.ds(gg, 1)]
    views = dict(z=lanes(gw), xs=lanes(gw), dtc=lanes(LANES), bm=lanes(LANES), cm=lanes(LANES),
                 bgrp=lanes(LANES), dtr=second, state=second, acc_ch=second, acc_ln=second,
                 plane=first, psub=first, chan=lambda ref, gg: ref.at[:, pl.ds(gg * gw, gw)],
                 scratch=lambda ref, gg: ref.at[gg])
    return specs, views, gpb


def _per_group(body, names, views, gpb):
    def run(*refs):
        _round_robin([body(*[views[nm](ref, gg) for nm, ref in zip(names, refs)]) for gg in range(gpb)])
    return run


def _ssd_fwd(zx, xbc, dtr_row, plane, psub, d_ch, nw, *, name):
    bsz, s, _ = zx.shape
    d_inner = d_ch.shape[1]
    g_n = d_inner // SSM_GROUP_W
    nc = s // SSM_CHUNK
    sp, views, gpb = _ssd_specs(bsz, s, g_n, d_inner, False)
    names = ["z", "dtc", "xs", "bm", "cm", "dtr", "plane", "psub", "chan", "chan", "z", "state", "scratch"]

    def body(z_ref, dtc_ref, xs_ref, bm_ref, cm_ref, dtr_ref, plane_ref, psub_ref, d_ref, nw_ref,
             gn_ref, st_out_ref, st_ref):
        @pl.when(pl.program_id(2) == 0)
        def _():
            st_ref[...] = jnp.zeros_like(st_ref)

        xs = xs_ref[0]
        st = st_ref[...]
        st_out_ref[0, 0, 0] = st
        q = yield from _ssd_shared(xs, bm_ref[0], cm_ref[0], dtc_ref[0], dtr_ref[0, 0], plane_ref[0],
                                   psub_ref[0], st)
        y = (yield from _ssd_ydiag(q)) + q["yoff"] + xs * d_ref[...]
        st_ref[...] = q["tot_ch"] * st + _dot_tn(q["bm_b"], (q["w_ch"] * q["xdt"]).astype(BF16))
        zv = z_ref[0]
        gy = y * (zv * _sigmoid(zv))
        rstd = lax.rsqrt(jnp.mean(gy * gy, axis=-1, keepdims=True) + SSM_NORM_EPS)
        gn_ref[0] = ((gy * rstd) * nw_ref[...]).astype(BF16)

    return pl.pallas_call(
        _per_group(body, names, views, gpb), name=name, grid=(bsz, g_n // gpb, nc),
        in_specs=[sp["z"], sp["dtc"], sp["xs"], sp["bm"], sp["cm"], sp["dtr"], sp["plane"],
                  sp["psub"], sp["chan"], sp["chan"]],
        out_specs=[sp["z"], sp["state"]],
        out_shape=[jax.ShapeDtypeStruct((bsz, s, d_inner), BF16),
                   jax.ShapeDtypeStruct((bsz, g_n, nc, SSM_STATE, SSM_GROUP_W), F32)],
        scratch_shapes=[pltpu.VMEM((gpb, SSM_STATE, SSM_GROUP_W), F32)],
        compiler_params=_params("parallel", "parallel", "arbitrary"),
    )(zx, zx, xbc, xbc, xbc, dtr_row, plane, psub, d_ch, nw)


def _ssd_bwd(zx, xbc, dtr_row, plane, psub, d_ch, nw, states, dgn, *, name):
    bsz, s, _ = zx.shape
    d_inner = d_ch.shape[1]
    g_n = d_inner // SSM_GROUP_W
    cl = SSM_CHUNK
    nc = s // cl
    sp, views, gpb = _ssd_specs(bsz, s, g_n, d_inner, True)
    acc_ch, acc_ln = sp["acc_ch"], sp["acc_ln"]
    names = ["z", "dtc", "xs", "bm", "cm", "dtr", "plane", "psub", "chan", "chan", "state", "z",
             "z", "bgrp", "bgrp", "z", "bgrp", "acc_ch", "acc_ln", "scratch"]

    def body(z_ref, dtc_ref, xs_ref, bm_ref, cm_ref, dtr_ref, plane_ref, psub_ref, d_ref, nw_ref,
             st_in_ref, dgn_ref,
             dxs_ref, dbm_ref, dcm_ref, dz_ref, ddt_ref, ach_ref, aln_ref, dst_ref):
        first = pl.program_id(2) == 0

        @pl.when(first)
        def _():
            dst_ref[...] = jnp.zeros_like(dst_ref)
            ach_ref[...] = jnp.zeros_like(ach_ref)
            aln_ref[...] = jnp.zeros_like(aln_ref)

        xs = xs_ref[0]
        st = st_in_ref[0, 0, 0]
        q = yield from _ssd_shared(xs, bm_ref[0], cm_ref[0], dtc_ref[0], dtr_ref[0, 0], plane_ref[0],
                                   psub_ref[0], st)
        d_chv = d_ref[...]
        nwv = nw_ref[...]
        y = (yield from _ssd_ydiag(q)) + q["yoff"] + xs * d_chv
        zv = z_ref[0]
        sz = _sigmoid(zv)
        silu_z = zv * sz
        gy = y * silu_z
        rstd = lax.rsqrt(jnp.mean(gy * gy, axis=-1, keepdims=True) + SSM_NORM_EPS)
        gyh = gy * rstd
        dgnv = dgn_ref[0]
        dgyh = dgnv * nwv
        dgy = rstd * (dgyh - gyh * jnp.mean(dgyh * gyh, axis=-1, keepdims=True))
        dy = dgy * silu_z
        dz_ref[0] = (dgy * y * (sz * (1.0 + zv * (1.0 - sz)))).astype(BF16)
        ach_ref[0, 0, 0:1, :] += jnp.sum(dgnv * gyh, axis=0, keepdims=True)
        ach_ref[0, 0, 1:2, :] += jnp.sum(dy * xs, axis=0, keepdims=True)
        yield
        st_b = st.astype(BF16)
        dyd = (dy * q["decay_ch"]).astype(BF16)
        dcm = _dot_nt(dyd, st_b)
        dstn = dst_ref[...]
        dstn_b = dstn.astype(BF16)
        bds = _dot(q["bm_b"], dstn_b)
        wx = q["w_ch"] * q["xdt"]
        dbm = _dot_nt(wx.astype(BF16), dstn_b)
        dst_ref[...] = q["tot_ch"] * dstn + _dot_tn(q["cm_b"], dyd)
        vterm = wx * bds
        cs_terms = dy * q["yoff"] - vterm
        last_ch = q["tot_ch"] * jnp.sum(dstn * st, axis=0, keepdims=True) + jnp.sum(vterm, axis=0, keepdims=True)
        yield
        lane = lax.broadcasted_iota(I32, (cl, LANES), 1)
        rowi = lax.broadcasted_iota(I32, (SSM_ROWS, cl), 0)
        dg_sum = jnp.zeros((cl, cl), F32)
        dcs_col = jnp.zeros((cl, LANES), F32)
        dcs_row = jnp.zeros((SSM_ROWS, cl), F32)
        dxdt_pairs = []
        for pr in range(SSM_HEADS_PER_GROUP // 2):
            xp_b = q["xdt"][:, pr * LANES:(pr + 1) * LANES].astype(BF16)
            dyp = dy[:, pr * LANES:(pr + 1) * LANES]
            acc = None
            for hh in range(2):
                r = 2 * pr + hh
                dm = _head_decay(q, r)
                mmat = q["gmat"] * dm
                dym = jnp.where(_half_mask(hh), dyp, 0.0).astype(BF16)
                dmat = jnp.where(q["tril"], _dot_nt(dym, xp_b), 0.0)
                part = _dot_tn(mmat.astype(BF16), dym)
                acc = part if acc is None else acc + part
                dg_sum = dg_sum + dmat * dm
                e = dmat * mmat
                dcs_col = dcs_col + jnp.where(lane == r, jnp.sum(e, axis=1, keepdims=True), 0.0)
                dcs_row = dcs_row + jnp.where(rowi == r, jnp.sum(e, axis=0, keepdims=True), 0.0)
            dxdt_pairs.append(acc)
            yield
        dg_b = dg_sum.astype(BF16)
        dcm_ref[0] = dcm + _dot(dg_b, q["bm_b"])
        dbm_ref[0] = dbm + _dot_tn(dg_b, q["cm_b"])
        dxdt = q["w_ch"] * bds + jnp.concatenate(dxdt_pairs, axis=1)
        dxs_ref[0] = dy * d_chv + dxdt * q["dt_ch"]
        yield
        seg = (lax.broadcasted_iota(I32, (SSM_GROUP_W, LANES), 0) // SSM_HEAD_DIM
               == lax.broadcasted_iota(I32, (SSM_GROUP_W, LANES), 1)).astype(BF16)
        row_as_col = jnp.transpose(jnp.concatenate(
            [dcs_row, jnp.zeros((cl - SSM_ROWS, cl), F32)], axis=0))
        dcs = dcs_col - row_as_col + _dot2(cs_terms, seg)
        last = _dot2(jnp.zeros((8, SSM_GROUP_W), F32) + last_ch, seg)[0:1, :]
        da = _dot2_left(q["up_incl"], dcs) + last
        ddt = _dot2(dxdt * xs, seg) + da * q["a_l"]
        ddtr = ddt * _sigmoid(dtc_ref[0] + q["bias_l"])
        ddt_ref[0] = ddtr.astype(BF16)
        aln_ref[0, 0, 0:1, :] += jnp.sum(ddtr, axis=0, keepdims=True)
        aln_ref[0, 0, 1:2, :] += jnp.sum(da * q["dt_c"], axis=0, keepdims=True)

    outs = pl.pallas_call(
        _per_group(body, names, views, gpb), name=name, grid=(bsz, g_n // gpb, nc),
        in_specs=[sp["z"], sp["dtc"], sp["xs"], sp["bm"], sp["cm"], sp["dtr"], sp["plane"],
                  sp["psub"], sp["chan"], sp["chan"], sp["state"], sp["z"]],
        out_specs=[sp["z"], sp["bgrp"], sp["bgrp"], sp["z"], sp["bgrp"], acc_ch, acc_ln],
        out_shape=[jax.ShapeDtypeStruct((bsz, s, d_inner), F32),
                   jax.ShapeDtypeStruct((bsz, s, g_n * SSM_STATE), F32),
                   jax.ShapeDtypeStruct((bsz, s, g_n * SSM_STATE), F32),
                   jax.ShapeDtypeStruct((bsz, s, d_inner), BF16),
                   jax.ShapeDtypeStruct((bsz, s, g_n * LANES), BF16),
                   jax.ShapeDtypeStruct((bsz, g_n, 8, SSM_GROUP_W), F32),
                   jax.ShapeDtypeStruct((bsz, g_n, 8, LANES), F32)],
        scratch_shapes=[pltpu.VMEM((gpb, SSM_STATE, SSM_GROUP_W), F32)],
        compiler_params=_params("parallel", "parallel", "arbitrary"),
    )(zx, zx, xbc, xbc, xbc, dtr_row, plane, psub, d_ch, nw, states, dgn)
    return outs


def _sb_stack(x):
    out = []
    for i in range(x.shape[0] // SB_BLOCK):
        xb = x[i * SB_BLOCK:(i + 1) * SB_BLOCK]
        lane = lax.broadcasted_iota(I32, xb.shape, 1)
        zero = jnp.zeros_like(xb)
        out += [jnp.where(lane < SB_HEAD_DIM, xb, zero), jnp.where(lane >= SB_HEAD_DIM, xb, zero)]
    return jnp.concatenate(out, axis=0)


def _sb_unstack_t(acc_t):
    row = lax.broadcasted_iota(I32, (LANES, SB_BLOCK), 0)
    out = []
    for i in range(acc_t.shape[1] // (2 * SB_BLOCK)):
        a = acc_t[:, 2 * i * SB_BLOCK:(2 * i + 1) * SB_BLOCK]
        b = acc_t[:, (2 * i + 1) * SB_BLOCK:(2 * i + 2) * SB_BLOCK]
        out.append(jnp.transpose(jnp.where(row < SB_HEAD_DIM, a, b)))
    return jnp.concatenate(out, axis=0)


def _sb_tile_blocks(nq, q_blocks):
    nb = 4 if nq % 4 == 0 else (2 if nq % 2 == 0 else 1)
    return nb, min(nb, q_blocks)


def _sb_valid(u, qi0, nb, nqb):
    shape = (nb * SB_BLOCK, nqb * 2 * SB_BLOCK)
    key = u * (nb * SB_BLOCK) + lax.broadcasted_iota(I32, shape, 0)
    col = lax.broadcasted_iota(I32, shape, 1)
    qpos = (qi0 + col // (2 * SB_BLOCK)) * SB_BLOCK + col % SB_BLOCK
    return key < qpos


def _sb_logits(kb, qs, valid):
    z = _dot_nt(kb, qs)
    lb = jnp.minimum(z, 0.0) - jnp.log(1.0 + jnp.exp(-jnp.abs(z)))
    lk_all = lb - z
    lk = lk_all if valid is None else jnp.where(valid, lk_all, 0.0)
    return z, lb, lk_all, lk


def _sb_diag(x):
    w2 = 2 * SB_BLOCK
    ri = lax.broadcasted_iota(I32, (SB_BLOCK, w2), 0)
    ci = lax.broadcasted_iota(I32, (SB_BLOCK, w2), 1) % SB_BLOCK
    first = jnp.where(ri < ci, x[:, :w2], 0.0)
    return first if x.shape[1] == w2 else jnp.concatenate([first, x[:, w2:]], axis=1)


def _sb_add_from(full, part, lo):
    if lo == 0:
        return full + part
    return jnp.concatenate([full[:, :lo], full[:, lo:] + part], axis=1)


def _sb_scan(tri2, x, nb, reverse, exact=True):
    blk = SB_BLOCK
    edge = 0 if reverse else blk - 1
    carry = jnp.zeros((1, x.shape[1]), F32)
    res = [None] * nb
    for i in (reversed(range(nb)) if reverse else range(nb)):
        part = x[i * blk:(i + 1) * blk]
        if exact:
            hi, lo = _split2(part)
            raw = _dot(tri2, jnp.concatenate([hi, lo], axis=0))
        else:
            raw = _dot(tri2[:, :blk], part.astype(BF16))
        res[i] = raw + carry
        carry = carry + (raw[edge:edge + 1] + part[edge:edge + 1])
    return jnp.concatenate(res, axis=0), carry


def _sb_fwd(q, kv, kvt, *, name):
    bsz, s, w = q.shape
    blk = SB_BLOCK
    npair = w // LANES
    nq = s // blk
    nb, nqb = _sb_tile_blocks(nq, SB_Q_BLOCKS_FWD)
    width = nqb * 2 * blk

    def body(q_ref, k_ref, vt_ref, o_ref, tot_ref):
        qi0 = pl.program_id(2) * nqb
        qs = _sb_stack(q_ref[0] * SB_SCALE)
        ri = lax.broadcasted_iota(I32, (blk, blk), 0)
        ci = lax.broadcasted_iota(I32, (blk, blk), 1)
        upper = (ri < ci).astype(BF16)
        tri2 = jnp.concatenate([upper, upper], axis=1)

        def tile_phases(u, carry, masked):
            rows = pl.ds(pl.multiple_of(u * (nb * blk), nb * blk), nb * blk)
            valid = _sb_valid(u, qi0, nb, nqb) if masked else None
            _, lb, _, lk = _sb_logits(k_ref[0, rows, :], qs, valid)
            yield
            sfx, total = _sb_scan(tri2, lk, nb, True)
            yield
            wgt = jnp.exp(lb + sfx + carry["r"])
            if masked:
                wgt = jnp.where(valid, wgt, 0.0)
            carry["r"] = carry["r"] + total
            wb = wgt.astype(BF16)
            yield
            for i in range(nb):
                carry["acc"] = carry["acc"] + _dot(vt_ref[0, 0, u * nb + i], wb[i * blk:(i + 1) * blk])

        def tile(us, r, acc, masked):
            carry = {"r": r, "acc": acc}
            _round_robin([tile_phases(u, carry, masked) for u in us])
            return carry["r"], carry["acc"]

        def top_tile(u):
            carry = {"r": jnp.zeros((1, width), F32), "acc": jnp.zeros((LANES, width), F32)}

            def block(kb):
                lo = kb * 2 * blk
                rows = pl.ds(pl.multiple_of((u * nb + kb) * blk, blk), blk)
                _, lb, lk_all, _ = _sb_logits(k_ref[0, rows, :], qs[lo:], None)
                lk = _sb_diag(lk_all)
                yield
                hi, lo_part = _split2(lk)
                raw = _dot(tri2, jnp.concatenate([hi, lo_part], axis=0))
                yield
                wgt = _sb_diag(jnp.exp(lb + raw + carry["r"][:, lo:]))
                carry["r"] = _sb_add_from(carry["r"], raw[0:1] + lk[0:1], lo)
                yield
                carry["acc"] = _sb_add_from(carry["acc"], _dot(vt_ref[0, 0, u * nb + kb], wgt.astype(BF16)), lo)

            _round_robin([block(kb) for kb in reversed(range(nb))])
            return carry["r"], carry["acc"]

        top = qi0 // nb
        zero_r, zero_acc = jnp.zeros((1, width), F32), jnp.zeros((LANES, width), F32)
        r, acc = top_tile(top) if nb == nqb else tile([top], zero_r, zero_acc, True)
        r, acc = lax.fori_loop(
            0, top // 2, lambda t, c: tile([top - 1 - 2 * t, top - 2 - 2 * t], c[0], c[1], False), (r, acc))
        r, acc = lax.fori_loop(0, top % 2, lambda t, c: tile([0], c[0], c[1], False), (r, acc))
        o_ref[0] = _sb_unstack_t(acc).astype(BF16)
        tot_ref[0, 0, 0] = r

    qspec = pl.BlockSpec((1, nqb * blk, LANES), lambda b, p, i: (b, i, p))
    return pl.pallas_call(
        body, name=name, grid=(bsz, npair, nq // nqb),
        in_specs=[qspec,
                  pl.BlockSpec((1, s, LANES), lambda b, p, i: (b, 0, p)),
                  pl.BlockSpec((1, 1, nq, LANES, blk), lambda b, p, i: (b, npair + p, 0, 0, 0))],
        out_specs=[qspec, pl.BlockSpec((1, 1, 1, 1, width), lambda b, p, i: (b, p, i, 0, 0))],
        out_shape=[jax.ShapeDtypeStruct((bsz, s, w), BF16),
                   jax.ShapeDtypeStruct((bsz, npair, nq // nqb, 1, width), F32)],
        compiler_params=_params("parallel", "parallel", "arbitrary"),
    )(q, kv, kvt)


def _kv_blocks_t(kv3):
    bsz, s, w2 = kv3.shape
    x = kv3.reshape(bsz, s // SB_BLOCK, SB_BLOCK, w2 // LANES, LANES)
    return jnp.transpose(x, (0, 3, 1, 4, 2))


def _sb_bwd(q, kv, kvt, do, tot, dk_in, dv_in, *, name):
    bsz, s, w = q.shape
    blk = SB_BLOCK
    npair = w // LANES
    nq = s // blk
    nb, nqb = _sb_tile_blocks(nq, SB_Q_BLOCKS_BWD)
    width = nqb * 2 * blk
    tot = tot.reshape(bsz, npair, nq // nqb, 1, width)
    has_init = dk_in is not None

    def body(*refs):
        if has_init:
            q_ref, k_ref, v_ref, kt_ref, do_ref, tot_ref, dki_ref, dvi_ref, dq_ref, dk_ref, dv_ref = refs
        else:
            q_ref, k_ref, v_ref, kt_ref, do_ref, tot_ref, dq_ref, dk_ref, dv_ref = refs
        qi0 = pl.program_id(2) * nqb

        @pl.when(qi0 == 0)
        def _():
            if has_init:
                dk_ref[...] = dki_ref[...]
                dv_ref[...] = dvi_ref[...]
            else:
                dk_ref[...] = jnp.zeros_like(dk_ref)
                dv_ref[...] = jnp.zeros_like(dv_ref)

        qs = _sb_stack(q_ref[0] * SB_SCALE)
        dos = _sb_stack(do_ref[0])
        totv = tot_ref[0, 0, 0]
        ri = lax.broadcasted_iota(I32, (blk, blk), 0)
        ci = lax.broadcasted_iota(I32, (blk, blk), 1)
        lower = (ri > ci).astype(BF16)
        tri2 = jnp.concatenate([lower, lower], axis=1)

        def tile_phases(u, carry, masked):
            rows = pl.ds(pl.multiple_of(u * (nb * blk), nb * blk), nb * blk)
            valid = _sb_valid(u, qi0, nb, nqb) if masked else None
            z, lb, lk_all, lk = _sb_logits(k_ref[0, rows, :], qs, valid)
            yield
            before, tot_lk = _sb_scan(tri2, lk, nb, False)
            yield
            wgt = jnp.exp(z + ((totv - carry["pre_lk"]) - before))
            if masked:
                wgt = jnp.where(valid, wgt, 0.0)
            carry["pre_lk"] = carry["pre_lk"] + tot_lk
            dlogit = _dot_nt(v_ref[0, rows, :], dos) * wgt
            yield
            dbefore, tot_d = _sb_scan(tri2, dlogit, nb, False, exact=False)
            yield
            sig = jnp.exp(lb)
            dz = dlogit * (1.0 - sig) - (carry["pre_d"] + dbefore) * sig
            if masked:
                dz = jnp.where(valid, dz, 0.0)
            carry["pre_d"] = carry["pre_d"] + tot_d
            dz_b = dz.astype(BF16)
            yield
            for i in range(nb):
                carry["dqt"] = carry["dqt"] + _dot(kt_ref[0, 0, u * nb + i], dz_b[i * blk:(i + 1) * blk])
            dk_ref[0, rows, :] += _dot(dz_b, qs)
            dv_ref[0, rows, :] += _dot(wgt.astype(BF16), dos)

        def tile(us, pre_lk, pre_d, dqt, masked):
            carry = {"pre_lk": pre_lk, "pre_d": pre_d, "dqt": dqt}
            _round_robin([tile_phases(u, carry, masked) for u in us])
            return carry["pre_lk"], carry["pre_d"], carry["dqt"]

        def top_tile(u, pre_lk, pre_d, dqt):
            carry = {"rest": totv - pre_lk, "pre_d": pre_d, "dqt": dqt}

            def block(kb):
                lo = kb * 2 * blk
                rows = pl.ds(pl.multiple_of((u * nb + kb) * blk, blk), blk)
                qs_k, dos_k = qs[lo:], dos[lo:]
                z, lb, lk_all, _ = _sb_logits(k_ref[0, rows, :], qs_k, None)
                lk = _sb_diag(lk_all)
                yield
                hi, lo_part = _split2(lk)
                raw = _dot(tri2, jnp.concatenate([hi, lo_part], axis=0))
                yield
                wgt = _sb_diag(jnp.exp(z + (carry["rest"][:, lo:] - raw)))
                carry["rest"] = _sb_add_from(carry["rest"], -(raw[blk - 1:blk] + lk[blk - 1:blk]), lo)
                dlogit = _dot_nt(v_ref[0, rows, :], dos_k) * wgt
                yield
                draw = _dot(tri2[:, :blk], dlogit.astype(BF16))
                yield
                sig = jnp.exp(lb)
                dz_b = _sb_diag(dlogit * (1.0 - sig) - (carry["pre_d"][:, lo:] + draw) * sig).astype(BF16)
                carry["pre_d"] = _sb_add_from(carry["pre_d"], draw[blk - 1:blk] + dlogit[blk - 1:blk], lo)
                yield
                carry["dqt"] = _sb_add_from(carry["dqt"], _dot(kt_ref[0, 0, u * nb + kb], dz_b), lo)
                dk_ref[0, rows, :] += _dot(dz_b, qs_k)
                dv_ref[0, rows, :] += _dot(wgt.astype(BF16), dos_k)

            _round_robin([block(kb) for kb in range(nb)])
            return carry["dqt"]

        zero = jnp.zeros((1, width), F32)
        top = qi0 // nb
        c = lax.fori_loop(0, top // 2, lambda t, c: tile([2 * t, 2 * t + 1], c[0], c[1], c[2], False),
                          (zero, zero, jnp.zeros((LANES, width), F32)))
        c = lax.fori_loop(0, top % 2, lambda t, c: tile([top - 1], c[0], c[1], c[2], False), c)
        dqt = top_tile(top, *c) if nb == nqb else tile([top], c[0], c[1], c[2], True)[2]
        dq_ref[0] = (_sb_unstack_t(dqt) * SB_SCALE).astype(BF16)

    qspec = pl.BlockSpec((1, nqb * blk, LANES), lambda b, p, i: (b, i, p))
    kspec = pl.BlockSpec((1, s, LANES), lambda b, p, i: (b, 0, p))
    vspec = pl.BlockSpec((1, s, LANES), lambda b, p, i: (b, 0, npair + p))
    ktspec = pl.BlockSpec((1, 1, nq, LANES, blk), lambda b, p, i: (b, p, 0, 0, 0))
    tspec = pl.BlockSpec((1, 1, 1, 1, width), lambda b, p, i: (b, p, i, 0, 0))
    in_specs = [qspec, kspec, vspec, ktspec, qspec, tspec] + ([kspec, kspec] if has_init else [])
    args = (q, kv, kv, kvt, do, tot) + ((dk_in, dv_in) if has_init else ())
    return pl.pallas_call(
        body, name=name, grid=(bsz, npair, nq // nqb), in_specs=in_specs,
        out_specs=[qspec, kspec, kspec],
        out_shape=[jax.ShapeDtypeStruct((bsz, s, w), BF16), jax.ShapeDtypeStruct((bsz, s, w), F32),
                   jax.ShapeDtypeStruct((bsz, s, w), F32)],
        compiler_params=_params("parallel", "parallel", "arbitrary"),
    )(*args)


ADAM_BLOCK_BYTES = 1 << 20


def _adamw(w, g, m, v, *, name):
    shape = w.shape
    r, c = shape[-2], shape[-1]
    lead = _size(shape[:-2])
    tr = r
    for cand in range(8, r, 8):
        if r % cand == 0 and cand * c * 4 <= ADAM_BLOCK_BYTES:
            tr = cand
    if r * c * 4 <= ADAM_BLOCK_BYTES:
        tr = r

    def body(w_ref, g_ref, m_ref, v_ref, d_ref, mo_ref, vo_ref):
        gv = g_ref[...]
        mn = ADAM_B1 * m_ref[...] + (1.0 - ADAM_B1) * gv
        vn = ADAM_B2 * v_ref[...] + (1.0 - ADAM_B2) * (gv * gv)
        m_hat = mn / (1.0 - ADAM_B1 ** ADAM_STEP)
        v_hat = vn / (1.0 - ADAM_B2 ** ADAM_STEP)
        d_ref[...] = -ADAM_LR * (m_hat / (jnp.sqrt(v_hat) + ADAM_EPS) + ADAM_WD * w_ref[...])
        mo_ref[...] = mn
        vo_ref[...] = vn

    blk = pl.BlockSpec((1, tr, c), lambda l, i: (l, i, 0))
    outs = pl.pallas_call(
        body, name=name, grid=(lead, r // tr), in_specs=[blk] * 4, out_specs=[blk] * 3,
        out_shape=[jax.ShapeDtypeStruct((lead, r, c), F32)] * 3,
        compiler_params=_params("parallel", "parallel"),
    )(*[a.reshape(lead, r, c) for a in (w, g, m, v)])
    return [o.reshape(shape) for o in outs]


def _row_tile(r, c, itemsize):
    if r * c * 4 <= ADAM_BLOCK_BYTES:
        return r
    step = 32 // itemsize
    tr = r
    for cand in range(step, r, step):
        if r % cand == 0 and cand * c * 4 <= ADAM_BLOCK_BYTES:
            tr = cand
    return tr


def _add_own_half(idx, g, recv, *, name):
    _, lh, r, c = recv.shape
    tr = _row_tile(r, c, g.dtype.itemsize)

    def body(idx_ref, a_ref, b_ref, o_ref):
        o_ref[...] = (a_ref[...].astype(F32) + b_ref[...].astype(F32)).astype(o_ref.dtype)

    blk = pl.BlockSpec((1, 1, tr, c), lambda k, l, i, idx: (k, l, i, 0))
    return pl.pallas_call(
        body, name=name,
        grid_spec=pltpu.PrefetchScalarGridSpec(
            num_scalar_prefetch=1, grid=(N_CHIPS, lh, r // tr),
            in_specs=[pl.BlockSpec((1, 1, tr, c), lambda k, l, i, idx: (k, idx[0] * lh + l, i, 0)), blk],
            out_specs=blk),
        out_shape=jax.ShapeDtypeStruct(recv.shape, g.dtype),
        compiler_params=_params("parallel", "parallel", "parallel"),
    )(idx, g, recv)


def _add_chips(idx, own, recv, *, name):
    _, lh, r, c = own.shape
    tr = _row_tile(r, c, own.dtype.itemsize)

    def body(idx_ref, a_ref, b_ref, o_ref):
        f = lambda v: v.astype(F32)
        o_ref[0] = ((f(a_ref[0, 0]) + f(b_ref[0, 0])) + f(b_ref[1, 0])) + f(b_ref[2, 0])

    return pl.pallas_call(
        body, name=name,
        grid_spec=pltpu.PrefetchScalarGridSpec(
            num_scalar_prefetch=1, grid=(lh, r // tr),
            in_specs=[pl.BlockSpec((1, 1, tr, c), lambda l, i, idx: (idx[0], l, i, 0)),
                      pl.BlockSpec((3, 1, tr, c), lambda l, i, idx: (0, l, i, 0))],
            out_specs=pl.BlockSpec((1, tr, c), lambda l, i, idx: (l, i, 0))),
        out_shape=jax.ShapeDtypeStruct((lh, r, c), F32),
        compiler_params=_params("parallel", "parallel"),
    )(idx, own, recv)


def _sum_devices(parts):
    _, r, _ = parts.shape

    def body(p_ref, o_ref):
        acc = p_ref[0]
        for k in range(1, N_DEV):
            acc = acc + p_ref[k]
        o_ref[...] = acc

    return pl.pallas_call(
        body, name="small_grad_sum", grid=(1,),
        in_specs=[pl.BlockSpec((N_DEV, r, LANES), lambda i: (0, 0, 0))],
        out_specs=pl.BlockSpec((r, LANES), lambda i: (0, 0)),
        out_shape=jax.ShapeDtypeStruct((r, LANES), F32),
    )(parts)


def _place():
    return lax.axis_index("x"), lax.axis_index("y"), lax.axis_index("c")


def _rcopy(src, dst, send_sems, recv_sems, k, to):
    return pltpu.make_async_remote_copy(src_ref=src, dst_ref=dst, send_sem=send_sems.at[k],
                                        recv_sem=recv_sems.at[k], device_id=to, device_id_type=MESH)


def _exchange_call(body, name, ins, out_shapes, n_sems):
    return pl.pallas_call(
        body, name=name, in_specs=[ANY] * len(ins), out_specs=[ANY] * len(out_shapes),
        out_shape=out_shapes,
        scratch_shapes=[pltpu.SemaphoreType.DMA((n_sems,)), pltpu.SemaphoreType.DMA((n_sems,))],
    )(*ins)


def _gather_weights(shards):
    n = len(shards)

    def body(*refs):
        ins, outs, send_sems, recv_sems = refs[:n], refs[n:2 * n], refs[2 * n], refs[2 * n + 1]
        x, y, c = _place()
        sibling = (x, y, 1 - c)
        chips = [(1 - x, y), (x, 1 - y), (1 - x, 1 - y)]

        def piece(i, px, py, pc):
            lh = ins[i].shape[0] // 2
            return outs[i].at[2 * px + py, pl.ds(pc * lh, lh)]

        def mine(i):
            lh = ins[i].shape[0] // 2
            return ins[i].at[pl.ds(c * lh, lh)]

        first = [_rcopy(mine(i), piece(i, x, y, c), send_sems, recv_sems, 6 * i + j, (*chip, c))
                 for i in range(n) for j, chip in enumerate(chips)]
        for cp in first:
            cp.start()
        passed = []
        for i in range(n):
            for j, chip in enumerate(chips):
                landed = piece(i, *chip, c)
                _rcopy(landed, landed, send_sems, recv_sems, 6 * i + j, (*chip, c)).wait_recv()
                passed.append(_rcopy(landed, landed, send_sems, recv_sems, 6 * i + 3 + j, sibling))
                passed[-1].start()
        for i in range(n):
            for j, chip in enumerate(chips):
                theirs = piece(i, *chip, 1 - c)
                _rcopy(theirs, theirs, send_sems, recv_sems, 6 * i + 3 + j, sibling).wait_recv()
        for cp in first + passed:
            cp.wait_send()

    shapes = [jax.ShapeDtypeStruct((N_CHIPS,) + s_.shape, s_.dtype) for s_ in shards]
    return _exchange_call(body, "gather_weights", shards, shapes, 6 * n)


def _swap_halves(gs):
    n = len(gs)

    def body(*refs):
        ins, outs, send_sems, recv_sems = refs[:n], refs[n:2 * n], refs[2 * n], refs[2 * n + 1]
        x, y, c = _place()
        cps = []
        for i in range(n):
            lh = ins[i].shape[1] // 2
            src = ins[i].at[pl.ds(0, N_CHIPS), pl.ds((1 - c) * lh, lh)]
            cps.append(_rcopy(src, outs[i], send_sems, recv_sems, i, (x, y, 1 - c)))
        for cp in cps:
            cp.start()
        for cp in cps:
            cp.wait()

    shapes = [jax.ShapeDtypeStruct((N_CHIPS, g.shape[1] // 2) + g.shape[2:], g.dtype) for g in gs]
    return _exchange_call(body, "grad_swap_halves", gs, shapes, n)


def _scatter_chunks(sums):
    n = len(sums)

    def body(*refs):
        ins, outs, send_sems, recv_sems = refs[:n], refs[n:2 * n], refs[2 * n], refs[2 * n + 1]
        x, y, c = _place()
        chips = [(1 - x, y), (x, 1 - y), (1 - x, 1 - y)]
        cps = [_rcopy(ins[i].at[2 * chip[0] + chip[1]], outs[i].at[j], send_sems, recv_sems, 3 * i + j,
                      (*chip, c)) for i in range(n) for j, chip in enumerate(chips)]
        for cp in cps:
            cp.start()
        for cp in cps:
            cp.wait()

    shapes = [jax.ShapeDtypeStruct((3,) + s_.shape[1:], s_.dtype) for s_ in sums]
    return _exchange_call(body, "grad_scatter_chunks", sums, shapes, 3 * n)


def _share_half(tots):
    n = len(tots)

    def body(*refs):
        ins, outs, send_sems, recv_sems = refs[:n], refs[n:2 * n], refs[2 * n], refs[2 * n + 1]
        x, y, c = _place()
        cps = [_rcopy(ins[i], outs[i], send_sems, recv_sems, i, (x, y, 1 - c)) for i in range(n)]
        for cp in cps:
            cp.start()
        for cp in cps:
            cp.wait()

    shapes = [jax.ShapeDtypeStruct(t_.shape, t_.dtype) for t_ in tots]
    return _exchange_call(body, "grad_share_half", tots, shapes, n)


def _exchange_small(r):
    rr, _ = r.shape

    def body(r_ref, out_ref, send_sems, recv_sems, local_sem):
        x, y, c = _place()
        me = 4 * x + 2 * y + c
        mine = pltpu.make_async_copy(r_ref, out_ref.at[me], local_sem)
        mine.start()
        cps = []
        for k in range(N_DEV - 1):
            fx, fy, fc = ((k + 1) >> 2) & 1, ((k + 1) >> 1) & 1, (k + 1) & 1
            to = (x ^ fx, y ^ fy, c ^ fc)
            cps.append((_rcopy(r_ref, out_ref.at[me], send_sems, recv_sems, k, to), to))
        for cp, _ in cps:
            cp.start()
        for k, (cp, to) in enumerate(cps):
            src = 4 * to[0] + 2 * to[1] + to[2]
            _rcopy(r_ref, out_ref.at[src], send_sems, recv_sems, k, to).wait_recv()
        for cp, _ in cps:
            cp.wait_send()
        mine.wait()

    return pl.pallas_call(
        body, name="small_grad_exchange", in_specs=[ANY], out_specs=ANY,
        out_shape=jax.ShapeDtypeStruct((N_DEV, rr, LANES), r.dtype),
        scratch_shapes=[pltpu.SemaphoreType.DMA((N_DEV - 1,)), pltpu.SemaphoreType.DMA((N_DEV - 1,)),
                        pltpu.SemaphoreType.DMA],
    )(r)


def _size(shape):
    n = 1
    for d in shape:
        n *= d
    return n


def _slab_rows(shape):
    rows = -(-_size(shape) // LANES)
    return -(-rows // SLAB_ROW_ALIGN) * SLAB_ROW_ALIGN


def _pack_rows(arrs, dtype, lead=0, unit=PACK_ROWS):
    parts, total = [], 0
    for a in arrs:
        front, shp = a.shape[:lead], a.shape[lead:]
        n, rows = _size(shp), _slab_rows(shp)
        nopad = [(0, 0)] * lead
        if n % LANES == 0:
            p = a.reshape(front + (n // LANES, LANES)).astype(dtype)
        else:
            p = jnp.pad(a.reshape(front + (n,)).astype(dtype), nopad + [(0, rows * LANES - n)])
            p = p.reshape(front + (rows, LANES))
        if p.shape[lead] != rows:
            p = jnp.pad(p, nopad + [(0, rows - p.shape[lead]), (0, 0)])
        parts.append(p)
        total += rows
    pad = (-total) % unit
    if pad:
        parts.append(jnp.zeros(parts[0].shape[:lead] + (pad, LANES), dtype))
    return jnp.concatenate(parts, axis=lead)


def _unpack_rows(slab, shapes):
    lead = slab.shape[:-2]
    out, off = [], 0
    for shp in shapes:
        n, rows = _size(shp), _slab_rows(shp)
        piece = slab[..., off:off + rows, :]
        if n % LANES == 0:
            piece = piece[..., :n // LANES, :].reshape(lead + tuple(shp))
        else:
            piece = piece.reshape(lead + (rows * LANES,))[..., :n].reshape(lead + tuple(shp))
        out.append(piece)
        off += rows
    return out


def _ffn_fwd(h, bsz, s, gain, w_up, cw, cb, w_down, i):
    hf = _rmsnorm_fwd(h, gain, name=f"ffn_norm_{i}")
    up = _mm(hf, w_up, name=f"ffn_up_{i}")
    up3 = up.reshape(bsz, s, -1)
    f = _conv_glu_fwd(up3, cw, cb, name=f"ffn_glu_{i}").reshape(h.shape[0], -1)
    h2 = _mm(f, w_down, add=h, name=f"ffn_down_{i}")
    return h2, (h, hf, up3, f)


def _ffn_bwd(dh, saved, gain, w_up, cw, cb, w_down, i):
    h, hf, up3, f = saved
    t = h.shape[0]
    d_down = _mm(f, dh, ta=True, name=f"ffn_down_dw_{i}")
    df = _mm(dh, w_down, tb=True, name=f"ffn_down_dx_{i}")
    dug, duv, dcw, dcb = _conv_glu_bwd(up3, cw, cb, df.reshape(up3.shape[0], up3.shape[1], -1),
                                       name=f"ffn_glu_bwd_{i}")
    dug, duv = dug.reshape(t, -1), duv.reshape(t, -1)
    fdim = dug.shape[1]
    d_up = jnp.concatenate([_mm(hf, dug, ta=True, name=f"ffn_up_dwg_{i}"),
                            _mm(hf, duv, ta=True, name=f"ffn_up_dwv_{i}")], axis=1)
    dhf = _mm(dug, w_up[:, :fdim], tb=True, name=f"ffn_up_dxg_{i}")
    dh, dgain = _mm_norm_bwd(duv, w_up[:, fdim:], h, gain, dh, add=dhf, name=f"ffn_up_dxv_{i}")
    return dh, dgain, d_up, dcw, dcb, d_down


def _ple_layer_fwd(h, p_i, gain, w_gate, w_proj, i):
    hp = _rmsnorm_fwd(h, gain, name=f"ple_norm_{i}")
    a = _mm(hp, w_gate, name=f"ple_gate_{i}")
    pp = _mm(p_i, w_proj, name=f"ple_proj_{i}")
    return _ple_fwd(h, a, pp, name=f"ple_mix_{i}"), (h, hp, a, pp)


def _ple_layer_bwd(dh, saved, p_i, gain, w_gate, i):
    h, hp, a, pp = saved
    da, dpp = _ple_bwd(dh, a, pp, name=f"ple_mix_bwd_{i}")
    d_gate = _mm(hp, da, ta=True, name=f"ple_gate_dw_{i}")
    d_proj = _mm(p_i, dpp, ta=True, name=f"ple_proj_dw_{i}")
    dh, dgain = _mm_norm_bwd(da, w_gate, h, gain, dh, name=f"ple_gate_dx_{i}")
    return dh, dgain, d_gate, d_proj


def _ssm_consts(dt_bias, a_log, d_skip, g_n):
    hpg = SSM_HEADS_PER_GROUP
    a = -jnp.exp(a_log)
    rows = jnp.stack([dt_bias.reshape(g_n, hpg), a.reshape(g_n, hpg)], axis=1)
    plane = jnp.zeros((g_n, 8, LANES), F32).at[:, 0:2, 0:hpg].set(rows)
    psub = jnp.zeros((g_n, SSM_ROWS, LANES), F32).at[:, 0:hpg, 0:2].set(jnp.swapaxes(rows, 1, 2))
    d_ch = jnp.repeat(d_skip, SSM_HEAD_DIM).reshape(1, -1)
    return a, plane, psub, d_ch


def _ssm_in_big(w_in, d_inner, g_n):
    d = w_in.shape[0]
    cut = w_in.shape[1] - g_n * SSM_HEADS_PER_GROUP
    wdt = w_in[:, cut:].reshape(d, g_n, SSM_HEADS_PER_GROUP)
    wdt = jnp.pad(wdt, ((0, 0), (0, 0), (0, LANES - SSM_HEADS_PER_GROUP))).reshape(d, g_n * LANES)
    return jnp.concatenate([w_in[:, :cut], wdt], axis=1)


def _ssm_in_small(dw_big, g_n):
    d = dw_big.shape[0]
    cut = dw_big.shape[1] - g_n * LANES
    ddt = dw_big[:, cut:].reshape(d, g_n, LANES)[:, :, :SSM_HEADS_PER_GROUP].reshape(d, -1)
    return jnp.concatenate([dw_big[:, :cut], ddt], axis=1)


def _ssm_fwd(h, bsz, s, gain, w_in_big, cw, cb, plane, psub, d_ch, nw, w_out, i):
    d_inner = d_ch.shape[1]
    g_n = d_inner // SSM_GROUP_W
    conv_dim = cw.shape[1]
    hn = _rmsnorm_fwd(h, gain, name=f"attn_norm_{i}")
    zx = _mm(hn, w_in_big, name=f"ssm_in_{i}").reshape(bsz, s, -1)
    xbc = _conv_silu_fwd(zx, d_inner, cw, cb, name=f"ssm_conv_{i}")
    dtr = zx[:, :, d_inner + conv_dim:].reshape(bsz, s, g_n, LANES)[..., :SSM_HEADS_PER_GROUP]
    dtr_row = jnp.pad(jnp.transpose(dtr, (0, 2, 3, 1)),
                      ((0, 0), (0, 0), (0, SSM_ROWS - SSM_HEADS_PER_GROUP), (0, 0)))
    gn, states = _ssd_fwd(zx, xbc, dtr_row, plane, psub, d_ch, nw, name=f"ssd_{i}")
    gn2 = gn.reshape(h.shape[0], -1)
    h1 = _mm(gn2, w_out, add=h, name=f"ssm_out_{i}")
    return h1, (h, hn, zx, xbc, dtr_row, states, gn2)


def _ssm_bwd(dh, saved, gain, w_in_big, cw, cb, plane, psub, d_ch, nw, w_out, i):
    h, hn, zx, xbc, dtr_row, states, gn2 = saved
    t = h.shape[0]
    bsz, s, _ = zx.shape
    d_inner = d_ch.shape[1]
    d_out = _mm(gn2, dh, ta=True, name=f"ssm_out_dw_{i}")
    dgn = _mm(dh, w_out, tb=True, name=f"ssm_out_dx_{i}").reshape(bsz, s, -1)
    dxs, dbm, dcm, dz, ddtr, ach, aln = _ssd_bwd(zx, xbc, dtr_row, plane, psub, d_ch, nw, states, dgn,
                                                  name=f"ssd_bwd_{i}")
    dxbc, dcw, dcb = _conv_silu_bwd(zx, d_inner, cw, cb, [dxs, dbm, dcm], name=f"ssm_conv_bwd_{i}")
    d_in_parts, dhn, col = [], None, 0
    for tag, part in (("z", dz), ("dt", ddtr), ("xbc", dxbc)):
        part = part.reshape(t, -1)
        col = {"z": 0, "xbc": dz.shape[-1], "dt": dz.shape[-1] + dxbc.shape[-1]}[tag]
        w_part = w_in_big[:, col:col + part.shape[1]]
        d_in_parts.append(_mm(hn, part, ta=True, name=f"ssm_in_dw_{tag}_{i}"))
        if tag == "xbc":
            dh, dgain = _mm_norm_bwd(part, w_part, h, gain, dh, add=dhn, name=f"ssm_in_dx_{tag}_{i}")
        else:
            dhn = _mm(part, w_part, tb=True, add=dhn, name=f"ssm_in_dx_{tag}_{i}")
    d_in_big = jnp.concatenate([d_in_parts[0], d_in_parts[2], d_in_parts[1]], axis=1)
    hpg = SSM_HEADS_PER_GROUP
    ach = jnp.sum(ach, axis=0)
    aln = jnp.sum(aln, axis=0)
    d_nw = ach[:, 0, :].reshape(-1)
    d_dskip = jnp.sum(ach[:, 1, :].reshape(-1, SSM_HEAD_DIM), axis=1)
    d_bias = aln[:, 0, :hpg].reshape(-1)
    d_a = aln[:, 1, :hpg].reshape(-1)
    return dh, dgain, d_in_big, dcw, dcb, d_bias, d_a, d_dskip, d_nw, d_out


def _sb_layer_fwd(h, bsz, s, gain, w_q, w_o, kv3, kvt, i):
    hn = _rmsnorm_fwd(h, gain, name=f"attn_norm_{i}")
    q3 = _mm(hn, w_q, out_dtype=BF16, name=f"sb_q_{i}").reshape(bsz, s, -1)
    o3, tot = _sb_fwd(q3, kv3, kvt, name=f"sb_attn_{i}")
    o2 = o3.reshape(h.shape[0], -1)
    h1 = _mm(o2, w_o, add=h, name=f"sb_o_{i}")
    return h1, (h, hn, q3, o2, tot)


def _sb_layer_bwd(dh, saved, gain, w_q, w_o, kv3, kvt, dk, dv, i):
    h, hn, q3, o2, tot = saved
    t = h.shape[0]
    d_o = _mm(o2, dh, ta=True, name=f"sb_o_dw_{i}")
    do3 = _mm(dh, w_o, tb=True, out_dtype=BF16, name=f"sb_o_dx_{i}").reshape(q3.shape)
    dq3, dk, dv = _sb_bwd(q3, kv3, kvt, do3, tot, dk, dv, name=f"sb_attn_bwd_{i}")
    dq = dq3.reshape(t, -1)
    d_q = _mm(hn, dq, ta=True, name=f"sb_q_dw_{i}")
    dh, dgain = _mm_norm_bwd(dq, w_q, h, gain, dh, name=f"sb_q_dx_{i}")
    return dh, dgain, d_q, d_o, dk, dv


def kernel(x, p, attn_norm, ffn_norm, ple_norm, ssm_in_proj, ssm_conv_w, ssm_conv_b, ssm_dt_bias, ssm_a_log, ssm_d, ssm_norm, ssm_out_proj, kv_norm, w_kv, w_q, w_o, ffn_up, ffn_conv_w, ffn_conv_b, ffn_down, ple_gate, ple_proj, final_norm, loss_target, m_attn_norm, m_ffn_norm, m_ple_norm, m_ssm_in_proj, m_ssm_conv_w, m_ssm_conv_b, m_ssm_dt_bias, m_ssm_a_log, m_ssm_d, m_ssm_norm, m_ssm_out_proj, m_kv_norm, m_w_kv, m_w_q, m_w_o, m_ffn_up, m_ffn_conv_w, m_ffn_conv_b, m_ffn_down, m_ple_gate, m_ple_proj, m_final_norm, v_attn_norm, v_ffn_norm, v_ple_norm, v_ssm_in_proj, v_ssm_conv_w, v_ssm_conv_b, v_ssm_dt_bias, v_ssm_a_log, v_ssm_d, v_ssm_norm, v_ssm_out_proj, v_kv_norm, v_w_kv, v_w_q, v_w_o, v_ffn_up, v_ffn_conv_w, v_ffn_conv_b, v_ffn_down, v_ple_gate, v_ple_proj, v_final_norm):
    given = dict(locals())
    wl = {n: given[n] for n in WEIGHTS}
    bsz, s, d = x.shape
    t = bsz * s
    depth = attn_norm.shape[0]
    n_a = ssm_in_proj.shape[0]
    d_inner = ssm_norm.shape[1] * N_CHIPS
    g_n = d_inner // SSM_GROUP_W
    cidx = lax.axis_index("c").astype(I32).reshape(1)
    chip_idx = (2 * lax.axis_index("x") + lax.axis_index("y")).astype(I32).reshape(1)

    big = [n for n in SHARDED if _size(wl[n].shape) >= BIG_WEIGHT]
    small = [n for n in SHARDED if n not in big]
    small_shapes = [wl[n].shape for n in small]
    halves = lambda shp: shp if len(shp) == 3 else (2, shp[0] // 2, shp[1])
    small_slab = _pack_rows([wl[n] for n in small], BF16, unit=2 * SLAB_ROW_ALIGN)
    small_rows = small_slab.shape[0]
    mine = [wl[n].astype(BF16).reshape(halves(wl[n].shape)) for n in big]
    mine.append(small_slab.reshape(2, small_rows // 2, LANES))
    gathered = [lax.dynamic_update_index_in_dim(g, m_, chip_idx[0], 0)
                for g, m_ in zip(_gather_weights(mine), mine)]
    per_chip = {n: g.reshape((N_CHIPS,) + wl[n].shape) for n, g in zip(big, gathered)}
    per_chip.update(zip(small, _unpack_rows(gathered[-1].reshape(N_CHIPS, small_rows, LANES), small_shapes)))
    full = {}
    for n in SHARDED:
        ax, piece = SHARD_AXIS[n], per_chip[n]
        merged = piece.shape[1:ax + 1] + (N_CHIPS * piece.shape[ax + 1],) + piece.shape[ax + 2:]
        full[n] = jnp.moveaxis(piece, 0, ax).reshape(merged)

    h = x.reshape(t, d)
    tgt = loss_target.reshape(t, d)
    saved = []
    kv3 = kvt = hkv = h_kv_in = None
    consts = []
    for i in range(depth):
        if i < n_a:
            a_neg, plane, psub, d_ch = _ssm_consts(ssm_dt_bias[i], ssm_a_log[i], ssm_d[i], g_n)
            w_in_big = _ssm_in_big(full["ssm_in_proj"][i], d_inner, g_n)
            cw = full["ssm_conv_w"][i].astype(F32)
            cb = full["ssm_conv_b"][i].astype(F32)
            nw = full["ssm_norm"][i].astype(F32).reshape(1, -1)
            consts.append((a_neg, plane, psub, d_ch, w_in_big, cw, cb, nw))
            h, sv_mix = _ssm_fwd(h, bsz, s, attn_norm[i], w_in_big, cw, cb, plane, psub, d_ch, nw,
                                 full["ssm_out_proj"][i], i)
        else:
            j = i - n_a
            h, sv_mix = _sb_layer_fwd(h, bsz, s, attn_norm[i], full["w_q"][j], full["w_o"][j], kv3, kvt, i)
        fcw = full["ffn_conv_w"][i].astype(F32)
        h, sv_ffn = _ffn_fwd(h, bsz, s, ffn_norm[i], full["ffn_up"][i], fcw, ffn_conv_b[i],
                             full["ffn_down"][i], i)
        p_i = p[i].reshape(t, -1)
        h, sv_ple = _ple_layer_fwd(h, p_i, ple_norm[i], full["ple_gate"][i], full["ple_proj"][i], i)
        saved.append((sv_mix, sv_ffn, sv_ple))
        if i == n_a - 1:
            h_kv_in = h
            hkv = _rmsnorm_fwd(h, kv_norm, name="kv_norm")
            kv3 = _mm(hkv, full["w_kv"], out_dtype=BF16, name="kv_proj").reshape(bsz, s, -1)
            kvt = _kv_blocks_t(kv3)

    loss_local, dh, g_final = _final_loss(h, final_norm, tgt)
    gr = {n: [None] * wl[n].shape[0] for n in WEIGHTS if n not in ("kv_norm", "w_kv", "final_norm")}
    gr["final_norm"] = g_final
    dk = dv = None
    for i in reversed(range(depth)):
        sv_mix, sv_ffn, sv_ple = saved[i]
        if i == n_a - 1:
            dkv = jnp.concatenate([dk, dv], axis=-1).reshape(t, -1)
            gr["w_kv"] = _mm(hkv, dkv, ta=True, name="kv_proj_dw")
            dh, gr["kv_norm"] = _mm_norm_bwd(dkv, full["w_kv"], h_kv_in, kv_norm, dh, name="kv_proj_dx")
        p_i = p[i].reshape(t, -1)
        dh, gr["ple_norm"][i], gr["ple_gate"][i], gr["ple_proj"][i] = _ple_layer_bwd(
            dh, sv_ple, p_i, ple_norm[i], full["ple_gate"][i], i)
        fcw = full["ffn_conv_w"][i].astype(F32)
        (dh, gr["ffn_norm"][i], gr["ffn_up"][i], gr["ffn_conv_w"][i], gr["ffn_conv_b"][i],
         gr["ffn_down"][i]) = _ffn_bwd(dh, sv_ffn, ffn_norm[i], full["ffn_up"][i], fcw, ffn_conv_b[i],
                                       full["ffn_down"][i], i)
        if i < n_a:
            a_neg, plane, psub, d_ch, w_in_big, cw, cb, nw = consts[i]
            (dh, gr["attn_norm"][i], d_in_big, gr["ssm_conv_w"][i], gr["ssm_conv_b"][i],
             gr["ssm_dt_bias"][i], d_a, gr["ssm_d"][i], gr["ssm_norm"][i],
             gr["ssm_out_proj"][i]) = _ssm_bwd(dh, sv_mix, attn_norm[i], w_in_big, cw, cb, plane, psub,
                                               d_ch, nw, full["ssm_out_proj"][i], i)
            gr["ssm_in_proj"][i] = _ssm_in_small(d_in_big, g_n)
            gr["ssm_a_log"][i] = d_a * a_neg
        else:
            j = i - n_a
            dh, gr["attn_norm"][i], gr["w_q"][j], gr["w_o"][j], dk, dv = _sb_layer_bwd(
                dh, sv_mix, attn_norm[i], full["w_q"][j], full["w_o"][j], kv3, kvt, dk, dv, i)
    grad_x = dh.reshape(bsz, s, d)
    gfull = {n: (jnp.stack(v) if isinstance(v, list) else v) for n, v in gr.items()}

    by_chip = {}
    for n in SHARDED:
        ax, shp = SHARD_AXIS[n], gfull[n].shape
        split = gfull[n].reshape(shp[:ax] + (N_CHIPS, shp[ax] // N_CHIPS) + shp[ax + 1:])
        by_chip[n] = jnp.moveaxis(split, ax, 0)
    g4 = [by_chip[n].astype(BF16).reshape((N_CHIPS,) + halves(wl[n].shape)) for n in big]
    g4.append(_pack_rows([by_chip[n] for n in small], BF16, lead=1, unit=2 * SLAB_ROW_ALIGN)
              .reshape(N_CHIPS, 2, small_rows // 2, LANES))
    tags = big + ["small"]
    from_sibling = _swap_halves(g4)
    chip_sums = [_add_own_half(cidx, g, r_, name="grad_pair_sum_" + tg)
                 for g, r_, tg in zip(g4, from_sibling, tags)]
    from_chips = _scatter_chunks(chip_sums)
    my_half = [_add_chips(chip_idx, s_, r_, name="grad_chip_sum_" + tg)
               for s_, r_, tg in zip(chip_sums, from_chips, tags)]
    other_half = _share_half(my_half)
    low_core = cidx[0] == 0
    reduced = [jnp.concatenate([jnp.where(low_core, a, b_), jnp.where(low_core, b_, a)], axis=0)
               for a, b_ in zip(my_half, other_half)]
    grads = {n: g.reshape(wl[n].shape) for n, g in zip(big, reduced)}
    grads.update(zip(small, _unpack_rows(reduced[-1].reshape(small_rows, LANES), small_shapes)))

    rep_shapes = [wl[n].shape for n in REPLICATED]
    packed_r = _pack_rows([gfull[n] for n in REPLICATED], F32, unit=SLAB_ROW_ALIGN)
    g_rep = _sum_devices(_exchange_small(packed_r))

    delta, new_m, new_v = {}, {}, {}
    for n in SHARDED:
        delta[n], new_m[n], new_v[n] = _adamw(wl[n], grads[n], given["m_" + n], given["v_" + n],
                                              name="adamw_" + n)
    slabs = [_pack_rows([src[pre + n] for n in REPLICATED], F32, unit=SLAB_ROW_ALIGN)
             for src, pre in ((wl, ""), (given, "m_"), (given, "v_"))]
    rep_out = _adamw(slabs[0], g_rep, slabs[1], slabs[2], name="adamw_replicated")
    for dst, slab in zip((grads, delta, new_m, new_v), [g_rep] + list(rep_out)):
        dst.update(zip(REPLICATED, _unpack_rows(slab, rep_shapes)))
    loss = lax.psum(loss_local, ("x", "y", "c"))
    return (loss, grad_x, *[grads[n] for n in WEIGHTS], *[delta[n] for n in WEIGHTS],
            *[new_m[n] for n in WEIGHTS], *[new_v[n] for n in WEIGHTS])
```

```python
import functools

import jax
import jax.numpy as jnp
from jax import lax
from jax.experimental import pallas as pl
from jax.experimental.pallas import tpu as pltpu

F32 = jnp.float32
BF16 = jnp.bfloat16
I32 = jnp.int32

NORM_EPS = 1e-6
SSM_NORM_EPS = 1e-5
SSM_HEAD_DIM = 64
SSM_STATE = 128
SSM_CHUNK = 128
SSM_HEADS_PER_GROUP = 8
SSM_GROUP_W = SSM_HEADS_PER_GROUP * SSM_HEAD_DIM
SSM_CONV = 4
SSM_ROWS = 16
SB_HEAD_DIM = 64
SB_BLOCK = 128
SB_SCALE = SB_HEAD_DIM ** -0.5
SB_Q_BLOCKS_FWD = 4
SB_Q_BLOCKS_BWD = 4
FFN_CONV = 3
LANES = 128
N_CHIPS = 4
N_DEV = 8

ADAM_LR = 0.001
ADAM_B1 = 0.9
ADAM_B2 = 0.999
ADAM_EPS = 1e-08
ADAM_WD = 0.01
ADAM_STEP = 10

MESH = pl.DeviceIdType.MESH
ANY = pl.BlockSpec(memory_space=pl.ANY)

SHARD_AXIS = {
    "ssm_in_proj": 2, "ssm_conv_w": 2, "ssm_conv_b": 1, "ssm_norm": 1, "ssm_out_proj": 1,
    "w_kv": 1, "w_q": 1, "w_o": 1, "ffn_up": 2, "ffn_conv_w": 2, "ffn_down": 1,
    "ple_gate": 1, "ple_proj": 2,
}
REPLICATED = ["attn_norm", "ffn_norm", "ple_norm", "ssm_dt_bias", "ssm_a_log", "ssm_d",
              "kv_norm", "ffn_conv_b", "final_norm"]
WEIGHTS = ["attn_norm", "ffn_norm", "ple_norm", "ssm_in_proj", "ssm_conv_w", "ssm_conv_b",
           "ssm_dt_bias", "ssm_a_log", "ssm_d", "ssm_norm", "ssm_out_proj", "kv_norm", "w_kv",
           "w_q", "w_o", "ffn_up", "ffn_conv_w", "ffn_conv_b", "ffn_down", "ple_gate",
           "ple_proj", "final_norm"]
SHARDED = [n for n in WEIGHTS if n in SHARD_AXIS]
PACK_ROWS = 2048
SLAB_ROW_ALIGN = 16
BIG_WEIGHT = 1 << 17


def _tile(n, pref):
    t = (min(pref, n) // 128) * 128
    while t >= 128:
        if n % t == 0:
            return t
        t -= 128
    return n


def _dot(a, b):
    return jnp.dot(a, b, preferred_element_type=F32)


def _dot_nt(a, b):
    return lax.dot_general(a, b, (((1,), (1,)), ((), ())), preferred_element_type=F32)


def _dot_tn(a, b):
    return lax.dot_general(a, b, (((0,), (0,)), ((), ())), preferred_element_type=F32)


def _split2(x):
    hi = x.astype(BF16)
    lo = (x - hi.astype(F32)).astype(BF16)
    return hi, lo


def _dot2(x, m):
    hi, lo = _split2(x)
    return _dot(hi, m) + _dot(lo, m)


def _dot2_left(m, x):
    hi, lo = _split2(x)
    return _dot(m, hi) + _dot(m, lo)


def _softplus(x):
    return jnp.maximum(x, 0.0) + jnp.log(1.0 + jnp.exp(-jnp.abs(x)))


def _sigmoid(x):
    return 0.5 * jnp.tanh(0.5 * x) + 0.5


def _params(*sem):
    return pltpu.CompilerParams(dimension_semantics=sem)


def _round_robin(gens):
    live = list(gens)
    while live:
        still = []
        for gen in live:
            try:
                next(gen)
                still.append(gen)
            except StopIteration:
                pass
        live = still


MM_VMEM_BUDGET = 36 * 1024 * 1024
MM_FULL_K = 2816


def _mm_tiles(m, n, k, sa, sb, so, has_add, extra=0):
    tk = k if k <= MM_FULL_K else _tile(k, 1024)
    tn = _tile(n, 1408)
    tm = _tile(m, 1408)

    def need(tm_):
        return (2 * tm_ * tk * sa + 2 * tk * tn * sb + tm_ * tn * 4 + 2 * tm_ * tn * so
                + 2 * tm_ * tn * 4 * (extra + (1 if has_add else 0)))

    while need(tm) > MM_VMEM_BUDGET and tm % 256 == 0:
        tm //= 2
    return tm, tn, tk


def _mm(a, b, *, name, ta=False, tb=False, add=None, out_dtype=F32):
    m = a.shape[1] if ta else a.shape[0]
    k = a.shape[0] if ta else a.shape[1]
    n = b.shape[0] if tb else b.shape[1]
    assert (b.shape[1] if tb else b.shape[0]) == k, (a.shape, b.shape, ta, tb)
    tm, tn, tk = _mm_tiles(m, n, k, a.dtype.itemsize, b.dtype.itemsize,
                           jnp.dtype(out_dtype).itemsize, add is not None)
    nk = k // tk
    dims = (((0 if ta else 1,), (1 if tb else 0,)), ((), ()))
    has_add = add is not None

    def body(*refs):
        if has_add:
            a_ref, b_ref, add_ref, o_ref, acc_ref = refs
        else:
            a_ref, b_ref, o_ref, acc_ref = refs
        kk = pl.program_id(2)

        @pl.when(kk == 0)
        def _():
            acc_ref[...] = jnp.zeros_like(acc_ref)

        acc_ref[...] += lax.dot_general(a_ref[...].astype(BF16), b_ref[...].astype(BF16), dims,
                                        preferred_element_type=F32)

        @pl.when(kk == nk - 1)
        def _():
            r = acc_ref[...]
            if has_add:
                r = r + add_ref[...].astype(F32)
            o_ref[...] = r.astype(out_dtype)

    a_spec = (pl.BlockSpec((tk, tm), lambda i, j, kk: (kk, i)) if ta
              else pl.BlockSpec((tm, tk), lambda i, j, kk: (i, kk)))
    b_spec = (pl.BlockSpec((tn, tk), lambda i, j, kk: (j, kk)) if tb
              else pl.BlockSpec((tk, tn), lambda i, j, kk: (kk, j)))
    o_spec = pl.BlockSpec((tm, tn), lambda i, j, kk: (i, j))
    in_specs = [a_spec, b_spec] + ([o_spec] if has_add else [])
    args = (a, b) + ((add,) if has_add else ())
    return pl.pallas_call(
        body, name=name, grid=(m // tm, n // tn, nk), in_specs=in_specs, out_specs=o_spec,
        out_shape=jax.ShapeDtypeStruct((m, n), out_dtype),
        scratch_shapes=[pltpu.VMEM((tm, tn), F32)],
        compiler_params=_params("parallel", "parallel", "arbitrary"),
    )(*args)


def _mm_norm_bwd(a, b, x, gain, dres, *, name, add=None):
    m, k = a.shape
    n = b.shape[0]
    has_add = add is not None
    tm, tn, tk = _mm_tiles(m, n, k, a.dtype.itemsize, b.dtype.itemsize, 4, has_add, extra=2)
    assert tn == n, (tn, n)
    nk = k // tk

    def body(*refs):
        if has_add:
            a_ref, b_ref, add_ref, x_ref, g_ref, dres_ref, dx_ref, dxb_ref, dg_ref, acc_ref = refs
        else:
            a_ref, b_ref, x_ref, g_ref, dres_ref, dx_ref, dxb_ref, dg_ref, acc_ref = refs
        i, kk = pl.program_id(0), pl.program_id(1)

        @pl.when(kk == 0)
        def _():
            acc_ref[...] = jnp.zeros_like(acc_ref)

        acc_ref[...] += _dot_nt(a_ref[...].astype(BF16), b_ref[...].astype(BF16))

        @pl.when(kk == nk - 1)
        def _():
            dyv = acc_ref[...]
            if has_add:
                dyv = dyv + add_ref[...]
            xv = x_ref[...]
            r = lax.rsqrt(jnp.mean(xv * xv, axis=-1, keepdims=True) + NORM_EPS)
            xh = xv * r
            dxh = dyv * g_ref[...]
            dxv = dres_ref[...] + r * (dxh - xh * jnp.mean(dxh * xh, axis=-1, keepdims=True))
            dx_ref[...] = dxv
            dxb_ref[...] = dxv.astype(BF16)
            part = jnp.sum(dyv * xh, axis=0, keepdims=True)

            @pl.when(i == 0)
            def _():
                dg_ref[...] = part

            @pl.when(i > 0)
            def _():
                dg_ref[...] += part

    row = pl.BlockSpec((tm, n), lambda i, kk: (i, 0))
    vec = pl.BlockSpec((1, n), lambda i, kk: (0, 0))
    in_specs = ([pl.BlockSpec((tm, tk), lambda i, kk: (i, kk)), pl.BlockSpec((n, tk), lambda i, kk: (0, kk))]
                + ([row] if has_add else []) + [row, vec, row])
    args = (a, b) + ((add,) if has_add else ()) + (x, gain.reshape(1, n), dres)
    dx, dxb, dg = pl.pallas_call(
        body, name=name, grid=(m // tm, nk), in_specs=in_specs, out_specs=[row, row, vec],
        out_shape=[jax.ShapeDtypeStruct((m, n), F32), jax.ShapeDtypeStruct((m, n), BF16),
                   jax.ShapeDtypeStruct((1, n), F32)],
        scratch_shapes=[pltpu.VMEM((tm, n), F32)],
        compiler_params=_params("arbitrary", "arbitrary"),
    )(*args)
    return (dx, dxb), dg.reshape(n)


def _rmsnorm_fwd(x, gain, *, name, rows=512):
    t, d = x.shape
    tr = _tile(t, rows)

    def body(x_ref, g_ref, o_ref):
        xv = x_ref[...]
        r = lax.rsqrt(jnp.mean(xv * xv, axis=-1, keepdims=True) + NORM_EPS)
        o_ref[...] = ((xv * r) * g_ref[...]).astype(BF16)

    return pl.pallas_call(
        body, name=name, grid=(t // tr,),
        in_specs=[pl.BlockSpec((tr, d), lambda i: (i, 0)), pl.BlockSpec((1, d), lambda i: (0, 0))],
        out_specs=pl.BlockSpec((tr, d), lambda i: (i, 0)),
        out_shape=jax.ShapeDtypeStruct((t, d), BF16),
        compiler_params=_params("parallel"),
    )(x, gain.reshape(1, d))


def _rmsnorm_bwd(x, gain, dy, dres, *, name, rows=512):
    t, d = x.shape
    tr = _tile(t, rows)

    def body(x_ref, g_ref, dy_ref, dres_ref, dx_ref, dg_ref):
        xv = x_ref[...]
        r = lax.rsqrt(jnp.mean(xv * xv, axis=-1, keepdims=True) + NORM_EPS)
        xh = xv * r
        dyv = dy_ref[...].astype(F32)
        dxh = dyv * g_ref[...]
        dx = r * (dxh - xh * jnp.mean(dxh * xh, axis=-1, keepdims=True))
        dx_ref[...] = dres_ref[...] + dx
        part = jnp.sum(dyv * xh, axis=0, keepdims=True)

        @pl.when(pl.program_id(0) == 0)
        def _():
            dg_ref[...] = part

        @pl.when(pl.program_id(0) > 0)
        def _():
            dg_ref[...] += part

    row = pl.BlockSpec((tr, d), lambda i: (i, 0))
    vec = pl.BlockSpec((1, d), lambda i: (0, 0))
    dx, dg = pl.pallas_call(
        body, name=name, grid=(t // tr,), in_specs=[row, vec, row, row], out_specs=[row, vec],
        out_shape=[jax.ShapeDtypeStruct((t, d), F32), jax.ShapeDtypeStruct((1, d), F32)],
        compiler_params=_params("arbitrary"),
    )(x, gain.reshape(1, d), dy, dres)
    return dx, dg.reshape(d)


def _final_loss(h, gain, target, *, rows=512):
    t, d = h.shape
    tr = _tile(t, rows)

    def body(x_ref, g_ref, tg_ref, dx_ref, dg_ref, loss_ref):
        xv = x_ref[...]
        g = g_ref[...]
        r = lax.rsqrt(jnp.mean(xv * xv, axis=-1, keepdims=True) + NORM_EPS)
        xh = xv * r
        err = xh * g - tg_ref[...]
        dyv = err * (1.0 / d)
        dxh = dyv * g
        dx_ref[...] = r * (dxh - xh * jnp.mean(dxh * xh, axis=-1, keepdims=True))
        part = jnp.sum(dyv * xh, axis=0, keepdims=True)
        lpart = jnp.zeros((1, LANES), F32) + (0.5 / d) * jnp.sum(err * err)

        @pl.when(pl.program_id(0) == 0)
        def _():
            dg_ref[...] = part
            loss_ref[...] = lpart

        @pl.when(pl.program_id(0) > 0)
        def _():
            dg_ref[...] += part
            loss_ref[...] += lpart

    row = pl.BlockSpec((tr, d), lambda i: (i, 0))
    vec = pl.BlockSpec((1, d), lambda i: (0, 0))
    dx, dg, loss = pl.pallas_call(
        body, name="final_loss", grid=(t // tr,), in_specs=[row, vec, row],
        out_specs=[row, vec, pl.BlockSpec((1, LANES), lambda i: (0, 0))],
        out_shape=[jax.ShapeDtypeStruct((t, d), F32), jax.ShapeDtypeStruct((1, d), F32),
                   jax.ShapeDtypeStruct((1, LANES), F32)],
        compiler_params=_params("arbitrary"),
    )(h, gain.reshape(1, d), target)
    return loss[0, 0], dx, dg.reshape(d)


def _ple_fwd(h, a, pp, *, name, rows=512):
    t, d = h.shape
    tr = _tile(t, rows)

    def body(h_ref, a_ref, p_ref, o_ref):
        o_ref[...] = h_ref[...] + _sigmoid(a_ref[...]) * p_ref[...]

    row = pl.BlockSpec((tr, d), lambda i: (i, 0))
    return pl.pallas_call(
        body, name=name, grid=(t // tr,), in_specs=[row, row, row], out_specs=row,
        out_shape=jax.ShapeDtypeStruct((t, d), F32), compiler_params=_params("parallel"),
    )(h, a, pp)


def _ple_bwd(dh, a, pp, *, name, rows=512):
    t, d = dh.shape
    tr = _tile(t, rows)

    def body(dh_ref, a_ref, p_ref, da_ref, dp_ref):
        s = _sigmoid(a_ref[...])
        dhv = dh_ref[...]
        da_ref[...] = (dhv * p_ref[...] * (s * (1.0 - s))).astype(BF16)
        dp_ref[...] = (dhv * s).astype(BF16)

    row = pl.BlockSpec((tr, d), lambda i: (i, 0))
    return pl.pallas_call(
        body, name=name, grid=(t // tr,), in_specs=[row, row, row], out_specs=[row, row],
        out_shape=[jax.ShapeDtypeStruct((t, d), BF16)] * 2, compiler_params=_params("parallel"),
    )(dh, a, pp)


CONV_ROWS = 256
CONV_HALO = 8


def _conv_window(ref, r0, with_prev, with_next):
    s = ref.shape[1]
    parts = []
    if with_prev:
        prev = ref[0, pl.ds(pl.multiple_of(jnp.maximum(r0 - CONV_HALO, 0), CONV_HALO), CONV_HALO), :]
        parts.append(jnp.where(r0 > 0, prev, 0.0))
    parts.append(ref[0, pl.ds(r0, CONV_ROWS), :])
    if with_next:
        nxt = pl.multiple_of(jnp.minimum(r0 + CONV_ROWS, s - CONV_HALO), CONV_HALO)
        parts.append(ref[0, pl.ds(nxt, CONV_HALO), :])
    return jnp.concatenate(parts, axis=0)


def _conv_taps(win, kw, n):
    return [win[CONV_HALO - (kw - 1 - k):CONV_HALO - (kw - 1 - k) + n] for k in range(kw)]


def _conv_apply(taps, wv, bv):
    pre = bv + wv[0:1, :] * taps[0]
    for k in range(1, len(taps)):
        pre = pre + wv[k:k + 1, :] * taps[k]
    return pre


def _rows8(x):
    acc = x[0:8]
    for i in range(1, x.shape[0] // 8):
        acc = acc + x[8 * i:8 * i + 8]
    return acc


def _conv_grad_step(dpre_ext, taps, wv, is_last):
    kw = wv.shape[0]
    halo = jnp.where(is_last, 0.0, dpre_ext[CONV_ROWS:])
    dpre_ext = jnp.concatenate([dpre_ext[:CONV_ROWS], halo], axis=0)
    dpre = dpre_ext[:CONV_ROWS]
    du = wv[kw - 1:kw, :] * dpre
    for k in range(kw - 1):
        du = du + wv[k:k + 1, :] * dpre_ext[kw - 1 - k:kw - 1 - k + CONV_ROWS]
    sums = [_rows8(dpre * taps[k][:CONV_ROWS]) for k in range(kw)] + [_rows8(dpre)]
    return du, sums


def _conv_store_sums(sums, dw_ref, db_ref, first):
    kw = len(sums) - 1
    vals = [jnp.sum(s_, axis=0, keepdims=True) for s_ in sums]

    @pl.when(first)
    def _():
        for k in range(kw):
            dw_ref[k:k + 1, :] = vals[k]
        db_ref[...] = vals[kw]

    @pl.when(jnp.logical_not(first))
    def _():
        for k in range(kw):
            dw_ref[k:k + 1, :] += vals[k]
        db_ref[...] += vals[kw]


def _dsilu(pre):
    s = _sigmoid(pre)
    return s, s * (1.0 + pre * (1.0 - s))


def _conv_silu_fwd(zx, off, w, b, *, name, tc=128):
    bsz, s, _ = zx.shape
    kw, c = w.shape
    o0 = off // tc

    def body(u_ref, w_ref, b_ref, o_ref):
        wv, bv = w_ref[...], b_ref[...]

        def step(i, carry):
            r0 = pl.multiple_of(i * CONV_ROWS, CONV_ROWS)
            taps = _conv_taps(_conv_window(u_ref, r0, True, False), kw, CONV_ROWS)
            pre = _conv_apply(taps, wv, bv)
            o_ref[0, pl.ds(r0, CONV_ROWS), :] = pre * _sigmoid(pre)
            return carry

        lax.fori_loop(0, s // CONV_ROWS, step, 0)

    return pl.pallas_call(
        body, name=name, grid=(bsz, c // tc),
        in_specs=[pl.BlockSpec((1, s, tc), lambda i, j: (i, 0, o0 + j)),
                  pl.BlockSpec((kw, tc), lambda i, j: (0, j)),
                  pl.BlockSpec((1, tc), lambda i, j: (0, j))],
        out_specs=pl.BlockSpec((1, s, tc), lambda i, j: (i, 0, j)),
        out_shape=jax.ShapeDtypeStruct((bsz, s, c), F32),
        compiler_params=_params("parallel", "parallel"),
    )(zx, w, b.reshape(1, c))


def _conv_silu_bwd(zx, off, w, b, douts, *, name, tc=128):
    bsz, s, _ = zx.shape
    kw, c = w.shape
    o0 = off // tc
    counts = [d.shape[2] // tc for d in douts]
    starts = [sum(counts[:k]) for k in range(len(douts))]
    assert sum(counts) == c // tc

    def body(u_ref, w_ref, b_ref, *rest):
        dy_refs = rest[:len(douts)]
        du_ref, dw_ref, db_ref = rest[len(douts):]
        j = pl.program_id(0)
        wv, bv = w_ref[...], b_ref[...]
        n = CONV_ROWS + CONV_HALO

        def step(i, sums):
            r0 = pl.multiple_of(i * CONV_ROWS, CONV_ROWS)
            taps = _conv_taps(_conv_window(u_ref, r0, True, True), kw, n)
            _, ds = _dsilu(_conv_apply(taps, wv, bv))
            dy = _conv_window(dy_refs[0], r0, False, True)
            for k in range(1, len(douts)):
                dy = jnp.where(j >= starts[k], _conv_window(dy_refs[k], r0, False, True), dy)
            du, new = _conv_grad_step(dy * ds, taps, wv, r0 + CONV_ROWS >= s)
            du_ref[0, pl.ds(r0, CONV_ROWS), :] = du.astype(BF16)
            return tuple(a + b_ for a, b_ in zip(sums, new))

        zero = tuple(jnp.zeros((8, tc), F32) for _ in range(kw + 1))
        sums = lax.fori_loop(0, s // CONV_ROWS, step, zero)
        _conv_store_sums(sums, dw_ref, db_ref, pl.program_id(1) == 0)

    def part_spec(k):
        return pl.BlockSpec((1, s, tc), lambda j, i: (i, 0, jnp.clip(j - starts[k], 0, counts[k] - 1)))

    du, dw, db = pl.pallas_call(
        body, name=name, grid=(c // tc, bsz),
        in_specs=[pl.BlockSpec((1, s, tc), lambda j, i: (i, 0, o0 + j)),
                  pl.BlockSpec((kw, tc), lambda j, i: (0, j)),
                  pl.BlockSpec((1, tc), lambda j, i: (0, j))] + [part_spec(k) for k in range(len(douts))],
        out_specs=[pl.BlockSpec((1, s, tc), lambda j, i: (i, 0, j)),
                   pl.BlockSpec((kw, tc), lambda j, i: (0, j)),
                   pl.BlockSpec((1, tc), lambda j, i: (0, j))],
        out_shape=[jax.ShapeDtypeStruct((bsz, s, c), BF16), jax.ShapeDtypeStruct((kw, c), F32),
                   jax.ShapeDtypeStruct((1, c), F32)],
        compiler_params=_params("parallel", "arbitrary"),
    )(zx, w, b.reshape(1, c), *douts)
    return du, dw, db.reshape(c)


def _conv_glu_fwd(up, w, b, *, name, tc=128):
    bsz, s, c2 = up.shape
    kw = w.shape[0]
    f = c2 // 2
    nt = f // tc

    def body(ug_ref, uv_ref, wg_ref, wv_ref, bg_ref, bv_ref, o_ref):
        wg, wv, bg, bv = wg_ref[...], wv_ref[...], bg_ref[...], bv_ref[...]

        def step(i, carry):
            r0 = pl.multiple_of(i * CONV_ROWS, CONV_ROWS)
            pg = _conv_apply(_conv_taps(_conv_window(ug_ref, r0, True, False), kw, CONV_ROWS), wg, bg)
            pv = _conv_apply(_conv_taps(_conv_window(uv_ref, r0, True, False), kw, CONV_ROWS), wv, bv)
            o_ref[0, pl.ds(r0, CONV_ROWS), :] = (pg * _sigmoid(pg) * pv).astype(BF16)
            return carry

        lax.fori_loop(0, s // CONV_ROWS, step, 0)

    b2 = b.reshape(1, c2)
    return pl.pallas_call(
        body, name=name, grid=(bsz, nt),
        in_specs=[pl.BlockSpec((1, s, tc), lambda i, j: (i, 0, j)),
                  pl.BlockSpec((1, s, tc), lambda i, j: (i, 0, nt + j)),
                  pl.BlockSpec((kw, tc), lambda i, j: (0, j)),
                  pl.BlockSpec((kw, tc), lambda i, j: (0, nt + j)),
                  pl.BlockSpec((1, tc), lambda i, j: (0, j)),
                  pl.BlockSpec((1, tc), lambda i, j: (0, nt + j))],
        out_specs=pl.BlockSpec((1, s, tc), lambda i, j: (i, 0, j)),
        out_shape=jax.ShapeDtypeStruct((bsz, s, f), BF16),
        compiler_params=_params("parallel", "parallel"),
    )(up, up, w, w, b2, b2)


def _conv_glu_bwd(up, w, b, df, *, name, tc=128):
    bsz, s, c2 = up.shape
    kw = w.shape[0]
    f = c2 // 2
    nt = f // tc

    def body(ug_ref, uv_ref, wg_ref, wv_ref, bg_ref, bv_ref, df_ref,
             dug_ref, duv_ref, dwg_ref, dwv_ref, dbg_ref, dbv_ref):
        first = pl.program_id(1) == 0
        wg, wv, bg, bv = wg_ref[...], wv_ref[...], bg_ref[...], bv_ref[...]
        n = CONV_ROWS + CONV_HALO

        def step(i, sums):
            r0 = pl.multiple_of(i * CONV_ROWS, CONV_ROWS)
            is_last = r0 + CONV_ROWS >= s
            tg = _conv_taps(_conv_window(ug_ref, r0, True, True), kw, n)
            tv = _conv_taps(_conv_window(uv_ref, r0, True, True), kw, n)
            pg = _conv_apply(tg, wg, bg)
            pv = _conv_apply(tv, wv, bv)
            sig, dsl = _dsilu(pg)
            dfv = _conv_window(df_ref, r0, False, True)
            dug, new_g = _conv_grad_step(dfv * pv * dsl, tg, wg, is_last)
            duv, new_v = _conv_grad_step(dfv * (pg * sig), tv, wv, is_last)
            dug_ref[0, pl.ds(r0, CONV_ROWS), :] = dug.astype(BF16)
            duv_ref[0, pl.ds(r0, CONV_ROWS), :] = duv.astype(BF16)
            return tuple(a + b_ for a, b_ in zip(sums, new_g + new_v))

        zero = tuple(jnp.zeros((8, tc), F32) for _ in range(2 * (kw + 1)))
        sums = lax.fori_loop(0, s // CONV_ROWS, step, zero)
        _conv_store_sums(sums[:kw + 1], dwg_ref, dbg_ref, first)
        _conv_store_sums(sums[kw + 1:], dwv_ref, dbv_ref, first)

    b2 = b.reshape(1, c2)
    act = lambda j, i: (i, 0, j)
    wsp = pl.BlockSpec((kw, tc), lambda j, i: (0, j))
    bsp = pl.BlockSpec((1, tc), lambda j, i: (0, j))
    dug, duv, dwg, dwv, dbg, dbv = pl.pallas_call(
        body, name=name, grid=(nt, bsz),
        in_specs=[pl.BlockSpec((1, s, tc), act),
                  pl.BlockSpec((1, s, tc), lambda j, i: (i, 0, nt + j)),
                  wsp, pl.BlockSpec((kw, tc), lambda j, i: (0, nt + j)),
                  bsp, pl.BlockSpec((1, tc), lambda j, i: (0, nt + j)),
                  pl.BlockSpec((1, s, tc), act)],
        out_specs=[pl.BlockSpec((1, s, tc), act), pl.BlockSpec((1, s, tc), act), wsp, wsp, bsp, bsp],
        out_shape=[jax.ShapeDtypeStruct((bsz, s, f), BF16)] * 2
        + [jax.ShapeDtypeStruct((kw, f), F32)] * 2 + [jax.ShapeDtypeStruct((1, f), F32)] * 2,
        compiler_params=_params("parallel", "arbitrary"),
    )(up, up, w, w, b2, b2, df)
    return (dug, duv, jnp.concatenate([dwg, dwv], axis=1),
            jnp.concatenate([dbg.reshape(f), dbv.reshape(f)]))


def _ssd_shared(xs, bm, cm, dtc_raw, dtr_raw, plane, psub, st):
    cl = SSM_CHUNK
    bias_l, a_l = plane[0:1, :], plane[1:2, :]
    bias_s, a_s = psub[:, 0:1], psub[:, 1:2]
    ri = lax.broadcasted_iota(I32, (cl, cl), 0)
    ci = lax.broadcasted_iota(I32, (cl, cl), 1)
    tril = ri >= ci
    low_incl = tril.astype(BF16)
    up_incl = (ri <= ci).astype(BF16)
    seg_t = (lax.broadcasted_iota(I32, (LANES, SSM_GROUP_W), 0)
             == lax.broadcasted_iota(I32, (LANES, SSM_GROUP_W), 1) // SSM_HEAD_DIM).astype(BF16)
    dt_c = _softplus(dtc_raw + bias_l)
    cs_c = _dot2_left(low_incl, dt_c * a_l)
    dt_r = _softplus(dtr_raw + bias_s)
    cs_r = _dot2(dt_r * a_s, up_incl)
    yield
    dt_ch = _dot2(dt_c, seg_t)
    cs_ch = _dot2(cs_c, seg_t)
    yield
    cs_last = cs_ch[cl - 1:cl, :]
    decay_ch = jnp.exp(cs_ch)
    w_ch = jnp.exp(cs_last - cs_ch)
    tot_ch = jnp.exp(cs_last)
    xdt = xs * dt_ch
    bm_b, cm_b = bm.astype(BF16), cm.astype(BF16)
    gmat = _dot_nt(cm_b, bm_b)
    cst = _dot(cm_b, st.astype(BF16))
    yield
    yoff = decay_ch * cst
    return dict(tril=tril, low_incl=low_incl, up_incl=up_incl, seg_t=seg_t, a_l=a_l, bias_l=bias_l,
                dt_c=dt_c, cs_c=cs_c, cs_r=cs_r, dt_ch=dt_ch, decay_ch=decay_ch, w_ch=w_ch,
                tot_ch=tot_ch, xdt=xdt, bm_b=bm_b, cm_b=cm_b, gmat=gmat, yoff=yoff)


def _head_decay(q, r):
    diff = q["cs_c"][:, r:r + 1] - q["cs_r"][r:r + 1, :]
    return jnp.where(q["tril"], jnp.exp(jnp.minimum(diff, 0.0)), 0.0)


def _half_mask(hh):
    lane = lax.broadcasted_iota(I32, (SSM_CHUNK, LANES), 1)
    return (lane < SSM_HEAD_DIM) if hh == 0 else (lane >= SSM_HEAD_DIM)


def _ssd_ydiag(q):
    pairs = []
    for pr in range(SSM_HEADS_PER_GROUP // 2):
        xp = q["xdt"][:, pr * LANES:(pr + 1) * LANES]
        acc = None
        for hh in range(2):
            mm_ = (q["gmat"] * _head_decay(q, 2 * pr + hh)).astype(BF16)
            part = _dot(mm_, jnp.where(_half_mask(hh), xp, 0.0).astype(BF16))
            acc = part if acc is None else acc + part
        pairs.append(acc)
        yield
    return jnp.concatenate(pairs, axis=1)


def _ssd_specs(bsz, s, g_n, d_inner, rev):
    cl = SSM_CHUNK
    nc = s // cl
    cc = (lambda c: nc - 1 - c) if rev else (lambda c: c)
    gb = d_inner // LANES
    dt0 = (d_inner + d_inner + 2 * g_n * SSM_STATE) // LANES
    gpb = next(n_ for n_ in (4, 2, 1) if all(v % n_ == 0 for v in (g_n, gb, dt0)))
    gw = SSM_GROUP_W
    specs = dict(
        z=pl.BlockSpec((1, cl, gw * gpb), lambda b, g, c: (b, cc(c), g)),
        dtc=pl.BlockSpec((1, cl, LANES * gpb), lambda b, g, c: (b, cc(c), dt0 // gpb + g)),
        xs=pl.BlockSpec((1, cl, gw * gpb), lambda b, g, c: (b, cc(c), g)),
        bm=pl.BlockSpec((1, cl, LANES * gpb), lambda b, g, c: (b, cc(c), gb // gpb + g)),
        cm=pl.BlockSpec((1, cl, LANES * gpb), lambda b, g, c: (b, cc(c), (gb + g_n) // gpb + g)),
        dtr=pl.BlockSpec((1, gpb, SSM_ROWS, cl), lambda b, g, c: (b, g, 0, cc(c))),
        plane=pl.BlockSpec((gpb, 8, LANES), lambda b, g, c: (g, 0, 0)),
        psub=pl.BlockSpec((gpb, SSM_ROWS, LANES), lambda b, g, c: (g, 0, 0)),
        chan=pl.BlockSpec((1, gw * gpb), lambda b, g, c: (0, g)),
        state=pl.BlockSpec((1, gpb, 1, SSM_STATE, gw), lambda b, g, c: (b, g, cc(c), 0, 0)),
        bgrp=pl.BlockSpec((1, cl, LANES * gpb), lambda b, g, c: (b, cc(c), g)),
        acc_ch=pl.BlockSpec((1, gpb, 8, gw), lambda b, g, c: (b, g, 0, 0)),
        acc_ln=pl.BlockSpec((1, gpb, 8, LANES), lambda b, g, c: (b, g, 0, 0)),
    )
    lanes = lambda w: (lambda ref, gg: ref.at[:, :, pl.ds(gg * w, w)])
    second = lambda ref, gg: ref.at[:, pl.ds(gg, 1)]
    first = lambda ref, gg: ref.at[pl.ds(gg, 1)]
    views = dict(z=lanes(gw), xs=lanes(gw), dtc=lanes(LANES), bm=lanes(LANES), cm=lanes(LANES),
                 bgrp=lanes(LANES), dtr=second, state=second, acc_ch=second, acc_ln=second,
                 plane=first, psub=first, chan=lambda ref, gg: ref.at[:, pl.ds(gg * gw, gw)],
                 scratch=lambda ref, gg: ref.at[gg])
    return specs, views, gpb


def _per_group(body, names, views, gpb):
    def run(*refs):
        _round_robin([body(*[views[nm](ref, gg) for nm, ref in zip(names, refs)]) for gg in range(gpb)])
    return run


def _ssd_fwd(zx, xbc, dtr_row, plane, psub, d_ch, nw, *, name):
    bsz, s, _ = zx.shape
    d_inner = d_ch.shape[1]
    g_n = d_inner // SSM_GROUP_W
    nc = s // SSM_CHUNK
    sp, views, gpb = _ssd_specs(bsz, s, g_n, d_inner, False)
    names = ["z", "dtc", "xs", "bm", "cm", "dtr", "plane", "psub", "chan", "chan", "z", "state", "scratch"]

    def body(z_ref, dtc_ref, xs_ref, bm_ref, cm_ref, dtr_ref, plane_ref, psub_ref, d_ref, nw_ref,
             gn_ref, st_out_ref, st_ref):
        @pl.when(pl.program_id(2) == 0)
        def _():
            st_ref[...] = jnp.zeros_like(st_ref)

        xs = xs_ref[0]
        st = st_ref[...]
        st_out_ref[0, 0, 0] = st
        q = yield from _ssd_shared(xs, bm_ref[0], cm_ref[0], dtc_ref[0], dtr_ref[0, 0], plane_ref[0],
                                   psub_ref[0], st)
        y = (yield from _ssd_ydiag(q)) + q["yoff"] + xs * d_ref[...]
        st_ref[...] = q["tot_ch"] * st + _dot_tn(q["bm_b"], (q["w_ch"] * q["xdt"]).astype(BF16))
        zv = z_ref[0]
        gy = y * (zv * _sigmoid(zv))
        rstd = lax.rsqrt(jnp.mean(gy * gy, axis=-1, keepdims=True) + SSM_NORM_EPS)
        gn_ref[0] = ((gy * rstd) * nw_ref[...]).astype(BF16)

    return pl.pallas_call(
        _per_group(body, names, views, gpb), name=name, grid=(bsz, g_n // gpb, nc),
        in_specs=[sp["z"], sp["dtc"], sp["xs"], sp["bm"], sp["cm"], sp["dtr"], sp["plane"],
                  sp["psub"], sp["chan"], sp["chan"]],
        out_specs=[sp["z"], sp["state"]],
        out_shape=[jax.ShapeDtypeStruct((bsz, s, d_inner), BF16),
                   jax.ShapeDtypeStruct((bsz, g_n, nc, SSM_STATE, SSM_GROUP_W), F32)],
        scratch_shapes=[pltpu.VMEM((gpb, SSM_STATE, SSM_GROUP_W), F32)],
        compiler_params=_params("parallel", "parallel", "arbitrary"),
    )(zx, zx, xbc, xbc, xbc, dtr_row, plane, psub, d_ch, nw)


def _ssd_bwd(zx, xbc, dtr_row, plane, psub, d_ch, nw, states, dgn, *, name):
    bsz, s, _ = zx.shape
    d_inner = d_ch.shape[1]
    g_n = d_inner // SSM_GROUP_W
    cl = SSM_CHUNK
    nc = s // cl
    sp, views, gpb = _ssd_specs(bsz, s, g_n, d_inner, True)
    acc_ch, acc_ln = sp["acc_ch"], sp["acc_ln"]
    names = ["z", "dtc", "xs", "bm", "cm", "dtr", "plane", "psub", "chan", "chan", "state", "z",
             "z", "bgrp", "bgrp", "z", "bgrp", "acc_ch", "acc_ln", "scratch"]

    def body(z_ref, dtc_ref, xs_ref, bm_ref, cm_ref, dtr_ref, plane_ref, psub_ref, d_ref, nw_ref,
             st_in_ref, dgn_ref,
             dxs_ref, dbm_ref, dcm_ref, dz_ref, ddt_ref, ach_ref, aln_ref, dst_ref):
        first = pl.program_id(2) == 0

        @pl.when(first)
        def _():
            dst_ref[...] = jnp.zeros_like(dst_ref)
            ach_ref[...] = jnp.zeros_like(ach_ref)
            aln_ref[...] = jnp.zeros_like(aln_ref)

        xs = xs_ref[0]
        st = st_in_ref[0, 0, 0]
        q = yield from _ssd_shared(xs, bm_ref[0], cm_ref[0], dtc_ref[0], dtr_ref[0, 0], plane_ref[0],
                                   psub_ref[0], st)
        d_chv = d_ref[...]
        nwv = nw_ref[...]
        y = (yield from _ssd_ydiag(q)) + q["yoff"] + xs * d_chv
        zv = z_ref[0]
        sz = _sigmoid(zv)
        silu_z = zv * sz
        gy = y * silu_z
        rstd = lax.rsqrt(jnp.mean(gy * gy, axis=-1, keepdims=True) + SSM_NORM_EPS)
        gyh = gy * rstd
        dgnv = dgn_ref[0]
        dgyh = dgnv * nwv
        dgy = rstd * (dgyh - gyh * jnp.mean(dgyh * gyh, axis=-1, keepdims=True))
        dy = dgy * silu_z
        dz_ref[0] = (dgy * y * (sz * (1.0 + zv * (1.0 - sz)))).astype(BF16)
        ach_ref[0, 0, 0:1, :] += jnp.sum(dgnv * gyh, axis=0, keepdims=True)
        ach_ref[0, 0, 1:2, :] += jnp.sum(dy * xs, axis=0, keepdims=True)
        yield
        st_b = st.astype(BF16)
        dyd = (dy * q["decay_ch"]).astype(BF16)
        dcm = _dot_nt(dyd, st_b)
        dstn = dst_ref[...]
        dstn_b = dstn.astype(BF16)
        bds = _dot(q["bm_b"], dstn_b)
        wx = q["w_ch"] * q["xdt"]
        dbm = _dot_nt(wx.astype(BF16), dstn_b)
        dst_ref[...] = q["tot_ch"] * dstn + _dot_tn(q["cm_b"], dyd)
        vterm = wx * bds
        cs_terms = dy * q["yoff"] - vterm
        last_ch = q["tot_ch"] * jnp.sum(dstn * st, axis=0, keepdims=True) + jnp.sum(vterm, axis=0, keepdims=True)
        yield
        lane = lax.broadcasted_iota(I32, (cl, LANES), 1)
        rowi = lax.broadcasted_iota(I32, (SSM_ROWS, cl), 0)
        dg_sum = jnp.zeros((cl, cl), F32)
        dcs_col = jnp.zeros((cl, LANES), F32)
        dcs_row = jnp.zeros((SSM_ROWS, cl), F32)
        dxdt_pairs = []
        for pr in range(SSM_HEADS_PER_GROUP // 2):
            xp_b = q["xdt"][:, pr * LANES:(pr + 1) * LANES].astype(BF16)
            dyp = dy[:, pr * LANES:(pr + 1) * LANES]
            acc = None
            for hh in range(2):
                r = 2 * pr + hh
                dm = _head_decay(q, r)
                mmat = q["gmat"] * dm
                dym = jnp.where(_half_mask(hh), dyp, 0.0).astype(BF16)
                dmat = jnp.where(q["tril"], _dot_nt(dym, xp_b), 0.0)
                part = _dot_tn(mmat.astype(BF16), dym)
                acc = part if acc is None else acc + part
                dg_sum = dg_sum + dmat * dm
                e = dmat * mmat
                dcs_col = dcs_col + jnp.where(lane == r, jnp.sum(e, axis=1, keepdims=True), 0.0)
                dcs_row = dcs_row + jnp.where(rowi == r, jnp.sum(e, axis=0, keepdims=True), 0.0)
            dxdt_pairs.append(acc)
            yield
        dg_b = dg_sum.astype(BF16)
        dcm_ref[0] = dcm + _dot(dg_b, q["bm_b"])
        dbm_ref[0] = dbm + _dot_tn(dg_b, q["cm_b"])
        dxdt = q["w_ch"] * bds + jnp.concatenate(dxdt_pairs, axis=1)
        dxs_ref[0] = dy * d_chv + dxdt * q["dt_ch"]
        yield
        seg = (lax.broadcasted_iota(I32, (SSM_GROUP_W, LANES), 0) // SSM_HEAD_DIM
               == lax.broadcasted_iota(I32, (SSM_GROUP_W, LANES), 1)).astype(BF16)
        row_as_col = jnp.transpose(jnp.concatenate(
            [dcs_row, jnp.zeros((cl - SSM_ROWS, cl), F32)], axis=0))
        dcs = dcs_col - row_as_col + _dot2(cs_terms, seg)
        last = _dot2(jnp.zeros((8, SSM_GROUP_W), F32) + last_ch, seg)[0:1, :]
        da = _dot2_left(q["up_incl"], dcs) + last
        ddt = _dot2(dxdt * xs, seg) + da * q["a_l"]
        ddtr = ddt * _sigmoid(dtc_ref[0] + q["bias_l"])
        ddt_ref[0] = ddtr.astype(BF16)
        aln_ref[0, 0, 0:1, :] += jnp.sum(ddtr, axis=0, keepdims=True)
        aln_ref[0, 0, 1:2, :] += jnp.sum(da * q["dt_c"], axis=0, keepdims=True)

    outs = pl.pallas_call(
        _per_group(body, names, views, gpb), name=name, grid=(bsz, g_n // gpb, nc),
        in_specs=[sp["z"], sp["dtc"], sp["xs"], sp["bm"], sp["cm"], sp["dtr"], sp["plane"],
                  sp["psub"], sp["chan"], sp["chan"], sp["state"], sp["z"]],
        out_specs=[sp["z"], sp["bgrp"], sp["bgrp"], sp["z"], sp["bgrp"], acc_ch, acc_ln],
        out_shape=[jax.ShapeDtypeStruct((bsz, s, d_inner), F32),
                   jax.ShapeDtypeStruct((bsz, s, g_n * SSM_STATE), F32),
                   jax.ShapeDtypeStruct((bsz, s, g_n * SSM_STATE), F32),
                   jax.ShapeDtypeStruct((bsz, s, d_inner), BF16),
                   jax.ShapeDtypeStruct((bsz, s, g_n * LANES), BF16),
                   jax.ShapeDtypeStruct((bsz, g_n, 8, SSM_GROUP_W), F32),
                   jax.ShapeDtypeStruct((bsz, g_n, 8, LANES), F32)],
        scratch_shapes=[pltpu.VMEM((gpb, SSM_STATE, SSM_GROUP_W), F32)],
        compiler_params=_params("parallel", "parallel", "arbitrary"),
    )(zx, zx, xbc, xbc, xbc, dtr_row, plane, psub, d_ch, nw, states, dgn)
    return outs


def _sb_stack(x):
    out = []
    for i in range(x.shape[0] // SB_BLOCK):
        xb = x[i * SB_BLOCK:(i + 1) * SB_BLOCK]
        lane = lax.broadcasted_iota(I32, xb.shape, 1)
        zero = jnp.zeros_like(xb)
        out += [jnp.where(lane < SB_HEAD_DIM, xb, zero), jnp.where(lane >= SB_HEAD_DIM, xb, zero)]
    return jnp.concatenate(out, axis=0)


def _sb_unstack_t(acc_t):
    row = lax.broadcasted_iota(I32, (LANES, SB_BLOCK), 0)
    out = []
    for i in range(acc_t.shape[1] // (2 * SB_BLOCK)):
        a = acc_t[:, 2 * i * SB_BLOCK:(2 * i + 1) * SB_BLOCK]
        b = acc_t[:, (2 * i + 1) * SB_BLOCK:(2 * i + 2) * SB_BLOCK]
        out.append(jnp.transpose(jnp.where(row < SB_HEAD_DIM, a, b)))
    return jnp.concatenate(out, axis=0)


def _sb_tile_blocks(nq, q_blocks):
    nb = 4 if nq % 4 == 0 else (2 if nq % 2 == 0 else 1)
    return nb, min(nb, q_blocks)


def _sb_valid(u, qi0, nb, nqb):
    shape = (nb * SB_BLOCK, nqb * 2 * SB_BLOCK)
    key = u * (nb * SB_BLOCK) + lax.broadcasted_iota(I32, shape, 0)
    col = lax.broadcasted_iota(I32, shape, 1)
    qpos = (qi0 + col // (2 * SB_BLOCK)) * SB_BLOCK + col % SB_BLOCK
    return key < qpos


def _sb_logits(kb, qs, valid):
    z = _dot_nt(kb, qs)
    lb = jnp.minimum(z, 0.0) - jnp.log(1.0 + jnp.exp(-jnp.abs(z)))
    lk_all = lb - z
    lk = lk_all if valid is None else jnp.where(valid, lk_all, 0.0)
    return z, lb, lk_all, lk


def _sb_diag(x):
    w2 = 2 * SB_BLOCK
    ri = lax.broadcasted_iota(I32, (SB_BLOCK, w2), 0)
    ci = lax.broadcasted_iota(I32, (SB_BLOCK, w2), 1) % SB_BLOCK
    first = jnp.where(ri < ci, x[:, :w2], 0.0)
    return first if x.shape[1] == w2 else jnp.concatenate([first, x[:, w2:]], axis=1)


def _sb_add_from(full, part, lo):
    if lo == 0:
        return full + part
    return jnp.concatenate([full[:, :lo], full[:, lo:] + part], axis=1)


def _sb_scan(tri2, x, nb, reverse, exact=True):
    blk = SB_BLOCK
    edge = 0 if reverse else blk - 1
    carry = jnp.zeros((1, x.shape[1]), F32)
    res = [None] * nb
    for i in (reversed(range(nb)) if reverse else range(nb)):
        part = x[i * blk:(i + 1) * blk]
        if exact:
            hi, lo = _split2(part)
            raw = _dot(tri2, jnp.concatenate([hi, lo], axis=0))
        else:
            raw = _dot(tri2[:, :blk], part.astype(BF16))
        res[i] = raw + carry
        carry = carry + (raw[edge:edge + 1] + part[edge:edge + 1])
    return jnp.concatenate(res, axis=0), carry


def _sb_fwd(q, kv, kvt, *, name):
    bsz, s, w = q.shape
    blk = SB_BLOCK
    npair = w // LANES
    nq = s // blk
    nb, nqb = _sb_tile_blocks(nq, SB_Q_BLOCKS_FWD)
    width = nqb * 2 * blk

    def body(q_ref, k_ref, vt_ref, o_ref, tot_ref):
        qi0 = pl.program_id(2) * nqb
        qs = _sb_stack(q_ref[0] * SB_SCALE)
        ri = lax.broadcasted_iota(I32, (blk, blk), 0)
        ci = lax.broadcasted_iota(I32, (blk, blk), 1)
        upper = (ri < ci).astype(BF16)
        tri2 = jnp.concatenate([upper, upper], axis=1)

        def tile_phases(u, carry, masked):
            rows = pl.ds(pl.multiple_of(u * (nb * blk), nb * blk), nb * blk)
            valid = _sb_valid(u, qi0, nb, nqb) if masked else None
            _, lb, _, lk = _sb_logits(k_ref[0, rows, :], qs, valid)
            yield
            sfx, total = _sb_scan(tri2, lk, nb, True)
            yield
            wgt = jnp.exp(lb + sfx + carry["r"])
            if masked:
                wgt = jnp.where(valid, wgt, 0.0)
            carry["r"] = carry["r"] + total
            wb = wgt.astype(BF16)
            yield
            for i in range(nb):
                carry["acc"] = carry["acc"] + _dot(vt_ref[0, 0, u * nb + i], wb[i * blk:(i + 1) * blk])

        def tile(us, r, acc, masked):
            carry = {"r": r, "acc": acc}
            _round_robin([tile_phases(u, carry, masked) for u in us])
            return carry["r"], carry["acc"]

        def top_tile(u):
            carry = {"r": jnp.zeros((1, width), F32), "acc": jnp.zeros((LANES, width), F32)}

            def block(kb):
                lo = kb * 2 * blk
                rows = pl.ds(pl.multiple_of((u * nb + kb) * blk, blk), blk)
                _, lb, lk_all, _ = _sb_logits(k_ref[0, rows, :], qs[lo:], None)
                lk = _sb_diag(lk_all)
                yield
                hi, lo_part = _split2(lk)
                raw = _dot(tri2, jnp.concatenate([hi, lo_part], axis=0))
                yield
                wgt = _sb_diag(jnp.exp(lb + raw + carry["r"][:, lo:]))
                carry["r"] = _sb_add_from(carry["r"], raw[0:1] + lk[0:1], lo)
                yield
                carry["acc"] = _sb_add_from(carry["acc"], _dot(vt_ref[0, 0, u * nb + kb], wgt.astype(BF16)), lo)

            _round_robin([block(kb) for kb in reversed(range(nb))])
            return carry["r"], carry["acc"]

        top = qi0 // nb
        zero_r, zero_acc = jnp.zeros((1, width), F32), jnp.zeros((LANES, width), F32)
        r, acc = top_tile(top) if nb == nqb else tile([top], zero_r, zero_acc, True)
        r, acc = lax.fori_loop(
            0, top // 2, lambda t, c: tile([top - 1 - 2 * t, top - 2 - 2 * t], c[0], c[1], False), (r, acc))
        r, acc = lax.fori_loop(0, top % 2, lambda t, c: tile([0], c[0], c[1], False), (r, acc))
        o_ref[0] = _sb_unstack_t(acc).astype(BF16)
        tot_ref[0, 0, 0] = r

    qspec = pl.BlockSpec((1, nqb * blk, LANES), lambda b, p, i: (b, i, p))
    return pl.pallas_call(
        body, name=name, grid=(bsz, npair, nq // nqb),
        in_specs=[qspec,
                  pl.BlockSpec((1, s, LANES), lambda b, p, i: (b, 0, p)),
                  pl.BlockSpec((1, 1, nq, LANES, blk), lambda b, p, i: (b, npair + p, 0, 0, 0))],
        out_specs=[qspec, pl.BlockSpec((1, 1, 1, 1, width), lambda b, p, i: (b, p, i, 0, 0))],
        out_shape=[jax.ShapeDtypeStruct((bsz, s, w), BF16),
                   jax.ShapeDtypeStruct((bsz, npair, nq // nqb, 1, width), F32)],
        compiler_params=_params("parallel", "parallel", "arbitrary"),
    )(q, kv, kvt)


def _kv_blocks_t(kv3):
    bsz, s, w2 = kv3.shape
    x = kv3.reshape(bsz, s // SB_BLOCK, SB_BLOCK, w2 // LANES, LANES)
    return jnp.transpose(x, (0, 3, 1, 4, 2))


def _sb_bwd(q, kv, kvt, do, tot, dk_in, dv_in, *, name):
    bsz, s, w = q.shape
    blk = SB_BLOCK
    npair = w // LANES
    nq = s // blk
    nb, nqb = _sb_tile_blocks(nq, SB_Q_BLOCKS_BWD)
    width = nqb * 2 * blk
    tot = tot.reshape(bsz, npair, nq // nqb, 1, width)
    has_init = dk_in is not None

    def body(*refs):
        if has_init:
            q_ref, k_ref, v_ref, kt_ref, do_ref, tot_ref, dki_ref, dvi_ref, dq_ref, dk_ref, dv_ref = refs
        else:
            q_ref, k_ref, v_ref, kt_ref, do_ref, tot_ref, dq_ref, dk_ref, dv_ref = refs
        qi0 = pl.program_id(2) * nqb

        @pl.when(qi0 == 0)
        def _():
            if has_init:
                dk_ref[...] = dki_ref[...]
                dv_ref[...] = dvi_ref[...]
            else:
                dk_ref[...] = jnp.zeros_like(dk_ref)
                dv_ref[...] = jnp.zeros_like(dv_ref)

        qs = _sb_stack(q_ref[0] * SB_SCALE)
        dos = _sb_stack(do_ref[0])
        totv = tot_ref[0, 0, 0]
        ri = lax.broadcasted_iota(I32, (blk, blk), 0)
        ci = lax.broadcasted_iota(I32, (blk, blk), 1)
        lower = (ri > ci).astype(BF16)
        tri2 = jnp.concatenate([lower, lower], axis=1)

        def tile_phases(u, carry, masked):
            rows = pl.ds(pl.multiple_of(u * (nb * blk), nb * blk), nb * blk)
            valid = _sb_valid(u, qi0, nb, nqb) if masked else None
            z, lb, lk_all, lk = _sb_logits(k_ref[0, rows, :], qs, valid)
            yield
            before, tot_lk = _sb_scan(tri2, lk, nb, False)
            yield
            wgt = jnp.exp(z + ((totv - carry["pre_lk"]) - before))
            if masked:
                wgt = jnp.where(valid, wgt, 0.0)
            carry["pre_lk"] = carry["pre_lk"] + tot_lk
            dlogit = _dot_nt(v_ref[0, rows, :], dos) * wgt
            yield
            dbefore, tot_d = _sb_scan(tri2, dlogit, nb, False, exact=False)
            yield
            sig = jnp.exp(lb)
            dz = dlogit * (1.0 - sig) - (carry["pre_d"] + dbefore) * sig
            if masked:
                dz = jnp.where(valid, dz, 0.0)
            carry["pre_d"] = carry["pre_d"] + tot_d
            dz_b = dz.astype(BF16)
            yield
            for i in range(nb):
                carry["dqt"] = carry["dqt"] + _dot(kt_ref[0, 0, u * nb + i], dz_b[i * blk:(i + 1) * blk])
            dk_ref[0, rows, :] += _dot(dz_b, qs)
            dv_ref[0, rows, :] += _dot(wgt.astype(BF16), dos)

        def tile(us, pre_lk, pre_d, dqt, masked):
            carry = {"pre_lk": pre_lk, "pre_d": pre_d, "dqt": dqt}
            _round_robin([tile_phases(u, carry, masked) for u in us])
            return carry["pre_lk"], carry["pre_d"], carry["dqt"]

        def top_tile(u, pre_lk, pre_d, dqt):
            carry = {"rest": totv - pre_lk, "pre_d": pre_d, "dqt": dqt}

            def block(kb):
                lo = kb * 2 * blk
                rows = pl.ds(pl.multiple_of((u * nb + kb) * blk, blk), blk)
                qs_k, dos_k = qs[lo:], dos[lo:]
                z, lb, lk_all, _ = _sb_logits(k_ref[0, rows, :], qs_k, None)
                lk = _sb_diag(lk_all)
                yield
                hi, lo_part = _split2(lk)
                raw = _dot(tri2, jnp.concatenate([hi, lo_part], axis=0))
                yield
                wgt = _sb_diag(jnp.exp(z + (carry["rest"][:, lo:] - raw)))
                carry["rest"] = _sb_add_from(carry["rest"], -(raw[blk - 1:blk] + lk[blk - 1:blk]), lo)
                dlogit = _dot_nt(v_ref[0, rows, :], dos_k) * wgt
                yield
                draw = _dot(tri2[:, :blk], dlogit.astype(BF16))
                yield
                sig = jnp.exp(lb)
                dz_b = _sb_diag(dlogit * (1.0 - sig) - (carry["pre_d"][:, lo:] + draw) * sig).astype(BF16)
                carry["pre_d"] = _sb_add_from(carry["pre_d"], draw[blk - 1:blk] + dlogit[blk - 1:blk], lo)
                yield
                carry["dqt"] = _sb_add_from(carry["dqt"], _dot(kt_ref[0, 0, u * nb + kb], dz_b), lo)
                dk_ref[0, rows, :] += _dot(dz_b, qs_k)
                dv_ref[0, rows, :] += _dot(wgt.astype(BF16), dos_k)

            _round_robin([block(kb) for kb in range(nb)])
            return carry["dqt"]

        zero = jnp.zeros((1, width), F32)
        top = qi0 // nb
        c = lax.fori_loop(0, top // 2, lambda t, c: tile([2 * t, 2 * t + 1], c[0], c[1], c[2], False),
                          (zero, zero, jnp.zeros((LANES, width), F32)))
        c = lax.fori_loop(0, top % 2, lambda t, c: tile([top - 1], c[0], c[1], c[2], False), c)
        dqt = top_tile(top, *c) if nb == nqb else tile([top], c[0], c[1], c[2], True)[2]
        dq_ref[0] = (_sb_unstack_t(dqt) * SB_SCALE).astype(BF16)

    qspec = pl.BlockSpec((1, nqb * blk, LANES), lambda b, p, i: (b, i, p))
    kspec = pl.BlockSpec((1, s, LANES), lambda b, p, i: (b, 0, p))
    vspec = pl.BlockSpec((1, s, LANES), lambda b, p, i: (b, 0, npair + p))
    ktspec = pl.BlockSpec((1, 1, nq, LANES, blk), lambda b, p, i: (b, p, 0, 0, 0))
    tspec = pl.BlockSpec((1, 1, 1, 1, width), lambda b, p, i: (b, p, i, 0, 0))
    in_specs = [qspec, kspec, vspec, ktspec, qspec, tspec] + ([kspec, kspec] if has_init else [])
    args = (q, kv, kv, kvt, do, tot) + ((dk_in, dv_in) if has_init else ())
    return pl.pallas_call(
        body, name=name, grid=(bsz, npair, nq // nqb), in_specs=in_specs,
        out_specs=[qspec, kspec, kspec],
        out_shape=[jax.ShapeDtypeStruct((bsz, s, w), BF16), jax.ShapeDtypeStruct((bsz, s, w), F32),
                   jax.ShapeDtypeStruct((bsz, s, w), F32)],
        compiler_params=_params("parallel", "parallel", "arbitrary"),
    )(*args)


ADAM_BLOCK_BYTES = 1 << 20


def _adamw(w, g, m, v, *, name):
    shape = w.shape
    r, c = shape[-2], shape[-1]
    lead = _size(shape[:-2])
    tr = r
    for cand in range(8, r, 8):
        if r % cand == 0 and cand * c * 4 <= ADAM_BLOCK_BYTES:
            tr = cand
    if r * c * 4 <= ADAM_BLOCK_BYTES:
        tr = r

    def body(w_ref, g_ref, m_ref, v_ref, d_ref, mo_ref, vo_ref):
        gv = g_ref[...]
        mn = ADAM_B1 * m_ref[...] + (1.0 - ADAM_B1) * gv
        vn = ADAM_B2 * v_ref[...] + (1.0 - ADAM_B2) * (gv * gv)
        m_hat = mn / (1.0 - ADAM_B1 ** ADAM_STEP)
        v_hat = vn / (1.0 - ADAM_B2 ** ADAM_STEP)
        d_ref[...] = -ADAM_LR * (m_hat / (jnp.sqrt(v_hat) + ADAM_EPS) + ADAM_WD * w_ref[...])
        mo_ref[...] = mn
        vo_ref[...] = vn

    blk = pl.BlockSpec((1, tr, c), lambda l, i: (l, i, 0))
    outs = pl.pallas_call(
        body, name=name, grid=(lead, r // tr), in_specs=[blk] * 4, out_specs=[blk] * 3,
        out_shape=[jax.ShapeDtypeStruct((lead, r, c), F32)] * 3,
        compiler_params=_params("parallel", "parallel"),
    )(*[a.reshape(lead, r, c) for a in (w, g, m, v)])
    return [o.reshape(shape) for o in outs]


def _row_tile(r, c, itemsize):
    if r * c * 4 <= ADAM_BLOCK_BYTES:
        return r
    step = 32 // itemsize
    tr = r
    for cand in range(step, r, step):
        if r % cand == 0 and cand * c * 4 <= ADAM_BLOCK_BYTES:
            tr = cand
    return tr


def _add_own_half(idx, g, recv, *, name):
    _, lh, r, c = recv.shape
    tr = _row_tile(r, c, g.dtype.itemsize)

    def body(idx_ref, a_ref, b_ref, o_ref):
        o_ref[...] = (a_ref[...].astype(F32) + b_ref[...].astype(F32)).astype(o_ref.dtype)

    blk = pl.BlockSpec((1, 1, tr, c), lambda k, l, i, idx: (k, l, i, 0))
    return pl.pallas_call(
        body, name=name,
        grid_spec=pltpu.PrefetchScalarGridSpec(
            num_scalar_prefetch=1, grid=(N_CHIPS, lh, r // tr),
            in_specs=[pl.BlockSpec((1, 1, tr, c), lambda k, l, i, idx: (k, idx[0] * lh + l, i, 0)), blk],
            out_specs=blk),
        out_shape=jax.ShapeDtypeStruct(recv.shape, g.dtype),
        compiler_params=_params("parallel", "parallel", "parallel"),
    )(idx, g, recv)


def _add_chips(idx, own, recv, *, name):
    _, lh, r, c = own.shape
    tr = _row_tile(r, c, own.dtype.itemsize)

    def body(idx_ref, a_ref, b_ref, o_ref):
        f = lambda v: v.astype(F32)
        o_ref[0] = ((f(a_ref[0, 0]) + f(b_ref[0, 0])) + f(b_ref[1, 0])) + f(b_ref[2, 0])

    return pl.pallas_call(
        body, name=name,
        grid_spec=pltpu.PrefetchScalarGridSpec(
            num_scalar_prefetch=1, grid=(lh, r // tr),
            in_specs=[pl.BlockSpec((1, 1, tr, c), lambda l, i, idx: (idx[0], l, i, 0)),
                      pl.BlockSpec((3, 1, tr, c), lambda l, i, idx: (0, l, i, 0))],
            out_specs=pl.BlockSpec((1, tr, c), lambda l, i, idx: (l, i, 0))),
        out_shape=jax.ShapeDtypeStruct((lh, r, c), F32),
        compiler_params=_params("parallel", "parallel"),
    )(idx, own, recv)


def _sum_devices(parts):
    _, r, _ = parts.shape

    def body(p_ref, o_ref):
        acc = p_ref[0]
        for k in range(1, N_DEV):
            acc = acc + p_ref[k]
        o_ref[...] = acc

    return pl.pallas_call(
        body, name="small_grad_sum", grid=(1,),
        in_specs=[pl.BlockSpec((N_DEV, r, LANES), lambda i: (0, 0, 0))],
        out_specs=pl.BlockSpec((r, LANES), lambda i: (0, 0)),
        out_shape=jax.ShapeDtypeStruct((r, LANES), F32),
    )(parts)


def _place():
    return lax.axis_index("x"), lax.axis_index("y"), lax.axis_index("c")


def _rcopy(src, dst, send_sems, recv_sems, k, to):
    return pltpu.make_async_remote_copy(src_ref=src, dst_ref=dst, send_sem=send_sems.at[k],
                                        recv_sem=recv_sems.at[k], device_id=to, device_id_type=MESH)


def _exchange_call(body, name, ins, out_shapes, n_sems):
    return pl.pallas_call(
        body, name=name, in_specs=[ANY] * len(ins), out_specs=[ANY] * len(out_shapes),
        out_shape=out_shapes,
        scratch_shapes=[pltpu.SemaphoreType.DMA((n_sems,)), pltpu.SemaphoreType.DMA((n_sems,))],
    )(*ins)


def _gather_weights(shards):
    n = len(shards)

    def body(*refs):
        ins, outs, send_sems, recv_sems = refs[:n], refs[n:2 * n], refs[2 * n], refs[2 * n + 1]
        x, y, c = _place()
        sibling = (x, y, 1 - c)
        chips = [(1 - x, y), (x, 1 - y), (1 - x, 1 - y)]

        def piece(i, px, py, pc):
            lh = ins[i].shape[0] // 2
            return outs[i].at[2 * px + py, pl.ds(pc * lh, lh)]

        def mine(i):
            lh = ins[i].shape[0] // 2
            return ins[i].at[pl.ds(c * lh, lh)]

        first = [_rcopy(mine(i), piece(i, x, y, c), send_sems, recv_sems, 6 * i + j, (*chip, c))
                 for i in range(n) for j, chip in enumerate(chips)]
        for cp in first:
            cp.start()
        passed = []
        for i in range(n):
            for j, chip in enumerate(chips):
                landed = piece(i, *chip, c)
                _rcopy(landed, landed, send_sems, recv_sems, 6 * i + j, (*chip, c)).wait_recv()
                passed.append(_rcopy(landed, landed, send_sems, recv_sems, 6 * i + 3 + j, sibling))
                passed[-1].start()
        for i in range(n):
            for j, chip in enumerate(chips):
                theirs = piece(i, *chip, 1 - c)
                _rcopy(theirs, theirs, send_sems, recv_sems, 6 * i + 3 + j, sibling).wait_recv()
        for cp in first + passed:
            cp.wait_send()

    shapes = [jax.ShapeDtypeStruct((N_CHIPS,) + s_.shape, s_.dtype) for s_ in shards]
    return _exchange_call(body, "gather_weights", shards, shapes, 6 * n)


def _swap_halves(gs):
    n = len(gs)

    def body(*refs):
        ins, outs, send_sems, recv_sems = refs[:n], refs[n:2 * n], refs[2 * n], refs[2 * n + 1]
        x, y, c = _place()
        cps = []
        for i in range(n):
            lh = ins[i].shape[1] // 2
            src = ins[i].at[pl.ds(0, N_CHIPS), pl.ds((1 - c) * lh, lh)]
            cps.append(_rcopy(src, outs[i], send_sems, recv_sems, i, (x, y, 1 - c)))
        for cp in cps:
            cp.start()
        for cp in cps:
            cp.wait()

    shapes = [jax.ShapeDtypeStruct((N_CHIPS, g.shape[1] // 2) + g.shape[2:], g.dtype) for g in gs]
    return _exchange_call(body, "grad_swap_halves", gs, shapes, n)


def _scatter_chunks(sums):
    n = len(sums)

    def body(*refs):
        ins, outs, send_sems, recv_sems = refs[:n], refs[n:2 * n], refs[2 * n], refs[2 * n + 1]
        x, y, c = _place()
        chips = [(1 - x, y), (x, 1 - y), (1 - x, 1 - y)]
        cps = [_rcopy(ins[i].at[2 * chip[0] + chip[1]], outs[i].at[j], send_sems, recv_sems, 3 * i + j,
                      (*chip, c)) for i in range(n) for j, chip in enumerate(chips)]
        for cp in cps:
            cp.start()
        for cp in cps:
            cp.wait()

    shapes = [jax.ShapeDtypeStruct((3,) + s_.shape[1:], s_.dtype) for s_ in sums]
    return _exchange_call(body, "grad_scatter_chunks", sums, shapes, 3 * n)


def _share_half(tots):
    n = len(tots)

    def body(*refs):
        ins, outs, send_sems, recv_sems = refs[:n], refs[n:2 * n], refs[2 * n], refs[2 * n + 1]
        x, y, c = _place()
        cps = [_rcopy(ins[i], outs[i], send_sems, recv_sems, i, (x, y, 1 - c)) for i in range(n)]
        for cp in cps:
            cp.start()
        for cp in cps:
            cp.wait()

    shapes = [jax.ShapeDtypeStruct(t_.shape, t_.dtype) for t_ in tots]
    return _exchange_call(body, "grad_share_half", tots, shapes, n)


def _exchange_small(r):
    rr, _ = r.shape

    def body(r_ref, out_ref, send_sems, recv_sems, local_sem):
        x, y, c = _place()
        me = 4 * x + 2 * y + c
        mine = pltpu.make_async_copy(r_ref, out_ref.at[me], local_sem)
        mine.start()
        cps = []
        for k in range(N_DEV - 1):
            fx, fy, fc = ((k + 1) >> 2) & 1, ((k + 1) >> 1) & 1, (k + 1) & 1
            to = (x ^ fx, y ^ fy, c ^ fc)
            cps.append((_rcopy(r_ref, out_ref.at[me], send_sems, recv_sems, k, to), to))
        for cp, _ in cps:
            cp.start()
        for k, (cp, to) in enumerate(cps):
            src = 4 * to[0] + 2 * to[1] + to[2]
            _rcopy(r_ref, out_ref.at[src], send_sems, recv_sems, k, to).wait_recv()
        for cp, _ in cps:
            cp.wait_send()
        mine.wait()

    return pl.pallas_call(
        body, name="small_grad_exchange", in_specs=[ANY], out_specs=ANY,
        out_shape=jax.ShapeDtypeStruct((N_DEV, rr, LANES), r.dtype),
        scratch_shapes=[pltpu.SemaphoreType.DMA((N_DEV - 1,)), pltpu.SemaphoreType.DMA((N_DEV - 1,)),
                        pltpu.SemaphoreType.DMA],
    )(r)


def _size(shape):
    n = 1
    for d in shape:
        n *= d
    return n


def _slab_rows(shape):
    rows = -(-_size(shape) // LANES)
    return -(-rows // SLAB_ROW_ALIGN) * SLAB_ROW_ALIGN


def _pack_rows(arrs, dtype, lead=0, unit=PACK_ROWS):
    parts, total = [], 0
    for a in arrs:
        front, shp = a.shape[:lead], a.shape[lead:]
        n, rows = _size(shp), _slab_rows(shp)
        nopad = [(0, 0)] * lead
        if n % LANES == 0:
            p = a.reshape(front + (n // LANES, LANES)).astype(dtype)
        else:
            p = jnp.pad(a.reshape(front + (n,)).astype(dtype), nopad + [(0, rows * LANES - n)])
            p = p.reshape(front + (rows, LANES))
        if p.shape[lead] != rows:
            p = jnp.pad(p, nopad + [(0, rows - p.shape[lead]), (0, 0)])
        parts.append(p)
        total += rows
    pad = (-total) % unit
    if pad:
        parts.append(jnp.zeros(parts[0].shape[:lead] + (pad, LANES), dtype))
    return jnp.concatenate(parts, axis=lead)


def _unpack_rows(slab, shapes):
    lead = slab.shape[:-2]
    out, off = [], 0
    for shp in shapes:
        n, rows = _size(shp), _slab_rows(shp)
        piece = slab[..., off:off + rows, :]
        if n % LANES == 0:
            piece = piece[..., :n // LANES, :].reshape(lead + tuple(shp))
        else:
            piece = piece.reshape(lead + (rows * LANES,))[..., :n].reshape(lead + tuple(shp))
        out.append(piece)
        off += rows
    return out


def _ffn_fwd(h, bsz, s, gain, w_up, cw, cb, w_down, i):
    hf = _rmsnorm_fwd(h, gain, name=f"ffn_norm_{i}")
    up = _mm(hf, w_up, name=f"ffn_up_{i}")
    up3 = up.reshape(bsz, s, -1)
    f = _conv_glu_fwd(up3, cw, cb, name=f"ffn_glu_{i}").reshape(h.shape[0], -1)
    h2 = _mm(f, w_down, add=h, name=f"ffn_down_{i}")
    return h2, (h, hf, up3, f)


def _ffn_bwd(dh, saved, gain, w_up, cw, cb, w_down, i):
    h, hf, up3, f = saved
    t = h.shape[0]
    dh, dhm = dh
    d_down = _mm(f, dhm, ta=True, name=f"ffn_down_dw_{i}")
    df = _mm(dhm, w_down, tb=True, name=f"ffn_down_dx_{i}")
    dug, duv, dcw, dcb = _conv_glu_bwd(up3, cw, cb, df.reshape(up3.shape[0], up3.shape[1], -1),
                                       name=f"ffn_glu_bwd_{i}")
    dug, duv = dug.reshape(t, -1), duv.reshape(t, -1)
    fdim = dug.shape[1]
    d_up = jnp.concatenate([_mm(hf, dug, ta=True, name=f"ffn_up_dwg_{i}"),
                            _mm(hf, duv, ta=True, name=f"ffn_up_dwv_{i}")], axis=1)
    dhf = _mm(dug, w_up[:, :fdim], tb=True, name=f"ffn_up_dxg_{i}")
    dh, dgain = _mm_norm_bwd(duv, w_up[:, fdim:], h, gain, dh, add=dhf, name=f"ffn_up_dxv_{i}")
    return dh, dgain, d_up, dcw, dcb, d_down


def _ple_layer_fwd(h, p_i, gain, w_gate, w_proj, i):
    hp = _rmsnorm_fwd(h, gain, name=f"ple_norm_{i}")
    a = _mm(hp, w_gate, name=f"ple_gate_{i}")
    pp = _mm(p_i, w_proj, name=f"ple_proj_{i}")
    return _ple_fwd(h, a, pp, name=f"ple_mix_{i}"), (h, hp, a, pp)


def _ple_layer_bwd(dh, saved, p_i, gain, w_gate, i):
    h, hp, a, pp = saved
    dh, _ = dh
    da, dpp = _ple_bwd(dh, a, pp, name=f"ple_mix_bwd_{i}")
    d_gate = _mm(hp, da, ta=True, name=f"ple_gate_dw_{i}")
    d_proj = _mm(p_i, dpp, ta=True, name=f"ple_proj_dw_{i}")
    dh, dgain = _mm_norm_bwd(da, w_gate, h, gain, dh, name=f"ple_gate_dx_{i}")
    return dh, dgain, d_gate, d_proj


def _ssm_consts(dt_bias, a_log, d_skip, g_n):
    hpg = SSM_HEADS_PER_GROUP
    a = -jnp.exp(a_log)
    rows = jnp.stack([dt_bias.reshape(g_n, hpg), a.reshape(g_n, hpg)], axis=1)
    plane = jnp.zeros((g_n, 8, LANES), F32).at[:, 0:2, 0:hpg].set(rows)
    psub = jnp.zeros((g_n, SSM_ROWS, LANES), F32).at[:, 0:hpg, 0:2].set(jnp.swapaxes(rows, 1, 2))
    d_ch = jnp.repeat(d_skip, SSM_HEAD_DIM).reshape(1, -1)
    return a, plane, psub, d_ch


def _ssm_in_big(w_in, d_inner, g_n):
    d = w_in.shape[0]
    cut = w_in.shape[1] - g_n * SSM_HEADS_PER_GROUP
    wdt = w_in[:, cut:].reshape(d, g_n, SSM_HEADS_PER_GROUP)
    wdt = jnp.pad(wdt, ((0, 0), (0, 0), (0, LANES - SSM_HEADS_PER_GROUP))).reshape(d, g_n * LANES)
    return jnp.concatenate([w_in[:, :cut], wdt], axis=1)


def _ssm_in_small(dw_big, g_n):
    d = dw_big.shape[0]
    cut = dw_big.shape[1] - g_n * LANES
    ddt = dw_big[:, cut:].reshape(d, g_n, LANES)[:, :, :SSM_HEADS_PER_GROUP].reshape(d, -1)
    return jnp.concatenate([dw_big[:, :cut], ddt], axis=1)


def _ssm_fwd(h, bsz, s, gain, w_in_big, cw, cb, plane, psub, d_ch, nw, w_out, i):
    d_inner = d_ch.shape[1]
    g_n = d_inner // SSM_GROUP_W
    conv_dim = cw.shape[1]
    hn = _rmsnorm_fwd(h, gain, name=f"attn_norm_{i}")
    zx = _mm(hn, w_in_big, name=f"ssm_in_{i}").reshape(bsz, s, -1)
    xbc = _conv_silu_fwd(zx, d_inner, cw, cb, name=f"ssm_conv_{i}")
    dtr = zx[:, :, d_inner + conv_dim:].reshape(bsz, s, g_n, LANES)[..., :SSM_HEADS_PER_GROUP]
    dtr_row = jnp.pad(jnp.transpose(dtr, (0, 2, 3, 1)),
                      ((0, 0), (0, 0), (0, SSM_ROWS - SSM_HEADS_PER_GROUP), (0, 0)))
    gn, states = _ssd_fwd(zx, xbc, dtr_row, plane, psub, d_ch, nw, name=f"ssd_{i}")
    gn2 = gn.reshape(h.shape[0], -1)
    h1 = _mm(gn2, w_out, add=h, name=f"ssm_out_{i}")
    return h1, (h, hn, zx, xbc, dtr_row, states, gn2)


def _ssm_bwd(dh, saved, gain, w_in_big, cw, cb, plane, psub, d_ch, nw, w_out, i):
    h, hn, zx, xbc, dtr_row, states, gn2 = saved
    t = h.shape[0]
    bsz, s, _ = zx.shape
    d_inner = d_ch.shape[1]
    dh, dhm = dh
    d_out = _mm(gn2, dhm, ta=True, name=f"ssm_out_dw_{i}")
    dgn = _mm(dhm, w_out, tb=True, name=f"ssm_out_dx_{i}").reshape(bsz, s, -1)
    dxs, dbm, dcm, dz, ddtr, ach, aln = _ssd_bwd(zx, xbc, dtr_row, plane, psub, d_ch, nw, states, dgn,
                                                  name=f"ssd_bwd_{i}")
    dxbc, dcw, dcb = _conv_silu_bwd(zx, d_inner, cw, cb, [dxs, dbm, dcm], name=f"ssm_conv_bwd_{i}")
    d_in_parts, dhn, col = [], None, 0
    for tag, part in (("z", dz), ("dt", ddtr), ("xbc", dxbc)):
        part = part.reshape(t, -1)
        col = {"z": 0, "xbc": dz.shape[-1], "dt": dz.shape[-1] + dxbc.shape[-1]}[tag]
        w_part = w_in_big[:, col:col + part.shape[1]]
        d_in_parts.append(_mm(hn, part, ta=True, name=f"ssm_in_dw_{tag}_{i}"))
        if tag == "xbc":
            dh, dgain = _mm_norm_bwd(part, w_part, h, gain, dh, add=dhn, name=f"ssm_in_dx_{tag}_{i}")
        else:
            dhn = _mm(part, w_part, tb=True, add=dhn, name=f"ssm_in_dx_{tag}_{i}")
    d_in_big = jnp.concatenate([d_in_parts[0], d_in_parts[2], d_in_parts[1]], axis=1)
    hpg = SSM_HEADS_PER_GROUP
    ach = jnp.sum(ach, axis=0)
    aln = jnp.sum(aln, axis=0)
    d_nw = ach[:, 0, :].reshape(-1)
    d_dskip = jnp.sum(ach[:, 1, :].reshape(-1, SSM_HEAD_DIM), axis=1)
    d_bias = aln[:, 0, :hpg].reshape(-1)
    d_a = aln[:, 1, :hpg].reshape(-1)
    return dh, dgain, d_in_big, dcw, dcb, d_bias, d_a, d_dskip, d_nw, d_out


def _sb_layer_fwd(h, bsz, s, gain, w_q, w_o, kv3, kvt, i):
    hn = _rmsnorm_fwd(h, gain, name=f"attn_norm_{i}")
    q3 = _mm(hn, w_q, out_dtype=BF16, name=f"sb_q_{i}").reshape(bsz, s, -1)
    o3, tot = _sb_fwd(q3, kv3, kvt, name=f"sb_attn_{i}")
    o2 = o3.reshape(h.shape[0], -1)
    h1 = _mm(o2, w_o, add=h, name=f"sb_o_{i}")
    return h1, (h, hn, q3, o2, tot)


def _sb_layer_bwd(dh, saved, gain, w_q, w_o, kv3, kvt, dk, dv, i):
    h, hn, q3, o2, tot = saved
    t = h.shape[0]
    dh, dhm = dh
    d_o = _mm(o2, dhm, ta=True, name=f"sb_o_dw_{i}")
    do3 = _mm(dhm, w_o, tb=True, out_dtype=BF16, name=f"sb_o_dx_{i}").reshape(q3.shape)
    dq3, dk, dv = _sb_bwd(q3, kv3, kvt, do3, tot, dk, dv, name=f"sb_attn_bwd_{i}")
    dq = dq3.reshape(t, -1)
    d_q = _mm(hn, dq, ta=True, name=f"sb_q_dw_{i}")
    dh, dgain = _mm_norm_bwd(dq, w_q, h, gain, dh, name=f"sb_q_dx_{i}")
    return dh, dgain, d_q, d_o, dk, dv


def kernel(x, p, attn_norm, ffn_norm, ple_norm, ssm_in_proj, ssm_conv_w, ssm_conv_b, ssm_dt_bias, ssm_a_log, ssm_d, ssm_norm, ssm_out_proj, kv_norm, w_kv, w_q, w_o, ffn_up, ffn_conv_w, ffn_conv_b, ffn_down, ple_gate, ple_proj, final_norm, loss_target, m_attn_norm, m_ffn_norm, m_ple_norm, m_ssm_in_proj, m_ssm_conv_w, m_ssm_conv_b, m_ssm_dt_bias, m_ssm_a_log, m_ssm_d, m_ssm_norm, m_ssm_out_proj, m_kv_norm, m_w_kv, m_w_q, m_w_o, m_ffn_up, m_ffn_conv_w, m_ffn_conv_b, m_ffn_down, m_ple_gate, m_ple_proj, m_final_norm, v_attn_norm, v_ffn_norm, v_ple_norm, v_ssm_in_proj, v_ssm_conv_w, v_ssm_conv_b, v_ssm_dt_bias, v_ssm_a_log, v_ssm_d, v_ssm_norm, v_ssm_out_proj, v_kv_norm, v_w_kv, v_w_q, v_w_o, v_ffn_up, v_ffn_conv_w, v_ffn_conv_b, v_ffn_down, v_ple_gate, v_ple_proj, v_final_norm):
    given = dict(locals())
    wl = {n: given[n] for n in WEIGHTS}
    bsz, s, d = x.shape
    t = bsz * s
    depth = attn_norm.shape[0]
    n_a = ssm_in_proj.shape[0]
    d_inner = ssm_norm.shape[1] * N_CHIPS
    g_n = d_inner // SSM_GROUP_W
    cidx = lax.axis_index("c").astype(I32).reshape(1)
    chip_idx = (2 * lax.axis_index("x") + lax.axis_index("y")).astype(I32).reshape(1)

    big = [n for n in SHARDED if _size(wl[n].shape) >= BIG_WEIGHT]
    small = [n for n in SHARDED if n not in big]
    small_shapes = [wl[n].shape for n in small]
    halves = lambda shp: shp if len(shp) == 3 else (2, shp[0] // 2, shp[1])
    small_slab = _pack_rows([wl[n] for n in small], BF16, unit=2 * SLAB_ROW_ALIGN)
    small_rows = small_slab.shape[0]
    mine = [wl[n].astype(BF16).reshape(halves(wl[n].shape)) for n in big]
    mine.append(small_slab.reshape(2, small_rows // 2, LANES))
    gathered = [lax.dynamic_update_index_in_dim(g, m_, chip_idx[0], 0)
                for g, m_ in zip(_gather_weights(mine), mine)]
    per_chip = {n: g.reshape((N_CHIPS,) + wl[n].shape) for n, g in zip(big, gathered)}
    per_chip.update(zip(small, _unpack_rows(gathered[-1].reshape(N_CHIPS, small_rows, LANES), small_shapes)))
    full = {}
    for n in SHARDED:
        ax, piece = SHARD_AXIS[n], per_chip[n]
        merged = piece.shape[1:ax + 1] + (N_CHIPS * piece.shape[ax + 1],) + piece.shape[ax + 2:]
        full[n] = jnp.moveaxis(piece, 0, ax).reshape(merged)

    h = x.reshape(t, d)
    tgt = loss_target.reshape(t, d)
    saved = []
    kv3 = kvt = hkv = h_kv_in = None
    consts = []
    for i in range(depth):
        if i < n_a:
            a_neg, plane, psub, d_ch = _ssm_consts(ssm_dt_bias[i], ssm_a_log[i], ssm_d[i], g_n)
            w_in_big = _ssm_in_big(full["ssm_in_proj"][i], d_inner, g_n)
            cw = full["ssm_conv_w"][i].astype(F32)
            cb = full["ssm_conv_b"][i].astype(F32)
            nw = full["ssm_norm"][i].astype(F32).reshape(1, -1)
            consts.append((a_neg, plane, psub, d_ch, w_in_big, cw, cb, nw))
            h, sv_mix = _ssm_fwd(h, bsz, s, attn_norm[i], w_in_big, cw, cb, plane, psub, d_ch, nw,
                                 full["ssm_out_proj"][i], i)
        else:
            j = i - n_a
            h, sv_mix = _sb_layer_fwd(h, bsz, s, attn_norm[i], full["w_q"][j], full["w_o"][j], kv3, kvt, i)
        fcw = full["ffn_conv_w"][i].astype(F32)
        h, sv_ffn = _ffn_fwd(h, bsz, s, ffn_norm[i], full["ffn_up"][i], fcw, ffn_conv_b[i],
                             full["ffn_down"][i], i)
        p_i = p[i].reshape(t, -1)
        h, sv_ple = _ple_layer_fwd(h, p_i, ple_norm[i], full["ple_gate"][i], full["ple_proj"][i], i)
        saved.append((sv_mix, sv_ffn, sv_ple))
        if i == n_a - 1:
            h_kv_in = h
            hkv = _rmsnorm_fwd(h, kv_norm, name="kv_norm")
            kv3 = _mm(hkv, full["w_kv"], out_dtype=BF16, name="kv_proj").reshape(bsz, s, -1)
            kvt = _kv_blocks_t(kv3)

    loss_local, dh, g_final = _final_loss(h, final_norm, tgt)
    dh = (dh, dh)
    gr = {n: [None] * wl[n].shape[0] for n in WEIGHTS if n not in ("kv_norm", "w_kv", "final_norm")}
    gr["final_norm"] = g_final
    dk = dv = None
    for i in reversed(range(depth)):
        sv_mix, sv_ffn, sv_ple = saved[i]
        if i == n_a - 1:
            dkv = jnp.concatenate([dk, dv], axis=-1).reshape(t, -1)
            gr["w_kv"] = _mm(hkv, dkv, ta=True, name="kv_proj_dw")
            dh, gr["kv_norm"] = _mm_norm_bwd(dkv, full["w_kv"], h_kv_in, kv_norm, dh[0], name="kv_proj_dx")
        p_i = p[i].reshape(t, -1)
        dh, gr["ple_norm"][i], gr["ple_gate"][i], gr["ple_proj"][i] = _ple_layer_bwd(
            dh, sv_ple, p_i, ple_norm[i], full["ple_gate"][i], i)
        fcw = full["ffn_conv_w"][i].astype(F32)
        (dh, gr["ffn_norm"][i], gr["ffn_up"][i], gr["ffn_conv_w"][i], gr["ffn_conv_b"][i],
         gr["ffn_down"][i]) = _ffn_bwd(dh, sv_ffn, ffn_norm[i], full["ffn_up"][i], fcw, ffn_conv_b[i],
                                       full["ffn_down"][i], i)
        if i < n_a:
            a_neg, plane, psub, d_ch, w_in_big, cw, cb, nw = consts[i]
            (dh, gr["attn_norm"][i], d_in_big, gr["ssm_conv_w"][i], gr["ssm_conv_b"][i],
             gr["ssm_dt_bias"][i], d_a, gr["ssm_d"][i], gr["ssm_norm"][i],
             gr["ssm_out_proj"][i]) = _ssm_bwd(dh, sv_mix, attn_norm[i], w_in_big, cw, cb, plane, psub,
                                               d_ch, nw, full["ssm_out_proj"][i], i)
            gr["ssm_in_proj"][i] = _ssm_in_small(d_in_big, g_n)
            gr["ssm_a_log"][i] = d_a * a_neg
        else:
            j = i - n_a
            dh, gr["attn_norm"][i], gr["w_q"][j], gr["w_o"][j], dk, dv = _sb_layer_bwd(
                dh, sv_mix, attn_norm[i], full["w_q"][j], full["w_o"][j], kv3, kvt, dk, dv, i)
    grad_x = dh[0].reshape(bsz, s, d)
    gfull = {n: (jnp.stack(v) if isinstance(v, list) else v) for n, v in gr.items()}

    by_chip = {}
    for n in SHARDED:
        ax, shp = SHARD_AXIS[n], gfull[n].shape
        split = gfull[n].reshape(shp[:ax] + (N_CHIPS, shp[ax] // N_CHIPS) + shp[ax + 1:])
        by_chip[n] = jnp.moveaxis(split, ax, 0)
    g4 = [by_chip[n].astype(BF16).reshape((N_CHIPS,) + halves(wl[n].shape)) for n in big]
    g4.append(_pack_rows([by_chip[n] for n in small], BF16, lead=1, unit=2 * SLAB_ROW_ALIGN)
              .reshape(N_CHIPS, 2, small_rows // 2, LANES))
    tags = big + ["small"]
    from_sibling = _swap_halves(g4)
    chip_sums = [_add_own_half(cidx, g, r_, name="grad_pair_sum_" + tg)
                 for g, r_, tg in zip(g4, from_sibling, tags)]
    from_chips = _scatter_chunks(chip_sums)
    my_half = [_add_chips(chip_idx, s_, r_, name="grad_chip_sum_" + tg)
               for s_, r_, tg in zip(chip_sums, from_chips, tags)]
    other_half = _share_half(my_half)
    low_core = cidx[0] == 0
    reduced = [jnp.concatenate([jnp.where(low_core, a, b_), jnp.where(low_core, b_, a)], axis=0)
               for a, b_ in zip(my_half, other_half)]
    grads = {n: g.reshape(wl[n].shape) for n, g in zip(big, reduced)}
    grads.update(zip(small, _unpack_rows(reduced[-1].reshape(small_rows, LANES), small_shapes)))

    rep_shapes = [wl[n].shape for n in REPLICATED]
    packed_r = _pack_rows([gfull[n] for n in REPLICATED], F32, unit=SLAB_ROW_ALIGN)
    g_rep = _sum_devices(_exchange_small(packed_r))

    delta, new_m, new_v = {}, {}, {}
    for n in SHARDED:
        delta[n], new_m[n], new_v[n] = _adamw(wl[n], grads[n], given["m_" + n], given["v_" + n],
                                              name="adamw_" + n)
    slabs = [_pack_rows([src[pre + n] for n in REPLICATED], F32, unit=SLAB_ROW_ALIGN)
             for src, pre in ((wl, ""), (given, "m_"), (given, "v_"))]
    rep_out = _adamw(slabs[0], g_rep, slabs[1], slabs[2], name="adamw_replicated")
    for dst, slab in zip((grads, delta, new_m, new_v), [g_rep] + list(rep_out)):
        dst.update(zip(REPLICATED, _unpack_rows(slab, rep_shapes)))
    loss = lax.psum(loss_local, ("x", "y", "c"))
    return (loss, grad_x, *[grads[n] for n in WEIGHTS], *[delta[n] for n in WEIGHTS],
            *[new_m[n] for n in WEIGHTS], *[new_v[n] for n in WEIGHTS])
```
